```python
import jax, jax.numpy as jnp
from jax import lax
import numpy as np

D_MODEL = 1024
BATCH = 8
SEQ = 4096
DEPTH = 1

CONV_WIDTH = 512
CONV_GROUPS = 8
CONV_K = 3
GLA_HEADS = 4
GLA_DK = 64
GLA_DV = 128
GLA_KEY_WIDTH = GLA_HEADS * GLA_DK
GLA_VAL_WIDTH = GLA_HEADS * GLA_DV
GATE_RANK = 16
GATE_NORMALIZER = 16.0
CHUNK = 64
MIX_WIDTH = CONV_WIDTH + GLA_VAL_WIDTH
IN_SIZES = [CONV_WIDTH, CONV_WIDTH, CONV_WIDTH,
            GLA_KEY_WIDTH, GLA_KEY_WIDTH, GLA_VAL_WIDTH,
            GLA_VAL_WIDTH, GATE_RANK]
IN_COLS = sum(IN_SIZES)
IN_SPLITS = np.cumsum(IN_SIZES)[:-1].tolist()
D_FF = 2816
N_ADA = 9
EPS = 1e-6

kernel_name = "hybrid_conv_gla_macaron_adaln"


def rms_norm(x, gain):
    xf = x.astype(jnp.float32)
    y = xf * lax.rsqrt(jnp.mean(xf * xf, axis=-1, keepdims=True) + EPS)
    return (y * gain.astype(jnp.float32)).astype(x.dtype)


def modulate(h, shift, scale):
    return h * (1.0 + scale[:, None, :]) + shift[:, None, :]


def swiglu_ffn(h, w_in, w_out):
    gate, up = jnp.split(h @ w_in, 2, axis=-1)
    return (jax.nn.silu(gate) * up) @ w_out


def causal_short_conv(u, w):
    T = u.shape[1]
    up = jnp.pad(u, ((0, 0), (CONV_K - 1, 0), (0, 0)))
    y = up[:, 0:T, :] * w[0]
    for k in range(1, CONV_K):
        y = y + up[:, k:k + T, :] * w[k]
    return y


def gla_chunked(q, k, v, log_g):
    B, H, T, DK = q.shape
    DV = v.shape[-1]
    N = T // CHUNK

    def to_chunks(a):
        return a.reshape(B, H, N, CHUNK, a.shape[-1]).transpose(2, 0, 1, 3, 4)

    qc, kc, vc, gc = to_chunks(q), to_chunks(k), to_chunks(v), to_chunks(log_g)
    bc = jnp.cumsum(gc, axis=3)
    causal = jnp.tril(jnp.ones((CHUNK, CHUNK), dtype=bool))[:, :, None]

    def step(S, inp):
        q_, k_, v_, b_ = inp
        o_inter = jnp.einsum('bhik,bhkv->bhiv', q_ * jnp.exp(b_), S)
        rel = b_[:, :, :, None, :] - b_[:, :, None, :, :]
        decay = jnp.where(causal, jnp.exp(jnp.where(causal, rel, 0.0)), 0.0)
        scores = jnp.einsum('bhik,bhijk,bhjk->bhij', q_, decay, k_)
        o_intra = jnp.einsum('bhij,bhjv->bhiv', scores, v_)
        b_last = b_[:, :, -1:, :]
        S_new = (jnp.exp(b_last[:, :, 0, :])[..., None] * S
                 + jnp.einsum('bhjk,bhjv->bhkv', k_ * jnp.exp(b_last - b_), v_))
        return S_new, o_inter + o_intra

    S0 = jnp.zeros((B, H, DK, DV), jnp.float32)
    _, o = lax.scan(step, S0, (qc, kc, vc, bc))
    return o.transpose(1, 2, 0, 3, 4).reshape(B, H, T, DV)


def token_mixer(h, w_in, conv_w, w_gk2, b_gk, gla_norm, w_out):
    Bsz, T, _ = h.shape
    proj = h @ w_in
    cb, cc, cv, q, k, v, g_out, gk_low = jnp.split(proj, IN_SPLITS, axis=-1)
    y_conv = cb * causal_short_conv(cc * cv, conv_w)
    log_g = jax.nn.log_sigmoid((gk_low @ w_gk2 + b_gk).astype(jnp.float32)) / GATE_NORMALIZER

    def heads(a, d):
        return a.reshape(Bsz, T, GLA_HEADS, d).transpose(0, 2, 1, 3).astype(jnp.float32)

    o = gla_chunked(heads(q, GLA_DK) * (GLA_DK ** -0.5), heads(k, GLA_DK),
                    heads(v, GLA_DV), heads(log_g, GLA_DK))
    o = rms_norm(o, gla_norm)
    o = o.transpose(0, 2, 1, 3).reshape(Bsz, T, GLA_VAL_WIDTH).astype(h.dtype)
    y_gla = o * jax.nn.silu(g_out)
    return jnp.concatenate([y_conv, y_gla], axis=-1) @ w_out


def _fwd_setup_inputs(seed: int = 0) -> dict:
    key = jax.random.key(seed)
    ks = jax.random.split(key, 20)
    f32 = jnp.float32
    nrm = lambda k, shape, s: jax.random.normal(k, shape, f32) * s
    L, D = DEPTH, D_MODEL
    return {
        "x": nrm(ks[0], (BATCH, SEQ, D), 1.0),
        "c": nrm(ks[1], (BATCH, D), 1.0),
        "w_ada": nrm(ks[2], (L, D, N_ADA * D), 0.5 * D ** -0.5),
        "b_ada": nrm(ks[3], (L, N_ADA * D), 0.01),
        "norm_ffn1": 1.0 + nrm(ks[4], (L, D), 0.02),
        "w_ffn1_in": nrm(ks[5], (L, D, 2 * D_FF), D ** -0.5),
        "w_ffn1_out": nrm(ks[6], (L, D_FF, D), D_FF ** -0.5),
        "norm_mix": 1.0 + nrm(ks[7], (L, D), 0.02),
        "w_mix_in": nrm(ks[8], (L, D, IN_COLS), D ** -0.5),
        "conv_w": nrm(ks[9], (L, CONV_K, CONV_WIDTH), CONV_K ** -0.5),
        "w_gk2": nrm(ks[10], (L, GATE_RANK, GLA_KEY_WIDTH), GATE_RANK ** -0.5),
        "b_gk": nrm(ks[11], (L, GLA_KEY_WIDTH), 0.01),
        "gla_norm": 1.0 + nrm(ks[12], (L, GLA_DV), 0.02),
        "w_mix_out": nrm(ks[13], (L, MIX_WIDTH, D), MIX_WIDTH ** -0.5),
        "norm_ffn2": 1.0 + nrm(ks[14], (L, D), 0.02),
        "w_ffn2_in": nrm(ks[15], (L, D, 2 * D_FF), D ** -0.5),
        "w_ffn2_out": nrm(ks[16], (L, D_FF, D), D_FF ** -0.5),
        "norm_final": 1.0 + nrm(ks[17], (D,), 0.02),
    }


def _fwd_reference(x, c, w_ada, b_ada, norm_ffn1, w_ffn1_in, w_ffn1_out, norm_mix,
              w_mix_in, conv_w, w_gk2, b_gk, gla_norm, w_mix_out, norm_ffn2,
              w_ffn2_in, w_ffn2_out, norm_final):
    c_act = jax.nn.silu(c)
    for l in range(DEPTH):
        ada = c_act @ w_ada[l] + b_ada[l]
        sh1, sc1, g1, sh2, sc2, g2, sh3, sc3, g3 = jnp.split(ada, N_ADA, axis=-1)
        h = modulate(rms_norm(x, norm_ffn1[l]), sh1, sc1)
        x = x + 0.5 * g1[:, None, :] * swiglu_ffn(h, w_ffn1_in[l], w_ffn1_out[l])
        h = modulate(rms_norm(x, norm_mix[l]), sh2, sc2)
        x = x + g2[:, None, :] * token_mixer(h, w_mix_in[l], conv_w[l], w_gk2[l],
                                             b_gk[l], gla_norm[l], w_mix_out[l])
        h = modulate(rms_norm(x, norm_ffn2[l]), sh3, sc3)
        x = x + 0.5 * g3[:, None, :] * swiglu_ffn(h, w_ffn2_in[l], w_ffn2_out[l])
    return rms_norm(x, norm_final)


import jax as _jax
import jax.numpy as _jnp

TWIN_FORMAT = 'train_step'
FWD_PARAMS = ['x', 'c', 'w_ada', 'b_ada', 'norm_ffn1', 'w_ffn1_in', 'w_ffn1_out', 'norm_mix', 'w_mix_in', 'conv_w', 'w_gk2', 'b_gk', 'gla_norm', 'w_mix_out', 'norm_ffn2', 'w_ffn2_in', 'w_ffn2_out', 'norm_final']
TWIN_WEIGHTS = ['w_ada', 'b_ada', 'norm_ffn1', 'w_ffn1_in', 'w_ffn1_out', 'norm_mix', 'w_mix_in', 'conv_w', 'w_gk2', 'b_gk', 'gla_norm', 'w_mix_out', 'norm_ffn2', 'w_ffn2_in', 'w_ffn2_out', 'norm_final']
TWIN_DIFF_INPUT = 'x'
TWIN_INPUTS = ['x', 'c', 'w_ada', 'b_ada', 'norm_ffn1', 'w_ffn1_in', 'w_ffn1_out', 'norm_mix', 'w_mix_in', 'conv_w', 'w_gk2', 'b_gk', 'gla_norm', 'w_mix_out', 'norm_ffn2', 'w_ffn2_in', 'w_ffn2_out', 'norm_final', 'loss_target', 'm_w_ada', 'm_b_ada', 'm_norm_ffn1', 'm_w_ffn1_in', 'm_w_ffn1_out', 'm_norm_mix', 'm_w_mix_in', 'm_conv_w', 'm_w_gk2', 'm_b_gk', 'm_gla_norm', 'm_w_mix_out', 'm_norm_ffn2', 'm_w_ffn2_in', 'm_w_ffn2_out', 'm_norm_final', 'v_w_ada', 'v_b_ada', 'v_norm_ffn1', 'v_w_ffn1_in', 'v_w_ffn1_out', 'v_norm_mix', 'v_w_mix_in', 'v_conv_w', 'v_w_gk2', 'v_b_gk', 'v_gla_norm', 'v_w_mix_out', 'v_norm_ffn2', 'v_w_ffn2_in', 'v_w_ffn2_out', 'v_norm_final']
TWIN_OUTPUTS = ['loss', 'grad_x', 'grad_w_ada', 'grad_b_ada', 'grad_norm_ffn1', 'grad_w_ffn1_in', 'grad_w_ffn1_out', 'grad_norm_mix', 'grad_w_mix_in', 'grad_conv_w', 'grad_w_gk2', 'grad_b_gk', 'grad_gla_norm', 'grad_w_mix_out', 'grad_norm_ffn2', 'grad_w_ffn2_in', 'grad_w_ffn2_out', 'grad_norm_final', 'delta_w_ada', 'delta_b_ada', 'delta_norm_ffn1', 'delta_w_ffn1_in', 'delta_w_ffn1_out', 'delta_norm_mix', 'delta_w_mix_in', 'delta_conv_w', 'delta_w_gk2', 'delta_b_gk', 'delta_gla_norm', 'delta_w_mix_out', 'delta_norm_ffn2', 'delta_w_ffn2_in', 'delta_w_ffn2_out', 'delta_norm_final', 'new_m_w_ada', 'new_m_b_ada', 'new_m_norm_ffn1', 'new_m_w_ffn1_in', 'new_m_w_ffn1_out', 'new_m_norm_mix', 'new_m_w_mix_in', 'new_m_conv_w', 'new_m_w_gk2', 'new_m_b_gk', 'new_m_gla_norm', 'new_m_w_mix_out', 'new_m_norm_ffn2', 'new_m_w_ffn2_in', 'new_m_w_ffn2_out', 'new_m_norm_final', 'new_v_w_ada', 'new_v_b_ada', 'new_v_norm_ffn1', 'new_v_w_ffn1_in', 'new_v_w_ffn1_out', 'new_v_norm_mix', 'new_v_w_mix_in', 'new_v_conv_w', 'new_v_w_gk2', 'new_v_b_gk', 'new_v_gla_norm', 'new_v_w_mix_out', 'new_v_norm_ffn2', 'new_v_w_ffn2_in', 'new_v_w_ffn2_out', 'new_v_norm_final']
TWIN_LEAF_KINDS = {'loss': 'loss', 'grad_x': 'grad_x', 'grad_w_ada': 'grad_w', 'grad_b_ada': 'grad_w', 'grad_norm_ffn1': 'grad_w', 'grad_w_ffn1_in': 'grad_w', 'grad_w_ffn1_out': 'grad_w', 'grad_norm_mix': 'grad_w', 'grad_w_mix_in': 'grad_w', 'grad_conv_w': 'grad_w', 'grad_w_gk2': 'grad_w', 'grad_b_gk': 'grad_w', 'grad_gla_norm': 'grad_w', 'grad_w_mix_out': 'grad_w', 'grad_norm_ffn2': 'grad_w', 'grad_w_ffn2_in': 'grad_w', 'grad_w_ffn2_out': 'grad_w', 'grad_norm_final': 'grad_w', 'delta_w_ada': 'delta_w', 'delta_b_ada': 'delta_w', 'delta_norm_ffn1': 'delta_w', 'delta_w_ffn1_in': 'delta_w', 'delta_w_ffn1_out': 'delta_w', 'delta_norm_mix': 'delta_w', 'delta_w_mix_in': 'delta_w', 'delta_conv_w': 'delta_w', 'delta_w_gk2': 'delta_w', 'delta_b_gk': 'delta_w', 'delta_gla_norm': 'delta_w', 'delta_w_mix_out': 'delta_w', 'delta_norm_ffn2': 'delta_w', 'delta_w_ffn2_in': 'delta_w', 'delta_w_ffn2_out': 'delta_w', 'delta_norm_final': 'delta_w', 'new_m_w_ada': 'new_m', 'new_m_b_ada': 'new_m', 'new_m_norm_ffn1': 'new_m', 'new_m_w_ffn1_in': 'new_m', 'new_m_w_ffn1_out': 'new_m', 'new_m_norm_mix': 'new_m', 'new_m_w_mix_in': 'new_m', 'new_m_conv_w': 'new_m', 'new_m_w_gk2': 'new_m', 'new_m_b_gk': 'new_m', 'new_m_gla_norm': 'new_m', 'new_m_w_mix_out': 'new_m', 'new_m_norm_ffn2': 'new_m', 'new_m_w_ffn2_in': 'new_m', 'new_m_w_ffn2_out': 'new_m', 'new_m_norm_final': 'new_m', 'new_v_w_ada': 'new_v', 'new_v_b_ada': 'new_v', 'new_v_norm_ffn1': 'new_v', 'new_v_w_ffn1_in': 'new_v', 'new_v_w_ffn1_out': 'new_v', 'new_v_norm_mix': 'new_v', 'new_v_w_mix_in': 'new_v', 'new_v_conv_w': 'new_v', 'new_v_w_gk2': 'new_v', 'new_v_b_gk': 'new_v', 'new_v_gla_norm': 'new_v', 'new_v_w_mix_out': 'new_v', 'new_v_norm_ffn2': 'new_v', 'new_v_w_ffn2_in': 'new_v', 'new_v_w_ffn2_out': 'new_v', 'new_v_norm_final': 'new_v'}


def _forward(args):
    return _fwd_reference(*[args[k] for k in FWD_PARAMS])


def _output_shape():
    def fwd():
        inp = _fwd_setup_inputs(0)
        return _fwd_reference(*[inp[k] for k in FWD_PARAMS])
    out = _jax.eval_shape(fwd)
    return out.shape, out.dtype

N_MICROBATCH = 1
ADAM_LR = 0.001
ADAM_B1 = 0.9
ADAM_B2 = 0.999
ADAM_EPS = 1e-08
ADAM_WD = 0.01
ADAM_STEP = 10
PER_EXAMPLE_BATCH_AXIS = {'x': 0, 'c': 0, 'loss_target': 0}
SHARED_INPUTS = []
_WEIGHT_DTYPES = {'w_ada': _jnp.float32, 'b_ada': _jnp.float32, 'norm_ffn1': _jnp.float32, 'w_ffn1_in': _jnp.float32, 'w_ffn1_out': _jnp.float32, 'norm_mix': _jnp.float32, 'w_mix_in': _jnp.float32, 'conv_w': _jnp.float32, 'w_gk2': _jnp.float32, 'b_gk': _jnp.float32, 'gla_norm': _jnp.float32, 'w_mix_out': _jnp.float32, 'norm_ffn2': _jnp.float32, 'w_ffn2_in': _jnp.float32, 'w_ffn2_out': _jnp.float32, 'norm_final': _jnp.float32}
MOMENT_SCALE = {'w_ada': 5.221976e-02, 'b_ada': 8.644035e-02, 'norm_ffn1': 2.880014e-02, 'w_ffn1_in': 1.241282e-02, 'w_ffn1_out': 2.029655e-02, 'norm_mix': 9.142225e-02, 'w_mix_in': 5.581562e-02, 'conv_w': 6.681156e-02, 'w_gk2': 9.267901e-03, 'b_gk': 2.489492e-02, 'gla_norm': 6.918636e-02, 'w_mix_out': 5.337940e-02, 'norm_ffn2': 2.578574e-02, 'w_ffn2_in': 1.108502e-02, 'w_ffn2_out': 1.806592e-02, 'norm_final': 3.207118e+01}


def _to_microbatches(a, axis):
    t = _jnp.moveaxis(a, axis, 0)
    t = t.reshape((N_MICROBATCH, t.shape[0] // N_MICROBATCH) + t.shape[1:])
    return _jnp.moveaxis(t, 1, axis + 1)


def setup_inputs(seed: int = 0) -> dict:
    inp = _fwd_setup_inputs(seed)
    key = _jax.random.fold_in(_jax.random.key(seed), 7919)
    shape, _ = _output_shape()
    out = dict(inp)
    out["loss_target"] = _jax.random.normal(_jax.random.fold_in(key, 0), shape, _jnp.float32)
    for i, name in enumerate(TWIN_WEIGHTS):
        w = inp[name].astype(_jnp.float32)
        if MOMENT_SCALE is None:
            s = _jnp.sqrt(_jnp.mean(_jnp.square(w)) + 1e-30)
        else:
            s = MOMENT_SCALE[name]
        km, kv = _jax.random.split(_jax.random.fold_in(key, i + 1))
        out[name] = w
        out["m_" + name] = s * _jax.random.normal(km, w.shape, _jnp.float32)
        out["v_" + name] = (s * s) * _jax.random.uniform(kv, w.shape, _jnp.float32, 0.5, 1.5)
    if N_MICROBATCH > 1:
        for name, axis in PER_EXAMPLE_BATCH_AXIS.items():
            out[name] = _to_microbatches(out[name], axis)
    return {'x': out['x'], 'c': out['c'], 'w_ada': out['w_ada'], 'b_ada': out['b_ada'], 'norm_ffn1': out['norm_ffn1'], 'w_ffn1_in': out['w_ffn1_in'], 'w_ffn1_out': out['w_ffn1_out'], 'norm_mix': out['norm_mix'], 'w_mix_in': out['w_mix_in'], 'conv_w': out['conv_w'], 'w_gk2': out['w_gk2'], 'b_gk': out['b_gk'], 'gla_norm': out['gla_norm'], 'w_mix_out': out['w_mix_out'], 'norm_ffn2': out['norm_ffn2'], 'w_ffn2_in': out['w_ffn2_in'], 'w_ffn2_out': out['w_ffn2_out'], 'norm_final': out['norm_final'], 'loss_target': out['loss_target'], 'm_w_ada': out['m_w_ada'], 'm_b_ada': out['m_b_ada'], 'm_norm_ffn1': out['m_norm_ffn1'], 'm_w_ffn1_in': out['m_w_ffn1_in'], 'm_w_ffn1_out': out['m_w_ffn1_out'], 'm_norm_mix': out['m_norm_mix'], 'm_w_mix_in': out['m_w_mix_in'], 'm_conv_w': out['m_conv_w'], 'm_w_gk2': out['m_w_gk2'], 'm_b_gk': out['m_b_gk'], 'm_gla_norm': out['m_gla_norm'], 'm_w_mix_out': out['m_w_mix_out'], 'm_norm_ffn2': out['m_norm_ffn2'], 'm_w_ffn2_in': out['m_w_ffn2_in'], 'm_w_ffn2_out': out['m_w_ffn2_out'], 'm_norm_final': out['m_norm_final'], 'v_w_ada': out['v_w_ada'], 'v_b_ada': out['v_b_ada'], 'v_norm_ffn1': out['v_norm_ffn1'], 'v_w_ffn1_in': out['v_w_ffn1_in'], 'v_w_ffn1_out': out['v_w_ffn1_out'], 'v_norm_mix': out['v_norm_mix'], 'v_w_mix_in': out['v_w_mix_in'], 'v_conv_w': out['v_conv_w'], 'v_w_gk2': out['v_w_gk2'], 'v_b_gk': out['v_b_gk'], 'v_gla_norm': out['v_gla_norm'], 'v_w_mix_out': out['v_w_mix_out'], 'v_norm_ffn2': out['v_norm_ffn2'], 'v_w_ffn2_in': out['v_w_ffn2_in'], 'v_w_ffn2_out': out['v_w_ffn2_out'], 'v_norm_final': out['v_norm_final']}


def _loss(weights, diff, rest, loss_target):
    with _jax.named_scope("forward"):
        args = {**rest, TWIN_DIFF_INPUT: diff, **{k: w.astype(_WEIGHT_DTYPES[k]) for k, w in weights.items()}}
        y = _forward(args)
    with _jax.named_scope("loss_head"):
        err = _jnp.square(y.astype(_jnp.float32) - loss_target)
        return 0.5 * _jnp.sum(_jnp.mean(err, axis=-1)) if err.ndim else 0.5 * err


def _adamw(w, g, m, v):
    m = ADAM_B1 * m + (1.0 - ADAM_B1) * g
    v = ADAM_B2 * v + (1.0 - ADAM_B2) * _jnp.square(g)
    m_hat = m / (1.0 - ADAM_B1 ** ADAM_STEP)
    v_hat = v / (1.0 - ADAM_B2 ** ADAM_STEP)
    delta = -ADAM_LR * (m_hat / (_jnp.sqrt(v_hat) + ADAM_EPS) + ADAM_WD * w)
    return delta, m, v


def reference(x, c, w_ada, b_ada, norm_ffn1, w_ffn1_in, w_ffn1_out, norm_mix, w_mix_in, conv_w, w_gk2, b_gk, gla_norm, w_mix_out, norm_ffn2, w_ffn2_in, w_ffn2_out, norm_final, loss_target, m_w_ada, m_b_ada, m_norm_ffn1, m_w_ffn1_in, m_w_ffn1_out, m_norm_mix, m_w_mix_in, m_conv_w, m_w_gk2, m_b_gk, m_gla_norm, m_w_mix_out, m_norm_ffn2, m_w_ffn2_in, m_w_ffn2_out, m_norm_final, v_w_ada, v_b_ada, v_norm_ffn1, v_w_ffn1_in, v_w_ffn1_out, v_norm_mix, v_w_mix_in, v_conv_w, v_w_gk2, v_b_gk, v_gla_norm, v_w_mix_out, v_norm_ffn2, v_w_ffn2_in, v_w_ffn2_out, v_norm_final):
    given = dict(x=x, c=c, w_ada=w_ada, b_ada=b_ada, norm_ffn1=norm_ffn1, w_ffn1_in=w_ffn1_in, w_ffn1_out=w_ffn1_out, norm_mix=norm_mix, w_mix_in=w_mix_in, conv_w=conv_w, w_gk2=w_gk2, b_gk=b_gk, gla_norm=gla_norm, w_mix_out=w_mix_out, norm_ffn2=norm_ffn2, w_ffn2_in=w_ffn2_in, w_ffn2_out=w_ffn2_out, norm_final=norm_final, loss_target=loss_target, m_w_ada=m_w_ada, m_b_ada=m_b_ada, m_norm_ffn1=m_norm_ffn1, m_w_ffn1_in=m_w_ffn1_in, m_w_ffn1_out=m_w_ffn1_out, m_norm_mix=m_norm_mix, m_w_mix_in=m_w_mix_in, m_conv_w=m_conv_w, m_w_gk2=m_w_gk2, m_b_gk=m_b_gk, m_gla_norm=m_gla_norm, m_w_mix_out=m_w_mix_out, m_norm_ffn2=m_norm_ffn2, m_w_ffn2_in=m_w_ffn2_in, m_w_ffn2_out=m_w_ffn2_out, m_norm_final=m_norm_final, v_w_ada=v_w_ada, v_b_ada=v_b_ada, v_norm_ffn1=v_norm_ffn1, v_w_ffn1_in=v_w_ffn1_in, v_w_ffn1_out=v_w_ffn1_out, v_norm_mix=v_norm_mix, v_w_mix_in=v_w_mix_in, v_conv_w=v_conv_w, v_w_gk2=v_w_gk2, v_b_gk=v_b_gk, v_gla_norm=v_gla_norm, v_w_mix_out=v_w_mix_out, v_norm_ffn2=v_norm_ffn2, v_w_ffn2_in=v_w_ffn2_in, v_w_ffn2_out=v_w_ffn2_out, v_norm_final=v_norm_final)
    weights = {n: given[n] for n in TWIN_WEIGHTS}
    shared = {n: given[n] for n in SHARED_INPUTS}
    per_example = {n: given[n] for n in ['x', 'c']}
    grad_fn = _jax.value_and_grad(_loss, argnums=(0, 1))

    def one_microbatch(ex, loss_target):
        ex = dict(ex)
        diff = ex.pop(TWIN_DIFF_INPUT)
        return grad_fn(weights, diff, {**shared, **ex}, loss_target)

    if N_MICROBATCH == 1:
        loss, (grad_w, grad_x) = one_microbatch(per_example, given["loss_target"])
    else:
        def body(carry, xs):
            loss_sum, grad_sum = carry
            l_k, (gw_k, gx_k) = one_microbatch(xs[0], xs[1])
            with _jax.named_scope("update"):
                return (loss_sum + l_k, _jax.tree.map(_jnp.add, grad_sum, gw_k)), gx_k

        init = (_jnp.zeros((), _jnp.float32), _jax.tree.map(_jnp.zeros_like, weights))
        (loss, grad_w), grad_x = _jax.lax.scan(body, init, (per_example, given["loss_target"]))
    with _jax.named_scope("update"):
        delta_w, new_m, new_v = {}, {}, {}
        for n in TWIN_WEIGHTS:
            delta_w[n], new_m[n], new_v[n] = _adamw(weights[n], grad_w[n], given["m_" + n], given["v_" + n])
    return (loss, grad_x, *[grad_w[n] for n in TWIN_WEIGHTS], *[delta_w[n] for n in TWIN_WEIGHTS],
            *[new_m[n] for n in TWIN_WEIGHTS], *[new_v[n] for n in TWIN_WEIGHTS])
```

```python
import functools

import jax
import jax.numpy as jnp
from jax import lax
from jax.experimental import pallas as pl
from jax.experimental.pallas import tpu as pltpu

F32 = jnp.float32
BF16 = jnp.bfloat16
S = jax.ShapeDtypeStruct

NDEV = 8
EPS = 1e-6
GATE_NORMALIZER = 16.0
CHUNK = 64
N_ADA = 9
ADAM_LR, ADAM_B1, ADAM_B2, ADAM_EPS, ADAM_WD, ADAM_STEP = 0.001, 0.9, 0.999, 1e-08, 0.01, 10
V7X_VMEM_LIMIT = 56 * 1024 * 1024
ROW_TILE = 512
K_TILE = 1024
PACK_ROWS = 24
ANY = pl.BlockSpec(memory_space=pl.ANY)


def _cp(*sem):
    return pltpu.CompilerParams(dimension_semantics=sem or None, vmem_limit_bytes=V7X_VMEM_LIMIT)


def _dot(a, b):
    return jnp.dot(a, b, preferred_element_type=F32)


def _dot_nt(a, b):
    return lax.dot_general(a, b, (((1,), (1,)), ((), ())), preferred_element_type=F32)


def _dot_tn(a, b):
    return lax.dot_general(a, b, (((0,), (0,)), ((), ())), preferred_element_type=F32)


def _rsum8(a):
    r, c = a.shape
    return jnp.sum(a.reshape(r // 8, 8, c), axis=0)


def _sigmoid(x):
    return 1.0 / (1.0 + jnp.exp(-x))


def _normmod(x, nrm, sc, sh):
    rstd = lax.rsqrt(jnp.mean(x * x, axis=-1, keepdims=True) + EPS)
    xhat = x * rstd
    return xhat, rstd, (xhat * nrm) * (1.0 + sc) + sh


def _normmod_bwd(dh, x, nrm, sc):
    rstd = lax.rsqrt(jnp.mean(x * x, axis=-1, keepdims=True) + EPS)
    xhat = x * rstd
    dxhat = dh * (nrm * (1.0 + sc))
    dx = rstd * (dxhat - xhat * jnp.mean(dxhat * xhat, axis=-1, keepdims=True))
    return dx, dh, dh * (xhat * nrm), dh * ((1.0 + sc) * xhat)


def _place():
    x, y, c = lax.axis_index("x"), lax.axis_index("y"), lax.axis_index("c")
    return x, y, c, 4 * x + 2 * y + c


def _peer(x, y, c, k):
    px = 1 - x if k & 4 else x
    py = 1 - y if k & 2 else y
    pc = 1 - c if k & 1 else c
    return (px, py, pc), 4 * px + 2 * py + pc


def _exchange(xs, gather, name):
    n = len(xs)

    def body(*refs):
        ins, outs = refs[:n], refs[n:2 * n]
        send_sems, recv_sems, local_sems = refs[2 * n:]
        x, y, c, me = _place()
        local = []
        for i in range(n):
            src = ins[i] if gather else ins[i].at[me]
            cp = pltpu.make_async_copy(src, outs[i].at[me], local_sems.at[i])
            cp.start()
            local.append(cp)
        sends = []
        for k in range(1, NDEV):
            peer, pid = _peer(x, y, c, k)
            for i in range(n):
                src = ins[i] if gather else ins[i].at[pid]
                cp = pltpu.make_async_remote_copy(
                    src_ref=src, dst_ref=outs[i].at[me], send_sem=send_sems.at[i, k - 1], recv_sem=recv_sems.at[i, k - 1],
                    device_id=peer, device_id_type=pl.DeviceIdType.MESH)
                cp.start()
                sends.append(cp)
        for k in range(1, NDEV):
            peer, pid = _peer(x, y, c, k)
            for i in range(n):
                src = ins[i] if gather else ins[i].at[pid]
                pltpu.make_async_remote_copy(
                    src_ref=src, dst_ref=outs[i].at[pid], send_sem=send_sems.at[i, k - 1], recv_sem=recv_sems.at[i, k - 1],
                    device_id=peer, device_id_type=pl.DeviceIdType.MESH).wait_recv()
        for cp in sends:
            cp.wait_send()
        for cp in local:
            cp.wait()

    out_shape = [S((NDEV,) + a.shape, a.dtype) if gather else S(a.shape, a.dtype) for a in xs]
    return pl.pallas_call(
        body, name=name, out_shape=out_shape, in_specs=[ANY] * n, out_specs=[ANY] * n,
        scratch_shapes=[pltpu.SemaphoreType.DMA((n, NDEV - 1)), pltpu.SemaphoreType.DMA((n, NDEV - 1)),
                        pltpu.SemaphoreType.DMA((n,))],
    )(*xs)


def _ada_partial(c_all, w_ada):
    nb, d = c_all.shape
    cols = w_ada.shape[1]

    def body(c_ref, w_ref, act_ref, p_ref):
        cc = c_ref[...]
        act = cc * _sigmoid(cc)
        act_ref[...] = act
        p_ref[...] = _dot(act.astype(BF16), w_ref[...].astype(BF16))

    return pl.pallas_call(body, name="ada_partial", out_shape=[S((nb, d), F32), S((nb, cols), F32)],
                          compiler_params=_cp())(c_all, w_ada)


def _ada_finish(p_all, b_ada):
    _, nb, cols = p_all.shape

    def body(p_ref, b_ref, o_ref):
        me = _place()[3]
        for s in range(NDEV):
            o_ref[:, s * cols:(s + 1) * cols] = p_ref[s, pl.ds(me, 1), :] + b_ref[:, s * cols:(s + 1) * cols]

    return pl.pallas_call(body, name="ada_finish", out_shape=S((1, NDEV * cols), F32), compiler_params=_cp())(p_all, b_ada)


def _ffn_in(x, ada9, nrm, w_in, sh_row, sc_row, name):
    t, d = x.shape
    nb, bw = w_in.shape[0] // 2, w_in.shape[2]
    tm = min(ROW_TILE, t)

    def body(x_ref, ada_ref, n_ref, wg_ref, wu_ref, h_ref, gu_ref, s_ref):
        @pl.when(pl.program_id(1) == 0)
        def _():
            _, _, h = _normmod(x_ref[...], n_ref[...], ada_ref[sc_row:sc_row + 1, :], ada_ref[sh_row:sh_row + 1, :])
            h_ref[...] = h.astype(BF16)

        h = h_ref[...]
        g = _dot(h, wg_ref[...])
        u = _dot(h, wu_ref[...])
        gu_ref[0] = g.astype(BF16)
        gu_ref[1] = u.astype(BF16)
        s_ref[...] = (g * _sigmoid(g) * u).astype(BF16)

    return pl.pallas_call(
        body, name=name, grid=(t // tm, nb),
        in_specs=[pl.BlockSpec((tm, d), lambda i, j: (i, 0)), pl.BlockSpec((N_ADA, d), lambda i, j: (0, 0)),
                  pl.BlockSpec((1, d), lambda i, j: (0, 0)),
                  pl.BlockSpec((None, d, bw), lambda i, j: (j, 0, 0)), pl.BlockSpec((None, d, bw), lambda i, j: (j + nb, 0, 0))],
        out_shape=[S((t, d), BF16), S((2, nb, t, bw), BF16), S((nb, t, bw), BF16)],
        out_specs=[pl.BlockSpec((tm, d), lambda i, j: (i, 0)), pl.BlockSpec((2, None, tm, bw), lambda i, j: (0, j, i, 0)),
                   pl.BlockSpec((None, tm, bw), lambda i, j: (j, i, 0))],
        compiler_params=_cp("parallel", "arbitrary"),
    )(x, ada9, nrm, w_in, w_in)


def _ffn_out(s, w_out, x, ada9, g_row, res_scale, name):
    nb, t, bw = s.shape
    d = x.shape[1]
    tm = min(ROW_TILE, t)

    def body(s_ref, w_ref, x_ref, ada_ref, xo_ref, f_ref):
        acc = _dot(s_ref[0], w_ref[0])
        for b in range(1, nb):
            acc = acc + _dot(s_ref[b], w_ref[b])
        f_ref[...] = acc.astype(BF16)
        xo_ref[...] = x_ref[...] + (res_scale * ada_ref[g_row:g_row + 1, :]) * acc

    return pl.pallas_call(
        body, name=name, grid=(t // tm,),
        in_specs=[pl.BlockSpec((nb, tm, bw), lambda i: (0, i, 0)), pl.BlockSpec((nb, bw, d), lambda i: (0, 0, 0)),
                  pl.BlockSpec((tm, d), lambda i: (i, 0)), pl.BlockSpec((N_ADA, d), lambda i: (0, 0))],
        out_shape=[S((t, d), F32), S((t, d), BF16)],
        out_specs=[pl.BlockSpec((tm, d), lambda i: (i, 0)), pl.BlockSpec((tm, d), lambda i: (i, 0))],
        compiler_params=_cp("parallel"),
    )(s, w_out, x, ada9)


def _ffn_bwd_ds(dxo, f, ada9, w_out, gu, g_row, res_scale, name):
    t, d = dxo.shape
    nb, bw = w_out.shape[0], w_out.shape[1]
    tm = min(ROW_TILE, t)

    def body(dxo_ref, f_ref, ada_ref, w_ref, gu_ref, df_ref, da_ref, dg_ref):
        i, j = pl.program_id(0), pl.program_id(1)

        @pl.when((i == 0) & (j == 0))
        def _():
            dg_ref[...] = jnp.zeros_like(dg_ref)

        @pl.when(j == 0)
        def _():
            dxo_t = dxo_ref[...]
            df_ref[...] = ((res_scale * ada_ref[g_row:g_row + 1, :]) * dxo_t).astype(BF16)
            dg_ref[...] += res_scale * _rsum8(dxo_t * f_ref[...].astype(F32))

        ds = _dot_nt(df_ref[...], w_ref[...])
        g = gu_ref[0].astype(F32)
        u = gu_ref[1].astype(F32)
        sg = _sigmoid(g)
        da_ref[0] = (ds * u * (sg * (1.0 + g * (1.0 - sg)))).astype(BF16)
        da_ref[1] = (ds * (g * sg)).astype(BF16)

    return pl.pallas_call(
        body, name=name, grid=(t // tm, nb),
        in_specs=[pl.BlockSpec((tm, d), lambda i, j: (i, 0)), pl.BlockSpec((tm, d), lambda i, j: (i, 0)),
                  pl.BlockSpec((N_ADA, d), lambda i, j: (0, 0)), pl.BlockSpec((None, bw, d), lambda i, j: (j, 0, 0)),
                  pl.BlockSpec((2, None, tm, bw), lambda i, j: (0, j, i, 0))],
        out_shape=[S((t, d), BF16), S((2, nb, t, bw), BF16), S((8, d), F32)],
        out_specs=[pl.BlockSpec((tm, d), lambda i, j: (i, 0)), pl.BlockSpec((2, None, tm, bw), lambda i, j: (0, j, i, 0)),
                   pl.BlockSpec((8, d), lambda i, j: (0, 0))],
        compiler_params=_cp("arbitrary", "arbitrary"),
    )(dxo, f, ada9, w_out, gu)


def _ffn_bwd_dh(da, w_in, x, dxo, ada9, nrm, sh_row, sc_row, name):
    t, d = x.shape
    nb, bw = w_in.shape[0] // 2, w_in.shape[2]
    tm = min(ROW_TILE, t)

    def body(da_ref, wg_ref, wu_ref, x_ref, dxo_ref, ada_ref, n_ref, dx_ref, dsh_ref, dsc_ref, dn_ref, acc_ref):
        i, j = pl.program_id(0), pl.program_id(1)

        @pl.when((i == 0) & (j == 0))
        def _():
            dsh_ref[...] = jnp.zeros_like(dsh_ref)
            dsc_ref[...] = jnp.zeros_like(dsc_ref)
            dn_ref[...] = jnp.zeros_like(dn_ref)

        part = _dot_nt(da_ref[0], wg_ref[...]) + _dot_nt(da_ref[1], wu_ref[...])

        @pl.when(j == 0)
        def _():
            acc_ref[...] = part

        @pl.when(j > 0)
        def _():
            acc_ref[...] += part

        @pl.when(j == nb - 1)
        def _():
            dx, tsh, tsc, tn = _normmod_bwd(acc_ref[...], x_ref[...], n_ref[...], ada_ref[sc_row:sc_row + 1, :])
            dx_ref[...] = dxo_ref[...] + dx
            dsh_ref[...] += _rsum8(tsh)
            dsc_ref[...] += _rsum8(tsc)
            dn_ref[...] += _rsum8(tn)

    vec = pl.BlockSpec((8, d), lambda i, j: (0, 0))
    return pl.pallas_call(
        body, name=name, grid=(t // tm, nb),
        in_specs=[pl.BlockSpec((2, None, tm, bw), lambda i, j: (0, j, i, 0)),
                  pl.BlockSpec((None, d, bw), lambda i, j: (j, 0, 0)), pl.BlockSpec((None, d, bw), lambda i, j: (j + nb, 0, 0)),
                  pl.BlockSpec((tm, d), lambda i, j: (i, 0)), pl.BlockSpec((tm, d), lambda i, j: (i, 0)),
                  pl.BlockSpec((N_ADA, d), lambda i, j: (0, 0)), pl.BlockSpec((1, d), lambda i, j: (0, 0))],
        out_shape=[S((t, d), F32), S((8, d), F32), S((8, d), F32), S((8, d), F32)],
        out_specs=[pl.BlockSpec((tm, d), lambda i, j: (i, 0)), vec, vec, vec],
        scratch_shapes=[pltpu.VMEM((tm, d), F32)],
        compiler_params=_cp("arbitrary", "arbitrary"),
    )(da, w_in, w_in, x, dxo, ada9, nrm)


def _tn_matmul(a, b, a_block, a_map, b_block, b_map, out_shape, out_block, out_map, nblk, name):
    t = a.shape[-2]
    tk = min(K_TILE, t)

    def body(a_ref, b_ref, o_ref):
        part = _dot_tn(a_ref[...], b_ref[...])

        @pl.when(pl.program_id(1) == 0)
        def _():
            o_ref[...] = part

        @pl.when(pl.program_id(1) > 0)
        def _():
            o_ref[...] += part

    return pl.pallas_call(
        body, name=name, grid=(nblk, t // tk),
        in_specs=[pl.BlockSpec(a_block(tk), a_map), pl.BlockSpec(b_block(tk), b_map)],
        out_shape=S(out_shape, F32), out_specs=pl.BlockSpec(out_block, out_map),
        compiler_params=_cp("parallel", "arbitrary"),
    )(a, b)


def _ffn_backward(dxo, x_in, h, gu, s, f, ada9, nrm, w_in, w_out, rows, name):
    sh_row, sc_row, g_row = rows
    nb, t, bw = s.shape
    d = x_in.shape[1]
    df, da, dg = _ffn_bwd_ds(dxo, f, ada9, w_out, gu, g_row, 0.5, name + "_ds")
    dw_out = _tn_matmul(s, df, lambda tk: (None, tk, bw), lambda sb, k: (sb, k, 0), lambda tk: (tk, d), lambda sb, k: (k, 0),
                        (nb, bw, d), (None, bw, d), lambda sb, k: (sb, 0, 0), nb, name + "_dwout")
    da8 = da.reshape(2 * nb, t, bw)
    dw_in = _tn_matmul(h, da8, lambda tk: (tk, d), lambda sb, k: (k, 0), lambda tk: (None, tk, bw), lambda sb, k: (sb, k, 0),
                       (2 * nb, d, bw), (None, d, bw), lambda sb, k: (sb, 0, 0), 2 * nb, name + "_dwin")
    dx, dsh, dsc, dn = _ffn_bwd_dh(da, w_in, x_in, dxo, ada9, nrm, sh_row, sc_row, name + "_dh")
    return dx, dw_in, dw_out, (dsh, dsc, dg, dn)


def _mix_in(x, ada9, nrm, w, widths, dts, name):
    t, d = x.shape
    n = w.shape[1]
    tm = min(ROW_TILE, t)
    starts = [sum(widths[:i]) for i in range(len(widths))]

    def body(x_ref, ada_ref, n_ref, w_ref, h_ref, *out_refs):
        _, _, h = _normmod(x_ref[...], n_ref[...], ada_ref[4:5, :], ada_ref[3:4, :])
        hb = h.astype(BF16)
        h_ref[...] = hb
        for o_ref, st, wd in zip(out_refs, starts, widths):
            o_ref[...] = _dot(hb, w_ref[:, st:st + wd]).astype(o_ref.dtype)

    return pl.pallas_call(
        body, name=name, grid=(t // tm,),
        in_specs=[pl.BlockSpec((tm, d), lambda i: (i, 0)), pl.BlockSpec((N_ADA, d), lambda i: (0, 0)),
                  pl.BlockSpec((1, d), lambda i: (0, 0)), pl.BlockSpec((d, n), lambda i: (0, 0))],
        out_shape=[S((t, d), BF16)] + [S((t, wd), dt) for wd, dt in zip(widths, dts)],
        out_specs=[pl.BlockSpec((tm, d), lambda i: (i, 0))] + [pl.BlockSpec((tm, wd), lambda i: (i, 0)) for wd in widths],
        compiler_params=_cp("parallel"),
    )(x, ada9, nrm, w)


def _tri(lower):
    r = lax.broadcasted_iota(jnp.int32, (CHUNK, CHUNK), 0)
    c = lax.broadcasted_iota(jnp.int32, (CHUNK, CHUNK), 1)
    return (r >= c) if lower else (c >= r)


def _dot_01(m, x):
    hi = x.astype(BF16)
    r1 = x - hi.astype(F32)
    mid = r1.astype(BF16)
    lo = (r1 - mid.astype(F32)).astype(BF16)
    return _dot(m, hi) + _dot(m, mid) + _dot(m, lo)


def _gla_chunk_terms(q, k, lg, low01):
    b = _dot_01(low01, lg)
    bl = b[CHUNK - 1:CHUNK, :]
    r = 0.5 * bl
    eb, ebl, em, en = jnp.exp(b), jnp.exp(bl - b), jnp.exp(b - r), jnp.exp(r - b)
    return eb, ebl, em, en, jnp.exp(bl), q * eb, k * ebl, q * em, k * en


def _scores(qm_h, knp, qk1_h):
    r = lax.broadcasted_iota(jnp.int32, (CHUNK, CHUNK), 0)
    c = lax.broadcasted_iota(jnp.int32, (CHUNK, CHUNK), 1)
    p = jnp.where(r > c, _dot_nt(qm_h, knp), 0.0)
    return jnp.where(r == c, jnp.sum(qk1_h, axis=1, keepdims=True), p)


def _gla_fwd(qk, v, gl, wg, bg, heads, name):
    t = qk.shape[0]
    kw, vw = qk.shape[1] // 2, v.shape[1]
    dk, dv = kw // heads, vw // heads
    assert dk == 64 and dv == 128 and kw % 128 == 0
    gt = min(ROW_TILE, t)
    nc = gt // CHUNK
    scale = dk ** -0.5

    def body(qk_ref, v_ref, gl_ref, wg_ref, bg_ref, o_ref, lg_ref, sall_ref, st_ref):
        @pl.when(pl.program_id(0) == 0)
        def _():
            st_ref[...] = jnp.zeros_like(st_ref)

        gk = _dot(gl_ref[...].astype(BF16), wg_ref[...]) + bg_ref[...]
        lg_ref[...] = (jnp.minimum(gk, 0.0) - jnp.log(1.0 + jnp.exp(-jnp.abs(gk)))) / GATE_NORMALIZER
        low01 = _tri(True).astype(BF16)
        lane = lax.broadcasted_iota(jnp.int32, (CHUNK, 128), 1)

        def chunk(ci, carry):
            rows = pl.ds(pl.multiple_of(ci * CHUNK, CHUNK), CHUNK)
            q = qk_ref[rows, 0:kw] * scale
            k = qk_ref[rows, kw:2 * kw]
            qk1 = q.astype(BF16).astype(F32) * k.astype(BF16).astype(F32)
            eb, ebl, em, en, ebl_row, qe, ke, qm, kn = _gla_chunk_terms(q, k, lg_ref[rows, :], low01)
            for h in range(heads):
                lanes = slice(128 * (h // 2), 128 * (h // 2) + 128)
                own = (lane < 64) if h % 2 == 0 else (lane >= 64)
                knp = kn[:, lanes].astype(BF16)
                qm_h = jnp.where(own, qm[:, lanes], 0.0).astype(BF16)
                qe_h = jnp.where(own, qe[:, lanes], 0.0).astype(BF16)
                ke_h = jnp.where(own, ke[:, lanes], 0.0).astype(BF16)
                v_h = v_ref[rows, h * dv:(h + 1) * dv]
                st = st_ref[h]
                sall_ref[ci, h] = st
                p = _scores(qm_h, knp, jnp.where(own, qk1[:, lanes], 0.0))
                o_ref[rows, h * dv:(h + 1) * dv] = _dot(p.astype(BF16), v_h) + _dot_nt(qe_h, st.astype(BF16))
                st_ref[h] = st * ebl_row[:, lanes] + _dot_tn(v_h, ke_h)
            return carry

        lax.fori_loop(0, nc, chunk, 0)

    return pl.pallas_call(
        body, name=name, grid=(t // gt,),
        in_specs=[pl.BlockSpec((gt, 2 * kw), lambda i: (i, 0)), pl.BlockSpec((gt, vw), lambda i: (i, 0)),
                  pl.BlockSpec((gt, 128), lambda i: (i, 0)), pl.BlockSpec((128, kw), lambda i: (0, 0)),
                  pl.BlockSpec((1, kw), lambda i: (0, 0))],
        out_shape=[S((t, vw), F32), S((t, kw), F32), S((t // CHUNK, heads, dv, 128), F32)],
        out_specs=[pl.BlockSpec((gt, vw), lambda i: (i, 0)), pl.BlockSpec((gt, kw), lambda i: (i, 0)),
                   pl.BlockSpec((nc, heads, dv, 128), lambda i: (i, 0, 0, 0))],
        scratch_shapes=[pltpu.VMEM((heads, dv, 128), F32)],
        compiler_params=_cp("arbitrary"),
    )(qk, v, gl, wg, bg)


def _gla_bwd(qk, v, lg, do, sall, gl, wg, heads, name):
    t = qk.shape[0]
    kw, vw = qk.shape[1] // 2, v.shape[1]
    dk, dv = kw // heads, vw // heads
    gt = min(ROW_TILE, t)
    nc = gt // CHUNK
    nt = t // gt
    scale = dk ** -0.5

    def body(qk_ref, v_ref, lg_ref, do_ref, sall_ref, gl_ref, wg_ref, dqk_ref, dv_ref, dgl_ref, dwg_ref, dbg_ref, dst_ref, dgk_ref):
        @pl.when(pl.program_id(0) == 0)
        def _():
            dst_ref[...] = jnp.zeros_like(dst_ref)
            dwg_ref[...] = jnp.zeros_like(dwg_ref)
            dbg_ref[...] = jnp.zeros_like(dbg_ref)

        low01 = _tri(True).astype(BF16)
        up01 = _tri(False).astype(BF16)
        causal = _tri(True)
        lane = lax.broadcasted_iota(jnp.int32, (CHUNK, 128), 1)
        last_row = lax.broadcasted_iota(jnp.int32, (CHUNK, kw), 0) == CHUNK - 1

        def chunk(cj, carry):
            ci = nc - 1 - cj
            rows = pl.ds(pl.multiple_of(ci * CHUNK, CHUNK), CHUNK)
            q = qk_ref[rows, 0:kw] * scale
            k = qk_ref[rows, kw:2 * kw]
            qk1 = q.astype(BF16).astype(F32) * k.astype(BF16).astype(F32)
            lgc = lg_ref[rows, :]
            eb, ebl, em, en, ebl_row, qe, ke, qm, kn = _gla_chunk_terms(q, k, lgc, low01)
            dqe, dqm, dkn, dke, drow = [], [], [], [], []
            for pr in range(kw // 128):
                lanes = slice(128 * pr, 128 * pr + 128)
                knp = kn[:, lanes].astype(BF16)
                parts = []
                for half in range(2):
                    h = 2 * pr + half
                    own = (lane < 64) if half == 0 else (lane >= 64)
                    qm_h = jnp.where(own, qm[:, lanes], 0.0).astype(BF16)
                    qe_h = jnp.where(own, qe[:, lanes], 0.0).astype(BF16)
                    ke_h = jnp.where(own, ke[:, lanes], 0.0).astype(BF16)
                    v_h = v_ref[rows, h * dv:(h + 1) * dv]
                    do_h = do_ref[rows, h * dv:(h + 1) * dv]
                    st = sall_ref[ci, h]
                    dst = dst_ref[h]
                    stb, dstb = st.astype(BF16), dst.astype(BF16)
                    p = _scores(qm_h, knp, jnp.where(own, qk1[:, lanes], 0.0)).astype(BF16)
                    dp = jnp.where(causal, _dot_nt(do_h, v_h), 0.0).astype(BF16)
                    dv_ref[rows, h * dv:(h + 1) * dv] = (_dot_tn(p, do_h) + _dot_nt(ke_h, dstb)).astype(BF16)
                    parts.append((jnp.where(own, _dot(dp, knp), 0.0), _dot_tn(dp, qm_h), _dot(do_h, stb), _dot(v_h, dstb),
                                  jnp.sum(st * dst, axis=0, keepdims=True)))
                    dst_ref[h] = dst * ebl_row[:, lanes] + _dot_tn(do_h, qe_h)
                dqm.append(parts[0][0] + parts[1][0])
                dkn.append(parts[0][1] + parts[1][1])
                dqe.append(parts[0][2] + parts[1][2])
                dke.append(parts[0][3] + parts[1][3])
                drow.append(parts[0][4] + parts[1][4])
            dqm, dkn, dqe, dke, drow = [jnp.concatenate(a, axis=1) for a in (dqm, dkn, dqe, dke, drow)]
            dqk_ref[rows, 0:kw] = ((dqe * eb + dqm * em) * scale).astype(BF16)
            dqk_ref[rows, kw:2 * kw] = (dke * ebl + dkn * en).astype(BF16)
            tke = dke * ke
            db = dqe * qe + dqm * qm - dkn * kn - tke
            dbl = jnp.sum(tke, axis=0, keepdims=True) + drow * ebl_row
            db = db + jnp.where(last_row, dbl, 0.0)
            dlg = _dot_01(up01, db)
            dgk_ref[rows, :] = dlg * ((1.0 - jnp.exp(GATE_NORMALIZER * lgc)) / GATE_NORMALIZER)
            return carry

        lax.fori_loop(0, nc, chunk, 0)
        dgk = dgk_ref[...]
        dgkb = dgk.astype(BF16)
        dgl_ref[...] = _dot_nt(dgkb, wg_ref[...]).astype(BF16)
        dwg_ref[...] += _dot_tn(gl_ref[...].astype(BF16), dgkb)
        dbg_ref[...] += _rsum8(dgk)

    rev = lambda i: (nt - 1 - i, 0)
    return pl.pallas_call(
        body, name=name, grid=(nt,),
        in_specs=[pl.BlockSpec((gt, 2 * kw), rev), pl.BlockSpec((gt, vw), rev), pl.BlockSpec((gt, kw), rev),
                  pl.BlockSpec((gt, vw), rev), pl.BlockSpec((nc, heads, dv, 128), lambda i: (nt - 1 - i, 0, 0, 0)),
                  pl.BlockSpec((gt, 128), rev), pl.BlockSpec((128, kw), lambda i: (0, 0))],
        out_shape=[S((t, 2 * kw), BF16), S((t, vw), BF16), S((t, 128), BF16), S((128, kw), F32), S((8, kw), F32)],
        out_specs=[pl.BlockSpec((gt, 2 * kw), rev), pl.BlockSpec((gt, vw), rev), pl.BlockSpec((gt, 128), rev),
                   pl.BlockSpec((128, kw), lambda i: (0, 0)), pl.BlockSpec((8, kw), lambda i: (0, 0))],
        scratch_shapes=[pltpu.VMEM((heads, dv, 128), F32), pltpu.VMEM((gt, kw), F32)],
        compiler_params=_cp("arbitrary"),
    )(qk, v, lg, do, sall, gl, wg)


def _conv_taps(cx_ref, halo_ref, first, cw):
    tm = cx_ref.shape[0]
    u = cx_ref[:, cw:2 * cw].astype(F32) * cx_ref[:, 2 * cw:3 * cw].astype(F32)
    uh = halo_ref[:, cw:2 * cw].astype(F32) * halo_ref[:, 2 * cw:3 * cw].astype(F32)
    uh = jnp.where(first, 0.0, uh)
    row = lax.broadcasted_iota(jnp.int32, (tm, cw), 0)
    u1 = jnp.where(row == 0, uh[15:16, :], pltpu.roll(u, 1, 0))
    u2 = jnp.where(row == 0, uh[14:15, :], jnp.where(row == 1, uh[15:16, :], pltpu.roll(u, 2, 0)))
    return u, u1, u2


def _head_norm(o_h, gn):
    rstd = lax.rsqrt(jnp.mean(o_h * o_h, axis=-1, keepdims=True) + EPS)
    ohat = o_h * rstd
    return ohat, rstd, ohat * gn


def _mix_out(cx, o, go, conv_w, gn, w_out, x, ada9, heads, name):
    t, d = x.shape
    cw, vw = conv_w.shape[1], o.shape[1]
    dv = vw // heads
    tm = min(ROW_TILE, t)

    def body(cx_ref, halo_ref, o_ref, go_ref, cwt_ref, gn_ref, w_ref, x_ref, ada_ref, xo_ref, m_ref, y_ref):
        u, u1, u2 = _conv_taps(cx_ref, halo_ref, pl.program_id(0) == 0, cw)
        yc = cwt_ref[0:1, :] * u2 + cwt_ref[1:2, :] * u1 + cwt_ref[2:3, :] * u
        y_ref[:, 0:cw] = (cx_ref[:, 0:cw].astype(F32) * yc).astype(BF16)
        for h in range(heads):
            cols = slice(h * dv, (h + 1) * dv)
            _, _, on = _head_norm(o_ref[:, cols], gn_ref[...])
            g = go_ref[:, cols].astype(F32)
            y_ref[:, cw + h * dv:cw + (h + 1) * dv] = (on * (g * _sigmoid(g))).astype(BF16)
        m = _dot(y_ref[...], w_ref[...])
        m_ref[...] = m.astype(BF16)
        xo_ref[...] = x_ref[...] + ada_ref[5:6, :] * m

    return pl.pallas_call(
        body, name=name, grid=(t // tm,),
        in_specs=[pl.BlockSpec((tm, 3 * cw), lambda i: (i, 0)),
                  pl.BlockSpec((16, 3 * cw), lambda i: (jnp.maximum(i * (tm // 16) - 1, 0), 0)),
                  pl.BlockSpec((tm, vw), lambda i: (i, 0)), pl.BlockSpec((tm, vw), lambda i: (i, 0)),
                  pl.BlockSpec((3, cw), lambda i: (0, 0)), pl.BlockSpec((1, dv), lambda i: (0, 0)),
                  pl.BlockSpec((cw + vw, d), lambda i: (0, 0)), pl.BlockSpec((tm, d), lambda i: (i, 0)),
                  pl.BlockSpec((N_ADA, d), lambda i: (0, 0))],
        out_shape=[S((t, d), F32), S((t, d), BF16), S((t, cw + vw), BF16)],
        out_specs=[pl.BlockSpec((tm, d), lambda i: (i, 0)), pl.BlockSpec((tm, d), lambda i: (i, 0)),
                   pl.BlockSpec((tm, cw + vw), lambda i: (i, 0))],
        compiler_params=_cp("parallel"),
    )(cx, cx, o, go, conv_w, gn, w_out, x, ada9)


def _mix_bwd_a(dxo, m, ada9, w_out, cx, o, go, conv_w, gn, heads, name):
    t, d = dxo.shape
    cw, vw = conv_w.shape[1], o.shape[1]
    dv = vw // heads
    tm = min(ROW_TILE, t)

    def body(dxo_ref, m_ref, ada_ref, w_ref, cx_ref, halo_ref, o_ref, go_ref, cwt_ref, gn_ref,
             dm_ref, dyc_ref, dcb_ref, do_ref, dgo_ref, dg_ref, dcw_ref, dgn_ref):
        @pl.when(pl.program_id(0) == 0)
        def _():
            dg_ref[...] = jnp.zeros_like(dg_ref)
            dcw_ref[...] = jnp.zeros_like(dcw_ref)
            dgn_ref[...] = jnp.zeros_like(dgn_ref)

        dxo_t = dxo_ref[...]
        dmb = (ada_ref[5:6, :] * dxo_t).astype(BF16)
        dm_ref[...] = dmb
        dg_ref[...] += _rsum8(dxo_t * m_ref[...].astype(F32))
        dy = _dot_nt(dmb, w_ref[...])
        u, u1, u2 = _conv_taps(cx_ref, halo_ref, pl.program_id(0) == 0, cw)
        yc = cwt_ref[0:1, :] * u2 + cwt_ref[1:2, :] * u1 + cwt_ref[2:3, :] * u
        dyv = dy[:, 0:cw]
        dcb_ref[...] = (dyv * yc).astype(BF16)
        dyc = dyv * cx_ref[:, 0:cw].astype(F32)
        dyc_ref[...] = dyc
        dcw_ref[0] += _rsum8(dyc * u2)
        dcw_ref[1] += _rsum8(dyc * u1)
        dcw_ref[2] += _rsum8(dyc * u)
        for h in range(heads):
            cols = slice(h * dv, (h + 1) * dv)
            ohat, rstd, on = _head_norm(o_ref[:, cols], gn_ref[...])
            g = go_ref[:, cols].astype(F32)
            sg = _sigmoid(g)
            dyg = dy[:, cw + h * dv:cw + (h + 1) * dv]
            dgo_ref[:, cols] = (dyg * on * (sg * (1.0 + g * (1.0 - sg)))).astype(BF16)
            don = dyg * (g * sg)
            dgn_ref[...] += _rsum8(don * ohat)
            tt = don * gn_ref[...]
            do_ref[:, cols] = (rstd * (tt - ohat * jnp.mean(tt * ohat, axis=-1, keepdims=True))).astype(BF16)

    return pl.pallas_call(
        body, name=name, grid=(t // tm,),
        in_specs=[pl.BlockSpec((tm, d), lambda i: (i, 0)), pl.BlockSpec((tm, d), lambda i: (i, 0)),
                  pl.BlockSpec((N_ADA, d), lambda i: (0, 0)), pl.BlockSpec((cw + vw, d), lambda i: (0, 0)),
                  pl.BlockSpec((tm, 3 * cw), lambda i: (i, 0)),
                  pl.BlockSpec((16, 3 * cw), lambda i: (jnp.maximum(i * (tm // 16) - 1, 0), 0)),
                  pl.BlockSpec((tm, vw), lambda i: (i, 0)), pl.BlockSpec((tm, vw), lambda i: (i, 0)),
                  pl.BlockSpec((3, cw), lambda i: (0, 0)), pl.BlockSpec((1, dv), lambda i: (0, 0))],
        out_shape=[S((t, d), BF16), S((t, cw), F32), S((t, cw), BF16), S((t, vw), BF16), S((t, vw), BF16),
                   S((8, d), F32), S((3, 8, cw), F32), S((8, dv), F32)],
        out_specs=[pl.BlockSpec((tm, d), lambda i: (i, 0)), pl.BlockSpec((tm, cw), lambda i: (i, 0)),
                   pl.BlockSpec((tm, cw), lambda i: (i, 0)), pl.BlockSpec((tm, vw), lambda i: (i, 0)),
                   pl.BlockSpec((tm, vw), lambda i: (i, 0)), pl.BlockSpec((8, d), lambda i: (0, 0)),
                   pl.BlockSpec((3, 8, cw), lambda i: (0, 0, 0)), pl.BlockSpec((8, dv), lambda i: (0, 0))],
        compiler_params=_cp("arbitrary"),
    )(dxo, m, ada9, w_out, cx, cx, o, go, conv_w, gn)


def _mix_bwd_b(dyc, cx, dcb, dqk, dvv, dgo, dgl, conv_w, w, x, dxo, ada9, nrm, name):
    t, d = x.shape
    cw = conv_w.shape[1]
    n = w.shape[1]
    tm = min(ROW_TILE, t)
    nt = t // tm
    pieces = [dcb.shape[1], cw, cw, dqk.shape[1], dvv.shape[1], dgo.shape[1], dgl.shape[1]]
    assert sum(pieces) == n

    def body(dyc_ref, nxt_ref, cx_ref, dcb_ref, dqk_ref, dv_ref, dgo_ref, dgl_ref, cwt_ref, w_ref, x_ref, dxo_ref, ada_ref, n_ref,
             dx_ref, dp_ref, dsh_ref, dsc_ref, dn_ref):
        i = pl.program_id(0)

        @pl.when(i == 0)
        def _():
            dsh_ref[...] = jnp.zeros_like(dsh_ref)
            dsc_ref[...] = jnp.zeros_like(dsc_ref)
            dn_ref[...] = jnp.zeros_like(dn_ref)

        dyc_t = dyc_ref[...]
        nxt = jnp.where(i == nt - 1, 0.0, nxt_ref[...])
        row = lax.broadcasted_iota(jnp.int32, (tm, cw), 0)
        d1 = jnp.where(row == tm - 1, nxt[0:1, :], pltpu.roll(dyc_t, tm - 1, 0))
        d2 = jnp.where(row == tm - 2, nxt[0:1, :], jnp.where(row == tm - 1, nxt[1:2, :], pltpu.roll(dyc_t, tm - 2, 0)))
        du = cwt_ref[2:3, :] * dyc_t + cwt_ref[1:2, :] * d1 + cwt_ref[0:1, :] * d2
        c0 = 0
        dp_ref[:, c0:c0 + cw] = dcb_ref[...]
        dp_ref[:, cw:2 * cw] = (du * cx_ref[:, 2 * cw:3 * cw].astype(F32)).astype(BF16)
        dp_ref[:, 2 * cw:3 * cw] = (du * cx_ref[:, cw:2 * cw].astype(F32)).astype(BF16)
        c0 = 3 * cw
        for ref in (dqk_ref, dv_ref, dgo_ref, dgl_ref):
            wd = ref.shape[1]
            dp_ref[:, c0:c0 + wd] = ref[...]
            c0 += wd
        dh = _dot_nt(dp_ref[...], w_ref[...])
        dx, tsh, tsc, tn = _normmod_bwd(dh, x_ref[...], n_ref[...], ada_ref[4:5, :])
        dx_ref[...] = dxo_ref[...] + dx
        dsh_ref[...] += _rsum8(tsh)
        dsc_ref[...] += _rsum8(tsc)
        dn_ref[...] += _rsum8(tn)

    row_spec = lambda wd: pl.BlockSpec((tm, wd), lambda i: (i, 0))
    vec = pl.BlockSpec((8, d), lambda i: (0, 0))
    return pl.pallas_call(
        body, name=name, grid=(nt,),
        in_specs=[row_spec(cw), pl.BlockSpec((8, cw), lambda i: (jnp.minimum((i + 1) * (tm // 8), t // 8 - 1), 0)),
                  row_spec(3 * cw), row_spec(cw), row_spec(dqk.shape[1]), row_spec(dvv.shape[1]), row_spec(dgo.shape[1]),
                  row_spec(dgl.shape[1]), pl.BlockSpec((3, cw), lambda i: (0, 0)), pl.BlockSpec((d, n), lambda i: (0, 0)),
                  row_spec(d), row_spec(d), pl.BlockSpec((N_ADA, d), lambda i: (0, 0)), pl.BlockSpec((1, d), lambda i: (0, 0))],
        out_shape=[S((t, d), F32), S((t, n), BF16), S((8, d), F32), S((8, d), F32), S((8, d), F32)],
        out_specs=[row_spec(d), row_spec(n), vec, vec, vec],
        compiler_params=_cp("arbitrary"),
    )(dyc, dyc, cx, dcb, dqk, dvv, dgo, dgl, conv_w, w, x, dxo, ada9, nrm)


def _loss_head(x, target, nrm, name):
    t, d = x.shape
    tm = min(ROW_TILE, t)
    nt = t // tm

    def body(x_ref, tg_ref, n_ref, loss_ref, dx_ref, dn_ref, acc_ref):
        i = pl.program_id(0)

        @pl.when(i == 0)
        def _():
            acc_ref[...] = jnp.zeros_like(acc_ref)
            dn_ref[...] = jnp.zeros_like(dn_ref)

        xt = x_ref[...]
        rstd = lax.rsqrt(jnp.mean(xt * xt, axis=-1, keepdims=True) + EPS)
        xhat = xt * rstd
        err = xhat * n_ref[...] - tg_ref[...]
        acc_ref[...] += _rsum8(err * err)
        dy = err * (1.0 / d)
        dn_ref[...] += _rsum8(dy * xhat)
        dxhat = dy * n_ref[...]
        dx_ref[...] = rstd * (dxhat - xhat * jnp.mean(dxhat * xhat, axis=-1, keepdims=True))

        @pl.when(i == nt - 1)
        def _():
            loss_ref[...] = jnp.full(loss_ref.shape, (0.5 / d) * jnp.sum(acc_ref[...]), F32)

    return pl.pallas_call(
        body, name=name, grid=(nt,),
        in_specs=[pl.BlockSpec((tm, d), lambda i: (i, 0)), pl.BlockSpec((tm, d), lambda i: (i, 0)),
                  pl.BlockSpec((1, d), lambda i: (0, 0))],
        out_shape=[S((1, 128), F32), S((t, d), F32), S((8, d), F32)],
        out_specs=[pl.BlockSpec((1, 128), lambda i: (0, 0)), pl.BlockSpec((tm, d), lambda i: (i, 0)),
                   pl.BlockSpec((8, d), lambda i: (0, 0))],
        scratch_shapes=[pltpu.VMEM((8, d), F32)],
        compiler_params=_cp("arbitrary"),
    )(x, target, nrm)


def _pack_smalls(vec_parts, dcw, dbg, dgn, dwg, rank, name):
    d = vec_parts[0].shape[1]
    cw, kw, dv = dcw.shape[2], dbg.shape[1], dgn.shape[1]
    nv = len(vec_parts)
    assert 2 * cw == d and cw + kw + dv <= d and (rank * kw) % d == 0 and nv + 2 + rank * kw // d <= PACK_ROWS
    per_row = d // kw

    def body(*refs):
        vrefs, (dcw_ref, dbg_ref, dgn_ref, dwg_ref, o_ref) = refs[:nv], refs[nv:]
        o_ref[...] = jnp.zeros_like(o_ref)
        for r, ref in enumerate(vrefs):
            o_ref[r:r + 1, :] = jnp.sum(ref[...], axis=0, keepdims=True)
        o_ref[nv:nv + 1, 0:cw] = jnp.sum(dcw_ref[0], axis=0, keepdims=True)
        o_ref[nv:nv + 1, cw:2 * cw] = jnp.sum(dcw_ref[1], axis=0, keepdims=True)
        o_ref[nv + 1:nv + 2, 0:cw] = jnp.sum(dcw_ref[2], axis=0, keepdims=True)
        o_ref[nv + 1:nv + 2, cw:cw + kw] = jnp.sum(dbg_ref[...], axis=0, keepdims=True)
        o_ref[nv + 1:nv + 2, cw + kw:cw + kw + dv] = jnp.sum(dgn_ref[...], axis=0, keepdims=True)
        for r in range(rank):
            o_ref[nv + 2 + r // per_row:nv + 3 + r // per_row, (r % per_row) * kw:(r % per_row + 1) * kw] = dwg_ref[r:r + 1, :]

    return pl.pallas_call(body, name=name, out_shape=S((PACK_ROWS, d), F32), compiler_params=_cp())(*vec_parts, dcw, dbg, dgn, dwg)


def _sum_slots(a, name):
    def body(a_ref, o_ref):
        acc = a_ref[0]
        for s in range(1, NDEV):
            acc = acc + a_ref[s]
        o_ref[...] = acc

    return pl.pallas_call(body, name=name, out_shape=S(a.shape[1:], F32), compiler_params=_cp())(a)


def _adamw(w, g, m, v):
    m = ADAM_B1 * m + (1.0 - ADAM_B1) * g
    v = ADAM_B2 * v + (1.0 - ADAM_B2) * (g * g)
    m_hat = m / (1.0 - ADAM_B1 ** ADAM_STEP)
    v_hat = v / (1.0 - ADAM_B2 ** ADAM_STEP)
    return -ADAM_LR * (m_hat / (jnp.sqrt(v_hat) + ADAM_EPS) + ADAM_WD * w), m, v


def _adam_slots(recv, w, m, v, name):
    r, c = w.shape
    tr = r
    for cand in (256, 128, 176, 88, 64, 32, 16, 8):
        if r % cand == 0 and cand * c * 4 <= 1024 * 1024:
            tr = cand
            break

    def body(recv_ref, w_ref, m_ref, v_ref, g_ref, d_ref, mo_ref, vo_ref):
        g = recv_ref[0].astype(F32)
        for s in range(1, NDEV):
            g = g + recv_ref[s].astype(F32)
        g_ref[...] = g
        d_ref[...], mo_ref[...], vo_ref[...] = _adamw(w_ref[...], g, m_ref[...], v_ref[...])

    blk = pl.BlockSpec((tr, c), lambda i: (i, 0))
    return pl.pallas_call(
        body, name=name, grid=(r // tr,),
        in_specs=[pl.BlockSpec((NDEV, tr, c), lambda i: (0, i, 0)), blk, blk, blk],
        out_shape=[S((r, c), F32)] * 4, out_specs=[blk] * 4, compiler_params=_cp("parallel"),
    )(recv, w, m, v)


def _adam_w_ada(act_t, dada, w, m, v, name):
    r, c = w.shape
    tr = 128
    nb = act_t.shape[1]

    def body(a_ref, da_ref, w_ref, m_ref, v_ref, g_ref, d_ref, mo_ref, vo_ref):
        g = a_ref[:, 0:1] * da_ref[0:1, :]
        for b in range(1, nb):
            g = g + a_ref[:, b:b + 1] * da_ref[b:b + 1, :]
        g_ref[...] = g
        d_ref[...], mo_ref[...], vo_ref[...] = _adamw(w_ref[...], g, m_ref[...], v_ref[...])

    blk = pl.BlockSpec((tr, c), lambda i: (i, 0))
    return pl.pallas_call(
        body, name=name, grid=(r // tr,),
        in_specs=[pl.BlockSpec((tr, nb), lambda i: (i, 0)), pl.BlockSpec((nb, c), lambda i: (0, 0)), blk, blk, blk],
        out_shape=[S((r, c), F32)] * 4, out_specs=[blk] * 4, compiler_params=_cp("parallel"),
    )(act_t, dada, w, m, v)


def _adam_smalls(ws, gs, ms, vs, name):
    n = len(ws)

    def body(*refs):
        w_r, g_r, m_r, v_r = (refs[k * n:(k + 1) * n] for k in range(4))
        d_o, m_o, v_o = (refs[(4 + k) * n:(5 + k) * n] for k in range(3))
        for i in range(n):
            d_o[i][...], m_o[i][...], v_o[i][...] = _adamw(w_r[i][...], g_r[i][...], m_r[i][...], v_r[i][...])

    shapes = [S(w.shape, F32) for w in ws]
    outs = pl.pallas_call(body, name=name, out_shape=shapes * 3, compiler_params=_cp())(*ws, *gs, *ms, *vs)
    return outs[:n], outs[n:2 * n], outs[2 * n:]


def kernel(x, c, w_ada, b_ada, norm_ffn1, w_ffn1_in, w_ffn1_out, norm_mix, w_mix_in, conv_w, w_gk2, b_gk, gla_norm, w_mix_out, norm_ffn2, w_ffn2_in, w_ffn2_out, norm_final, loss_target, m_w_ada, m_b_ada, m_norm_ffn1, m_w_ffn1_in, m_w_ffn1_out, m_norm_mix, m_w_mix_in, m_conv_w, m_w_gk2, m_b_gk, m_gla_norm, m_w_mix_out, m_norm_ffn2, m_w_ffn2_in, m_w_ffn2_out, m_norm_final, v_w_ada, v_b_ada, v_norm_ffn1, v_w_ffn1_in, v_w_ffn1_out, v_norm_mix, v_w_mix_in, v_conv_w, v_w_gk2, v_b_gk, v_gla_norm, v_w_mix_out, v_norm_ffn2, v_w_ffn2_in, v_w_ffn2_out, v_norm_final):
    t, d = x.shape[1], x.shape[2]
    x0, tgt = x[0], loss_target[0]
    rank, kw = w_gk2.shape[1], w_gk2.shape[2] * NDEV
    cw = conv_w.shape[2] * NDEV
    dv = gla_norm.shape[1]
    vw = d - cw
    heads = vw // dv
    mix_cols = w_mix_in.shape[2]
    widths = [3 * cw, 2 * kw, vw, vw, 128]
    n_proj = 3 * cw + 2 * kw + 2 * vw + rank
    assert n_proj == mix_cols * NDEV and rank <= 128
    me = 4 * lax.axis_index("x") + 2 * lax.axis_index("y") + lax.axis_index("c")

    bf = lambda a: a[0].astype(BF16)
    (c_all, w1i, w1o, wmi, wmo, w2i, w2o, cwt_all, wg_all) = _exchange(
        [c, bf(w_ffn1_in), bf(w_ffn1_out), bf(w_mix_in), bf(w_mix_out), bf(w_ffn2_in), bf(w_ffn2_out), conv_w[0], w_gk2[0]],
        True, "gather_weights")
    nb = NDEV // 2
    w1o = w1o.reshape(nb, -1, d)
    w2o = w2o.reshape(nb, -1, d)
    wmo = wmo.reshape(cw + vw, d)
    wmi = jnp.pad(wmi.transpose(1, 0, 2).reshape(d, n_proj), ((0, 0), (0, sum(widths) - n_proj)))
    cwt = cwt_all.transpose(1, 0, 2).reshape(conv_w.shape[1], cw)
    wg = jnp.pad(wg_all.transpose(1, 0, 2).reshape(rank, kw), ((0, 128 - rank), (0, 0))).astype(BF16)

    act_all, p_ada = _ada_partial(c_all.reshape(NDEV, d), w_ada[0])
    (p_all,) = _exchange([p_ada], True, "gather_ada")
    ada9 = _ada_finish(p_all, b_ada).reshape(N_ADA, d)

    h1, gu1, s1 = _ffn_in(x0, ada9, norm_ffn1, w1i, 0, 1, "ffn1_in")
    x1, f1 = _ffn_out(s1, w1o, x0, ada9, 2, 0.5, "ffn1_out")
    h2, cx, qk, vv, go, gl = _mix_in(x1, ada9, norm_mix, wmi, widths, [BF16, F32, BF16, BF16, F32], "mix_in")
    o, lg, sall = _gla_fwd(qk, vv, gl, wg, b_gk, heads, "gla_fwd")
    x2, mm, ycat = _mix_out(cx, o, go, cwt, gla_norm, wmo, x1, ada9, heads, "mix_out")
    h3, gu3, s3 = _ffn_in(x2, ada9, norm_ffn2, w2i, 6, 7, "ffn2_in")
    x3, f3 = _ffn_out(s3, w2o, x2, ada9, 8, 0.5, "ffn2_out")
    loss_v, dx3, dnf = _loss_head(x3, tgt, norm_final.reshape(1, d), "loss_head")

    dx2, dw2i, dw2o, (dsh3, dsc3, dg3, dn3) = _ffn_backward(dx3, x2, h3, gu3, s3, f3, ada9, norm_ffn2, w2i, w2o, (6, 7, 8), "ffn2_bwd")
    dm, dyc, dcb, do, dgo, dg2, dcw, dgn = _mix_bwd_a(dx2, mm, ada9, wmo, cx, o, go, cwt, gla_norm, heads, "mix_bwd_a")
    dqk, dvv, dgl, dwg, dbg = _gla_bwd(qk, vv, lg, do, sall, gl, wg, heads, "gla_bwd")
    dx1, dproj, dsh2, dsc2, dnm = _mix_bwd_b(dyc, cx, dcb, dqk, dvv, dgo, dgl, cwt, wmi, x1, dx2, ada9, norm_mix, "mix_bwd_b")
    n_pad = sum(widths)
    tn = n_pad // 5
    dwmi = _tn_matmul(h2, dproj, lambda tk: (tk, d), lambda sb, k: (k, 0), lambda tk: (tk, tn), lambda sb, k: (k, sb),
                      (d, n_pad), (d, tn), lambda sb, k: (0, sb), 5, "mix_dwin")
    dwmo = _tn_matmul(ycat, dm, lambda tk: (tk, cw + vw), lambda sb, k: (k, 0), lambda tk: (tk, d), lambda sb, k: (k, 0),
                      (cw + vw, d), (cw + vw, d), lambda sb, k: (0, 0), 1, "mix_dwout")
    dx0, dw1i, dw1o, (dsh1, dsc1, dg1, dn1) = _ffn_backward(dx1, x0, h1, gu1, s1, f1, ada9, norm_ffn1, w1i, w1o, (0, 1, 2), "ffn1_bwd")

    dwmi8 = dwmi[:, :n_proj].reshape(d, NDEV, mix_cols).transpose(1, 0, 2)
    pack = _pack_smalls([dn1, dnm, dn3, dnf, dsh1, dsc1, dg1, dsh2, dsc2, dg2, dsh3, dsc3, dg3], dcw, dbg, dgn, dwg, rank, "pack_smalls")
    r1i, r1o, rmi, rmo, r2i, r2o = _exchange(
        [dw1i, dw1o.reshape(NDEV, -1, d), dwmi8, dwmo.reshape(NDEV, -1, d), dw2i, dw2o.reshape(NDEV, -1, d)], False, "scatter_grads")
    (pack_all,) = _exchange([pack], True, "gather_smalls")
    tot = _sum_slots(pack_all, "sum_smalls")

    res = {}
    for nm, recv, w, m, v in (("w_ffn1_in", r1i, w_ffn1_in, m_w_ffn1_in, v_w_ffn1_in), ("w_ffn1_out", r1o, w_ffn1_out, m_w_ffn1_out, v_w_ffn1_out),
                              ("w_mix_in", rmi, w_mix_in, m_w_mix_in, v_w_mix_in), ("w_mix_out", rmo, w_mix_out, m_w_mix_out, v_w_mix_out),
                              ("w_ffn2_in", r2i, w_ffn2_in, m_w_ffn2_in, v_w_ffn2_in), ("w_ffn2_out", r2o, w_ffn2_out, m_w_ffn2_out, v_w_ffn2_out)):
        res[nm] = [a[None] for a in _adam_slots(recv, w[0], m[0], v[0], "adam_" + nm)]

    cols_ada = w_ada.shape[2]
    dada_all = pack_all[:, 4:4 + N_ADA, :].reshape(NDEV, N_ADA * d)
    dada_mine = lax.dynamic_slice_in_dim(dada_all, me * cols_ada, cols_ada, axis=1)
    res["w_ada"] = [a[None] for a in _adam_w_ada(act_all.T, dada_mine, w_ada[0], m_w_ada[0], v_w_ada[0], "adam_w_ada")]

    nv = 4 + N_ADA
    g_small = {
        "b_ada": tot[4:nv].reshape(1, N_ADA * d),
        "norm_ffn1": tot[0:1], "norm_mix": tot[1:2], "norm_ffn2": tot[2:3], "norm_final": tot[3:4],
        "conv_w": lax.dynamic_slice_in_dim(
            jnp.concatenate([tot[nv:nv + 1, 0:cw], tot[nv:nv + 1, cw:2 * cw], tot[nv + 1:nv + 2, 0:cw]], axis=0), me * (cw // NDEV), cw // NDEV, axis=1),
        "w_gk2": lax.dynamic_slice_in_dim(tot[nv + 2:nv + 2 + rank * kw // d].reshape(rank, kw), me * (kw // NDEV), kw // NDEV, axis=1),
        "b_gk": tot[nv + 1:nv + 2, cw:cw + kw],
        "gla_norm": tot[nv + 1:nv + 2, cw + kw:cw + kw + dv],
    }
    small = {"b_ada": (b_ada, m_b_ada, v_b_ada), "norm_ffn1": (norm_ffn1, m_norm_ffn1, v_norm_ffn1), "norm_mix": (norm_mix, m_norm_mix, v_norm_mix),
             "norm_ffn2": (norm_ffn2, m_norm_ffn2, v_norm_ffn2), "norm_final": (norm_final, m_norm_final, v_norm_final),
             "conv_w": (conv_w, m_conv_w, v_conv_w), "w_gk2": (w_gk2, m_w_gk2, v_w_gk2), "b_gk": (b_gk, m_b_gk, v_b_gk),
             "gla_norm": (gla_norm, m_gla_norm, v_gla_norm)}
    names = list(small)
    flat = lambda a: a.reshape(-1, a.shape[-1])
    dl, mo, vo = _adam_smalls([flat(small[n][0]) for n in names], [g_small[n] for n in names],
                              [flat(small[n][1]) for n in names], [flat(small[n][2]) for n in names], "adam_smalls")
    for i, n in enumerate(names):
        shp = small[n][0].shape
        res[n] = [g_small[n].reshape(shp), dl[i].reshape(shp), mo[i].reshape(shp), vo[i].reshape(shp)]

    loss = lax.psum(loss_v[0, 0], ("x", "y", "c"))
    order = ["w_ada", "b_ada", "norm_ffn1", "w_ffn1_in", "w_ffn1_out", "norm_mix", "w_mix_in", "conv_w", "w_gk2", "b_gk", "gla_norm",
             "w_mix_out", "norm_ffn2", "w_ffn2_in", "w_ffn2_out", "norm_final"]
    return (loss, dx0[None], *[res[n][0] for n in order], *[res[n][1] for n in order], *[res[n][2] for n in order], *[res[n][3] for n in order])
```

```python
import functools

import jax
import jax.numpy as jnp
from jax import lax
from jax.experimental import pallas as pl
from jax.experimental.pallas import tpu as pltpu

F32 = jnp.float32
BF16 = jnp.bfloat16
S = jax.ShapeDtypeStruct

NDEV = 8
EPS = 1e-6
GATE_NORMALIZER = 16.0
CHUNK = 64
N_ADA = 9
ADAM_LR, ADAM_B1, ADAM_B2, ADAM_EPS, ADAM_WD, ADAM_STEP = 0.001, 0.9, 0.999, 1e-08, 0.01, 10
V7X_VMEM_LIMIT = 56 * 1024 * 1024
ROW_TILE = 512
K_TILE = 1024
PACK_ROWS = 24
ANY = pl.BlockSpec(memory_space=pl.ANY)


def _cp(*sem):
    return pltpu.CompilerParams(dimension_semantics=sem or None, vmem_limit_bytes=V7X_VMEM_LIMIT)


def _dot(a, b):
    return jnp.dot(a, b, preferred_element_type=F32)


def _dot_nt(a, b):
    return lax.dot_general(a, b, (((1,), (1,)), ((), ())), preferred_element_type=F32)


def _dot_tn(a, b):
    return lax.dot_general(a, b, (((0,), (0,)), ((), ())), preferred_element_type=F32)


def _rsum8(a):
    r, c = a.shape
    return jnp.sum(a.reshape(r // 8, 8, c), axis=0)


def _row_tile(r, c):
    for cand in (256, 128, 176, 88, 64, 32, 16, 8):
        if r % cand == 0 and cand * c * 4 <= 1024 * 1024:
            return cand
    return r


def _sigmoid(x):
    return 1.0 / (1.0 + jnp.exp(-x))


def _normmod(x, nrm, sc, sh):
    rstd = lax.rsqrt(jnp.mean(x * x, axis=-1, keepdims=True) + EPS)
    xhat = x * rstd
    return xhat, rstd, (xhat * nrm) * (1.0 + sc) + sh


def _normmod_bwd(dh, x, nrm, sc):
    rstd = lax.rsqrt(jnp.mean(x * x, axis=-1, keepdims=True) + EPS)
    xhat = x * rstd
    dxhat = dh * (nrm * (1.0 + sc))
    dx = rstd * (dxhat - xhat * jnp.mean(dxhat * xhat, axis=-1, keepdims=True))
    return dx, dh, dh * (xhat * nrm), dh * ((1.0 + sc) * xhat)


def _place():
    x, y, c = lax.axis_index("x"), lax.axis_index("y"), lax.axis_index("c")
    return x, y, c, 4 * x + 2 * y + c


def _peer(x, y, c, k):
    px = 1 - x if k & 4 else x
    py = 1 - y if k & 2 else y
    pc = 1 - c if k & 1 else c
    return (px, py, pc), 4 * px + 2 * py + pc


def _exchange(xs, gather, name):
    n = len(xs)

    def body(*refs):
        ins, outs = refs[:n], refs[n:2 * n]
        send_sems, recv_sems, local_sems = refs[2 * n:]
        x, y, c, me = _place()
        local = []
        for i in range(n):
            src = ins[i] if gather else ins[i].at[me]
            cp = pltpu.make_async_copy(src, outs[i].at[me], local_sems.at[i])
            cp.start()
            local.append(cp)
        sends = []
        for k in range(1, NDEV):
            peer, pid = _peer(x, y, c, k)
            for i in range(n):
                src = ins[i] if gather else ins[i].at[pid]
                cp = pltpu.make_async_remote_copy(
                    src_ref=src, dst_ref=outs[i].at[me], send_sem=send_sems.at[i, k - 1], recv_sem=recv_sems.at[i, k - 1],
                    device_id=peer, device_id_type=pl.DeviceIdType.MESH)
                cp.start()
                sends.append(cp)
        for k in range(1, NDEV):
            peer, pid = _peer(x, y, c, k)
            for i in range(n):
                src = ins[i] if gather else ins[i].at[pid]
                pltpu.make_async_remote_copy(
                    src_ref=src, dst_ref=outs[i].at[pid], send_sem=send_sems.at[i, k - 1], recv_sem=recv_sems.at[i, k - 1],
                    device_id=peer, device_id_type=pl.DeviceIdType.MESH).wait_recv()
        for cp in sends:
            cp.wait_send()
        for cp in local:
            cp.wait()

    out_shape = [S((NDEV,) + a.shape, a.dtype) if gather else S(a.shape, a.dtype) for a in xs]
    return pl.pallas_call(
        body, name=name, out_shape=out_shape, in_specs=[ANY] * n, out_specs=[ANY] * n,
        scratch_shapes=[pltpu.SemaphoreType.DMA((n, NDEV - 1)), pltpu.SemaphoreType.DMA((n, NDEV - 1)),
                        pltpu.SemaphoreType.DMA((n,))],
    )(*xs)


def _remote(src, dst, send_sem, recv_sem, peer):
    return pltpu.make_async_remote_copy(src_ref=src, dst_ref=dst, send_sem=send_sem, recv_sem=recv_sem,
                                        device_id=peer, device_id_type=pl.DeviceIdType.MESH)


def _other_chips(x, y):
    return [(1 - x, y), (x, 1 - y), (1 - x, 1 - y)]


def _gather_two_level(xs, name):
    n = len(xs)

    def body(*refs):
        ins, outs = refs[:n], refs[n:2 * n]
        send_sems, recv_sems, local_sems = refs[2 * n:]
        x, y, c, me = _place()
        sib, sib_id = (x, y, 1 - c), 4 * x + 2 * y + 1 - c
        chips = _other_chips(x, y)
        local = [pltpu.make_async_copy(ins[i], outs[i].at[me], local_sems.at[i]) for i in range(n)]
        for cp in local:
            cp.start()
        started = []
        for i in range(n):
            started.append(_remote(ins[i], outs[i].at[me], send_sems.at[i, 0], recv_sems.at[i, 0], sib))
            for j, (px, py) in enumerate(chips):
                started.append(_remote(ins[i], outs[i].at[me], send_sems.at[i, 1 + j], recv_sems.at[i, 1 + j], (px, py, c)))
        for cp in started:
            cp.start()
        for j, (px, py) in enumerate(chips):
            slot = 4 * px + 2 * py + c
            for i in range(n):
                _remote(ins[i], outs[i].at[slot], send_sems.at[i, 1 + j], recv_sems.at[i, 1 + j], (px, py, c)).wait_recv()
                fwd = _remote(outs[i].at[slot], outs[i].at[slot], send_sems.at[i, 4 + j], recv_sems.at[i, 4 + j], sib)
                fwd.start()
                started.append(fwd)
        for i in range(n):
            _remote(ins[i], outs[i].at[sib_id], send_sems.at[i, 0], recv_sems.at[i, 0], sib).wait_recv()
            for j, (px, py) in enumerate(chips):
                slot = 4 * px + 2 * py + 1 - c
                _remote(ins[i], outs[i].at[slot], send_sems.at[i, 4 + j], recv_sems.at[i, 4 + j], sib).wait_recv()
        for cp in started:
            cp.wait_send()
        for cp in local:
            cp.wait()

    return pl.pallas_call(
        body, name=name, out_shape=[S((NDEV,) + a.shape, a.dtype) for a in xs], in_specs=[ANY] * n, out_specs=[ANY] * n,
        scratch_shapes=[pltpu.SemaphoreType.DMA((n, NDEV - 1)), pltpu.SemaphoreType.DMA((n, NDEV - 1)),
                        pltpu.SemaphoreType.DMA((n,))],
    )(*xs)


def _swap_with_sibling(gs, name):
    n = len(gs)

    def body(*refs):
        ins, outs = refs[:n], refs[n:2 * n]
        send_sems, recv_sems = refs[2 * n:]
        x, y, c, _ = _place()
        sib = (x, y, 1 - c)
        started = []
        for i in range(n):
            for j in range(NDEV // 2):
                started.append(_remote(ins[i].at[2 * j + 1 - c], outs[i].at[j], send_sems.at[i, j], recv_sems.at[i, j], sib))
        for cp in started:
            cp.start()
        for cp in started:
            cp.wait_recv()
        for cp in started:
            cp.wait_send()

    return pl.pallas_call(
        body, name=name, out_shape=[S((NDEV // 2,) + a.shape[1:], a.dtype) for a in gs], in_specs=[ANY] * n, out_specs=[ANY] * n,
        scratch_shapes=[pltpu.SemaphoreType.DMA((n, NDEV // 2)), pltpu.SemaphoreType.DMA((n, NDEV // 2))],
    )(*gs)


def _pair_add(g, r1, core, name):
    _, r, c = g.shape
    tr = _row_tile(r, c)

    def body(core_ref, g_ref, r_ref, o_ref):
        o_ref[...] = (g_ref[...].astype(F32) + r_ref[...].astype(F32)).astype(BF16)

    return pl.pallas_call(
        body, name=name,
        grid_spec=pltpu.PrefetchScalarGridSpec(
            num_scalar_prefetch=1, grid=(NDEV // 2, r // tr),
            in_specs=[pl.BlockSpec((None, tr, c), lambda j, k, core_ref: (2 * j + core_ref[0], k, 0)),
                      pl.BlockSpec((None, tr, c), lambda j, k, core_ref: (j, k, 0))],
            out_specs=pl.BlockSpec((None, tr, c), lambda j, k, core_ref: (j, k, 0))),
        out_shape=S((NDEV // 2, r, c), BF16), compiler_params=_cp("parallel", "parallel"),
    )(core, g, r1)


def _swap_between_chips(ps, name):
    n = len(ps)

    def body(*refs):
        ins, outs = refs[:n], refs[n:2 * n]
        send_sems, recv_sems, local_sems = refs[2 * n:]
        x, y, c, _ = _place()
        mine = 2 * x + y
        chips = _other_chips(x, y)
        local = [pltpu.make_async_copy(ins[i].at[mine], outs[i].at[mine], local_sems.at[i]) for i in range(n)]
        for cp in local:
            cp.start()
        started = []
        for j, (px, py) in enumerate(chips):
            for i in range(n):
                started.append(_remote(ins[i].at[2 * px + py], outs[i].at[mine], send_sems.at[i, j], recv_sems.at[i, j], (px, py, c)))
        for cp in started:
            cp.start()
        for j, (px, py) in enumerate(chips):
            for i in range(n):
                _remote(ins[i].at[mine], outs[i].at[2 * px + py], send_sems.at[i, j], recv_sems.at[i, j], (px, py, c)).wait_recv()
        for cp in started:
            cp.wait_send()
        for cp in local:
            cp.wait()

    return pl.pallas_call(
        body, name=name, out_shape=[S(a.shape, a.dtype) for a in ps], in_specs=[ANY] * n, out_specs=[ANY] * n,
        scratch_shapes=[pltpu.SemaphoreType.DMA((n, 3)), pltpu.SemaphoreType.DMA((n, 3)), pltpu.SemaphoreType.DMA((n,))],
    )(*ps)


def _ada_partial(c_all, w_ada):
    nb, d = c_all.shape
    cols = w_ada.shape[1]

    def body(c_ref, w_ref, act_ref, p_ref):
        cc = c_ref[...]
        act = cc * _sigmoid(cc)
        act_ref[...] = act
        p_ref[...] = _dot(act.astype(BF16), w_ref[...].astype(BF16))

    return pl.pallas_call(body, name="ada_partial", out_shape=[S((nb, d), F32), S((nb, cols), F32)],
                          compiler_params=_cp())(c_all, w_ada)


def _ada_finish(p_all, b_ada):
    _, nb, cols = p_all.shape

    def body(p_ref, b_ref, o_ref):
        me = _place()[3]
        for s in range(NDEV):
            o_ref[:, s * cols:(s + 1) * cols] = p_ref[s, pl.ds(me, 1), :] + b_ref[:, s * cols:(s + 1) * cols]

    return pl.pallas_call(body, name="ada_finish", out_shape=S((1, NDEV * cols), F32), compiler_params=_cp())(p_all, b_ada)


def _ffn_in(x, ada9, nrm, w_in, sh_row, sc_row, name):
    t, d = x.shape
    nb, bw = w_in.shape[0] // 2, w_in.shape[2]
    tm = min(ROW_TILE, t)

    def body(x_ref, ada_ref, n_ref, wg_ref, wu_ref, h_ref, gu_ref, s_ref):
        @pl.when(pl.program_id(1) == 0)
        def _():
            _, _, h = _normmod(x_ref[...], n_ref[...], ada_ref[sc_row:sc_row + 1, :], ada_ref[sh_row:sh_row + 1, :])
            h_ref[...] = h.astype(BF16)

        h = h_ref[...]
        g = _dot(h, wg_ref[...])
        u = _dot(h, wu_ref[...])
        gu_ref[0] = g.astype(BF16)
        gu_ref[1] = u.astype(BF16)
        s_ref[...] = (g * _sigmoid(g) * u).astype(BF16)

    return pl.pallas_call(
        body, name=name, grid=(t // tm, nb),
        in_specs=[pl.BlockSpec((tm, d), lambda i, j: (i, 0)), pl.BlockSpec((N_ADA, d), lambda i, j: (0, 0)),
                  pl.BlockSpec((1, d), lambda i, j: (0, 0)),
                  pl.BlockSpec((None, d, bw), lambda i, j: (j, 0, 0)), pl.BlockSpec((None, d, bw), lambda i, j: (j + nb, 0, 0))],
        out_shape=[S((t, d), BF16), S((2, nb, t, bw), BF16), S((nb, t, bw), BF16)],
        out_specs=[pl.BlockSpec((tm, d), lambda i, j: (i, 0)), pl.BlockSpec((2, None, tm, bw), lambda i, j: (0, j, i, 0)),
                   pl.BlockSpec((None, tm, bw), lambda i, j: (j, i, 0))],
        compiler_params=_cp("parallel", "arbitrary"),
    )(x, ada9, nrm, w_in, w_in)


def _ffn_out(s, w_out, x, ada9, g_row, res_scale, name):
    nb, t, bw = s.shape
    d = x.shape[1]
    tm = min(ROW_TILE, t)

    def body(s_ref, w_ref, x_ref, ada_ref, xo_ref, f_ref):
        acc = _dot(s_ref[0], w_ref[0])
        for b in range(1, nb):
            acc = acc + _dot(s_ref[b], w_ref[b])
        f_ref[...] = acc.astype(BF16)
        xo_ref[...] = x_ref[...] + (res_scale * ada_ref[g_row:g_row + 1, :]) * acc

    return pl.pallas_call(
        body, name=name, grid=(t // tm,),
        in_specs=[pl.BlockSpec((nb, tm, bw), lambda i: (0, i, 0)), pl.BlockSpec((nb, bw, d), lambda i: (0, 0, 0)),
                  pl.BlockSpec((tm, d), lambda i: (i, 0)), pl.BlockSpec((N_ADA, d), lambda i: (0, 0))],
        out_shape=[S((t, d), F32), S((t, d), BF16)],
        out_specs=[pl.BlockSpec((tm, d), lambda i: (i, 0)), pl.BlockSpec((tm, d), lambda i: (i, 0))],
        compiler_params=_cp("parallel"),
    )(s, w_out, x, ada9)


def _ffn_bwd_ds(dxo, f, ada9, w_out, gu, g_row, res_scale, name):
    t, d = dxo.shape
    nb, bw = w_out.shape[0], w_out.shape[1]
    tm = min(ROW_TILE, t)

    def body(dxo_ref, f_ref, ada_ref, w_ref, gu_ref, df_ref, da_ref, dg_ref):
        i, j = pl.program_id(0), pl.program_id(1)

        @pl.when((i == 0) & (j == 0))
        def _():
            dg_ref[...] = jnp.zeros_like(dg_ref)

        @pl.when(j == 0)
        def _():
            dxo_t = dxo_ref[...]
            df_ref[...] = ((res_scale * ada_ref[g_row:g_row + 1, :]) * dxo_t).astype(BF16)
            dg_ref[...] += res_scale * _rsum8(dxo_t * f_ref[...].astype(F32))

        ds = _dot_nt(df_ref[...], w_ref[...])
        g = gu_ref[0].astype(F32)
        u = gu_ref[1].astype(F32)
        sg = _sigmoid(g)
        da_ref[0] = (ds * u * (sg * (1.0 + g * (1.0 - sg)))).astype(BF16)
        da_ref[1] = (ds * (g * sg)).astype(BF16)

    return pl.pallas_call(
        body, name=name, grid=(t // tm, nb),
        in_specs=[pl.BlockSpec((tm, d), lambda i, j: (i, 0)), pl.BlockSpec((tm, d), lambda i, j: (i, 0)),
                  pl.BlockSpec((N_ADA, d), lambda i, j: (0, 0)), pl.BlockSpec((None, bw, d), lambda i, j: (j, 0, 0)),
                  pl.BlockSpec((2, None, tm, bw), lambda i, j: (0, j, i, 0))],
        out_shape=[S((t, d), BF16), S((2, nb, t, bw), BF16), S((8, d), F32)],
        out_specs=[pl.BlockSpec((tm, d), lambda i, j: (i, 0)), pl.BlockSpec((2, None, tm, bw), lambda i, j: (0, j, i, 0)),
                   pl.BlockSpec((8, d), lambda i, j: (0, 0))],
        compiler_params=_cp("arbitrary", "arbitrary"),
    )(dxo, f, ada9, w_out, gu)


def _ffn_bwd_dh(da, w_in, x, dxo, ada9, nrm, sh_row, sc_row, name):
    t, d = x.shape
    nb, bw = w_in.shape[0] // 2, w_in.shape[2]
    tm = min(ROW_TILE, t)

    def body(da_ref, wg_ref, wu_ref, x_ref, dxo_ref, ada_ref, n_ref, dx_ref, dsh_ref, dsc_ref, dn_ref, acc_ref):
        i, j = pl.program_id(0), pl.program_id(1)

        @pl.when((i == 0) & (j == 0))
        def _():
            dsh_ref[...] = jnp.zeros_like(dsh_ref)
            dsc_ref[...] = jnp.zeros_like(dsc_ref)
            dn_ref[...] = jnp.zeros_like(dn_ref)

        part = _dot_nt(da_ref[0], wg_ref[...]) + _dot_nt(da_ref[1], wu_ref[...])

        @pl.when(j == 0)
        def _():
            acc_ref[...] = part

        @pl.when(j > 0)
        def _():
            acc_ref[...] += part

        @pl.when(j == nb - 1)
        def _():
            dx, tsh, tsc, tn = _normmod_bwd(acc_ref[...], x_ref[...], n_ref[...], ada_ref[sc_row:sc_row + 1, :])
            dx_ref[...] = dxo_ref[...] + dx
            dsh_ref[...] += _rsum8(tsh)
            dsc_ref[...] += _rsum8(tsc)
            dn_ref[...] += _rsum8(tn)

    vec = pl.BlockSpec((8, d), lambda i, j: (0, 0))
    return pl.pallas_call(
        body, name=name, grid=(t // tm, nb),
        in_specs=[pl.BlockSpec((2, None, tm, bw), lambda i, j: (0, j, i, 0)),
                  pl.BlockSpec((None, d, bw), lambda i, j: (j, 0, 0)), pl.BlockSpec((None, d, bw), lambda i, j: (j + nb, 0, 0)),
                  pl.BlockSpec((tm, d), lambda i, j: (i, 0)), pl.BlockSpec((tm, d), lambda i, j: (i, 0)),
                  pl.BlockSpec((N_ADA, d), lambda i, j: (0, 0)), pl.BlockSpec((1, d), lambda i, j: (0, 0))],
        out_shape=[S((t, d), F32), S((8, d), F32), S((8, d), F32), S((8, d), F32)],
        out_specs=[pl.BlockSpec((tm, d), lambda i, j: (i, 0)), vec, vec, vec],
        scratch_shapes=[pltpu.VMEM((tm, d), F32)],
        compiler_params=_cp("arbitrary", "arbitrary"),
    )(da, w_in, w_in, x, dxo, ada9, nrm)


def _tn_matmul(a, b, a_block, a_map, b_block, b_map, out_shape, out_block, out_map, nblk, name):
    t = a.shape[-2]
    tk = min(K_TILE, t)
    nk = t // tk

    def body(a_ref, b_ref, o_ref, acc_ref):
        part = _dot_tn(a_ref[...], b_ref[...])

        @pl.when(pl.program_id(1) == 0)
        def _():
            acc_ref[...] = part

        @pl.when(pl.program_id(1) > 0)
        def _():
            acc_ref[...] += part

        @pl.when(pl.program_id(1) == nk - 1)
        def _():
            o_ref[...] = acc_ref[...].astype(BF16)

    return pl.pallas_call(
        body, name=name, grid=(nblk, nk),
        in_specs=[pl.BlockSpec(a_block(tk), a_map), pl.BlockSpec(b_block(tk), b_map)],
        out_shape=S(out_shape, BF16), out_specs=pl.BlockSpec(out_block, out_map),
        scratch_shapes=[pltpu.VMEM(tuple(n for n in out_block if n is not None), F32)],
        compiler_params=_cp("parallel", "arbitrary"),
    )(a, b)


def _ffn_backward(dxo, x_in, h, gu, s, f, ada9, nrm, w_in, w_out, rows, name):
    sh_row, sc_row, g_row = rows
    nb, t, bw = s.shape
    d = x_in.shape[1]
    df, da, dg = _ffn_bwd_ds(dxo, f, ada9, w_out, gu, g_row, 0.5, name + "_ds")
    dw_out = _tn_matmul(s, df, lambda tk: (None, tk, bw), lambda sb, k: (sb, k, 0), lambda tk: (tk, d), lambda sb, k: (k, 0),
                        (nb, bw, d), (None, bw, d), lambda sb, k: (sb, 0, 0), nb, name + "_dwout")
    da8 = da.reshape(2 * nb, t, bw)
    dw_in = _tn_matmul(h, da8, lambda tk: (tk, d), lambda sb, k: (k, 0), lambda tk: (None, tk, bw), lambda sb, k: (sb, k, 0),
                       (2 * nb, d, bw), (None, d, bw), lambda sb, k: (sb, 0, 0), 2 * nb, name + "_dwin")
    dx, dsh, dsc, dn = _ffn_bwd_dh(da, w_in, x_in, dxo, ada9, nrm, sh_row, sc_row, name + "_dh")
    return dx, dw_in, dw_out, (dsh, dsc, dg, dn)


def _mix_in(x, ada9, nrm, w, widths, dts, name):
    t, d = x.shape
    n = w.shape[1]
    tm = min(ROW_TILE, t)
    starts = [sum(widths[:i]) for i in range(len(widths))]

    def body(x_ref, ada_ref, n_ref, w_ref, h_ref, *out_refs):
        _, _, h = _normmod(x_ref[...], n_ref[...], ada_ref[4:5, :], ada_ref[3:4, :])
        hb = h.astype(BF16)
        h_ref[...] = hb
        for o_ref, st, wd in zip(out_refs, starts, widths):
            o_ref[...] = _dot(hb, w_ref[:, st:st + wd]).astype(o_ref.dtype)

    return pl.pallas_call(
        body, name=name, grid=(t // tm,),
        in_specs=[pl.BlockSpec((tm, d), lambda i: (i, 0)), pl.BlockSpec((N_ADA, d), lambda i: (0, 0)),
                  pl.BlockSpec((1, d), lambda i: (0, 0)), pl.BlockSpec((d, n), lambda i: (0, 0))],
        out_shape=[S((t, d), BF16)] + [S((t, wd), dt) for wd, dt in zip(widths, dts)],
        out_specs=[pl.BlockSpec((tm, d), lambda i: (i, 0))] + [pl.BlockSpec((tm, wd), lambda i: (i, 0)) for wd in widths],
        compiler_params=_cp("parallel"),
    )(x, ada9, nrm, w)


def _tri(lower):
    r = lax.broadcasted_iota(jnp.int32, (CHUNK, CHUNK), 0)
    c = lax.broadcasted_iota(jnp.int32, (CHUNK, CHUNK), 1)
    return (r >= c) if lower else (c >= r)


def _dot_01(m, x):
    hi = x.astype(BF16)
    r1 = x - hi.astype(F32)
    mid = r1.astype(BF16)
    lo = (r1 - mid.astype(F32)).astype(BF16)
    return _dot(m, hi) + _dot(m, mid) + _dot(m, lo)


def _gla_chunk_terms(q, k, lg, low01):
    b = _dot_01(low01, lg)
    bl = b[CHUNK - 1:CHUNK, :]
    r = 0.5 * bl
    eb, ebl, em, en = jnp.exp(b), jnp.exp(bl - b), jnp.exp(b - r), jnp.exp(r - b)
    return eb, ebl, em, en, jnp.exp(bl), q * eb, k * ebl, q * em, k * en


def _scores(qm_h, knp, qk1_h):
    r = lax.broadcasted_iota(jnp.int32, (CHUNK, CHUNK), 0)
    c = lax.broadcasted_iota(jnp.int32, (CHUNK, CHUNK), 1)
    p = jnp.where(r > c, _dot_nt(qm_h, knp), 0.0)
    return jnp.where(r == c, jnp.sum(qk1_h, axis=1, keepdims=True), p)


def _gla_fwd(qk, v, gl, wg, bg, heads, name):
    t = qk.shape[0]
    kw, vw = qk.shape[1] // 2, v.shape[1]
    dk, dv = kw // heads, vw // heads
    assert dk == 64 and dv == 128 and kw % 128 == 0
    gt = min(ROW_TILE, t)
    nc = gt // CHUNK
    scale = dk ** -0.5

    def body(qk_ref, v_ref, gl_ref, wg_ref, bg_ref, o_ref, lg_ref, sall_ref, st_ref):
        @pl.when(pl.program_id(0) == 0)
        def _():
            st_ref[...] = jnp.zeros_like(st_ref)

        gk = _dot(gl_ref[...].astype(BF16), wg_ref[...]) + bg_ref[...]
        lg_ref[...] = (jnp.minimum(gk, 0.0) - jnp.log(1.0 + jnp.exp(-jnp.abs(gk)))) / GATE_NORMALIZER
        low01 = _tri(True).astype(BF16)
        lane = lax.broadcasted_iota(jnp.int32, (CHUNK, 128), 1)

        def chunk(ci, carry):
            rows = pl.ds(pl.multiple_of(ci * CHUNK, CHUNK), CHUNK)
            q = qk_ref[rows, 0:kw] * scale
            k = qk_ref[rows, kw:2 * kw]
            qk1 = q.astype(BF16).astype(F32) * k.astype(BF16).astype(F32)
            eb, ebl, em, en, ebl_row, qe, ke, qm, kn = _gla_chunk_terms(q, k, lg_ref[rows, :], low01)
            for h in range(heads):
                lanes = slice(128 * (h // 2), 128 * (h // 2) + 128)
                own = (lane < 64) if h % 2 == 0 else (lane >= 64)
                knp = kn[:, lanes].astype(BF16)
                qm_h = jnp.where(own, qm[:, lanes], 0.0).astype(BF16)
                qe_h = jnp.where(own, qe[:, lanes], 0.0).astype(BF16)
                ke_h = jnp.where(own, ke[:, lanes], 0.0).astype(BF16)
                v_h = v_ref[rows, h * dv:(h + 1) * dv]
                st = st_ref[h]
                sall_ref[ci, h] = st
                p = _scores(qm_h, knp, jnp.where(own, qk1[:, lanes], 0.0))
                o_ref[rows, h * dv:(h + 1) * dv] = _dot(p.astype(BF16), v_h) + _dot_nt(qe_h, st.astype(BF16))
                st_ref[h] = st * ebl_row[:, lanes] + _dot_tn(v_h, ke_h)
            return carry

        lax.fori_loop(0, nc, chunk, 0)

    return pl.pallas_call(
        body, name=name, grid=(t // gt,),
        in_specs=[pl.BlockSpec((gt, 2 * kw), lambda i: (i, 0)), pl.BlockSpec((gt, vw), lambda i: (i, 0)),
                  pl.BlockSpec((gt, 128), lambda i: (i, 0)), pl.BlockSpec((128, kw), lambda i: (0, 0)),
                  pl.BlockSpec((1, kw), lambda i: (0, 0))],
        out_shape=[S((t, vw), F32), S((t, kw), F32), S((t // CHUNK, heads, dv, 128), F32)],
        out_specs=[pl.BlockSpec((gt, vw), lambda i: (i, 0)), pl.BlockSpec((gt, kw), lambda i: (i, 0)),
                   pl.BlockSpec((nc, heads, dv, 128), lambda i: (i, 0, 0, 0))],
        scratch_shapes=[pltpu.VMEM((heads, dv, 128), F32)],
        compiler_params=_cp("arbitrary"),
    )(qk, v, gl, wg, bg)


def _gla_bwd(qk, v, lg, do, sall, gl, wg, heads, name):
    t = qk.shape[0]
    kw, vw = qk.shape[1] // 2, v.shape[1]
    dk, dv = kw // heads, vw // heads
    gt = min(ROW_TILE, t)
    nc = gt // CHUNK
    nt = t // gt
    scale = dk ** -0.5

    def body(qk_ref, v_ref, lg_ref, do_ref, sall_ref, gl_ref, wg_ref, dqk_ref, dv_ref, dgl_ref, dwg_ref, dbg_ref, dst_ref, dgk_ref):
        @pl.when(pl.program_id(0) == 0)
        def _():
            dst_ref[...] = jnp.zeros_like(dst_ref)
            dwg_ref[...] = jnp.zeros_like(dwg_ref)
            dbg_ref[...] = jnp.zeros_like(dbg_ref)

        low01 = _tri(True).astype(BF16)
        up01 = _tri(False).astype(BF16)
        causal = _tri(True)
        lane = lax.broadcasted_iota(jnp.int32, (CHUNK, 128), 1)
        last_row = lax.broadcasted_iota(jnp.int32, (CHUNK, kw), 0) == CHUNK - 1

        def chunk(cj, carry):
            ci = nc - 1 - cj
            rows = pl.ds(pl.multiple_of(ci * CHUNK, CHUNK), CHUNK)
            q = qk_ref[rows, 0:kw] * scale
            k = qk_ref[rows, kw:2 * kw]
            qk1 = q.astype(BF16).astype(F32) * k.astype(BF16).astype(F32)
            lgc = lg_ref[rows, :]
            eb, ebl, em, en, ebl_row, qe, ke, qm, kn = _gla_chunk_terms(q, k, lgc, low01)
            dqe, dqm, dkn, dke, drow = [], [], [], [], []
            for pr in range(kw // 128):
                lanes = slice(128 * pr, 128 * pr + 128)
                knp = kn[:, lanes].astype(BF16)
                parts = []
                for half in range(2):
                    h = 2 * pr + half
                    own = (lane < 64) if half == 0 else (lane >= 64)
                    qm_h = jnp.where(own, qm[:, lanes], 0.0).astype(BF16)
                    qe_h = jnp.where(own, qe[:, lanes], 0.0).astype(BF16)
                    ke_h = jnp.where(own, ke[:, lanes], 0.0).astype(BF16)
                    v_h = v_ref[rows, h * dv:(h + 1) * dv]
                    do_h = do_ref[rows, h * dv:(h + 1) * dv]
                    st = sall_ref[ci, h]
                    dst = dst_ref[h]
                    stb, dstb = st.astype(BF16), dst.astype(BF16)
                    p = _scores(qm_h, knp, jnp.where(own, qk1[:, lanes], 0.0)).astype(BF16)
                    dp = jnp.where(causal, _dot_nt(do_h, v_h), 0.0).astype(BF16)
                    dv_ref[rows, h * dv:(h + 1) * dv] = (_dot_tn(p, do_h) + _dot_nt(ke_h, dstb)).astype(BF16)
                    parts.append((jnp.where(own, _dot(dp, knp), 0.0), _dot_tn(dp, qm_h), _dot(do_h, stb), _dot(v_h, dstb),
                                  jnp.sum(st * dst, axis=0, keepdims=True)))
                    dst_ref[h] = dst * ebl_row[:, lanes] + _dot_tn(do_h, qe_h)
                dqm.append(parts[0][0] + parts[1][0])
                dkn.append(parts[0][1] + parts[1][1])
                dqe.append(parts[0][2] + parts[1][2])
                dke.append(parts[0][3] + parts[1][3])
                drow.append(parts[0][4] + parts[1][4])
            dqm, dkn, dqe, dke, drow = [jnp.concatenate(a, axis=1) for a in (dqm, dkn, dqe, dke, drow)]
            dqk_ref[rows, 0:kw] = ((dqe * eb + dqm * em) * scale).astype(BF16)
            dqk_ref[rows, kw:2 * kw] = (dke * ebl + dkn * en).astype(BF16)
            tke = dke * ke
            db = dqe * qe + dqm * qm - dkn * kn - tke
            dbl = jnp.sum(tke, axis=0, keepdims=True) + drow * ebl_row
            db = db + jnp.where(last_row, dbl, 0.0)
            dlg = _dot_01(up01, db)
            dgk_ref[rows, :] = dlg * ((1.0 - jnp.exp(GATE_NORMALIZER * lgc)) / GATE_NORMALIZER)
            return carry

        lax.fori_loop(0, nc, chunk, 0)
        dgk = dgk_ref[...]
        dgkb = dgk.astype(BF16)
        dgl_ref[...] = _dot_nt(dgkb, wg_ref[...]).astype(BF16)
        dwg_ref[...] += _dot_tn(gl_ref[...].astype(BF16), dgkb)
        dbg_ref[...] += _rsum8(dgk)

    rev = lambda i: (nt - 1 - i, 0)
    return pl.pallas_call(
        body, name=name, grid=(nt,),
        in_specs=[pl.BlockSpec((gt, 2 * kw), rev), pl.BlockSpec((gt, vw), rev), pl.BlockSpec((gt, kw), rev),
                  pl.BlockSpec((gt, vw), rev), pl.BlockSpec((nc, heads, dv, 128), lambda i: (nt - 1 - i, 0, 0, 0)),
                  pl.BlockSpec((gt, 128), rev), pl.BlockSpec((128, kw), lambda i: (0, 0))],
        out_shape=[S((t, 2 * kw), BF16), S((t, vw), BF16), S((t, 128), BF16), S((128, kw), F32), S((8, kw), F32)],
        out_specs=[pl.BlockSpec((gt, 2 * kw), rev), pl.BlockSpec((gt, vw), rev), pl.BlockSpec((gt, 128), rev),
                   pl.BlockSpec((128, kw), lambda i: (0, 0)), pl.BlockSpec((8, kw), lambda i: (0, 0))],
        scratch_shapes=[pltpu.VMEM((heads, dv, 128), F32), pltpu.VMEM((gt, kw), F32)],
        compiler_params=_cp("arbitrary"),
    )(qk, v, lg, do, sall, gl, wg)


def _conv_taps(cx_ref, halo_ref, first, cw):
    tm = cx_ref.shape[0]
    u = cx_ref[:, cw:2 * cw].astype(F32) * cx_ref[:, 2 * cw:3 * cw].astype(F32)
    uh = halo_ref[:, cw:2 * cw].astype(F32) * halo_ref[:, 2 * cw:3 * cw].astype(F32)
    uh = jnp.where(first, 0.0, uh)
    row = lax.broadcasted_iota(jnp.int32, (tm, cw), 0)
    u1 = jnp.where(row == 0, uh[15:16, :], pltpu.roll(u, 1, 0))
    u2 = jnp.where(row == 0, uh[14:15, :], jnp.where(row == 1, uh[15:16, :], pltpu.roll(u, 2, 0)))
    return u, u1, u2


def _head_norm(o_h, gn):
    rstd = lax.rsqrt(jnp.mean(o_h * o_h, axis=-1, keepdims=True) + EPS)
    ohat = o_h * rstd
    return ohat, rstd, ohat * gn


def _mix_out(cx, o, go, conv_w, gn, w_out, x, ada9, heads, name):
    t, d = x.shape
    cw, vw = conv_w.shape[1], o.shape[1]
    dv = vw // heads
    tm = min(ROW_TILE, t)

    def body(cx_ref, halo_ref, o_ref, go_ref, cwt_ref, gn_ref, w_ref, x_ref, ada_ref, xo_ref, m_ref, y_ref):
        u, u1, u2 = _conv_taps(cx_ref, halo_ref, pl.program_id(0) == 0, cw)
        yc = cwt_ref[0:1, :] * u2 + cwt_ref[1:2, :] * u1 + cwt_ref[2:3, :] * u
        y_ref[:, 0:cw] = (cx_ref[:, 0:cw].astype(F32) * yc).astype(BF16)
        for h in range(heads):
            cols = slice(h * dv, (h + 1) * dv)
            _, _, on = _head_norm(o_ref[:, cols], gn_ref[...])
            g = go_ref[:, cols].astype(F32)
            y_ref[:, cw + h * dv:cw + (h + 1) * dv] = (on * (g * _sigmoid(g))).astype(BF16)
        m = _dot(y_ref[...], w_ref[...])
        m_ref[...] = m.astype(BF16)
        xo_ref[...] = x_ref[...] + ada_ref[5:6, :] * m

    return pl.pallas_call(
        body, name=name, grid=(t // tm,),
        in_specs=[pl.BlockSpec((tm, 3 * cw), lambda i: (i, 0)),
                  pl.BlockSpec((16, 3 * cw), lambda i: (jnp.maximum(i * (tm // 16) - 1, 0), 0)),
                  pl.BlockSpec((tm, vw), lambda i: (i, 0)), pl.BlockSpec((tm, vw), lambda i: (i, 0)),
                  pl.BlockSpec((3, cw), lambda i: (0, 0)), pl.BlockSpec((1, dv), lambda i: (0, 0)),
                  pl.BlockSpec((cw + vw, d), lambda i: (0, 0)), pl.BlockSpec((tm, d), lambda i: (i, 0)),
                  pl.BlockSpec((N_ADA, d), lambda i: (0, 0))],
        out_shape=[S((t, d), F32), S((t, d), BF16), S((t, cw + vw), BF16)],
        out_specs=[pl.BlockSpec((tm, d), lambda i: (i, 0)), pl.BlockSpec((tm, d), lambda i: (i, 0)),
                   pl.BlockSpec((tm, cw + vw), lambda i: (i, 0))],
        compiler_params=_cp("parallel"),
    )(cx, cx, o, go, conv_w, gn, w_out, x, ada9)


def _mix_bwd_a(dxo, m, ada9, w_out, cx, o, go, conv_w, gn, heads, name):
    t, d = dxo.shape
    cw, vw = conv_w.shape[1], o.shape[1]
    dv = vw // heads
    tm = min(ROW_TILE, t)

    def body(dxo_ref, m_ref, ada_ref, w_ref, cx_ref, halo_ref, o_ref, go_ref, cwt_ref, gn_ref,
             dm_ref, dyc_ref, dcb_ref, do_ref, dgo_ref, dg_ref, dcw_ref, dgn_ref):
        @pl.when(pl.program_id(0) == 0)
        def _():
            dg_ref[...] = jnp.zeros_like(dg_ref)
            dcw_ref[...] = jnp.zeros_like(dcw_ref)
            dgn_ref[...] = jnp.zeros_like(dgn_ref)

        dxo_t = dxo_ref[...]
        dmb = (ada_ref[5:6, :] * dxo_t).astype(BF16)
        dm_ref[...] = dmb
        dg_ref[...] += _rsum8(dxo_t * m_ref[...].astype(F32))
        dy = _dot_nt(dmb, w_ref[...])
        u, u1, u2 = _conv_taps(cx_ref, halo_ref, pl.program_id(0) == 0, cw)
        yc = cwt_ref[0:1, :] * u2 + cwt_ref[1:2, :] * u1 + cwt_ref[2:3, :] * u
        dyv = dy[:, 0:cw]
        dcb_ref[...] = (dyv * yc).astype(BF16)
        dyc = dyv * cx_ref[:, 0:cw].astype(F32)
        dyc_ref[...] = dyc
        dcw_ref[0] += _rsum8(dyc * u2)
        dcw_ref[1] += _rsum8(dyc * u1)
        dcw_ref[2] += _rsum8(dyc * u)
        for h in range(heads):
            cols = slice(h * dv, (h + 1) * dv)
            ohat, rstd, on = _head_norm(o_ref[:, cols], gn_ref[...])
            g = go_ref[:, cols].astype(F32)
            sg = _sigmoid(g)
            dyg = dy[:, cw + h * dv:cw + (h + 1) * dv]
            dgo_ref[:, cols] = (dyg * on * (sg * (1.0 + g * (1.0 - sg)))).astype(BF16)
            don = dyg * (g * sg)
            dgn_ref[...] += _rsum8(don * ohat)
            tt = don * gn_ref[...]
            do_ref[:, cols] = (rstd * (tt - ohat * jnp.mean(tt * ohat, axis=-1, keepdims=True))).astype(BF16)

    return pl.pallas_call(
        body, name=name, grid=(t // tm,),
        in_specs=[pl.BlockSpec((tm, d), lambda i: (i, 0)), pl.BlockSpec((tm, d), lambda i: (i, 0)),
                  pl.BlockSpec((N_ADA, d), lambda i: (0, 0)), pl.BlockSpec((cw + vw, d), lambda i: (0, 0)),
                  pl.BlockSpec((tm, 3 * cw), lambda i: (i, 0)),
                  pl.BlockSpec((16, 3 * cw), lambda i: (jnp.maximum(i * (tm // 16) - 1, 0), 0)),
                  pl.BlockSpec((tm, vw), lambda i: (i, 0)), pl.BlockSpec((tm, vw), lambda i: (i, 0)),
                  pl.BlockSpec((3, cw), lambda i: (0, 0)), pl.BlockSpec((1, dv), lambda i: (0, 0))],
        out_shape=[S((t, d), BF16), S((t, cw), F32), S((t, cw), BF16), S((t, vw), BF16), S((t, vw), BF16),
                   S((8, d), F32), S((3, 8, cw), F32), S((8, dv), F32)],
        out_specs=[pl.BlockSpec((tm, d), lambda i: (i, 0)), pl.BlockSpec((tm, cw), lambda i: (i, 0)),
                   pl.BlockSpec((tm, cw), lambda i: (i, 0)), pl.BlockSpec((tm, vw), lambda i: (i, 0)),
                   pl.BlockSpec((tm, vw), lambda i: (i, 0)), pl.BlockSpec((8, d), lambda i: (0, 0)),
                   pl.BlockSpec((3, 8, cw), lambda i: (0, 0, 0)), pl.BlockSpec((8, dv), lambda i: (0, 0))],
        compiler_params=_cp("arbitrary"),
    )(dxo, m, ada9, w_out, cx, cx, o, go, conv_w, gn)


def _mix_bwd_b(dyc, cx, dcb, dqk, dvv, dgo, dgl, conv_w, w, x, dxo, ada9, nrm, name):
    t, d = x.shape
    cw = conv_w.shape[1]
    n = w.shape[1]
    tm = min(ROW_TILE, t)
    nt = t // tm
    pieces = [dcb.shape[1], cw, cw, dqk.shape[1], dvv.shape[1], dgo.shape[1], dgl.shape[1]]
    assert sum(pieces) == n

    def body(dyc_ref, nxt_ref, cx_ref, dcb_ref, dqk_ref, dv_ref, dgo_ref, dgl_ref, cwt_ref, w_ref, x_ref, dxo_ref, ada_ref, n_ref,
             dx_ref, dp_ref, dsh_ref, dsc_ref, dn_ref):
        i = pl.program_id(0)

        @pl.when(i == 0)
        def _():
            dsh_ref[...] = jnp.zeros_like(dsh_ref)
            dsc_ref[...] = jnp.zeros_like(dsc_ref)
            dn_ref[...] = jnp.zeros_like(dn_ref)

        dyc_t = dyc_ref[...]
        nxt = jnp.where(i == nt - 1, 0.0, nxt_ref[...])
        row = lax.broadcasted_iota(jnp.int32, (tm, cw), 0)
        d1 = jnp.where(row == tm - 1, nxt[0:1, :], pltpu.roll(dyc_t, tm - 1, 0))
        d2 = jnp.where(row == tm - 2, nxt[0:1, :], jnp.where(row == tm - 1, nxt[1:2, :], pltpu.roll(dyc_t, tm - 2, 0)))
        du = cwt_ref[2:3, :] * dyc_t + cwt_ref[1:2, :] * d1 + cwt_ref[0:1, :] * d2
        c0 = 0
        dp_ref[:, c0:c0 + cw] = dcb_ref[...]
        dp_ref[:, cw:2 * cw] = (du * cx_ref[:, 2 * cw:3 * cw].astype(F32)).astype(BF16)
        dp_ref[:, 2 * cw:3 * cw] = (du * cx_ref[:, cw:2 * cw].astype(F32)).astype(BF16)
        c0 = 3 * cw
        for ref in (dqk_ref, dv_ref, dgo_ref, dgl_ref):
            wd = ref.shape[1]
            dp_ref[:, c0:c0 + wd] = ref[...]
            c0 += wd
        dh = _dot_nt(dp_ref[...], w_ref[...])
        dx, tsh, tsc, tn = _normmod_bwd(dh, x_ref[...], n_ref[...], ada_ref[4:5, :])
        dx_ref[...] = dxo_ref[...] + dx
        dsh_ref[...] += _rsum8(tsh)
        dsc_ref[...] += _rsum8(tsc)
        dn_ref[...] += _rsum8(tn)

    row_spec = lambda wd: pl.BlockSpec((tm, wd), lambda i: (i, 0))
    vec = pl.BlockSpec((8, d), lambda i: (0, 0))
    return pl.pallas_call(
        body, name=name, grid=(nt,),
        in_specs=[row_spec(cw), pl.BlockSpec((8, cw), lambda i: (jnp.minimum((i + 1) * (tm // 8), t // 8 - 1), 0)),
                  row_spec(3 * cw), row_spec(cw), row_spec(dqk.shape[1]), row_spec(dvv.shape[1]), row_spec(dgo.shape[1]),
                  row_spec(dgl.shape[1]), pl.BlockSpec((3, cw), lambda i: (0, 0)), pl.BlockSpec((d, n), lambda i: (0, 0)),
                  row_spec(d), row_spec(d), pl.BlockSpec((N_ADA, d), lambda i: (0, 0)), pl.BlockSpec((1, d), lambda i: (0, 0))],
        out_shape=[S((t, d), F32), S((t, n), BF16), S((8, d), F32), S((8, d), F32), S((8, d), F32)],
        out_specs=[row_spec(d), row_spec(n), vec, vec, vec],
        compiler_params=_cp("arbitrary"),
    )(dyc, dyc, cx, dcb, dqk, dvv, dgo, dgl, conv_w, w, x, dxo, ada9, nrm)


def _loss_head(x, target, nrm, name):
    t, d = x.shape
    tm = min(ROW_TILE, t)
    nt = t // tm

    def body(x_ref, tg_ref, n_ref, loss_ref, dx_ref, dn_ref, acc_ref):
        i = pl.program_id(0)

        @pl.when(i == 0)
        def _():
            acc_ref[...] = jnp.zeros_like(acc_ref)
            dn_ref[...] = jnp.zeros_like(dn_ref)

        xt = x_ref[...]
        rstd = lax.rsqrt(jnp.mean(xt * xt, axis=-1, keepdims=True) + EPS)
        xhat = xt * rstd
        err = xhat * n_ref[...] - tg_ref[...]
        acc_ref[...] += _rsum8(err * err)
        dy = err * (1.0 / d)
        dn_ref[...] += _rsum8(dy * xhat)
        dxhat = dy * n_ref[...]
        dx_ref[...] = rstd * (dxhat - xhat * jnp.mean(dxhat * xhat, axis=-1, keepdims=True))

        @pl.when(i == nt - 1)
        def _():
            loss_ref[...] = jnp.full(loss_ref.shape, (0.5 / d) * jnp.sum(acc_ref[...]), F32)

    return pl.pallas_call(
        body, name=name, grid=(nt,),
        in_specs=[pl.BlockSpec((tm, d), lambda i: (i, 0)), pl.BlockSpec((tm, d), lambda i: (i, 0)),
                  pl.BlockSpec((1, d), lambda i: (0, 0))],
        out_shape=[S((1, 128), F32), S((t, d), F32), S((8, d), F32)],
        out_specs=[pl.BlockSpec((1, 128), lambda i: (0, 0)), pl.BlockSpec((tm, d), lambda i: (i, 0)),
                   pl.BlockSpec((8, d), lambda i: (0, 0))],
        scratch_shapes=[pltpu.VMEM((8, d), F32)],
        compiler_params=_cp("arbitrary"),
    )(x, target, nrm)


def _pack_smalls(vec_parts, dcw, dbg, dgn, dwg, rank, name):
    d = vec_parts[0].shape[1]
    cw, kw, dv = dcw.shape[2], dbg.shape[1], dgn.shape[1]
    nv = len(vec_parts)
    assert 2 * cw == d and cw + kw + dv <= d and (rank * kw) % d == 0 and nv + 2 + rank * kw // d <= PACK_ROWS
    per_row = d // kw

    def body(*refs):
        vrefs, (dcw_ref, dbg_ref, dgn_ref, dwg_ref, o_ref) = refs[:nv], refs[nv:]
        o_ref[...] = jnp.zeros_like(o_ref)
        for r, ref in enumerate(vrefs):
            o_ref[r:r + 1, :] = jnp.sum(ref[...], axis=0, keepdims=True)
        o_ref[nv:nv + 1, 0:cw] = jnp.sum(dcw_ref[0], axis=0, keepdims=True)
        o_ref[nv:nv + 1, cw:2 * cw] = jnp.sum(dcw_ref[1], axis=0, keepdims=True)
        o_ref[nv + 1:nv + 2, 0:cw] = jnp.sum(dcw_ref[2], axis=0, keepdims=True)
        o_ref[nv + 1:nv + 2, cw:cw + kw] = jnp.sum(dbg_ref[...], axis=0, keepdims=True)
        o_ref[nv + 1:nv + 2, cw + kw:cw + kw + dv] = jnp.sum(dgn_ref[...], axis=0, keepdims=True)
        for r in range(rank):
            o_ref[nv + 2 + r // per_row:nv + 3 + r // per_row, (r % per_row) * kw:(r % per_row + 1) * kw] = dwg_ref[r:r + 1, :]

    return pl.pallas_call(body, name=name, out_shape=S((PACK_ROWS, d), F32), compiler_params=_cp())(*vec_parts, dcw, dbg, dgn, dwg)


def _sum_slots(a, name):
    def body(a_ref, o_ref):
        acc = a_ref[0]
        for s in range(1, NDEV):
            acc = acc + a_ref[s]
        o_ref[...] = acc

    return pl.pallas_call(body, name=name, out_shape=S(a.shape[1:], F32), compiler_params=_cp())(a)


def _adamw(w, g, m, v):
    m = ADAM_B1 * m + (1.0 - ADAM_B1) * g
    v = ADAM_B2 * v + (1.0 - ADAM_B2) * (g * g)
    m_hat = m / (1.0 - ADAM_B1 ** ADAM_STEP)
    v_hat = v / (1.0 - ADAM_B2 ** ADAM_STEP)
    return -ADAM_LR * (m_hat / (jnp.sqrt(v_hat) + ADAM_EPS) + ADAM_WD * w), m, v


def _adam_slots(recv, w, m, v, name):
    r, c = w.shape
    slots = recv.shape[0]
    tr = _row_tile(r, c)

    def body(recv_ref, w_ref, m_ref, v_ref, g_ref, d_ref, mo_ref, vo_ref):
        g = recv_ref[0].astype(F32)
        for s in range(1, slots):
            g = g + recv_ref[s].astype(F32)
        g_ref[...] = g
        d_ref[...], mo_ref[...], vo_ref[...] = _adamw(w_ref[...], g, m_ref[...], v_ref[...])

    blk = pl.BlockSpec((tr, c), lambda i: (i, 0))
    return pl.pallas_call(
        body, name=name, grid=(r // tr,),
        in_specs=[pl.BlockSpec((slots, tr, c), lambda i: (0, i, 0)), blk, blk, blk],
        out_shape=[S((r, c), F32)] * 4, out_specs=[blk] * 4, compiler_params=_cp("parallel"),
    )(recv, w, m, v)


def _adam_w_ada(act_t, dada, w, m, v, name):
    r, c = w.shape
    tr = 128
    nb = act_t.shape[1]

    def body(a_ref, da_ref, w_ref, m_ref, v_ref, g_ref, d_ref, mo_ref, vo_ref):
        g = a_ref[:, 0:1] * da_ref[0:1, :]
        for b in range(1, nb):
            g = g + a_ref[:, b:b + 1] * da_ref[b:b + 1, :]
        g_ref[...] = g
        d_ref[...], mo_ref[...], vo_ref[...] = _adamw(w_ref[...], g, m_ref[...], v_ref[...])

    blk = pl.BlockSpec((tr, c), lambda i: (i, 0))
    return pl.pallas_call(
        body, name=name, grid=(r // tr,),
        in_specs=[pl.BlockSpec((tr, nb), lambda i: (i, 0)), pl.BlockSpec((nb, c), lambda i: (0, 0)), blk, blk, blk],
        out_shape=[S((r, c), F32)] * 4, out_specs=[blk] * 4, compiler_params=_cp("parallel"),
    )(act_t, dada, w, m, v)


def _adam_smalls(ws, gs, ms, vs, name):
    n = len(ws)

    def body(*refs):
        w_r, g_r, m_r, v_r = (refs[k * n:(k + 1) * n] for k in range(4))
        d_o, m_o, v_o = (refs[(4 + k) * n:(5 + k) * n] for k in range(3))
        for i in range(n):
            d_o[i][...], m_o[i][...], v_o[i][...] = _adamw(w_r[i][...], g_r[i][...], m_r[i][...], v_r[i][...])

    shapes = [S(w.shape, F32) for w in ws]
    outs = pl.pallas_call(body, name=name, out_shape=shapes * 3, compiler_params=_cp())(*ws, *gs, *ms, *vs)
    return outs[:n], outs[n:2 * n], outs[2 * n:]


def kernel(x, c, w_ada, b_ada, norm_ffn1, w_ffn1_in, w_ffn1_out, norm_mix, w_mix_in, conv_w, w_gk2, b_gk, gla_norm, w_mix_out, norm_ffn2, w_ffn2_in, w_ffn2_out, norm_final, loss_target, m_w_ada, m_b_ada, m_norm_ffn1, m_w_ffn1_in, m_w_ffn1_out, m_norm_mix, m_w_mix_in, m_conv_w, m_w_gk2, m_b_gk, m_gla_norm, m_w_mix_out, m_norm_ffn2, m_w_ffn2_in, m_w_ffn2_out, m_norm_final, v_w_ada, v_b_ada, v_norm_ffn1, v_w_ffn1_in, v_w_ffn1_out, v_norm_mix, v_w_mix_in, v_conv_w, v_w_gk2, v_b_gk, v_gla_norm, v_w_mix_out, v_norm_ffn2, v_w_ffn2_in, v_w_ffn2_out, v_norm_final):
    t, d = x.shape[1], x.shape[2]
    x0, tgt = x[0], loss_target[0]
    rank, kw = w_gk2.shape[1], w_gk2.shape[2] * NDEV
    cw = conv_w.shape[2] * NDEV
    dv = gla_norm.shape[1]
    vw = d - cw
    heads = vw // dv
    mix_cols = w_mix_in.shape[2]
    widths = [3 * cw, 2 * kw, vw, vw, 128]
    n_proj = 3 * cw + 2 * kw + 2 * vw + rank
    assert n_proj == mix_cols * NDEV and rank <= 128
    me = 4 * lax.axis_index("x") + 2 * lax.axis_index("y") + lax.axis_index("c")

    bf = lambda a: a[0].astype(BF16)
    (c_all, w1i, w1o, wmi, wmo, w2i, w2o, cwt_all, wg_all) = _gather_two_level(
        [c, bf(w_ffn1_in), bf(w_ffn1_out), bf(w_mix_in), bf(w_mix_out), bf(w_ffn2_in), bf(w_ffn2_out), conv_w[0], w_gk2[0]],
        "gather_weights")
    nb = NDEV // 2
    w1o = w1o.reshape(nb, -1, d)
    w2o = w2o.reshape(nb, -1, d)
    wmo = wmo.reshape(cw + vw, d)
    wmi = jnp.pad(wmi.transpose(1, 0, 2).reshape(d, n_proj), ((0, 0), (0, sum(widths) - n_proj)))
    cwt = cwt_all.transpose(1, 0, 2).reshape(conv_w.shape[1], cw)
    wg = jnp.pad(wg_all.transpose(1, 0, 2).reshape(rank, kw), ((0, 128 - rank), (0, 0))).astype(BF16)

    act_all, p_ada = _ada_partial(c_all.reshape(NDEV, d), w_ada[0])
    (p_all,) = _exchange([p_ada], True, "gather_ada")
    ada9 = _ada_finish(p_all, b_ada).reshape(N_ADA, d)

    h1, gu1, s1 = _ffn_in(x0, ada9, norm_ffn1, w1i, 0, 1, "ffn1_in")
    x1, f1 = _ffn_out(s1, w1o, x0, ada9, 2, 0.5, "ffn1_out")
    h2, cx, qk, vv, go, gl = _mix_in(x1, ada9, norm_mix, wmi, widths, [BF16, F32, BF16, BF16, F32], "mix_in")
    o, lg, sall = _gla_fwd(qk, vv, gl, wg, b_gk, heads, "gla_fwd")
    x2, mm, ycat = _mix_out(cx, o, go, cwt, gla_norm, wmo, x1, ada9, heads, "mix_out")
    h3, gu3, s3 = _ffn_in(x2, ada9, norm_ffn2, w2i, 6, 7, "ffn2_in")
    x3, f3 = _ffn_out(s3, w2o, x2, ada9, 8, 0.5, "ffn2_out")
    loss_v, dx3, dnf = _loss_head(x3, tgt, norm_final.reshape(1, d), "loss_head")

    dx2, dw2i, dw2o, (dsh3, dsc3, dg3, dn3) = _ffn_backward(dx3, x2, h3, gu3, s3, f3, ada9, norm_ffn2, w2i, w2o, (6, 7, 8), "ffn2_bwd")
    dm, dyc, dcb, do, dgo, dg2, dcw, dgn = _mix_bwd_a(dx2, mm, ada9, wmo, cx, o, go, cwt, gla_norm, heads, "mix_bwd_a")
    dqk, dvv, dgl, dwg, dbg = _gla_bwd(qk, vv, lg, do, sall, gl, wg, heads, "gla_bwd")
    dx1, dproj, dsh2, dsc2, dnm = _mix_bwd_b(dyc, cx, dcb, dqk, dvv, dgo, dgl, cwt, wmi, x1, dx2, ada9, norm_mix, "mix_bwd_b")
    n_pad = sum(widths)
    tn = n_pad // 5
    dwmi = _tn_matmul(h2, dproj, lambda tk: (tk, d), lambda sb, k: (k, 0), lambda tk: (tk, tn), lambda sb, k: (k, sb),
                      (d, n_pad), (d, tn), lambda sb, k: (0, sb), 5, "mix_dwin")
    dwmo = _tn_matmul(ycat, dm, lambda tk: (tk, cw + vw), lambda sb, k: (k, 0), lambda tk: (tk, d), lambda sb, k: (k, 0),
                      (cw + vw, d), (cw + vw, d), lambda sb, k: (0, 0), 1, "mix_dwout")
    dx0, dw1i, dw1o, (dsh1, dsc1, dg1, dn1) = _ffn_backward(dx1, x0, h1, gu1, s1, f1, ada9, norm_ffn1, w1i, w1o, (0, 1, 2), "ffn1_bwd")

    dwmi8 = dwmi[:, :n_proj].reshape(d, NDEV, mix_cols).transpose(1, 0, 2)
    pack = _pack_smalls([dn1, dnm, dn3, dnf, dsh1, dsc1, dg1, dsh2, dsc2, dg2, dsh3, dsc3, dg3], dcw, dbg, dgn, dwg, rank, "pack_smalls")
    grads = [dw1i, dw1o.reshape(NDEV, -1, d), dwmi8, dwmo.reshape(NDEV, -1, d), dw2i, dw2o.reshape(NDEV, -1, d)]
    from_sibling = _swap_with_sibling(grads, "grads_d2d")
    core = lax.axis_index("c").astype(jnp.int32).reshape(1)
    chip_sums = [_pair_add(g, r, core, "grads_pair_add_%d" % i) for i, (g, r) in enumerate(zip(grads, from_sibling))]
    r1i, r1o, rmi, rmo, r2i, r2o = _swap_between_chips(chip_sums, "grads_ici")
    (pack_all,) = _exchange([pack], True, "gather_smalls")
    tot = _sum_slots(pack_all, "sum_smalls")

    res = {}
    for nm, recv, w, m, v in (("w_ffn1_in", r1i, w_ffn1_in, m_w_ffn1_in, v_w_ffn1_in), ("w_ffn1_out", r1o, w_ffn1_out, m_w_ffn1_out, v_w_ffn1_out),
                              ("w_mix_in", rmi, w_mix_in, m_w_mix_in, v_w_mix_in), ("w_mix_out", rmo, w_mix_out, m_w_mix_out, v_w_mix_out),
                              ("w_ffn2_in", r2i, w_ffn2_in, m_w_ffn2_in, v_w_ffn2_in), ("w_ffn2_out", r2o, w_ffn2_out, m_w_ffn2_out, v_w_ffn2_out)):
        res[nm] = [a[None] for a in _adam_slots(recv, w[0], m[0], v[0], "adam_" + nm)]

    cols_ada = w_ada.shape[2]
    dada_all = pack_all[:, 4:4 + N_ADA, :].reshape(NDEV, N_ADA * d)
    dada_mine = lax.dynamic_slice_in_dim(dada_all, me * cols_ada, cols_ada, axis=1)
    res["w_ada"] = [a[None] for a in _adam_w_ada(act_all.T, dada_mine, w_ada[0], m_w_ada[0], v_w_ada[0], "adam_w_ada")]

    nv = 4 + N_ADA
    g_small = {
        "b_ada": tot[4:nv].reshape(1, N_ADA * d),
        "norm_ffn1": tot[0:1], "norm_mix": tot[1:2], "norm_ffn2": tot[2:3], "norm_final": tot[3:4],
        "conv_w": lax.dynamic_slice_in_dim(
            jnp.concatenate([tot[nv:nv + 1, 0:cw], tot[nv:nv + 1, cw:2 * cw], tot[nv + 1:nv + 2, 0:cw]], axis=0), me * (cw // NDEV), cw // NDEV, axis=1),
        "w_gk2": lax.dynamic_slice_in_dim(tot[nv + 2:nv + 2 + rank * kw // d].reshape(rank, kw), me * (kw // NDEV), kw // NDEV, axis=1),
        "b_gk": tot[nv + 1:nv + 2, cw:cw + kw],
        "gla_norm": tot[nv + 1:nv + 2, cw + kw:cw + kw + dv],
    }
    small = {"b_ada": (b_ada, m_b_ada, v_b_ada), "norm_ffn1": (norm_ffn1, m_norm_ffn1, v_norm_ffn1), "norm_mix": (norm_mix, m_norm_mix, v_norm_mix),
             "norm_ffn2": (norm_ffn2, m_norm_ffn2, v_norm_ffn2), "norm_final": (norm_final, m_norm_final, v_norm_final),
             "conv_w": (conv_w, m_conv_w, v_conv_w), "w_gk2": (w_gk2, m_w_gk2, v_w_gk2), "b_gk": (b_gk, m_b_gk, v_b_gk),
             "gla_norm": (gla_norm, m_gla_norm, v_gla_norm)}
    names = list(small)
    flat = lambda a: a.reshape(-1, a.shape[-1])
    dl, mo, vo = _adam_smalls([flat(small[n][0]) for n in names], [g_small[n] for n in names],
                              [flat(small[n][1]) for n in names], [flat(small[n][2]) for n in names], "adam_smalls")
    for i, n in enumerate(names):
        shp = small[n][0].shape
        res[n] = [g_small[n].reshape(shp), dl[i].reshape(shp), mo[i].reshape(shp), vo[i].reshape(shp)]

    loss = lax.psum(loss_v[0, 0], ("x", "y", "c"))
    order = ["w_ada", "b_ada", "norm_ffn1", "w_ffn1_in", "w_ffn1_out", "norm_mix", "w_mix_in", "conv_w", "w_gk2", "b_gk", "gla_norm",
             "w_mix_out", "norm_ffn2", "w_ffn2_in", "w_ffn2_out", "norm_final"]
    return (loss, dx0[None], *[res[n][0] for n in order], *[res[n][1] for n in order], *[res[n][2] for n in order], *[res[n][3] for n in order])
```

```python
import collections
import functools

import jax
import jax.numpy as jnp
from jax import lax
from jax.experimental import pallas as pl
from jax.experimental.pallas import tpu as pltpu

F32 = jnp.float32
BF16 = jnp.bfloat16
S = jax.ShapeDtypeStruct

NDEV = 8
EPS = 1e-6
GATE_NORMALIZER = 16.0
CHUNK = 64
N_ADA = 9
ADAM_LR, ADAM_B1, ADAM_B2, ADAM_EPS, ADAM_WD, ADAM_STEP = 0.001, 0.9, 0.999, 1e-08, 0.01, 10
V7X_VMEM_LIMIT = 56 * 1024 * 1024
ROW_TILE = 512
K_TILE = 1024
PACK_ROWS = 24
ANY = pl.BlockSpec(memory_space=pl.ANY)


def _cp(*sem):
    return pltpu.CompilerParams(dimension_semantics=sem or None, vmem_limit_bytes=V7X_VMEM_LIMIT)


def _dot(a, b):
    return jnp.dot(a, b, preferred_element_type=F32)


def _dot_nt(a, b):
    return lax.dot_general(a, b, (((1,), (1,)), ((), ())), preferred_element_type=F32)


def _dot_tn(a, b):
    return lax.dot_general(a, b, (((0,), (0,)), ((), ())), preferred_element_type=F32)


def _rsum8(a):
    r, c = a.shape
    return jnp.sum(a.reshape(r // 8, 8, c), axis=0)


def _row_tile(r, c):
    for cand in (256, 128, 176, 88, 64, 32, 16, 8):
        if r % cand == 0 and cand * c * 4 <= 1024 * 1024:
            return cand
    return r


def _sigmoid(x):
    return 1.0 / (1.0 + jnp.exp(-x))


def _normmod(x, nrm, sc, sh):
    rstd = lax.rsqrt(jnp.mean(x * x, axis=-1, keepdims=True) + EPS)
    xhat = x * rstd
    return xhat, rstd, (xhat * nrm) * (1.0 + sc) + sh


def _normmod_bwd(dh, x, nrm, sc):
    rstd = lax.rsqrt(jnp.mean(x * x, axis=-1, keepdims=True) + EPS)
    xhat = x * rstd
    dxhat = dh * (nrm * (1.0 + sc))
    dx = rstd * (dxhat - xhat * jnp.mean(dxhat * xhat, axis=-1, keepdims=True))
    return dx, dh, dh * (xhat * nrm), dh * ((1.0 + sc) * xhat)


def _place():
    x, y, c = lax.axis_index("x"), lax.axis_index("y"), lax.axis_index("c")
    return x, y, c, 4 * x + 2 * y + c


def _peer(x, y, c, k):
    px = 1 - x if k & 4 else x
    py = 1 - y if k & 2 else y
    pc = 1 - c if k & 1 else c
    return (px, py, pc), 4 * px + 2 * py + pc


def _remote(src, dst, send_sem, recv_sem, peer):
    return pltpu.make_async_remote_copy(src_ref=src, dst_ref=dst, send_sem=send_sem, recv_sem=recv_sem,
                                        device_id=peer, device_id_type=pl.DeviceIdType.MESH)


_Plan = collections.namedtuple("_Plan", "inputs out_shapes sem_shapes start finish")


def _plan_all_to_all(xs, gather):
    n = len(xs)

    def copies(ins, outs, sems, landed):
        send_sems, recv_sems, local_sems = sems
        x, y, c, me = _place()
        local = [pltpu.make_async_copy(ins[i] if gather else ins[i].at[me], outs[i].at[me], local_sems.at[i]) for i in range(n)]
        remote = []
        for k in range(1, NDEV):
            peer, pid = _peer(x, y, c, k)
            for i in range(n):
                remote.append(_remote(ins[i] if gather else ins[i].at[pid], outs[i].at[pid if landed else me],
                                      send_sems.at[i, k - 1], recv_sems.at[i, k - 1], peer))
        return local, remote

    def start(ins, outs, sems):
        local, remote = copies(ins, outs, sems, False)
        for cp in local + remote:
            cp.start()

    def finish(ins, outs, sems):
        local, remote = copies(ins, outs, sems, True)
        for cp in remote + local:
            cp.wait()

    return _Plan(list(xs), [S((NDEV,) + a.shape, a.dtype) if gather else S(a.shape, a.dtype) for a in xs],
                 [pltpu.SemaphoreType.DMA((n, NDEV - 1)), pltpu.SemaphoreType.DMA((n, NDEV - 1)), pltpu.SemaphoreType.DMA((n,))],
                 start, finish)


def _other_chips(x, y):
    return [(1 - x, y), (x, 1 - y), (1 - x, 1 - y)]


def _plan_gather(xs):
    n = len(xs)

    def copies(ins, outs, sems):
        send_sems, recv_sems, local_sems = sems
        x, y, c, me = _place()
        sib, sib_id = (x, y, 1 - c), 4 * x + 2 * y + 1 - c
        chips = _other_chips(x, y)
        local = [pltpu.make_async_copy(ins[i], outs[i].at[me], local_sems.at[i]) for i in range(n)]
        first, arrive, forward, from_sibling = [], [], [], []
        for i in range(n):
            first.append(_remote(ins[i], outs[i].at[me], send_sems.at[i, 0], recv_sems.at[i, 0], sib))
            from_sibling.append(_remote(ins[i], outs[i].at[sib_id], send_sems.at[i, 0], recv_sems.at[i, 0], sib))
        for j, (px, py) in enumerate(chips):
            slot = 4 * px + 2 * py
            arrive.append([])
            forward.append([])
            for i in range(n):
                first.append(_remote(ins[i], outs[i].at[me], send_sems.at[i, 1 + j], recv_sems.at[i, 1 + j], (px, py, c)))
                arrive[j].append(_remote(ins[i], outs[i].at[slot + c], send_sems.at[i, 1 + j], recv_sems.at[i, 1 + j], (px, py, c)))
                forward[j].append(_remote(outs[i].at[slot + c], outs[i].at[slot + c], send_sems.at[i, 4 + j], recv_sems.at[i, 4 + j], sib))
                from_sibling.append(_remote(ins[i], outs[i].at[slot + 1 - c], send_sems.at[i, 4 + j], recv_sems.at[i, 4 + j], sib))
        return local, first, arrive, forward, from_sibling

    def start(ins, outs, sems):
        local, first, _, _, _ = copies(ins, outs, sems)
        for cp in local + first:
            cp.start()

    def finish(ins, outs, sems):
        local, first, arrive, forward, from_sibling = copies(ins, outs, sems)
        for landed, onward in zip(arrive, forward):
            for cp in landed:
                cp.wait_recv()
            for cp in onward:
                cp.start()
        for cp in from_sibling:
            cp.wait_recv()
        for cp in first + [cp for onward in forward for cp in onward]:
            cp.wait_send()
        for cp in local:
            cp.wait()

    return _Plan(list(xs), [S((NDEV,) + a.shape, a.dtype) for a in xs],
                 [pltpu.SemaphoreType.DMA((n, NDEV - 1)), pltpu.SemaphoreType.DMA((n, NDEV - 1)), pltpu.SemaphoreType.DMA((n,))],
                 start, finish)


def _plan_sibling_swap(gs):
    n = len(gs)

    def copies(ins, outs, sems):
        send_sems, recv_sems = sems
        x, y, c, _ = _place()
        return [_remote(ins[i].at[2 * j + 1 - c], outs[i].at[j], send_sems.at[i, j], recv_sems.at[i, j], (x, y, 1 - c))
                for i in range(n) for j in range(NDEV // 2)]

    def start(ins, outs, sems):
        for cp in copies(ins, outs, sems):
            cp.start()

    def finish(ins, outs, sems):
        for cp in copies(ins, outs, sems):
            cp.wait()

    return _Plan(list(gs), [S((NDEV // 2,) + a.shape[1:], a.dtype) for a in gs],
                 [pltpu.SemaphoreType.DMA((n, NDEV // 2)), pltpu.SemaphoreType.DMA((n, NDEV // 2))], start, finish)


def _pair_add(g, r1, core, name):
    _, r, c = g.shape
    tr = _row_tile(r, c)

    def body(core_ref, g_ref, r_ref, o_ref):
        o_ref[...] = (g_ref[...].astype(F32) + r_ref[...].astype(F32)).astype(BF16)

    return pl.pallas_call(
        body, name=name,
        grid_spec=pltpu.PrefetchScalarGridSpec(
            num_scalar_prefetch=1, grid=(NDEV // 2, r // tr),
            in_specs=[pl.BlockSpec((None, tr, c), lambda j, k, core_ref: (2 * j + core_ref[0], k, 0)),
                      pl.BlockSpec((None, tr, c), lambda j, k, core_ref: (j, k, 0))],
            out_specs=pl.BlockSpec((None, tr, c), lambda j, k, core_ref: (j, k, 0))),
        out_shape=S((NDEV // 2, r, c), BF16), compiler_params=_cp("parallel", "parallel"),
    )(core, g, r1)


def _plan_chip_swap(ps):
    n = len(ps)

    def copies(ins, outs, sems, landed):
        send_sems, recv_sems, local_sems = sems
        x, y, c, _ = _place()
        mine = 2 * x + y
        local = [pltpu.make_async_copy(ins[i].at[mine], outs[i].at[mine], local_sems.at[i]) for i in range(n)]
        remote = [_remote(ins[i].at[2 * px + py], outs[i].at[2 * px + py if landed else mine], send_sems.at[i, j], recv_sems.at[i, j], (px, py, c))
                  for j, (px, py) in enumerate(_other_chips(x, y)) for i in range(n)]
        return local, remote

    def start(ins, outs, sems):
        local, remote = copies(ins, outs, sems, False)
        for cp in local + remote:
            cp.start()

    def finish(ins, outs, sems):
        local, remote = copies(ins, outs, sems, True)
        for cp in remote + local:
            cp.wait()

    return _Plan(list(ps), [S(a.shape, a.dtype) for a in ps],
                 [pltpu.SemaphoreType.DMA((n, 3)), pltpu.SemaphoreType.DMA((n, 3)), pltpu.SemaphoreType.DMA((n,))], start, finish)


def _pcall(body, name, args, in_specs, out_shape, out_specs, grid=(), scratch_shapes=(), sem=(), plans=()):
    n_in, n_out, n_scr = len(args), len(out_shape), len(scratch_shapes)
    counts = [(len(p.inputs), len(p.out_shapes), len(p.sem_shapes)) for p in plans]
    c_args = [a for p in plans for a in p.inputs]
    c_outs = [s for p in plans for s in p.out_shapes]
    c_sems = [s for p in plans for s in p.sem_shapes]

    def wrapped(*refs):
        cuts = [n_in, len(c_args), n_out, len(c_outs), n_scr, len(c_sems)]
        ins, c_in, outs, c_out, scr, c_sem = [refs[sum(cuts[:k]):sum(cuts[:k + 1])] for k in range(6)]

        def halves(which):
            a = b = s = 0
            for p, (na, nb, ns) in zip(plans, counts):
                getattr(p, which)(c_in[a:a + na], c_out[b:b + nb], c_sem[s:s + ns])
                a, b, s = a + na, b + nb, s + ns

        if not plans:
            body(*ins, *outs, *scr)
        elif not grid:
            halves("start")
            body(*ins, *outs, *scr)
            halves("finish")
        else:
            first = functools.reduce(jnp.logical_and, [pl.program_id(a) == 0 for a in range(len(grid))])
            last = functools.reduce(jnp.logical_and, [pl.program_id(a) == grid[a] - 1 for a in range(len(grid))])
            pl.when(first)(lambda: halves("start"))
            body(*ins, *outs, *scr)
            pl.when(last)(lambda: halves("finish"))

    res = pl.pallas_call(
        wrapped, name=name, grid=grid, in_specs=list(in_specs) + [ANY] * len(c_args),
        out_shape=list(out_shape) + c_outs, out_specs=list(out_specs) + [ANY] * len(c_outs),
        scratch_shapes=list(scratch_shapes) + c_sems,
        compiler_params=_cp(*(("arbitrary",) * len(grid) if plans else sem)),
    )(*args, *c_args)
    c_res, b = [], n_out
    for _, nb, _ in counts:
        c_res.append(res[b:b + nb])
        b += nb
    return res[:n_out], c_res


def _exchange(plans, name):
    return _pcall(lambda: None, name, [], [], [], [], plans=plans)[1]


def _ada_partial(c_all, w_ada):
    nb, d = c_all.shape
    cols = w_ada.shape[1]

    def body(c_ref, w_ref, act_ref, p_ref):
        cc = c_ref[...]
        act = cc * _sigmoid(cc)
        act_ref[...] = act
        p_ref[...] = _dot(act.astype(BF16), w_ref[...].astype(BF16))

    return pl.pallas_call(body, name="ada_partial", out_shape=[S((nb, d), F32), S((nb, cols), F32)],
                          compiler_params=_cp())(c_all, w_ada)


def _ada_finish(p_all, b_ada):
    _, nb, cols = p_all.shape

    def body(p_ref, b_ref, o_ref):
        me = _place()[3]
        for s in range(NDEV):
            o_ref[:, s * cols:(s + 1) * cols] = p_ref[s, pl.ds(me, 1), :] + b_ref[:, s * cols:(s + 1) * cols]

    return pl.pallas_call(body, name="ada_finish", out_shape=S((1, NDEV * cols), F32), compiler_params=_cp())(p_all, b_ada)


def _ffn_in(x, ada9, nrm, w_in, sh_row, sc_row, name, plans=()):
    t, d = x.shape
    nb, bw = w_in.shape[0] // 2, w_in.shape[2]
    tm = min(ROW_TILE, t)

    def body(x_ref, ada_ref, n_ref, wg_ref, wu_ref, h_ref, gu_ref, s_ref):
        @pl.when(pl.program_id(1) == 0)
        def _():
            _, _, h = _normmod(x_ref[...], n_ref[...], ada_ref[sc_row:sc_row + 1, :], ada_ref[sh_row:sh_row + 1, :])
            h_ref[...] = h.astype(BF16)

        h = h_ref[...]
        g = _dot(h, wg_ref[...])
        u = _dot(h, wu_ref[...])
        gu_ref[0] = g.astype(BF16)
        gu_ref[1] = u.astype(BF16)
        s_ref[...] = (g * _sigmoid(g) * u).astype(BF16)

    return _pcall(
        body, name, [x, ada9, nrm, w_in, w_in], grid=(t // tm, nb),
        in_specs=[pl.BlockSpec((tm, d), lambda i, j: (i, 0)), pl.BlockSpec((N_ADA, d), lambda i, j: (0, 0)),
                  pl.BlockSpec((1, d), lambda i, j: (0, 0)),
                  pl.BlockSpec((None, d, bw), lambda i, j: (j, 0, 0)), pl.BlockSpec((None, d, bw), lambda i, j: (j + nb, 0, 0))],
        out_shape=[S((t, d), BF16), S((2, nb, t, bw), BF16), S((nb, t, bw), BF16)],
        out_specs=[pl.BlockSpec((tm, d), lambda i, j: (i, 0)), pl.BlockSpec((2, None, tm, bw), lambda i, j: (0, j, i, 0)),
                   pl.BlockSpec((None, tm, bw), lambda i, j: (j, i, 0))],
        sem=("parallel", "arbitrary"), plans=plans)


def _ffn_out(s, w_out, x, ada9, g_row, res_scale, name, plans=()):
    nb, t, bw = s.shape
    d = x.shape[1]
    tm = min(ROW_TILE, t)

    def body(s_ref, w_ref, x_ref, ada_ref, xo_ref, f_ref):
        acc = _dot(s_ref[0], w_ref[0])
        for b in range(1, nb):
            acc = acc + _dot(s_ref[b], w_ref[b])
        f_ref[...] = acc.astype(BF16)
        xo_ref[...] = x_ref[...] + (res_scale * ada_ref[g_row:g_row + 1, :]) * acc

    return _pcall(
        body, name, [s, w_out, x, ada9], grid=(t // tm,),
        in_specs=[pl.BlockSpec((nb, tm, bw), lambda i: (0, i, 0)), pl.BlockSpec((nb, bw, d), lambda i: (0, 0, 0)),
                  pl.BlockSpec((tm, d), lambda i: (i, 0)), pl.BlockSpec((N_ADA, d), lambda i: (0, 0))],
        out_shape=[S((t, d), F32), S((t, d), BF16)],
        out_specs=[pl.BlockSpec((tm, d), lambda i: (i, 0)), pl.BlockSpec((tm, d), lambda i: (i, 0))],
        sem=("parallel",), plans=plans)


def _ffn_bwd_ds(dxo, f, ada9, w_out, gu, g_row, res_scale, name, plans=()):
    t, d = dxo.shape
    nb, bw = w_out.shape[0], w_out.shape[1]
    tm = min(ROW_TILE, t)

    def body(dxo_ref, f_ref, ada_ref, w_ref, gu_ref, df_ref, da_ref, dg_ref):
        i, j = pl.program_id(0), pl.program_id(1)

        @pl.when((i == 0) & (j == 0))
        def _():
            dg_ref[...] = jnp.zeros_like(dg_ref)

        @pl.when(j == 0)
        def _():
            dxo_t = dxo_ref[...]
            df_ref[...] = ((res_scale * ada_ref[g_row:g_row + 1, :]) * dxo_t).astype(BF16)
            dg_ref[...] += res_scale * _rsum8(dxo_t * f_ref[...].astype(F32))

        ds = _dot_nt(df_ref[...], w_ref[...])
        g = gu_ref[0].astype(F32)
        u = gu_ref[1].astype(F32)
        sg = _sigmoid(g)
        da_ref[0] = (ds * u * (sg * (1.0 + g * (1.0 - sg)))).astype(BF16)
        da_ref[1] = (ds * (g * sg)).astype(BF16)

    return _pcall(
        body, name, [dxo, f, ada9, w_out, gu], grid=(t // tm, nb),
        in_specs=[pl.BlockSpec((tm, d), lambda i, j: (i, 0)), pl.BlockSpec((tm, d), lambda i, j: (i, 0)),
                  pl.BlockSpec((N_ADA, d), lambda i, j: (0, 0)), pl.BlockSpec((None, bw, d), lambda i, j: (j, 0, 0)),
                  pl.BlockSpec((2, None, tm, bw), lambda i, j: (0, j, i, 0))],
        out_shape=[S((t, d), BF16), S((2, nb, t, bw), BF16), S((8, d), F32)],
        out_specs=[pl.BlockSpec((tm, d), lambda i, j: (i, 0)), pl.BlockSpec((2, None, tm, bw), lambda i, j: (0, j, i, 0)),
                   pl.BlockSpec((8, d), lambda i, j: (0, 0))],
        sem=("arbitrary", "arbitrary"), plans=plans)


def _ffn_bwd_dh(da, w_in, x, dxo, ada9, nrm, sh_row, sc_row, name, plans=()):
    t, d = x.shape
    nb, bw = w_in.shape[0] // 2, w_in.shape[2]
    tm = min(ROW_TILE, t)

    def body(da_ref, wg_ref, wu_ref, x_ref, dxo_ref, ada_ref, n_ref, dx_ref, dsh_ref, dsc_ref, dn_ref, acc_ref):
        i, j = pl.program_id(0), pl.program_id(1)

        @pl.when((i == 0) & (j == 0))
        def _():
            dsh_ref[...] = jnp.zeros_like(dsh_ref)
            dsc_ref[...] = jnp.zeros_like(dsc_ref)
            dn_ref[...] = jnp.zeros_like(dn_ref)

        part = _dot_nt(da_ref[0], wg_ref[...]) + _dot_nt(da_ref[1], wu_ref[...])

        @pl.when(j == 0)
        def _():
            acc_ref[...] = part

        @pl.when(j > 0)
        def _():
            acc_ref[...] += part

        @pl.when(j == nb - 1)
        def _():
            dx, tsh, tsc, tn = _normmod_bwd(acc_ref[...], x_ref[...], n_ref[...], ada_ref[sc_row:sc_row + 1, :])
            dx_ref[...] = dxo_ref[...] + dx
            dsh_ref[...] += _rsum8(tsh)
            dsc_ref[...] += _rsum8(tsc)
            dn_ref[...] += _rsum8(tn)

    vec = pl.BlockSpec((8, d), lambda i, j: (0, 0))
    return _pcall(
        body, name, [da, w_in, w_in, x, dxo, ada9, nrm], grid=(t // tm, nb),
        in_specs=[pl.BlockSpec((2, None, tm, bw), lambda i, j: (0, j, i, 0)),
                  pl.BlockSpec((None, d, bw), lambda i, j: (j, 0, 0)), pl.BlockSpec((None, d, bw), lambda i, j: (j + nb, 0, 0)),
                  pl.BlockSpec((tm, d), lambda i, j: (i, 0)), pl.BlockSpec((tm, d), lambda i, j: (i, 0)),
                  pl.BlockSpec((N_ADA, d), lambda i, j: (0, 0)), pl.BlockSpec((1, d), lambda i, j: (0, 0))],
        out_shape=[S((t, d), F32), S((8, d), F32), S((8, d), F32), S((8, d), F32)],
        out_specs=[pl.BlockSpec((tm, d), lambda i, j: (i, 0)), vec, vec, vec],
        scratch_shapes=[pltpu.VMEM((tm, d), F32)],
        sem=("arbitrary", "arbitrary"), plans=plans)


def _tn_matmul(a, b, a_block, a_map, b_block, b_map, out_shape, out_block, out_map, nblk, name, plans=()):
    t = a.shape[-2]
    tk = min(K_TILE, t)
    nk = t // tk

    def body(a_ref, b_ref, o_ref, acc_ref):
        part = _dot_tn(a_ref[...], b_ref[...])

        @pl.when(pl.program_id(1) == 0)
        def _():
            acc_ref[...] = part

        @pl.when(pl.program_id(1) > 0)
        def _():
            acc_ref[...] += part

        @pl.when(pl.program_id(1) == nk - 1)
        def _():
            o_ref[...] = acc_ref[...].astype(BF16)

    (out,), moved = _pcall(
        body, name, [a, b], grid=(nblk, nk),
        in_specs=[pl.BlockSpec(a_block(tk), a_map), pl.BlockSpec(b_block(tk), b_map)],
        out_shape=[S(out_shape, BF16)], out_specs=[pl.BlockSpec(out_block, out_map)],
        scratch_shapes=[pltpu.VMEM(tuple(n for n in out_block if n is not None), F32)],
        sem=("parallel", "arbitrary"), plans=plans)
    return out, moved


def _ffn_backward(dxo, x_in, h, gu, s, f, ada9, nrm, w_in, w_out, rows, core, name, ds_plans=(), dwin_plans=()):
    sh_row, sc_row, g_row = rows
    nb, t, bw = s.shape
    d = x_in.shape[1]
    (df, da, dg), ds_moved = _ffn_bwd_ds(dxo, f, ada9, w_out, gu, g_row, 0.5, name + "_ds", plans=ds_plans)
    dw_in, dwin_moved = _tn_matmul(
        h, da.reshape(2 * nb, t, bw), lambda tk: (tk, d), lambda sb, k: (k, 0), lambda tk: (None, tk, bw), lambda sb, k: (sb, k, 0),
        (2 * nb, d, bw), (None, d, bw), lambda sb, k: (sb, 0, 0), 2 * nb, name + "_dwin",
        plans=dwin_plans(ds_moved) if callable(dwin_plans) else dwin_plans)
    dw_out, ((half_in,),) = _tn_matmul(
        s, df, lambda tk: (None, tk, bw), lambda sb, k: (sb, k, 0), lambda tk: (tk, d), lambda sb, k: (k, 0),
        (nb, bw, d), (None, bw, d), lambda sb, k: (sb, 0, 0), nb, name + "_dwout", plans=[_plan_sibling_swap([dw_in])])
    dw_out = dw_out.reshape(NDEV, -1, d)
    sum_in = _pair_add(dw_in, half_in, core, name + "_dwin_add")
    (dx, dsh, dsc, dn), ((recv_in,), (half_out,)) = _ffn_bwd_dh(
        da, w_in, x_in, dxo, ada9, nrm, sh_row, sc_row, name + "_dh", plans=[_plan_chip_swap([sum_in]), _plan_sibling_swap([dw_out])])
    sum_out = _pair_add(dw_out, half_out, core, name + "_dwout_add")
    return dx, recv_in, sum_out, (dsh, dsc, dg, dn), ds_moved, dwin_moved


def _mix_in(x, ada9, nrm, w, widths, dts, name, plans=()):
    t, d = x.shape
    n = w.shape[1]
    tm = min(ROW_TILE, t)
    starts = [sum(widths[:i]) for i in range(len(widths))]

    def body(x_ref, ada_ref, n_ref, w_ref, h_ref, *out_refs):
        _, _, h = _normmod(x_ref[...], n_ref[...], ada_ref[4:5, :], ada_ref[3:4, :])
        hb = h.astype(BF16)
        h_ref[...] = hb
        for o_ref, st, wd in zip(out_refs, starts, widths):
            o_ref[...] = _dot(hb, w_ref[:, st:st + wd]).astype(o_ref.dtype)

    return _pcall(
        body, name, [x, ada9, nrm, w], grid=(t // tm,),
        in_specs=[pl.BlockSpec((tm, d), lambda i: (i, 0)), pl.BlockSpec((N_ADA, d), lambda i: (0, 0)),
                  pl.BlockSpec((1, d), lambda i: (0, 0)), pl.BlockSpec((d, n), lambda i: (0, 0))],
        out_shape=[S((t, d), BF16)] + [S((t, wd), dt) for wd, dt in zip(widths, dts)],
        out_specs=[pl.BlockSpec((tm, d), lambda i: (i, 0))] + [pl.BlockSpec((tm, wd), lambda i: (i, 0)) for wd in widths],
        sem=("parallel",), plans=plans)


def _tri(lower):
    r = lax.broadcasted_iota(jnp.int32, (CHUNK, CHUNK), 0)
    c = lax.broadcasted_iota(jnp.int32, (CHUNK, CHUNK), 1)
    return (r >= c) if lower else (c >= r)


def _dot_01(m, x):
    hi = x.astype(BF16)
    r1 = x - hi.astype(F32)
    mid = r1.astype(BF16)
    lo = (r1 - mid.astype(F32)).astype(BF16)
    return _dot(m, hi) + _dot(m, mid) + _dot(m, lo)


def _gla_chunk_terms(q, k, lg, low01):
    b = _dot_01(low01, lg)
    bl = b[CHUNK - 1:CHUNK, :]
    r = 0.5 * bl
    eb, ebl, em, en = jnp.exp(b), jnp.exp(bl - b), jnp.exp(b - r), jnp.exp(r - b)
    return eb, ebl, em, en, jnp.exp(bl), q * eb, k * ebl, q * em, k * en


def _scores(qm_h, knp, qk1_h):
    r = lax.broadcasted_iota(jnp.int32, (CHUNK, CHUNK), 0)
    c = lax.broadcasted_iota(jnp.int32, (CHUNK, CHUNK), 1)
    p = jnp.where(r > c, _dot_nt(qm_h, knp), 0.0)
    return jnp.where(r == c, jnp.sum(qk1_h, axis=1, keepdims=True), p)


def _gla_fwd(qk, v, gl, wg, bg, heads, name, plans=()):
    t = qk.shape[0]
    kw, vw = qk.shape[1] // 2, v.shape[1]
    dk, dv = kw // heads, vw // heads
    assert dk == 64 and dv == 128 and kw % 128 == 0
    gt = min(ROW_TILE, t)
    nc = gt // CHUNK
    scale = dk ** -0.5

    def body(qk_ref, v_ref, gl_ref, wg_ref, bg_ref, o_ref, lg_ref, sall_ref, st_ref):
        @pl.when(pl.program_id(0) == 0)
        def _():
            st_ref[...] = jnp.zeros_like(st_ref)

        gk = _dot(gl_ref[...].astype(BF16), wg_ref[...]) + bg_ref[...]
        lg_ref[...] = (jnp.minimum(gk, 0.0) - jnp.log(1.0 + jnp.exp(-jnp.abs(gk)))) / GATE_NORMALIZER
        low01 = _tri(True).astype(BF16)
        lane = lax.broadcasted_iota(jnp.int32, (CHUNK, 128), 1)

        def chunk(ci, carry):
            rows = pl.ds(pl.multiple_of(ci * CHUNK, CHUNK), CHUNK)
            q = qk_ref[rows, 0:kw] * scale
            k = qk_ref[rows, kw:2 * kw]
            qk1 = q.astype(BF16).astype(F32) * k.astype(BF16).astype(F32)
            eb, ebl, em, en, ebl_row, qe, ke, qm, kn = _gla_chunk_terms(q, k, lg_ref[rows, :], low01)
            for h in range(heads):
                lanes = slice(128 * (h // 2), 128 * (h // 2) + 128)
                own = (lane < 64) if h % 2 == 0 else (lane >= 64)
                knp = kn[:, lanes].astype(BF16)
                qm_h = jnp.where(own, qm[:, lanes], 0.0).astype(BF16)
                qe_h = jnp.where(own, qe[:, lanes], 0.0).astype(BF16)
                ke_h = jnp.where(own, ke[:, lanes], 0.0).astype(BF16)
                v_h = v_ref[rows, h * dv:(h + 1) * dv]
                st = st_ref[h]
                sall_ref[ci, h] = st
                p = _scores(qm_h, knp, jnp.where(own, qk1[:, lanes], 0.0))
                o_ref[rows, h * dv:(h + 1) * dv] = _dot(p.astype(BF16), v_h) + _dot_nt(qe_h, st.astype(BF16))
                st_ref[h] = st * ebl_row[:, lanes] + _dot_tn(v_h, ke_h)
            return carry

        lax.fori_loop(0, nc, chunk, 0)

    return _pcall(
        body, name, [qk, v, gl, wg, bg], grid=(t // gt,),
        in_specs=[pl.BlockSpec((gt, 2 * kw), lambda i: (i, 0)), pl.BlockSpec((gt, vw), lambda i: (i, 0)),
                  pl.BlockSpec((gt, 128), lambda i: (i, 0)), pl.BlockSpec((128, kw), lambda i: (0, 0)),
                  pl.BlockSpec((1, kw), lambda i: (0, 0))],
        out_shape=[S((t, vw), F32), S((t, kw), F32), S((t // CHUNK, heads, dv, 128), F32)],
        out_specs=[pl.BlockSpec((gt, vw), lambda i: (i, 0)), pl.BlockSpec((gt, kw), lambda i: (i, 0)),
                   pl.BlockSpec((nc, heads, dv, 128), lambda i: (i, 0, 0, 0))],
        scratch_shapes=[pltpu.VMEM((heads, dv, 128), F32)],
        sem=("arbitrary",), plans=plans)


def _gla_bwd(qk, v, lg, do, sall, gl, wg, heads, name):
    t = qk.shape[0]
    kw, vw = qk.shape[1] // 2, v.shape[1]
    dk, dv = kw // heads, vw // heads
    gt = min(ROW_TILE, t)
    nc = gt // CHUNK
    nt = t // gt
    scale = dk ** -0.5

    def body(qk_ref, v_ref, lg_ref, do_ref, sall_ref, gl_ref, wg_ref, dqk_ref, dv_ref, dgl_ref, dwg_ref, dbg_ref, dst_ref, dgk_ref):
        @pl.when(pl.program_id(0) == 0)
        def _():
            dst_ref[...] = jnp.zeros_like(dst_ref)
            dwg_ref[...] = jnp.zeros_like(dwg_ref)
            dbg_ref[...] = jnp.zeros_like(dbg_ref)

        low01 = _tri(True).astype(BF16)
        up01 = _tri(False).astype(BF16)
        causal = _tri(True)
        lane = lax.broadcasted_iota(jnp.int32, (CHUNK, 128), 1)
        last_row = lax.broadcasted_iota(jnp.int32, (CHUNK, kw), 0) == CHUNK - 1

        def chunk(cj, carry):
            ci = nc - 1 - cj
            rows = pl.ds(pl.multiple_of(ci * CHUNK, CHUNK), CHUNK)
            q = qk_ref[rows, 0:kw] * scale
            k = qk_ref[rows, kw:2 * kw]
            qk1 = q.astype(BF16).astype(F32) * k.astype(BF16).astype(F32)
            lgc = lg_ref[rows, :]
            eb, ebl, em, en, ebl_row, qe, ke, qm, kn = _gla_chunk_terms(q, k, lgc, low01)
            dqe, dqm, dkn, dke, drow = [], [], [], [], []
            for pr in range(kw // 128):
                lanes = slice(128 * pr, 128 * pr + 128)
                knp = kn[:, lanes].astype(BF16)
                parts = []
                for half in range(2):
                    h = 2 * pr + half
                    own = (lane < 64) if half == 0 else (lane >= 64)
                    qm_h = jnp.where(own, qm[:, lanes], 0.0).astype(BF16)
                    qe_h = jnp.where(own, qe[:, lanes], 0.0).astype(BF16)
                    ke_h = jnp.where(own, ke[:, lanes], 0.0).astype(BF16)
                    v_h = v_ref[rows, h * dv:(h + 1) * dv]
                    do_h = do_ref[rows, h * dv:(h + 1) * dv]
                    st = sall_ref[ci, h]
                    dst = dst_ref[h]
                    stb, dstb = st.astype(BF16), dst.astype(BF16)
                    p = _scores(qm_h, knp, jnp.where(own, qk1[:, lanes], 0.0)).astype(BF16)
                    dp = jnp.where(causal, _dot_nt(do_h, v_h), 0.0).astype(BF16)
                    dv_ref[rows, h * dv:(h + 1) * dv] = (_dot_tn(p, do_h) + _dot_nt(ke_h, dstb)).astype(BF16)
                    parts.append((jnp.where(own, _dot(dp, knp), 0.0), _dot_tn(dp, qm_h), _dot(do_h, stb), _dot(v_h, dstb),
                                  jnp.sum(st * dst, axis=0, keepdims=True)))
                    dst_ref[h] = dst * ebl_row[:, lanes] + _dot_tn(do_h, qe_h)
                dqm.append(parts[0][0] + parts[1][0])
                dkn.append(parts[0][1] + parts[1][1])
                dqe.append(parts[0][2] + parts[1][2])
                dke.append(parts[0][3] + parts[1][3])
                drow.append(parts[0][4] + parts[1][4])
            dqm, dkn, dqe, dke, drow = [jnp.concatenate(a, axis=1) for a in (dqm, dkn, dqe, dke, drow)]
            dqk_ref[rows, 0:kw] = ((dqe * eb + dqm * em) * scale).astype(BF16)
            dqk_ref[rows, kw:2 * kw] = (dke * ebl + dkn * en).astype(BF16)
            tke = dke * ke
            db = dqe * qe + dqm * qm - dkn * kn - tke
            dbl = jnp.sum(tke, axis=0, keepdims=True) + drow * ebl_row
            db = db + jnp.where(last_row, dbl, 0.0)
            dlg = _dot_01(up01, db)
            dgk_ref[rows, :] = dlg * ((1.0 - jnp.exp(GATE_NORMALIZER * lgc)) / GATE_NORMALIZER)
            return carry

        lax.fori_loop(0, nc, chunk, 0)
        dgk = dgk_ref[...]
        dgkb = dgk.astype(BF16)
        dgl_ref[...] = _dot_nt(dgkb, wg_ref[...]).astype(BF16)
        dwg_ref[...] += _dot_tn(gl_ref[...].astype(BF16), dgkb)
        dbg_ref[...] += _rsum8(dgk)

    rev = lambda i: (nt - 1 - i, 0)
    return pl.pallas_call(
        body, name=name, grid=(nt,),
        in_specs=[pl.BlockSpec((gt, 2 * kw), rev), pl.BlockSpec((gt, vw), rev), pl.BlockSpec((gt, kw), rev),
                  pl.BlockSpec((gt, vw), rev), pl.BlockSpec((nc, heads, dv, 128), lambda i: (nt - 1 - i, 0, 0, 0)),
                  pl.BlockSpec((gt, 128), rev), pl.BlockSpec((128, kw), lambda i: (0, 0))],
        out_shape=[S((t, 2 * kw), BF16), S((t, vw), BF16), S((t, 128), BF16), S((128, kw), F32), S((8, kw), F32)],
        out_specs=[pl.BlockSpec((gt, 2 * kw), rev), pl.BlockSpec((gt, vw), rev), pl.BlockSpec((gt, 128), rev),
                   pl.BlockSpec((128, kw), lambda i: (0, 0)), pl.BlockSpec((8, kw), lambda i: (0, 0))],
        scratch_shapes=[pltpu.VMEM((heads, dv, 128), F32), pltpu.VMEM((gt, kw), F32)],
        compiler_params=_cp("arbitrary"),
    )(qk, v, lg, do, sall, gl, wg)


def _conv_taps(cx_ref, halo_ref, first, cw):
    tm = cx_ref.shape[0]
    u = cx_ref[:, cw:2 * cw].astype(F32) * cx_ref[:, 2 * cw:3 * cw].astype(F32)
    uh = halo_ref[:, cw:2 * cw].astype(F32) * halo_ref[:, 2 * cw:3 * cw].astype(F32)
    uh = jnp.where(first, 0.0, uh)
    row = lax.broadcasted_iota(jnp.int32, (tm, cw), 0)
    u1 = jnp.where(row == 0, uh[15:16, :], pltpu.roll(u, 1, 0))
    u2 = jnp.where(row == 0, uh[14:15, :], jnp.where(row == 1, uh[15:16, :], pltpu.roll(u, 2, 0)))
    return u, u1, u2


def _head_norm(o_h, gn):
    rstd = lax.rsqrt(jnp.mean(o_h * o_h, axis=-1, keepdims=True) + EPS)
    ohat = o_h * rstd
    return ohat, rstd, ohat * gn


def _mix_out(cx, o, go, conv_w, gn, w_out, x, ada9, heads, name):
    t, d = x.shape
    cw, vw = conv_w.shape[1], o.shape[1]
    dv = vw // heads
    tm = min(ROW_TILE, t)

    def body(cx_ref, halo_ref, o_ref, go_ref, cwt_ref, gn_ref, w_ref, x_ref, ada_ref, xo_ref, m_ref, y_ref):
        u, u1, u2 = _conv_taps(cx_ref, halo_ref, pl.program_id(0) == 0, cw)
        yc = cwt_ref[0:1, :] * u2 + cwt_ref[1:2, :] * u1 + cwt_ref[2:3, :] * u
        y_ref[:, 0:cw] = (cx_ref[:, 0:cw].astype(F32) * yc).astype(BF16)
        for h in range(heads):
            cols = slice(h * dv, (h + 1) * dv)
            _, _, on = _head_norm(o_ref[:, cols], gn_ref[...])
            g = go_ref[:, cols].astype(F32)
            y_ref[:, cw + h * dv:cw + (h + 1) * dv] = (on * (g * _sigmoid(g))).astype(BF16)
        m = _dot(y_ref[...], w_ref[...])
        m_ref[...] = m.astype(BF16)
        xo_ref[...] = x_ref[...] + ada_ref[5:6, :] * m

    return pl.pallas_call(
        body, name=name, grid=(t // tm,),
        in_specs=[pl.BlockSpec((tm, 3 * cw), lambda i: (i, 0)),
                  pl.BlockSpec((16, 3 * cw), lambda i: (jnp.maximum(i * (tm // 16) - 1, 0), 0)),
                  pl.BlockSpec((tm, vw), lambda i: (i, 0)), pl.BlockSpec((tm, vw), lambda i: (i, 0)),
                  pl.BlockSpec((3, cw), lambda i: (0, 0)), pl.BlockSpec((1, dv), lambda i: (0, 0)),
                  pl.BlockSpec((cw + vw, d), lambda i: (0, 0)), pl.BlockSpec((tm, d), lambda i: (i, 0)),
                  pl.BlockSpec((N_ADA, d), lambda i: (0, 0))],
        out_shape=[S((t, d), F32), S((t, d), BF16), S((t, cw + vw), BF16)],
        out_specs=[pl.BlockSpec((tm, d), lambda i: (i, 0)), pl.BlockSpec((tm, d), lambda i: (i, 0)),
                   pl.BlockSpec((tm, cw + vw), lambda i: (i, 0))],
        compiler_params=_cp("parallel"),
    )(cx, cx, o, go, conv_w, gn, w_out, x, ada9)


def _mix_bwd_a(dxo, m, ada9, w_out, cx, o, go, conv_w, gn, heads, name, plans=()):
    t, d = dxo.shape
    cw, vw = conv_w.shape[1], o.shape[1]
    dv = vw // heads
    tm = min(ROW_TILE, t)

    def body(dxo_ref, m_ref, ada_ref, w_ref, cx_ref, halo_ref, o_ref, go_ref, cwt_ref, gn_ref,
             dm_ref, dyc_ref, dcb_ref, do_ref, dgo_ref, dg_ref, dcw_ref, dgn_ref):
        @pl.when(pl.program_id(0) == 0)
        def _():
            dg_ref[...] = jnp.zeros_like(dg_ref)
            dcw_ref[...] = jnp.zeros_like(dcw_ref)
            dgn_ref[...] = jnp.zeros_like(dgn_ref)

        dxo_t = dxo_ref[...]
        dmb = (ada_ref[5:6, :] * dxo_t).astype(BF16)
        dm_ref[...] = dmb
        dg_ref[...] += _rsum8(dxo_t * m_ref[...].astype(F32))
        dy = _dot_nt(dmb, w_ref[...])
        u, u1, u2 = _conv_taps(cx_ref, halo_ref, pl.program_id(0) == 0, cw)
        yc = cwt_ref[0:1, :] * u2 + cwt_ref[1:2, :] * u1 + cwt_ref[2:3, :] * u
        dyv = dy[:, 0:cw]
        dcb_ref[...] = (dyv * yc).astype(BF16)
        dyc = dyv * cx_ref[:, 0:cw].astype(F32)
        dyc_ref[...] = dyc
        dcw_ref[0] += _rsum8(dyc * u2)
        dcw_ref[1] += _rsum8(dyc * u1)
        dcw_ref[2] += _rsum8(dyc * u)
        for h in range(heads):
            cols = slice(h * dv, (h + 1) * dv)
            ohat, rstd, on = _head_norm(o_ref[:, cols], gn_ref[...])
            g = go_ref[:, cols].astype(F32)
            sg = _sigmoid(g)
            dyg = dy[:, cw + h * dv:cw + (h + 1) * dv]
            dgo_ref[:, cols] = (dyg * on * (sg * (1.0 + g * (1.0 - sg)))).astype(BF16)
            don = dyg * (g * sg)
            dgn_ref[...] += _rsum8(don * ohat)
            tt = don * gn_ref[...]
            do_ref[:, cols] = (rstd * (tt - ohat * jnp.mean(tt * ohat, axis=-1, keepdims=True))).astype(BF16)

    return _pcall(
        body, name, [dxo, m, ada9, w_out, cx, cx, o, go, conv_w, gn], grid=(t // tm,),
        in_specs=[pl.BlockSpec((tm, d), lambda i: (i, 0)), pl.BlockSpec((tm, d), lambda i: (i, 0)),
                  pl.BlockSpec((N_ADA, d), lambda i: (0, 0)), pl.BlockSpec((cw + vw, d), lambda i: (0, 0)),
                  pl.BlockSpec((tm, 3 * cw), lambda i: (i, 0)),
                  pl.BlockSpec((16, 3 * cw), lambda i: (jnp.maximum(i * (tm // 16) - 1, 0), 0)),
                  pl.BlockSpec((tm, vw), lambda i: (i, 0)), pl.BlockSpec((tm, vw), lambda i: (i, 0)),
                  pl.BlockSpec((3, cw), lambda i: (0, 0)), pl.BlockSpec((1, dv), lambda i: (0, 0))],
        out_shape=[S((t, d), BF16), S((t, cw), F32), S((t, cw), BF16), S((t, vw), BF16), S((t, vw), BF16),
                   S((8, d), F32), S((3, 8, cw), F32), S((8, dv), F32)],
        out_specs=[pl.BlockSpec((tm, d), lambda i: (i, 0)), pl.BlockSpec((tm, cw), lambda i: (i, 0)),
                   pl.BlockSpec((tm, cw), lambda i: (i, 0)), pl.BlockSpec((tm, vw), lambda i: (i, 0)),
                   pl.BlockSpec((tm, vw), lambda i: (i, 0)), pl.BlockSpec((8, d), lambda i: (0, 0)),
                   pl.BlockSpec((3, 8, cw), lambda i: (0, 0, 0)), pl.BlockSpec((8, dv), lambda i: (0, 0))],
        sem=("arbitrary",), plans=plans)


def _mix_bwd_b(dyc, cx, dcb, dqk, dvv, dgo, dgl, conv_w, w, x, dxo, ada9, nrm, name):
    t, d = x.shape
    cw = conv_w.shape[1]
    n = w.shape[1]
    tm = min(ROW_TILE, t)
    nt = t // tm
    pieces = [dcb.shape[1], cw, cw, dqk.shape[1], dvv.shape[1], dgo.shape[1], dgl.shape[1]]
    assert sum(pieces) == n

    def body(dyc_ref, nxt_ref, cx_ref, dcb_ref, dqk_ref, dv_ref, dgo_ref, dgl_ref, cwt_ref, w_ref, x_ref, dxo_ref, ada_ref, n_ref,
             dx_ref, dp_ref, dsh_ref, dsc_ref, dn_ref):
        i = pl.program_id(0)

        @pl.when(i == 0)
        def _():
            dsh_ref[...] = jnp.zeros_like(dsh_ref)
            dsc_ref[...] = jnp.zeros_like(dsc_ref)
            dn_ref[...] = jnp.zeros_like(dn_ref)

        dyc_t = dyc_ref[...]
        nxt = jnp.where(i == nt - 1, 0.0, nxt_ref[...])
        row = lax.broadcasted_iota(jnp.int32, (tm, cw), 0)
        d1 = jnp.where(row == tm - 1, nxt[0:1, :], pltpu.roll(dyc_t, tm - 1, 0))
        d2 = jnp.where(row == tm - 2, nxt[0:1, :], jnp.where(row == tm - 1, nxt[1:2, :], pltpu.roll(dyc_t, tm - 2, 0)))
        du = cwt_ref[2:3, :] * dyc_t + cwt_ref[1:2, :] * d1 + cwt_ref[0:1, :] * d2
        c0 = 0
        dp_ref[:, c0:c0 + cw] = dcb_ref[...]
        dp_ref[:, cw:2 * cw] = (du * cx_ref[:, 2 * cw:3 * cw].astype(F32)).astype(BF16)
        dp_ref[:, 2 * cw:3 * cw] = (du * cx_ref[:, cw:2 * cw].astype(F32)).astype(BF16)
        c0 = 3 * cw
        for ref in (dqk_ref, dv_ref, dgo_ref, dgl_ref):
            wd = ref.shape[1]
            dp_ref[:, c0:c0 + wd] = ref[...]
            c0 += wd
        dh = _dot_nt(dp_ref[...], w_ref[...])
        dx, tsh, tsc, tn = _normmod_bwd(dh, x_ref[...], n_ref[...], ada_ref[4:5, :])
        dx_ref[...] = dxo_ref[...] + dx
        dsh_ref[...] += _rsum8(tsh)
        dsc_ref[...] += _rsum8(tsc)
        dn_ref[...] += _rsum8(tn)

    row_spec = lambda wd: pl.BlockSpec((tm, wd), lambda i: (i, 0))
    vec = pl.BlockSpec((8, d), lambda i: (0, 0))
    return pl.pallas_call(
        body, name=name, grid=(nt,),
        in_specs=[row_spec(cw), pl.BlockSpec((8, cw), lambda i: (jnp.minimum((i + 1) * (tm // 8), t // 8 - 1), 0)),
                  row_spec(3 * cw), row_spec(cw), row_spec(dqk.shape[1]), row_spec(dvv.shape[1]), row_spec(dgo.shape[1]),
                  row_spec(dgl.shape[1]), pl.BlockSpec((3, cw), lambda i: (0, 0)), pl.BlockSpec((d, n), lambda i: (0, 0)),
                  row_spec(d), row_spec(d), pl.BlockSpec((N_ADA, d), lambda i: (0, 0)), pl.BlockSpec((1, d), lambda i: (0, 0))],
        out_shape=[S((t, d), F32), S((t, n), BF16), S((8, d), F32), S((8, d), F32), S((8, d), F32)],
        out_specs=[row_spec(d), row_spec(n), vec, vec, vec],
        compiler_params=_cp("arbitrary"),
    )(dyc, dyc, cx, dcb, dqk, dvv, dgo, dgl, conv_w, w, x, dxo, ada9, nrm)


def _loss_head(x, target, nrm, name):
    t, d = x.shape
    tm = min(ROW_TILE, t)
    nt = t // tm

    def body(x_ref, tg_ref, n_ref, loss_ref, dx_ref, dn_ref, acc_ref):
        i = pl.program_id(0)

        @pl.when(i == 0)
        def _():
            acc_ref[...] = jnp.zeros_like(acc_ref)
            dn_ref[...] = jnp.zeros_like(dn_ref)

        xt = x_ref[...]
        rstd = lax.rsqrt(jnp.mean(xt * xt, axis=-1, keepdims=True) + EPS)
        xhat = xt * rstd
        err = xhat * n_ref[...] - tg_ref[...]
        acc_ref[...] += _rsum8(err * err)
        dy = err * (1.0 / d)
        dn_ref[...] += _rsum8(dy * xhat)
        dxhat = dy * n_ref[...]
        dx_ref[...] = rstd * (dxhat - xhat * jnp.mean(dxhat * xhat, axis=-1, keepdims=True))

        @pl.when(i == nt - 1)
        def _():
            loss_ref[...] = jnp.full(loss_ref.shape, (0.5 / d) * jnp.sum(acc_ref[...]), F32)

    return pl.pallas_call(
        body, name=name, grid=(nt,),
        in_specs=[pl.BlockSpec((tm, d), lambda i: (i, 0)), pl.BlockSpec((tm, d), lambda i: (i, 0)),
                  pl.BlockSpec((1, d), lambda i: (0, 0))],
        out_shape=[S((1, 128), F32), S((t, d), F32), S((8, d), F32)],
        out_specs=[pl.BlockSpec((1, 128), lambda i: (0, 0)), pl.BlockSpec((tm, d), lambda i: (i, 0)),
                   pl.BlockSpec((8, d), lambda i: (0, 0))],
        scratch_shapes=[pltpu.VMEM((8, d), F32)],
        compiler_params=_cp("arbitrary"),
    )(x, target, nrm)


def _pack_smalls(vec_parts, dcw, dbg, dgn, dwg, rank, name):
    d = vec_parts[0].shape[1]
    cw, kw, dv = dcw.shape[2], dbg.shape[1], dgn.shape[1]
    nv = len(vec_parts)
    assert 2 * cw == d and cw + kw + dv <= d and (rank * kw) % d == 0 and nv + 2 + rank * kw // d <= PACK_ROWS
    per_row = d // kw

    def body(*refs):
        vrefs, (dcw_ref, dbg_ref, dgn_ref, dwg_ref, o_ref) = refs[:nv], refs[nv:]
        o_ref[...] = jnp.zeros_like(o_ref)
        for r, ref in enumerate(vrefs):
            o_ref[r:r + 1, :] = jnp.sum(ref[...], axis=0, keepdims=True)
        o_ref[nv:nv + 1, 0:cw] = jnp.sum(dcw_ref[0], axis=0, keepdims=True)
        o_ref[nv:nv + 1, cw:2 * cw] = jnp.sum(dcw_ref[1], axis=0, keepdims=True)
        o_ref[nv + 1:nv + 2, 0:cw] = jnp.sum(dcw_ref[2], axis=0, keepdims=True)
        o_ref[nv + 1:nv + 2, cw:cw + kw] = jnp.sum(dbg_ref[...], axis=0, keepdims=True)
        o_ref[nv + 1:nv + 2, cw + kw:cw + kw + dv] = jnp.sum(dgn_ref[...], axis=0, keepdims=True)
        for r in range(rank):
            o_ref[nv + 2 + r // per_row:nv + 3 + r // per_row, (r % per_row) * kw:(r % per_row + 1) * kw] = dwg_ref[r:r + 1, :]

    return pl.pallas_call(body, name=name, out_shape=S((PACK_ROWS, d), F32), compiler_params=_cp())(*vec_parts, dcw, dbg, dgn, dwg)


def _sum_slots(a, name):
    def body(a_ref, o_ref):
        acc = a_ref[0]
        for s in range(1, NDEV):
            acc = acc + a_ref[s]
        o_ref[...] = acc

    return pl.pallas_call(body, name=name, out_shape=S(a.shape[1:], F32), compiler_params=_cp())(a)


def _adamw(w, g, m, v):
    m = ADAM_B1 * m + (1.0 - ADAM_B1) * g
    v = ADAM_B2 * v + (1.0 - ADAM_B2) * (g * g)
    m_hat = m / (1.0 - ADAM_B1 ** ADAM_STEP)
    v_hat = v / (1.0 - ADAM_B2 ** ADAM_STEP)
    return -ADAM_LR * (m_hat / (jnp.sqrt(v_hat) + ADAM_EPS) + ADAM_WD * w), m, v


def _adam_slots(recv, w, m, v, name):
    r, c = w.shape
    slots = recv.shape[0]
    tr = _row_tile(r, c)

    def body(recv_ref, w_ref, m_ref, v_ref, g_ref, d_ref, mo_ref, vo_ref):
        g = recv_ref[0].astype(F32)
        for s in range(1, slots):
            g = g + recv_ref[s].astype(F32)
        g_ref[...] = g
        d_ref[...], mo_ref[...], vo_ref[...] = _adamw(w_ref[...], g, m_ref[...], v_ref[...])

    blk = pl.BlockSpec((tr, c), lambda i: (i, 0))
    return pl.pallas_call(
        body, name=name, grid=(r // tr,),
        in_specs=[pl.BlockSpec((slots, tr, c), lambda i: (0, i, 0)), blk, blk, blk],
        out_shape=[S((r, c), F32)] * 4, out_specs=[blk] * 4, compiler_params=_cp("parallel"),
    )(recv, w, m, v)


def _adam_w_ada(act_t, dada, w, m, v, name):
    r, c = w.shape
    tr = 128
    nb = act_t.shape[1]

    def body(a_ref, da_ref, w_ref, m_ref, v_ref, g_ref, d_ref, mo_ref, vo_ref):
        g = a_ref[:, 0:1] * da_ref[0:1, :]
        for b in range(1, nb):
            g = g + a_ref[:, b:b + 1] * da_ref[b:b + 1, :]
        g_ref[...] = g
        d_ref[...], mo_ref[...], vo_ref[...] = _adamw(w_ref[...], g, m_ref[...], v_ref[...])

    blk = pl.BlockSpec((tr, c), lambda i: (i, 0))
    return pl.pallas_call(
        body, name=name, grid=(r // tr,),
        in_specs=[pl.BlockSpec((tr, nb), lambda i: (i, 0)), pl.BlockSpec((nb, c), lambda i: (0, 0)), blk, blk, blk],
        out_shape=[S((r, c), F32)] * 4, out_specs=[blk] * 4, compiler_params=_cp("parallel"),
    )(act_t, dada, w, m, v)


def _adam_smalls(ws, gs, ms, vs, name):
    n = len(ws)

    def body(*refs):
        w_r, g_r, m_r, v_r = (refs[k * n:(k + 1) * n] for k in range(4))
        d_o, m_o, v_o = (refs[(4 + k) * n:(5 + k) * n] for k in range(3))
        for i in range(n):
            d_o[i][...], m_o[i][...], v_o[i][...] = _adamw(w_r[i][...], g_r[i][...], m_r[i][...], v_r[i][...])

    shapes = [S(w.shape, F32) for w in ws]
    outs = pl.pallas_call(body, name=name, out_shape=shapes * 3, compiler_params=_cp())(*ws, *gs, *ms, *vs)
    return outs[:n], outs[n:2 * n], outs[2 * n:]


def kernel(x, c, w_ada, b_ada, norm_ffn1, w_ffn1_in, w_ffn1_out, norm_mix, w_mix_in, conv_w, w_gk2, b_gk, gla_norm, w_mix_out, norm_ffn2, w_ffn2_in, w_ffn2_out, norm_final, loss_target, m_w_ada, m_b_ada, m_norm_ffn1, m_w_ffn1_in, m_w_ffn1_out, m_norm_mix, m_w_mix_in, m_conv_w, m_w_gk2, m_b_gk, m_gla_norm, m_w_mix_out, m_norm_ffn2, m_w_ffn2_in, m_w_ffn2_out, m_norm_final, v_w_ada, v_b_ada, v_norm_ffn1, v_w_ffn1_in, v_w_ffn1_out, v_norm_mix, v_w_mix_in, v_conv_w, v_w_gk2, v_b_gk, v_gla_norm, v_w_mix_out, v_norm_ffn2, v_w_ffn2_in, v_w_ffn2_out, v_norm_final):
    t, d = x.shape[1], x.shape[2]
    x0, tgt = x[0], loss_target[0]
    rank, kw = w_gk2.shape[1], w_gk2.shape[2] * NDEV
    cw = conv_w.shape[2] * NDEV
    dv = gla_norm.shape[1]
    vw = d - cw
    heads = vw // dv
    mix_cols = w_mix_in.shape[2]
    widths = [3 * cw, 2 * kw, vw, vw, 128]
    n_proj = 3 * cw + 2 * kw + 2 * vw + rank
    assert n_proj == mix_cols * NDEV and rank <= 128
    me = 4 * lax.axis_index("x") + 2 * lax.axis_index("y") + lax.axis_index("c")

    core = lax.axis_index("c").astype(jnp.int32).reshape(1)
    bf = lambda a: a[0].astype(BF16)
    nb = NDEV // 2

    ((c_all, w1i, cwt_all, wg_all),) = _exchange([_plan_gather([c, bf(w_ffn1_in), conv_w[0], w_gk2[0]])], "gather_first")
    cwt = cwt_all.transpose(1, 0, 2).reshape(conv_w.shape[1], cw)
    wg = jnp.pad(wg_all.transpose(1, 0, 2).reshape(rank, kw), ((0, 128 - rank), (0, 0))).astype(BF16)

    act_all, p_ada = _ada_partial(c_all.reshape(NDEV, d), w_ada[0])
    ((p_all,),) = _exchange([_plan_all_to_all([p_ada], True)], "gather_ada")
    ada9 = _ada_finish(p_all, b_ada).reshape(N_ADA, d)

    (h1, gu1, s1), ((w1o, wmi),) = _ffn_in(x0, ada9, norm_ffn1, w1i, 0, 1, "ffn1_in", plans=[_plan_gather([bf(w_ffn1_out), bf(w_mix_in)])])
    w1o = w1o.reshape(nb, -1, d)
    wmi = jnp.pad(wmi.transpose(1, 0, 2).reshape(d, n_proj), ((0, 0), (0, sum(widths) - n_proj)))
    (x1, f1), ((wmo,),) = _ffn_out(s1, w1o, x0, ada9, 2, 0.5, "ffn1_out", plans=[_plan_gather([bf(w_mix_out)])])
    wmo = wmo.reshape(cw + vw, d)
    (h2, cx, qk, vv, go, gl), ((w2o,),) = _mix_in(x1, ada9, norm_mix, wmi, widths, [BF16, F32, BF16, BF16, F32], "mix_in",
                                                 plans=[_plan_gather([bf(w_ffn2_out)])])
    w2o = w2o.reshape(nb, -1, d)
    (o, lg, sall), ((w2i,),) = _gla_fwd(qk, vv, gl, wg, b_gk, heads, "gla_fwd", plans=[_plan_gather([bf(w_ffn2_in)])])
    x2, mm, ycat = _mix_out(cx, o, go, cwt, gla_norm, wmo, x1, ada9, heads, "mix_out")
    (h3, gu3, s3), _ = _ffn_in(x2, ada9, norm_ffn2, w2i, 6, 7, "ffn2_in")
    (x3, f3), _ = _ffn_out(s3, w2o, x2, ada9, 8, 0.5, "ffn2_out")
    loss_v, dx3, dnf = _loss_head(x3, tgt, norm_final.reshape(1, d), "loss_head")

    dx2, r2i, sum2o, (dsh3, dsc3, dg3, dn3), _, _ = _ffn_backward(
        dx3, x2, h3, gu3, s3, f3, ada9, norm_ffn2, w2i, w2o, (6, 7, 8), core, "ffn2_bwd")
    (dm, dyc, dcb, do, dgo, dg2, dcw, dgn), ((r2o,),) = _mix_bwd_a(dx2, mm, ada9, wmo, cx, o, go, cwt, gla_norm, heads, "mix_bwd_a",
                                                                 plans=[_plan_chip_swap([sum2o])])
    dqk, dvv, dgl, dwg, dbg = _gla_bwd(qk, vv, lg, do, sall, gl, wg, heads, "gla_bwd")
    dx1, dproj, dsh2, dsc2, dnm = _mix_bwd_b(dyc, cx, dcb, dqk, dvv, dgo, dgl, cwt, wmi, x1, dx2, ada9, norm_mix, "mix_bwd_b")
    n_pad = sum(widths)
    tn = n_pad // 5
    dwmi, _ = _tn_matmul(h2, dproj, lambda tk: (tk, d), lambda sb, k: (k, 0), lambda tk: (tk, tn), lambda sb, k: (k, sb),
                         (d, n_pad), (d, tn), lambda sb, k: (0, sb), 5, "mix_dwin")
    dwmo, _ = _tn_matmul(ycat, dm, lambda tk: (tk, cw + vw), lambda sb, k: (k, 0), lambda tk: (tk, d), lambda sb, k: (k, 0),
                         (cw + vw, d), (cw + vw, d), lambda sb, k: (0, 0), 1, "mix_dwout")
    dwmi = dwmi[:, :n_proj].reshape(d, NDEV, mix_cols).transpose(1, 0, 2)
    dwmo = dwmo.reshape(NDEV, -1, d)
    dx0, r1i, sum1o, (dsh1, dsc1, dg1, dn1), _, ((rmi, rmo),) = _ffn_backward(
        dx1, x0, h1, gu1, s1, f1, ada9, norm_ffn1, w1i, w1o, (0, 1, 2), core, "ffn1_bwd",
        ds_plans=[_plan_sibling_swap([dwmi, dwmo])],
        dwin_plans=lambda moved: [_plan_chip_swap([_pair_add(dwmi, moved[0][0], core, "mix_dwin_add"),
                                                   _pair_add(dwmo, moved[0][1], core, "mix_dwout_add")])])
    pack = _pack_smalls([dn1, dnm, dn3, dnf, dsh1, dsc1, dg1, dsh2, dsc2, dg2, dsh3, dsc3, dg3], dcw, dbg, dgn, dwg, rank, "pack_smalls")
    (r1o,), (pack_all,) = _exchange([_plan_chip_swap([sum1o]), _plan_all_to_all([pack], True)], "grads_last")
    tot = _sum_slots(pack_all, "sum_smalls")

    res = {}
    for nm, recv, w, m, v in (("w_ffn1_in", r1i, w_ffn1_in, m_w_ffn1_in, v_w_ffn1_in), ("w_ffn1_out", r1o, w_ffn1_out, m_w_ffn1_out, v_w_ffn1_out),
                              ("w_mix_in", rmi, w_mix_in, m_w_mix_in, v_w_mix_in), ("w_mix_out", rmo, w_mix_out, m_w_mix_out, v_w_mix_out),
                              ("w_ffn2_in", r2i, w_ffn2_in, m_w_ffn2_in, v_w_ffn2_in), ("w_ffn2_out", r2o, w_ffn2_out, m_w_ffn2_out, v_w_ffn2_out)):
        res[nm] = [a[None] for a in _adam_slots(recv, w[0], m[0], v[0], "adam_" + nm)]

    cols_ada = w_ada.shape[2]
    dada_all = pack_all[:, 4:4 + N_ADA, :].reshape(NDEV, N_ADA * d)
    dada_mine = lax.dynamic_slice_in_dim(dada_all, me * cols_ada, cols_ada, axis=1)
    res["w_ada"] = [a[None] for a in _adam_w_ada(act_all.T, dada_mine, w_ada[0], m_w_ada[0], v_w_ada[0], "adam_w_ada")]

    nv = 4 + N_ADA
    g_small = {
        "b_ada": tot[4:nv].reshape(1, N_ADA * d),
        "norm_ffn1": tot[0:1], "norm_mix": tot[1:2], "norm_ffn2": tot[2:3], "norm_final": tot[3:4],
        "conv_w": lax.dynamic_slice_in_dim(
            jnp.concatenate([tot[nv:nv + 1, 0:cw], tot[nv:nv + 1, cw:2 * cw], tot[nv + 1:nv + 2, 0:cw]], axis=0), me * (cw // NDEV), cw // NDEV, axis=1),
        "w_gk2": lax.dynamic_slice_in_dim(tot[nv + 2:nv + 2 + rank * kw // d].reshape(rank, kw), me * (kw // NDEV), kw // NDEV, axis=1),
        "b_gk": tot[nv + 1:nv + 2, cw:cw + kw],
        "gla_norm": tot[nv + 1:nv + 2, cw + kw:cw + kw + dv],
    }
    small = {"b_ada": (b_ada, m_b_ada, v_b_ada), "norm_ffn1": (norm_ffn1, m_norm_ffn1, v_norm_ffn1), "norm_mix": (norm_mix, m_norm_mix, v_norm_mix),
             "norm_ffn2": (norm_ffn2, m_norm_ffn2, v_norm_ffn2), "norm_final": (norm_final, m_norm_final, v_norm_final),
             "conv_w": (conv_w, m_conv_w, v_conv_w), "w_gk2": (w_gk2, m_w_gk2, v_w_gk2), "b_gk": (b_gk, m_b_gk, v_b_gk),
             "gla_norm": (gla_norm, m_gla_norm, v_gla_norm)}
    names = list(small)
    flat = lambda a: a.reshape(-1, a.shape[-1])
    dl, mo, vo = _adam_smalls([flat(small[n][0]) for n in names], [g_small[n] for n in names],
                              [flat(small[n][1]) for n in names], [flat(small[n][2]) for n in names], "adam_smalls")
    for i, n in enumerate(names):
        shp = small[n][0].shape
        res[n] = [g_small[n].reshape(shp), dl[i].reshape(shp), mo[i].reshape(shp), vo[i].reshape(shp)]

    loss = lax.psum(loss_v[0, 0], ("x", "y", "c"))
    order = ["w_ada", "b_ada", "norm_ffn1", "w_ffn1_in", "w_ffn1_out", "norm_mix", "w_mix_in", "conv_w", "w_gk2", "b_gk", "gla_norm",
             "w_mix_out", "norm_ffn2", "w_ffn2_in", "w_ffn2_out", "norm_final"]
    return (loss, dx0[None], *[res[n][0] for n in order], *[res[n][1] for n in order], *[res[n][2] for n in order], *[res[n][3] for n in order])
```

```python
import collections
import functools

import jax
import jax.numpy as jnp
from jax import lax
from jax.experimental import pallas as pl
from jax.experimental.pallas import tpu as pltpu

F32 = jnp.float32
BF16 = jnp.bfloat16
S = jax.ShapeDtypeStruct

NDEV = 8
EPS = 1e-6
GATE_NORMALIZER = 16.0
CHUNK = 64
N_ADA = 9
ADAM_LR, ADAM_B1, ADAM_B2, ADAM_EPS, ADAM_WD, ADAM_STEP = 0.001, 0.9, 0.999, 1e-08, 0.01, 10
V7X_VMEM_LIMIT = 56 * 1024 * 1024
ROW_TILE = 512
K_TILE = 1024
PACK_ROWS = 24
ANY = pl.BlockSpec(memory_space=pl.ANY)


def _cp(*sem):
    return pltpu.CompilerParams(dimension_semantics=sem or None, vmem_limit_bytes=V7X_VMEM_LIMIT)


def _dot(a, b):
    return jnp.dot(a, b, preferred_element_type=F32)


def _dot_nt(a, b):
    return lax.dot_general(a, b, (((1,), (1,)), ((), ())), preferred_element_type=F32)


def _dot_tn(a, b):
    return lax.dot_general(a, b, (((0,), (0,)), ((), ())), preferred_element_type=F32)


def _rsum8(a):
    r, c = a.shape
    return jnp.sum(a.reshape(r // 8, 8, c), axis=0)


def _row_tile(r, c):
    for cand in (256, 128, 176, 88, 64, 32, 16, 8):
        if r % cand == 0 and cand * c * 4 <= 1024 * 1024:
            return cand
    return r


def _sigmoid(x):
    return 1.0 / (1.0 + jnp.exp(-x))


def _normmod(x, nrm, sc, sh):
    rstd = lax.rsqrt(jnp.mean(x * x, axis=-1, keepdims=True) + EPS)
    xhat = x * rstd
    return xhat, rstd, (xhat * nrm) * (1.0 + sc) + sh


def _normmod_bwd(dh, x, nrm, sc):
    rstd = lax.rsqrt(jnp.mean(x * x, axis=-1, keepdims=True) + EPS)
    xhat = x * rstd
    dxhat = dh * (nrm * (1.0 + sc))
    dx = rstd * (dxhat - xhat * jnp.mean(dxhat * xhat, axis=-1, keepdims=True))
    return dx, dh, dh * (xhat * nrm), dh * ((1.0 + sc) * xhat)


def _place():
    x, y, c = lax.axis_index("x"), lax.axis_index("y"), lax.axis_index("c")
    return x, y, c, 4 * x + 2 * y + c


def _peer(x, y, c, k):
    px = 1 - x if k & 4 else x
    py = 1 - y if k & 2 else y
    pc = 1 - c if k & 1 else c
    return (px, py, pc), 4 * px + 2 * py + pc


def _remote(src, dst, send_sem, recv_sem, peer):
    return pltpu.make_async_remote_copy(src_ref=src, dst_ref=dst, send_sem=send_sem, recv_sem=recv_sem,
                                        device_id=peer, device_id_type=pl.DeviceIdType.MESH)


_Plan = collections.namedtuple("_Plan", "inputs out_shapes sem_shapes start finish")


def _plan_all_to_all(xs, gather):
    n = len(xs)

    def copies(ins, outs, sems, landed):
        send_sems, recv_sems, local_sems = sems
        x, y, c, me = _place()
        local = [pltpu.make_async_copy(ins[i] if gather else ins[i].at[me], outs[i].at[me], local_sems.at[i]) for i in range(n)]
        remote = []
        for k in range(1, NDEV):
            peer, pid = _peer(x, y, c, k)
            for i in range(n):
                remote.append(_remote(ins[i] if gather else ins[i].at[pid], outs[i].at[pid if landed else me],
                                      send_sems.at[i, k - 1], recv_sems.at[i, k - 1], peer))
        return local, remote

    def start(ins, outs, sems):
        local, remote = copies(ins, outs, sems, False)
        for cp in local + remote:
            cp.start()

    def finish(ins, outs, sems):
        local, remote = copies(ins, outs, sems, True)
        for cp in remote + local:
            cp.wait()

    return _Plan(list(xs), [S((NDEV,) + a.shape, a.dtype) if gather else S(a.shape, a.dtype) for a in xs],
                 [pltpu.SemaphoreType.DMA((n, NDEV - 1)), pltpu.SemaphoreType.DMA((n, NDEV - 1)), pltpu.SemaphoreType.DMA((n,))],
                 start, finish)


def _other_chips(x, y):
    return [(1 - x, y), (x, 1 - y), (1 - x, 1 - y)]


def _plan_gather(xs):
    n = len(xs)

    def copies(ins, outs, sems):
        send_sems, recv_sems, local_sems = sems
        x, y, c, me = _place()
        sib, sib_id = (x, y, 1 - c), 4 * x + 2 * y + 1 - c
        chips = _other_chips(x, y)
        local = [pltpu.make_async_copy(ins[i], outs[i].at[me], local_sems.at[i]) for i in range(n)]
        first, arrive, forward, from_sibling = [], [], [], []
        for i in range(n):
            first.append(_remote(ins[i], outs[i].at[me], send_sems.at[i, 0], recv_sems.at[i, 0], sib))
            from_sibling.append(_remote(ins[i], outs[i].at[sib_id], send_sems.at[i, 0], recv_sems.at[i, 0], sib))
        for j, (px, py) in enumerate(chips):
            slot = 4 * px + 2 * py
            arrive.append([])
            forward.append([])
            for i in range(n):
                first.append(_remote(ins[i], outs[i].at[me], send_sems.at[i, 1 + j], recv_sems.at[i, 1 + j], (px, py, c)))
                arrive[j].append(_remote(ins[i], outs[i].at[slot + c], send_sems.at[i, 1 + j], recv_sems.at[i, 1 + j], (px, py, c)))
                forward[j].append(_remote(outs[i].at[slot + c], outs[i].at[slot + c], send_sems.at[i, 4 + j], recv_sems.at[i, 4 + j], sib))
                from_sibling.append(_remote(ins[i], outs[i].at[slot + 1 - c], send_sems.at[i, 4 + j], recv_sems.at[i, 4 + j], sib))
        return local, first, arrive, forward, from_sibling

    def start(ins, outs, sems):
        local, first, _, _, _ = copies(ins, outs, sems)
        for cp in local + first:
            cp.start()

    def finish(ins, outs, sems):
        local, first, arrive, forward, from_sibling = copies(ins, outs, sems)
        for landed, onward in zip(arrive, forward):
            for cp in landed:
                cp.wait_recv()
            for cp in onward:
                cp.start()
        for cp in from_sibling:
            cp.wait_recv()
        for cp in first + [cp for onward in forward for cp in onward]:
            cp.wait_send()
        for cp in local:
            cp.wait()

    return _Plan(list(xs), [S((NDEV,) + a.shape, a.dtype) for a in xs],
                 [pltpu.SemaphoreType.DMA((n, NDEV - 1)), pltpu.SemaphoreType.DMA((n, NDEV - 1)), pltpu.SemaphoreType.DMA((n,))],
                 start, finish)


def _plan_sibling_swap(gs):
    n = len(gs)

    def copies(ins, outs, sems):
        send_sems, recv_sems = sems
        x, y, c, _ = _place()
        return [_remote(ins[i].at[2 * j + 1 - c], outs[i].at[j], send_sems.at[i, j], recv_sems.at[i, j], (x, y, 1 - c))
                for i in range(n) for j in range(NDEV // 2)]

    def start(ins, outs, sems):
        for cp in copies(ins, outs, sems):
            cp.start()

    def finish(ins, outs, sems):
        for cp in copies(ins, outs, sems):
            cp.wait()

    return _Plan(list(gs), [S((NDEV // 2,) + a.shape[1:], a.dtype) for a in gs],
                 [pltpu.SemaphoreType.DMA((n, NDEV // 2)), pltpu.SemaphoreType.DMA((n, NDEV // 2))], start, finish)


def _pair_add(g, r1, core, name):
    _, r, c = g.shape
    tr = _row_tile(r, c)

    def body(core_ref, g_ref, r_ref, o_ref):
        o_ref[...] = (g_ref[...].astype(F32) + r_ref[...].astype(F32)).astype(BF16)

    return pl.pallas_call(
        body, name=name,
        grid_spec=pltpu.PrefetchScalarGridSpec(
            num_scalar_prefetch=1, grid=(NDEV // 2, r // tr),
            in_specs=[pl.BlockSpec((None, tr, c), lambda j, k, core_ref: (2 * j + core_ref[0], k, 0)),
                      pl.BlockSpec((None, tr, c), lambda j, k, core_ref: (j, k, 0))],
            out_specs=pl.BlockSpec((None, tr, c), lambda j, k, core_ref: (j, k, 0))),
        out_shape=S((NDEV // 2, r, c), BF16), compiler_params=_cp("parallel", "parallel"),
    )(core, g, r1)


def _plan_chip_swap(ps):
    n = len(ps)

    def copies(ins, outs, sems, landed):
        send_sems, recv_sems, local_sems = sems
        x, y, c, _ = _place()
        mine = 2 * x + y
        local = [pltpu.make_async_copy(ins[i].at[mine], outs[i].at[mine], local_sems.at[i]) for i in range(n)]
        remote = [_remote(ins[i].at[2 * px + py], outs[i].at[2 * px + py if landed else mine], send_sems.at[i, j], recv_sems.at[i, j], (px, py, c))
                  for j, (px, py) in enumerate(_other_chips(x, y)) for i in range(n)]
        return local, remote

    def start(ins, outs, sems):
        local, remote = copies(ins, outs, sems, False)
        for cp in local + remote:
            cp.start()

    def finish(ins, outs, sems):
        local, remote = copies(ins, outs, sems, True)
        for cp in remote + local:
            cp.wait()

    return _Plan(list(ps), [S(a.shape, a.dtype) for a in ps],
                 [pltpu.SemaphoreType.DMA((n, 3)), pltpu.SemaphoreType.DMA((n, 3)), pltpu.SemaphoreType.DMA((n,))], start, finish)


def _pcall(body, name, args, in_specs, out_shape, out_specs, grid=(), scratch_shapes=(), sem=(), plans=()):
    n_in, n_out, n_scr = len(args), len(out_shape), len(scratch_shapes)
    counts = [(len(p.inputs), len(p.out_shapes), len(p.sem_shapes)) for p in plans]
    c_args = [a for p in plans for a in p.inputs]
    c_outs = [s for p in plans for s in p.out_shapes]
    c_sems = [s for p in plans for s in p.sem_shapes]

    def wrapped(*refs):
        cuts = [n_in, len(c_args), n_out, len(c_outs), n_scr, len(c_sems)]
        ins, c_in, outs, c_out, scr, c_sem = [refs[sum(cuts[:k]):sum(cuts[:k + 1])] for k in range(6)]

        def halves(which):
            a = b = s = 0
            for p, (na, nb, ns) in zip(plans, counts):
                getattr(p, which)(c_in[a:a + na], c_out[b:b + nb], c_sem[s:s + ns])
                a, b, s = a + na, b + nb, s + ns

        if not plans:
            body(*ins, *outs, *scr)
        elif not grid:
            halves("start")
            body(*ins, *outs, *scr)
            halves("finish")
        else:
            first = functools.reduce(jnp.logical_and, [pl.program_id(a) == 0 for a in range(len(grid))])
            last = functools.reduce(jnp.logical_and, [pl.program_id(a) == grid[a] - 1 for a in range(len(grid))])
            pl.when(first)(lambda: halves("start"))
            body(*ins, *outs, *scr)
            pl.when(last)(lambda: halves("finish"))

    res = pl.pallas_call(
        wrapped, name=name, grid=grid, in_specs=list(in_specs) + [ANY] * len(c_args),
        out_shape=list(out_shape) + c_outs, out_specs=list(out_specs) + [ANY] * len(c_outs),
        scratch_shapes=list(scratch_shapes) + c_sems,
        compiler_params=_cp(*(("arbitrary",) * len(grid) if plans else sem)),
    )(*args, *c_args)
    c_res, b = [], n_out
    for _, nb, _ in counts:
        c_res.append(res[b:b + nb])
        b += nb
    return res[:n_out], c_res


def _exchange(plans, name):
    return _pcall(lambda: None, name, [], [], [], [], plans=plans)[1]


def _ada_partial(c_all, w_ada):
    nb, d = c_all.shape
    cols = w_ada.shape[1]

    def body(c_ref, w_ref, act_ref, p_ref):
        cc = c_ref[...]
        act = cc * _sigmoid(cc)
        act_ref[...] = act
        p_ref[...] = _dot(act.astype(BF16), w_ref[...].astype(BF16))

    return pl.pallas_call(body, name="ada_partial", out_shape=[S((nb, d), F32), S((nb, cols), F32)],
                          compiler_params=_cp())(c_all, w_ada)


def _ada_finish(p_all, b_ada):
    _, nb, cols = p_all.shape

    def body(p_ref, b_ref, o_ref):
        me = _place()[3]
        for s in range(NDEV):
            o_ref[:, s * cols:(s + 1) * cols] = p_ref[s, pl.ds(me, 1), :] + b_ref[:, s * cols:(s + 1) * cols]

    return pl.pallas_call(body, name="ada_finish", out_shape=S((1, NDEV * cols), F32), compiler_params=_cp())(p_all, b_ada)


def _ffn_in(x, ada9, nrm, w_in, sh_row, sc_row, name, plans=()):
    t, d = x.shape
    nb, bw = w_in.shape[0] // 2, w_in.shape[1]
    tm = min(ROW_TILE, t)

    def body(x_ref, ada_ref, n_ref, wg_ref, wu_ref, h_ref, gu_ref, s_ref):
        @pl.when(pl.program_id(1) == 0)
        def _():
            _, _, h = _normmod(x_ref[...], n_ref[...], ada_ref[sc_row:sc_row + 1, :], ada_ref[sh_row:sh_row + 1, :])
            h_ref[...] = h.astype(BF16)

        h = h_ref[...]
        g = _dot_nt(h, wg_ref[...])
        u = _dot_nt(h, wu_ref[...])
        gu_ref[0] = g.astype(BF16)
        gu_ref[1] = u.astype(BF16)
        s_ref[...] = (g * _sigmoid(g) * u).astype(BF16)

    return _pcall(
        body, name, [x, ada9, nrm, w_in, w_in], grid=(t // tm, nb),
        in_specs=[pl.BlockSpec((tm, d), lambda i, j: (i, 0)), pl.BlockSpec((N_ADA, d), lambda i, j: (0, 0)),
                  pl.BlockSpec((1, d), lambda i, j: (0, 0)),
                  pl.BlockSpec((None, bw, d), lambda i, j: (j, 0, 0)), pl.BlockSpec((None, bw, d), lambda i, j: (j + nb, 0, 0))],
        out_shape=[S((t, d), BF16), S((2, nb, t, bw), BF16), S((nb, t, bw), BF16)],
        out_specs=[pl.BlockSpec((tm, d), lambda i, j: (i, 0)), pl.BlockSpec((2, None, tm, bw), lambda i, j: (0, j, i, 0)),
                   pl.BlockSpec((None, tm, bw), lambda i, j: (j, i, 0))],
        sem=("parallel", "arbitrary"), plans=plans)


def _ffn_out(s, w_out, x, ada9, g_row, res_scale, name, plans=()):
    nb, t, bw = s.shape
    d = x.shape[1]
    tm = min(ROW_TILE, t)

    def body(s_ref, w_ref, x_ref, ada_ref, xo_ref, f_ref):
        acc = _dot(s_ref[0], w_ref[0])
        for b in range(1, nb):
            acc = acc + _dot(s_ref[b], w_ref[b])
        f_ref[...] = acc.astype(BF16)
        xo_ref[...] = x_ref[...] + (res_scale * ada_ref[g_row:g_row + 1, :]) * acc

    return _pcall(
        body, name, [s, w_out, x, ada9], grid=(t // tm,),
        in_specs=[pl.BlockSpec((nb, tm, bw), lambda i: (0, i, 0)), pl.BlockSpec((nb, bw, d), lambda i: (0, 0, 0)),
                  pl.BlockSpec((tm, d), lambda i: (i, 0)), pl.BlockSpec((N_ADA, d), lambda i: (0, 0))],
        out_shape=[S((t, d), F32), S((t, d), BF16)],
        out_specs=[pl.BlockSpec((tm, d), lambda i: (i, 0)), pl.BlockSpec((tm, d), lambda i: (i, 0))],
        sem=("parallel",), plans=plans)


def _ffn_bwd_ds(dxo, f, ada9, w_out, gu, g_row, res_scale, name, plans=()):
    t, d = dxo.shape
    nb, bw = w_out.shape[0], w_out.shape[1]
    tm = min(ROW_TILE, t)

    def body(dxo_ref, f_ref, ada_ref, w_ref, gu_ref, df_ref, da_ref, dg_ref):
        i, j = pl.program_id(0), pl.program_id(1)

        @pl.when((i == 0) & (j == 0))
        def _():
            dg_ref[...] = jnp.zeros_like(dg_ref)

        @pl.when(j == 0)
        def _():
            dxo_t = dxo_ref[...]
            df_ref[...] = ((res_scale * ada_ref[g_row:g_row + 1, :]) * dxo_t).astype(BF16)
            dg_ref[...] += res_scale * _rsum8(dxo_t * f_ref[...].astype(F32))

        ds = _dot_nt(df_ref[...], w_ref[...])
        g = gu_ref[0].astype(F32)
        u = gu_ref[1].astype(F32)
        sg = _sigmoid(g)
        da_ref[0] = (ds * u * (sg * (1.0 + g * (1.0 - sg)))).astype(BF16)
        da_ref[1] = (ds * (g * sg)).astype(BF16)

    return _pcall(
        body, name, [dxo, f, ada9, w_out, gu], grid=(t // tm, nb),
        in_specs=[pl.BlockSpec((tm, d), lambda i, j: (i, 0)), pl.BlockSpec((tm, d), lambda i, j: (i, 0)),
                  pl.BlockSpec((N_ADA, d), lambda i, j: (0, 0)), pl.BlockSpec((None, bw, d), lambda i, j: (j, 0, 0)),
                  pl.BlockSpec((2, None, tm, bw), lambda i, j: (0, j, i, 0))],
        out_shape=[S((t, d), BF16), S((2, nb, t, bw), BF16), S((8, d), F32)],
        out_specs=[pl.BlockSpec((tm, d), lambda i, j: (i, 0)), pl.BlockSpec((2, None, tm, bw), lambda i, j: (0, j, i, 0)),
                   pl.BlockSpec((8, d), lambda i, j: (0, 0))],
        sem=("arbitrary", "arbitrary"), plans=plans)


def _ffn_bwd_dh(da, w_in, x, dxo, ada9, nrm, sh_row, sc_row, name, plans=()):
    t, d = x.shape
    nb, bw = w_in.shape[0] // 2, w_in.shape[1]
    tm = min(ROW_TILE, t)

    def body(da_ref, wg_ref, wu_ref, x_ref, dxo_ref, ada_ref, n_ref, dx_ref, dsh_ref, dsc_ref, dn_ref, acc_ref):
        i, j = pl.program_id(0), pl.program_id(1)

        @pl.when((i == 0) & (j == 0))
        def _():
            dsh_ref[...] = jnp.zeros_like(dsh_ref)
            dsc_ref[...] = jnp.zeros_like(dsc_ref)
            dn_ref[...] = jnp.zeros_like(dn_ref)

        part = _dot(da_ref[0], wg_ref[...]) + _dot(da_ref[1], wu_ref[...])

        @pl.when(j == 0)
        def _():
            acc_ref[...] = part

        @pl.when(j > 0)
        def _():
            acc_ref[...] += part

        @pl.when(j == nb - 1)
        def _():
            dx, tsh, tsc, tn = _normmod_bwd(acc_ref[...], x_ref[...], n_ref[...], ada_ref[sc_row:sc_row + 1, :])
            dx_ref[...] = dxo_ref[...] + dx
            dsh_ref[...] += _rsum8(tsh)
            dsc_ref[...] += _rsum8(tsc)
            dn_ref[...] += _rsum8(tn)

    vec = pl.BlockSpec((8, d), lambda i, j: (0, 0))
    return _pcall(
        body, name, [da, w_in, w_in, x, dxo, ada9, nrm], grid=(t // tm, nb),
        in_specs=[pl.BlockSpec((2, None, tm, bw), lambda i, j: (0, j, i, 0)),
                  pl.BlockSpec((None, bw, d), lambda i, j: (j, 0, 0)), pl.BlockSpec((None, bw, d), lambda i, j: (j + nb, 0, 0)),
                  pl.BlockSpec((tm, d), lambda i, j: (i, 0)), pl.BlockSpec((tm, d), lambda i, j: (i, 0)),
                  pl.BlockSpec((N_ADA, d), lambda i, j: (0, 0)), pl.BlockSpec((1, d), lambda i, j: (0, 0))],
        out_shape=[S((t, d), F32), S((8, d), F32), S((8, d), F32), S((8, d), F32)],
        out_specs=[pl.BlockSpec((tm, d), lambda i, j: (i, 0)), vec, vec, vec],
        scratch_shapes=[pltpu.VMEM((tm, d), F32)],
        sem=("arbitrary", "arbitrary"), plans=plans)


def _tn_matmul(a, b, a_block, a_map, b_block, b_map, out_shape, out_block, out_map, nblk, name, plans=()):
    t = a.shape[-2]
    tk = min(K_TILE, t)
    nk = t // tk

    def body(a_ref, b_ref, o_ref, acc_ref):
        part = _dot_tn(a_ref[...], b_ref[...])

        @pl.when(pl.program_id(1) == 0)
        def _():
            acc_ref[...] = part

        @pl.when(pl.program_id(1) > 0)
        def _():
            acc_ref[...] += part

        @pl.when(pl.program_id(1) == nk - 1)
        def _():
            o_ref[...] = acc_ref[...].astype(BF16)

    (out,), moved = _pcall(
        body, name, [a, b], grid=(nblk, nk),
        in_specs=[pl.BlockSpec(a_block(tk), a_map), pl.BlockSpec(b_block(tk), b_map)],
        out_shape=[S(out_shape, BF16)], out_specs=[pl.BlockSpec(out_block, out_map)],
        scratch_shapes=[pltpu.VMEM(tuple(n for n in out_block if n is not None), F32)],
        sem=("parallel", "arbitrary"), plans=plans)
    return out, moved


def _ffn_backward(dxo, x_in, h, gu, s, f, ada9, nrm, w_in, w_out, rows, core, name, ds_plans=(), dwin_plans=()):
    sh_row, sc_row, g_row = rows
    nb, t, bw = s.shape
    d = x_in.shape[1]
    (df, da, dg), ds_moved = _ffn_bwd_ds(dxo, f, ada9, w_out, gu, g_row, 0.5, name + "_ds", plans=ds_plans)
    dw_in, dwin_moved = _tn_matmul(
        da.reshape(2 * nb, t, bw), h, lambda tk: (None, tk, bw), lambda sb, k: (sb, k, 0), lambda tk: (tk, d), lambda sb, k: (k, 0),
        (2 * nb, bw, d), (None, bw, d), lambda sb, k: (sb, 0, 0), 2 * nb, name + "_dwin",
        plans=dwin_plans(ds_moved) if callable(dwin_plans) else dwin_plans)
    dw_out, ((half_in,),) = _tn_matmul(
        s, df, lambda tk: (None, tk, bw), lambda sb, k: (sb, k, 0), lambda tk: (tk, d), lambda sb, k: (k, 0),
        (nb, bw, d), (None, bw, d), lambda sb, k: (sb, 0, 0), nb, name + "_dwout", plans=[_plan_sibling_swap([dw_in])])
    dw_out = dw_out.reshape(NDEV, -1, d)
    sum_in = _pair_add(dw_in, half_in, core, name + "_dwin_add")
    (dx, dsh, dsc, dn), ((recv_in,), (half_out,)) = _ffn_bwd_dh(
        da, w_in, x_in, dxo, ada9, nrm, sh_row, sc_row, name + "_dh", plans=[_plan_chip_swap([sum_in]), _plan_sibling_swap([dw_out])])
    sum_out = _pair_add(dw_out, half_out, core, name + "_dwout_add")
    return dx, recv_in, sum_out, (dsh, dsc, dg, dn), ds_moved, dwin_moved


def _mix_in(x, ada9, nrm, w, widths, dts, name, plans=()):
    t, d = x.shape
    n = w.shape[0]
    tm = min(ROW_TILE, t)
    starts = [sum(widths[:i]) for i in range(len(widths))]

    def body(x_ref, ada_ref, n_ref, w_ref, h_ref, *out_refs):
        _, _, h = _normmod(x_ref[...], n_ref[...], ada_ref[4:5, :], ada_ref[3:4, :])
        hb = h.astype(BF16)
        h_ref[...] = hb
        for o_ref, st, wd in zip(out_refs, starts, widths):
            o_ref[...] = _dot_nt(hb, w_ref[st:st + wd, :]).astype(o_ref.dtype)

    return _pcall(
        body, name, [x, ada9, nrm, w], grid=(t // tm,),
        in_specs=[pl.BlockSpec((tm, d), lambda i: (i, 0)), pl.BlockSpec((N_ADA, d), lambda i: (0, 0)),
                  pl.BlockSpec((1, d), lambda i: (0, 0)), pl.BlockSpec((n, d), lambda i: (0, 0))],
        out_shape=[S((t, d), BF16)] + [S((t, wd), dt) for wd, dt in zip(widths, dts)],
        out_specs=[pl.BlockSpec((tm, d), lambda i: (i, 0))] + [pl.BlockSpec((tm, wd), lambda i: (i, 0)) for wd in widths],
        sem=("parallel",), plans=plans)


def _tri(lower):
    r = lax.broadcasted_iota(jnp.int32, (CHUNK, CHUNK), 0)
    c = lax.broadcasted_iota(jnp.int32, (CHUNK, CHUNK), 1)
    return (r >= c) if lower else (c >= r)


def _dot_01(m, x):
    hi = x.astype(BF16)
    r1 = x - hi.astype(F32)
    mid = r1.astype(BF16)
    lo = (r1 - mid.astype(F32)).astype(BF16)
    return _dot(m, hi) + _dot(m, mid) + _dot(m, lo)


def _gla_chunk_terms(q, k, lg, low01):
    b = _dot_01(low01, lg)
    bl = b[CHUNK - 1:CHUNK, :]
    r = 0.5 * bl
    eb, ebl, em, en = jnp.exp(b), jnp.exp(bl - b), jnp.exp(b - r), jnp.exp(r - b)
    return eb, ebl, em, en, jnp.exp(bl), q * eb, k * ebl, q * em, k * en


def _scores(qm_h, knp, qk1_h):
    r = lax.broadcasted_iota(jnp.int32, (CHUNK, CHUNK), 0)
    c = lax.broadcasted_iota(jnp.int32, (CHUNK, CHUNK), 1)
    p = jnp.where(r > c, _dot_nt(qm_h, knp), 0.0)
    return jnp.where(r == c, jnp.sum(qk1_h, axis=1, keepdims=True), p)


def _gla_fwd(qk, v, gl, wg, bg, heads, name, plans=()):
    t = qk.shape[0]
    kw, vw = qk.shape[1] // 2, v.shape[1]
    dk, dv = kw // heads, vw // heads
    assert dk == 64 and dv == 128 and kw % 128 == 0
    gt = min(ROW_TILE, t)
    nc = gt // CHUNK
    scale = dk ** -0.5

    def body(qk_ref, v_ref, gl_ref, wg_ref, bg_ref, o_ref, lg_ref, sall_ref, st_ref):
        @pl.when(pl.program_id(0) == 0)
        def _():
            st_ref[...] = jnp.zeros_like(st_ref)

        gk = _dot(gl_ref[...].astype(BF16), wg_ref[...]) + bg_ref[...]
        lg_ref[...] = (jnp.minimum(gk, 0.0) - jnp.log(1.0 + jnp.exp(-jnp.abs(gk)))) / GATE_NORMALIZER
        low01 = _tri(True).astype(BF16)
        lane = lax.broadcasted_iota(jnp.int32, (CHUNK, 128), 1)

        def chunk(ci, carry):
            rows = pl.ds(pl.multiple_of(ci * CHUNK, CHUNK), CHUNK)
            q = qk_ref[rows, 0:kw] * scale
            k = qk_ref[rows, kw:2 * kw]
            qk1 = q.astype(BF16).astype(F32) * k.astype(BF16).astype(F32)
            eb, ebl, em, en, ebl_row, qe, ke, qm, kn = _gla_chunk_terms(q, k, lg_ref[rows, :], low01)
            for h in range(heads):
                lanes = slice(128 * (h // 2), 128 * (h // 2) + 128)
                own = (lane < 64) if h % 2 == 0 else (lane >= 64)
                knp = kn[:, lanes].astype(BF16)
                qm_h = jnp.where(own, qm[:, lanes], 0.0).astype(BF16)
                qe_h = jnp.where(own, qe[:, lanes], 0.0).astype(BF16)
                ke_h = jnp.where(own, ke[:, lanes], 0.0).astype(BF16)
                v_h = v_ref[rows, h * dv:(h + 1) * dv]
                st = st_ref[h]
                sall_ref[ci, h] = st
                p = _scores(qm_h, knp, jnp.where(own, qk1[:, lanes], 0.0))
                o_ref[rows, h * dv:(h + 1) * dv] = _dot(p.astype(BF16), v_h) + _dot_nt(qe_h, st.astype(BF16))
                st_ref[h] = st * ebl_row[:, lanes] + _dot_tn(v_h, ke_h)
            return carry

        lax.fori_loop(0, nc, chunk, 0, unroll=True)

    return _pcall(
        body, name, [qk, v, gl, wg, bg], grid=(t // gt,),
        in_specs=[pl.BlockSpec((gt, 2 * kw), lambda i: (i, 0)), pl.BlockSpec((gt, vw), lambda i: (i, 0)),
                  pl.BlockSpec((gt, 128), lambda i: (i, 0)), pl.BlockSpec((128, kw), lambda i: (0, 0)),
                  pl.BlockSpec((1, kw), lambda i: (0, 0))],
        out_shape=[S((t, vw), F32), S((t, kw), F32), S((t // CHUNK, heads, dv, 128), F32)],
        out_specs=[pl.BlockSpec((gt, vw), lambda i: (i, 0)), pl.BlockSpec((gt, kw), lambda i: (i, 0)),
                   pl.BlockSpec((nc, heads, dv, 128), lambda i: (i, 0, 0, 0))],
        scratch_shapes=[pltpu.VMEM((heads, dv, 128), F32)],
        sem=("arbitrary",), plans=plans)


def _gla_bwd(qk, v, lg, do, sall, gl, wg, heads, name):
    t = qk.shape[0]
    kw, vw = qk.shape[1] // 2, v.shape[1]
    dk, dv = kw // heads, vw // heads
    gt = min(ROW_TILE, t)
    nc = gt // CHUNK
    nt = t // gt
    scale = dk ** -0.5

    def body(qk_ref, v_ref, lg_ref, do_ref, sall_ref, gl_ref, wg_ref, dqk_ref, dv_ref, dgl_ref, dwg_ref, dbg_ref, dst_ref, dgk_ref):
        @pl.when(pl.program_id(0) == 0)
        def _():
            dst_ref[...] = jnp.zeros_like(dst_ref)
            dwg_ref[...] = jnp.zeros_like(dwg_ref)
            dbg_ref[...] = jnp.zeros_like(dbg_ref)

        low01 = _tri(True).astype(BF16)
        up01 = _tri(False).astype(BF16)
        causal = _tri(True)
        lane = lax.broadcasted_iota(jnp.int32, (CHUNK, 128), 1)
        last_row = lax.broadcasted_iota(jnp.int32, (CHUNK, kw), 0) == CHUNK - 1

        def chunk(cj, carry):
            ci = nc - 1 - cj
            rows = pl.ds(pl.multiple_of(ci * CHUNK, CHUNK), CHUNK)
            q = qk_ref[rows, 0:kw] * scale
            k = qk_ref[rows, kw:2 * kw]
            qk1 = q.astype(BF16).astype(F32) * k.astype(BF16).astype(F32)
            lgc = lg_ref[rows, :]
            eb, ebl, em, en, ebl_row, qe, ke, qm, kn = _gla_chunk_terms(q, k, lgc, low01)
            dqe, dqm, dkn, dke, drow = [], [], [], [], []
            for pr in range(kw // 128):
                lanes = slice(128 * pr, 128 * pr + 128)
                knp = kn[:, lanes].astype(BF16)
                parts = []
                for half in range(2):
                    h = 2 * pr + half
                    own = (lane < 64) if half == 0 else (lane >= 64)
                    qm_h = jnp.where(own, qm[:, lanes], 0.0).astype(BF16)
                    qe_h = jnp.where(own, qe[:, lanes], 0.0).astype(BF16)
                    ke_h = jnp.where(own, ke[:, lanes], 0.0).astype(BF16)
                    v_h = v_ref[rows, h * dv:(h + 1) * dv]
                    do_h = do_ref[rows, h * dv:(h + 1) * dv]
                    st = sall_ref[ci, h]
                    dst = dst_ref[h]
                    stb, dstb = st.astype(BF16), dst.astype(BF16)
                    p = _scores(qm_h, knp, jnp.where(own, qk1[:, lanes], 0.0)).astype(BF16)
                    dp = jnp.where(causal, _dot_nt(do_h, v_h), 0.0).astype(BF16)
                    dv_ref[rows, h * dv:(h + 1) * dv] = (_dot_tn(p, do_h) + _dot_nt(ke_h, dstb)).astype(BF16)
                    parts.append((jnp.where(own, _dot(dp, knp), 0.0), _dot_tn(dp, qm_h), _dot(do_h, stb), _dot(v_h, dstb),
                                  jnp.sum(st * dst, axis=0, keepdims=True)))
                    dst_ref[h] = dst * ebl_row[:, lanes] + _dot_tn(do_h, qe_h)
                dqm.append(parts[0][0] + parts[1][0])
                dkn.append(parts[0][1] + parts[1][1])
                dqe.append(parts[0][2] + parts[1][2])
                dke.append(parts[0][3] + parts[1][3])
                drow.append(parts[0][4] + parts[1][4])
            dqm, dkn, dqe, dke, drow = [jnp.concatenate(a, axis=1) for a in (dqm, dkn, dqe, dke, drow)]
            dqk_ref[rows, 0:kw] = ((dqe * eb + dqm * em) * scale).astype(BF16)
            dqk_ref[rows, kw:2 * kw] = (dke * ebl + dkn * en).astype(BF16)
            tke = dke * ke
            db = dqe * qe + dqm * qm - dkn * kn - tke
            dbl = jnp.sum(tke, axis=0, keepdims=True) + drow * ebl_row
            db = db + jnp.where(last_row, dbl, 0.0)
            dlg = _dot_01(up01, db)
            dgk_ref[rows, :] = dlg * ((1.0 - jnp.exp(GATE_NORMALIZER * lgc)) / GATE_NORMALIZER)
            return carry

        lax.fori_loop(0, nc, chunk, 0, unroll=True)
        dgk = dgk_ref[...]
        dgkb = dgk.astype(BF16)
        dgl_ref[...] = _dot_nt(dgkb, wg_ref[...]).astype(BF16)
        dwg_ref[...] += _dot_tn(gl_ref[...].astype(BF16), dgkb)
        dbg_ref[...] += _rsum8(dgk)

    rev = lambda i: (nt - 1 - i, 0)
    return pl.pallas_call(
        body, name=name, grid=(nt,),
        in_specs=[pl.BlockSpec((gt, 2 * kw), rev), pl.BlockSpec((gt, vw), rev), pl.BlockSpec((gt, kw), rev),
                  pl.BlockSpec((gt, vw), rev), pl.BlockSpec((nc, heads, dv, 128), lambda i: (nt - 1 - i, 0, 0, 0)),
                  pl.BlockSpec((gt, 128), rev), pl.BlockSpec((128, kw), lambda i: (0, 0))],
        out_shape=[S((t, 2 * kw), BF16), S((t, vw), BF16), S((t, 128), BF16), S((128, kw), F32), S((8, kw), F32)],
        out_specs=[pl.BlockSpec((gt, 2 * kw), rev), pl.BlockSpec((gt, vw), rev), pl.BlockSpec((gt, 128), rev),
                   pl.BlockSpec((128, kw), lambda i: (0, 0)), pl.BlockSpec((8, kw), lambda i: (0, 0))],
        scratch_shapes=[pltpu.VMEM((heads, dv, 128), F32), pltpu.VMEM((gt, kw), F32)],
        compiler_params=_cp("arbitrary"),
    )(qk, v, lg, do, sall, gl, wg)


def _conv_taps(cx_ref, halo_ref, first, cw):
    tm = cx_ref.shape[0]
    u = cx_ref[:, cw:2 * cw].astype(F32) * cx_ref[:, 2 * cw:3 * cw].astype(F32)
    uh = halo_ref[:, cw:2 * cw].astype(F32) * halo_ref[:, 2 * cw:3 * cw].astype(F32)
    uh = jnp.where(first, 0.0, uh)
    row = lax.broadcasted_iota(jnp.int32, (tm, cw), 0)
    u1 = jnp.where(row == 0, uh[15:16, :], pltpu.roll(u, 1, 0))
    u2 = jnp.where(row == 0, uh[14:15, :], jnp.where(row == 1, uh[15:16, :], pltpu.roll(u, 2, 0)))
    return u, u1, u2


def _head_norm(o_h, gn):
    rstd = lax.rsqrt(jnp.mean(o_h * o_h, axis=-1, keepdims=True) + EPS)
    ohat = o_h * rstd
    return ohat, rstd, ohat * gn


def _mix_out(cx, o, go, conv_w, gn, w_out, x, ada9, heads, name):
    t, d = x.shape
    cw, vw = conv_w.shape[1], o.shape[1]
    dv = vw // heads
    tm = min(ROW_TILE, t)

    def body(cx_ref, halo_ref, o_ref, go_ref, cwt_ref, gn_ref, w_ref, x_ref, ada_ref, xo_ref, m_ref, y_ref):
        u, u1, u2 = _conv_taps(cx_ref, halo_ref, pl.program_id(0) == 0, cw)
        yc = cwt_ref[0:1, :] * u2 + cwt_ref[1:2, :] * u1 + cwt_ref[2:3, :] * u
        y_ref[:, 0:cw] = (cx_ref[:, 0:cw].astype(F32) * yc).astype(BF16)
        for h in range(heads):
            cols = slice(h * dv, (h + 1) * dv)
            _, _, on = _head_norm(o_ref[:, cols], gn_ref[...])
            g = go_ref[:, cols].astype(F32)
            y_ref[:, cw + h * dv:cw + (h + 1) * dv] = (on * (g * _sigmoid(g))).astype(BF16)
        m = _dot(y_ref[...], w_ref[...])
        m_ref[...] = m.astype(BF16)
        xo_ref[...] = x_ref[...] + ada_ref[5:6, :] * m

    return pl.pallas_call(
        body, name=name, grid=(t // tm,),
        in_specs=[pl.BlockSpec((tm, 3 * cw), lambda i: (i, 0)),
                  pl.BlockSpec((16, 3 * cw), lambda i: (jnp.maximum(i * (tm // 16) - 1, 0), 0)),
                  pl.BlockSpec((tm, vw), lambda i: (i, 0)), pl.BlockSpec((tm, vw), lambda i: (i, 0)),
                  pl.BlockSpec((3, cw), lambda i: (0, 0)), pl.BlockSpec((1, dv), lambda i: (0, 0)),
                  pl.BlockSpec((cw + vw, d), lambda i: (0, 0)), pl.BlockSpec((tm, d), lambda i: (i, 0)),
                  pl.BlockSpec((N_ADA, d), lambda i: (0, 0))],
        out_shape=[S((t, d), F32), S((t, d), BF16), S((t, cw + vw), BF16)],
        out_specs=[pl.BlockSpec((tm, d), lambda i: (i, 0)), pl.BlockSpec((tm, d), lambda i: (i, 0)),
                   pl.BlockSpec((tm, cw + vw), lambda i: (i, 0))],
        compiler_params=_cp("parallel"),
    )(cx, cx, o, go, conv_w, gn, w_out, x, ada9)


def _mix_bwd_a(dxo, m, ada9, w_out, cx, o, go, conv_w, gn, heads, name, plans=()):
    t, d = dxo.shape
    cw, vw = conv_w.shape[1], o.shape[1]
    dv = vw // heads
    tm = min(ROW_TILE, t)

    def body(dxo_ref, m_ref, ada_ref, w_ref, cx_ref, halo_ref, o_ref, go_ref, cwt_ref, gn_ref,
             dm_ref, dyc_ref, dcb_ref, do_ref, dgo_ref, dg_ref, dcw_ref, dgn_ref):
        @pl.when(pl.program_id(0) == 0)
        def _():
            dg_ref[...] = jnp.zeros_like(dg_ref)
            dcw_ref[...] = jnp.zeros_like(dcw_ref)
            dgn_ref[...] = jnp.zeros_like(dgn_ref)

        dxo_t = dxo_ref[...]
        dmb = (ada_ref[5:6, :] * dxo_t).astype(BF16)
        dm_ref[...] = dmb
        dg_ref[...] += _rsum8(dxo_t * m_ref[...].astype(F32))
        dy = _dot_nt(dmb, w_ref[...])
        u, u1, u2 = _conv_taps(cx_ref, halo_ref, pl.program_id(0) == 0, cw)
        yc = cwt_ref[0:1, :] * u2 + cwt_ref[1:2, :] * u1 + cwt_ref[2:3, :] * u
        dyv = dy[:, 0:cw]
        dcb_ref[...] = (dyv * yc).astype(BF16)
        dyc = dyv * cx_ref[:, 0:cw].astype(F32)
        dyc_ref[...] = dyc
        dcw_ref[0] += _rsum8(dyc * u2)
        dcw_ref[1] += _rsum8(dyc * u1)
        dcw_ref[2] += _rsum8(dyc * u)
        for h in range(heads):
            cols = slice(h * dv, (h + 1) * dv)
            ohat, rstd, on = _head_norm(o_ref[:, cols], gn_ref[...])
            g = go_ref[:, cols].astype(F32)
            sg = _sigmoid(g)
            dyg = dy[:, cw + h * dv:cw + (h + 1) * dv]
            dgo_ref[:, cols] = (dyg * on * (sg * (1.0 + g * (1.0 - sg)))).astype(BF16)
            don = dyg * (g * sg)
            dgn_ref[...] += _rsum8(don * ohat)
            tt = don * gn_ref[...]
            do_ref[:, cols] = (rstd * (tt - ohat * jnp.mean(tt * ohat, axis=-1, keepdims=True))).astype(BF16)

    return _pcall(
        body, name, [dxo, m, ada9, w_out, cx, cx, o, go, conv_w, gn], grid=(t // tm,),
        in_specs=[pl.BlockSpec((tm, d), lambda i: (i, 0)), pl.BlockSpec((tm, d), lambda i: (i, 0)),
                  pl.BlockSpec((N_ADA, d), lambda i: (0, 0)), pl.BlockSpec((cw + vw, d), lambda i: (0, 0)),
                  pl.BlockSpec((tm, 3 * cw), lambda i: (i, 0)),
                  pl.BlockSpec((16, 3 * cw), lambda i: (jnp.maximum(i * (tm // 16) - 1, 0), 0)),
                  pl.BlockSpec((tm, vw), lambda i: (i, 0)), pl.BlockSpec((tm, vw), lambda i: (i, 0)),
                  pl.BlockSpec((3, cw), lambda i: (0, 0)), pl.BlockSpec((1, dv), lambda i: (0, 0))],
        out_shape=[S((t, d), BF16), S((t, cw), F32), S((t, cw), BF16), S((t, vw), BF16), S((t, vw), BF16),
                   S((8, d), F32), S((3, 8, cw), F32), S((8, dv), F32)],
        out_specs=[pl.BlockSpec((tm, d), lambda i: (i, 0)), pl.BlockSpec((tm, cw), lambda i: (i, 0)),
                   pl.BlockSpec((tm, cw), lambda i: (i, 0)), pl.BlockSpec((tm, vw), lambda i: (i, 0)),
                   pl.BlockSpec((tm, vw), lambda i: (i, 0)), pl.BlockSpec((8, d), lambda i: (0, 0)),
                   pl.BlockSpec((3, 8, cw), lambda i: (0, 0, 0)), pl.BlockSpec((8, dv), lambda i: (0, 0))],
        sem=("arbitrary",), plans=plans)


def _mix_bwd_b(dyc, cx, dcb, dqk, dvv, dgo, dgl, conv_w, w, x, dxo, ada9, nrm, name):
    t, d = x.shape
    cw = conv_w.shape[1]
    n = w.shape[0]
    tm = min(ROW_TILE, t)
    nt = t // tm
    pieces = [dcb.shape[1], cw, cw, dqk.shape[1], dvv.shape[1], dgo.shape[1], dgl.shape[1]]
    assert sum(pieces) == n

    def body(dyc_ref, nxt_ref, cx_ref, dcb_ref, dqk_ref, dv_ref, dgo_ref, dgl_ref, cwt_ref, w_ref, x_ref, dxo_ref, ada_ref, n_ref,
             dx_ref, dp_ref, dsh_ref, dsc_ref, dn_ref):
        i = pl.program_id(0)

        @pl.when(i == 0)
        def _():
            dsh_ref[...] = jnp.zeros_like(dsh_ref)
            dsc_ref[...] = jnp.zeros_like(dsc_ref)
            dn_ref[...] = jnp.zeros_like(dn_ref)

        dyc_t = dyc_ref[...]
        nxt = jnp.where(i == nt - 1, 0.0, nxt_ref[...])
        row = lax.broadcasted_iota(jnp.int32, (tm, cw), 0)
        d1 = jnp.where(row == tm - 1, nxt[0:1, :], pltpu.roll(dyc_t, tm - 1, 0))
        d2 = jnp.where(row == tm - 2, nxt[0:1, :], jnp.where(row == tm - 1, nxt[1:2, :], pltpu.roll(dyc_t, tm - 2, 0)))
        du = cwt_ref[2:3, :] * dyc_t + cwt_ref[1:2, :] * d1 + cwt_ref[0:1, :] * d2
        c0 = 0
        dp_ref[:, c0:c0 + cw] = dcb_ref[...]
        dp_ref[:, cw:2 * cw] = (du * cx_ref[:, 2 * cw:3 * cw].astype(F32)).astype(BF16)
        dp_ref[:, 2 * cw:3 * cw] = (du * cx_ref[:, cw:2 * cw].astype(F32)).astype(BF16)
        c0 = 3 * cw
        for ref in (dqk_ref, dv_ref, dgo_ref, dgl_ref):
            wd = ref.shape[1]
            dp_ref[:, c0:c0 + wd] = ref[...]
            c0 += wd
        dh = _dot(dp_ref[...], w_ref[...])
        dx, tsh, tsc, tn = _normmod_bwd(dh, x_ref[...], n_ref[...], ada_ref[4:5, :])
        dx_ref[...] = dxo_ref[...] + dx
        dsh_ref[...] += _rsum8(tsh)
        dsc_ref[...] += _rsum8(tsc)
        dn_ref[...] += _rsum8(tn)

    row_spec = lambda wd: pl.BlockSpec((tm, wd), lambda i: (i, 0))
    vec = pl.BlockSpec((8, d), lambda i: (0, 0))
    return pl.pallas_call(
        body, name=name, grid=(nt,),
        in_specs=[row_spec(cw), pl.BlockSpec((8, cw), lambda i: (jnp.minimum((i + 1) * (tm // 8), t // 8 - 1), 0)),
                  row_spec(3 * cw), row_spec(cw), row_spec(dqk.shape[1]), row_spec(dvv.shape[1]), row_spec(dgo.shape[1]),
                  row_spec(dgl.shape[1]), pl.BlockSpec((3, cw), lambda i: (0, 0)), pl.BlockSpec((n, d), lambda i: (0, 0)),
                  row_spec(d), row_spec(d), pl.BlockSpec((N_ADA, d), lambda i: (0, 0)), pl.BlockSpec((1, d), lambda i: (0, 0))],
        out_shape=[S((t, d), F32), S((t, n), BF16), S((8, d), F32), S((8, d), F32), S((8, d), F32)],
        out_specs=[row_spec(d), row_spec(n), vec, vec, vec],
        compiler_params=_cp("arbitrary"),
    )(dyc, dyc, cx, dcb, dqk, dvv, dgo, dgl, conv_w, w, x, dxo, ada9, nrm)


def _loss_head(x, target, nrm, name):
    t, d = x.shape
    tm = min(ROW_TILE, t)
    nt = t // tm

    def body(x_ref, tg_ref, n_ref, loss_ref, dx_ref, dn_ref, acc_ref):
        i = pl.program_id(0)

        @pl.when(i == 0)
        def _():
            acc_ref[...] = jnp.zeros_like(acc_ref)
            dn_ref[...] = jnp.zeros_like(dn_ref)

        xt = x_ref[...]
        rstd = lax.rsqrt(jnp.mean(xt * xt, axis=-1, keepdims=True) + EPS)
        xhat = xt * rstd
        err = xhat * n_ref[...] - tg_ref[...]
        acc_ref[...] += _rsum8(err * err)
        dy = err * (1.0 / d)
        dn_ref[...] += _rsum8(dy * xhat)
        dxhat = dy * n_ref[...]
        dx_ref[...] = rstd * (dxhat - xhat * jnp.mean(dxhat * xhat, axis=-1, keepdims=True))

        @pl.when(i == nt - 1)
        def _():
            loss_ref[...] = jnp.full(loss_ref.shape, (0.5 / d) * jnp.sum(acc_ref[...]), F32)

    return pl.pallas_call(
        body, name=name, grid=(nt,),
        in_specs=[pl.BlockSpec((tm, d), lambda i: (i, 0)), pl.BlockSpec((tm, d), lambda i: (i, 0)),
                  pl.BlockSpec((1, d), lambda i: (0, 0))],
        out_shape=[S((1, 128), F32), S((t, d), F32), S((8, d), F32)],
        out_specs=[pl.BlockSpec((1, 128), lambda i: (0, 0)), pl.BlockSpec((tm, d), lambda i: (i, 0)),
                   pl.BlockSpec((8, d), lambda i: (0, 0))],
        scratch_shapes=[pltpu.VMEM((8, d), F32)],
        compiler_params=_cp("arbitrary"),
    )(x, target, nrm)


def _pack_smalls(vec_parts, dcw, dbg, dgn, dwg, rank, name):
    d = vec_parts[0].shape[1]
    cw, kw, dv = dcw.shape[2], dbg.shape[1], dgn.shape[1]
    nv = len(vec_parts)
    assert 2 * cw == d and cw + kw + dv <= d and (rank * kw) % d == 0 and nv + 2 + rank * kw // d <= PACK_ROWS
    per_row = d // kw

    def body(*refs):
        vrefs, (dcw_ref, dbg_ref, dgn_ref, dwg_ref, o_ref) = refs[:nv], refs[nv:]
        o_ref[...] = jnp.zeros_like(o_ref)
        for r, ref in enumerate(vrefs):
            o_ref[r:r + 1, :] = jnp.sum(ref[...], axis=0, keepdims=True)
        o_ref[nv:nv + 1, 0:cw] = jnp.sum(dcw_ref[0], axis=0, keepdims=True)
        o_ref[nv:nv + 1, cw:2 * cw] = jnp.sum(dcw_ref[1], axis=0, keepdims=True)
        o_ref[nv + 1:nv + 2, 0:cw] = jnp.sum(dcw_ref[2], axis=0, keepdims=True)
        o_ref[nv + 1:nv + 2, cw:cw + kw] = jnp.sum(dbg_ref[...], axis=0, keepdims=True)
        o_ref[nv + 1:nv + 2, cw + kw:cw + kw + dv] = jnp.sum(dgn_ref[...], axis=0, keepdims=True)
        for r in range(rank):
            o_ref[nv + 2 + r // per_row:nv + 3 + r // per_row, (r % per_row) * kw:(r % per_row + 1) * kw] = dwg_ref[r:r + 1, :]

    return pl.pallas_call(body, name=name, out_shape=S((PACK_ROWS, d), F32), compiler_params=_cp())(*vec_parts, dcw, dbg, dgn, dwg)


def _sum_slots(a, name):
    def body(a_ref, o_ref):
        acc = a_ref[0]
        for s in range(1, NDEV):
            acc = acc + a_ref[s]
        o_ref[...] = acc

    return pl.pallas_call(body, name=name, out_shape=S(a.shape[1:], F32), compiler_params=_cp())(a)


def _adamw(w, g, m, v):
    m = ADAM_B1 * m + (1.0 - ADAM_B1) * g
    v = ADAM_B2 * v + (1.0 - ADAM_B2) * (g * g)
    m_hat = m / (1.0 - ADAM_B1 ** ADAM_STEP)
    v_hat = v / (1.0 - ADAM_B2 ** ADAM_STEP)
    return -ADAM_LR * (m_hat / (jnp.sqrt(v_hat) + ADAM_EPS) + ADAM_WD * w), m, v


def _adam_slots(recv, w, m, v, name):
    r, c = w.shape
    slots = recv.shape[0]
    tr = _row_tile(r, c)

    def body(recv_ref, w_ref, m_ref, v_ref, g_ref, d_ref, mo_ref, vo_ref):
        g = recv_ref[0].astype(F32)
        for s in range(1, slots):
            g = g + recv_ref[s].astype(F32)
        g_ref[...] = g
        d_ref[...], mo_ref[...], vo_ref[...] = _adamw(w_ref[...], g, m_ref[...], v_ref[...])

    blk = pl.BlockSpec((tr, c), lambda i: (i, 0))
    return pl.pallas_call(
        body, name=name, grid=(r // tr,),
        in_specs=[pl.BlockSpec((slots, tr, c), lambda i: (0, i, 0)), blk, blk, blk],
        out_shape=[S((r, c), F32)] * 4, out_specs=[blk] * 4, compiler_params=_cp("parallel"),
    )(recv, w, m, v)


def _adam_w_ada(act_t, dada, w, m, v, name):
    r, c = w.shape
    tr = 128
    nb = act_t.shape[1]

    def body(a_ref, da_ref, w_ref, m_ref, v_ref, g_ref, d_ref, mo_ref, vo_ref):
        g = a_ref[:, 0:1] * da_ref[0:1, :]
        for b in range(1, nb):
            g = g + a_ref[:, b:b + 1] * da_ref[b:b + 1, :]
        g_ref[...] = g
        d_ref[...], mo_ref[...], vo_ref[...] = _adamw(w_ref[...], g, m_ref[...], v_ref[...])

    blk = pl.BlockSpec((tr, c), lambda i: (i, 0))
    return pl.pallas_call(
        body, name=name, grid=(r // tr,),
        in_specs=[pl.BlockSpec((tr, nb), lambda i: (i, 0)), pl.BlockSpec((nb, c), lambda i: (0, 0)), blk, blk, blk],
        out_shape=[S((r, c), F32)] * 4, out_specs=[blk] * 4, compiler_params=_cp("parallel"),
    )(act_t, dada, w, m, v)


def _adam_smalls(ws, gs, ms, vs, name):
    n = len(ws)

    def body(*refs):
        w_r, g_r, m_r, v_r = (refs[k * n:(k + 1) * n] for k in range(4))
        d_o, m_o, v_o = (refs[(4 + k) * n:(5 + k) * n] for k in range(3))
        for i in range(n):
            d_o[i][...], m_o[i][...], v_o[i][...] = _adamw(w_r[i][...], g_r[i][...], m_r[i][...], v_r[i][...])

    shapes = [S(w.shape, F32) for w in ws]
    outs = pl.pallas_call(body, name=name, out_shape=shapes * 3, compiler_params=_cp())(*ws, *gs, *ms, *vs)
    return outs[:n], outs[n:2 * n], outs[2 * n:]


def kernel(x, c, w_ada, b_ada, norm_ffn1, w_ffn1_in, w_ffn1_out, norm_mix, w_mix_in, conv_w, w_gk2, b_gk, gla_norm, w_mix_out, norm_ffn2, w_ffn2_in, w_ffn2_out, norm_final, loss_target, m_w_ada, m_b_ada, m_norm_ffn1, m_w_ffn1_in, m_w_ffn1_out, m_norm_mix, m_w_mix_in, m_conv_w, m_w_gk2, m_b_gk, m_gla_norm, m_w_mix_out, m_norm_ffn2, m_w_ffn2_in, m_w_ffn2_out, m_norm_final, v_w_ada, v_b_ada, v_norm_ffn1, v_w_ffn1_in, v_w_ffn1_out, v_norm_mix, v_w_mix_in, v_conv_w, v_w_gk2, v_b_gk, v_gla_norm, v_w_mix_out, v_norm_ffn2, v_w_ffn2_in, v_w_ffn2_out, v_norm_final):
    t, d = x.shape[1], x.shape[2]
    x0, tgt = x[0], loss_target[0]
    rank, kw = w_gk2.shape[1], w_gk2.shape[2] * NDEV
    cw = conv_w.shape[2] * NDEV
    dv = gla_norm.shape[1]
    vw = d - cw
    heads = vw // dv
    mix_cols = w_mix_in.shape[2]
    widths = [3 * cw, 2 * kw, vw, vw, 128]
    n_proj = 3 * cw + 2 * kw + 2 * vw + rank
    assert n_proj == mix_cols * NDEV and rank <= 128
    me = 4 * lax.axis_index("x") + 2 * lax.axis_index("y") + lax.axis_index("c")

    core = lax.axis_index("c").astype(jnp.int32).reshape(1)
    bf = lambda a: a[0].astype(BF16)
    bft = lambda a: a[0].T.astype(BF16)
    nb = NDEV // 2

    ((c_all, w1i, cwt_all, wg_all),) = _exchange([_plan_gather([c, bft(w_ffn1_in), conv_w[0], w_gk2[0]])], "gather_first")
    cwt = cwt_all.transpose(1, 0, 2).reshape(conv_w.shape[1], cw)
    wg = jnp.pad(wg_all.transpose(1, 0, 2).reshape(rank, kw), ((0, 128 - rank), (0, 0))).astype(BF16)

    act_all, p_ada = _ada_partial(c_all.reshape(NDEV, d), w_ada[0])
    ((p_all,),) = _exchange([_plan_all_to_all([p_ada], True)], "gather_ada")
    ada9 = _ada_finish(p_all, b_ada).reshape(N_ADA, d)

    (h1, gu1, s1), ((w1o, wmi),) = _ffn_in(x0, ada9, norm_ffn1, w1i, 0, 1, "ffn1_in", plans=[_plan_gather([bf(w_ffn1_out), bft(w_mix_in)])])
    w1o = w1o.reshape(nb, -1, d)
    wmi = jnp.pad(wmi.reshape(n_proj, d), ((0, sum(widths) - n_proj), (0, 0)))
    (x1, f1), ((wmo,),) = _ffn_out(s1, w1o, x0, ada9, 2, 0.5, "ffn1_out", plans=[_plan_gather([bf(w_mix_out)])])
    wmo = wmo.reshape(cw + vw, d)
    (h2, cx, qk, vv, go, gl), ((w2o,),) = _mix_in(x1, ada9, norm_mix, wmi, widths, [BF16, F32, BF16, BF16, F32], "mix_in",
                                                 plans=[_plan_gather([bf(w_ffn2_out)])])
    w2o = w2o.reshape(nb, -1, d)
    (o, lg, sall), ((w2i,),) = _gla_fwd(qk, vv, gl, wg, b_gk, heads, "gla_fwd", plans=[_plan_gather([bft(w_ffn2_in)])])
    x2, mm, ycat = _mix_out(cx, o, go, cwt, gla_norm, wmo, x1, ada9, heads, "mix_out")
    (h3, gu3, s3), _ = _ffn_in(x2, ada9, norm_ffn2, w2i, 6, 7, "ffn2_in")
    (x3, f3), _ = _ffn_out(s3, w2o, x2, ada9, 8, 0.5, "ffn2_out")
    loss_v, dx3, dnf = _loss_head(x3, tgt, norm_final.reshape(1, d), "loss_head")

    dx2, r2i, sum2o, (dsh3, dsc3, dg3, dn3), _, _ = _ffn_backward(
        dx3, x2, h3, gu3, s3, f3, ada9, norm_ffn2, w2i, w2o, (6, 7, 8), core, "ffn2_bwd")
    (dm, dyc, dcb, do, dgo, dg2, dcw, dgn), ((r2o,),) = _mix_bwd_a(dx2, mm, ada9, wmo, cx, o, go, cwt, gla_norm, heads, "mix_bwd_a",
                                                                 plans=[_plan_chip_swap([sum2o])])
    dqk, dvv, dgl, dwg, dbg = _gla_bwd(qk, vv, lg, do, sall, gl, wg, heads, "gla_bwd")
    dx1, dproj, dsh2, dsc2, dnm = _mix_bwd_b(dyc, cx, dcb, dqk, dvv, dgo, dgl, cwt, wmi, x1, dx2, ada9, norm_mix, "mix_bwd_b")
    n_pad = sum(widths)
    tn = n_pad // 5
    dwmi, _ = _tn_matmul(dproj, h2, lambda tk: (tk, tn), lambda sb, k: (k, sb), lambda tk: (tk, d), lambda sb, k: (k, 0),
                         (n_pad, d), (tn, d), lambda sb, k: (sb, 0), 5, "mix_dwin")
    dwmo, _ = _tn_matmul(ycat, dm, lambda tk: (tk, cw + vw), lambda sb, k: (k, 0), lambda tk: (tk, d), lambda sb, k: (k, 0),
                         (cw + vw, d), (cw + vw, d), lambda sb, k: (0, 0), 1, "mix_dwout")
    dwmi = dwmi[:n_proj].reshape(NDEV, mix_cols, d)
    dwmo = dwmo.reshape(NDEV, -1, d)
    dx0, r1i, sum1o, (dsh1, dsc1, dg1, dn1), _, ((rmi, rmo),) = _ffn_backward(
        dx1, x0, h1, gu1, s1, f1, ada9, norm_ffn1, w1i, w1o, (0, 1, 2), core, "ffn1_bwd",
        ds_plans=[_plan_sibling_swap([dwmi, dwmo])],
        dwin_plans=lambda moved: [_plan_chip_swap([_pair_add(dwmi, moved[0][0], core, "mix_dwin_add"),
                                                   _pair_add(dwmo, moved[0][1], core, "mix_dwout_add")])])
    pack = _pack_smalls([dn1, dnm, dn3, dnf, dsh1, dsc1, dg1, dsh2, dsc2, dg2, dsh3, dsc3, dg3], dcw, dbg, dgn, dwg, rank, "pack_smalls")
    (r1o,), (pack_all,) = _exchange([_plan_chip_swap([sum1o]), _plan_all_to_all([pack], True)], "grads_last")
    tot = _sum_slots(pack_all, "sum_smalls")

    res = {}
    for nm, recv, w, m, v in (("w_ffn1_out", r1o, w_ffn1_out, m_w_ffn1_out, v_w_ffn1_out), ("w_mix_out", rmo, w_mix_out, m_w_mix_out, v_w_mix_out),
                              ("w_ffn2_out", r2o, w_ffn2_out, m_w_ffn2_out, v_w_ffn2_out)):
        res[nm] = [a[None] for a in _adam_slots(recv, w[0], m[0], v[0], "adam_" + nm)]
    for nm, recv, w, m, v in (("w_ffn1_in", r1i, w_ffn1_in, m_w_ffn1_in, v_w_ffn1_in), ("w_mix_in", rmi, w_mix_in, m_w_mix_in, v_w_mix_in),
                              ("w_ffn2_in", r2i, w_ffn2_in, m_w_ffn2_in, v_w_ffn2_in)):
        res[nm] = [a.T[None] for a in _adam_slots(recv, w[0].T, m[0].T, v[0].T, "adam_" + nm)]

    cols_ada = w_ada.shape[2]
    dada_all = pack_all[:, 4:4 + N_ADA, :].reshape(NDEV, N_ADA * d)
    dada_mine = lax.dynamic_slice_in_dim(dada_all, me * cols_ada, cols_ada, axis=1)
    res["w_ada"] = [a[None] for a in _adam_w_ada(act_all.T, dada_mine, w_ada[0], m_w_ada[0], v_w_ada[0], "adam_w_ada")]

    nv = 4 + N_ADA
    g_small = {
        "b_ada": tot[4:nv].reshape(1, N_ADA * d),
        "norm_ffn1": tot[0:1], "norm_mix": tot[1:2], "norm_ffn2": tot[2:3], "norm_final": tot[3:4],
        "conv_w": lax.dynamic_slice_in_dim(
            jnp.concatenate([tot[nv:nv + 1, 0:cw], tot[nv:nv + 1, cw:2 * cw], tot[nv + 1:nv + 2, 0:cw]], axis=0), me * (cw // NDEV), cw // NDEV, axis=1),
        "w_gk2": lax.dynamic_slice_in_dim(tot[nv + 2:nv + 2 + rank * kw // d].reshape(rank, kw), me * (kw // NDEV), kw // NDEV, axis=1),
        "b_gk": tot[nv + 1:nv + 2, cw:cw + kw],
        "gla_norm": tot[nv + 1:nv + 2, cw + kw:cw + kw + dv],
    }
    small = {"b_ada": (b_ada, m_b_ada, v_b_ada), "norm_ffn1": (norm_ffn1, m_norm_ffn1, v_norm_ffn1), "norm_mix": (norm_mix, m_norm_mix, v_norm_mix),
             "norm_ffn2": (norm_ffn2, m_norm_ffn2, v_norm_ffn2), "norm_final": (norm_final, m_norm_final, v_norm_final),
             "conv_w": (conv_w, m_conv_w, v_conv_w), "w_gk2": (w_gk2, m_w_gk2, v_w_gk2), "b_gk": (b_gk, m_b_gk, v_b_gk),
             "gla_norm": (gla_norm, m_gla_norm, v_gla_norm)}
    names = list(small)
    flat = lambda a: a.reshape(-1, a.shape[-1])
    dl, mo, vo = _adam_smalls([flat(small[n][0]) for n in names], [g_small[n] for n in names],
                              [flat(small[n][1]) for n in names], [flat(small[n][2]) for n in names], "adam_smalls")
    for i, n in enumerate(names):
        shp = small[n][0].shape
        res[n] = [g_small[n].reshape(shp), dl[i].reshape(shp), mo[i].reshape(shp), vo[i].reshape(shp)]

    loss = lax.psum(loss_v[0, 0], ("x", "y", "c"))
    order = ["w_ada", "b_ada", "norm_ffn1", "w_ffn1_in", "w_ffn1_out", "norm_mix", "w_mix_in", "conv_w", "w_gk2", "b_gk", "gla_norm",
             "w_mix_out", "norm_ffn2", "w_ffn2_in", "w_ffn2_out", "norm_final"]
    return (loss, dx0[None], *[res[n][0] for n in order], *[res[n][1] for n in order], *[res[n][2] for n in order], *[res[n][3] for n in order])
```

```python
import collections
import functools

import jax
import jax.numpy as jnp
from jax import lax
from jax.experimental import pallas as pl
from jax.experimental.pallas import tpu as pltpu

F32 = jnp.float32
BF16 = jnp.bfloat16
S = jax.ShapeDtypeStruct

NDEV = 8
EPS = 1e-6
GATE_NORMALIZER = 16.0
CHUNK = 64
N_ADA = 9
ADAM_LR, ADAM_B1, ADAM_B2, ADAM_EPS, ADAM_WD, ADAM_STEP = 0.001, 0.9, 0.999, 1e-08, 0.01, 10
V7X_VMEM_LIMIT = 56 * 1024 * 1024
ROW_TILE = 512
WIDE_ROW_TILE = 1024
K_TILE = 1024
PACK_ROWS = 24
ANY = pl.BlockSpec(memory_space=pl.ANY)


def _cp(*sem):
    return pltpu.CompilerParams(dimension_semantics=sem or None, vmem_limit_bytes=V7X_VMEM_LIMIT)


def _dot(a, b):
    return jnp.dot(a, b, preferred_element_type=F32)


def _dot_nt(a, b):
    return lax.dot_general(a, b, (((1,), (1,)), ((), ())), preferred_element_type=F32)


def _dot_tn(a, b):
    return lax.dot_general(a, b, (((0,), (0,)), ((), ())), preferred_element_type=F32)


def _rsum8(a):
    r, c = a.shape
    return jnp.sum(a.reshape(r // 8, 8, c), axis=0)


def _row_tile(r, c):
    for cand in (256, 128, 176, 88, 64, 32, 16, 8):
        if r % cand == 0 and cand * c * 4 <= 1024 * 1024:
            return cand
    return r


def _sigmoid(x):
    return 1.0 / (1.0 + jnp.exp(-x))


def _sigmoid_tanh(x):
    return 0.5 * jnp.tanh(0.5 * x) + 0.5


def _normmod(x, nrm, sc, sh):
    rstd = lax.rsqrt(jnp.mean(x * x, axis=-1, keepdims=True) + EPS)
    xhat = x * rstd
    return xhat, rstd, (xhat * nrm) * (1.0 + sc) + sh


def _normmod_bwd(dh, x, nrm, sc):
    rstd = lax.rsqrt(jnp.mean(x * x, axis=-1, keepdims=True) + EPS)
    xhat = x * rstd
    dxhat = dh * (nrm * (1.0 + sc))
    dx = rstd * (dxhat - xhat * jnp.mean(dxhat * xhat, axis=-1, keepdims=True))
    return dx, dh, dh * (xhat * nrm), dh * ((1.0 + sc) * xhat)


def _place():
    x, y, c = lax.axis_index("x"), lax.axis_index("y"), lax.axis_index("c")
    return x, y, c, 4 * x + 2 * y + c


def _peer(x, y, c, k):
    px = 1 - x if k & 4 else x
    py = 1 - y if k & 2 else y
    pc = 1 - c if k & 1 else c
    return (px, py, pc), 4 * px + 2 * py + pc


def _remote(src, dst, send_sem, recv_sem, peer):
    return pltpu.make_async_remote_copy(src_ref=src, dst_ref=dst, send_sem=send_sem, recv_sem=recv_sem,
                                        device_id=peer, device_id_type=pl.DeviceIdType.MESH)


_Plan = collections.namedtuple("_Plan", "inputs out_shapes sem_shapes start finish")


def _plan_all_to_all(xs, gather):
    n = len(xs)

    def copies(ins, outs, sems, landed):
        send_sems, recv_sems, local_sems = sems
        x, y, c, me = _place()
        local = [pltpu.make_async_copy(ins[i] if gather else ins[i].at[me], outs[i].at[me], local_sems.at[i]) for i in range(n)]
        remote = []
        for k in range(1, NDEV):
            peer, pid = _peer(x, y, c, k)
            for i in range(n):
                remote.append(_remote(ins[i] if gather else ins[i].at[pid], outs[i].at[pid if landed else me],
                                      send_sems.at[i, k - 1], recv_sems.at[i, k - 1], peer))
        return local, remote

    def start(ins, outs, sems):
        local, remote = copies(ins, outs, sems, False)
        for cp in local + remote:
            cp.start()

    def finish(ins, outs, sems):
        local, remote = copies(ins, outs, sems, True)
        for cp in remote + local:
            cp.wait()

    return _Plan(list(xs), [S((NDEV,) + a.shape, a.dtype) if gather else S(a.shape, a.dtype) for a in xs],
                 [pltpu.SemaphoreType.DMA((n, NDEV - 1)), pltpu.SemaphoreType.DMA((n, NDEV - 1)), pltpu.SemaphoreType.DMA((n,))],
                 start, finish)


def _other_chips(x, y):
    return [(1 - x, y), (x, 1 - y), (1 - x, 1 - y)]


def _plan_gather(xs):
    n = len(xs)

    def copies(ins, outs, sems, rest):
        send_sems, recv_sems, local_sems = sems
        x, y, c, me = _place()
        sib, sib_id = (x, y, 1 - c), 4 * x + 2 * y + 1 - c
        chips = _other_chips(x, y)
        local = [pltpu.make_async_copy(ins[i], outs[i].at[me], local_sems.at[i]) for i in range(n)]
        first = [_remote(ins[i], outs[i].at[me], send_sems.at[i, 0], recv_sems.at[i, 0], sib) for i in range(n)]
        first += [_remote(ins[i], outs[i].at[me], send_sems.at[i, 1 + j], recv_sems.at[i, 1 + j], (px, py, c))
                  for j, (px, py) in enumerate(chips) for i in range(n)]
        if not rest:
            return local, first
        from_sibling = [_remote(ins[i], outs[i].at[sib_id], send_sems.at[i, 0], recv_sems.at[i, 0], sib) for i in range(n)]
        arrive, forward = [], []
        for j, (px, py) in enumerate(chips):
            slot = 4 * px + 2 * py
            arrive.append([_remote(ins[i], outs[i].at[slot + c], send_sems.at[i, 1 + j], recv_sems.at[i, 1 + j], (px, py, c)) for i in range(n)])
            forward.append([_remote(outs[i].at[slot + c], outs[i].at[slot + c], send_sems.at[i, 4 + j], recv_sems.at[i, 4 + j], sib) for i in range(n)])
            from_sibling += [_remote(ins[i], outs[i].at[slot + 1 - c], send_sems.at[i, 4 + j], recv_sems.at[i, 4 + j], sib) for i in range(n)]
        return local, first, arrive, forward, from_sibling

    def start(ins, outs, sems):
        local, first = copies(ins, outs, sems, False)
        for cp in local + first:
            cp.start()

    def finish(ins, outs, sems):
        local, first, arrive, forward, from_sibling = copies(ins, outs, sems, True)
        for landed, onward in zip(arrive, forward):
            for cp in landed:
                cp.wait_recv()
            for cp in onward:
                cp.start()
        for cp in from_sibling:
            cp.wait_recv()
        for cp in first + [cp for onward in forward for cp in onward]:
            cp.wait_send()
        for cp in local:
            cp.wait()

    return _Plan(list(xs), [S((NDEV,) + a.shape, a.dtype) for a in xs],
                 [pltpu.SemaphoreType.DMA((n, NDEV - 1)), pltpu.SemaphoreType.DMA((n, NDEV - 1)), pltpu.SemaphoreType.DMA((n,))],
                 start, finish)


def _plan_sibling_swap(gs):
    n = len(gs)

    def copies(ins, outs, sems):
        send_sems, recv_sems = sems
        x, y, c, _ = _place()
        return [_remote(ins[i].at[2 * j + 1 - c], outs[i].at[j], send_sems.at[i, j], recv_sems.at[i, j], (x, y, 1 - c))
                for i in range(n) for j in range(NDEV // 2)]

    def start(ins, outs, sems):
        for cp in copies(ins, outs, sems):
            cp.start()

    def finish(ins, outs, sems):
        for cp in copies(ins, outs, sems):
            cp.wait()

    return _Plan(list(gs), [S((NDEV // 2,) + a.shape[1:], a.dtype) for a in gs],
                 [pltpu.SemaphoreType.DMA((n, NDEV // 2)), pltpu.SemaphoreType.DMA((n, NDEV // 2))], start, finish)


def _pair_add(g, r1, core, name):
    _, r, c = g.shape
    tr = _row_tile(r, c)

    def body(core_ref, g_ref, r_ref, o_ref):
        o_ref[...] = (g_ref[...].astype(F32) + r_ref[...].astype(F32)).astype(BF16)

    return pl.pallas_call(
        body, name=name,
        grid_spec=pltpu.PrefetchScalarGridSpec(
            num_scalar_prefetch=1, grid=(NDEV // 2, r // tr),
            in_specs=[pl.BlockSpec((None, tr, c), lambda j, k, core_ref: (2 * j + core_ref[0], k, 0)),
                      pl.BlockSpec((None, tr, c), lambda j, k, core_ref: (j, k, 0))],
            out_specs=pl.BlockSpec((None, tr, c), lambda j, k, core_ref: (j, k, 0))),
        out_shape=S((NDEV // 2, r, c), BF16), compiler_params=_cp("parallel", "parallel"),
    )(core, g, r1)


def _plan_chip_swap(ps):
    n = len(ps)

    def copies(ins, outs, sems, landed):
        send_sems, recv_sems, local_sems = sems
        x, y, c, _ = _place()
        mine = 2 * x + y
        local = [pltpu.make_async_copy(ins[i].at[mine], outs[i].at[mine], local_sems.at[i]) for i in range(n)]
        remote = [_remote(ins[i].at[2 * px + py], outs[i].at[2 * px + py if landed else mine], send_sems.at[i, j], recv_sems.at[i, j], (px, py, c))
                  for j, (px, py) in enumerate(_other_chips(x, y)) for i in range(n)]
        return local, remote

    def start(ins, outs, sems):
        local, remote = copies(ins, outs, sems, False)
        for cp in local + remote:
            cp.start()

    def finish(ins, outs, sems):
        local, remote = copies(ins, outs, sems, True)
        for cp in remote + local:
            cp.wait()

    return _Plan(list(ps), [S(a.shape, a.dtype) for a in ps],
                 [pltpu.SemaphoreType.DMA((n, 3)), pltpu.SemaphoreType.DMA((n, 3)), pltpu.SemaphoreType.DMA((n,))], start, finish)


def _pcall(body, name, args, in_specs, out_shape, out_specs, grid=(), scratch_shapes=(), sem=(), plans=()):
    n_in, n_out, n_scr = len(args), len(out_shape), len(scratch_shapes)
    counts = [(len(p.inputs), len(p.out_shapes), len(p.sem_shapes)) for p in plans]
    c_args = [a for p in plans for a in p.inputs]
    c_outs = [s for p in plans for s in p.out_shapes]
    c_sems = [s for p in plans for s in p.sem_shapes]

    def wrapped(*refs):
        cuts = [n_in, len(c_args), n_out, len(c_outs), n_scr, len(c_sems)]
        ins, c_in, outs, c_out, scr, c_sem = [refs[sum(cuts[:k]):sum(cuts[:k + 1])] for k in range(6)]

        def halves(which):
            a = b = s = 0
            for p, (na, nb, ns) in zip(plans, counts):
                getattr(p, which)(c_in[a:a + na], c_out[b:b + nb], c_sem[s:s + ns])
                a, b, s = a + na, b + nb, s + ns

        if not plans:
            body(*ins, *outs, *scr)
        elif not grid:
            halves("start")
            body(*ins, *outs, *scr)
            halves("finish")
        else:
            first = functools.reduce(jnp.logical_and, [pl.program_id(a) == 0 for a in range(len(grid))])
            last = functools.reduce(jnp.logical_and, [pl.program_id(a) == grid[a] - 1 for a in range(len(grid))])
            pl.when(first)(lambda: halves("start"))
            body(*ins, *outs, *scr)
            pl.when(last)(lambda: halves("finish"))

    res = pl.pallas_call(
        wrapped, name=name, grid=grid, in_specs=list(in_specs) + [ANY] * len(c_args),
        out_shape=list(out_shape) + c_outs, out_specs=list(out_specs) + [ANY] * len(c_outs),
        scratch_shapes=list(scratch_shapes) + c_sems,
        compiler_params=_cp(*(("arbitrary",) * len(grid) if plans else sem)),
    )(*args, *c_args)
    c_res, b = [], n_out
    for _, nb, _ in counts:
        c_res.append(res[b:b + nb])
        b += nb
    return res[:n_out], c_res


def _exchange(plans, name):
    return _pcall(lambda: None, name, [], [], [], [], plans=plans)[1]


def _ada_partial(c_all, w_ada):
    nb, d = c_all.shape
    cols = w_ada.shape[1]

    def body(c_ref, w_ref, act_ref, p_ref):
        cc = c_ref[...]
        act = cc * _sigmoid(cc)
        act_ref[...] = act
        p_ref[...] = _dot(act.astype(BF16), w_ref[...].astype(BF16))

    return pl.pallas_call(body, name="ada_partial", out_shape=[S((nb, d), F32), S((nb, cols), F32)],
                          compiler_params=_cp())(c_all, w_ada)


def _ada_finish(p_all, b_ada):
    _, nb, cols = p_all.shape

    def body(p_ref, b_ref, o_ref):
        me = _place()[3]
        for s in range(NDEV):
            o_ref[:, s * cols:(s + 1) * cols] = p_ref[s, pl.ds(me, 1), :] + b_ref[:, s * cols:(s + 1) * cols]

    return pl.pallas_call(body, name="ada_finish", out_shape=S((1, NDEV * cols), F32), compiler_params=_cp())(p_all, b_ada)


def _ffn_in(x, ada9, nrm, w_in, sh_row, sc_row, name, plans=()):
    t, d = x.shape
    nb, bw = w_in.shape[0] // 2, w_in.shape[1]
    tm = min(WIDE_ROW_TILE, t)

    def body(x_ref, ada_ref, n_ref, wg_ref, wu_ref, h_ref, gu_ref, s_ref):
        @pl.when(pl.program_id(1) == 0)
        def _():
            _, _, h = _normmod(x_ref[...], n_ref[...], ada_ref[sc_row:sc_row + 1, :], ada_ref[sh_row:sh_row + 1, :])
            h_ref[...] = h.astype(BF16)

        h = h_ref[...]
        g = _dot_nt(h, wg_ref[...])
        u = _dot_nt(h, wu_ref[...])
        gu_ref[0] = g.astype(BF16)
        gu_ref[1] = u.astype(BF16)
        s_ref[...] = (g * _sigmoid(g) * u).astype(BF16)

    return _pcall(
        body, name, [x, ada9, nrm, w_in, w_in], grid=(t // tm, nb),
        in_specs=[pl.BlockSpec((tm, d), lambda i, j: (i, 0)), pl.BlockSpec((N_ADA, d), lambda i, j: (0, 0)),
                  pl.BlockSpec((1, d), lambda i, j: (0, 0)),
                  pl.BlockSpec((None, bw, d), lambda i, j: (j, 0, 0)), pl.BlockSpec((None, bw, d), lambda i, j: (j + nb, 0, 0))],
        out_shape=[S((t, d), BF16), S((2, nb, t, bw), BF16), S((nb, t, bw), BF16)],
        out_specs=[pl.BlockSpec((tm, d), lambda i, j: (i, 0)), pl.BlockSpec((2, None, tm, bw), lambda i, j: (0, j, i, 0)),
                   pl.BlockSpec((None, tm, bw), lambda i, j: (j, i, 0))],
        sem=("parallel", "arbitrary"), plans=plans)


def _ffn_out(s, w_out, x, ada9, g_row, res_scale, name, plans=()):
    nb, t, bw = s.shape
    d = x.shape[1]
    tm = min(ROW_TILE, t)

    def body(s_ref, w_ref, x_ref, ada_ref, xo_ref, f_ref):
        acc = _dot(s_ref[0], w_ref[0])
        for b in range(1, nb):
            acc = acc + _dot(s_ref[b], w_ref[b])
        f_ref[...] = acc.astype(BF16)
        xo_ref[...] = x_ref[...] + (res_scale * ada_ref[g_row:g_row + 1, :]) * acc

    return _pcall(
        body, name, [s, w_out, x, ada9], grid=(t // tm,),
        in_specs=[pl.BlockSpec((nb, tm, bw), lambda i: (0, i, 0)), pl.BlockSpec((nb, bw, d), lambda i: (0, 0, 0)),
                  pl.BlockSpec((tm, d), lambda i: (i, 0)), pl.BlockSpec((N_ADA, d), lambda i: (0, 0))],
        out_shape=[S((t, d), F32), S((t, d), BF16)],
        out_specs=[pl.BlockSpec((tm, d), lambda i: (i, 0)), pl.BlockSpec((tm, d), lambda i: (i, 0))],
        sem=("parallel",), plans=plans)


def _ffn_bwd_ds(dxo, f, ada9, w_out, gu, g_row, res_scale, name, plans=()):
    t, d = dxo.shape
    nb, bw = w_out.shape[0], w_out.shape[1]
    tm = min(WIDE_ROW_TILE, t)

    def body(dxo_ref, f_ref, ada_ref, w_ref, gu_ref, df_ref, da_ref, dg_ref):
        i, j = pl.program_id(0), pl.program_id(1)

        @pl.when((i == 0) & (j == 0))
        def _():
            dg_ref[...] = jnp.zeros_like(dg_ref)

        @pl.when(j == 0)
        def _():
            dxo_t = dxo_ref[...]
            df_ref[...] = ((res_scale * ada_ref[g_row:g_row + 1, :]) * dxo_t).astype(BF16)
            dg_ref[...] += res_scale * _rsum8(dxo_t * f_ref[...].astype(F32))

        ds = _dot_nt(df_ref[...], w_ref[...])
        g = gu_ref[0].astype(F32)
        u = gu_ref[1].astype(F32)
        sg = _sigmoid_tanh(g)
        da_ref[0] = (ds * u * (sg * (1.0 + g * (1.0 - sg)))).astype(BF16)
        da_ref[1] = (ds * (g * sg)).astype(BF16)

    return _pcall(
        body, name, [dxo, f, ada9, w_out, gu], grid=(t // tm, nb),
        in_specs=[pl.BlockSpec((tm, d), lambda i, j: (i, 0)), pl.BlockSpec((tm, d), lambda i, j: (i, 0)),
                  pl.BlockSpec((N_ADA, d), lambda i, j: (0, 0)), pl.BlockSpec((None, bw, d), lambda i, j: (j, 0, 0)),
                  pl.BlockSpec((2, None, tm, bw), lambda i, j: (0, j, i, 0))],
        out_shape=[S((t, d), BF16), S((2, nb, t, bw), BF16), S((8, d), F32)],
        out_specs=[pl.BlockSpec((tm, d), lambda i, j: (i, 0)), pl.BlockSpec((2, None, tm, bw), lambda i, j: (0, j, i, 0)),
                   pl.BlockSpec((8, d), lambda i, j: (0, 0))],
        sem=("arbitrary", "arbitrary"), plans=plans)


def _ffn_bwd_dh(da, w_in, x, dxo, ada9, nrm, sh_row, sc_row, name, plans=()):
    t, d = x.shape
    nb, bw = w_in.shape[0] // 2, w_in.shape[1]
    tm = min(WIDE_ROW_TILE, t)

    def body(da_ref, wg_ref, wu_ref, x_ref, dxo_ref, ada_ref, n_ref, dx_ref, dsh_ref, dsc_ref, dn_ref, acc_ref):
        i, j = pl.program_id(0), pl.program_id(1)

        @pl.when((i == 0) & (j == 0))
        def _():
            dsh_ref[...] = jnp.zeros_like(dsh_ref)
            dsc_ref[...] = jnp.zeros_like(dsc_ref)
            dn_ref[...] = jnp.zeros_like(dn_ref)

        part = _dot(da_ref[0], wg_ref[...]) + _dot(da_ref[1], wu_ref[...])

        @pl.when(j == 0)
        def _():
            acc_ref[...] = part

        @pl.when(j > 0)
        def _():
            acc_ref[...] += part

        @pl.when(j == nb - 1)
        def _():
            for r0 in range(0, tm, min(256, tm)):
                rows = slice(r0, r0 + min(256, tm))
                dx, tsh, tsc, tn = _normmod_bwd(acc_ref[rows, :], x_ref[rows, :], n_ref[...], ada_ref[sc_row:sc_row + 1, :])
                dx_ref[rows, :] = dxo_ref[rows, :] + dx
                dsh_ref[...] += _rsum8(tsh)
                dsc_ref[...] += _rsum8(tsc)
                dn_ref[...] += _rsum8(tn)

    vec = pl.BlockSpec((8, d), lambda i, j: (0, 0))
    return _pcall(
        body, name, [da, w_in, w_in, x, dxo, ada9, nrm], grid=(t // tm, nb),
        in_specs=[pl.BlockSpec((2, None, tm, bw), lambda i, j: (0, j, i, 0)),
                  pl.BlockSpec((None, bw, d), lambda i, j: (j, 0, 0)), pl.BlockSpec((None, bw, d), lambda i, j: (j + nb, 0, 0)),
                  pl.BlockSpec((tm, d), lambda i, j: (i, 0)), pl.BlockSpec((tm, d), lambda i, j: (i, 0)),
                  pl.BlockSpec((N_ADA, d), lambda i, j: (0, 0)), pl.BlockSpec((1, d), lambda i, j: (0, 0))],
        out_shape=[S((t, d), F32), S((8, d), F32), S((8, d), F32), S((8, d), F32)],
        out_specs=[pl.BlockSpec((tm, d), lambda i, j: (i, 0)), vec, vec, vec],
        scratch_shapes=[pltpu.VMEM((tm, d), F32)],
        sem=("arbitrary", "arbitrary"), plans=plans)


def _tn_matmul(a, b, a_block, a_map, b_block, b_map, out_shape, out_block, out_map, nblk, name, plans=()):
    t = a.shape[-2]
    tk = min(K_TILE, t)
    nk = t // tk

    def body(a_ref, b_ref, o_ref, acc_ref):
        k = pl.program_id(1)
        for q in (range(a_ref.shape[0]) if len(a_ref.shape) == 3 else [Ellipsis]):
            part = _dot_tn(a_ref[q], b_ref[...])

            @pl.when(k == 0)
            def _():
                acc_ref[q] = part

            @pl.when(k > 0)
            def _():
                acc_ref[q] += part

        @pl.when(k == nk - 1)
        def _():
            o_ref[...] = acc_ref[...].astype(BF16)

    (out,), moved = _pcall(
        body, name, [a, b], grid=(nblk, nk),
        in_specs=[pl.BlockSpec(a_block(tk), a_map), pl.BlockSpec(b_block(tk), b_map)],
        out_shape=[S(out_shape, BF16)], out_specs=[pl.BlockSpec(out_block, out_map)],
        scratch_shapes=[pltpu.VMEM(tuple(n for n in out_block if n is not None), F32)],
        sem=("parallel", "arbitrary"), plans=plans)
    return out, moved


def _ffn_backward(dxo, x_in, h, gu, s, f, ada9, nrm, w_in, w_out, rows, core, name, ds_plans=(), dwin_plans=()):
    sh_row, sc_row, g_row = rows
    nb, t, bw = s.shape
    d = x_in.shape[1]
    (df, da, dg), ds_moved = _ffn_bwd_ds(dxo, f, ada9, w_out, gu, g_row, 0.5, name + "_ds", plans=ds_plans)
    dw_in, dwin_moved = _tn_matmul(
        da.reshape(2 * nb, t, bw), h, lambda tk: (2, tk, bw), lambda sb, k: (sb, k, 0), lambda tk: (tk, d), lambda sb, k: (k, 0),
        (2 * nb, bw, d), (2, bw, d), lambda sb, k: (sb, 0, 0), nb, name + "_dwin",
        plans=dwin_plans(ds_moved) if callable(dwin_plans) else dwin_plans)
    dw_out, ((half_in,),) = _tn_matmul(
        s, df, lambda tk: (None, tk, bw), lambda sb, k: (sb, k, 0), lambda tk: (tk, d), lambda sb, k: (k, 0),
        (nb, bw, d), (None, bw, d), lambda sb, k: (sb, 0, 0), nb, name + "_dwout", plans=[_plan_sibling_swap([dw_in])])
    dw_out = dw_out.reshape(NDEV, -1, d)
    sum_in = _pair_add(dw_in, half_in, core, name + "_dwin_add")
    (dx, dsh, dsc, dn), ((recv_in,), (half_out,)) = _ffn_bwd_dh(
        da, w_in, x_in, dxo, ada9, nrm, sh_row, sc_row, name + "_dh", plans=[_plan_chip_swap([sum_in]), _plan_sibling_swap([dw_out])])
    sum_out = _pair_add(dw_out, half_out, core, name + "_dwout_add")
    return dx, recv_in, sum_out, (dsh, dsc, dg, dn), ds_moved, dwin_moved


def _mix_in(x, ada9, nrm, w, widths, dts, name, plans=()):
    t, d = x.shape
    n = w.shape[0]
    tm = min(ROW_TILE, t)
    starts = [sum(widths[:i]) for i in range(len(widths))]

    def body(x_ref, ada_ref, n_ref, w_ref, h_ref, *out_refs):
        _, _, h = _normmod(x_ref[...], n_ref[...], ada_ref[4:5, :], ada_ref[3:4, :])
        hb = h.astype(BF16)
        h_ref[...] = hb
        for o_ref, st, wd in zip(out_refs, starts, widths):
            o_ref[...] = _dot_nt(hb, w_ref[st:st + wd, :]).astype(o_ref.dtype)

    return _pcall(
        body, name, [x, ada9, nrm, w], grid=(t // tm,),
        in_specs=[pl.BlockSpec((tm, d), lambda i: (i, 0)), pl.BlockSpec((N_ADA, d), lambda i: (0, 0)),
                  pl.BlockSpec((1, d), lambda i: (0, 0)), pl.BlockSpec((n, d), lambda i: (0, 0))],
        out_shape=[S((t, d), BF16)] + [S((t, wd), dt) for wd, dt in zip(widths, dts)],
        out_specs=[pl.BlockSpec((tm, d), lambda i: (i, 0))] + [pl.BlockSpec((tm, wd), lambda i: (i, 0)) for wd in widths],
        sem=("parallel",), plans=plans)


def _tri(lower):
    r = lax.broadcasted_iota(jnp.int32, (CHUNK, CHUNK), 0)
    c = lax.broadcasted_iota(jnp.int32, (CHUNK, CHUNK), 1)
    return (r >= c) if lower else (c >= r)


def _dot_01(m, x):
    hi = x.astype(BF16)
    r1 = x - hi.astype(F32)
    mid = r1.astype(BF16)
    lo = (r1 - mid.astype(F32)).astype(BF16)
    return _dot(m, hi) + _dot(m, mid) + _dot(m, lo)


def _gla_chunk_terms(q, k, lg, low01):
    b = _dot_01(low01, lg)
    bl = b[CHUNK - 1:CHUNK, :]
    r = 0.5 * bl
    eb, ebl, em, en = jnp.exp(b), jnp.exp(bl - b), jnp.exp(b - r), jnp.exp(r - b)
    return eb, ebl, em, en, jnp.exp(bl), q * eb, k * ebl, q * em, k * en


def _scores(qm_h, knp, qk1_h):
    r = lax.broadcasted_iota(jnp.int32, (CHUNK, CHUNK), 0)
    c = lax.broadcasted_iota(jnp.int32, (CHUNK, CHUNK), 1)
    p = jnp.where(r > c, _dot_nt(qm_h, knp), 0.0)
    return jnp.where(r == c, jnp.sum(qk1_h, axis=1, keepdims=True), p)


def _gla_fwd(qk, v, gl, wg, bg, heads, name, plans=()):
    t = qk.shape[0]
    kw, vw = qk.shape[1] // 2, v.shape[1]
    dk, dv = kw // heads, vw // heads
    assert dk == 64 and dv == 128 and kw % 128 == 0
    gt = min(ROW_TILE, t)
    nc = gt // CHUNK
    scale = dk ** -0.5

    def body(qk_ref, v_ref, gl_ref, wg_ref, bg_ref, o_ref, lg_ref, sall_ref, st_ref):
        @pl.when(pl.program_id(0) == 0)
        def _():
            st_ref[...] = jnp.zeros_like(st_ref)

        gk = _dot(gl_ref[...].astype(BF16), wg_ref[...]) + bg_ref[...]
        lg_ref[...] = (jnp.minimum(gk, 0.0) - jnp.log(1.0 + jnp.exp(-jnp.abs(gk)))) / GATE_NORMALIZER
        low01 = _tri(True).astype(BF16)
        lane = lax.broadcasted_iota(jnp.int32, (CHUNK, 128), 1)

        def chunk(ci, carry):
            rows = pl.ds(pl.multiple_of(ci * CHUNK, CHUNK), CHUNK)
            q = qk_ref[rows, 0:kw] * scale
            k = qk_ref[rows, kw:2 * kw]
            qk1 = q.astype(BF16).astype(F32) * k.astype(BF16).astype(F32)
            eb, ebl, em, en, ebl_row, qe, ke, qm, kn = _gla_chunk_terms(q, k, lg_ref[rows, :], low01)
            for h in range(heads):
                lanes = slice(128 * (h // 2), 128 * (h // 2) + 128)
                own = (lane < 64) if h % 2 == 0 else (lane >= 64)
                knp = kn[:, lanes].astype(BF16)
                qm_h = jnp.where(own, qm[:, lanes], 0.0).astype(BF16)
                qe_h = jnp.where(own, qe[:, lanes], 0.0).astype(BF16)
                ke_h = jnp.where(own, ke[:, lanes], 0.0).astype(BF16)
                v_h = v_ref[rows, h * dv:(h + 1) * dv]
                st = st_ref[h]
                sall_ref[ci, h] = st
                p = _scores(qm_h, knp, jnp.where(own, qk1[:, lanes], 0.0))
                o_ref[rows, h * dv:(h + 1) * dv] = _dot(p.astype(BF16), v_h) + _dot_nt(qe_h, st.astype(BF16))
                st_ref[h] = st * ebl_row[:, lanes] + _dot_tn(v_h, ke_h)
            return carry

        lax.fori_loop(0, nc, chunk, 0, unroll=True)

    return _pcall(
        body, name, [qk, v, gl, wg, bg], grid=(t // gt,),
        in_specs=[pl.BlockSpec((gt, 2 * kw), lambda i: (i, 0)), pl.BlockSpec((gt, vw), lambda i: (i, 0)),
                  pl.BlockSpec((gt, 128), lambda i: (i, 0)), pl.BlockSpec((128, kw), lambda i: (0, 0)),
                  pl.BlockSpec((1, kw), lambda i: (0, 0))],
        out_shape=[S((t, vw), F32), S((t, kw), F32), S((t // CHUNK, heads, dv, 128), F32)],
        out_specs=[pl.BlockSpec((gt, vw), lambda i: (i, 0)), pl.BlockSpec((gt, kw), lambda i: (i, 0)),
                   pl.BlockSpec((nc, heads, dv, 128), lambda i: (i, 0, 0, 0))],
        scratch_shapes=[pltpu.VMEM((heads, dv, 128), F32)],
        sem=("arbitrary",), plans=plans)


def _gla_bwd(qk, v, lg, do, sall, gl, wg, heads, name):
    t = qk.shape[0]
    kw, vw = qk.shape[1] // 2, v.shape[1]
    dk, dv = kw // heads, vw // heads
    gt = min(ROW_TILE, t)
    nc = gt // CHUNK
    nt = t // gt
    scale = dk ** -0.5

    def body(qk_ref, v_ref, lg_ref, do_ref, sall_ref, gl_ref, wg_ref, dqk_ref, dv_ref, dgl_ref, dwg_ref, dbg_ref, dst_ref, dgk_ref):
        @pl.when(pl.program_id(0) == 0)
        def _():
            dst_ref[...] = jnp.zeros_like(dst_ref)
            dwg_ref[...] = jnp.zeros_like(dwg_ref)
            dbg_ref[...] = jnp.zeros_like(dbg_ref)

        low01 = _tri(True).astype(BF16)
        up01 = _tri(False).astype(BF16)
        causal = _tri(True)
        lane = lax.broadcasted_iota(jnp.int32, (CHUNK, 128), 1)
        last_row = lax.broadcasted_iota(jnp.int32, (CHUNK, kw), 0) == CHUNK - 1

        def chunk(cj, carry):
            ci = nc - 1 - cj
            rows = pl.ds(pl.multiple_of(ci * CHUNK, CHUNK), CHUNK)
            q = qk_ref[rows, 0:kw] * scale
            k = qk_ref[rows, kw:2 * kw]
            qk1 = q.astype(BF16).astype(F32) * k.astype(BF16).astype(F32)
            lgc = lg_ref[rows, :]
            eb, ebl, em, en, ebl_row, qe, ke, qm, kn = _gla_chunk_terms(q, k, lgc, low01)
            dqe, dqm, dkn, dke, drow = [], [], [], [], []
            for pr in range(kw // 128):
                lanes = slice(128 * pr, 128 * pr + 128)
                knp = kn[:, lanes].astype(BF16)
                parts = []
                for half in range(2):
                    h = 2 * pr + half
                    own = (lane < 64) if half == 0 else (lane >= 64)
                    qm_h = jnp.where(own, qm[:, lanes], 0.0).astype(BF16)
                    qe_h = jnp.where(own, qe[:, lanes], 0.0).astype(BF16)
                    ke_h = jnp.where(own, ke[:, lanes], 0.0).astype(BF16)
                    v_h = v_ref[rows, h * dv:(h + 1) * dv]
                    do_h = do_ref[rows, h * dv:(h + 1) * dv]
                    st = sall_ref[ci, h]
                    dst = dst_ref[h]
                    stb, dstb = st.astype(BF16), dst.astype(BF16)
                    p = _scores(qm_h, knp, jnp.where(own, qk1[:, lanes], 0.0)).astype(BF16)
                    dp = jnp.where(causal, _dot_nt(do_h, v_h), 0.0).astype(BF16)
                    dv_ref[rows, h * dv:(h + 1) * dv] = (_dot_tn(p, do_h) + _dot_nt(ke_h, dstb)).astype(BF16)
                    parts.append((jnp.where(own, _dot(dp, knp), 0.0), _dot_tn(dp, qm_h), _dot(do_h, stb), _dot(v_h, dstb),
                                  jnp.sum(st * dst, axis=0, keepdims=True)))
                    dst_ref[h] = dst * ebl_row[:, lanes] + _dot_tn(do_h, qe_h)
                dqm.append(parts[0][0] + parts[1][0])
                dkn.append(parts[0][1] + parts[1][1])
                dqe.append(parts[0][2] + parts[1][2])
                dke.append(parts[0][3] + parts[1][3])
                drow.append(parts[0][4] + parts[1][4])
            dqm, dkn, dqe, dke, drow = [jnp.concatenate(a, axis=1) for a in (dqm, dkn, dqe, dke, drow)]
            dqk_ref[rows, 0:kw] = ((dqe * eb + dqm * em) * scale).astype(BF16)
            dqk_ref[rows, kw:2 * kw] = (dke * ebl + dkn * en).astype(BF16)
            tke = dke * ke
            db = dqe * qe + dqm * qm - dkn * kn - tke
            dbl = jnp.sum(tke, axis=0, keepdims=True) + drow * ebl_row
            db = db + jnp.where(last_row, dbl, 0.0)
            dlg = _dot_01(up01, db)
            dgk_ref[rows, :] = dlg * ((1.0 - jnp.exp(GATE_NORMALIZER * lgc)) / GATE_NORMALIZER)
            return carry

        lax.fori_loop(0, nc, chunk, 0, unroll=True)
        dgk = dgk_ref[...]
        dgkb = dgk.astype(BF16)
        dgl_ref[...] = _dot_nt(dgkb, wg_ref[...]).astype(BF16)
        dwg_ref[...] += _dot_tn(gl_ref[...].astype(BF16), dgkb)
        dbg_ref[...] += _rsum8(dgk)

    rev = lambda i: (nt - 1 - i, 0)
    return pl.pallas_call(
        body, name=name, grid=(nt,),
        in_specs=[pl.BlockSpec((gt, 2 * kw), rev), pl.BlockSpec((gt, vw), rev), pl.BlockSpec((gt, kw), rev),
                  pl.BlockSpec((gt, vw), rev), pl.BlockSpec((nc, heads, dv, 128), lambda i: (nt - 1 - i, 0, 0, 0)),
                  pl.BlockSpec((gt, 128), rev), pl.BlockSpec((128, kw), lambda i: (0, 0))],
        out_shape=[S((t, 2 * kw), BF16), S((t, vw), BF16), S((t, 128), BF16), S((128, kw), F32), S((8, kw), F32)],
        out_specs=[pl.BlockSpec((gt, 2 * kw), rev), pl.BlockSpec((gt, vw), rev), pl.BlockSpec((gt, 128), rev),
                   pl.BlockSpec((128, kw), lambda i: (0, 0)), pl.BlockSpec((8, kw), lambda i: (0, 0))],
        scratch_shapes=[pltpu.VMEM((heads, dv, 128), F32), pltpu.VMEM((gt, kw), F32)],
        compiler_params=_cp("arbitrary"),
    )(qk, v, lg, do, sall, gl, wg)


def _conv_taps(cx_ref, halo_ref, first, cw):
    tm = cx_ref.shape[0]
    u = cx_ref[:, cw:2 * cw].astype(F32) * cx_ref[:, 2 * cw:3 * cw].astype(F32)
    uh = halo_ref[:, cw:2 * cw].astype(F32) * halo_ref[:, 2 * cw:3 * cw].astype(F32)
    uh = jnp.where(first, 0.0, uh)
    row = lax.broadcasted_iota(jnp.int32, (tm, cw), 0)
    u1 = jnp.where(row == 0, uh[15:16, :], pltpu.roll(u, 1, 0))
    u2 = jnp.where(row == 0, uh[14:15, :], jnp.where(row == 1, uh[15:16, :], pltpu.roll(u, 2, 0)))
    return u, u1, u2


def _head_norm(o_h, gn):
    rstd = lax.rsqrt(jnp.mean(o_h * o_h, axis=-1, keepdims=True) + EPS)
    ohat = o_h * rstd
    return ohat, rstd, ohat * gn


def _mix_out(cx, o, go, conv_w, gn, w_out, x, ada9, heads, name):
    t, d = x.shape
    cw, vw = conv_w.shape[1], o.shape[1]
    dv = vw // heads
    tm = min(ROW_TILE, t)

    def body(cx_ref, halo_ref, o_ref, go_ref, cwt_ref, gn_ref, w_ref, x_ref, ada_ref, xo_ref, m_ref, y_ref):
        u, u1, u2 = _conv_taps(cx_ref, halo_ref, pl.program_id(0) == 0, cw)
        yc = cwt_ref[0:1, :] * u2 + cwt_ref[1:2, :] * u1 + cwt_ref[2:3, :] * u
        y_ref[:, 0:cw] = (cx_ref[:, 0:cw].astype(F32) * yc).astype(BF16)
        for h in range(heads):
            cols = slice(h * dv, (h + 1) * dv)
            _, _, on = _head_norm(o_ref[:, cols], gn_ref[...])
            g = go_ref[:, cols].astype(F32)
            y_ref[:, cw + h * dv:cw + (h + 1) * dv] = (on * (g * _sigmoid(g))).astype(BF16)
        m = _dot(y_ref[...], w_ref[...])
        m_ref[...] = m.astype(BF16)
        xo_ref[...] = x_ref[...] + ada_ref[5:6, :] * m

    return pl.pallas_call(
        body, name=name, grid=(t // tm,),
        in_specs=[pl.BlockSpec((tm, 3 * cw), lambda i: (i, 0)),
                  pl.BlockSpec((16, 3 * cw), lambda i: (jnp.maximum(i * (tm // 16) - 1, 0), 0)),
                  pl.BlockSpec((tm, vw), lambda i: (i, 0)), pl.BlockSpec((tm, vw), lambda i: (i, 0)),
                  pl.BlockSpec((3, cw), lambda i: (0, 0)), pl.BlockSpec((1, dv), lambda i: (0, 0)),
                  pl.BlockSpec((cw + vw, d), lambda i: (0, 0)), pl.BlockSpec((tm, d), lambda i: (i, 0)),
                  pl.BlockSpec((N_ADA, d), lambda i: (0, 0))],
        out_shape=[S((t, d), F32), S((t, d), BF16), S((t, cw + vw), BF16)],
        out_specs=[pl.BlockSpec((tm, d), lambda i: (i, 0)), pl.BlockSpec((tm, d), lambda i: (i, 0)),
                   pl.BlockSpec((tm, cw + vw), lambda i: (i, 0))],
        compiler_params=_cp("parallel"),
    )(cx, cx, o, go, conv_w, gn, w_out, x, ada9)


def _mix_bwd_a(dxo, m, ada9, w_out, cx, o, go, conv_w, gn, heads, name, plans=()):
    t, d = dxo.shape
    cw, vw = conv_w.shape[1], o.shape[1]
    dv = vw // heads
    tm = min(ROW_TILE, t)

    def body(dxo_ref, m_ref, ada_ref, w_ref, cx_ref, halo_ref, o_ref, go_ref, cwt_ref, gn_ref,
             dm_ref, dyc_ref, dcb_ref, do_ref, dgo_ref, dg_ref, dcw_ref, dgn_ref):
        @pl.when(pl.program_id(0) == 0)
        def _():
            dg_ref[...] = jnp.zeros_like(dg_ref)
            dcw_ref[...] = jnp.zeros_like(dcw_ref)
            dgn_ref[...] = jnp.zeros_like(dgn_ref)

        dxo_t = dxo_ref[...]
        dmb = (ada_ref[5:6, :] * dxo_t).astype(BF16)
        dm_ref[...] = dmb
        dg_ref[...] += _rsum8(dxo_t * m_ref[...].astype(F32))
        dy = _dot_nt(dmb, w_ref[...])
        u, u1, u2 = _conv_taps(cx_ref, halo_ref, pl.program_id(0) == 0, cw)
        yc = cwt_ref[0:1, :] * u2 + cwt_ref[1:2, :] * u1 + cwt_ref[2:3, :] * u
        dyv = dy[:, 0:cw]
        dcb_ref[...] = (dyv * yc).astype(BF16)
        dyc = dyv * cx_ref[:, 0:cw].astype(F32)
        dyc_ref[...] = dyc
        dcw_ref[0] += _rsum8(dyc * u2)
        dcw_ref[1] += _rsum8(dyc * u1)
        dcw_ref[2] += _rsum8(dyc * u)
        for h in range(heads):
            cols = slice(h * dv, (h + 1) * dv)
            ohat, rstd, on = _head_norm(o_ref[:, cols], gn_ref[...])
            g = go_ref[:, cols].astype(F32)
            sg = _sigmoid_tanh(g)
            dyg = dy[:, cw + h * dv:cw + (h + 1) * dv]
            dgo_ref[:, cols] = (dyg * on * (sg * (1.0 + g * (1.0 - sg)))).astype(BF16)
            don = dyg * (g * sg)
            dgn_ref[...] += _rsum8(don * ohat)
            tt = don * gn_ref[...]
            do_ref[:, cols] = (rstd * (tt - ohat * jnp.mean(tt * ohat, axis=-1, keepdims=True))).astype(BF16)

    return _pcall(
        body, name, [dxo, m, ada9, w_out, cx, cx, o, go, conv_w, gn], grid=(t // tm,),
        in_specs=[pl.BlockSpec((tm, d), lambda i: (i, 0)), pl.BlockSpec((tm, d), lambda i: (i, 0)),
                  pl.BlockSpec((N_ADA, d), lambda i: (0, 0)), pl.BlockSpec((cw + vw, d), lambda i: (0, 0)),
                  pl.BlockSpec((tm, 3 * cw), lambda i: (i, 0)),
                  pl.BlockSpec((16, 3 * cw), lambda i: (jnp.maximum(i * (tm // 16) - 1, 0), 0)),
                  pl.BlockSpec((tm, vw), lambda i: (i, 0)), pl.BlockSpec((tm, vw), lambda i: (i, 0)),
                  pl.BlockSpec((3, cw), lambda i: (0, 0)), pl.BlockSpec((1, dv), lambda i: (0, 0))],
        out_shape=[S((t, d), BF16), S((t, cw), F32), S((t, cw), BF16), S((t, vw), BF16), S((t, vw), BF16),
                   S((8, d), F32), S((3, 8, cw), F32), S((8, dv), F32)],
        out_specs=[pl.BlockSpec((tm, d), lambda i: (i, 0)), pl.BlockSpec((tm, cw), lambda i: (i, 0)),
                   pl.BlockSpec((tm, cw), lambda i: (i, 0)), pl.BlockSpec((tm, vw), lambda i: (i, 0)),
                   pl.BlockSpec((tm, vw), lambda i: (i, 0)), pl.BlockSpec((8, d), lambda i: (0, 0)),
                   pl.BlockSpec((3, 8, cw), lambda i: (0, 0, 0)), pl.BlockSpec((8, dv), lambda i: (0, 0))],
        sem=("arbitrary",), plans=plans)


def _mix_bwd_b(dyc, cx, dcb, dqk, dvv, dgo, dgl, conv_w, w, x, dxo, ada9, nrm, name):
    t, d = x.shape
    cw = conv_w.shape[1]
    n = w.shape[0]
    tm = min(ROW_TILE, t)
    nt = t // tm
    pieces = [dcb.shape[1], cw, cw, dqk.shape[1], dvv.shape[1], dgo.shape[1], dgl.shape[1]]
    assert sum(pieces) == n

    def body(dyc_ref, nxt_ref, cx_ref, dcb_ref, dqk_ref, dv_ref, dgo_ref, dgl_ref, cwt_ref, w_ref, x_ref, dxo_ref, ada_ref, n_ref,
             dx_ref, dp_ref, dsh_ref, dsc_ref, dn_ref):
        i = pl.program_id(0)

        @pl.when(i == 0)
        def _():
            dsh_ref[...] = jnp.zeros_like(dsh_ref)
            dsc_ref[...] = jnp.zeros_like(dsc_ref)
            dn_ref[...] = jnp.zeros_like(dn_ref)

        dyc_t = dyc_ref[...]
        nxt = jnp.where(i == nt - 1, 0.0, nxt_ref[...])
        row = lax.broadcasted_iota(jnp.int32, (tm, cw), 0)
        d1 = jnp.where(row == tm - 1, nxt[0:1, :], pltpu.roll(dyc_t, tm - 1, 0))
        d2 = jnp.where(row == tm - 2, nxt[0:1, :], jnp.where(row == tm - 1, nxt[1:2, :], pltpu.roll(dyc_t, tm - 2, 0)))
        du = cwt_ref[2:3, :] * dyc_t + cwt_ref[1:2, :] * d1 + cwt_ref[0:1, :] * d2
        c0 = 0
        dp_ref[:, c0:c0 + cw] = dcb_ref[...]
        dp_ref[:, cw:2 * cw] = (du * cx_ref[:, 2 * cw:3 * cw].astype(F32)).astype(BF16)
        dp_ref[:, 2 * cw:3 * cw] = (du * cx_ref[:, cw:2 * cw].astype(F32)).astype(BF16)
        c0 = 3 * cw
        for ref in (dqk_ref, dv_ref, dgo_ref, dgl_ref):
            wd = ref.shape[1]
            dp_ref[:, c0:c0 + wd] = ref[...]
            c0 += wd
        dh = _dot(dp_ref[...], w_ref[...])
        dx, tsh, tsc, tn = _normmod_bwd(dh, x_ref[...], n_ref[...], ada_ref[4:5, :])
        dx_ref[...] = dxo_ref[...] + dx
        dsh_ref[...] += _rsum8(tsh)
        dsc_ref[...] += _rsum8(tsc)
        dn_ref[...] += _rsum8(tn)

    row_spec = lambda wd: pl.BlockSpec((tm, wd), lambda i: (i, 0))
    vec = pl.BlockSpec((8, d), lambda i: (0, 0))
    return pl.pallas_call(
        body, name=name, grid=(nt,),
        in_specs=[row_spec(cw), pl.BlockSpec((8, cw), lambda i: (jnp.minimum((i + 1) * (tm // 8), t // 8 - 1), 0)),
                  row_spec(3 * cw), row_spec(cw), row_spec(dqk.shape[1]), row_spec(dvv.shape[1]), row_spec(dgo.shape[1]),
                  row_spec(dgl.shape[1]), pl.BlockSpec((3, cw), lambda i: (0, 0)), pl.BlockSpec((n, d), lambda i: (0, 0)),
                  row_spec(d), row_spec(d), pl.BlockSpec((N_ADA, d), lambda i: (0, 0)), pl.BlockSpec((1, d), lambda i: (0, 0))],
        out_shape=[S((t, d), F32), S((t, n), BF16), S((8, d), F32), S((8, d), F32), S((8, d), F32)],
        out_specs=[row_spec(d), row_spec(n), vec, vec, vec],
        compiler_params=_cp("arbitrary"),
    )(dyc, dyc, cx, dcb, dqk, dvv, dgo, dgl, conv_w, w, x, dxo, ada9, nrm)


def _loss_head(x, target, nrm, name):
    t, d = x.shape
    tm = min(ROW_TILE, t)
    nt = t // tm

    def body(x_ref, tg_ref, n_ref, loss_ref, dx_ref, dn_ref, acc_ref):
        i = pl.program_id(0)

        @pl.when(i == 0)
        def _():
            acc_ref[...] = jnp.zeros_like(acc_ref)
            dn_ref[...] = jnp.zeros_like(dn_ref)

        xt = x_ref[...]
        rstd = lax.rsqrt(jnp.mean(xt * xt, axis=-1, keepdims=True) + EPS)
        xhat = xt * rstd
        err = xhat * n_ref[...] - tg_ref[...]
        acc_ref[...] += _rsum8(err * err)
        dy = err * (1.0 / d)
        dn_ref[...] += _rsum8(dy * xhat)
        dxhat = dy * n_ref[...]
        dx_ref[...] = rstd * (dxhat - xhat * jnp.mean(dxhat * xhat, axis=-1, keepdims=True))

        @pl.when(i == nt - 1)
        def _():
            loss_ref[...] = jnp.full(loss_ref.shape, (0.5 / d) * jnp.sum(acc_ref[...]), F32)

    return pl.pallas_call(
        body, name=name, grid=(nt,),
        in_specs=[pl.BlockSpec((tm, d), lambda i: (i, 0)), pl.BlockSpec((tm, d), lambda i: (i, 0)),
                  pl.BlockSpec((1, d), lambda i: (0, 0))],
        out_shape=[S((1, 128), F32), S((t, d), F32), S((8, d), F32)],
        out_specs=[pl.BlockSpec((1, 128), lambda i: (0, 0)), pl.BlockSpec((tm, d), lambda i: (i, 0)),
                   pl.BlockSpec((8, d), lambda i: (0, 0))],
        scratch_shapes=[pltpu.VMEM((8, d), F32)],
        compiler_params=_cp("arbitrary"),
    )(x, target, nrm)


def _pack_smalls(vec_parts, dcw, dbg, dgn, dwg, loss_v, rank, name):
    d = vec_parts[0].shape[1]
    cw, kw, dv = dcw.shape[2], dbg.shape[1], dgn.shape[1]
    nv = len(vec_parts)
    loss_row = nv + 2 + rank * kw // d
    assert 2 * cw == d and cw + kw + dv <= d and (rank * kw) % d == 0 and loss_row < PACK_ROWS
    per_row = d // kw

    def body(*refs):
        vrefs, (dcw_ref, dbg_ref, dgn_ref, dwg_ref, loss_ref, o_ref) = refs[:nv], refs[nv:]
        o_ref[...] = jnp.zeros_like(o_ref)
        o_ref[loss_row:loss_row + 1, 0:loss_ref.shape[1]] = loss_ref[...]
        for r, ref in enumerate(vrefs):
            o_ref[r:r + 1, :] = jnp.sum(ref[...], axis=0, keepdims=True)
        o_ref[nv:nv + 1, 0:cw] = jnp.sum(dcw_ref[0], axis=0, keepdims=True)
        o_ref[nv:nv + 1, cw:2 * cw] = jnp.sum(dcw_ref[1], axis=0, keepdims=True)
        o_ref[nv + 1:nv + 2, 0:cw] = jnp.sum(dcw_ref[2], axis=0, keepdims=True)
        o_ref[nv + 1:nv + 2, cw:cw + kw] = jnp.sum(dbg_ref[...], axis=0, keepdims=True)
        o_ref[nv + 1:nv + 2, cw + kw:cw + kw + dv] = jnp.sum(dgn_ref[...], axis=0, keepdims=True)
        for r in range(rank):
            o_ref[nv + 2 + r // per_row:nv + 3 + r // per_row, (r % per_row) * kw:(r % per_row + 1) * kw] = dwg_ref[r:r + 1, :]

    return pl.pallas_call(body, name=name, out_shape=S((PACK_ROWS, d), F32), compiler_params=_cp())(*vec_parts, dcw, dbg, dgn, dwg, loss_v)


def _sum_slots(a, name):
    def body(a_ref, o_ref):
        acc = a_ref[0]
        for s in range(1, NDEV):
            acc = acc + a_ref[s]
        o_ref[...] = acc

    return pl.pallas_call(body, name=name, out_shape=S(a.shape[1:], F32), compiler_params=_cp())(a)


def _adamw(w, g, m, v):
    m = ADAM_B1 * m + (1.0 - ADAM_B1) * g
    v = ADAM_B2 * v + (1.0 - ADAM_B2) * (g * g)
    m_hat = m / (1.0 - ADAM_B1 ** ADAM_STEP)
    v_hat = v / (1.0 - ADAM_B2 ** ADAM_STEP)
    return -ADAM_LR * (m_hat / (jnp.sqrt(v_hat) + ADAM_EPS) + ADAM_WD * w), m, v


def _adam_slots(recv, w, m, v, name):
    r, c = w.shape
    slots = recv.shape[0]
    tr = _row_tile(r, c)

    def body(recv_ref, w_ref, m_ref, v_ref, g_ref, d_ref, mo_ref, vo_ref):
        g = recv_ref[0].astype(F32)
        for s in range(1, slots):
            g = g + recv_ref[s].astype(F32)
        g_ref[...] = g
        d_ref[...], mo_ref[...], vo_ref[...] = _adamw(w_ref[...], g, m_ref[...], v_ref[...])

    blk = pl.BlockSpec((tr, c), lambda i: (i, 0))
    return pl.pallas_call(
        body, name=name, grid=(r // tr,),
        in_specs=[pl.BlockSpec((slots, tr, c), lambda i: (0, i, 0)), blk, blk, blk],
        out_shape=[S((r, c), F32)] * 4, out_specs=[blk] * 4, compiler_params=_cp("parallel"),
    )(recv, w, m, v)


def _adam_w_ada(act_t, dada, w, m, v, name):
    r, c = w.shape
    tr = 128
    nb = act_t.shape[1]

    def body(a_ref, da_ref, w_ref, m_ref, v_ref, g_ref, d_ref, mo_ref, vo_ref):
        g = a_ref[:, 0:1] * da_ref[0:1, :]
        for b in range(1, nb):
            g = g + a_ref[:, b:b + 1] * da_ref[b:b + 1, :]
        g_ref[...] = g
        d_ref[...], mo_ref[...], vo_ref[...] = _adamw(w_ref[...], g, m_ref[...], v_ref[...])

    blk = pl.BlockSpec((tr, c), lambda i: (i, 0))
    return pl.pallas_call(
        body, name=name, grid=(r // tr,),
        in_specs=[pl.BlockSpec((tr, nb), lambda i: (i, 0)), pl.BlockSpec((nb, c), lambda i: (0, 0)), blk, blk, blk],
        out_shape=[S((r, c), F32)] * 4, out_specs=[blk] * 4, compiler_params=_cp("parallel"),
    )(act_t, dada, w, m, v)


def _adam_smalls(ws, gs, ms, vs, name):
    n = len(ws)

    def body(*refs):
        w_r, g_r, m_r, v_r = (refs[k * n:(k + 1) * n] for k in range(4))
        d_o, m_o, v_o = (refs[(4 + k) * n:(5 + k) * n] for k in range(3))
        for i in range(n):
            d_o[i][...], m_o[i][...], v_o[i][...] = _adamw(w_r[i][...], g_r[i][...], m_r[i][...], v_r[i][...])

    shapes = [S(w.shape, F32) for w in ws]
    outs = pl.pallas_call(body, name=name, out_shape=shapes * 3, compiler_params=_cp())(*ws, *gs, *ms, *vs)
    return outs[:n], outs[n:2 * n], outs[2 * n:]


def kernel(x, c, w_ada, b_ada, norm_ffn1, w_ffn1_in, w_ffn1_out, norm_mix, w_mix_in, conv_w, w_gk2, b_gk, gla_norm, w_mix_out, norm_ffn2, w_ffn2_in, w_ffn2_out, norm_final, loss_target, m_w_ada, m_b_ada, m_norm_ffn1, m_w_ffn1_in, m_w_ffn1_out, m_norm_mix, m_w_mix_in, m_conv_w, m_w_gk2, m_b_gk, m_gla_norm, m_w_mix_out, m_norm_ffn2, m_w_ffn2_in, m_w_ffn2_out, m_norm_final, v_w_ada, v_b_ada, v_norm_ffn1, v_w_ffn1_in, v_w_ffn1_out, v_norm_mix, v_w_mix_in, v_conv_w, v_w_gk2, v_b_gk, v_gla_norm, v_w_mix_out, v_norm_ffn2, v_w_ffn2_in, v_w_ffn2_out, v_norm_final):
    t, d = x.shape[1], x.shape[2]
    x0, tgt = x[0], loss_target[0]
    rank, kw = w_gk2.shape[1], w_gk2.shape[2] * NDEV
    cw = conv_w.shape[2] * NDEV
    dv = gla_norm.shape[1]
    vw = d - cw
    heads = vw // dv
    mix_cols = w_mix_in.shape[2]
    widths = [3 * cw, 2 * kw, vw, vw, 128]
    n_proj = 3 * cw + 2 * kw + 2 * vw + rank
    assert n_proj == mix_cols * NDEV and rank <= 128
    me = 4 * lax.axis_index("x") + 2 * lax.axis_index("y") + lax.axis_index("c")

    core = lax.axis_index("c").astype(jnp.int32).reshape(1)
    bf = lambda a: a[0].astype(BF16)
    bft = lambda a: a[0].T.astype(BF16)
    nb = NDEV // 2

    ((c_all, w1i, cwt_all, wg_all),) = _exchange([_plan_gather([c, bft(w_ffn1_in), conv_w[0], w_gk2[0]])], "gather_first")
    cwt = cwt_all.transpose(1, 0, 2).reshape(conv_w.shape[1], cw)
    wg = jnp.pad(wg_all.transpose(1, 0, 2).reshape(rank, kw), ((0, 128 - rank), (0, 0))).astype(BF16)

    act_all, p_ada = _ada_partial(c_all.reshape(NDEV, d), w_ada[0])
    ((p_all,),) = _exchange([_plan_all_to_all([p_ada], True)], "gather_ada")
    ada9 = _ada_finish(p_all, b_ada).reshape(N_ADA, d)

    (h1, gu1, s1), ((w1o, wmi),) = _ffn_in(x0, ada9, norm_ffn1, w1i, 0, 1, "ffn1_in", plans=[_plan_gather([bf(w_ffn1_out), bft(w_mix_in)])])
    w1o = w1o.reshape(nb, -1, d)
    wmi = jnp.pad(wmi.reshape(n_proj, d), ((0, sum(widths) - n_proj), (0, 0)))
    (x1, f1), ((wmo,),) = _ffn_out(s1, w1o, x0, ada9, 2, 0.5, "ffn1_out", plans=[_plan_gather([bf(w_mix_out)])])
    wmo = wmo.reshape(cw + vw, d)
    (h2, cx, qk, vv, go, gl), ((w2o,),) = _mix_in(x1, ada9, norm_mix, wmi, widths, [BF16, F32, BF16, BF16, F32], "mix_in",
                                                 plans=[_plan_gather([bf(w_ffn2_out)])])
    w2o = w2o.reshape(nb, -1, d)
    (o, lg, sall), ((w2i,),) = _gla_fwd(qk, vv, gl, wg, b_gk, heads, "gla_fwd", plans=[_plan_gather([bft(w_ffn2_in)])])
    x2, mm, ycat = _mix_out(cx, o, go, cwt, gla_norm, wmo, x1, ada9, heads, "mix_out")
    (h3, gu3, s3), _ = _ffn_in(x2, ada9, norm_ffn2, w2i, 6, 7, "ffn2_in")
    (x3, f3), _ = _ffn_out(s3, w2o, x2, ada9, 8, 0.5, "ffn2_out")
    loss_v, dx3, dnf = _loss_head(x3, tgt, norm_final.reshape(1, d), "loss_head")

    dx2, r2i, sum2o, (dsh3, dsc3, dg3, dn3), _, _ = _ffn_backward(
        dx3, x2, h3, gu3, s3, f3, ada9, norm_ffn2, w2i, w2o, (6, 7, 8), core, "ffn2_bwd")
    (dm, dyc, dcb, do, dgo, dg2, dcw, dgn), ((r2o,),) = _mix_bwd_a(dx2, mm, ada9, wmo, cx, o, go, cwt, gla_norm, heads, "mix_bwd_a",
                                                                 plans=[_plan_chip_swap([sum2o])])
    dqk, dvv, dgl, dwg, dbg = _gla_bwd(qk, vv, lg, do, sall, gl, wg, heads, "gla_bwd")
    dx1, dproj, dsh2, dsc2, dnm = _mix_bwd_b(dyc, cx, dcb, dqk, dvv, dgo, dgl, cwt, wmi, x1, dx2, ada9, norm_mix, "mix_bwd_b")
    n_pad = sum(widths)
    tn = n_pad // 5
    dwmi, _ = _tn_matmul(dproj, h2, lambda tk: (tk, tn), lambda sb, k: (k, sb), lambda tk: (tk, d), lambda sb, k: (k, 0),
                         (n_pad, d), (tn, d), lambda sb, k: (sb, 0), 5, "mix_dwin")
    dwmo, _ = _tn_matmul(ycat, dm, lambda tk: (tk, cw + vw), lambda sb, k: (k, 0), lambda tk: (tk, d), lambda sb, k: (k, 0),
                         (cw + vw, d), (cw + vw, d), lambda sb, k: (0, 0), 1, "mix_dwout")
    dwmi = dwmi[:n_proj].reshape(NDEV, mix_cols, d)
    dwmo = dwmo.reshape(NDEV, -1, d)
    dx0, r1i, sum1o, (dsh1, dsc1, dg1, dn1), _, ((rmi, rmo),) = _ffn_backward(
        dx1, x0, h1, gu1, s1, f1, ada9, norm_ffn1, w1i, w1o, (0, 1, 2), core, "ffn1_bwd",
        ds_plans=[_plan_sibling_swap([dwmi, dwmo])],
        dwin_plans=lambda moved: [_plan_chip_swap([_pair_add(dwmi, moved[0][0], core, "mix_dwin_add"),
                                                   _pair_add(dwmo, moved[0][1], core, "mix_dwout_add")])])
    pack = _pack_smalls([dn1, dnm, dn3, dnf, dsh1, dsc1, dg1, dsh2, dsc2, dg2, dsh3, dsc3, dg3], dcw, dbg, dgn, dwg, loss_v, rank, "pack_smalls")
    (r1o,), (pack_all,) = _exchange([_plan_chip_swap([sum1o]), _plan_all_to_all([pack], True)], "grads_last")
    tot = _sum_slots(pack_all, "sum_smalls")

    res = {}
    for nm, recv, w, m, v in (("w_ffn1_out", r1o, w_ffn1_out, m_w_ffn1_out, v_w_ffn1_out), ("w_mix_out", rmo, w_mix_out, m_w_mix_out, v_w_mix_out),
                              ("w_ffn2_out", r2o, w_ffn2_out, m_w_ffn2_out, v_w_ffn2_out)):
        res[nm] = [a[None] for a in _adam_slots(recv, w[0], m[0], v[0], "adam_" + nm)]
    for nm, recv, w, m, v in (("w_ffn1_in", r1i, w_ffn1_in, m_w_ffn1_in, v_w_ffn1_in), ("w_mix_in", rmi, w_mix_in, m_w_mix_in, v_w_mix_in),
                              ("w_ffn2_in", r2i, w_ffn2_in, m_w_ffn2_in, v_w_ffn2_in)):
        res[nm] = [a.T[None] for a in _adam_slots(recv, w[0].T, m[0].T, v[0].T, "adam_" + nm)]

    cols_ada = w_ada.shape[2]
    dada_all = pack_all[:, 4:4 + N_ADA, :].reshape(NDEV, N_ADA * d)
    dada_mine = lax.dynamic_slice_in_dim(dada_all, me * cols_ada, cols_ada, axis=1)
    res["w_ada"] = [a[None] for a in _adam_w_ada(act_all.T, dada_mine, w_ada[0], m_w_ada[0], v_w_ada[0], "adam_w_ada")]

    nv = 4 + N_ADA
    g_small = {
        "b_ada": tot[4:nv].reshape(1, N_ADA * d),
        "norm_ffn1": tot[0:1], "norm_mix": tot[1:2], "norm_ffn2": tot[2:3], "norm_final": tot[3:4],
        "conv_w": lax.dynamic_slice_in_dim(
            jnp.concatenate([tot[nv:nv + 1, 0:cw], tot[nv:nv + 1, cw:2 * cw], tot[nv + 1:nv + 2, 0:cw]], axis=0), me * (cw // NDEV), cw // NDEV, axis=1),
        "w_gk2": lax.dynamic_slice_in_dim(tot[nv + 2:nv + 2 + rank * kw // d].reshape(rank, kw), me * (kw // NDEV), kw // NDEV, axis=1),
        "b_gk": tot[nv + 1:nv + 2, cw:cw + kw],
        "gla_norm": tot[nv + 1:nv + 2, cw + kw:cw + kw + dv],
    }
    small = {"b_ada": (b_ada, m_b_ada, v_b_ada), "norm_ffn1": (norm_ffn1, m_norm_ffn1, v_norm_ffn1), "norm_mix": (norm_mix, m_norm_mix, v_norm_mix),
             "norm_ffn2": (norm_ffn2, m_norm_ffn2, v_norm_ffn2), "norm_final": (norm_final, m_norm_final, v_norm_final),
             "conv_w": (conv_w, m_conv_w, v_conv_w), "w_gk2": (w_gk2, m_w_gk2, v_w_gk2), "b_gk": (b_gk, m_b_gk, v_b_gk),
             "gla_norm": (gla_norm, m_gla_norm, v_gla_norm)}
    names = list(small)
    flat = lambda a: a.reshape(-1, a.shape[-1])
    dl, mo, vo = _adam_smalls([flat(small[n][0]) for n in names], [g_small[n] for n in names],
                              [flat(small[n][1]) for n in names], [flat(small[n][2]) for n in names], "adam_smalls")
    for i, n in enumerate(names):
        shp = small[n][0].shape
        res[n] = [g_small[n].reshape(shp), dl[i].reshape(shp), mo[i].reshape(shp), vo[i].reshape(shp)]

    loss = tot[nv + 2 + rank * kw // d, 0]
    order = ["w_ada", "b_ada", "norm_ffn1", "w_ffn1_in", "w_ffn1_out", "norm_mix", "w_mix_in", "conv_w", "w_gk2", "b_gk", "gla_norm",
             "w_mix_out", "norm_ffn2", "w_ffn2_in", "w_ffn2_out", "norm_final"]
    return (loss, dx0[None], *[res[n][0] for n in order], *[res[n][1] for n in order], *[res[n][2] for n in order], *[res[n][3] for n in order])
```

```python
import collections
import functools

import jax
import jax.numpy as jnp
from jax import lax
from jax.experimental import pallas as pl
from jax.experimental.pallas import tpu as pltpu

F32 = jnp.float32
BF16 = jnp.bfloat16
S = jax.ShapeDtypeStruct

NDEV = 8
EPS = 1e-6
GATE_NORMALIZER = 16.0
CHUNK = 128
N_ADA = 9
ADAM_LR, ADAM_B1, ADAM_B2, ADAM_EPS, ADAM_WD, ADAM_STEP = 0.001, 0.9, 0.999, 1e-08, 0.01, 10
V7X_VMEM_LIMIT = 56 * 1024 * 1024
ROW_TILE = 512
WIDE_ROW_TILE = 1024
K_TILE = 1024
PACK_ROWS = 24
ANY = pl.BlockSpec(memory_space=pl.ANY)


def _cp(*sem):
    return pltpu.CompilerParams(dimension_semantics=sem or None, vmem_limit_bytes=V7X_VMEM_LIMIT)


def _dot(a, b):
    return jnp.dot(a, b, preferred_element_type=F32)


def _dot_nt(a, b):
    return lax.dot_general(a, b, (((1,), (1,)), ((), ())), preferred_element_type=F32)


def _dot_tn(a, b):
    return lax.dot_general(a, b, (((0,), (0,)), ((), ())), preferred_element_type=F32)


def _rsum8(a):
    r, c = a.shape
    return jnp.sum(a.reshape(r // 8, 8, c), axis=0)


def _row_tile(r, c):
    for cand in (256, 128, 176, 88, 64, 32, 16, 8):
        if r % cand == 0 and cand * c * 4 <= 1024 * 1024:
            return cand
    return r


def _sigmoid(x):
    return 1.0 / (1.0 + jnp.exp(-x))


def _sigmoid_tanh(x):
    return 0.5 * jnp.tanh(0.5 * x) + 0.5


def _normmod(x, nrm, sc, sh):
    rstd = lax.rsqrt(jnp.mean(x * x, axis=-1, keepdims=True) + EPS)
    xhat = x * rstd
    return xhat, rstd, (xhat * nrm) * (1.0 + sc) + sh


def _normmod_bwd(dh, x, nrm, sc):
    rstd = lax.rsqrt(jnp.mean(x * x, axis=-1, keepdims=True) + EPS)
    xhat = x * rstd
    dxhat = dh * (nrm * (1.0 + sc))
    dx = rstd * (dxhat - xhat * jnp.mean(dxhat * xhat, axis=-1, keepdims=True))
    return dx, dh, dh * (xhat * nrm), dh * ((1.0 + sc) * xhat)


def _place():
    x, y, c = lax.axis_index("x"), lax.axis_index("y"), lax.axis_index("c")
    return x, y, c, 4 * x + 2 * y + c


def _peer(x, y, c, k):
    px = 1 - x if k & 4 else x
    py = 1 - y if k & 2 else y
    pc = 1 - c if k & 1 else c
    return (px, py, pc), 4 * px + 2 * py + pc


def _remote(src, dst, send_sem, recv_sem, peer):
    return pltpu.make_async_remote_copy(src_ref=src, dst_ref=dst, send_sem=send_sem, recv_sem=recv_sem,
                                        device_id=peer, device_id_type=pl.DeviceIdType.MESH)


_Plan = collections.namedtuple("_Plan", "inputs out_shapes sem_shapes start finish aliases", defaults=({},))


def _plan_all_to_all(xs, gather):
    n = len(xs)

    def copies(ins, outs, sems, landed):
        send_sems, recv_sems, local_sems = sems
        x, y, c, me = _place()
        local = [pltpu.make_async_copy(ins[i] if gather else ins[i].at[me], outs[i].at[me], local_sems.at[i]) for i in range(n)]
        remote = []
        for k in range(1, NDEV):
            peer, pid = _peer(x, y, c, k)
            for i in range(n):
                remote.append(_remote(ins[i] if gather else ins[i].at[pid], outs[i].at[pid if landed else me],
                                      send_sems.at[i, k - 1], recv_sems.at[i, k - 1], peer))
        return local, remote

    def start(ins, outs, sems):
        local, remote = copies(ins, outs, sems, False)
        for cp in local + remote:
            cp.start()

    def finish(ins, outs, sems):
        local, remote = copies(ins, outs, sems, True)
        for cp in remote + local:
            cp.wait()

    return _Plan(list(xs), [S((NDEV,) + a.shape, a.dtype) if gather else S(a.shape, a.dtype) for a in xs],
                 [pltpu.SemaphoreType.DMA((n, NDEV - 1)), pltpu.SemaphoreType.DMA((n, NDEV - 1)), pltpu.SemaphoreType.DMA((n,))],
                 start, finish)


def _other_chips(x, y):
    return [(1 - x, y), (x, 1 - y), (1 - x, 1 - y)]


def _plan_gather(xs, rows=None, into=None):
    n = len(xs)

    def copies(ins, outs, sems, rest):
        send_sems, recv_sems, local_sems = sems
        x, y, c, me = _place()
        sib, sib_id = (x, y, 1 - c), 4 * x + 2 * y + 1 - c
        chips = _other_chips(x, y)
        mine = lambda i: ins[i] if rows is None else ins[i].at[pl.ds(*rows)]
        slot_of = lambda i, s: outs[i].at[s] if rows is None else outs[i].at[s, pl.ds(*rows)]
        local = [pltpu.make_async_copy(mine(i), slot_of(i, me), local_sems.at[i]) for i in range(n)]
        first = [_remote(mine(i), slot_of(i, me), send_sems.at[i, 0], recv_sems.at[i, 0], sib) for i in range(n)]
        first += [_remote(mine(i), slot_of(i, me), send_sems.at[i, 1 + j], recv_sems.at[i, 1 + j], (px, py, c))
                  for j, (px, py) in enumerate(chips) for i in range(n)]
        if not rest:
            return local, first
        from_sibling = [_remote(mine(i), slot_of(i, sib_id), send_sems.at[i, 0], recv_sems.at[i, 0], sib) for i in range(n)]
        arrive, forward = [], []
        for j, (px, py) in enumerate(chips):
            s = 4 * px + 2 * py
            arrive.append([_remote(mine(i), slot_of(i, s + c), send_sems.at[i, 1 + j], recv_sems.at[i, 1 + j], (px, py, c)) for i in range(n)])
            forward.append([_remote(slot_of(i, s + c), slot_of(i, s + c), send_sems.at[i, 4 + j], recv_sems.at[i, 4 + j], sib) for i in range(n)])
            from_sibling += [_remote(mine(i), slot_of(i, s + 1 - c), send_sems.at[i, 4 + j], recv_sems.at[i, 4 + j], sib) for i in range(n)]
        return local, first, arrive, forward, from_sibling

    def start(ins, outs, sems):
        local, first = copies(ins, outs, sems, False)
        for cp in local + first:
            cp.start()

    def finish(ins, outs, sems):
        local, first, arrive, forward, from_sibling = copies(ins, outs, sems, True)
        for landed, onward in zip(arrive, forward):
            for cp in landed:
                cp.wait_recv()
            for cp in onward:
                cp.start()
        for cp in from_sibling:
            cp.wait_recv()
        for cp in first + [cp for onward in forward for cp in onward]:
            cp.wait_send()
        for cp in local:
            cp.wait()

    return _Plan(list(xs) + list(into or []), [S((NDEV,) + a.shape, a.dtype) for a in xs],
                 [pltpu.SemaphoreType.DMA((n, NDEV - 1)), pltpu.SemaphoreType.DMA((n, NDEV - 1)), pltpu.SemaphoreType.DMA((n,))],
                 start, finish, {n + i: i for i in range(len(into or []))})


def _plan_sibling_swap(gs):
    n = len(gs)

    def copies(ins, outs, sems):
        send_sems, recv_sems = sems
        x, y, c, _ = _place()
        return [_remote(ins[i].at[2 * j + 1 - c], outs[i].at[j], send_sems.at[i, j], recv_sems.at[i, j], (x, y, 1 - c))
                for i in range(n) for j in range(NDEV // 2)]

    def start(ins, outs, sems):
        for cp in copies(ins, outs, sems):
            cp.start()

    def finish(ins, outs, sems):
        for cp in copies(ins, outs, sems):
            cp.wait()

    return _Plan(list(gs), [S((NDEV // 2,) + a.shape[1:], a.dtype) for a in gs],
                 [pltpu.SemaphoreType.DMA((n, NDEV // 2)), pltpu.SemaphoreType.DMA((n, NDEV // 2))], start, finish)


def _pair_add(g, r1, core, name):
    _, r, c = g.shape
    tr = _row_tile(r, c)

    def body(core_ref, g_ref, r_ref, o_ref):
        o_ref[...] = (g_ref[...].astype(F32) + r_ref[...].astype(F32)).astype(BF16)

    return pl.pallas_call(
        body, name=name,
        grid_spec=pltpu.PrefetchScalarGridSpec(
            num_scalar_prefetch=1, grid=(NDEV // 2, r // tr),
            in_specs=[pl.BlockSpec((None, tr, c), lambda j, k, core_ref: (2 * j + core_ref[0], k, 0)),
                      pl.BlockSpec((None, tr, c), lambda j, k, core_ref: (j, k, 0))],
            out_specs=pl.BlockSpec((None, tr, c), lambda j, k, core_ref: (j, k, 0))),
        out_shape=S((NDEV // 2, r, c), BF16), compiler_params=_cp("parallel", "parallel"),
    )(core, g, r1)


def _plan_chip_swap(ps):
    n = len(ps)

    def copies(ins, outs, sems, landed):
        send_sems, recv_sems, local_sems = sems
        x, y, c, _ = _place()
        mine = 2 * x + y
        local = [pltpu.make_async_copy(ins[i].at[mine], outs[i].at[mine], local_sems.at[i]) for i in range(n)]
        remote = [_remote(ins[i].at[2 * px + py], outs[i].at[2 * px + py if landed else mine], send_sems.at[i, j], recv_sems.at[i, j], (px, py, c))
                  for j, (px, py) in enumerate(_other_chips(x, y)) for i in range(n)]
        return local, remote

    def start(ins, outs, sems):
        local, remote = copies(ins, outs, sems, False)
        for cp in local + remote:
            cp.start()

    def finish(ins, outs, sems):
        local, remote = copies(ins, outs, sems, True)
        for cp in remote + local:
            cp.wait()

    return _Plan(list(ps), [S(a.shape, a.dtype) for a in ps],
                 [pltpu.SemaphoreType.DMA((n, 3)), pltpu.SemaphoreType.DMA((n, 3)), pltpu.SemaphoreType.DMA((n,))], start, finish)


def _pcall(body, name, args, in_specs, out_shape, out_specs, grid=(), scratch_shapes=(), sem=(), plans=()):
    n_in, n_out, n_scr = len(args), len(out_shape), len(scratch_shapes)
    counts = [(len(p.inputs), len(p.out_shapes), len(p.sem_shapes)) for p in plans]
    c_args = [a for p in plans for a in p.inputs]
    c_outs = [s for p in plans for s in p.out_shapes]
    c_sems = [s for p in plans for s in p.sem_shapes]

    def wrapped(*refs):
        cuts = [n_in, len(c_args), n_out, len(c_outs), n_scr, len(c_sems)]
        ins, c_in, outs, c_out, scr, c_sem = [refs[sum(cuts[:k]):sum(cuts[:k + 1])] for k in range(6)]

        def halves(which):
            a = b = s = 0
            for p, (na, nb, ns) in zip(plans, counts):
                getattr(p, which)(c_in[a:a + na], c_out[b:b + nb], c_sem[s:s + ns])
                a, b, s = a + na, b + nb, s + ns

        if not plans:
            body(*ins, *outs, *scr)
        elif not grid:
            halves("start")
            body(*ins, *outs, *scr)
            halves("finish")
        else:
            first = functools.reduce(jnp.logical_and, [pl.program_id(a) == 0 for a in range(len(grid))])
            last = functools.reduce(jnp.logical_and, [pl.program_id(a) == grid[a] - 1 for a in range(len(grid))])
            pl.when(first)(lambda: halves("start"))
            body(*ins, *outs, *scr)
            pl.when(last)(lambda: halves("finish"))

    aliases, a, b = {}, n_in, n_out
    for p, (na, nb, _) in zip(plans, counts):
        aliases.update({a + k: b + v for k, v in p.aliases.items()})
        a, b = a + na, b + nb
    res = pl.pallas_call(
        wrapped, name=name, grid=grid, in_specs=list(in_specs) + [ANY] * len(c_args),
        out_shape=list(out_shape) + c_outs, out_specs=list(out_specs) + [ANY] * len(c_outs),
        scratch_shapes=list(scratch_shapes) + c_sems, input_output_aliases=aliases,
        compiler_params=_cp(*(("arbitrary",) * len(grid) if plans else sem)),
    )(*args, *c_args)
    c_res, b = [], n_out
    for _, nb, _ in counts:
        c_res.append(res[b:b + nb])
        b += nb
    return res[:n_out], c_res


def _exchange(plans, name):
    return _pcall(lambda: None, name, [], [], [], [], plans=plans)[1]


def _ada_partial(c_all, w_ada):
    nb, d = c_all.shape
    cols = w_ada.shape[1]

    def body(c_ref, w_ref, act_ref, p_ref):
        cc = c_ref[...]
        act = cc * _sigmoid(cc)
        act_ref[...] = act
        p_ref[...] = _dot(act.astype(BF16), w_ref[...].astype(BF16))

    return pl.pallas_call(body, name="ada_partial", out_shape=[S((nb, d), F32), S((nb, cols), F32)],
                          compiler_params=_cp())(c_all, w_ada)


def _ada_finish(p_all, b_ada):
    _, nb, cols = p_all.shape

    def body(p_ref, b_ref, o_ref):
        me = _place()[3]
        for s in range(NDEV):
            o_ref[:, s * cols:(s + 1) * cols] = p_ref[s, pl.ds(me, 1), :] + b_ref[:, s * cols:(s + 1) * cols]

    return pl.pallas_call(body, name="ada_finish", out_shape=S((1, NDEV * cols), F32), compiler_params=_cp())(p_all, b_ada)


def _ffn_in(x, ada9, nrm, w_in, sh_row, sc_row, name, plans=()):
    t, d = x.shape
    nb, bw = w_in.shape[0] // 2, w_in.shape[1]
    tm = min(WIDE_ROW_TILE, t)

    def body(x_ref, ada_ref, n_ref, wg_ref, wu_ref, h_ref, gu_ref, s_ref):
        @pl.when(pl.program_id(1) == 0)
        def _():
            _, _, h = _normmod(x_ref[...], n_ref[...], ada_ref[sc_row:sc_row + 1, :], ada_ref[sh_row:sh_row + 1, :])
            h_ref[...] = h.astype(BF16)

        h = h_ref[...]
        g = _dot_nt(h, wg_ref[...])
        u = _dot_nt(h, wu_ref[...])
        gu_ref[0] = g.astype(BF16)
        gu_ref[1] = u.astype(BF16)
        s_ref[...] = (g * _sigmoid(g) * u).astype(BF16)

    return _pcall(
        body, name, [x, ada9, nrm, w_in, w_in], grid=(t // tm, nb),
        in_specs=[pl.BlockSpec((tm, d), lambda i, j: (i, 0)), pl.BlockSpec((N_ADA, d), lambda i, j: (0, 0)),
                  pl.BlockSpec((1, d), lambda i, j: (0, 0)),
                  pl.BlockSpec((None, bw, d), lambda i, j: (j, 0, 0)), pl.BlockSpec((None, bw, d), lambda i, j: (j + nb, 0, 0))],
        out_shape=[S((t, d), BF16), S((2, nb, t, bw), BF16), S((nb, t, bw), BF16)],
        out_specs=[pl.BlockSpec((tm, d), lambda i, j: (i, 0)), pl.BlockSpec((2, None, tm, bw), lambda i, j: (0, j, i, 0)),
                   pl.BlockSpec((None, tm, bw), lambda i, j: (j, i, 0))],
        sem=("parallel", "arbitrary"), plans=plans)


def _ffn_out(s, w_out, x, ada9, g_row, res_scale, name, plans=()):
    nb, t, bw = s.shape
    d = x.shape[1]
    tm = min(ROW_TILE, t)

    def body(s_ref, w_ref, x_ref, ada_ref, xo_ref, f_ref):
        acc = _dot(s_ref[0], w_ref[0])
        for b in range(1, nb):
            acc = acc + _dot(s_ref[b], w_ref[b])
        f_ref[...] = acc.astype(BF16)
        xo_ref[...] = x_ref[...] + (res_scale * ada_ref[g_row:g_row + 1, :]) * acc

    return _pcall(
        body, name, [s, w_out, x, ada9], grid=(t // tm,),
        in_specs=[pl.BlockSpec((nb, tm, bw), lambda i: (0, i, 0)), pl.BlockSpec((nb, bw, d), lambda i: (0, 0, 0)),
                  pl.BlockSpec((tm, d), lambda i: (i, 0)), pl.BlockSpec((N_ADA, d), lambda i: (0, 0))],
        out_shape=[S((t, d), F32), S((t, d), BF16)],
        out_specs=[pl.BlockSpec((tm, d), lambda i: (i, 0)), pl.BlockSpec((tm, d), lambda i: (i, 0))],
        sem=("parallel",), plans=plans)


def _ffn_bwd_ds(dxo, f, ada9, w_out, gu, g_row, res_scale, name, plans=()):
    t, d = dxo.shape
    nb, bw = w_out.shape[0], w_out.shape[1]
    tm = min(WIDE_ROW_TILE, t)

    def body(dxo_ref, f_ref, ada_ref, w_ref, gu_ref, df_ref, da_ref, dg_ref):
        i, j = pl.program_id(0), pl.program_id(1)

        @pl.when((i == 0) & (j == 0))
        def _():
            dg_ref[...] = jnp.zeros_like(dg_ref)

        @pl.when(j == 0)
        def _():
            dxo_t = dxo_ref[...]
            df_ref[...] = ((res_scale * ada_ref[g_row:g_row + 1, :]) * dxo_t).astype(BF16)
            dg_ref[...] += res_scale * _rsum8(dxo_t * f_ref[...].astype(F32))

        ds = _dot_nt(df_ref[...], w_ref[...])
        g = gu_ref[0].astype(F32)
        u = gu_ref[1].astype(F32)
        sg = _sigmoid_tanh(g)
        da_ref[0] = (ds * u * (sg * (1.0 + g * (1.0 - sg)))).astype(BF16)
        da_ref[1] = (ds * (g * sg)).astype(BF16)

    return _pcall(
        body, name, [dxo, f, ada9, w_out, gu], grid=(t // tm, nb),
        in_specs=[pl.BlockSpec((tm, d), lambda i, j: (i, 0)), pl.BlockSpec((tm, d), lambda i, j: (i, 0)),
                  pl.BlockSpec((N_ADA, d), lambda i, j: (0, 0)), pl.BlockSpec((None, bw, d), lambda i, j: (j, 0, 0)),
                  pl.BlockSpec((2, None, tm, bw), lambda i, j: (0, j, i, 0))],
        out_shape=[S((t, d), BF16), S((2, nb, t, bw), BF16), S((8, d), F32)],
        out_specs=[pl.BlockSpec((tm, d), lambda i, j: (i, 0)), pl.BlockSpec((2, None, tm, bw), lambda i, j: (0, j, i, 0)),
                   pl.BlockSpec((8, d), lambda i, j: (0, 0))],
        sem=("arbitrary", "arbitrary"), plans=plans)


def _ffn_bwd_dh(da, w_in, x, dxo, ada9, nrm, sh_row, sc_row, name, plans=()):
    t, d = x.shape
    nb, bw = w_in.shape[0] // 2, w_in.shape[1]
    tm = min(WIDE_ROW_TILE, t)

    def body(da_ref, wg_ref, wu_ref, x_ref, dxo_ref, ada_ref, n_ref, dx_ref, dsh_ref, dsc_ref, dn_ref, acc_ref):
        i, j = pl.program_id(0), pl.program_id(1)

        @pl.when((i == 0) & (j == 0))
        def _():
            dsh_ref[...] = jnp.zeros_like(dsh_ref)
            dsc_ref[...] = jnp.zeros_like(dsc_ref)
            dn_ref[...] = jnp.zeros_like(dn_ref)

        part = _dot(da_ref[0], wg_ref[...]) + _dot(da_ref[1], wu_ref[...])

        @pl.when(j == 0)
        def _():
            acc_ref[...] = part

        @pl.when(j > 0)
        def _():
            acc_ref[...] += part

        @pl.when(j == nb - 1)
        def _():
            for r0 in range(0, tm, min(256, tm)):
                rows = slice(r0, r0 + min(256, tm))
                dx, tsh, tsc, tn = _normmod_bwd(acc_ref[rows, :], x_ref[rows, :], n_ref[...], ada_ref[sc_row:sc_row + 1, :])
                dx_ref[rows, :] = dxo_ref[rows, :] + dx
                dsh_ref[...] += _rsum8(tsh)
                dsc_ref[...] += _rsum8(tsc)
                dn_ref[...] += _rsum8(tn)

    vec = pl.BlockSpec((8, d), lambda i, j: (0, 0))
    return _pcall(
        body, name, [da, w_in, w_in, x, dxo, ada9, nrm], grid=(t // tm, nb),
        in_specs=[pl.BlockSpec((2, None, tm, bw), lambda i, j: (0, j, i, 0)),
                  pl.BlockSpec((None, bw, d), lambda i, j: (j, 0, 0)), pl.BlockSpec((None, bw, d), lambda i, j: (j + nb, 0, 0)),
                  pl.BlockSpec((tm, d), lambda i, j: (i, 0)), pl.BlockSpec((tm, d), lambda i, j: (i, 0)),
                  pl.BlockSpec((N_ADA, d), lambda i, j: (0, 0)), pl.BlockSpec((1, d), lambda i, j: (0, 0))],
        out_shape=[S((t, d), F32), S((8, d), F32), S((8, d), F32), S((8, d), F32)],
        out_specs=[pl.BlockSpec((tm, d), lambda i, j: (i, 0)), vec, vec, vec],
        scratch_shapes=[pltpu.VMEM((tm, d), F32)],
        sem=("arbitrary", "arbitrary"), plans=plans)


def _tn_matmul(a, b, a_block, a_map, b_block, b_map, out_shape, out_block, out_map, nblk, name, plans=()):
    t = a.shape[-2]
    tk = min(K_TILE, t)
    nk = t // tk

    def body(a_ref, b_ref, o_ref, acc_ref):
        k = pl.program_id(1)
        for q in (range(a_ref.shape[0]) if len(a_ref.shape) == 3 else [Ellipsis]):
            part = _dot_tn(a_ref[q], b_ref[...])

            @pl.when(k == 0)
            def _():
                acc_ref[q] = part

            @pl.when(k > 0)
            def _():
                acc_ref[q] += part

        @pl.when(k == nk - 1)
        def _():
            o_ref[...] = acc_ref[...].astype(BF16)

    (out,), moved = _pcall(
        body, name, [a, b], grid=(nblk, nk),
        in_specs=[pl.BlockSpec(a_block(tk), a_map), pl.BlockSpec(b_block(tk), b_map)],
        out_shape=[S(out_shape, BF16)], out_specs=[pl.BlockSpec(out_block, out_map)],
        scratch_shapes=[pltpu.VMEM(tuple(n for n in out_block if n is not None), F32)],
        sem=("parallel", "arbitrary"), plans=plans)
    return out, moved


def _ffn_backward(dxo, x_in, h, gu, s, f, ada9, nrm, w_in, w_out, rows, core, name, ds_plans=(), dwin_plans=()):
    sh_row, sc_row, g_row = rows
    nb, t, bw = s.shape
    d = x_in.shape[1]
    (df, da, dg), ds_moved = _ffn_bwd_ds(dxo, f, ada9, w_out, gu, g_row, 0.5, name + "_ds", plans=ds_plans)
    dw_in, dwin_moved = _tn_matmul(
        da.reshape(2 * nb, t, bw), h, lambda tk: (2, tk, bw), lambda sb, k: (sb, k, 0), lambda tk: (tk, d), lambda sb, k: (k, 0),
        (2 * nb, bw, d), (2, bw, d), lambda sb, k: (sb, 0, 0), nb, name + "_dwin",
        plans=dwin_plans(ds_moved) if callable(dwin_plans) else dwin_plans)
    dw_out, ((half_in,),) = _tn_matmul(
        s, df, lambda tk: (None, tk, bw), lambda sb, k: (sb, k, 0), lambda tk: (tk, d), lambda sb, k: (k, 0),
        (nb, bw, d), (None, bw, d), lambda sb, k: (sb, 0, 0), nb, name + "_dwout", plans=[_plan_sibling_swap([dw_in])])
    dw_out = dw_out.reshape(NDEV, -1, d)
    sum_in = _pair_add(dw_in, half_in, core, name + "_dwin_add")
    (dx, dsh, dsc, dn), ((recv_in,), (half_out,)) = _ffn_bwd_dh(
        da, w_in, x_in, dxo, ada9, nrm, sh_row, sc_row, name + "_dh", plans=[_plan_chip_swap([sum_in]), _plan_sibling_swap([dw_out])])
    sum_out = _pair_add(dw_out, half_out, core, name + "_dwout_add")
    return dx, recv_in, sum_out, (dsh, dsc, dg, dn), ds_moved, dwin_moved


def _mix_in(x, ada9, nrm, w, widths, dts, name, plans=()):
    t, d = x.shape
    n = w.shape[0]
    tm = min(ROW_TILE, t)
    starts = [sum(widths[:i]) for i in range(len(widths))]

    def body(x_ref, ada_ref, n_ref, w_ref, h_ref, *out_refs):
        _, _, h = _normmod(x_ref[...], n_ref[...], ada_ref[4:5, :], ada_ref[3:4, :])
        hb = h.astype(BF16)
        h_ref[...] = hb
        for o_ref, st, wd in zip(out_refs, starts, widths):
            o_ref[...] = _dot_nt(hb, w_ref[st:st + wd, :]).astype(o_ref.dtype)

    return _pcall(
        body, name, [x, ada9, nrm, w], grid=(t // tm,),
        in_specs=[pl.BlockSpec((tm, d), lambda i: (i, 0)), pl.BlockSpec((N_ADA, d), lambda i: (0, 0)),
                  pl.BlockSpec((1, d), lambda i: (0, 0)), pl.BlockSpec((n, d), lambda i: (0, 0))],
        out_shape=[S((t, d), BF16)] + [S((t, wd), dt) for wd, dt in zip(widths, dts)],
        out_specs=[pl.BlockSpec((tm, d), lambda i: (i, 0))] + [pl.BlockSpec((tm, wd), lambda i: (i, 0)) for wd in widths],
        sem=("parallel",), plans=plans)


def _tri(lower):
    r = lax.broadcasted_iota(jnp.int32, (CHUNK, CHUNK), 0)
    c = lax.broadcasted_iota(jnp.int32, (CHUNK, CHUNK), 1)
    return (r >= c) if lower else (c >= r)


def _dot_01(m, x):
    hi = x.astype(BF16)
    r1 = x - hi.astype(F32)
    mid = r1.astype(BF16)
    lo = (r1 - mid.astype(F32)).astype(BF16)
    return _dot(m, hi) + _dot(m, mid) + _dot(m, lo)


def _gla_chunk_terms(q, k, lg, low01):
    b = _dot_01(low01, lg)
    bl = b[CHUNK - 1:CHUNK, :]
    r = 0.5 * bl
    eb, ebl, em, en = jnp.exp(b), jnp.exp(bl - b), jnp.exp(b - r), jnp.exp(r - b)
    return eb, ebl, em, en, jnp.exp(bl), q * eb, k * ebl, q * em, k * en


def _scores(qm_h, knp, qk1_h):
    r = lax.broadcasted_iota(jnp.int32, (CHUNK, CHUNK), 0)
    c = lax.broadcasted_iota(jnp.int32, (CHUNK, CHUNK), 1)
    p = jnp.where(r > c, _dot_nt(qm_h, knp), 0.0)
    return jnp.where(r == c, jnp.sum(qk1_h, axis=1, keepdims=True), p)


def _gla_fwd(qk, v, gl, wg, bg, heads, name, plans=()):
    t = qk.shape[0]
    kw, vw = qk.shape[1] // 2, v.shape[1]
    dk, dv = kw // heads, vw // heads
    assert dk == 64 and dv == 128 and kw % 128 == 0
    gt = min(ROW_TILE, t)
    nc = gt // CHUNK
    scale = dk ** -0.5

    def body(qk_ref, v_ref, gl_ref, wg_ref, bg_ref, o_ref, lg_ref, sall_ref, st_ref):
        @pl.when(pl.program_id(0) == 0)
        def _():
            st_ref[...] = jnp.zeros_like(st_ref)

        gk = _dot(gl_ref[...].astype(BF16), wg_ref[...]) + bg_ref[...]
        lg_ref[...] = (jnp.minimum(gk, 0.0) - jnp.log(1.0 + jnp.exp(-jnp.abs(gk)))) / GATE_NORMALIZER
        low01 = _tri(True).astype(BF16)
        lane = lax.broadcasted_iota(jnp.int32, (CHUNK, 128), 1)

        def chunk(ci, carry):
            rows = pl.ds(pl.multiple_of(ci * CHUNK, CHUNK), CHUNK)
            q = qk_ref[rows, 0:kw] * scale
            k = qk_ref[rows, kw:2 * kw]
            qk1 = q.astype(BF16).astype(F32) * k.astype(BF16).astype(F32)
            eb, ebl, em, en, ebl_row, qe, ke, qm, kn = _gla_chunk_terms(q, k, lg_ref[rows, :], low01)
            for h in range(heads):
                lanes = slice(128 * (h // 2), 128 * (h // 2) + 128)
                own = (lane < 64) if h % 2 == 0 else (lane >= 64)
                knp = kn[:, lanes].astype(BF16)
                qm_h = jnp.where(own, qm[:, lanes], 0.0).astype(BF16)
                qe_h = jnp.where(own, qe[:, lanes], 0.0).astype(BF16)
                ke_h = jnp.where(own, ke[:, lanes], 0.0).astype(BF16)
                v_h = v_ref[rows, h * dv:(h + 1) * dv]
                st = st_ref[h]
                sall_ref[ci, h] = st
                p = _scores(qm_h, knp, jnp.where(own, qk1[:, lanes], 0.0))
                o_ref[rows, h * dv:(h + 1) * dv] = _dot(p.astype(BF16), v_h) + _dot_nt(qe_h, st.astype(BF16))
                st_ref[h] = st * ebl_row[:, lanes] + _dot_tn(v_h, ke_h)
            return carry

        lax.fori_loop(0, nc, chunk, 0, unroll=True)

    return _pcall(
        body, name, [qk, v, gl, wg, bg], grid=(t // gt,),
        in_specs=[pl.BlockSpec((gt, 2 * kw), lambda i: (i, 0)), pl.BlockSpec((gt, vw), lambda i: (i, 0)),
                  pl.BlockSpec((gt, 128), lambda i: (i, 0)), pl.BlockSpec((128, kw), lambda i: (0, 0)),
                  pl.BlockSpec((1, kw), lambda i: (0, 0))],
        out_shape=[S((t, vw), F32), S((t, kw), F32), S((t // CHUNK, heads, dv, 128), F32)],
        out_specs=[pl.BlockSpec((gt, vw), lambda i: (i, 0)), pl.BlockSpec((gt, kw), lambda i: (i, 0)),
                   pl.BlockSpec((nc, heads, dv, 128), lambda i: (i, 0, 0, 0))],
        scratch_shapes=[pltpu.VMEM((heads, dv, 128), F32)],
        sem=("arbitrary",), plans=plans)


def _gla_bwd(qk, v, lg, do, sall, gl, wg, heads, name):
    t = qk.shape[0]
    kw, vw = qk.shape[1] // 2, v.shape[1]
    dk, dv = kw // heads, vw // heads
    gt = min(ROW_TILE, t)
    nc = gt // CHUNK
    nt = t // gt
    scale = dk ** -0.5

    def body(qk_ref, v_ref, lg_ref, do_ref, sall_ref, gl_ref, wg_ref, dqk_ref, dv_ref, dgl_ref, dwg_ref, dbg_ref, dst_ref, dgk_ref):
        @pl.when(pl.program_id(0) == 0)
        def _():
            dst_ref[...] = jnp.zeros_like(dst_ref)
            dwg_ref[...] = jnp.zeros_like(dwg_ref)
            dbg_ref[...] = jnp.zeros_like(dbg_ref)

        low01 = _tri(True).astype(BF16)
        up01 = _tri(False).astype(BF16)
        causal = _tri(True)
        lane = lax.broadcasted_iota(jnp.int32, (CHUNK, 128), 1)
        last_row = lax.broadcasted_iota(jnp.int32, (CHUNK, kw), 0) == CHUNK - 1

        def chunk(cj, carry):
            ci = nc - 1 - cj
            rows = pl.ds(pl.multiple_of(ci * CHUNK, CHUNK), CHUNK)
            q = qk_ref[rows, 0:kw] * scale
            k = qk_ref[rows, kw:2 * kw]
            qk1 = q.astype(BF16).astype(F32) * k.astype(BF16).astype(F32)
            lgc = lg_ref[rows, :]
            eb, ebl, em, en, ebl_row, qe, ke, qm, kn = _gla_chunk_terms(q, k, lgc, low01)
            dqe, dqm, dkn, dke, drow = [], [], [], [], []
            for pr in range(kw // 128):
                lanes = slice(128 * pr, 128 * pr + 128)
                knp = kn[:, lanes].astype(BF16)
                parts = []
                for half in range(2):
                    h = 2 * pr + half
                    own = (lane < 64) if half == 0 else (lane >= 64)
                    qm_h = jnp.where(own, qm[:, lanes], 0.0).astype(BF16)
                    qe_h = jnp.where(own, qe[:, lanes], 0.0).astype(BF16)
                    ke_h = jnp.where(own, ke[:, lanes], 0.0).astype(BF16)
                    v_h = v_ref[rows, h * dv:(h + 1) * dv]
                    do_h = do_ref[rows, h * dv:(h + 1) * dv]
                    st = sall_ref[ci, h]
                    dst = dst_ref[h]
                    stb, dstb = st.astype(BF16), dst.astype(BF16)
                    p = _scores(qm_h, knp, jnp.where(own, qk1[:, lanes], 0.0)).astype(BF16)
                    dp = jnp.where(causal, _dot_nt(do_h, v_h), 0.0).astype(BF16)
                    dv_ref[rows, h * dv:(h + 1) * dv] = (_dot_tn(p, do_h) + _dot_nt(ke_h, dstb)).astype(BF16)
                    parts.append((jnp.where(own, _dot(dp, knp), 0.0), _dot_tn(dp, qm_h), _dot(do_h, stb), _dot(v_h, dstb),
                                  jnp.sum(st * dst, axis=0, keepdims=True)))
                    dst_ref[h] = dst * ebl_row[:, lanes] + _dot_tn(do_h, qe_h)
                dqm.append(parts[0][0] + parts[1][0])
                dkn.append(parts[0][1] + parts[1][1])
                dqe.append(parts[0][2] + parts[1][2])
                dke.append(parts[0][3] + parts[1][3])
                drow.append(parts[0][4] + parts[1][4])
            dqm, dkn, dqe, dke, drow = [jnp.concatenate(a, axis=1) for a in (dqm, dkn, dqe, dke, drow)]
            dqk_ref[rows, 0:kw] = ((dqe * eb + dqm * em) * scale).astype(BF16)
            dqk_ref[rows, kw:2 * kw] = (dke * ebl + dkn * en).astype(BF16)
            tke = dke * ke
            db = dqe * qe + dqm * qm - dkn * kn - tke
            dbl = jnp.sum(tke, axis=0, keepdims=True) + drow * ebl_row
            db = db + jnp.where(last_row, dbl, 0.0)
            dlg = _dot_01(up01, db)
            dgk_ref[rows, :] = dlg * ((1.0 - jnp.exp(GATE_NORMALIZER * lgc)) / GATE_NORMALIZER)
            return carry

        lax.fori_loop(0, nc, chunk, 0, unroll=True)
        dgk = dgk_ref[...]
        dgkb = dgk.astype(BF16)
        dgl_ref[...] = _dot_nt(dgkb, wg_ref[...]).astype(BF16)
        dwg_ref[...] += _dot_tn(gl_ref[...].astype(BF16), dgkb)
        dbg_ref[...] += _rsum8(dgk)

    rev = lambda i: (nt - 1 - i, 0)
    return pl.pallas_call(
        body, name=name, grid=(nt,),
        in_specs=[pl.BlockSpec((gt, 2 * kw), rev), pl.BlockSpec((gt, vw), rev), pl.BlockSpec((gt, kw), rev),
                  pl.BlockSpec((gt, vw), rev), pl.BlockSpec((nc, heads, dv, 128), lambda i: (nt - 1 - i, 0, 0, 0)),
                  pl.BlockSpec((gt, 128), rev), pl.BlockSpec((128, kw), lambda i: (0, 0))],
        out_shape=[S((t, 2 * kw), BF16), S((t, vw), BF16), S((t, 128), BF16), S((128, kw), F32), S((8, kw), F32)],
        out_specs=[pl.BlockSpec((gt, 2 * kw), rev), pl.BlockSpec((gt, vw), rev), pl.BlockSpec((gt, 128), rev),
                   pl.BlockSpec((128, kw), lambda i: (0, 0)), pl.BlockSpec((8, kw), lambda i: (0, 0))],
        scratch_shapes=[pltpu.VMEM((heads, dv, 128), F32), pltpu.VMEM((gt, kw), F32)],
        compiler_params=_cp("arbitrary"),
    )(qk, v, lg, do, sall, gl, wg)


def _conv_taps(cx_ref, halo_ref, first, cw):
    tm = cx_ref.shape[0]
    u = cx_ref[:, cw:2 * cw].astype(F32) * cx_ref[:, 2 * cw:3 * cw].astype(F32)
    uh = halo_ref[:, cw:2 * cw].astype(F32) * halo_ref[:, 2 * cw:3 * cw].astype(F32)
    uh = jnp.where(first, 0.0, uh)
    row = lax.broadcasted_iota(jnp.int32, (tm, cw), 0)
    u1 = jnp.where(row == 0, uh[15:16, :], pltpu.roll(u, 1, 0))
    u2 = jnp.where(row == 0, uh[14:15, :], jnp.where(row == 1, uh[15:16, :], pltpu.roll(u, 2, 0)))
    return u, u1, u2


def _head_norm(o_h, gn):
    rstd = lax.rsqrt(jnp.mean(o_h * o_h, axis=-1, keepdims=True) + EPS)
    ohat = o_h * rstd
    return ohat, rstd, ohat * gn


def _mix_out(cx, o, go, conv_w, gn, w_out, x, ada9, heads, name, plans=()):
    t, d = x.shape
    cw, vw = conv_w.shape[1], o.shape[1]
    dv = vw // heads
    tm = min(ROW_TILE, t)

    def body(cx_ref, halo_ref, o_ref, go_ref, cwt_ref, gn_ref, w_ref, x_ref, ada_ref, xo_ref, m_ref, y_ref):
        u, u1, u2 = _conv_taps(cx_ref, halo_ref, pl.program_id(0) == 0, cw)
        yc = cwt_ref[0:1, :] * u2 + cwt_ref[1:2, :] * u1 + cwt_ref[2:3, :] * u
        y_ref[:, 0:cw] = (cx_ref[:, 0:cw].astype(F32) * yc).astype(BF16)
        for h in range(heads):
            cols = slice(h * dv, (h + 1) * dv)
            _, _, on = _head_norm(o_ref[:, cols], gn_ref[...])
            g = go_ref[:, cols].astype(F32)
            y_ref[:, cw + h * dv:cw + (h + 1) * dv] = (on * (g * _sigmoid(g))).astype(BF16)
        m = _dot(y_ref[...], w_ref[...])
        m_ref[...] = m.astype(BF16)
        xo_ref[...] = x_ref[...] + ada_ref[5:6, :] * m

    return _pcall(
        body, name, [cx, cx, o, go, conv_w, gn, w_out, x, ada9], grid=(t // tm,),
        in_specs=[pl.BlockSpec((tm, 3 * cw), lambda i: (i, 0)),
                  pl.BlockSpec((16, 3 * cw), lambda i: (jnp.maximum(i * (tm // 16) - 1, 0), 0)),
                  pl.BlockSpec((tm, vw), lambda i: (i, 0)), pl.BlockSpec((tm, vw), lambda i: (i, 0)),
                  pl.BlockSpec((3, cw), lambda i: (0, 0)), pl.BlockSpec((1, dv), lambda i: (0, 0)),
                  pl.BlockSpec((cw + vw, d), lambda i: (0, 0)), pl.BlockSpec((tm, d), lambda i: (i, 0)),
                  pl.BlockSpec((N_ADA, d), lambda i: (0, 0))],
        out_shape=[S((t, d), F32), S((t, d), BF16), S((t, cw + vw), BF16)],
        out_specs=[pl.BlockSpec((tm, d), lambda i: (i, 0)), pl.BlockSpec((tm, d), lambda i: (i, 0)),
                   pl.BlockSpec((tm, cw + vw), lambda i: (i, 0))],
        sem=("parallel",), plans=plans)


def _mix_bwd_a(dxo, m, ada9, w_out, cx, o, go, conv_w, gn, heads, name, plans=()):
    t, d = dxo.shape
    cw, vw = conv_w.shape[1], o.shape[1]
    dv = vw // heads
    tm = min(ROW_TILE, t)

    def body(dxo_ref, m_ref, ada_ref, w_ref, cx_ref, halo_ref, o_ref, go_ref, cwt_ref, gn_ref,
             dm_ref, dyc_ref, dcb_ref, do_ref, dgo_ref, dg_ref, dcw_ref, dgn_ref):
        @pl.when(pl.program_id(0) == 0)
        def _():
            dg_ref[...] = jnp.zeros_like(dg_ref)
            dcw_ref[...] = jnp.zeros_like(dcw_ref)
            dgn_ref[...] = jnp.zeros_like(dgn_ref)

        dxo_t = dxo_ref[...]
        dmb = (ada_ref[5:6, :] * dxo_t).astype(BF16)
        dm_ref[...] = dmb
        dg_ref[...] += _rsum8(dxo_t * m_ref[...].astype(F32))
        dy = _dot_nt(dmb, w_ref[...])
        u, u1, u2 = _conv_taps(cx_ref, halo_ref, pl.program_id(0) == 0, cw)
        yc = cwt_ref[0:1, :] * u2 + cwt_ref[1:2, :] * u1 + cwt_ref[2:3, :] * u
        dyv = dy[:, 0:cw]
        dcb_ref[...] = (dyv * yc).astype(BF16)
        dyc = dyv * cx_ref[:, 0:cw].astype(F32)
        dyc_ref[...] = dyc
        dcw_ref[0] += _rsum8(dyc * u2)
        dcw_ref[1] += _rsum8(dyc * u1)
        dcw_ref[2] += _rsum8(dyc * u)
        for h in range(heads):
            cols = slice(h * dv, (h + 1) * dv)
            ohat, rstd, on = _head_norm(o_ref[:, cols], gn_ref[...])
            g = go_ref[:, cols].astype(F32)
            sg = _sigmoid_tanh(g)
            dyg = dy[:, cw + h * dv:cw + (h + 1) * dv]
            dgo_ref[:, cols] = (dyg * on * (sg * (1.0 + g * (1.0 - sg)))).astype(BF16)
            don = dyg * (g * sg)
            dgn_ref[...] += _rsum8(don * ohat)
            tt = don * gn_ref[...]
            do_ref[:, cols] = (rstd * (tt - ohat * jnp.mean(tt * ohat, axis=-1, keepdims=True))).astype(BF16)

    return _pcall(
        body, name, [dxo, m, ada9, w_out, cx, cx, o, go, conv_w, gn], grid=(t // tm,),
        in_specs=[pl.BlockSpec((tm, d), lambda i: (i, 0)), pl.BlockSpec((tm, d), lambda i: (i, 0)),
                  pl.BlockSpec((N_ADA, d), lambda i: (0, 0)), pl.BlockSpec((cw + vw, d), lambda i: (0, 0)),
                  pl.BlockSpec((tm, 3 * cw), lambda i: (i, 0)),
                  pl.BlockSpec((16, 3 * cw), lambda i: (jnp.maximum(i * (tm // 16) - 1, 0), 0)),
                  pl.BlockSpec((tm, vw), lambda i: (i, 0)), pl.BlockSpec((tm, vw), lambda i: (i, 0)),
                  pl.BlockSpec((3, cw), lambda i: (0, 0)), pl.BlockSpec((1, dv), lambda i: (0, 0))],
        out_shape=[S((t, d), BF16), S((t, cw), F32), S((t, cw), BF16), S((t, vw), BF16), S((t, vw), BF16),
                   S((8, d), F32), S((3, 8, cw), F32), S((8, dv), F32)],
        out_specs=[pl.BlockSpec((tm, d), lambda i: (i, 0)), pl.BlockSpec((tm, cw), lambda i: (i, 0)),
                   pl.BlockSpec((tm, cw), lambda i: (i, 0)), pl.BlockSpec((tm, vw), lambda i: (i, 0)),
                   pl.BlockSpec((tm, vw), lambda i: (i, 0)), pl.BlockSpec((8, d), lambda i: (0, 0)),
                   pl.BlockSpec((3, 8, cw), lambda i: (0, 0, 0)), pl.BlockSpec((8, dv), lambda i: (0, 0))],
        sem=("arbitrary",), plans=plans)


def _mix_bwd_b(dyc, cx, dcb, dqk, dvv, dgo, dgl, conv_w, w, x, dxo, ada9, nrm, name):
    t, d = x.shape
    cw = conv_w.shape[1]
    n = w.shape[0]
    tm = min(ROW_TILE, t)
    nt = t // tm
    pieces = [dcb.shape[1], cw, cw, dqk.shape[1], dvv.shape[1], dgo.shape[1], dgl.shape[1]]
    assert sum(pieces) == n

    def body(dyc_ref, nxt_ref, cx_ref, dcb_ref, dqk_ref, dv_ref, dgo_ref, dgl_ref, cwt_ref, w_ref, x_ref, dxo_ref, ada_ref, n_ref,
             dx_ref, dp_ref, dsh_ref, dsc_ref, dn_ref):
        i = pl.program_id(0)

        @pl.when(i == 0)
        def _():
            dsh_ref[...] = jnp.zeros_like(dsh_ref)
            dsc_ref[...] = jnp.zeros_like(dsc_ref)
            dn_ref[...] = jnp.zeros_like(dn_ref)

        dyc_t = dyc_ref[...]
        nxt = jnp.where(i == nt - 1, 0.0, nxt_ref[...])
        row = lax.broadcasted_iota(jnp.int32, (tm, cw), 0)
        d1 = jnp.where(row == tm - 1, nxt[0:1, :], pltpu.roll(dyc_t, tm - 1, 0))
        d2 = jnp.where(row == tm - 2, nxt[0:1, :], jnp.where(row == tm - 1, nxt[1:2, :], pltpu.roll(dyc_t, tm - 2, 0)))
        du = cwt_ref[2:3, :] * dyc_t + cwt_ref[1:2, :] * d1 + cwt_ref[0:1, :] * d2
        c0 = 0
        dp_ref[:, c0:c0 + cw] = dcb_ref[...]
        dp_ref[:, cw:2 * cw] = (du * cx_ref[:, 2 * cw:3 * cw].astype(F32)).astype(BF16)
        dp_ref[:, 2 * cw:3 * cw] = (du * cx_ref[:, cw:2 * cw].astype(F32)).astype(BF16)
        c0 = 3 * cw
        for ref in (dqk_ref, dv_ref, dgo_ref, dgl_ref):
            wd = ref.shape[1]
            dp_ref[:, c0:c0 + wd] = ref[...]
            c0 += wd
        dh = _dot(dp_ref[...], w_ref[...])
        dx, tsh, tsc, tn = _normmod_bwd(dh, x_ref[...], n_ref[...], ada_ref[4:5, :])
        dx_ref[...] = dxo_ref[...] + dx
        dsh_ref[...] += _rsum8(tsh)
        dsc_ref[...] += _rsum8(tsc)
        dn_ref[...] += _rsum8(tn)

    row_spec = lambda wd: pl.BlockSpec((tm, wd), lambda i: (i, 0))
    vec = pl.BlockSpec((8, d), lambda i: (0, 0))
    return pl.pallas_call(
        body, name=name, grid=(nt,),
        in_specs=[row_spec(cw), pl.BlockSpec((8, cw), lambda i: (jnp.minimum((i + 1) * (tm // 8), t // 8 - 1), 0)),
                  row_spec(3 * cw), row_spec(cw), row_spec(dqk.shape[1]), row_spec(dvv.shape[1]), row_spec(dgo.shape[1]),
                  row_spec(dgl.shape[1]), pl.BlockSpec((3, cw), lambda i: (0, 0)), pl.BlockSpec((n, d), lambda i: (0, 0)),
                  row_spec(d), row_spec(d), pl.BlockSpec((N_ADA, d), lambda i: (0, 0)), pl.BlockSpec((1, d), lambda i: (0, 0))],
        out_shape=[S((t, d), F32), S((t, n), BF16), S((8, d), F32), S((8, d), F32), S((8, d), F32)],
        out_specs=[row_spec(d), row_spec(n), vec, vec, vec],
        compiler_params=_cp("arbitrary"),
    )(dyc, dyc, cx, dcb, dqk, dvv, dgo, dgl, conv_w, w, x, dxo, ada9, nrm)


def _loss_head(x, target, nrm, name):
    t, d = x.shape
    tm = min(ROW_TILE, t)
    nt = t // tm

    def body(x_ref, tg_ref, n_ref, loss_ref, dx_ref, dn_ref, acc_ref):
        i = pl.program_id(0)

        @pl.when(i == 0)
        def _():
            acc_ref[...] = jnp.zeros_like(acc_ref)
            dn_ref[...] = jnp.zeros_like(dn_ref)

        xt = x_ref[...]
        rstd = lax.rsqrt(jnp.mean(xt * xt, axis=-1, keepdims=True) + EPS)
        xhat = xt * rstd
        err = xhat * n_ref[...] - tg_ref[...]
        acc_ref[...] += _rsum8(err * err)
        dy = err * (1.0 / d)
        dn_ref[...] += _rsum8(dy * xhat)
        dxhat = dy * n_ref[...]
        dx_ref[...] = rstd * (dxhat - xhat * jnp.mean(dxhat * xhat, axis=-1, keepdims=True))

        @pl.when(i == nt - 1)
        def _():
            loss_ref[...] = jnp.full(loss_ref.shape, (0.5 / d) * jnp.sum(acc_ref[...]), F32)

    return pl.pallas_call(
        body, name=name, grid=(nt,),
        in_specs=[pl.BlockSpec((tm, d), lambda i: (i, 0)), pl.BlockSpec((tm, d), lambda i: (i, 0)),
                  pl.BlockSpec((1, d), lambda i: (0, 0))],
        out_shape=[S((1, 128), F32), S((t, d), F32), S((8, d), F32)],
        out_specs=[pl.BlockSpec((1, 128), lambda i: (0, 0)), pl.BlockSpec((tm, d), lambda i: (i, 0)),
                   pl.BlockSpec((8, d), lambda i: (0, 0))],
        scratch_shapes=[pltpu.VMEM((8, d), F32)],
        compiler_params=_cp("arbitrary"),
    )(x, target, nrm)


def _pack_smalls(vec_parts, dcw, dbg, dgn, dwg, loss_v, rank, name):
    d = vec_parts[0].shape[1]
    cw, kw, dv = dcw.shape[2], dbg.shape[1], dgn.shape[1]
    nv = len(vec_parts)
    loss_row = nv + 2 + rank * kw // d
    assert 2 * cw == d and cw + kw + dv <= d and (rank * kw) % d == 0 and loss_row < PACK_ROWS
    per_row = d // kw

    def body(*refs):
        vrefs, (dcw_ref, dbg_ref, dgn_ref, dwg_ref, loss_ref, o_ref) = refs[:nv], refs[nv:]
        o_ref[...] = jnp.zeros_like(o_ref)
        o_ref[loss_row:loss_row + 1, 0:loss_ref.shape[1]] = loss_ref[...]
        for r, ref in enumerate(vrefs):
            o_ref[r:r + 1, :] = jnp.sum(ref[...], axis=0, keepdims=True)
        o_ref[nv:nv + 1, 0:cw] = jnp.sum(dcw_ref[0], axis=0, keepdims=True)
        o_ref[nv:nv + 1, cw:2 * cw] = jnp.sum(dcw_ref[1], axis=0, keepdims=True)
        o_ref[nv + 1:nv + 2, 0:cw] = jnp.sum(dcw_ref[2], axis=0, keepdims=True)
        o_ref[nv + 1:nv + 2, cw:cw + kw] = jnp.sum(dbg_ref[...], axis=0, keepdims=True)
        o_ref[nv + 1:nv + 2, cw + kw:cw + kw + dv] = jnp.sum(dgn_ref[...], axis=0, keepdims=True)
        for r in range(rank):
            o_ref[nv + 2 + r // per_row:nv + 3 + r // per_row, (r % per_row) * kw:(r % per_row + 1) * kw] = dwg_ref[r:r + 1, :]

    return pl.pallas_call(body, name=name, out_shape=S((PACK_ROWS, d), F32), compiler_params=_cp())(*vec_parts, dcw, dbg, dgn, dwg, loss_v)


def _sum_slots(a, name):
    def body(a_ref, o_ref):
        acc = a_ref[0]
        for s in range(1, NDEV):
            acc = acc + a_ref[s]
        o_ref[...] = acc

    return pl.pallas_call(body, name=name, out_shape=S(a.shape[1:], F32), compiler_params=_cp())(a)


def _adamw(w, g, m, v):
    m = ADAM_B1 * m + (1.0 - ADAM_B1) * g
    v = ADAM_B2 * v + (1.0 - ADAM_B2) * (g * g)
    m_hat = m / (1.0 - ADAM_B1 ** ADAM_STEP)
    v_hat = v / (1.0 - ADAM_B2 ** ADAM_STEP)
    return -ADAM_LR * (m_hat / (jnp.sqrt(v_hat) + ADAM_EPS) + ADAM_WD * w), m, v


def _adam_slots(recv, w, m, v, name):
    r, c = w.shape
    slots = recv.shape[0]
    tr = _row_tile(r, c)

    def body(recv_ref, w_ref, m_ref, v_ref, g_ref, d_ref, mo_ref, vo_ref):
        g = recv_ref[0].astype(F32)
        for s in range(1, slots):
            g = g + recv_ref[s].astype(F32)
        g_ref[...] = g
        d_ref[...], mo_ref[...], vo_ref[...] = _adamw(w_ref[...], g, m_ref[...], v_ref[...])

    blk = pl.BlockSpec((tr, c), lambda i: (i, 0))
    return pl.pallas_call(
        body, name=name, grid=(r // tr,),
        in_specs=[pl.BlockSpec((slots, tr, c), lambda i: (0, i, 0)), blk, blk, blk],
        out_shape=[S((r, c), F32)] * 4, out_specs=[blk] * 4, compiler_params=_cp("parallel"),
    )(recv, w, m, v)


def _adam_w_ada(act_t, dada, w, m, v, name):
    r, c = w.shape
    tr = 128
    nb = act_t.shape[1]

    def body(a_ref, da_ref, w_ref, m_ref, v_ref, g_ref, d_ref, mo_ref, vo_ref):
        g = a_ref[:, 0:1] * da_ref[0:1, :]
        for b in range(1, nb):
            g = g + a_ref[:, b:b + 1] * da_ref[b:b + 1, :]
        g_ref[...] = g
        d_ref[...], mo_ref[...], vo_ref[...] = _adamw(w_ref[...], g, m_ref[...], v_ref[...])

    blk = pl.BlockSpec((tr, c), lambda i: (i, 0))
    return pl.pallas_call(
        body, name=name, grid=(r // tr,),
        in_specs=[pl.BlockSpec((tr, nb), lambda i: (i, 0)), pl.BlockSpec((nb, c), lambda i: (0, 0)), blk, blk, blk],
        out_shape=[S((r, c), F32)] * 4, out_specs=[blk] * 4, compiler_params=_cp("parallel"),
    )(act_t, dada, w, m, v)


def _adam_smalls(ws, gs, ms, vs, name):
    n = len(ws)

    def body(*refs):
        w_r, g_r, m_r, v_r = (refs[k * n:(k + 1) * n] for k in range(4))
        d_o, m_o, v_o = (refs[(4 + k) * n:(5 + k) * n] for k in range(3))
        for i in range(n):
            d_o[i][...], m_o[i][...], v_o[i][...] = _adamw(w_r[i][...], g_r[i][...], m_r[i][...], v_r[i][...])

    shapes = [S(w.shape, F32) for w in ws]
    outs = pl.pallas_call(body, name=name, out_shape=shapes * 3, compiler_params=_cp())(*ws, *gs, *ms, *vs)
    return outs[:n], outs[n:2 * n], outs[2 * n:]


def kernel(x, c, w_ada, b_ada, norm_ffn1, w_ffn1_in, w_ffn1_out, norm_mix, w_mix_in, conv_w, w_gk2, b_gk, gla_norm, w_mix_out, norm_ffn2, w_ffn2_in, w_ffn2_out, norm_final, loss_target, m_w_ada, m_b_ada, m_norm_ffn1, m_w_ffn1_in, m_w_ffn1_out, m_norm_mix, m_w_mix_in, m_conv_w, m_w_gk2, m_b_gk, m_gla_norm, m_w_mix_out, m_norm_ffn2, m_w_ffn2_in, m_w_ffn2_out, m_norm_final, v_w_ada, v_b_ada, v_norm_ffn1, v_w_ffn1_in, v_w_ffn1_out, v_norm_mix, v_w_mix_in, v_conv_w, v_w_gk2, v_b_gk, v_gla_norm, v_w_mix_out, v_norm_ffn2, v_w_ffn2_in, v_w_ffn2_out, v_norm_final):
    t, d = x.shape[1], x.shape[2]
    x0, tgt = x[0], loss_target[0]
    rank, kw = w_gk2.shape[1], w_gk2.shape[2] * NDEV
    cw = conv_w.shape[2] * NDEV
    dv = gla_norm.shape[1]
    vw = d - cw
    heads = vw // dv
    mix_cols = w_mix_in.shape[2]
    widths = [3 * cw, 2 * kw, vw, vw, 128]
    n_proj = 3 * cw + 2 * kw + 2 * vw + rank
    assert n_proj == mix_cols * NDEV and rank <= 128
    me = 4 * lax.axis_index("x") + 2 * lax.axis_index("y") + lax.axis_index("c")

    core = lax.axis_index("c").astype(jnp.int32).reshape(1)
    bf = lambda a: a[0].astype(BF16)
    bft = lambda a: a[0].T.astype(BF16)
    nb = NDEV // 2

    ((c_all, w1i, cwt_all, wg_all),) = _exchange([_plan_gather([c, bft(w_ffn1_in), conv_w[0], w_gk2[0]])], "gather_first")
    cwt = cwt_all.transpose(1, 0, 2).reshape(conv_w.shape[1], cw)
    wg = jnp.pad(wg_all.transpose(1, 0, 2).reshape(rank, kw), ((0, 128 - rank), (0, 0))).astype(BF16)

    act_all, p_ada = _ada_partial(c_all.reshape(NDEV, d), w_ada[0])
    ((p_all,),) = _exchange([_plan_all_to_all([p_ada], True)], "gather_ada")
    ada9 = _ada_finish(p_all, b_ada).reshape(N_ADA, d)

    (h1, gu1, s1), ((w1o, wmi),) = _ffn_in(x0, ada9, norm_ffn1, w1i, 0, 1, "ffn1_in", plans=[_plan_gather([bf(w_ffn1_out), bft(w_mix_in)])])
    w1o = w1o.reshape(nb, -1, d)
    wmi = jnp.pad(wmi.reshape(n_proj, d), ((0, sum(widths) - n_proj), (0, 0)))
    (x1, f1), ((wmo,),) = _ffn_out(s1, w1o, x0, ada9, 2, 0.5, "ffn1_out", plans=[_plan_gather([bf(w_mix_out)])])
    wmo = wmo.reshape(cw + vw, d)
    (h2, cx, qk, vv, go, gl), ((w2o,),) = _mix_in(x1, ada9, norm_mix, wmi, widths, [BF16, F32, BF16, BF16, F32], "mix_in",
                                                 plans=[_plan_gather([bf(w_ffn2_out)])])
    w2o = w2o.reshape(nb, -1, d)
    w2i_mine = bft(w_ffn2_in)
    half = w2i_mine.shape[0] // 2
    (o, lg, sall), ((w2i,),) = _gla_fwd(qk, vv, gl, wg, b_gk, heads, "gla_fwd", plans=[_plan_gather([w2i_mine], rows=(0, half))])
    (x2, mm, ycat), ((w2i,),) = _mix_out(cx, o, go, cwt, gla_norm, wmo, x1, ada9, heads, "mix_out",
                                         plans=[_plan_gather([w2i_mine], rows=(half, half), into=[w2i])])
    (h3, gu3, s3), _ = _ffn_in(x2, ada9, norm_ffn2, w2i, 6, 7, "ffn2_in")
    (x3, f3), _ = _ffn_out(s3, w2o, x2, ada9, 8, 0.5, "ffn2_out")
    loss_v, dx3, dnf = _loss_head(x3, tgt, norm_final.reshape(1, d), "loss_head")

    dx2, r2i, sum2o, (dsh3, dsc3, dg3, dn3), _, _ = _ffn_backward(
        dx3, x2, h3, gu3, s3, f3, ada9, norm_ffn2, w2i, w2o, (6, 7, 8), core, "ffn2_bwd")
    (dm, dyc, dcb, do, dgo, dg2, dcw, dgn), ((r2o,),) = _mix_bwd_a(dx2, mm, ada9, wmo, cx, o, go, cwt, gla_norm, heads, "mix_bwd_a",
                                                                 plans=[_plan_chip_swap([sum2o])])
    dqk, dvv, dgl, dwg, dbg = _gla_bwd(qk, vv, lg, do, sall, gl, wg, heads, "gla_bwd")
    dx1, dproj, dsh2, dsc2, dnm = _mix_bwd_b(dyc, cx, dcb, dqk, dvv, dgo, dgl, cwt, wmi, x1, dx2, ada9, norm_mix, "mix_bwd_b")
    n_pad = sum(widths)
    tn = n_pad // 5
    dwmi, _ = _tn_matmul(dproj, h2, lambda tk: (tk, tn), lambda sb, k: (k, sb), lambda tk: (tk, d), lambda sb, k: (k, 0),
                         (n_pad, d), (tn, d), lambda sb, k: (sb, 0), 5, "mix_dwin")
    dwmo, _ = _tn_matmul(ycat, dm, lambda tk: (tk, cw + vw), lambda sb, k: (k, 0), lambda tk: (tk, d), lambda sb, k: (k, 0),
                         (cw + vw, d), (cw + vw, d), lambda sb, k: (0, 0), 1, "mix_dwout")
    dwmi = dwmi[:n_proj].reshape(NDEV, mix_cols, d)
    dwmo = dwmo.reshape(NDEV, -1, d)
    dx0, r1i, sum1o, (dsh1, dsc1, dg1, dn1), _, ((rmi, rmo),) = _ffn_backward(
        dx1, x0, h1, gu1, s1, f1, ada9, norm_ffn1, w1i, w1o, (0, 1, 2), core, "ffn1_bwd",
        ds_plans=[_plan_sibling_swap([dwmi, dwmo])],
        dwin_plans=lambda moved: [_plan_chip_swap([_pair_add(dwmi, moved[0][0], core, "mix_dwin_add"),
                                                   _pair_add(dwmo, moved[0][1], core, "mix_dwout_add")])])
    pack = _pack_smalls([dn1, dnm, dn3, dnf, dsh1, dsc1, dg1, dsh2, dsc2, dg2, dsh3, dsc3, dg3], dcw, dbg, dgn, dwg, loss_v, rank, "pack_smalls")
    (r1o,), (pack_all,) = _exchange([_plan_chip_swap([sum1o]), _plan_all_to_all([pack], True)], "grads_last")
    tot = _sum_slots(pack_all, "sum_smalls")

    res = {}
    for nm, recv, w, m, v in (("w_ffn1_out", r1o, w_ffn1_out, m_w_ffn1_out, v_w_ffn1_out), ("w_mix_out", rmo, w_mix_out, m_w_mix_out, v_w_mix_out),
                              ("w_ffn2_out", r2o, w_ffn2_out, m_w_ffn2_out, v_w_ffn2_out)):
        res[nm] = [a[None] for a in _adam_slots(recv, w[0], m[0], v[0], "adam_" + nm)]
    for nm, recv, w, m, v in (("w_ffn1_in", r1i, w_ffn1_in, m_w_ffn1_in, v_w_ffn1_in), ("w_mix_in", rmi, w_mix_in, m_w_mix_in, v_w_mix_in),
                              ("w_ffn2_in", r2i, w_ffn2_in, m_w_ffn2_in, v_w_ffn2_in)):
        res[nm] = [a.T[None] for a in _adam_slots(recv, w[0].T, m[0].T, v[0].T, "adam_" + nm)]

    cols_ada = w_ada.shape[2]
    dada_all = pack_all[:, 4:4 + N_ADA, :].reshape(NDEV, N_ADA * d)
    dada_mine = lax.dynamic_slice_in_dim(dada_all, me * cols_ada, cols_ada, axis=1)
    res["w_ada"] = [a[None] for a in _adam_w_ada(act_all.T, dada_mine, w_ada[0], m_w_ada[0], v_w_ada[0], "adam_w_ada")]

    nv = 4 + N_ADA
    g_small = {
        "b_ada": tot[4:nv].reshape(1, N_ADA * d),
        "norm_ffn1": tot[0:1], "norm_mix": tot[1:2], "norm_ffn2": tot[2:3], "norm_final": tot[3:4],
        "conv_w": lax.dynamic_slice_in_dim(
            jnp.concatenate([tot[nv:nv + 1, 0:cw], tot[nv:nv + 1, cw:2 * cw], tot[nv + 1:nv + 2, 0:cw]], axis=0), me * (cw // NDEV), cw // NDEV, axis=1),
        "w_gk2": lax.dynamic_slice_in_dim(tot[nv + 2:nv + 2 + rank * kw // d].reshape(rank, kw), me * (kw // NDEV), kw // NDEV, axis=1),
        "b_gk": tot[nv + 1:nv + 2, cw:cw + kw],
        "gla_norm": tot[nv + 1:nv + 2, cw + kw:cw + kw + dv],
    }
    small = {"b_ada": (b_ada, m_b_ada, v_b_ada), "norm_ffn1": (norm_ffn1, m_norm_ffn1, v_norm_ffn1), "norm_mix": (norm_mix, m_norm_mix, v_norm_mix),
             "norm_ffn2": (norm_ffn2, m_norm_ffn2, v_norm_ffn2), "norm_final": (norm_final, m_norm_final, v_norm_final),
             "conv_w": (conv_w, m_conv_w, v_conv_w), "w_gk2": (w_gk2, m_w_gk2, v_w_gk2), "b_gk": (b_gk, m_b_gk, v_b_gk),
             "gla_norm": (gla_norm, m_gla_norm, v_gla_norm)}
    names = list(small)
    flat = lambda a: a.reshape(-1, a.shape[-1])
    dl, mo, vo = _adam_smalls([flat(small[n][0]) for n in names], [g_small[n] for n in names],
                              [flat(small[n][1]) for n in names], [flat(small[n][2]) for n in names], "adam_smalls")
    for i, n in enumerate(names):
        shp = small[n][0].shape
        res[n] = [g_small[n].reshape(shp), dl[i].reshape(shp), mo[i].reshape(shp), vo[i].reshape(shp)]

    loss = tot[nv + 2 + rank * kw // d, 0]
    order = ["w_ada", "b_ada", "norm_ffn1", "w_ffn1_in", "w_ffn1_out", "norm_mix", "w_mix_in", "conv_w", "w_gk2", "b_gk", "gla_norm",
             "w_mix_out", "norm_ffn2", "w_ffn2_in", "w_ffn2_out", "norm_final"]
    return (loss, dx0[None], *[res[n][0] for n in order], *[res[n][1] for n in order], *[res[n][2] for n in order], *[res[n][3] for n in order])
```

```python
import collections
import functools

import jax
import jax.numpy as jnp
from jax import lax
from jax.experimental import pallas as pl
from jax.experimental.pallas import tpu as pltpu

F32 = jnp.float32
BF16 = jnp.bfloat16
S = jax.ShapeDtypeStruct

NDEV = 8
EPS = 1e-6
GATE_NORMALIZER = 16.0
CHUNK = 128
N_ADA = 9
ADAM_LR, ADAM_B1, ADAM_B2, ADAM_EPS, ADAM_WD, ADAM_STEP = 0.001, 0.9, 0.999, 1e-08, 0.01, 10
V7X_VMEM_LIMIT = 56 * 1024 * 1024
ROW_TILE = 512
WIDE_ROW_TILE = 1024
K_TILE = 1024
PACK_ROWS = 24
ANY = pl.BlockSpec(memory_space=pl.ANY)


def _cp(*sem):
    return pltpu.CompilerParams(dimension_semantics=sem or None, vmem_limit_bytes=V7X_VMEM_LIMIT)


def _dot(a, b):
    return jnp.dot(a, b, preferred_element_type=F32)


def _dot_nt(a, b):
    return lax.dot_general(a, b, (((1,), (1,)), ((), ())), preferred_element_type=F32)


def _dot_tn(a, b):
    return lax.dot_general(a, b, (((0,), (0,)), ((), ())), preferred_element_type=F32)


def _rsum8(a):
    r, c = a.shape
    return jnp.sum(a.reshape(r // 8, 8, c), axis=0)


def _row_tile(r, c):
    for cand in (256, 128, 176, 88, 64, 32, 16, 8):
        if r % cand == 0 and cand * c * 4 <= 1024 * 1024:
            return cand
    return r


def _sigmoid(x):
    return 1.0 / (1.0 + jnp.exp(-x))


def _sigmoid_tanh(x):
    return 0.5 * jnp.tanh(0.5 * x) + 0.5


def _normmod(x, nrm, sc, sh):
    rstd = lax.rsqrt(jnp.mean(x * x, axis=-1, keepdims=True) + EPS)
    xhat = x * rstd
    return xhat, rstd, (xhat * nrm) * (1.0 + sc) + sh


def _normmod_bwd(dh, x, nrm, sc):
    rstd = lax.rsqrt(jnp.mean(x * x, axis=-1, keepdims=True) + EPS)
    xhat = x * rstd
    dxhat = dh * (nrm * (1.0 + sc))
    dx = rstd * (dxhat - xhat * jnp.mean(dxhat * xhat, axis=-1, keepdims=True))
    return dx, dh, dh * (xhat * nrm), dh * ((1.0 + sc) * xhat)


def _place():
    x, y, c = lax.axis_index("x"), lax.axis_index("y"), lax.axis_index("c")
    return x, y, c, 4 * x + 2 * y + c


def _peer(x, y, c, k):
    px = 1 - x if k & 4 else x
    py = 1 - y if k & 2 else y
    pc = 1 - c if k & 1 else c
    return (px, py, pc), 4 * px + 2 * py + pc


def _remote(src, dst, send_sem, recv_sem, peer):
    return pltpu.make_async_remote_copy(src_ref=src, dst_ref=dst, send_sem=send_sem, recv_sem=recv_sem,
                                        device_id=peer, device_id_type=pl.DeviceIdType.MESH)


_Plan = collections.namedtuple("_Plan", "inputs out_shapes sem_shapes start finish aliases", defaults=({},))


def _plan_all_to_all(xs, gather):
    n = len(xs)

    def copies(ins, outs, sems, landed):
        send_sems, recv_sems, local_sems = sems
        x, y, c, me = _place()
        local = [pltpu.make_async_copy(ins[i] if gather else ins[i].at[me], outs[i].at[me], local_sems.at[i]) for i in range(n)]
        remote = []
        for k in range(1, NDEV):
            peer, pid = _peer(x, y, c, k)
            for i in range(n):
                remote.append(_remote(ins[i] if gather else ins[i].at[pid], outs[i].at[pid if landed else me],
                                      send_sems.at[i, k - 1], recv_sems.at[i, k - 1], peer))
        return local, remote

    def start(ins, outs, sems):
        local, remote = copies(ins, outs, sems, False)
        for cp in local + remote:
            cp.start()

    def finish(ins, outs, sems):
        local, remote = copies(ins, outs, sems, True)
        for cp in remote + local:
            cp.wait()

    return _Plan(list(xs), [S((NDEV,) + a.shape, a.dtype) if gather else S(a.shape, a.dtype) for a in xs],
                 [pltpu.SemaphoreType.DMA((n, NDEV - 1)), pltpu.SemaphoreType.DMA((n, NDEV - 1)), pltpu.SemaphoreType.DMA((n,))],
                 start, finish)


def _other_chips(x, y):
    return [(1 - x, y), (x, 1 - y), (1 - x, 1 - y)]


def _plan_gather(xs, rows=None, into=None):
    n = len(xs)

    def copies(ins, outs, sems, rest):
        send_sems, recv_sems, local_sems = sems
        x, y, c, me = _place()
        sib, sib_id = (x, y, 1 - c), 4 * x + 2 * y + 1 - c
        chips = _other_chips(x, y)
        mine = lambda i: ins[i] if rows is None else ins[i].at[pl.ds(*rows)]
        slot_of = lambda i, s: outs[i].at[s] if rows is None else outs[i].at[s, pl.ds(*rows)]
        local = [pltpu.make_async_copy(mine(i), slot_of(i, me), local_sems.at[i]) for i in range(n)]
        first = [_remote(mine(i), slot_of(i, me), send_sems.at[i, 0], recv_sems.at[i, 0], sib) for i in range(n)]
        first += [_remote(mine(i), slot_of(i, me), send_sems.at[i, 1 + j], recv_sems.at[i, 1 + j], (px, py, c))
                  for j, (px, py) in enumerate(chips) for i in range(n)]
        if not rest:
            return local, first
        from_sibling = [_remote(mine(i), slot_of(i, sib_id), send_sems.at[i, 0], recv_sems.at[i, 0], sib) for i in range(n)]
        arrive, forward = [], []
        for j, (px, py) in enumerate(chips):
            s = 4 * px + 2 * py
            arrive.append([_remote(mine(i), slot_of(i, s + c), send_sems.at[i, 1 + j], recv_sems.at[i, 1 + j], (px, py, c)) for i in range(n)])
            forward.append([_remote(slot_of(i, s + c), slot_of(i, s + c), send_sems.at[i, 4 + j], recv_sems.at[i, 4 + j], sib) for i in range(n)])
            from_sibling += [_remote(mine(i), slot_of(i, s + 1 - c), send_sems.at[i, 4 + j], recv_sems.at[i, 4 + j], sib) for i in range(n)]
        return local, first, arrive, forward, from_sibling

    def start(ins, outs, sems):
        local, first = copies(ins, outs, sems, False)
        for cp in local + first:
            cp.start()

    def finish(ins, outs, sems):
        local, first, arrive, forward, from_sibling = copies(ins, outs, sems, True)
        for landed, onward in zip(arrive, forward):
            for cp in landed:
                cp.wait_recv()
            for cp in onward:
                cp.start()
        for cp in from_sibling:
            cp.wait_recv()
        for cp in first + [cp for onward in forward for cp in onward]:
            cp.wait_send()
        for cp in local:
            cp.wait()

    return _Plan(list(xs) + list(into or []), [S((NDEV,) + a.shape, a.dtype) for a in xs],
                 [pltpu.SemaphoreType.DMA((n, NDEV - 1)), pltpu.SemaphoreType.DMA((n, NDEV - 1)), pltpu.SemaphoreType.DMA((n,))],
                 start, finish, {n + i: i for i in range(len(into or []))})


def _plan_sibling_swap(gs):
    n = len(gs)

    def copies(ins, outs, sems):
        send_sems, recv_sems = sems
        x, y, c, _ = _place()
        return [_remote(ins[i].at[2 * j + 1 - c], outs[i].at[j], send_sems.at[i, j], recv_sems.at[i, j], (x, y, 1 - c))
                for i in range(n) for j in range(NDEV // 2)]

    def start(ins, outs, sems):
        for cp in copies(ins, outs, sems):
            cp.start()

    def finish(ins, outs, sems):
        for cp in copies(ins, outs, sems):
            cp.wait()

    return _Plan(list(gs), [S((NDEV // 2,) + a.shape[1:], a.dtype) for a in gs],
                 [pltpu.SemaphoreType.DMA((n, NDEV // 2)), pltpu.SemaphoreType.DMA((n, NDEV // 2))], start, finish)


def _pair_add(g, r1, core, name):
    _, r, c = g.shape
    tr = _row_tile(r, c)

    def body(core_ref, g_ref, r_ref, o_ref):
        o_ref[...] = (g_ref[...].astype(F32) + r_ref[...].astype(F32)).astype(BF16)

    return pl.pallas_call(
        body, name=name,
        grid_spec=pltpu.PrefetchScalarGridSpec(
            num_scalar_prefetch=1, grid=(NDEV // 2, r // tr),
            in_specs=[pl.BlockSpec((None, tr, c), lambda j, k, core_ref: (2 * j + core_ref[0], k, 0)),
                      pl.BlockSpec((None, tr, c), lambda j, k, core_ref: (j, k, 0))],
            out_specs=pl.BlockSpec((None, tr, c), lambda j, k, core_ref: (j, k, 0))),
        out_shape=S((NDEV // 2, r, c), BF16), compiler_params=_cp("parallel", "parallel"),
    )(core, g, r1)


def _plan_chip_swap(ps):
    n = len(ps)

    def copies(ins, outs, sems, landed):
        send_sems, recv_sems, local_sems = sems
        x, y, c, _ = _place()
        mine = 2 * x + y
        local = [pltpu.make_async_copy(ins[i].at[mine], outs[i].at[mine], local_sems.at[i]) for i in range(n)]
        remote = [_remote(ins[i].at[2 * px + py], outs[i].at[2 * px + py if landed else mine], send_sems.at[i, j], recv_sems.at[i, j], (px, py, c))
                  for j, (px, py) in enumerate(_other_chips(x, y)) for i in range(n)]
        return local, remote

    def start(ins, outs, sems):
        local, remote = copies(ins, outs, sems, False)
        for cp in local + remote:
            cp.start()

    def finish(ins, outs, sems):
        local, remote = copies(ins, outs, sems, True)
        for cp in remote + local:
            cp.wait()

    return _Plan(list(ps), [S(a.shape, a.dtype) for a in ps],
                 [pltpu.SemaphoreType.DMA((n, 3)), pltpu.SemaphoreType.DMA((n, 3)), pltpu.SemaphoreType.DMA((n,))], start, finish)


def _pcall(body, name, args, in_specs, out_shape, out_specs, grid=(), scratch_shapes=(), sem=(), plans=()):
    n_in, n_out, n_scr = len(args), len(out_shape), len(scratch_shapes)
    counts = [(len(p.inputs), len(p.out_shapes), len(p.sem_shapes)) for p in plans]
    c_args = [a for p in plans for a in p.inputs]
    c_outs = [s for p in plans for s in p.out_shapes]
    c_sems = [s for p in plans for s in p.sem_shapes]

    def wrapped(*refs):
        cuts = [n_in, len(c_args), n_out, len(c_outs), n_scr, len(c_sems)]
        ins, c_in, outs, c_out, scr, c_sem = [refs[sum(cuts[:k]):sum(cuts[:k + 1])] for k in range(6)]

        def halves(which):
            a = b = s = 0
            for p, (na, nb, ns) in zip(plans, counts):
                getattr(p, which)(c_in[a:a + na], c_out[b:b + nb], c_sem[s:s + ns])
                a, b, s = a + na, b + nb, s + ns

        if not plans:
            body(*ins, *outs, *scr)
        elif not grid:
            halves("start")
            body(*ins, *outs, *scr)
            halves("finish")
        else:
            first = functools.reduce(jnp.logical_and, [pl.program_id(a) == 0 for a in range(len(grid))])
            last = functools.reduce(jnp.logical_and, [pl.program_id(a) == grid[a] - 1 for a in range(len(grid))])
            pl.when(first)(lambda: halves("start"))
            body(*ins, *outs, *scr)
            pl.when(last)(lambda: halves("finish"))

    aliases, a, b = {}, n_in, n_out
    for p, (na, nb, _) in zip(plans, counts):
        aliases.update({a + k: b + v for k, v in p.aliases.items()})
        a, b = a + na, b + nb
    res = pl.pallas_call(
        wrapped, name=name, grid=grid, in_specs=list(in_specs) + [ANY] * len(c_args),
        out_shape=list(out_shape) + c_outs, out_specs=list(out_specs) + [ANY] * len(c_outs),
        scratch_shapes=list(scratch_shapes) + c_sems, input_output_aliases=aliases,
        compiler_params=_cp(*(("arbitrary",) * len(grid) if plans else sem)),
    )(*args, *c_args)
    c_res, b = [], n_out
    for _, nb, _ in counts:
        c_res.append(res[b:b + nb])
        b += nb
    return res[:n_out], c_res


def _exchange(plans, name):
    return _pcall(lambda: None, name, [], [], [], [], plans=plans)[1]


def _ada_rows(c, w_ada, b_ada, name, plans=()):
    d, cols = c.shape[1], w_ada.shape[1]
    gather_c = _plan_all_to_all([c], True)
    gather_p = _plan_all_to_all([S((NDEV, cols), F32)], True)
    n_sem = len(gather_c.sem_shapes)

    def body(c_ref, w_ref, b_ref, ada_ref, act_ref, c_all, p_mine, p_all, *sems):
        gather_c.start([c_ref], [c_all], sems[:n_sem])
        gather_c.finish([c_ref], [c_all], sems[:n_sem])
        for s in range(NDEV):
            cc = c_all[s]
            act_ref[s:s + 1, :] = cc * _sigmoid(cc)
        p_mine[...] = _dot(act_ref[...].astype(BF16), w_ref[...].astype(BF16))
        gather_p.start([p_mine], [p_all], sems[n_sem:])
        gather_p.finish([p_mine], [p_all], sems[n_sem:])
        me = _place()[3]
        for s in range(NDEV):
            ada_ref[:, s * cols:(s + 1) * cols] = p_all[s, pl.ds(me, 1), :] + b_ref[:, s * cols:(s + 1) * cols]

    whole = pl.BlockSpec(memory_space=pltpu.VMEM)
    return _pcall(body, name, [c, w_ada, b_ada], [whole] * 3, [S((1, NDEV * cols), F32), S((NDEV, d), F32)], [whole] * 2,
                  scratch_shapes=[pltpu.VMEM((NDEV,) + c.shape, F32), pltpu.VMEM((NDEV, cols), F32), pltpu.VMEM((NDEV, NDEV, cols), F32)]
                  + gather_c.sem_shapes + gather_p.sem_shapes, plans=plans)


def _ffn_in(x, ada9, nrm, w_in, sh_row, sc_row, name, plans=()):
    t, d = x.shape
    nb, bw = w_in.shape[0] // 2, w_in.shape[1]
    tm = min(WIDE_ROW_TILE, t)

    def body(x_ref, ada_ref, n_ref, wg_ref, wu_ref, h_ref, gu_ref, s_ref):
        @pl.when(pl.program_id(1) == 0)
        def _():
            _, _, h = _normmod(x_ref[...], n_ref[...], ada_ref[sc_row:sc_row + 1, :], ada_ref[sh_row:sh_row + 1, :])
            h_ref[...] = h.astype(BF16)

        h = h_ref[...]
        g = _dot_nt(h, wg_ref[...])
        u = _dot_nt(h, wu_ref[...])
        gu_ref[0] = g.astype(BF16)
        gu_ref[1] = u.astype(BF16)
        s_ref[...] = (g * _sigmoid(g) * u).astype(BF16)

    return _pcall(
        body, name, [x, ada9, nrm, w_in, w_in], grid=(t // tm, nb),
        in_specs=[pl.BlockSpec((tm, d), lambda i, j: (i, 0)), pl.BlockSpec((N_ADA, d), lambda i, j: (0, 0)),
                  pl.BlockSpec((1, d), lambda i, j: (0, 0)),
                  pl.BlockSpec((None, bw, d), lambda i, j: (j, 0, 0)), pl.BlockSpec((None, bw, d), lambda i, j: (j + nb, 0, 0))],
        out_shape=[S((t, d), BF16), S((2, nb, t, bw), BF16), S((nb, t, bw), BF16)],
        out_specs=[pl.BlockSpec((tm, d), lambda i, j: (i, 0)), pl.BlockSpec((2, None, tm, bw), lambda i, j: (0, j, i, 0)),
                   pl.BlockSpec((None, tm, bw), lambda i, j: (j, i, 0))],
        sem=("parallel", "arbitrary"), plans=plans)


def _ffn_out(s, w_out, x, ada9, g_row, res_scale, name, plans=()):
    nb, t, bw = s.shape
    d = x.shape[1]
    tm = min(ROW_TILE, t)

    def body(s_ref, w_ref, x_ref, ada_ref, xo_ref, f_ref):
        acc = _dot(s_ref[0], w_ref[0])
        for b in range(1, nb):
            acc = acc + _dot(s_ref[b], w_ref[b])
        f_ref[...] = acc.astype(BF16)
        xo_ref[...] = x_ref[...] + (res_scale * ada_ref[g_row:g_row + 1, :]) * acc

    return _pcall(
        body, name, [s, w_out, x, ada9], grid=(t // tm,),
        in_specs=[pl.BlockSpec((nb, tm, bw), lambda i: (0, i, 0)), pl.BlockSpec((nb, bw, d), lambda i: (0, 0, 0)),
                  pl.BlockSpec((tm, d), lambda i: (i, 0)), pl.BlockSpec((N_ADA, d), lambda i: (0, 0))],
        out_shape=[S((t, d), F32), S((t, d), BF16)],
        out_specs=[pl.BlockSpec((tm, d), lambda i: (i, 0)), pl.BlockSpec((tm, d), lambda i: (i, 0))],
        sem=("parallel",), plans=plans)


def _ffn_bwd_ds(dxo, f, ada9, w_out, gu, g_row, res_scale, name, plans=()):
    t, d = dxo.shape
    nb, bw = w_out.shape[0], w_out.shape[1]
    tm = min(WIDE_ROW_TILE, t)

    def body(dxo_ref, f_ref, ada_ref, w_ref, gu_ref, df_ref, da_ref, dg_ref):
        i, j = pl.program_id(0), pl.program_id(1)

        @pl.when((i == 0) & (j == 0))
        def _():
            dg_ref[...] = jnp.zeros_like(dg_ref)

        @pl.when(j == 0)
        def _():
            dxo_t = dxo_ref[...]
            df_ref[...] = ((res_scale * ada_ref[g_row:g_row + 1, :]) * dxo_t).astype(BF16)
            dg_ref[...] += res_scale * _rsum8(dxo_t * f_ref[...].astype(F32))

        ds = _dot_nt(df_ref[...], w_ref[...])
        g = gu_ref[0].astype(F32)
        u = gu_ref[1].astype(F32)
        sg = _sigmoid_tanh(g)
        da_ref[0] = (ds * u * (sg * (1.0 + g * (1.0 - sg)))).astype(BF16)
        da_ref[1] = (ds * (g * sg)).astype(BF16)

    return _pcall(
        body, name, [dxo, f, ada9, w_out, gu], grid=(t // tm, nb),
        in_specs=[pl.BlockSpec((tm, d), lambda i, j: (i, 0)), pl.BlockSpec((tm, d), lambda i, j: (i, 0)),
                  pl.BlockSpec((N_ADA, d), lambda i, j: (0, 0)), pl.BlockSpec((None, bw, d), lambda i, j: (j, 0, 0)),
                  pl.BlockSpec((2, None, tm, bw), lambda i, j: (0, j, i, 0))],
        out_shape=[S((t, d), BF16), S((2, nb, t, bw), BF16), S((8, d), F32)],
        out_specs=[pl.BlockSpec((tm, d), lambda i, j: (i, 0)), pl.BlockSpec((2, None, tm, bw), lambda i, j: (0, j, i, 0)),
                   pl.BlockSpec((8, d), lambda i, j: (0, 0))],
        sem=("arbitrary", "arbitrary"), plans=plans)


def _ffn_bwd_dh(da, w_in, x, dxo, ada9, nrm, sh_row, sc_row, name, plans=()):
    t, d = x.shape
    nb, bw = w_in.shape[0] // 2, w_in.shape[1]
    tm = min(WIDE_ROW_TILE, t)

    def body(da_ref, wg_ref, wu_ref, x_ref, dxo_ref, ada_ref, n_ref, dx_ref, dsh_ref, dsc_ref, dn_ref, acc_ref):
        i, j = pl.program_id(0), pl.program_id(1)

        @pl.when((i == 0) & (j == 0))
        def _():
            dsh_ref[...] = jnp.zeros_like(dsh_ref)
            dsc_ref[...] = jnp.zeros_like(dsc_ref)
            dn_ref[...] = jnp.zeros_like(dn_ref)

        part = _dot(da_ref[0], wg_ref[...]) + _dot(da_ref[1], wu_ref[...])

        @pl.when(j == 0)
        def _():
            acc_ref[...] = part

        @pl.when(j > 0)
        def _():
            acc_ref[...] += part

        @pl.when(j == nb - 1)
        def _():
            for r0 in range(0, tm, min(256, tm)):
                rows = slice(r0, r0 + min(256, tm))
                dx, tsh, tsc, tn = _normmod_bwd(acc_ref[rows, :], x_ref[rows, :], n_ref[...], ada_ref[sc_row:sc_row + 1, :])
                dx_ref[rows, :] = dxo_ref[rows, :] + dx
                dsh_ref[...] += _rsum8(tsh)
                dsc_ref[...] += _rsum8(tsc)
                dn_ref[...] += _rsum8(tn)

    vec = pl.BlockSpec((8, d), lambda i, j: (0, 0))
    return _pcall(
        body, name, [da, w_in, w_in, x, dxo, ada9, nrm], grid=(t // tm, nb),
        in_specs=[pl.BlockSpec((2, None, tm, bw), lambda i, j: (0, j, i, 0)),
                  pl.BlockSpec((None, bw, d), lambda i, j: (j, 0, 0)), pl.BlockSpec((None, bw, d), lambda i, j: (j + nb, 0, 0)),
                  pl.BlockSpec((tm, d), lambda i, j: (i, 0)), pl.BlockSpec((tm, d), lambda i, j: (i, 0)),
                  pl.BlockSpec((N_ADA, d), lambda i, j: (0, 0)), pl.BlockSpec((1, d), lambda i, j: (0, 0))],
        out_shape=[S((t, d), F32), S((8, d), F32), S((8, d), F32), S((8, d), F32)],
        out_specs=[pl.BlockSpec((tm, d), lambda i, j: (i, 0)), vec, vec, vec],
        scratch_shapes=[pltpu.VMEM((tm, d), F32)],
        sem=("arbitrary", "arbitrary"), plans=plans)


def _tn_matmul(a, b, a_block, a_map, b_block, b_map, out_shape, out_block, out_map, nblk, name, plans=()):
    t = a.shape[-2]
    tk = min(K_TILE, t)
    nk = t // tk

    def body(a_ref, b_ref, o_ref, acc_ref):
        k = pl.program_id(1)
        for q in (range(a_ref.shape[0]) if len(a_ref.shape) == 3 else [Ellipsis]):
            part = _dot_tn(a_ref[q], b_ref[...])

            @pl.when(k == 0)
            def _():
                acc_ref[q] = part

            @pl.when(k > 0)
            def _():
                acc_ref[q] += part

        @pl.when(k == nk - 1)
        def _():
            o_ref[...] = acc_ref[...].astype(BF16)

    (out,), moved = _pcall(
        body, name, [a, b], grid=(nblk, nk),
        in_specs=[pl.BlockSpec(a_block(tk), a_map), pl.BlockSpec(b_block(tk), b_map)],
        out_shape=[S(out_shape, BF16)], out_specs=[pl.BlockSpec(out_block, out_map)],
        scratch_shapes=[pltpu.VMEM(tuple(n for n in out_block if n is not None), F32)],
        sem=("parallel", "arbitrary"), plans=plans)
    return out, moved


def _ffn_backward(dxo, x_in, h, gu, s, f, ada9, nrm, w_in, w_out, rows, core, name, ds_plans=(), dwin_plans=()):
    sh_row, sc_row, g_row = rows
    nb, t, bw = s.shape
    d = x_in.shape[1]
    (df, da, dg), ds_moved = _ffn_bwd_ds(dxo, f, ada9, w_out, gu, g_row, 0.5, name + "_ds", plans=ds_plans)
    dw_in, dwin_moved = _tn_matmul(
        da.reshape(2 * nb, t, bw), h, lambda tk: (2, tk, bw), lambda sb, k: (sb, k, 0), lambda tk: (tk, d), lambda sb, k: (k, 0),
        (2 * nb, bw, d), (2, bw, d), lambda sb, k: (sb, 0, 0), nb, name + "_dwin",
        plans=dwin_plans(ds_moved) if callable(dwin_plans) else dwin_plans)
    dw_out, ((half_in,),) = _tn_matmul(
        s, df, lambda tk: (None, tk, bw), lambda sb, k: (sb, k, 0), lambda tk: (tk, d), lambda sb, k: (k, 0),
        (nb, bw, d), (None, bw, d), lambda sb, k: (sb, 0, 0), nb, name + "_dwout", plans=[_plan_sibling_swap([dw_in])])
    dw_out = dw_out.reshape(NDEV, -1, d)
    sum_in = _pair_add(dw_in, half_in, core, name + "_dwin_add")
    (dx, dsh, dsc, dn), ((recv_in,), (half_out,)) = _ffn_bwd_dh(
        da, w_in, x_in, dxo, ada9, nrm, sh_row, sc_row, name + "_dh", plans=[_plan_chip_swap([sum_in]), _plan_sibling_swap([dw_out])])
    sum_out = _pair_add(dw_out, half_out, core, name + "_dwout_add")
    return dx, recv_in, sum_out, (dsh, dsc, dg, dn), ds_moved, dwin_moved


def _mix_in(x, ada9, nrm, w, widths, dts, name, plans=()):
    t, d = x.shape
    n = w.shape[0]
    tm = min(ROW_TILE, t)
    starts = [sum(widths[:i]) for i in range(len(widths))]

    def body(x_ref, ada_ref, n_ref, w_ref, h_ref, *out_refs):
        _, _, h = _normmod(x_ref[...], n_ref[...], ada_ref[4:5, :], ada_ref[3:4, :])
        hb = h.astype(BF16)
        h_ref[...] = hb
        for o_ref, st, wd in zip(out_refs, starts, widths):
            o_ref[...] = _dot_nt(hb, w_ref[st:st + wd, :]).astype(o_ref.dtype)

    return _pcall(
        body, name, [x, ada9, nrm, w], grid=(t // tm,),
        in_specs=[pl.BlockSpec((tm, d), lambda i: (i, 0)), pl.BlockSpec((N_ADA, d), lambda i: (0, 0)),
                  pl.BlockSpec((1, d), lambda i: (0, 0)), pl.BlockSpec((n, d), lambda i: (0, 0))],
        out_shape=[S((t, d), BF16)] + [S((t, wd), dt) for wd, dt in zip(widths, dts)],
        out_specs=[pl.BlockSpec((tm, d), lambda i: (i, 0))] + [pl.BlockSpec((tm, wd), lambda i: (i, 0)) for wd in widths],
        sem=("parallel",), plans=plans)


def _tri(lower):
    r = lax.broadcasted_iota(jnp.int32, (CHUNK, CHUNK), 0)
    c = lax.broadcasted_iota(jnp.int32, (CHUNK, CHUNK), 1)
    return (r >= c) if lower else (c >= r)


def _dot_01(m, x):
    hi = x.astype(BF16)
    r1 = x - hi.astype(F32)
    mid = r1.astype(BF16)
    lo = (r1 - mid.astype(F32)).astype(BF16)
    return _dot(m, hi) + _dot(m, mid) + _dot(m, lo)


def _gla_chunk_terms(q, k, lg, low01):
    b = _dot_01(low01, lg)
    bl = b[CHUNK - 1:CHUNK, :]
    r = 0.5 * bl
    eb, ebl, em, en = jnp.exp(b), jnp.exp(bl - b), jnp.exp(b - r), jnp.exp(r - b)
    return eb, ebl, em, en, jnp.exp(bl), q * eb, k * ebl, q * em, k * en


def _scores(qm_h, knp, qk1_h):
    r = lax.broadcasted_iota(jnp.int32, (CHUNK, CHUNK), 0)
    c = lax.broadcasted_iota(jnp.int32, (CHUNK, CHUNK), 1)
    p = jnp.where(r > c, _dot_nt(qm_h, knp), 0.0)
    return jnp.where(r == c, jnp.sum(qk1_h, axis=1, keepdims=True), p)


def _gla_fwd(qk, v, gl, wg, bg, heads, name, plans=()):
    t = qk.shape[0]
    kw, vw = qk.shape[1] // 2, v.shape[1]
    dk, dv = kw // heads, vw // heads
    assert dk == 64 and dv == 128 and kw % 128 == 0
    gt = min(ROW_TILE, t)
    nc = gt // CHUNK
    scale = dk ** -0.5

    def body(qk_ref, v_ref, gl_ref, wg_ref, bg_ref, o_ref, lg_ref, sall_ref, st_ref):
        @pl.when(pl.program_id(0) == 0)
        def _():
            st_ref[...] = jnp.zeros_like(st_ref)

        gk = _dot(gl_ref[...].astype(BF16), wg_ref[...]) + bg_ref[...]
        lg_ref[...] = (jnp.minimum(gk, 0.0) - jnp.log(1.0 + jnp.exp(-jnp.abs(gk)))) / GATE_NORMALIZER
        low01 = _tri(True).astype(BF16)
        lane = lax.broadcasted_iota(jnp.int32, (CHUNK, 128), 1)

        def chunk(ci, carry):
            rows = pl.ds(pl.multiple_of(ci * CHUNK, CHUNK), CHUNK)
            q = qk_ref[rows, 0:kw] * scale
            k = qk_ref[rows, kw:2 * kw]
            qk1 = q.astype(BF16).astype(F32) * k.astype(BF16).astype(F32)
            eb, ebl, em, en, ebl_row, qe, ke, qm, kn = _gla_chunk_terms(q, k, lg_ref[rows, :], low01)
            for h in range(heads):
                lanes = slice(128 * (h // 2), 128 * (h // 2) + 128)
                own = (lane < 64) if h % 2 == 0 else (lane >= 64)
                knp = kn[:, lanes].astype(BF16)
                qm_h = jnp.where(own, qm[:, lanes], 0.0).astype(BF16)
                qe_h = jnp.where(own, qe[:, lanes], 0.0).astype(BF16)
                ke_h = jnp.where(own, ke[:, lanes], 0.0).astype(BF16)
                v_h = v_ref[rows, h * dv:(h + 1) * dv]
                st = st_ref[h]
                sall_ref[ci, h] = st
                p = _scores(qm_h, knp, jnp.where(own, qk1[:, lanes], 0.0))
                o_ref[rows, h * dv:(h + 1) * dv] = _dot(p.astype(BF16), v_h) + _dot_nt(qe_h, st.astype(BF16))
                st_ref[h] = st * ebl_row[:, lanes] + _dot_tn(v_h, ke_h)
            return carry

        lax.fori_loop(0, nc, chunk, 0, unroll=True)

    return _pcall(
        body, name, [qk, v, gl, wg, bg], grid=(t // gt,),
        in_specs=[pl.BlockSpec((gt, 2 * kw), lambda i: (i, 0)), pl.BlockSpec((gt, vw), lambda i: (i, 0)),
                  pl.BlockSpec((gt, 128), lambda i: (i, 0)), pl.BlockSpec((128, kw), lambda i: (0, 0)),
                  pl.BlockSpec((1, kw), lambda i: (0, 0))],
        out_shape=[S((t, vw), F32), S((t, kw), F32), S((t // CHUNK, heads, dv, 128), F32)],
        out_specs=[pl.BlockSpec((gt, vw), lambda i: (i, 0)), pl.BlockSpec((gt, kw), lambda i: (i, 0)),
                   pl.BlockSpec((nc, heads, dv, 128), lambda i: (i, 0, 0, 0))],
        scratch_shapes=[pltpu.VMEM((heads, dv, 128), F32)],
        sem=("arbitrary",), plans=plans)


def _gla_bwd(qk, v, lg, do, sall, gl, wg, heads, name):
    t = qk.shape[0]
    kw, vw = qk.shape[1] // 2, v.shape[1]
    dk, dv = kw // heads, vw // heads
    gt = min(ROW_TILE, t)
    nc = gt // CHUNK
    nt = t // gt
    scale = dk ** -0.5

    def body(qk_ref, v_ref, lg_ref, do_ref, sall_ref, gl_ref, wg_ref, dqk_ref, dv_ref, dgl_ref, dwg_ref, dbg_ref, dst_ref, dgk_ref):
        @pl.when(pl.program_id(0) == 0)
        def _():
            dst_ref[...] = jnp.zeros_like(dst_ref)
            dwg_ref[...] = jnp.zeros_like(dwg_ref)
            dbg_ref[...] = jnp.zeros_like(dbg_ref)

        low01 = _tri(True).astype(BF16)
        up01 = _tri(False).astype(BF16)
        causal = _tri(True)
        lane = lax.broadcasted_iota(jnp.int32, (CHUNK, 128), 1)
        last_row = lax.broadcasted_iota(jnp.int32, (CHUNK, kw), 0) == CHUNK - 1

        def chunk(cj, carry):
            ci = nc - 1 - cj
            rows = pl.ds(pl.multiple_of(ci * CHUNK, CHUNK), CHUNK)
            q = qk_ref[rows, 0:kw] * scale
            k = qk_ref[rows, kw:2 * kw]
            qk1 = q.astype(BF16).astype(F32) * k.astype(BF16).astype(F32)
            lgc = lg_ref[rows, :]
            eb, ebl, em, en, ebl_row, qe, ke, qm, kn = _gla_chunk_terms(q, k, lgc, low01)
            dqe, dqm, dkn, dke, drow = [], [], [], [], []
            for pr in range(kw // 128):
                lanes = slice(128 * pr, 128 * pr + 128)
                knp = kn[:, lanes].astype(BF16)
                parts = []
                for half in range(2):
                    h = 2 * pr + half
                    own = (lane < 64) if half == 0 else (lane >= 64)
                    qm_h = jnp.where(own, qm[:, lanes], 0.0).astype(BF16)
                    qe_h = jnp.where(own, qe[:, lanes], 0.0).astype(BF16)
                    ke_h = jnp.where(own, ke[:, lanes], 0.0).astype(BF16)
                    v_h = v_ref[rows, h * dv:(h + 1) * dv]
                    do_h = do_ref[rows, h * dv:(h + 1) * dv]
                    st = sall_ref[ci, h]
                    dst = dst_ref[h]
                    stb, dstb = st.astype(BF16), dst.astype(BF16)
                    p = _scores(qm_h, knp, jnp.where(own, qk1[:, lanes], 0.0)).astype(BF16)
                    dp = jnp.where(causal, _dot_nt(do_h, v_h), 0.0).astype(BF16)
                    dv_ref[rows, h * dv:(h + 1) * dv] = (_dot_tn(p, do_h) + _dot_nt(ke_h, dstb)).astype(BF16)
                    parts.append((jnp.where(own, _dot(dp, knp), 0.0), _dot_tn(dp, qm_h), _dot(do_h, stb), _dot(v_h, dstb),
                                  jnp.sum(st * dst, axis=0, keepdims=True)))
                    dst_ref[h] = dst * ebl_row[:, lanes] + _dot_tn(do_h, qe_h)
                dqm.append(parts[0][0] + parts[1][0])
                dkn.append(parts[0][1] + parts[1][1])
                dqe.append(parts[0][2] + parts[1][2])
                dke.append(parts[0][3] + parts[1][3])
                drow.append(parts[0][4] + parts[1][4])
            dqm, dkn, dqe, dke, drow = [jnp.concatenate(a, axis=1) for a in (dqm, dkn, dqe, dke, drow)]
            dqk_ref[rows, 0:kw] = ((dqe * eb + dqm * em) * scale).astype(BF16)
            dqk_ref[rows, kw:2 * kw] = (dke * ebl + dkn * en).astype(BF16)
            tke = dke * ke
            db = dqe * qe + dqm * qm - dkn * kn - tke
            dbl = jnp.sum(tke, axis=0, keepdims=True) + drow * ebl_row
            db = db + jnp.where(last_row, dbl, 0.0)
            dlg = _dot_01(up01, db)
            dgk_ref[rows, :] = dlg * ((1.0 - jnp.exp(GATE_NORMALIZER * lgc)) / GATE_NORMALIZER)
            return carry

        lax.fori_loop(0, nc, chunk, 0, unroll=True)
        dgk = dgk_ref[...]
        dgkb = dgk.astype(BF16)
        dgl_ref[...] = _dot_nt(dgkb, wg_ref[...]).astype(BF16)
        dwg_ref[...] += _dot_tn(gl_ref[...].astype(BF16), dgkb)
        dbg_ref[...] += _rsum8(dgk)

    rev = lambda i: (nt - 1 - i, 0)
    return pl.pallas_call(
        body, name=name, grid=(nt,),
        in_specs=[pl.BlockSpec((gt, 2 * kw), rev), pl.BlockSpec((gt, vw), rev), pl.BlockSpec((gt, kw), rev),
                  pl.BlockSpec((gt, vw), rev), pl.BlockSpec((nc, heads, dv, 128), lambda i: (nt - 1 - i, 0, 0, 0)),
                  pl.BlockSpec((gt, 128), rev), pl.BlockSpec((128, kw), lambda i: (0, 0))],
        out_shape=[S((t, 2 * kw), BF16), S((t, vw), BF16), S((t, 128), BF16), S((128, kw), F32), S((8, kw), F32)],
        out_specs=[pl.BlockSpec((gt, 2 * kw), rev), pl.BlockSpec((gt, vw), rev), pl.BlockSpec((gt, 128), rev),
                   pl.BlockSpec((128, kw), lambda i: (0, 0)), pl.BlockSpec((8, kw), lambda i: (0, 0))],
        scratch_shapes=[pltpu.VMEM((heads, dv, 128), F32), pltpu.VMEM((gt, kw), F32)],
        compiler_params=_cp("arbitrary"),
    )(qk, v, lg, do, sall, gl, wg)


def _conv_taps(cx_ref, halo_ref, first, cw):
    tm = cx_ref.shape[0]
    u = cx_ref[:, cw:2 * cw].astype(F32) * cx_ref[:, 2 * cw:3 * cw].astype(F32)
    uh = halo_ref[:, cw:2 * cw].astype(F32) * halo_ref[:, 2 * cw:3 * cw].astype(F32)
    uh = jnp.where(first, 0.0, uh)
    row = lax.broadcasted_iota(jnp.int32, (tm, cw), 0)
    u1 = jnp.where(row == 0, uh[15:16, :], pltpu.roll(u, 1, 0))
    u2 = jnp.where(row == 0, uh[14:15, :], jnp.where(row == 1, uh[15:16, :], pltpu.roll(u, 2, 0)))
    return u, u1, u2


def _head_norm(o_h, gn):
    rstd = lax.rsqrt(jnp.mean(o_h * o_h, axis=-1, keepdims=True) + EPS)
    ohat = o_h * rstd
    return ohat, rstd, ohat * gn


def _mix_out(cx, o, go, conv_w, gn, w_out, x, ada9, heads, name, plans=()):
    t, d = x.shape
    cw, vw = conv_w.shape[1], o.shape[1]
    dv = vw // heads
    tm = min(ROW_TILE, t)

    def body(cx_ref, halo_ref, o_ref, go_ref, cwt_ref, gn_ref, w_ref, x_ref, ada_ref, xo_ref, m_ref, y_ref):
        u, u1, u2 = _conv_taps(cx_ref, halo_ref, pl.program_id(0) == 0, cw)
        yc = cwt_ref[0:1, :] * u2 + cwt_ref[1:2, :] * u1 + cwt_ref[2:3, :] * u
        y_ref[:, 0:cw] = (cx_ref[:, 0:cw].astype(F32) * yc).astype(BF16)
        for h in range(heads):
            cols = slice(h * dv, (h + 1) * dv)
            _, _, on = _head_norm(o_ref[:, cols], gn_ref[...])
            g = go_ref[:, cols].astype(F32)
            y_ref[:, cw + h * dv:cw + (h + 1) * dv] = (on * (g * _sigmoid(g))).astype(BF16)
        m = _dot(y_ref[...], w_ref[...])
        m_ref[...] = m.astype(BF16)
        xo_ref[...] = x_ref[...] + ada_ref[5:6, :] * m

    return _pcall(
        body, name, [cx, cx, o, go, conv_w, gn, w_out, x, ada9], grid=(t // tm,),
        in_specs=[pl.BlockSpec((tm, 3 * cw), lambda i: (i, 0)),
                  pl.BlockSpec((16, 3 * cw), lambda i: (jnp.maximum(i * (tm // 16) - 1, 0), 0)),
                  pl.BlockSpec((tm, vw), lambda i: (i, 0)), pl.BlockSpec((tm, vw), lambda i: (i, 0)),
                  pl.BlockSpec((3, cw), lambda i: (0, 0)), pl.BlockSpec((1, dv), lambda i: (0, 0)),
                  pl.BlockSpec((cw + vw, d), lambda i: (0, 0)), pl.BlockSpec((tm, d), lambda i: (i, 0)),
                  pl.BlockSpec((N_ADA, d), lambda i: (0, 0))],
        out_shape=[S((t, d), F32), S((t, d), BF16), S((t, cw + vw), BF16)],
        out_specs=[pl.BlockSpec((tm, d), lambda i: (i, 0)), pl.BlockSpec((tm, d), lambda i: (i, 0)),
                   pl.BlockSpec((tm, cw + vw), lambda i: (i, 0))],
        sem=("parallel",), plans=plans)


def _mix_bwd_a(dxo, m, ada9, w_out, cx, o, go, conv_w, gn, heads, name, plans=()):
    t, d = dxo.shape
    cw, vw = conv_w.shape[1], o.shape[1]
    dv = vw // heads
    tm = min(ROW_TILE, t)

    def body(dxo_ref, m_ref, ada_ref, w_ref, cx_ref, halo_ref, o_ref, go_ref, cwt_ref, gn_ref,
             dm_ref, dyc_ref, dcb_ref, do_ref, dgo_ref, dg_ref, dcw_ref, dgn_ref):
        @pl.when(pl.program_id(0) == 0)
        def _():
            dg_ref[...] = jnp.zeros_like(dg_ref)
            dcw_ref[...] = jnp.zeros_like(dcw_ref)
            dgn_ref[...] = jnp.zeros_like(dgn_ref)

        dxo_t = dxo_ref[...]
        dmb = (ada_ref[5:6, :] * dxo_t).astype(BF16)
        dm_ref[...] = dmb
        dg_ref[...] += _rsum8(dxo_t * m_ref[...].astype(F32))
        dy = _dot_nt(dmb, w_ref[...])
        u, u1, u2 = _conv_taps(cx_ref, halo_ref, pl.program_id(0) == 0, cw)
        yc = cwt_ref[0:1, :] * u2 + cwt_ref[1:2, :] * u1 + cwt_ref[2:3, :] * u
        dyv = dy[:, 0:cw]
        dcb_ref[...] = (dyv * yc).astype(BF16)
        dyc = dyv * cx_ref[:, 0:cw].astype(F32)
        dyc_ref[...] = dyc
        dcw_ref[0] += _rsum8(dyc * u2)
        dcw_ref[1] += _rsum8(dyc * u1)
        dcw_ref[2] += _rsum8(dyc * u)
        for h in range(heads):
            cols = slice(h * dv, (h + 1) * dv)
            ohat, rstd, on = _head_norm(o_ref[:, cols], gn_ref[...])
            g = go_ref[:, cols].astype(F32)
            sg = _sigmoid_tanh(g)
            dyg = dy[:, cw + h * dv:cw + (h + 1) * dv]
            dgo_ref[:, cols] = (dyg * on * (sg * (1.0 + g * (1.0 - sg)))).astype(BF16)
            don = dyg * (g * sg)
            dgn_ref[...] += _rsum8(don * ohat)
            tt = don * gn_ref[...]
            do_ref[:, cols] = (rstd * (tt - ohat * jnp.mean(tt * ohat, axis=-1, keepdims=True))).astype(BF16)

    return _pcall(
        body, name, [dxo, m, ada9, w_out, cx, cx, o, go, conv_w, gn], grid=(t // tm,),
        in_specs=[pl.BlockSpec((tm, d), lambda i: (i, 0)), pl.BlockSpec((tm, d), lambda i: (i, 0)),
                  pl.BlockSpec((N_ADA, d), lambda i: (0, 0)), pl.BlockSpec((cw + vw, d), lambda i: (0, 0)),
                  pl.BlockSpec((tm, 3 * cw), lambda i: (i, 0)),
                  pl.BlockSpec((16, 3 * cw), lambda i: (jnp.maximum(i * (tm // 16) - 1, 0), 0)),
                  pl.BlockSpec((tm, vw), lambda i: (i, 0)), pl.BlockSpec((tm, vw), lambda i: (i, 0)),
                  pl.BlockSpec((3, cw), lambda i: (0, 0)), pl.BlockSpec((1, dv), lambda i: (0, 0))],
        out_shape=[S((t, d), BF16), S((t, cw), F32), S((t, cw), BF16), S((t, vw), BF16), S((t, vw), BF16),
                   S((8, d), F32), S((3, 8, cw), F32), S((8, dv), F32)],
        out_specs=[pl.BlockSpec((tm, d), lambda i: (i, 0)), pl.BlockSpec((tm, cw), lambda i: (i, 0)),
                   pl.BlockSpec((tm, cw), lambda i: (i, 0)), pl.BlockSpec((tm, vw), lambda i: (i, 0)),
                   pl.BlockSpec((tm, vw), lambda i: (i, 0)), pl.BlockSpec((8, d), lambda i: (0, 0)),
                   pl.BlockSpec((3, 8, cw), lambda i: (0, 0, 0)), pl.BlockSpec((8, dv), lambda i: (0, 0))],
        sem=("arbitrary",), plans=plans)


def _mix_bwd_b(dyc, cx, dcb, dqk, dvv, dgo, dgl, conv_w, w, x, dxo, ada9, nrm, name):
    t, d = x.shape
    cw = conv_w.shape[1]
    n = w.shape[0]
    tm = min(ROW_TILE, t)
    nt = t // tm
    pieces = [dcb.shape[1], cw, cw, dqk.shape[1], dvv.shape[1], dgo.shape[1], dgl.shape[1]]
    assert sum(pieces) == n

    def body(dyc_ref, nxt_ref, cx_ref, dcb_ref, dqk_ref, dv_ref, dgo_ref, dgl_ref, cwt_ref, w_ref, x_ref, dxo_ref, ada_ref, n_ref,
             dx_ref, dp_ref, dsh_ref, dsc_ref, dn_ref):
        i = pl.program_id(0)

        @pl.when(i == 0)
        def _():
            dsh_ref[...] = jnp.zeros_like(dsh_ref)
            dsc_ref[...] = jnp.zeros_like(dsc_ref)
            dn_ref[...] = jnp.zeros_like(dn_ref)

        dyc_t = dyc_ref[...]
        nxt = jnp.where(i == nt - 1, 0.0, nxt_ref[...])
        row = lax.broadcasted_iota(jnp.int32, (tm, cw), 0)
        d1 = jnp.where(row == tm - 1, nxt[0:1, :], pltpu.roll(dyc_t, tm - 1, 0))
        d2 = jnp.where(row == tm - 2, nxt[0:1, :], jnp.where(row == tm - 1, nxt[1:2, :], pltpu.roll(dyc_t, tm - 2, 0)))
        du = cwt_ref[2:3, :] * dyc_t + cwt_ref[1:2, :] * d1 + cwt_ref[0:1, :] * d2
        c0 = 0
        dp_ref[:, c0:c0 + cw] = dcb_ref[...]
        dp_ref[:, cw:2 * cw] = (du * cx_ref[:, 2 * cw:3 * cw].astype(F32)).astype(BF16)
        dp_ref[:, 2 * cw:3 * cw] = (du * cx_ref[:, cw:2 * cw].astype(F32)).astype(BF16)
        c0 = 3 * cw
        for ref in (dqk_ref, dv_ref, dgo_ref, dgl_ref):
            wd = ref.shape[1]
            dp_ref[:, c0:c0 + wd] = ref[...]
            c0 += wd
        dh = _dot(dp_ref[...], w_ref[...])
        dx, tsh, tsc, tn = _normmod_bwd(dh, x_ref[...], n_ref[...], ada_ref[4:5, :])
        dx_ref[...] = dxo_ref[...] + dx
        dsh_ref[...] += _rsum8(tsh)
        dsc_ref[...] += _rsum8(tsc)
        dn_ref[...] += _rsum8(tn)

    row_spec = lambda wd: pl.BlockSpec((tm, wd), lambda i: (i, 0))
    vec = pl.BlockSpec((8, d), lambda i: (0, 0))
    return pl.pallas_call(
        body, name=name, grid=(nt,),
        in_specs=[row_spec(cw), pl.BlockSpec((8, cw), lambda i: (jnp.minimum((i + 1) * (tm // 8), t // 8 - 1), 0)),
                  row_spec(3 * cw), row_spec(cw), row_spec(dqk.shape[1]), row_spec(dvv.shape[1]), row_spec(dgo.shape[1]),
                  row_spec(dgl.shape[1]), pl.BlockSpec((3, cw), lambda i: (0, 0)), pl.BlockSpec((n, d), lambda i: (0, 0)),
                  row_spec(d), row_spec(d), pl.BlockSpec((N_ADA, d), lambda i: (0, 0)), pl.BlockSpec((1, d), lambda i: (0, 0))],
        out_shape=[S((t, d), F32), S((t, n), BF16), S((8, d), F32), S((8, d), F32), S((8, d), F32)],
        out_specs=[row_spec(d), row_spec(n), vec, vec, vec],
        compiler_params=_cp("arbitrary"),
    )(dyc, dyc, cx, dcb, dqk, dvv, dgo, dgl, conv_w, w, x, dxo, ada9, nrm)


def _loss_head(x, target, nrm, name):
    t, d = x.shape
    tm = min(ROW_TILE, t)
    nt = t // tm

    def body(x_ref, tg_ref, n_ref, loss_ref, dx_ref, dn_ref, acc_ref):
        i = pl.program_id(0)

        @pl.when(i == 0)
        def _():
            acc_ref[...] = jnp.zeros_like(acc_ref)
            dn_ref[...] = jnp.zeros_like(dn_ref)

        xt = x_ref[...]
        rstd = lax.rsqrt(jnp.mean(xt * xt, axis=-1, keepdims=True) + EPS)
        xhat = xt * rstd
        err = xhat * n_ref[...] - tg_ref[...]
        acc_ref[...] += _rsum8(err * err)
        dy = err * (1.0 / d)
        dn_ref[...] += _rsum8(dy * xhat)
        dxhat = dy * n_ref[...]
        dx_ref[...] = rstd * (dxhat - xhat * jnp.mean(dxhat * xhat, axis=-1, keepdims=True))

        @pl.when(i == nt - 1)
        def _():
            loss_ref[...] = jnp.full(loss_ref.shape, (0.5 / d) * jnp.sum(acc_ref[...]), F32)

    return pl.pallas_call(
        body, name=name, grid=(nt,),
        in_specs=[pl.BlockSpec((tm, d), lambda i: (i, 0)), pl.BlockSpec((tm, d), lambda i: (i, 0)),
                  pl.BlockSpec((1, d), lambda i: (0, 0))],
        out_shape=[S((1, 128), F32), S((t, d), F32), S((8, d), F32)],
        out_specs=[pl.BlockSpec((1, 128), lambda i: (0, 0)), pl.BlockSpec((tm, d), lambda i: (i, 0)),
                   pl.BlockSpec((8, d), lambda i: (0, 0))],
        scratch_shapes=[pltpu.VMEM((8, d), F32)],
        compiler_params=_cp("arbitrary"),
    )(x, target, nrm)


def _pack_smalls(vec_parts, dcw, dbg, dgn, dwg, loss_v, rank, name):
    d = vec_parts[0].shape[1]
    cw, kw, dv = dcw.shape[2], dbg.shape[1], dgn.shape[1]
    nv = len(vec_parts)
    loss_row = nv + 2 + rank * kw // d
    assert 2 * cw == d and cw + kw + dv <= d and (rank * kw) % d == 0 and loss_row < PACK_ROWS
    per_row = d // kw

    def body(*refs):
        vrefs, (dcw_ref, dbg_ref, dgn_ref, dwg_ref, loss_ref, o_ref) = refs[:nv], refs[nv:]
        o_ref[...] = jnp.zeros_like(o_ref)
        o_ref[loss_row:loss_row + 1, 0:loss_ref.shape[1]] = loss_ref[...]
        for r, ref in enumerate(vrefs):
            o_ref[r:r + 1, :] = jnp.sum(ref[...], axis=0, keepdims=True)
        o_ref[nv:nv + 1, 0:cw] = jnp.sum(dcw_ref[0], axis=0, keepdims=True)
        o_ref[nv:nv + 1, cw:2 * cw] = jnp.sum(dcw_ref[1], axis=0, keepdims=True)
        o_ref[nv + 1:nv + 2, 0:cw] = jnp.sum(dcw_ref[2], axis=0, keepdims=True)
        o_ref[nv + 1:nv + 2, cw:cw + kw] = jnp.sum(dbg_ref[...], axis=0, keepdims=True)
        o_ref[nv + 1:nv + 2, cw + kw:cw + kw + dv] = jnp.sum(dgn_ref[...], axis=0, keepdims=True)
        for r in range(rank):
            o_ref[nv + 2 + r // per_row:nv + 3 + r // per_row, (r % per_row) * kw:(r % per_row + 1) * kw] = dwg_ref[r:r + 1, :]

    return pl.pallas_call(body, name=name, out_shape=S((PACK_ROWS, d), F32), compiler_params=_cp())(*vec_parts, dcw, dbg, dgn, dwg, loss_v)


def _sum_slots(a, name):
    def body(a_ref, o_ref):
        acc = a_ref[0]
        for s in range(1, NDEV):
            acc = acc + a_ref[s]
        o_ref[...] = acc

    return pl.pallas_call(body, name=name, out_shape=S(a.shape[1:], F32), compiler_params=_cp())(a)


def _adamw(w, g, m, v):
    m = ADAM_B1 * m + (1.0 - ADAM_B1) * g
    v = ADAM_B2 * v + (1.0 - ADAM_B2) * (g * g)
    m_hat = m / (1.0 - ADAM_B1 ** ADAM_STEP)
    v_hat = v / (1.0 - ADAM_B2 ** ADAM_STEP)
    return -ADAM_LR * (m_hat / (jnp.sqrt(v_hat) + ADAM_EPS) + ADAM_WD * w), m, v


def _adam_slots(recv, w, m, v, name):
    r, c = w.shape
    slots = recv.shape[0]
    tr = _row_tile(r, c)

    def body(recv_ref, w_ref, m_ref, v_ref, g_ref, d_ref, mo_ref, vo_ref):
        g = recv_ref[0].astype(F32)
        for s in range(1, slots):
            g = g + recv_ref[s].astype(F32)
        g_ref[...] = g
        d_ref[...], mo_ref[...], vo_ref[...] = _adamw(w_ref[...], g, m_ref[...], v_ref[...])

    blk = pl.BlockSpec((tr, c), lambda i: (i, 0))
    return pl.pallas_call(
        body, name=name, grid=(r // tr,),
        in_specs=[pl.BlockSpec((slots, tr, c), lambda i: (0, i, 0)), blk, blk, blk],
        out_shape=[S((r, c), F32)] * 4, out_specs=[blk] * 4, compiler_params=_cp("parallel"),
    )(recv, w, m, v)


def _adam_w_ada(act_t, dada, w, m, v, name):
    r, c = w.shape
    tr = 128
    nb = act_t.shape[1]

    def body(a_ref, da_ref, w_ref, m_ref, v_ref, g_ref, d_ref, mo_ref, vo_ref):
        g = a_ref[:, 0:1] * da_ref[0:1, :]
        for b in range(1, nb):
            g = g + a_ref[:, b:b + 1] * da_ref[b:b + 1, :]
        g_ref[...] = g
        d_ref[...], mo_ref[...], vo_ref[...] = _adamw(w_ref[...], g, m_ref[...], v_ref[...])

    blk = pl.BlockSpec((tr, c), lambda i: (i, 0))
    return pl.pallas_call(
        body, name=name, grid=(r // tr,),
        in_specs=[pl.BlockSpec((tr, nb), lambda i: (i, 0)), pl.BlockSpec((nb, c), lambda i: (0, 0)), blk, blk, blk],
        out_shape=[S((r, c), F32)] * 4, out_specs=[blk] * 4, compiler_params=_cp("parallel"),
    )(act_t, dada, w, m, v)


def _adam_smalls(ws, gs, ms, vs, name):
    n = len(ws)

    def body(*refs):
        w_r, g_r, m_r, v_r = (refs[k * n:(k + 1) * n] for k in range(4))
        d_o, m_o, v_o = (refs[(4 + k) * n:(5 + k) * n] for k in range(3))
        for i in range(n):
            d_o[i][...], m_o[i][...], v_o[i][...] = _adamw(w_r[i][...], g_r[i][...], m_r[i][...], v_r[i][...])

    shapes = [S(w.shape, F32) for w in ws]
    outs = pl.pallas_call(body, name=name, out_shape=shapes * 3, compiler_params=_cp())(*ws, *gs, *ms, *vs)
    return outs[:n], outs[n:2 * n], outs[2 * n:]


def kernel(x, c, w_ada, b_ada, norm_ffn1, w_ffn1_in, w_ffn1_out, norm_mix, w_mix_in, conv_w, w_gk2, b_gk, gla_norm, w_mix_out, norm_ffn2, w_ffn2_in, w_ffn2_out, norm_final, loss_target, m_w_ada, m_b_ada, m_norm_ffn1, m_w_ffn1_in, m_w_ffn1_out, m_norm_mix, m_w_mix_in, m_conv_w, m_w_gk2, m_b_gk, m_gla_norm, m_w_mix_out, m_norm_ffn2, m_w_ffn2_in, m_w_ffn2_out, m_norm_final, v_w_ada, v_b_ada, v_norm_ffn1, v_w_ffn1_in, v_w_ffn1_out, v_norm_mix, v_w_mix_in, v_conv_w, v_w_gk2, v_b_gk, v_gla_norm, v_w_mix_out, v_norm_ffn2, v_w_ffn2_in, v_w_ffn2_out, v_norm_final):
    t, d = x.shape[1], x.shape[2]
    x0, tgt = x[0], loss_target[0]
    rank, kw = w_gk2.shape[1], w_gk2.shape[2] * NDEV
    cw = conv_w.shape[2] * NDEV
    dv = gla_norm.shape[1]
    vw = d - cw
    heads = vw // dv
    mix_cols = w_mix_in.shape[2]
    widths = [3 * cw, 2 * kw, vw, vw, 128]
    n_proj = 3 * cw + 2 * kw + 2 * vw + rank
    assert n_proj == mix_cols * NDEV and rank <= 128
    me = 4 * lax.axis_index("x") + 2 * lax.axis_index("y") + lax.axis_index("c")

    core = lax.axis_index("c").astype(jnp.int32).reshape(1)
    bf = lambda a: a[0].astype(BF16)
    bft = lambda a: a[0].T.astype(BF16)
    nb = NDEV // 2

    (ada_row, act_all), ((w1i, cwt_all, wg_all),) = _ada_rows(
        c, w_ada[0], b_ada, "ada_rows", plans=[_plan_gather([bft(w_ffn1_in), conv_w[0], w_gk2[0]])])
    ada9 = ada_row.reshape(N_ADA, d)
    cwt = cwt_all.transpose(1, 0, 2).reshape(conv_w.shape[1], cw)
    wg = jnp.pad(wg_all.transpose(1, 0, 2).reshape(rank, kw), ((0, 128 - rank), (0, 0))).astype(BF16)

    (h1, gu1, s1), ((w1o, wmi),) = _ffn_in(x0, ada9, norm_ffn1, w1i, 0, 1, "ffn1_in", plans=[_plan_gather([bf(w_ffn1_out), bft(w_mix_in)])])
    w1o = w1o.reshape(nb, -1, d)
    wmi = jnp.pad(wmi.reshape(n_proj, d), ((0, sum(widths) - n_proj), (0, 0)))
    w2i_mine = bft(w_ffn2_in)
    quarter = w2i_mine.shape[0] // 4
    part = lambda k, into=None: _plan_gather([w2i_mine], rows=(k * quarter, quarter), into=into)
    (x1, f1), ((wmo,), (w2i,)) = _ffn_out(s1, w1o, x0, ada9, 2, 0.5, "ffn1_out", plans=[_plan_gather([bf(w_mix_out)]), part(0)])
    wmo = wmo.reshape(cw + vw, d)
    (h2, cx, qk, vv, go, gl), ((w2i,),) = _mix_in(x1, ada9, norm_mix, wmi, widths, [BF16, F32, BF16, BF16, F32], "mix_in",
                                                 plans=[part(1, [w2i])])
    (o, lg, sall), ((w2i,),) = _gla_fwd(qk, vv, gl, wg, b_gk, heads, "gla_fwd", plans=[part(2, [w2i])])
    (x2, mm, ycat), ((w2i,),) = _mix_out(cx, o, go, cwt, gla_norm, wmo, x1, ada9, heads, "mix_out", plans=[part(3, [w2i])])
    (h3, gu3, s3), ((w2o,),) = _ffn_in(x2, ada9, norm_ffn2, w2i, 6, 7, "ffn2_in", plans=[_plan_gather([bf(w_ffn2_out)])])
    w2o = w2o.reshape(nb, -1, d)
    (x3, f3), _ = _ffn_out(s3, w2o, x2, ada9, 8, 0.5, "ffn2_out")
    loss_v, dx3, dnf = _loss_head(x3, tgt, norm_final.reshape(1, d), "loss_head")

    dx2, r2i, sum2o, (dsh3, dsc3, dg3, dn3), _, _ = _ffn_backward(
        dx3, x2, h3, gu3, s3, f3, ada9, norm_ffn2, w2i, w2o, (6, 7, 8), core, "ffn2_bwd")
    (dm, dyc, dcb, do, dgo, dg2, dcw, dgn), ((r2o,),) = _mix_bwd_a(dx2, mm, ada9, wmo, cx, o, go, cwt, gla_norm, heads, "mix_bwd_a",
                                                                 plans=[_plan_chip_swap([sum2o])])
    dqk, dvv, dgl, dwg, dbg = _gla_bwd(qk, vv, lg, do, sall, gl, wg, heads, "gla_bwd")
    dx1, dproj, dsh2, dsc2, dnm = _mix_bwd_b(dyc, cx, dcb, dqk, dvv, dgo, dgl, cwt, wmi, x1, dx2, ada9, norm_mix, "mix_bwd_b")
    n_pad = sum(widths)
    tn = n_pad // 5
    dwmi, _ = _tn_matmul(dproj, h2, lambda tk: (tk, tn), lambda sb, k: (k, sb), lambda tk: (tk, d), lambda sb, k: (k, 0),
                         (n_pad, d), (tn, d), lambda sb, k: (sb, 0), 5, "mix_dwin")
    dwmo, _ = _tn_matmul(ycat, dm, lambda tk: (tk, cw + vw), lambda sb, k: (k, 0), lambda tk: (tk, d), lambda sb, k: (k, 0),
                         (cw + vw, d), (cw + vw, d), lambda sb, k: (0, 0), 1, "mix_dwout")
    dwmi = dwmi[:n_proj].reshape(NDEV, mix_cols, d)
    dwmo = dwmo.reshape(NDEV, -1, d)
    dx0, r1i, sum1o, (dsh1, dsc1, dg1, dn1), _, ((rmi, rmo),) = _ffn_backward(
        dx1, x0, h1, gu1, s1, f1, ada9, norm_ffn1, w1i, w1o, (0, 1, 2), core, "ffn1_bwd",
        ds_plans=[_plan_sibling_swap([dwmi, dwmo])],
        dwin_plans=lambda moved: [_plan_chip_swap([_pair_add(dwmi, moved[0][0], core, "mix_dwin_add"),
                                                   _pair_add(dwmo, moved[0][1], core, "mix_dwout_add")])])
    pack = _pack_smalls([dn1, dnm, dn3, dnf, dsh1, dsc1, dg1, dsh2, dsc2, dg2, dsh3, dsc3, dg3], dcw, dbg, dgn, dwg, loss_v, rank, "pack_smalls")
    (r1o,), (pack_all,) = _exchange([_plan_chip_swap([sum1o]), _plan_all_to_all([pack], True)], "grads_last")
    tot = _sum_slots(pack_all, "sum_smalls")

    res = {}
    for nm, recv, w, m, v in (("w_ffn1_out", r1o, w_ffn1_out, m_w_ffn1_out, v_w_ffn1_out), ("w_mix_out", rmo, w_mix_out, m_w_mix_out, v_w_mix_out),
                              ("w_ffn2_out", r2o, w_ffn2_out, m_w_ffn2_out, v_w_ffn2_out)):
        res[nm] = [a[None] for a in _adam_slots(recv, w[0], m[0], v[0], "adam_" + nm)]
    for nm, recv, w, m, v in (("w_ffn1_in", r1i, w_ffn1_in, m_w_ffn1_in, v_w_ffn1_in), ("w_mix_in", rmi, w_mix_in, m_w_mix_in, v_w_mix_in),
                              ("w_ffn2_in", r2i, w_ffn2_in, m_w_ffn2_in, v_w_ffn2_in)):
        res[nm] = [a.T[None] for a in _adam_slots(recv, w[0].T, m[0].T, v[0].T, "adam_" + nm)]

    cols_ada = w_ada.shape[2]
    dada_all = pack_all[:, 4:4 + N_ADA, :].reshape(NDEV, N_ADA * d)
    dada_mine = lax.dynamic_slice_in_dim(dada_all, me * cols_ada, cols_ada, axis=1)
    res["w_ada"] = [a[None] for a in _adam_w_ada(act_all.T, dada_mine, w_ada[0], m_w_ada[0], v_w_ada[0], "adam_w_ada")]

    nv = 4 + N_ADA
    g_small = {
        "b_ada": tot[4:nv].reshape(1, N_ADA * d),
        "norm_ffn1": tot[0:1], "norm_mix": tot[1:2], "norm_ffn2": tot[2:3], "norm_final": tot[3:4],
        "conv_w": lax.dynamic_slice_in_dim(
            jnp.concatenate([tot[nv:nv + 1, 0:cw], tot[nv:nv + 1, cw:2 * cw], tot[nv + 1:nv + 2, 0:cw]], axis=0), me * (cw // NDEV), cw // NDEV, axis=1),
        "w_gk2": lax.dynamic_slice_in_dim(tot[nv + 2:nv + 2 + rank * kw // d].reshape(rank, kw), me * (kw // NDEV), kw // NDEV, axis=1),
        "b_gk": tot[nv + 1:nv + 2, cw:cw + kw],
        "gla_norm": tot[nv + 1:nv + 2, cw + kw:cw + kw + dv],
    }
    small = {"b_ada": (b_ada, m_b_ada, v_b_ada), "norm_ffn1": (norm_ffn1, m_norm_ffn1, v_norm_ffn1), "norm_mix": (norm_mix, m_norm_mix, v_norm_mix),
             "norm_ffn2": (norm_ffn2, m_norm_ffn2, v_norm_ffn2), "norm_final": (norm_final, m_norm_final, v_norm_final),
             "conv_w": (conv_w, m_conv_w, v_conv_w), "w_gk2": (w_gk2, m_w_gk2, v_w_gk2), "b_gk": (b_gk, m_b_gk, v_b_gk),
             "gla_norm": (gla_norm, m_gla_norm, v_gla_norm)}
    names = list(small)
    flat = lambda a: a.reshape(-1, a.shape[-1])
    dl, mo, vo = _adam_smalls([flat(small[n][0]) for n in names], [g_small[n] for n in names],
                              [flat(small[n][1]) for n in names], [flat(small[n][2]) for n in names], "adam_smalls")
    for i, n in enumerate(names):
        shp = small[n][0].shape
        res[n] = [g_small[n].reshape(shp), dl[i].reshape(shp), mo[i].reshape(shp), vo[i].reshape(shp)]

    loss = tot[nv + 2 + rank * kw // d, 0]
    order = ["w_ada", "b_ada", "norm_ffn1", "w_ffn1_in", "w_ffn1_out", "norm_mix", "w_mix_in", "conv_w", "w_gk2", "b_gk", "gla_norm",
             "w_mix_out", "norm_ffn2", "w_ffn2_in", "w_ffn2_out", "norm_final"]
    return (loss, dx0[None], *[res[n][0] for n in order], *[res[n][1] for n in order], *[res[n][2] for n in order], *[res[n][3] for n in order])
```

```python
import collections
import functools

import jax
import jax.numpy as jnp
from jax import lax
from jax.experimental import pallas as pl
from jax.experimental.pallas import tpu as pltpu

F32 = jnp.float32
BF16 = jnp.bfloat16
S = jax.ShapeDtypeStruct

NDEV = 8
EPS = 1e-6
GATE_NORMALIZER = 16.0
CHUNK = 128
N_ADA = 9
ADAM_LR, ADAM_B1, ADAM_B2, ADAM_EPS, ADAM_WD, ADAM_STEP = 0.001, 0.9, 0.999, 1e-08, 0.01, 10
V7X_VMEM_LIMIT = 56 * 1024 * 1024
ROW_TILE = 512
WIDE_ROW_TILE = 1024
K_TILE = 1024
PACK_ROWS = 24
ANY = pl.BlockSpec(memory_space=pl.ANY)


def _cp(*sem):
    return pltpu.CompilerParams(dimension_semantics=sem or None, vmem_limit_bytes=V7X_VMEM_LIMIT)


def _dot(a, b):
    return jnp.dot(a, b, preferred_element_type=F32)


def _dot_nt(a, b):
    return lax.dot_general(a, b, (((1,), (1,)), ((), ())), preferred_element_type=F32)


def _dot_tn(a, b):
    return lax.dot_general(a, b, (((0,), (0,)), ((), ())), preferred_element_type=F32)


def _rsum8(a):
    r, c = a.shape
    return jnp.sum(a.reshape(r // 8, 8, c), axis=0)


def _row_tile(r, c):
    for cand in (256, 128, 176, 88, 64, 32, 16, 8):
        if r % cand == 0 and cand * c * 4 <= 1024 * 1024:
            return cand
    return r


def _sigmoid(x):
    return 1.0 / (1.0 + jnp.exp(-x))


def _sigmoid_tanh(x):
    return 0.5 * jnp.tanh(0.5 * x) + 0.5


def _normmod(x, nrm, sc, sh):
    rstd = lax.rsqrt(jnp.mean(x * x, axis=-1, keepdims=True) + EPS)
    xhat = x * rstd
    return xhat, rstd, (xhat * nrm) * (1.0 + sc) + sh


def _normmod_bwd(dh, x, nrm, sc):
    rstd = lax.rsqrt(jnp.mean(x * x, axis=-1, keepdims=True) + EPS)
    xhat = x * rstd
    dxhat = dh * (nrm * (1.0 + sc))
    dx = rstd * (dxhat - xhat * jnp.mean(dxhat * xhat, axis=-1, keepdims=True))
    return dx, dh, dh * (xhat * nrm), dh * ((1.0 + sc) * xhat)


def _place():
    x, y, c = lax.axis_index("x"), lax.axis_index("y"), lax.axis_index("c")
    return x, y, c, 4 * x + 2 * y + c


def _peer(x, y, c, k):
    px = 1 - x if k & 4 else x
    py = 1 - y if k & 2 else y
    pc = 1 - c if k & 1 else c
    return (px, py, pc), 4 * px + 2 * py + pc


def _remote(src, dst, send_sem, recv_sem, peer):
    return pltpu.make_async_remote_copy(src_ref=src, dst_ref=dst, send_sem=send_sem, recv_sem=recv_sem,
                                        device_id=peer, device_id_type=pl.DeviceIdType.MESH)


_Plan = collections.namedtuple("_Plan", "inputs out_shapes sem_shapes start finish aliases", defaults=({},))


def _plan_all_to_all(xs, gather):
    n = len(xs)

    def copies(ins, outs, sems, landed):
        send_sems, recv_sems, local_sems = sems
        x, y, c, me = _place()
        local = [pltpu.make_async_copy(ins[i] if gather else ins[i].at[me], outs[i].at[me], local_sems.at[i]) for i in range(n)]
        remote = []
        for k in range(1, NDEV):
            peer, pid = _peer(x, y, c, k)
            for i in range(n):
                remote.append(_remote(ins[i] if gather else ins[i].at[pid], outs[i].at[pid if landed else me],
                                      send_sems.at[i, k - 1], recv_sems.at[i, k - 1], peer))
        return local, remote

    def start(ins, outs, sems):
        local, remote = copies(ins, outs, sems, False)
        for cp in local + remote:
            cp.start()

    def finish(ins, outs, sems):
        local, remote = copies(ins, outs, sems, True)
        for cp in remote + local:
            cp.wait()

    return _Plan(list(xs), [S((NDEV,) + a.shape, a.dtype) if gather else S(a.shape, a.dtype) for a in xs],
                 [pltpu.SemaphoreType.DMA((n, NDEV - 1)), pltpu.SemaphoreType.DMA((n, NDEV - 1)), pltpu.SemaphoreType.DMA((n,))],
                 start, finish)


def _other_chips(x, y):
    return [(1 - x, y), (x, 1 - y), (1 - x, 1 - y)]


def _plan_gather(xs, rows=None, into=None, chips=(0, 1, 2), own=True):
    n = len(xs)

    def copies(ins, outs, sems, rest):
        send_sems, recv_sems, local_sems = sems
        x, y, c, me = _place()
        sib, sib_id = (x, y, 1 - c), 4 * x + 2 * y + 1 - c
        others = [(j, p) for j, p in enumerate(_other_chips(x, y)) if j in chips]
        mine = lambda i: ins[i] if rows is None else ins[i].at[pl.ds(*rows)]
        slot_of = lambda i, s: outs[i].at[s] if rows is None else outs[i].at[s, pl.ds(*rows)]
        local = [pltpu.make_async_copy(mine(i), slot_of(i, me), local_sems.at[i]) for i in range(n)] if own else []
        first = [_remote(mine(i), slot_of(i, me), send_sems.at[i, 0], recv_sems.at[i, 0], sib) for i in range(n)] if own else []
        first += [_remote(mine(i), slot_of(i, me), send_sems.at[i, 1 + j], recv_sems.at[i, 1 + j], (px, py, c))
                  for j, (px, py) in others for i in range(n)]
        if not rest:
            return local, first
        from_sibling = [_remote(mine(i), slot_of(i, sib_id), send_sems.at[i, 0], recv_sems.at[i, 0], sib) for i in range(n)] if own else []
        arrive, forward = [], []
        for j, (px, py) in others:
            s = 4 * px + 2 * py
            arrive.append([_remote(mine(i), slot_of(i, s + c), send_sems.at[i, 1 + j], recv_sems.at[i, 1 + j], (px, py, c)) for i in range(n)])
            forward.append([_remote(slot_of(i, s + c), slot_of(i, s + c), send_sems.at[i, 4 + j], recv_sems.at[i, 4 + j], sib) for i in range(n)])
            from_sibling += [_remote(mine(i), slot_of(i, s + 1 - c), send_sems.at[i, 4 + j], recv_sems.at[i, 4 + j], sib) for i in range(n)]
        return local, first, arrive, forward, from_sibling

    def start(ins, outs, sems):
        local, first = copies(ins, outs, sems, False)
        for cp in local + first:
            cp.start()

    def finish(ins, outs, sems):
        local, first, arrive, forward, from_sibling = copies(ins, outs, sems, True)
        for landed, onward in zip(arrive, forward):
            for cp in landed:
                cp.wait_recv()
            for cp in onward:
                cp.start()
        for cp in from_sibling:
            cp.wait_recv()
        for cp in first + [cp for onward in forward for cp in onward]:
            cp.wait_send()
        for cp in local:
            cp.wait()

    return _Plan(list(xs) + list(into or []), [S((NDEV,) + a.shape, a.dtype) for a in xs],
                 [pltpu.SemaphoreType.DMA((n, NDEV - 1)), pltpu.SemaphoreType.DMA((n, NDEV - 1)), pltpu.SemaphoreType.DMA((n,))],
                 start, finish, {n + i: i for i in range(len(into or []))})


def _plan_sibling_swap(gs):
    n = len(gs)

    def copies(ins, outs, sems):
        send_sems, recv_sems = sems
        x, y, c, _ = _place()
        return [_remote(ins[i].at[2 * j + 1 - c], outs[i].at[j], send_sems.at[i, j], recv_sems.at[i, j], (x, y, 1 - c))
                for i in range(n) for j in range(NDEV // 2)]

    def start(ins, outs, sems):
        for cp in copies(ins, outs, sems):
            cp.start()

    def finish(ins, outs, sems):
        for cp in copies(ins, outs, sems):
            cp.wait()

    return _Plan(list(gs), [S((NDEV // 2,) + a.shape[1:], a.dtype) for a in gs],
                 [pltpu.SemaphoreType.DMA((n, NDEV // 2)), pltpu.SemaphoreType.DMA((n, NDEV // 2))], start, finish)


def _pair_add(g, r1, core, name):
    _, r, c = g.shape
    tr = r if r * c * 2 <= 2 * 1024 * 1024 else _row_tile(r, c)

    def body(core_ref, g_ref, r_ref, o_ref):
        o_ref[...] = (g_ref[...].astype(F32) + r_ref[...].astype(F32)).astype(BF16)

    return pl.pallas_call(
        body, name=name,
        grid_spec=pltpu.PrefetchScalarGridSpec(
            num_scalar_prefetch=1, grid=(NDEV // 2, r // tr),
            in_specs=[pl.BlockSpec((None, tr, c), lambda j, k, core_ref: (2 * j + core_ref[0], k, 0)),
                      pl.BlockSpec((None, tr, c), lambda j, k, core_ref: (j, k, 0))],
            out_specs=pl.BlockSpec((None, tr, c), lambda j, k, core_ref: (j, k, 0))),
        out_shape=S((NDEV // 2, r, c), BF16), compiler_params=_cp("parallel", "parallel"),
    )(core, g, r1)


def _plan_chip_swap(ps):
    n = len(ps)

    def copies(ins, outs, sems, landed):
        send_sems, recv_sems, local_sems = sems
        x, y, c, _ = _place()
        mine = 2 * x + y
        local = [pltpu.make_async_copy(ins[i].at[mine], outs[i].at[mine], local_sems.at[i]) for i in range(n)]
        remote = [_remote(ins[i].at[2 * px + py], outs[i].at[2 * px + py if landed else mine], send_sems.at[i, j], recv_sems.at[i, j], (px, py, c))
                  for j, (px, py) in enumerate(_other_chips(x, y)) for i in range(n)]
        return local, remote

    def start(ins, outs, sems):
        local, remote = copies(ins, outs, sems, False)
        for cp in local + remote:
            cp.start()

    def finish(ins, outs, sems):
        local, remote = copies(ins, outs, sems, True)
        for cp in remote + local:
            cp.wait()

    return _Plan(list(ps), [S(a.shape, a.dtype) for a in ps],
                 [pltpu.SemaphoreType.DMA((n, 3)), pltpu.SemaphoreType.DMA((n, 3)), pltpu.SemaphoreType.DMA((n,))], start, finish)


def _pcall(body, name, args, in_specs, out_shape, out_specs, grid=(), scratch_shapes=(), sem=(), plans=(), prefetch=None, aliases=None):
    n_in, n_out, n_scr = len(args), len(out_shape), len(scratch_shapes)
    n_pre = 0 if prefetch is None else 1
    counts = [(len(p.inputs), len(p.out_shapes), len(p.sem_shapes)) for p in plans]
    c_args = [a for p in plans for a in p.inputs]
    c_outs = [s for p in plans for s in p.out_shapes]
    c_sems = [s for p in plans for s in p.sem_shapes]

    def wrapped(*refs):
        refs = refs[n_pre:]
        cuts = [n_in, len(c_args), n_out, len(c_outs), n_scr, len(c_sems)]
        ins, c_in, outs, c_out, scr, c_sem = [refs[sum(cuts[:k]):sum(cuts[:k + 1])] for k in range(6)]

        def halves(which):
            a = b = s = 0
            for p, (na, nb, ns) in zip(plans, counts):
                getattr(p, which)(c_in[a:a + na], c_out[b:b + nb], c_sem[s:s + ns])
                a, b, s = a + na, b + nb, s + ns

        if not plans:
            body(*ins, *outs, *scr)
        elif not grid:
            halves("start")
            body(*ins, *outs, *scr)
            halves("finish")
        else:
            first = functools.reduce(jnp.logical_and, [pl.program_id(a) == 0 for a in range(len(grid))])
            last = functools.reduce(jnp.logical_and, [pl.program_id(a) == grid[a] - 1 for a in range(len(grid))])
            pl.when(first)(lambda: halves("start"))
            body(*ins, *outs, *scr)
            pl.when(last)(lambda: halves("finish"))

    aliased = {n_pre + k: v for k, v in (aliases or {}).items()}
    a, b = n_pre + n_in, n_out
    for p, (na, nb, _) in zip(plans, counts):
        aliased.update({a + k: b + v for k, v in p.aliases.items()})
        a, b = a + na, b + nb
    specs = dict(grid=grid, in_specs=list(in_specs) + [ANY] * len(c_args), out_specs=list(out_specs) + [ANY] * len(c_outs),
                 scratch_shapes=list(scratch_shapes) + c_sems)
    if prefetch is not None:
        specs = dict(grid_spec=pltpu.PrefetchScalarGridSpec(num_scalar_prefetch=1, **specs))
    res = pl.pallas_call(
        wrapped, name=name, out_shape=list(out_shape) + c_outs, input_output_aliases=aliased,
        compiler_params=_cp(*(("arbitrary",) * len(grid) if plans else sem)), **specs,
    )(*([] if prefetch is None else [prefetch]), *args, *c_args)
    c_res, b = [], n_out
    for _, nb, _ in counts:
        c_res.append(res[b:b + nb])
        b += nb
    return res[:n_out], c_res


def _exchange(plans, name):
    return _pcall(lambda: None, name, [], [], [], [], plans=plans)[1]


def _ada_rows(c, w_ada, b_ada, name, plans=()):
    d, cols = c.shape[1], w_ada.shape[1]
    gather_c = _plan_all_to_all([c], True)
    gather_p = _plan_all_to_all([S((NDEV, cols), F32)], True)
    n_sem = len(gather_c.sem_shapes)

    def body(c_ref, w_ref, b_ref, ada_ref, act_ref, c_all, p_mine, p_all, *sems):
        gather_c.start([c_ref], [c_all], sems[:n_sem])
        gather_c.finish([c_ref], [c_all], sems[:n_sem])
        for s in range(NDEV):
            cc = c_all[s]
            act_ref[s:s + 1, :] = cc * _sigmoid(cc)
        p_mine[...] = _dot(act_ref[...].astype(BF16), w_ref[...].astype(BF16))
        gather_p.start([p_mine], [p_all], sems[n_sem:])
        gather_p.finish([p_mine], [p_all], sems[n_sem:])
        me = _place()[3]
        for s in range(NDEV):
            ada_ref[:, s * cols:(s + 1) * cols] = p_all[s, pl.ds(me, 1), :] + b_ref[:, s * cols:(s + 1) * cols]

    whole = pl.BlockSpec(memory_space=pltpu.VMEM)
    return _pcall(body, name, [c, w_ada, b_ada], [whole] * 3, [S((1, NDEV * cols), F32), S((NDEV, d), F32)], [whole] * 2,
                  scratch_shapes=[pltpu.VMEM((NDEV,) + c.shape, F32), pltpu.VMEM((NDEV, cols), F32), pltpu.VMEM((NDEV, NDEV, cols), F32)]
                  + gather_c.sem_shapes + gather_p.sem_shapes, plans=plans)


def _ffn_in(x, ada9, nrm, w_in, sh_row, sc_row, name, plans=(), some=None, begun=None):
    t, d = x.shape
    nb, bw = w_in.shape[0] // 2, w_in.shape[1]
    tm = min(WIDE_ROW_TILE, t)
    first, count = (None, nb) if some is None else some
    blk = lambda j, pre: j if first is None else j + pre[0][0]

    def body(x_ref, ada_ref, n_ref, wg_ref, wu_ref, *rest):
        h_ref, gu_ref, s_ref = rest[-3:]

        @pl.when(pl.program_id(1) == 0)
        def _():
            _, _, h = _normmod(x_ref[...], n_ref[...], ada_ref[sc_row:sc_row + 1, :], ada_ref[sh_row:sh_row + 1, :])
            h_ref[...] = h.astype(BF16)

        h = h_ref[...]
        g = _dot_nt(h, wg_ref[...])
        u = _dot_nt(h, wu_ref[...])
        gu_ref[0] = g.astype(BF16)
        gu_ref[1] = u.astype(BF16)
        s_ref[...] = (g * _sigmoid(g) * u).astype(BF16)

    return _pcall(
        body, name, [x, ada9, nrm, w_in, w_in] + list(begun or []), grid=(t // tm, count),
        in_specs=[pl.BlockSpec((tm, d), lambda i, j, *pre: (i, 0)), pl.BlockSpec((N_ADA, d), lambda i, j, *pre: (0, 0)),
                  pl.BlockSpec((1, d), lambda i, j, *pre: (0, 0)),
                  pl.BlockSpec((None, bw, d), lambda i, j, *pre: (blk(j, pre), 0, 0)),
                  pl.BlockSpec((None, bw, d), lambda i, j, *pre: (blk(j, pre) + nb, 0, 0))] + [ANY] * len(begun or []),
        out_shape=[S((t, d), BF16), S((2, nb, t, bw), BF16), S((nb, t, bw), BF16)],
        out_specs=[pl.BlockSpec((tm, d), lambda i, j, *pre: (i, 0)),
                   pl.BlockSpec((2, None, tm, bw), lambda i, j, *pre: (0, blk(j, pre), i, 0)),
                   pl.BlockSpec((None, tm, bw), lambda i, j, *pre: (blk(j, pre), i, 0))],
        sem=("parallel", "arbitrary"), plans=plans, prefetch=first, aliases={5: 1, 6: 2} if begun else None)


def _ffn_out(s, w_out, x, ada9, g_row, res_scale, name, plans=()):
    nb, t, bw = s.shape
    d = x.shape[1]
    tm = min(ROW_TILE, t)

    def body(s_ref, w_ref, x_ref, ada_ref, xo_ref, f_ref):
        acc = _dot(s_ref[0], w_ref[0])
        for b in range(1, nb):
            acc = acc + _dot(s_ref[b], w_ref[b])
        f_ref[...] = acc.astype(BF16)
        xo_ref[...] = x_ref[...] + (res_scale * ada_ref[g_row:g_row + 1, :]) * acc

    return _pcall(
        body, name, [s, w_out, x, ada9], grid=(t // tm,),
        in_specs=[pl.BlockSpec((nb, tm, bw), lambda i: (0, i, 0)), pl.BlockSpec((nb, bw, d), lambda i: (0, 0, 0)),
                  pl.BlockSpec((tm, d), lambda i: (i, 0)), pl.BlockSpec((N_ADA, d), lambda i: (0, 0))],
        out_shape=[S((t, d), F32), S((t, d), BF16)],
        out_specs=[pl.BlockSpec((tm, d), lambda i: (i, 0)), pl.BlockSpec((tm, d), lambda i: (i, 0))],
        sem=("parallel",), plans=plans)


def _ffn_bwd_ds(dxo, f, ada9, w_out, gu, g_row, res_scale, name, plans=()):
    t, d = dxo.shape
    nb, bw = w_out.shape[0], w_out.shape[1]
    tm = min(WIDE_ROW_TILE, t)

    def body(dxo_ref, f_ref, ada_ref, w_ref, gu_ref, df_ref, da_ref, dg_ref):
        i, j = pl.program_id(0), pl.program_id(1)

        @pl.when((i == 0) & (j == 0))
        def _():
            dg_ref[...] = jnp.zeros_like(dg_ref)

        @pl.when(j == 0)
        def _():
            dxo_t = dxo_ref[...]
            df_ref[...] = ((res_scale * ada_ref[g_row:g_row + 1, :]) * dxo_t).astype(BF16)
            dg_ref[...] += res_scale * _rsum8(dxo_t * f_ref[...].astype(F32))

        ds = _dot_nt(df_ref[...], w_ref[...])
        g = gu_ref[0].astype(F32)
        u = gu_ref[1].astype(F32)
        sg = _sigmoid_tanh(g)
        da_ref[0] = (ds * u * (sg * (1.0 + g * (1.0 - sg)))).astype(BF16)
        da_ref[1] = (ds * (g * sg)).astype(BF16)

    return _pcall(
        body, name, [dxo, f, ada9, w_out, gu], grid=(t // tm, nb),
        in_specs=[pl.BlockSpec((tm, d), lambda i, j: (i, 0)), pl.BlockSpec((tm, d), lambda i, j: (i, 0)),
                  pl.BlockSpec((N_ADA, d), lambda i, j: (0, 0)), pl.BlockSpec((None, bw, d), lambda i, j: (j, 0, 0)),
                  pl.BlockSpec((2, None, tm, bw), lambda i, j: (0, j, i, 0))],
        out_shape=[S((t, d), BF16), S((2, nb, t, bw), BF16), S((8, d), F32)],
        out_specs=[pl.BlockSpec((tm, d), lambda i, j: (i, 0)), pl.BlockSpec((2, None, tm, bw), lambda i, j: (0, j, i, 0)),
                   pl.BlockSpec((8, d), lambda i, j: (0, 0))],
        sem=("arbitrary", "arbitrary"), plans=plans)


def _ffn_bwd_dh(da, w_in, x, dxo, ada9, nrm, sh_row, sc_row, name, plans=()):
    t, d = x.shape
    nb, bw = w_in.shape[0] // 2, w_in.shape[1]
    tm = min(WIDE_ROW_TILE, t)

    def body(da_ref, wg_ref, wu_ref, x_ref, dxo_ref, ada_ref, n_ref, dx_ref, dsh_ref, dsc_ref, dn_ref, acc_ref):
        i, j = pl.program_id(0), pl.program_id(1)

        @pl.when((i == 0) & (j == 0))
        def _():
            dsh_ref[...] = jnp.zeros_like(dsh_ref)
            dsc_ref[...] = jnp.zeros_like(dsc_ref)
            dn_ref[...] = jnp.zeros_like(dn_ref)

        part = _dot(da_ref[0], wg_ref[...]) + _dot(da_ref[1], wu_ref[...])

        @pl.when(j == 0)
        def _():
            acc_ref[...] = part

        @pl.when(j > 0)
        def _():
            acc_ref[...] += part

        @pl.when(j == nb - 1)
        def _():
            for r0 in range(0, tm, min(256, tm)):
                rows = slice(r0, r0 + min(256, tm))
                dx, tsh, tsc, tn = _normmod_bwd(acc_ref[rows, :], x_ref[rows, :], n_ref[...], ada_ref[sc_row:sc_row + 1, :])
                dx_ref[rows, :] = dxo_ref[rows, :] + dx
                dsh_ref[...] += _rsum8(tsh)
                dsc_ref[...] += _rsum8(tsc)
                dn_ref[...] += _rsum8(tn)

    vec = pl.BlockSpec((8, d), lambda i, j: (0, 0))
    return _pcall(
        body, name, [da, w_in, w_in, x, dxo, ada9, nrm], grid=(t // tm, nb),
        in_specs=[pl.BlockSpec((2, None, tm, bw), lambda i, j: (0, j, i, 0)),
                  pl.BlockSpec((None, bw, d), lambda i, j: (j, 0, 0)), pl.BlockSpec((None, bw, d), lambda i, j: (j + nb, 0, 0)),
                  pl.BlockSpec((tm, d), lambda i, j: (i, 0)), pl.BlockSpec((tm, d), lambda i, j: (i, 0)),
                  pl.BlockSpec((N_ADA, d), lambda i, j: (0, 0)), pl.BlockSpec((1, d), lambda i, j: (0, 0))],
        out_shape=[S((t, d), F32), S((8, d), F32), S((8, d), F32), S((8, d), F32)],
        out_specs=[pl.BlockSpec((tm, d), lambda i, j: (i, 0)), vec, vec, vec],
        scratch_shapes=[pltpu.VMEM((tm, d), F32)],
        sem=("arbitrary", "arbitrary"), plans=plans)


def _tn_matmul(a, b, a_block, a_map, b_block, b_map, out_shape, out_block, out_map, nblk, name, plans=()):
    t = a.shape[-2]
    tk = min(K_TILE, t)
    nk = t // tk

    def body(a_ref, b_ref, o_ref, acc_ref):
        k = pl.program_id(1)
        for q in (range(a_ref.shape[0]) if len(a_ref.shape) == 3 else [Ellipsis]):
            part = _dot_tn(a_ref[q], b_ref[...])

            @pl.when(k == 0)
            def _():
                acc_ref[q] = part

            @pl.when(k > 0)
            def _():
                acc_ref[q] += part

        @pl.when(k == nk - 1)
        def _():
            o_ref[...] = acc_ref[...].astype(BF16)

    (out,), moved = _pcall(
        body, name, [a, b], grid=(nblk, nk),
        in_specs=[pl.BlockSpec(a_block(tk), a_map), pl.BlockSpec(b_block(tk), b_map)],
        out_shape=[S(out_shape, BF16)], out_specs=[pl.BlockSpec(out_block, out_map)],
        scratch_shapes=[pltpu.VMEM(tuple(n for n in out_block if n is not None), F32)],
        sem=("parallel", "arbitrary"), plans=plans)
    return out, moved


def _ffn_backward(dxo, x_in, h, gu, s, f, ada9, nrm, w_in, w_out, rows, core, name, ds_plans=(), dwin_plans=()):
    sh_row, sc_row, g_row = rows
    nb, t, bw = s.shape
    d = x_in.shape[1]
    (df, da, dg), ds_moved = _ffn_bwd_ds(dxo, f, ada9, w_out, gu, g_row, 0.5, name + "_ds", plans=ds_plans)
    dw_in, dwin_moved = _tn_matmul(
        da.reshape(2 * nb, t, bw), h, lambda tk: (2, tk, bw), lambda sb, k: (sb, k, 0), lambda tk: (tk, d), lambda sb, k: (k, 0),
        (2 * nb, bw, d), (2, bw, d), lambda sb, k: (sb, 0, 0), nb, name + "_dwin",
        plans=dwin_plans(ds_moved) if callable(dwin_plans) else dwin_plans)
    dw_out, ((half_in,),) = _tn_matmul(
        s, df, lambda tk: (None, tk, bw), lambda sb, k: (sb, k, 0), lambda tk: (tk, d), lambda sb, k: (k, 0),
        (nb, bw, d), (None, bw, d), lambda sb, k: (sb, 0, 0), nb, name + "_dwout", plans=[_plan_sibling_swap([dw_in])])
    dw_out = dw_out.reshape(NDEV, -1, d)
    sum_in = _pair_add(dw_in, half_in, core, name + "_dwin_add")
    (dx, dsh, dsc, dn), ((recv_in,), (half_out,)) = _ffn_bwd_dh(
        da, w_in, x_in, dxo, ada9, nrm, sh_row, sc_row, name + "_dh", plans=[_plan_chip_swap([sum_in]), _plan_sibling_swap([dw_out])])
    sum_out = _pair_add(dw_out, half_out, core, name + "_dwout_add")
    return dx, recv_in, sum_out, (dsh, dsc, dg, dn), ds_moved, dwin_moved


def _mix_in(x, ada9, nrm, w, widths, dts, name, plans=()):
    t, d = x.shape
    n = w.shape[0]
    tm = min(ROW_TILE, t)
    starts = [sum(widths[:i]) for i in range(len(widths))]

    def body(x_ref, ada_ref, n_ref, w_ref, h_ref, *out_refs):
        _, _, h = _normmod(x_ref[...], n_ref[...], ada_ref[4:5, :], ada_ref[3:4, :])
        hb = h.astype(BF16)
        h_ref[...] = hb
        for o_ref, st, wd in zip(out_refs, starts, widths):
            o_ref[...] = _dot_nt(hb, w_ref[st:st + wd, :]).astype(o_ref.dtype)

    return _pcall(
        body, name, [x, ada9, nrm, w], grid=(t // tm,),
        in_specs=[pl.BlockSpec((tm, d), lambda i: (i, 0)), pl.BlockSpec((N_ADA, d), lambda i: (0, 0)),
                  pl.BlockSpec((1, d), lambda i: (0, 0)), pl.BlockSpec((n, d), lambda i: (0, 0))],
        out_shape=[S((t, d), BF16)] + [S((t, wd), dt) for wd, dt in zip(widths, dts)],
        out_specs=[pl.BlockSpec((tm, d), lambda i: (i, 0))] + [pl.BlockSpec((tm, wd), lambda i: (i, 0)) for wd in widths],
        sem=("parallel",), plans=plans)


def _tri(lower):
    r = lax.broadcasted_iota(jnp.int32, (CHUNK, CHUNK), 0)
    c = lax.broadcasted_iota(jnp.int32, (CHUNK, CHUNK), 1)
    return (r >= c) if lower else (c >= r)


def _dot_01(m, x):
    hi = x.astype(BF16)
    r1 = x - hi.astype(F32)
    mid = r1.astype(BF16)
    lo = (r1 - mid.astype(F32)).astype(BF16)
    return _dot(m, hi) + _dot(m, mid) + _dot(m, lo)


def _gla_chunk_terms(q, k, lg, low01):
    b = _dot_01(low01, lg)
    bl = b[CHUNK - 1:CHUNK, :]
    r = 0.5 * bl
    eb, ebl, em, en = jnp.exp(b), jnp.exp(bl - b), jnp.exp(b - r), jnp.exp(r - b)
    return eb, ebl, em, en, jnp.exp(bl), q * eb, k * ebl, q * em, k * en


def _scores(qm_h, knp, qk1_h):
    r = lax.broadcasted_iota(jnp.int32, (CHUNK, CHUNK), 0)
    c = lax.broadcasted_iota(jnp.int32, (CHUNK, CHUNK), 1)
    p = jnp.where(r > c, _dot_nt(qm_h, knp), 0.0)
    return jnp.where(r == c, jnp.sum(qk1_h, axis=1, keepdims=True), p)


def _gla_fwd(qk, v, gl, wg, bg, heads, name, plans=()):
    t = qk.shape[0]
    kw, vw = qk.shape[1] // 2, v.shape[1]
    dk, dv = kw // heads, vw // heads
    assert dk == 64 and dv == 128 and kw % 128 == 0
    gt = min(ROW_TILE, t)
    nc = gt // CHUNK
    scale = dk ** -0.5

    def body(qk_ref, v_ref, gl_ref, wg_ref, bg_ref, o_ref, lg_ref, sall_ref, st_ref):
        @pl.when(pl.program_id(0) == 0)
        def _():
            st_ref[...] = jnp.zeros_like(st_ref)

        gk = _dot(gl_ref[...].astype(BF16), wg_ref[...]) + bg_ref[...]
        lg_ref[...] = (jnp.minimum(gk, 0.0) - jnp.log(1.0 + jnp.exp(-jnp.abs(gk)))) / GATE_NORMALIZER
        low01 = _tri(True).astype(BF16)
        lane = lax.broadcasted_iota(jnp.int32, (CHUNK, 128), 1)

        def chunk(ci, carry):
            rows = pl.ds(pl.multiple_of(ci * CHUNK, CHUNK), CHUNK)
            q = qk_ref[rows, 0:kw] * scale
            k = qk_ref[rows, kw:2 * kw]
            qk1 = q.astype(BF16).astype(F32) * k.astype(BF16).astype(F32)
            eb, ebl, em, en, ebl_row, qe, ke, qm, kn = _gla_chunk_terms(q, k, lg_ref[rows, :], low01)
            for h in range(heads):
                lanes = slice(128 * (h // 2), 128 * (h // 2) + 128)
                own = (lane < 64) if h % 2 == 0 else (lane >= 64)
                knp = kn[:, lanes].astype(BF16)
                qm_h = jnp.where(own, qm[:, lanes], 0.0).astype(BF16)
                qe_h = jnp.where(own, qe[:, lanes], 0.0).astype(BF16)
                ke_h = jnp.where(own, ke[:, lanes], 0.0).astype(BF16)
                v_h = v_ref[rows, h * dv:(h + 1) * dv]
                st = st_ref[h]
                sall_ref[ci, h] = st
                p = _scores(qm_h, knp, jnp.where(own, qk1[:, lanes], 0.0))
                o_ref[rows, h * dv:(h + 1) * dv] = _dot(p.astype(BF16), v_h) + _dot_nt(qe_h, st.astype(BF16))
                st_ref[h] = st * ebl_row[:, lanes] + _dot_tn(v_h, ke_h)
            return carry

        lax.fori_loop(0, nc, chunk, 0, unroll=True)

    return _pcall(
        body, name, [qk, v, gl, wg, bg], grid=(t // gt,),
        in_specs=[pl.BlockSpec((gt, 2 * kw), lambda i: (i, 0)), pl.BlockSpec((gt, vw), lambda i: (i, 0)),
                  pl.BlockSpec((gt, 128), lambda i: (i, 0)), pl.BlockSpec((128, kw), lambda i: (0, 0)),
                  pl.BlockSpec((1, kw), lambda i: (0, 0))],
        out_shape=[S((t, vw), F32), S((t, kw), F32), S((t // CHUNK, heads, dv, 128), F32)],
        out_specs=[pl.BlockSpec((gt, vw), lambda i: (i, 0)), pl.BlockSpec((gt, kw), lambda i: (i, 0)),
                   pl.BlockSpec((nc, heads, dv, 128), lambda i: (i, 0, 0, 0))],
        scratch_shapes=[pltpu.VMEM((heads, dv, 128), F32)],
        sem=("arbitrary",), plans=plans)


def _gla_bwd(qk, v, lg, do, sall, gl, wg, heads, name):
    t = qk.shape[0]
    kw, vw = qk.shape[1] // 2, v.shape[1]
    dk, dv = kw // heads, vw // heads
    gt = min(ROW_TILE, t)
    nc = gt // CHUNK
    nt = t // gt
    scale = dk ** -0.5

    def body(qk_ref, v_ref, lg_ref, do_ref, sall_ref, gl_ref, wg_ref, dqk_ref, dv_ref, dgl_ref, dwg_ref, dbg_ref, dst_ref, dgk_ref):
        @pl.when(pl.program_id(0) == 0)
        def _():
            dst_ref[...] = jnp.zeros_like(dst_ref)
            dwg_ref[...] = jnp.zeros_like(dwg_ref)
            dbg_ref[...] = jnp.zeros_like(dbg_ref)

        low01 = _tri(True).astype(BF16)
        up01 = _tri(False).astype(BF16)
        causal = _tri(True)
        lane = lax.broadcasted_iota(jnp.int32, (CHUNK, 128), 1)
        last_row = lax.broadcasted_iota(jnp.int32, (CHUNK, kw), 0) == CHUNK - 1

        def chunk(cj, carry):
            ci = nc - 1 - cj
            rows = pl.ds(pl.multiple_of(ci * CHUNK, CHUNK), CHUNK)
            q = qk_ref[rows, 0:kw] * scale
            k = qk_ref[rows, kw:2 * kw]
            qk1 = q.astype(BF16).astype(F32) * k.astype(BF16).astype(F32)
            lgc = lg_ref[rows, :]
            eb, ebl, em, en, ebl_row, qe, ke, qm, kn = _gla_chunk_terms(q, k, lgc, low01)
            dqe, dqm, dkn, dke, drow = [], [], [], [], []
            for pr in range(kw // 128):
                lanes = slice(128 * pr, 128 * pr + 128)
                knp = kn[:, lanes].astype(BF16)
                parts = []
                for half in range(2):
                    h = 2 * pr + half
                    own = (lane < 64) if half == 0 else (lane >= 64)
                    qm_h = jnp.where(own, qm[:, lanes], 0.0).astype(BF16)
                    qe_h = jnp.where(own, qe[:, lanes], 0.0).astype(BF16)
                    ke_h = jnp.where(own, ke[:, lanes], 0.0).astype(BF16)
                    v_h = v_ref[rows, h * dv:(h + 1) * dv]
                    do_h = do_ref[rows, h * dv:(h + 1) * dv]
                    st = sall_ref[ci, h]
                    dst = dst_ref[h]
                    stb, dstb = st.astype(BF16), dst.astype(BF16)
                    p = _scores(qm_h, knp, jnp.where(own, qk1[:, lanes], 0.0)).astype(BF16)
                    dp = jnp.where(causal, _dot_nt(do_h, v_h), 0.0).astype(BF16)
                    dv_ref[rows, h * dv:(h + 1) * dv] = (_dot_tn(p, do_h) + _dot_nt(ke_h, dstb)).astype(BF16)
                    parts.append((jnp.where(own, _dot(dp, knp), 0.0), _dot_tn(dp, qm_h), _dot(do_h, stb), _dot(v_h, dstb),
                                  jnp.sum(st * dst, axis=0, keepdims=True)))
                    dst_ref[h] = dst * ebl_row[:, lanes] + _dot_tn(do_h, qe_h)
                dqm.append(parts[0][0] + parts[1][0])
                dkn.append(parts[0][1] + parts[1][1])
                dqe.append(parts[0][2] + parts[1][2])
                dke.append(parts[0][3] + parts[1][3])
                drow.append(parts[0][4] + parts[1][4])
            dqm, dkn, dqe, dke, drow = [jnp.concatenate(a, axis=1) for a in (dqm, dkn, dqe, dke, drow)]
            dqk_ref[rows, 0:kw] = ((dqe * eb + dqm * em) * scale).astype(BF16)
            dqk_ref[rows, kw:2 * kw] = (dke * ebl + dkn * en).astype(BF16)
            tke = dke * ke
            db = dqe * qe + dqm * qm - dkn * kn - tke
            dbl = jnp.sum(tke, axis=0, keepdims=True) + drow * ebl_row
            db = db + jnp.where(last_row, dbl, 0.0)
            dlg = _dot_01(up01, db)
            dgk_ref[rows, :] = dlg * ((1.0 - jnp.exp(GATE_NORMALIZER * lgc)) / GATE_NORMALIZER)
            return carry

        lax.fori_loop(0, nc, chunk, 0, unroll=True)
        dgk = dgk_ref[...]
        dgkb = dgk.astype(BF16)
        dgl_ref[...] = _dot_nt(dgkb, wg_ref[...]).astype(BF16)
        dwg_ref[...] += _dot_tn(gl_ref[...].astype(BF16), dgkb)
        dbg_ref[...] += _rsum8(dgk)

    rev = lambda i: (nt - 1 - i, 0)
    return pl.pallas_call(
        body, name=name, grid=(nt,),
        in_specs=[pl.BlockSpec((gt, 2 * kw), rev), pl.BlockSpec((gt, vw), rev), pl.BlockSpec((gt, kw), rev),
                  pl.BlockSpec((gt, vw), rev), pl.BlockSpec((nc, heads, dv, 128), lambda i: (nt - 1 - i, 0, 0, 0)),
                  pl.BlockSpec((gt, 128), rev), pl.BlockSpec((128, kw), lambda i: (0, 0))],
        out_shape=[S((t, 2 * kw), BF16), S((t, vw), BF16), S((t, 128), BF16), S((128, kw), F32), S((8, kw), F32)],
        out_specs=[pl.BlockSpec((gt, 2 * kw), rev), pl.BlockSpec((gt, vw), rev), pl.BlockSpec((gt, 128), rev),
                   pl.BlockSpec((128, kw), lambda i: (0, 0)), pl.BlockSpec((8, kw), lambda i: (0, 0))],
        scratch_shapes=[pltpu.VMEM((heads, dv, 128), F32), pltpu.VMEM((gt, kw), F32)],
        compiler_params=_cp("arbitrary"),
    )(qk, v, lg, do, sall, gl, wg)


def _conv_taps(cx_ref, halo_ref, first, cw):
    tm = cx_ref.shape[0]
    u = cx_ref[:, cw:2 * cw].astype(F32) * cx_ref[:, 2 * cw:3 * cw].astype(F32)
    uh = halo_ref[:, cw:2 * cw].astype(F32) * halo_ref[:, 2 * cw:3 * cw].astype(F32)
    uh = jnp.where(first, 0.0, uh)
    row = lax.broadcasted_iota(jnp.int32, (tm, cw), 0)
    u1 = jnp.where(row == 0, uh[15:16, :], pltpu.roll(u, 1, 0))
    u2 = jnp.where(row == 0, uh[14:15, :], jnp.where(row == 1, uh[15:16, :], pltpu.roll(u, 2, 0)))
    return u, u1, u2


def _head_norm(o_h, gn):
    rstd = lax.rsqrt(jnp.mean(o_h * o_h, axis=-1, keepdims=True) + EPS)
    ohat = o_h * rstd
    return ohat, rstd, ohat * gn


def _mix_out(cx, o, go, conv_w, gn, w_out, x, ada9, heads, name, plans=()):
    t, d = x.shape
    cw, vw = conv_w.shape[1], o.shape[1]
    dv = vw // heads
    tm = min(ROW_TILE, t)

    def body(cx_ref, halo_ref, o_ref, go_ref, cwt_ref, gn_ref, w_ref, x_ref, ada_ref, xo_ref, m_ref, y_ref):
        u, u1, u2 = _conv_taps(cx_ref, halo_ref, pl.program_id(0) == 0, cw)
        yc = cwt_ref[0:1, :] * u2 + cwt_ref[1:2, :] * u1 + cwt_ref[2:3, :] * u
        y_ref[:, 0:cw] = (cx_ref[:, 0:cw].astype(F32) * yc).astype(BF16)
        for h in range(heads):
            cols = slice(h * dv, (h + 1) * dv)
            _, _, on = _head_norm(o_ref[:, cols], gn_ref[...])
            g = go_ref[:, cols].astype(F32)
            y_ref[:, cw + h * dv:cw + (h + 1) * dv] = (on * (g * _sigmoid(g))).astype(BF16)
        m = _dot(y_ref[...], w_ref[...])
        m_ref[...] = m.astype(BF16)
        xo_ref[...] = x_ref[...] + ada_ref[5:6, :] * m

    return _pcall(
        body, name, [cx, cx, o, go, conv_w, gn, w_out, x, ada9], grid=(t // tm,),
        in_specs=[pl.BlockSpec((tm, 3 * cw), lambda i: (i, 0)),
                  pl.BlockSpec((16, 3 * cw), lambda i: (jnp.maximum(i * (tm // 16) - 1, 0), 0)),
                  pl.BlockSpec((tm, vw), lambda i: (i, 0)), pl.BlockSpec((tm, vw), lambda i: (i, 0)),
                  pl.BlockSpec((3, cw), lambda i: (0, 0)), pl.BlockSpec((1, dv), lambda i: (0, 0)),
                  pl.BlockSpec((cw + vw, d), lambda i: (0, 0)), pl.BlockSpec((tm, d), lambda i: (i, 0)),
                  pl.BlockSpec((N_ADA, d), lambda i: (0, 0))],
        out_shape=[S((t, d), F32), S((t, d), BF16), S((t, cw + vw), BF16)],
        out_specs=[pl.BlockSpec((tm, d), lambda i: (i, 0)), pl.BlockSpec((tm, d), lambda i: (i, 0)),
                   pl.BlockSpec((tm, cw + vw), lambda i: (i, 0))],
        sem=("parallel",), plans=plans)


def _mix_bwd_a(dxo, m, ada9, w_out, cx, o, go, conv_w, gn, heads, name, plans=()):
    t, d = dxo.shape
    cw, vw = conv_w.shape[1], o.shape[1]
    dv = vw // heads
    tm = min(ROW_TILE, t)

    def body(dxo_ref, m_ref, ada_ref, w_ref, cx_ref, halo_ref, o_ref, go_ref, cwt_ref, gn_ref,
             dm_ref, dyc_ref, dcb_ref, do_ref, dgo_ref, dg_ref, dcw_ref, dgn_ref):
        @pl.when(pl.program_id(0) == 0)
        def _():
            dg_ref[...] = jnp.zeros_like(dg_ref)
            dcw_ref[...] = jnp.zeros_like(dcw_ref)
            dgn_ref[...] = jnp.zeros_like(dgn_ref)

        dxo_t = dxo_ref[...]
        dmb = (ada_ref[5:6, :] * dxo_t).astype(BF16)
        dm_ref[...] = dmb
        dg_ref[...] += _rsum8(dxo_t * m_ref[...].astype(F32))
        dy = _dot_nt(dmb, w_ref[...])
        u, u1, u2 = _conv_taps(cx_ref, halo_ref, pl.program_id(0) == 0, cw)
        yc = cwt_ref[0:1, :] * u2 + cwt_ref[1:2, :] * u1 + cwt_ref[2:3, :] * u
        dyv = dy[:, 0:cw]
        dcb_ref[...] = (dyv * yc).astype(BF16)
        dyc = dyv * cx_ref[:, 0:cw].astype(F32)
        dyc_ref[...] = dyc
        dcw_ref[0] += _rsum8(dyc * u2)
        dcw_ref[1] += _rsum8(dyc * u1)
        dcw_ref[2] += _rsum8(dyc * u)
        for h in range(heads):
            cols = slice(h * dv, (h + 1) * dv)
            ohat, rstd, on = _head_norm(o_ref[:, cols], gn_ref[...])
            g = go_ref[:, cols].astype(F32)
            sg = _sigmoid_tanh(g)
            dyg = dy[:, cw + h * dv:cw + (h + 1) * dv]
            dgo_ref[:, cols] = (dyg * on * (sg * (1.0 + g * (1.0 - sg)))).astype(BF16)
            don = dyg * (g * sg)
            dgn_ref[...] += _rsum8(don * ohat)
            tt = don * gn_ref[...]
            do_ref[:, cols] = (rstd * (tt - ohat * jnp.mean(tt * ohat, axis=-1, keepdims=True))).astype(BF16)

    return _pcall(
        body, name, [dxo, m, ada9, w_out, cx, cx, o, go, conv_w, gn], grid=(t // tm,),
        in_specs=[pl.BlockSpec((tm, d), lambda i: (i, 0)), pl.BlockSpec((tm, d), lambda i: (i, 0)),
                  pl.BlockSpec((N_ADA, d), lambda i: (0, 0)), pl.BlockSpec((cw + vw, d), lambda i: (0, 0)),
                  pl.BlockSpec((tm, 3 * cw), lambda i: (i, 0)),
                  pl.BlockSpec((16, 3 * cw), lambda i: (jnp.maximum(i * (tm // 16) - 1, 0), 0)),
                  pl.BlockSpec((tm, vw), lambda i: (i, 0)), pl.BlockSpec((tm, vw), lambda i: (i, 0)),
                  pl.BlockSpec((3, cw), lambda i: (0, 0)), pl.BlockSpec((1, dv), lambda i: (0, 0))],
        out_shape=[S((t, d), BF16), S((t, cw), F32), S((t, cw), BF16), S((t, vw), BF16), S((t, vw), BF16),
                   S((8, d), F32), S((3, 8, cw), F32), S((8, dv), F32)],
        out_specs=[pl.BlockSpec((tm, d), lambda i: (i, 0)), pl.BlockSpec((tm, cw), lambda i: (i, 0)),
                   pl.BlockSpec((tm, cw), lambda i: (i, 0)), pl.BlockSpec((tm, vw), lambda i: (i, 0)),
                   pl.BlockSpec((tm, vw), lambda i: (i, 0)), pl.BlockSpec((8, d), lambda i: (0, 0)),
                   pl.BlockSpec((3, 8, cw), lambda i: (0, 0, 0)), pl.BlockSpec((8, dv), lambda i: (0, 0))],
        sem=("arbitrary",), plans=plans)


def _mix_bwd_b(dyc, cx, dcb, dqk, dvv, dgo, dgl, conv_w, w, x, dxo, ada9, nrm, name):
    t, d = x.shape
    cw = conv_w.shape[1]
    n = w.shape[0]
    tm = min(ROW_TILE, t)
    nt = t // tm
    pieces = [dcb.shape[1], cw, cw, dqk.shape[1], dvv.shape[1], dgo.shape[1], dgl.shape[1]]
    assert sum(pieces) == n

    def body(dyc_ref, nxt_ref, cx_ref, dcb_ref, dqk_ref, dv_ref, dgo_ref, dgl_ref, cwt_ref, w_ref, x_ref, dxo_ref, ada_ref, n_ref,
             dx_ref, dp_ref, dsh_ref, dsc_ref, dn_ref):
        i = pl.program_id(0)

        @pl.when(i == 0)
        def _():
            dsh_ref[...] = jnp.zeros_like(dsh_ref)
            dsc_ref[...] = jnp.zeros_like(dsc_ref)
            dn_ref[...] = jnp.zeros_like(dn_ref)

        dyc_t = dyc_ref[...]
        nxt = jnp.where(i == nt - 1, 0.0, nxt_ref[...])
        row = lax.broadcasted_iota(jnp.int32, (tm, cw), 0)
        d1 = jnp.where(row == tm - 1, nxt[0:1, :], pltpu.roll(dyc_t, tm - 1, 0))
        d2 = jnp.where(row == tm - 2, nxt[0:1, :], jnp.where(row == tm - 1, nxt[1:2, :], pltpu.roll(dyc_t, tm - 2, 0)))
        du = cwt_ref[2:3, :] * dyc_t + cwt_ref[1:2, :] * d1 + cwt_ref[0:1, :] * d2
        c0 = 0
        dp_ref[:, c0:c0 + cw] = dcb_ref[...]
        dp_ref[:, cw:2 * cw] = (du * cx_ref[:, 2 * cw:3 * cw].astype(F32)).astype(BF16)
        dp_ref[:, 2 * cw:3 * cw] = (du * cx_ref[:, cw:2 * cw].astype(F32)).astype(BF16)
        c0 = 3 * cw
        for ref in (dqk_ref, dv_ref, dgo_ref, dgl_ref):
            wd = ref.shape[1]
            dp_ref[:, c0:c0 + wd] = ref[...]
            c0 += wd
        dh = _dot(dp_ref[...], w_ref[...])
        dx, tsh, tsc, tn = _normmod_bwd(dh, x_ref[...], n_ref[...], ada_ref[4:5, :])
        dx_ref[...] = dxo_ref[...] + dx
        dsh_ref[...] += _rsum8(tsh)
        dsc_ref[...] += _rsum8(tsc)
        dn_ref[...] += _rsum8(tn)

    row_spec = lambda wd: pl.BlockSpec((tm, wd), lambda i: (i, 0))
    vec = pl.BlockSpec((8, d), lambda i: (0, 0))
    return pl.pallas_call(
        body, name=name, grid=(nt,),
        in_specs=[row_spec(cw), pl.BlockSpec((8, cw), lambda i: (jnp.minimum((i + 1) * (tm // 8), t // 8 - 1), 0)),
                  row_spec(3 * cw), row_spec(cw), row_spec(dqk.shape[1]), row_spec(dvv.shape[1]), row_spec(dgo.shape[1]),
                  row_spec(dgl.shape[1]), pl.BlockSpec((3, cw), lambda i: (0, 0)), pl.BlockSpec((n, d), lambda i: (0, 0)),
                  row_spec(d), row_spec(d), pl.BlockSpec((N_ADA, d), lambda i: (0, 0)), pl.BlockSpec((1, d), lambda i: (0, 0))],
        out_shape=[S((t, d), F32), S((t, n), BF16), S((8, d), F32), S((8, d), F32), S((8, d), F32)],
        out_specs=[row_spec(d), row_spec(n), vec, vec, vec],
        compiler_params=_cp("arbitrary"),
    )(dyc, dyc, cx, dcb, dqk, dvv, dgo, dgl, conv_w, w, x, dxo, ada9, nrm)


def _ffn_out_loss(s, w_out, x, ada9, g_row, res_scale, target, nrm, name):
    nb, t, bw = s.shape
    d = x.shape[1]
    tm = min(ROW_TILE, t)
    nt = t // tm

    def body(s_ref, w_ref, x_ref, ada_ref, tg_ref, n_ref, f_ref, loss_ref, dx_ref, dn_ref, acc_ref):
        i = pl.program_id(0)

        @pl.when(i == 0)
        def _():
            acc_ref[...] = jnp.zeros_like(acc_ref)
            dn_ref[...] = jnp.zeros_like(dn_ref)

        f = _dot(s_ref[0], w_ref[0])
        for b in range(1, nb):
            f = f + _dot(s_ref[b], w_ref[b])
        f_ref[...] = f.astype(BF16)
        xt = x_ref[...] + (res_scale * ada_ref[g_row:g_row + 1, :]) * f
        rstd = lax.rsqrt(jnp.mean(xt * xt, axis=-1, keepdims=True) + EPS)
        xhat = xt * rstd
        err = xhat * n_ref[...] - tg_ref[...]
        acc_ref[...] += _rsum8(err * err)
        dy = err * (1.0 / d)
        dn_ref[...] += _rsum8(dy * xhat)
        dxhat = dy * n_ref[...]
        dx_ref[...] = rstd * (dxhat - xhat * jnp.mean(dxhat * xhat, axis=-1, keepdims=True))

        @pl.when(i == nt - 1)
        def _():
            loss_ref[...] = jnp.full(loss_ref.shape, (0.5 / d) * jnp.sum(acc_ref[...]), F32)

    return pl.pallas_call(
        body, name=name, grid=(nt,),
        in_specs=[pl.BlockSpec((nb, tm, bw), lambda i: (0, i, 0)), pl.BlockSpec((nb, bw, d), lambda i: (0, 0, 0)),
                  pl.BlockSpec((tm, d), lambda i: (i, 0)), pl.BlockSpec((N_ADA, d), lambda i: (0, 0)),
                  pl.BlockSpec((tm, d), lambda i: (i, 0)), pl.BlockSpec((1, d), lambda i: (0, 0))],
        out_shape=[S((t, d), BF16), S((1, 128), F32), S((t, d), F32), S((8, d), F32)],
        out_specs=[pl.BlockSpec((tm, d), lambda i: (i, 0)), pl.BlockSpec((1, 128), lambda i: (0, 0)),
                   pl.BlockSpec((tm, d), lambda i: (i, 0)), pl.BlockSpec((8, d), lambda i: (0, 0))],
        scratch_shapes=[pltpu.VMEM((8, d), F32)],
        compiler_params=_cp("arbitrary"),
    )(s, w_out, x, ada9, target, nrm)


def _pack_smalls(vec_parts, dcw, dbg, dgn, dwg, loss_v, rank, name):
    d = vec_parts[0].shape[1]
    cw, kw, dv = dcw.shape[2], dbg.shape[1], dgn.shape[1]
    nv = len(vec_parts)
    loss_row = nv + 2 + rank * kw // d
    assert 2 * cw == d and cw + kw + dv <= d and (rank * kw) % d == 0 and loss_row < PACK_ROWS
    per_row = d // kw

    def body(*refs):
        vrefs, (dcw_ref, dbg_ref, dgn_ref, dwg_ref, loss_ref, o_ref) = refs[:nv], refs[nv:]
        o_ref[...] = jnp.zeros_like(o_ref)
        o_ref[loss_row:loss_row + 1, 0:loss_ref.shape[1]] = loss_ref[...]
        for r, ref in enumerate(vrefs):
            o_ref[r:r + 1, :] = jnp.sum(ref[...], axis=0, keepdims=True)
        o_ref[nv:nv + 1, 0:cw] = jnp.sum(dcw_ref[0], axis=0, keepdims=True)
        o_ref[nv:nv + 1, cw:2 * cw] = jnp.sum(dcw_ref[1], axis=0, keepdims=True)
        o_ref[nv + 1:nv + 2, 0:cw] = jnp.sum(dcw_ref[2], axis=0, keepdims=True)
        o_ref[nv + 1:nv + 2, cw:cw + kw] = jnp.sum(dbg_ref[...], axis=0, keepdims=True)
        o_ref[nv + 1:nv + 2, cw + kw:cw + kw + dv] = jnp.sum(dgn_ref[...], axis=0, keepdims=True)
        for r in range(rank):
            o_ref[nv + 2 + r // per_row:nv + 3 + r // per_row, (r % per_row) * kw:(r % per_row + 1) * kw] = dwg_ref[r:r + 1, :]

    return pl.pallas_call(body, name=name, out_shape=S((PACK_ROWS, d), F32), compiler_params=_cp())(*vec_parts, dcw, dbg, dgn, dwg, loss_v)


def _sum_slots(a, name):
    def body(a_ref, o_ref):
        acc = a_ref[0]
        for s in range(1, NDEV):
            acc = acc + a_ref[s]
        o_ref[...] = acc

    return pl.pallas_call(body, name=name, out_shape=S(a.shape[1:], F32), compiler_params=_cp())(a)


def _adamw(w, g, m, v):
    m = ADAM_B1 * m + (1.0 - ADAM_B1) * g
    v = ADAM_B2 * v + (1.0 - ADAM_B2) * (g * g)
    m_hat = m / (1.0 - ADAM_B1 ** ADAM_STEP)
    v_hat = v / (1.0 - ADAM_B2 ** ADAM_STEP)
    return -ADAM_LR * (m_hat / (jnp.sqrt(v_hat) + ADAM_EPS) + ADAM_WD * w), m, v


def _adam_slots(recv, w, m, v, name):
    r, c = w.shape
    slots = recv.shape[0]
    tr = _row_tile(r, c)

    def body(recv_ref, w_ref, m_ref, v_ref, g_ref, d_ref, mo_ref, vo_ref):
        g = recv_ref[0].astype(F32)
        for s in range(1, slots):
            g = g + recv_ref[s].astype(F32)
        g_ref[...] = g
        d_ref[...], mo_ref[...], vo_ref[...] = _adamw(w_ref[...], g, m_ref[...], v_ref[...])

    blk = pl.BlockSpec((tr, c), lambda i: (i, 0))
    return pl.pallas_call(
        body, name=name, grid=(r // tr,),
        in_specs=[pl.BlockSpec((slots, tr, c), lambda i: (0, i, 0)), blk, blk, blk],
        out_shape=[S((r, c), F32)] * 4, out_specs=[blk] * 4, compiler_params=_cp("parallel"),
    )(recv, w, m, v)


def _adam_w_ada(act_t, dada, w, m, v, name):
    r, c = w.shape
    tr = 128
    nb = act_t.shape[1]

    def body(a_ref, da_ref, w_ref, m_ref, v_ref, g_ref, d_ref, mo_ref, vo_ref):
        g = a_ref[:, 0:1] * da_ref[0:1, :]
        for b in range(1, nb):
            g = g + a_ref[:, b:b + 1] * da_ref[b:b + 1, :]
        g_ref[...] = g
        d_ref[...], mo_ref[...], vo_ref[...] = _adamw(w_ref[...], g, m_ref[...], v_ref[...])

    blk = pl.BlockSpec((tr, c), lambda i: (i, 0))
    return pl.pallas_call(
        body, name=name, grid=(r // tr,),
        in_specs=[pl.BlockSpec((tr, nb), lambda i: (i, 0)), pl.BlockSpec((nb, c), lambda i: (0, 0)), blk, blk, blk],
        out_shape=[S((r, c), F32)] * 4, out_specs=[blk] * 4, compiler_params=_cp("parallel"),
    )(act_t, dada, w, m, v)


def _adam_smalls(ws, gs, ms, vs, name):
    n = len(ws)

    def body(*refs):
        w_r, g_r, m_r, v_r = (refs[k * n:(k + 1) * n] for k in range(4))
        d_o, m_o, v_o = (refs[(4 + k) * n:(5 + k) * n] for k in range(3))
        for i in range(n):
            d_o[i][...], m_o[i][...], v_o[i][...] = _adamw(w_r[i][...], g_r[i][...], m_r[i][...], v_r[i][...])

    shapes = [S(w.shape, F32) for w in ws]
    outs = pl.pallas_call(body, name=name, out_shape=shapes * 3, compiler_params=_cp())(*ws, *gs, *ms, *vs)
    return outs[:n], outs[n:2 * n], outs[2 * n:]


def kernel(x, c, w_ada, b_ada, norm_ffn1, w_ffn1_in, w_ffn1_out, norm_mix, w_mix_in, conv_w, w_gk2, b_gk, gla_norm, w_mix_out, norm_ffn2, w_ffn2_in, w_ffn2_out, norm_final, loss_target, m_w_ada, m_b_ada, m_norm_ffn1, m_w_ffn1_in, m_w_ffn1_out, m_norm_mix, m_w_mix_in, m_conv_w, m_w_gk2, m_b_gk, m_gla_norm, m_w_mix_out, m_norm_ffn2, m_w_ffn2_in, m_w_ffn2_out, m_norm_final, v_w_ada, v_b_ada, v_norm_ffn1, v_w_ffn1_in, v_w_ffn1_out, v_norm_mix, v_w_mix_in, v_conv_w, v_w_gk2, v_b_gk, v_gla_norm, v_w_mix_out, v_norm_ffn2, v_w_ffn2_in, v_w_ffn2_out, v_norm_final):
    t, d = x.shape[1], x.shape[2]
    x0, tgt = x[0], loss_target[0]
    rank, kw = w_gk2.shape[1], w_gk2.shape[2] * NDEV
    cw = conv_w.shape[2] * NDEV
    dv = gla_norm.shape[1]
    vw = d - cw
    heads = vw // dv
    mix_cols = w_mix_in.shape[2]
    widths = [3 * cw, 2 * kw, vw, vw, 128]
    n_proj = 3 * cw + 2 * kw + 2 * vw + rank
    assert n_proj == mix_cols * NDEV and rank <= 128
    me = 4 * lax.axis_index("x") + 2 * lax.axis_index("y") + lax.axis_index("c")

    core = lax.axis_index("c").astype(jnp.int32).reshape(1)
    bf = lambda a: a[0].astype(BF16)
    bft = lambda a: a[0].T.astype(BF16)
    nb = NDEV // 2

    first_needed = [bft(w_ffn1_in), conv_w[0], w_gk2[0]]
    (ada_row, act_all), (near_blocks,) = _ada_rows(c, w_ada[0], b_ada, "ada_rows", plans=[_plan_gather(first_needed, chips=(0,))])
    ada9 = ada_row.reshape(N_ADA, d)
    my_y = lax.axis_index("y").astype(jnp.int32)
    near, far = (2 * my_y).reshape(1), (2 - 2 * my_y).reshape(1)

    (h1, gu1, s1), ((w1i, cwt_all, wg_all),) = _ffn_in(
        x0, ada9, norm_ffn1, near_blocks[0], 0, 1, "ffn1_in_near", some=(near, nb // 2),
        plans=[_plan_gather(first_needed, chips=(1, 2), own=False, into=list(near_blocks))])
    (h1, gu1, s1), ((w1o, wmo),) = _ffn_in(x0, ada9, norm_ffn1, w1i, 0, 1, "ffn1_in_far", some=(far, nb // 2), begun=(gu1, s1),
                                           plans=[_plan_gather([bf(w_ffn1_out), bf(w_mix_out)])])
    w1o = w1o.reshape(nb, -1, d)
    cwt = cwt_all.transpose(1, 0, 2).reshape(conv_w.shape[1], cw)
    wg = jnp.pad(wg_all.transpose(1, 0, 2).reshape(rank, kw), ((0, 128 - rank), (0, 0))).astype(BF16)
    (x1, f1), ((wmi,),) = _ffn_out(s1, w1o, x0, ada9, 2, 0.5, "ffn1_out", plans=[_plan_gather([bft(w_mix_in)])])
    wmi = jnp.pad(wmi.reshape(n_proj, d), ((0, sum(widths) - n_proj), (0, 0)))
    wmo = wmo.reshape(cw + vw, d)
    w2i_mine = bft(w_ffn2_in)
    quarter = w2i_mine.shape[0] // 4
    part = lambda k, n, into=None: _plan_gather([w2i_mine], rows=(k * quarter, n * quarter), into=into)
    (h2, cx, qk, vv, go, gl), ((w2i,),) = _mix_in(x1, ada9, norm_mix, wmi, widths, [BF16, F32, BF16, BF16, F32], "mix_in",
                                                 plans=[part(0, 2)])
    (o, lg, sall), ((w2i,),) = _gla_fwd(qk, vv, gl, wg, b_gk, heads, "gla_fwd", plans=[part(2, 1, [w2i])])
    (x2, mm, ycat), ((w2i,),) = _mix_out(cx, o, go, cwt, gla_norm, wmo, x1, ada9, heads, "mix_out", plans=[part(3, 1, [w2i])])
    (h3, gu3, s3), ((w2o,),) = _ffn_in(x2, ada9, norm_ffn2, w2i, 6, 7, "ffn2_in", plans=[_plan_gather([bf(w_ffn2_out)])])
    w2o = w2o.reshape(nb, -1, d)
    f3, loss_v, dx3, dnf = _ffn_out_loss(s3, w2o, x2, ada9, 8, 0.5, tgt, norm_final.reshape(1, d), "ffn2_out_loss")

    dx2, r2i, sum2o, (dsh3, dsc3, dg3, dn3), _, _ = _ffn_backward(
        dx3, x2, h3, gu3, s3, f3, ada9, norm_ffn2, w2i, w2o, (6, 7, 8), core, "ffn2_bwd")
    (dm, dyc, dcb, do, dgo, dg2, dcw, dgn), ((r2o,),) = _mix_bwd_a(dx2, mm, ada9, wmo, cx, o, go, cwt, gla_norm, heads, "mix_bwd_a",
                                                                 plans=[_plan_chip_swap([sum2o])])
    dqk, dvv, dgl, dwg, dbg = _gla_bwd(qk, vv, lg, do, sall, gl, wg, heads, "gla_bwd")
    dx1, dproj, dsh2, dsc2, dnm = _mix_bwd_b(dyc, cx, dcb, dqk, dvv, dgo, dgl, cwt, wmi, x1, dx2, ada9, norm_mix, "mix_bwd_b")
    n_pad = sum(widths)
    tn = n_pad // 5
    dwmi, _ = _tn_matmul(dproj, h2, lambda tk: (tk, tn), lambda sb, k: (k, sb), lambda tk: (tk, d), lambda sb, k: (k, 0),
                         (n_pad, d), (tn, d), lambda sb, k: (sb, 0), 5, "mix_dwin")
    dwmo, _ = _tn_matmul(ycat, dm, lambda tk: (tk, cw + vw), lambda sb, k: (k, 0), lambda tk: (tk, d), lambda sb, k: (k, 0),
                         (cw + vw, d), (cw + vw, d), lambda sb, k: (0, 0), 1, "mix_dwout")
    dwmi = dwmi[:n_proj].reshape(NDEV, mix_cols, d)
    dwmo = dwmo.reshape(NDEV, -1, d)
    dx0, r1i, sum1o, (dsh1, dsc1, dg1, dn1), _, ((rmi, rmo),) = _ffn_backward(
        dx1, x0, h1, gu1, s1, f1, ada9, norm_ffn1, w1i, w1o, (0, 1, 2), core, "ffn1_bwd",
        ds_plans=[_plan_sibling_swap([dwmi, dwmo])],
        dwin_plans=lambda moved: [_plan_chip_swap([_pair_add(dwmi, moved[0][0], core, "mix_dwin_add"),
                                                   _pair_add(dwmo, moved[0][1], core, "mix_dwout_add")])])
    pack = _pack_smalls([dn1, dnm, dn3, dnf, dsh1, dsc1, dg1, dsh2, dsc2, dg2, dsh3, dsc3, dg3], dcw, dbg, dgn, dwg, loss_v, rank, "pack_smalls")
    (r1o,), (pack_all,) = _exchange([_plan_chip_swap([sum1o]), _plan_all_to_all([pack], True)], "grads_last")
    tot = _sum_slots(pack_all, "sum_smalls")

    res = {}
    for nm, recv, w, m, v in (("w_ffn1_out", r1o, w_ffn1_out, m_w_ffn1_out, v_w_ffn1_out), ("w_mix_out", rmo, w_mix_out, m_w_mix_out, v_w_mix_out),
                              ("w_ffn2_out", r2o, w_ffn2_out, m_w_ffn2_out, v_w_ffn2_out)):
        res[nm] = [a[None] for a in _adam_slots(recv, w[0], m[0], v[0], "adam_" + nm)]
    for nm, recv, w, m, v in (("w_ffn1_in", r1i, w_ffn1_in, m_w_ffn1_in, v_w_ffn1_in), ("w_mix_in", rmi, w_mix_in, m_w_mix_in, v_w_mix_in),
                              ("w_ffn2_in", r2i, w_ffn2_in, m_w_ffn2_in, v_w_ffn2_in)):
        res[nm] = [a.T[None] for a in _adam_slots(recv, w[0].T, m[0].T, v[0].T, "adam_" + nm)]

    cols_ada = w_ada.shape[2]
    dada_all = pack_all[:, 4:4 + N_ADA, :].reshape(NDEV, N_ADA * d)
    dada_mine = lax.dynamic_slice_in_dim(dada_all, me * cols_ada, cols_ada, axis=1)
    res["w_ada"] = [a[None] for a in _adam_w_ada(act_all.T, dada_mine, w_ada[0], m_w_ada[0], v_w_ada[0], "adam_w_ada")]

    nv = 4 + N_ADA
    g_small = {
        "b_ada": tot[4:nv].reshape(1, N_ADA * d),
        "norm_ffn1": tot[0:1], "norm_mix": tot[1:2], "norm_ffn2": tot[2:3], "norm_final": tot[3:4],
        "conv_w": lax.dynamic_slice_in_dim(
            jnp.concatenate([tot[nv:nv + 1, 0:cw], tot[nv:nv + 1, cw:2 * cw], tot[nv + 1:nv + 2, 0:cw]], axis=0), me * (cw // NDEV), cw // NDEV, axis=1),
        "w_gk2": lax.dynamic_slice_in_dim(tot[nv + 2:nv + 2 + rank * kw // d].reshape(rank, kw), me * (kw // NDEV), kw // NDEV, axis=1),
        "b_gk": tot[nv + 1:nv + 2, cw:cw + kw],
        "gla_norm": tot[nv + 1:nv + 2, cw + kw:cw + kw + dv],
    }
    small = {"b_ada": (b_ada, m_b_ada, v_b_ada), "norm_ffn1": (norm_ffn1, m_norm_ffn1, v_norm_ffn1), "norm_mix": (norm_mix, m_norm_mix, v_norm_mix),
             "norm_ffn2": (norm_ffn2, m_norm_ffn2, v_norm_ffn2), "norm_final": (norm_final, m_norm_final, v_norm_final),
             "conv_w": (conv_w, m_conv_w, v_conv_w), "w_gk2": (w_gk2, m_w_gk2, v_w_gk2), "b_gk": (b_gk, m_b_gk, v_b_gk),
             "gla_norm": (gla_norm, m_gla_norm, v_gla_norm)}
    names = list(small)
    flat = lambda a: a.reshape(-1, a.shape[-1])
    dl, mo, vo = _adam_smalls([flat(small[n][0]) for n in names], [g_small[n] for n in names],
                              [flat(small[n][1]) for n in names], [flat(small[n][2]) for n in names], "adam_smalls")
    for i, n in enumerate(names):
        shp = small[n][0].shape
        res[n] = [g_small[n].reshape(shp), dl[i].reshape(shp), mo[i].reshape(shp), vo[i].reshape(shp)]

    loss = tot[nv + 2 + rank * kw // d, 0]
    order = ["w_ada", "b_ada", "norm_ffn1", "w_ffn1_in", "w_ffn1_out", "norm_mix", "w_mix_in", "conv_w", "w_gk2", "b_gk", "gla_norm",
             "w_mix_out", "norm_ffn2", "w_ffn2_in", "w_ffn2_out", "norm_final"]
    return (loss, dx0[None], *[res[n][0] for n in order], *[res[n][1] for n in order], *[res[n][2] for n in order], *[res[n][3] for n in order])
```

```python
import collections
import functools

import jax
import jax.numpy as jnp
from jax import lax
from jax.experimental import pallas as pl
from jax.experimental.pallas import tpu as pltpu

F32 = jnp.float32
BF16 = jnp.bfloat16
S = jax.ShapeDtypeStruct

NDEV = 8
EPS = 1e-6
GATE_NORMALIZER = 16.0
CHUNK = 128
N_ADA = 9
ADAM_LR, ADAM_B1, ADAM_B2, ADAM_EPS, ADAM_WD, ADAM_STEP = 0.001, 0.9, 0.999, 1e-08, 0.01, 10
V7X_VMEM_LIMIT = 56 * 1024 * 1024
ROW_TILE = 512
WIDE_ROW_TILE = 1024
K_TILE = 1024
PACK_ROWS = 24
ANY = pl.BlockSpec(memory_space=pl.ANY)


def _cp(*sem):
    return pltpu.CompilerParams(dimension_semantics=sem or None, vmem_limit_bytes=V7X_VMEM_LIMIT)


def _dot(a, b):
    return jnp.dot(a, b, preferred_element_type=F32)


def _dot_nt(a, b):
    return lax.dot_general(a, b, (((1,), (1,)), ((), ())), preferred_element_type=F32)


def _dot_tn(a, b):
    return lax.dot_general(a, b, (((0,), (0,)), ((), ())), preferred_element_type=F32)


def _rsum8(a):
    r, c = a.shape
    return jnp.sum(a.reshape(r // 8, 8, c), axis=0)


def _row_tile(r, c):
    for cand in (256, 128, 176, 88, 64, 32, 16, 8):
        if r % cand == 0 and cand * c * 4 <= 1024 * 1024:
            return cand
    return r


def _sigmoid(x):
    return 1.0 / (1.0 + jnp.exp(-x))


def _sigmoid_tanh(x):
    return 0.5 * jnp.tanh(0.5 * x) + 0.5


def _normmod(x, nrm, sc, sh):
    rstd = lax.rsqrt(jnp.mean(x * x, axis=-1, keepdims=True) + EPS)
    xhat = x * rstd
    return xhat, rstd, (xhat * nrm) * (1.0 + sc) + sh


def _normmod_bwd(dh, x, nrm, sc):
    rstd = lax.rsqrt(jnp.mean(x * x, axis=-1, keepdims=True) + EPS)
    xhat = x * rstd
    dxhat = dh * (nrm * (1.0 + sc))
    dx = rstd * (dxhat - xhat * jnp.mean(dxhat * xhat, axis=-1, keepdims=True))
    return dx, dh, dh * (xhat * nrm), dh * ((1.0 + sc) * xhat)


def _place():
    x, y, c = lax.axis_index("x"), lax.axis_index("y"), lax.axis_index("c")
    return x, y, c, 4 * x + 2 * y + c


def _peer(x, y, c, k):
    px = 1 - x if k & 4 else x
    py = 1 - y if k & 2 else y
    pc = 1 - c if k & 1 else c
    return (px, py, pc), 4 * px + 2 * py + pc


def _remote(src, dst, send_sem, recv_sem, peer):
    return pltpu.make_async_remote_copy(src_ref=src, dst_ref=dst, send_sem=send_sem, recv_sem=recv_sem,
                                        device_id=peer, device_id_type=pl.DeviceIdType.MESH)


_Plan = collections.namedtuple("_Plan", "inputs out_shapes sem_shapes start finish aliases", defaults=({},))


def _plan_all_to_all(xs, gather):
    n = len(xs)

    def copies(ins, outs, sems, landed):
        send_sems, recv_sems, local_sems = sems
        x, y, c, me = _place()
        local = [pltpu.make_async_copy(ins[i] if gather else ins[i].at[me], outs[i].at[me], local_sems.at[i]) for i in range(n)]
        remote = []
        for k in range(1, NDEV):
            peer, pid = _peer(x, y, c, k)
            for i in range(n):
                remote.append(_remote(ins[i] if gather else ins[i].at[pid], outs[i].at[pid if landed else me],
                                      send_sems.at[i, k - 1], recv_sems.at[i, k - 1], peer))
        return local, remote

    def start(ins, outs, sems):
        local, remote = copies(ins, outs, sems, False)
        for cp in local + remote:
            cp.start()

    def finish(ins, outs, sems):
        local, remote = copies(ins, outs, sems, True)
        for cp in remote + local:
            cp.wait()

    return _Plan(list(xs), [S((NDEV,) + a.shape, a.dtype) if gather else S(a.shape, a.dtype) for a in xs],
                 [pltpu.SemaphoreType.DMA((n, NDEV - 1)), pltpu.SemaphoreType.DMA((n, NDEV - 1)), pltpu.SemaphoreType.DMA((n,))],
                 start, finish)


def _other_chips(x, y):
    return [(1 - x, y), (x, 1 - y), (1 - x, 1 - y)]


def _plan_gather(xs, rows=None, into=None):
    n = len(xs)

    def copies(ins, outs, sems, rest):
        send_sems, recv_sems, local_sems = sems
        x, y, c, me = _place()
        sib, sib_id = (x, y, 1 - c), 4 * x + 2 * y + 1 - c
        chips = _other_chips(x, y)
        mine = lambda i: ins[i] if rows is None else ins[i].at[pl.ds(*rows)]
        slot_of = lambda i, s: outs[i].at[s] if rows is None else outs[i].at[s, pl.ds(*rows)]
        local = [pltpu.make_async_copy(mine(i), slot_of(i, me), local_sems.at[i]) for i in range(n)]
        first = [_remote(mine(i), slot_of(i, me), send_sems.at[i, 0], recv_sems.at[i, 0], sib) for i in range(n)]
        first += [_remote(mine(i), slot_of(i, me), send_sems.at[i, 1 + j], recv_sems.at[i, 1 + j], (px, py, c))
                  for j, (px, py) in enumerate(chips) for i in range(n)]
        if not rest:
            return local, first
        from_sibling = [_remote(mine(i), slot_of(i, sib_id), send_sems.at[i, 0], recv_sems.at[i, 0], sib) for i in range(n)]
        arrive, forward = [], []
        for j, (px, py) in enumerate(chips):
            s = 4 * px + 2 * py
            arrive.append([_remote(mine(i), slot_of(i, s + c), send_sems.at[i, 1 + j], recv_sems.at[i, 1 + j], (px, py, c)) for i in range(n)])
            forward.append([_remote(slot_of(i, s + c), slot_of(i, s + c), send_sems.at[i, 4 + j], recv_sems.at[i, 4 + j], sib) for i in range(n)])
            from_sibling += [_remote(mine(i), slot_of(i, s + 1 - c), send_sems.at[i, 4 + j], recv_sems.at[i, 4 + j], sib) for i in range(n)]
        return local, first, arrive, forward, from_sibling

    def start(ins, outs, sems):
        local, first = copies(ins, outs, sems, False)
        for cp in local + first:
            cp.start()

    def finish(ins, outs, sems):
        local, first, arrive, forward, from_sibling = copies(ins, outs, sems, True)
        for landed, onward in zip(arrive, forward):
            for cp in landed:
                cp.wait_recv()
            for cp in onward:
                cp.start()
        for cp in from_sibling:
            cp.wait_recv()
        for cp in first + [cp for onward in forward for cp in onward]:
            cp.wait_send()
        for cp in local:
            cp.wait()

    return _Plan(list(xs) + list(into or []), [S((NDEV,) + a.shape, a.dtype) for a in xs],
                 [pltpu.SemaphoreType.DMA((n, NDEV - 1)), pltpu.SemaphoreType.DMA((n, NDEV - 1)), pltpu.SemaphoreType.DMA((n,))],
                 start, finish, {n + i: i for i in range(len(into or []))})


def _plan_sibling_swap(gs):
    n = len(gs)

    def copies(ins, outs, sems):
        send_sems, recv_sems = sems
        x, y, c, _ = _place()
        return [_remote(ins[i].at[2 * j + 1 - c], outs[i].at[j], send_sems.at[i, j], recv_sems.at[i, j], (x, y, 1 - c))
                for i in range(n) for j in range(NDEV // 2)]

    def start(ins, outs, sems):
        for cp in copies(ins, outs, sems):
            cp.start()

    def finish(ins, outs, sems):
        for cp in copies(ins, outs, sems):
            cp.wait()

    return _Plan(list(gs), [S((NDEV // 2,) + a.shape[1:], a.dtype) for a in gs],
                 [pltpu.SemaphoreType.DMA((n, NDEV // 2)), pltpu.SemaphoreType.DMA((n, NDEV // 2))], start, finish)


def _pair_add(g, r1, core, name):
    _, r, c = g.shape
    tr = r if r * c * 2 <= 2 * 1024 * 1024 else _row_tile(r, c)

    def body(core_ref, g_ref, r_ref, o_ref):
        o_ref[...] = (g_ref[...].astype(F32) + r_ref[...].astype(F32)).astype(BF16)

    return pl.pallas_call(
        body, name=name,
        grid_spec=pltpu.PrefetchScalarGridSpec(
            num_scalar_prefetch=1, grid=(NDEV // 2, r // tr),
            in_specs=[pl.BlockSpec((None, tr, c), lambda j, k, core_ref: (2 * j + core_ref[0], k, 0)),
                      pl.BlockSpec((None, tr, c), lambda j, k, core_ref: (j, k, 0))],
            out_specs=pl.BlockSpec((None, tr, c), lambda j, k, core_ref: (j, k, 0))),
        out_shape=S((NDEV // 2, r, c), BF16), compiler_params=_cp("parallel", "parallel"),
    )(core, g, r1)


def _plan_chip_swap(ps):
    n = len(ps)

    def copies(ins, outs, sems, landed):
        send_sems, recv_sems, local_sems = sems
        x, y, c, _ = _place()
        mine = 2 * x + y
        local = [pltpu.make_async_copy(ins[i].at[mine], outs[i].at[mine], local_sems.at[i]) for i in range(n)]
        remote = [_remote(ins[i].at[2 * px + py], outs[i].at[2 * px + py if landed else mine], send_sems.at[i, j], recv_sems.at[i, j], (px, py, c))
                  for j, (px, py) in enumerate(_other_chips(x, y)) for i in range(n)]
        return local, remote

    def start(ins, outs, sems):
        local, remote = copies(ins, outs, sems, False)
        for cp in local + remote:
            cp.start()

    def finish(ins, outs, sems):
        local, remote = copies(ins, outs, sems, True)
        for cp in remote + local:
            cp.wait()

    return _Plan(list(ps), [S(a.shape, a.dtype) for a in ps],
                 [pltpu.SemaphoreType.DMA((n, 3)), pltpu.SemaphoreType.DMA((n, 3)), pltpu.SemaphoreType.DMA((n,))], start, finish)


def _pcall(body, name, args, in_specs, out_shape, out_specs, grid=(), scratch_shapes=(), sem=(), plans=()):
    n_in, n_out, n_scr = len(args), len(out_shape), len(scratch_shapes)
    counts = [(len(p.inputs), len(p.out_shapes), len(p.sem_shapes)) for p in plans]
    c_args = [a for p in plans for a in p.inputs]
    c_outs = [s for p in plans for s in p.out_shapes]
    c_sems = [s for p in plans for s in p.sem_shapes]

    def wrapped(*refs):
        cuts = [n_in, len(c_args), n_out, len(c_outs), n_scr, len(c_sems)]
        ins, c_in, outs, c_out, scr, c_sem = [refs[sum(cuts[:k]):sum(cuts[:k + 1])] for k in range(6)]

        def halves(which):
            a = b = s = 0
            for p, (na, nb, ns) in zip(plans, counts):
                getattr(p, which)(c_in[a:a + na], c_out[b:b + nb], c_sem[s:s + ns])
                a, b, s = a + na, b + nb, s + ns

        if not plans:
            body(*ins, *outs, *scr)
        elif not grid:
            halves("start")
            body(*ins, *outs, *scr)
            halves("finish")
        else:
            first = functools.reduce(jnp.logical_and, [pl.program_id(a) == 0 for a in range(len(grid))])
            last = functools.reduce(jnp.logical_and, [pl.program_id(a) == grid[a] - 1 for a in range(len(grid))])
            pl.when(first)(lambda: halves("start"))
            body(*ins, *outs, *scr)
            pl.when(last)(lambda: halves("finish"))

    aliases, a, b = {}, n_in, n_out
    for p, (na, nb, _) in zip(plans, counts):
        aliases.update({a + k: b + v for k, v in p.aliases.items()})
        a, b = a + na, b + nb
    res = pl.pallas_call(
        wrapped, name=name, grid=grid, in_specs=list(in_specs) + [ANY] * len(c_args),
        out_shape=list(out_shape) + c_outs, out_specs=list(out_specs) + [ANY] * len(c_outs),
        scratch_shapes=list(scratch_shapes) + c_sems, input_output_aliases=aliases,
        compiler_params=_cp(*(("arbitrary",) * len(grid) if plans else sem)),
    )(*args, *c_args)
    c_res, b = [], n_out
    for _, nb, _ in counts:
        c_res.append(res[b:b + nb])
        b += nb
    return res[:n_out], c_res


def _exchange(plans, name):
    return _pcall(lambda: None, name, [], [], [], [], plans=plans)[1]


def _ada_rows(c, w_ada, b_ada, name, plans=()):
    d, cols = c.shape[1], w_ada.shape[1]
    gather_c = _plan_all_to_all([c], True)
    gather_p = _plan_all_to_all([S((NDEV, cols), F32)], True)
    n_sem = len(gather_c.sem_shapes)

    def body(c_ref, w_ref, b_ref, ada_ref, act_ref, c_all, p_mine, p_all, *sems):
        gather_c.start([c_ref], [c_all], sems[:n_sem])
        gather_c.finish([c_ref], [c_all], sems[:n_sem])
        for s in range(NDEV):
            cc = c_all[s]
            act_ref[s:s + 1, :] = cc * _sigmoid(cc)
        p_mine[...] = _dot(act_ref[...].astype(BF16), w_ref[...].astype(BF16))
        gather_p.start([p_mine], [p_all], sems[n_sem:])
        gather_p.finish([p_mine], [p_all], sems[n_sem:])
        me = _place()[3]
        for s in range(NDEV):
            ada_ref[:, s * cols:(s + 1) * cols] = p_all[s, pl.ds(me, 1), :] + b_ref[:, s * cols:(s + 1) * cols]

    whole = pl.BlockSpec(memory_space=pltpu.VMEM)
    return _pcall(body, name, [c, w_ada, b_ada], [whole] * 3, [S((1, NDEV * cols), F32), S((NDEV, d), F32)], [whole] * 2,
                  scratch_shapes=[pltpu.VMEM((NDEV,) + c.shape, F32), pltpu.VMEM((NDEV, cols), F32), pltpu.VMEM((NDEV, NDEV, cols), F32)]
                  + gather_c.sem_shapes + gather_p.sem_shapes, plans=plans)


def _ffn_in(x, ada9, nrm, w_in, sh_row, sc_row, name, plans=()):
    t, d = x.shape
    nb, bw = w_in.shape[0] // 2, w_in.shape[1]
    tm = min(WIDE_ROW_TILE, t)

    def body(x_ref, ada_ref, n_ref, wg_ref, wu_ref, h_ref, gu_ref, s_ref):
        @pl.when(pl.program_id(1) == 0)
        def _():
            _, _, h = _normmod(x_ref[...], n_ref[...], ada_ref[sc_row:sc_row + 1, :], ada_ref[sh_row:sh_row + 1, :])
            h_ref[...] = h.astype(BF16)

        h = h_ref[...]
        g = _dot_nt(h, wg_ref[...])
        u = _dot_nt(h, wu_ref[...])
        gu_ref[0] = g.astype(BF16)
        gu_ref[1] = u.astype(BF16)
        s_ref[...] = (g * _sigmoid(g) * u).astype(BF16)

    return _pcall(
        body, name, [x, ada9, nrm, w_in, w_in], grid=(t // tm, nb),
        in_specs=[pl.BlockSpec((tm, d), lambda i, j: (i, 0)), pl.BlockSpec((N_ADA, d), lambda i, j: (0, 0)),
                  pl.BlockSpec((1, d), lambda i, j: (0, 0)),
                  pl.BlockSpec((None, bw, d), lambda i, j: (j, 0, 0)), pl.BlockSpec((None, bw, d), lambda i, j: (j + nb, 0, 0))],
        out_shape=[S((t, d), BF16), S((2, nb, t, bw), BF16), S((nb, t, bw), BF16)],
        out_specs=[pl.BlockSpec((tm, d), lambda i, j: (i, 0)), pl.BlockSpec((2, None, tm, bw), lambda i, j: (0, j, i, 0)),
                   pl.BlockSpec((None, tm, bw), lambda i, j: (j, i, 0))],
        sem=("parallel", "arbitrary"), plans=plans)


def _ffn_out(s, w_out, x, ada9, g_row, res_scale, name, plans=()):
    nb, t, bw = s.shape
    d = x.shape[1]
    tm = min(ROW_TILE, t)

    def body(s_ref, w_ref, x_ref, ada_ref, xo_ref, f_ref):
        acc = _dot(s_ref[0], w_ref[0])
        for b in range(1, nb):
            acc = acc + _dot(s_ref[b], w_ref[b])
        f_ref[...] = acc.astype(BF16)
        xo_ref[...] = x_ref[...] + (res_scale * ada_ref[g_row:g_row + 1, :]) * acc

    return _pcall(
        body, name, [s, w_out, x, ada9], grid=(t // tm,),
        in_specs=[pl.BlockSpec((nb, tm, bw), lambda i: (0, i, 0)), pl.BlockSpec((nb, bw, d), lambda i: (0, 0, 0)),
                  pl.BlockSpec((tm, d), lambda i: (i, 0)), pl.BlockSpec((N_ADA, d), lambda i: (0, 0))],
        out_shape=[S((t, d), F32), S((t, d), BF16)],
        out_specs=[pl.BlockSpec((tm, d), lambda i: (i, 0)), pl.BlockSpec((tm, d), lambda i: (i, 0))],
        sem=("parallel",), plans=plans)


def _ffn_bwd_ds(dxo, f, ada9, w_out, gu, g_row, res_scale, name, plans=()):
    t, d = dxo.shape
    nb, bw = w_out.shape[0], w_out.shape[1]
    tm = min(WIDE_ROW_TILE, t)

    def body(dxo_ref, f_ref, ada_ref, w_ref, gu_ref, df_ref, da_ref, dg_ref):
        i, j = pl.program_id(0), pl.program_id(1)

        @pl.when((i == 0) & (j == 0))
        def _():
            dg_ref[...] = jnp.zeros_like(dg_ref)

        @pl.when(j == 0)
        def _():
            dxo_t = dxo_ref[...]
            df_ref[...] = ((res_scale * ada_ref[g_row:g_row + 1, :]) * dxo_t).astype(BF16)
            dg_ref[...] += res_scale * _rsum8(dxo_t * f_ref[...].astype(F32))

        ds = _dot_nt(df_ref[...], w_ref[...])
        g = gu_ref[0].astype(F32)
        u = gu_ref[1].astype(F32)
        sg = _sigmoid_tanh(g)
        da_ref[0] = (ds * u * (sg * (1.0 + g * (1.0 - sg)))).astype(BF16)
        da_ref[1] = (ds * (g * sg)).astype(BF16)

    return _pcall(
        body, name, [dxo, f, ada9, w_out, gu], grid=(t // tm, nb),
        in_specs=[pl.BlockSpec((tm, d), lambda i, j: (i, 0)), pl.BlockSpec((tm, d), lambda i, j: (i, 0)),
                  pl.BlockSpec((N_ADA, d), lambda i, j: (0, 0)), pl.BlockSpec((None, bw, d), lambda i, j: (j, 0, 0)),
                  pl.BlockSpec((2, None, tm, bw), lambda i, j: (0, j, i, 0))],
        out_shape=[S((t, d), BF16), S((2, nb, t, bw), BF16), S((8, d), F32)],
        out_specs=[pl.BlockSpec((tm, d), lambda i, j: (i, 0)), pl.BlockSpec((2, None, tm, bw), lambda i, j: (0, j, i, 0)),
                   pl.BlockSpec((8, d), lambda i, j: (0, 0))],
        sem=("arbitrary", "arbitrary"), plans=plans)


def _ffn_bwd_dh(da, w_in, x, dxo, ada9, nrm, sh_row, sc_row, name, plans=()):
    t, d = x.shape
    nb, bw = w_in.shape[0] // 2, w_in.shape[1]
    tm = min(WIDE_ROW_TILE, t)

    def body(da_ref, wg_ref, wu_ref, x_ref, dxo_ref, ada_ref, n_ref, dx_ref, dsh_ref, dsc_ref, dn_ref, acc_ref):
        i, j = pl.program_id(0), pl.program_id(1)

        @pl.when((i == 0) & (j == 0))
        def _():
            dsh_ref[...] = jnp.zeros_like(dsh_ref)
            dsc_ref[...] = jnp.zeros_like(dsc_ref)
            dn_ref[...] = jnp.zeros_like(dn_ref)

        part = _dot(da_ref[0], wg_ref[...]) + _dot(da_ref[1], wu_ref[...])

        @pl.when(j == 0)
        def _():
            acc_ref[...] = part

        @pl.when(j > 0)
        def _():
            acc_ref[...] += part

        @pl.when(j == nb - 1)
        def _():
            for r0 in range(0, tm, min(256, tm)):
                rows = slice(r0, r0 + min(256, tm))
                dx, tsh, tsc, tn = _normmod_bwd(acc_ref[rows, :], x_ref[rows, :], n_ref[...], ada_ref[sc_row:sc_row + 1, :])
                dx_ref[rows, :] = dxo_ref[rows, :] + dx
                dsh_ref[...] += _rsum8(tsh)
                dsc_ref[...] += _rsum8(tsc)
                dn_ref[...] += _rsum8(tn)

    vec = pl.BlockSpec((8, d), lambda i, j: (0, 0))
    return _pcall(
        body, name, [da, w_in, w_in, x, dxo, ada9, nrm], grid=(t // tm, nb),
        in_specs=[pl.BlockSpec((2, None, tm, bw), lambda i, j: (0, j, i, 0)),
                  pl.BlockSpec((None, bw, d), lambda i, j: (j, 0, 0)), pl.BlockSpec((None, bw, d), lambda i, j: (j + nb, 0, 0)),
                  pl.BlockSpec((tm, d), lambda i, j: (i, 0)), pl.BlockSpec((tm, d), lambda i, j: (i, 0)),
                  pl.BlockSpec((N_ADA, d), lambda i, j: (0, 0)), pl.BlockSpec((1, d), lambda i, j: (0, 0))],
        out_shape=[S((t, d), F32), S((8, d), F32), S((8, d), F32), S((8, d), F32)],
        out_specs=[pl.BlockSpec((tm, d), lambda i, j: (i, 0)), vec, vec, vec],
        scratch_shapes=[pltpu.VMEM((tm, d), F32)],
        sem=("arbitrary", "arbitrary"), plans=plans)


def _tn_matmul(a, b, a_block, a_map, b_block, b_map, out_shape, out_block, out_map, nblk, name, plans=()):
    t = a.shape[-2]
    tk = min(K_TILE, t)
    nk = t // tk

    def body(a_ref, b_ref, o_ref, acc_ref):
        k = pl.program_id(1)
        for q in (range(a_ref.shape[0]) if len(a_ref.shape) == 3 else [Ellipsis]):
            part = _dot_tn(a_ref[q], b_ref[...])

            @pl.when(k == 0)
            def _():
                acc_ref[q] = part

            @pl.when(k > 0)
            def _():
                acc_ref[q] += part

        @pl.when(k == nk - 1)
        def _():
            o_ref[...] = acc_ref[...].astype(BF16)

    (out,), moved = _pcall(
        body, name, [a, b], grid=(nblk, nk),
        in_specs=[pl.BlockSpec(a_block(tk), a_map), pl.BlockSpec(b_block(tk), b_map)],
        out_shape=[S(out_shape, BF16)], out_specs=[pl.BlockSpec(out_block, out_map)],
        scratch_shapes=[pltpu.VMEM(tuple(n for n in out_block if n is not None), F32)],
        sem=("parallel", "arbitrary"), plans=plans)
    return out, moved


def _ffn_backward(dxo, x_in, h, gu, s, f, ada9, nrm, w_in, w_out, rows, core, name, ds_plans=(), dwin_plans=()):
    sh_row, sc_row, g_row = rows
    nb, t, bw = s.shape
    d = x_in.shape[1]
    (df, da, dg), ds_moved = _ffn_bwd_ds(dxo, f, ada9, w_out, gu, g_row, 0.5, name + "_ds", plans=ds_plans)
    dw_in, dwin_moved = _tn_matmul(
        da.reshape(2 * nb, t, bw), h, lambda tk: (2, tk, bw), lambda sb, k: (sb, k, 0), lambda tk: (tk, d), lambda sb, k: (k, 0),
        (2 * nb, bw, d), (2, bw, d), lambda sb, k: (sb, 0, 0), nb, name + "_dwin",
        plans=dwin_plans(ds_moved) if callable(dwin_plans) else dwin_plans)
    dw_out, ((half_in,),) = _tn_matmul(
        s, df, lambda tk: (None, tk, bw), lambda sb, k: (sb, k, 0), lambda tk: (tk, d), lambda sb, k: (k, 0),
        (nb, bw, d), (None, bw, d), lambda sb, k: (sb, 0, 0), nb, name + "_dwout", plans=[_plan_sibling_swap([dw_in])])
    dw_out = dw_out.reshape(NDEV, -1, d)
    sum_in = _pair_add(dw_in, half_in, core, name + "_dwin_add")
    (dx, dsh, dsc, dn), ((recv_in,), (half_out,)) = _ffn_bwd_dh(
        da, w_in, x_in, dxo, ada9, nrm, sh_row, sc_row, name + "_dh", plans=[_plan_chip_swap([sum_in]), _plan_sibling_swap([dw_out])])
    sum_out = _pair_add(dw_out, half_out, core, name + "_dwout_add")
    return dx, recv_in, sum_out, (dsh, dsc, dg, dn), ds_moved, dwin_moved


def _mix_in(x, ada9, nrm, w, widths, dts, name, plans=()):
    t, d = x.shape
    n = w.shape[0]
    tm = min(ROW_TILE, t)
    starts = [sum(widths[:i]) for i in range(len(widths))]

    def body(x_ref, ada_ref, n_ref, w_ref, h_ref, *out_refs):
        _, _, h = _normmod(x_ref[...], n_ref[...], ada_ref[4:5, :], ada_ref[3:4, :])
        hb = h.astype(BF16)
        h_ref[...] = hb
        for o_ref, st, wd in zip(out_refs, starts, widths):
            o_ref[...] = _dot_nt(hb, w_ref[st:st + wd, :]).astype(o_ref.dtype)

    return _pcall(
        body, name, [x, ada9, nrm, w], grid=(t // tm,),
        in_specs=[pl.BlockSpec((tm, d), lambda i: (i, 0)), pl.BlockSpec((N_ADA, d), lambda i: (0, 0)),
                  pl.BlockSpec((1, d), lambda i: (0, 0)), pl.BlockSpec((n, d), lambda i: (0, 0))],
        out_shape=[S((t, d), BF16)] + [S((t, wd), dt) for wd, dt in zip(widths, dts)],
        out_specs=[pl.BlockSpec((tm, d), lambda i: (i, 0))] + [pl.BlockSpec((tm, wd), lambda i: (i, 0)) for wd in widths],
        sem=("parallel",), plans=plans)


def _tri(lower):
    r = lax.broadcasted_iota(jnp.int32, (CHUNK, CHUNK), 0)
    c = lax.broadcasted_iota(jnp.int32, (CHUNK, CHUNK), 1)
    return (r >= c) if lower else (c >= r)


def _dot_01(m, x):
    hi = x.astype(BF16)
    r1 = x - hi.astype(F32)
    mid = r1.astype(BF16)
    lo = (r1 - mid.astype(F32)).astype(BF16)
    return _dot(m, hi) + _dot(m, mid) + _dot(m, lo)


def _gla_chunk_terms(q, k, lg, low01):
    b = _dot_01(low01, lg)
    bl = b[CHUNK - 1:CHUNK, :]
    r = 0.5 * bl
    eb, ebl, em, en = jnp.exp(b), jnp.exp(bl - b), jnp.exp(b - r), jnp.exp(r - b)
    return eb, ebl, em, en, jnp.exp(bl), q * eb, k * ebl, q * em, k * en


def _scores(qm_h, knp, qk1_h):
    r = lax.broadcasted_iota(jnp.int32, (CHUNK, CHUNK), 0)
    c = lax.broadcasted_iota(jnp.int32, (CHUNK, CHUNK), 1)
    p = jnp.where(r > c, _dot_nt(qm_h, knp), 0.0)
    return jnp.where(r == c, jnp.sum(qk1_h, axis=1, keepdims=True), p)


def _gla_fwd(qk, v, gl, wg, bg, heads, name, plans=()):
    t = qk.shape[0]
    kw, vw = qk.shape[1] // 2, v.shape[1]
    dk, dv = kw // heads, vw // heads
    assert dk == 64 and dv == 128 and kw % 128 == 0
    gt = min(ROW_TILE, t)
    nc = gt // CHUNK
    scale = dk ** -0.5

    def body(qk_ref, v_ref, gl_ref, wg_ref, bg_ref, o_ref, lg_ref, sall_ref, st_ref):
        @pl.when(pl.program_id(0) == 0)
        def _():
            st_ref[...] = jnp.zeros_like(st_ref)

        gk = _dot(gl_ref[...].astype(BF16), wg_ref[...]) + bg_ref[...]
        lg_ref[...] = (jnp.minimum(gk, 0.0) - jnp.log(1.0 + jnp.exp(-jnp.abs(gk)))) / GATE_NORMALIZER
        low01 = _tri(True).astype(BF16)
        lane = lax.broadcasted_iota(jnp.int32, (CHUNK, 128), 1)

        def chunk(ci, carry):
            rows = pl.ds(pl.multiple_of(ci * CHUNK, CHUNK), CHUNK)
            q = qk_ref[rows, 0:kw] * scale
            k = qk_ref[rows, kw:2 * kw]
            qk1 = q.astype(BF16).astype(F32) * k.astype(BF16).astype(F32)
            eb, ebl, em, en, ebl_row, qe, ke, qm, kn = _gla_chunk_terms(q, k, lg_ref[rows, :], low01)
            for h in range(heads):
                lanes = slice(128 * (h // 2), 128 * (h // 2) + 128)
                own = (lane < 64) if h % 2 == 0 else (lane >= 64)
                knp = kn[:, lanes].astype(BF16)
                qm_h = jnp.where(own, qm[:, lanes], 0.0).astype(BF16)
                qe_h = jnp.where(own, qe[:, lanes], 0.0).astype(BF16)
                ke_h = jnp.where(own, ke[:, lanes], 0.0).astype(BF16)
                v_h = v_ref[rows, h * dv:(h + 1) * dv]
                st = st_ref[h]
                sall_ref[ci, h] = st
                p = _scores(qm_h, knp, jnp.where(own, qk1[:, lanes], 0.0))
                o_ref[rows, h * dv:(h + 1) * dv] = _dot(p.astype(BF16), v_h) + _dot_nt(qe_h, st.astype(BF16))
                st_ref[h] = st * ebl_row[:, lanes] + _dot_tn(v_h, ke_h)
            return carry

        lax.fori_loop(0, nc, chunk, 0, unroll=True)

    return _pcall(
        body, name, [qk, v, gl, wg, bg], grid=(t // gt,),
        in_specs=[pl.BlockSpec((gt, 2 * kw), lambda i: (i, 0)), pl.BlockSpec((gt, vw), lambda i: (i, 0)),
                  pl.BlockSpec((gt, 128), lambda i: (i, 0)), pl.BlockSpec((128, kw), lambda i: (0, 0)),
                  pl.BlockSpec((1, kw), lambda i: (0, 0))],
        out_shape=[S((t, vw), F32), S((t, kw), F32), S((t // CHUNK, heads, dv, 128), F32)],
        out_specs=[pl.BlockSpec((gt, vw), lambda i: (i, 0)), pl.BlockSpec((gt, kw), lambda i: (i, 0)),
                   pl.BlockSpec((nc, heads, dv, 128), lambda i: (i, 0, 0, 0))],
        scratch_shapes=[pltpu.VMEM((heads, dv, 128), F32)],
        sem=("arbitrary",), plans=plans)


def _gla_bwd(qk, v, lg, do, sall, gl, wg, heads, name):
    t = qk.shape[0]
    kw, vw = qk.shape[1] // 2, v.shape[1]
    dk, dv = kw // heads, vw // heads
    gt = min(ROW_TILE, t)
    nc = gt // CHUNK
    nt = t // gt
    scale = dk ** -0.5

    def body(qk_ref, v_ref, lg_ref, do_ref, sall_ref, gl_ref, wg_ref, dqk_ref, dv_ref, dgl_ref, dwg_ref, dbg_ref, dst_ref, dgk_ref):
        @pl.when(pl.program_id(0) == 0)
        def _():
            dst_ref[...] = jnp.zeros_like(dst_ref)
            dwg_ref[...] = jnp.zeros_like(dwg_ref)
            dbg_ref[...] = jnp.zeros_like(dbg_ref)

        low01 = _tri(True).astype(BF16)
        up01 = _tri(False).astype(BF16)
        causal = _tri(True)
        lane = lax.broadcasted_iota(jnp.int32, (CHUNK, 128), 1)
        last_row = lax.broadcasted_iota(jnp.int32, (CHUNK, kw), 0) == CHUNK - 1

        def chunk(cj, carry):
            ci = nc - 1 - cj
            rows = pl.ds(pl.multiple_of(ci * CHUNK, CHUNK), CHUNK)
            q = qk_ref[rows, 0:kw] * scale
            k = qk_ref[rows, kw:2 * kw]
            qk1 = q.astype(BF16).astype(F32) * k.astype(BF16).astype(F32)
            lgc = lg_ref[rows, :]
            eb, ebl, em, en, ebl_row, qe, ke, qm, kn = _gla_chunk_terms(q, k, lgc, low01)
            dqe, dqm, dkn, dke, drow = [], [], [], [], []
            for pr in range(kw // 128):
                lanes = slice(128 * pr, 128 * pr + 128)
                knp = kn[:, lanes].astype(BF16)
                parts = []
                for half in range(2):
                    h = 2 * pr + half
                    own = (lane < 64) if half == 0 else (lane >= 64)
                    qm_h = jnp.where(own, qm[:, lanes], 0.0).astype(BF16)
                    qe_h = jnp.where(own, qe[:, lanes], 0.0).astype(BF16)
                    ke_h = jnp.where(own, ke[:, lanes], 0.0).astype(BF16)
                    v_h = v_ref[rows, h * dv:(h + 1) * dv]
                    do_h = do_ref[rows, h * dv:(h + 1) * dv]
                    st = sall_ref[ci, h]
                    dst = dst_ref[h]
                    stb, dstb = st.astype(BF16), dst.astype(BF16)
                    p = _scores(qm_h, knp, jnp.where(own, qk1[:, lanes], 0.0)).astype(BF16)
                    dp = jnp.where(causal, _dot_nt(do_h, v_h), 0.0).astype(BF16)
                    dv_ref[rows, h * dv:(h + 1) * dv] = (_dot_tn(p, do_h) + _dot_nt(ke_h, dstb)).astype(BF16)
                    parts.append((jnp.where(own, _dot(dp, knp), 0.0), _dot_tn(dp, qm_h), _dot(do_h, stb), _dot(v_h, dstb),
                                  jnp.sum(st * dst, axis=0, keepdims=True)))
                    dst_ref[h] = dst * ebl_row[:, lanes] + _dot_tn(do_h, qe_h)
                dqm.append(parts[0][0] + parts[1][0])
                dkn.append(parts[0][1] + parts[1][1])
                dqe.append(parts[0][2] + parts[1][2])
                dke.append(parts[0][3] + parts[1][3])
                drow.append(parts[0][4] + parts[1][4])
            dqm, dkn, dqe, dke, drow = [jnp.concatenate(a, axis=1) for a in (dqm, dkn, dqe, dke, drow)]
            dqk_ref[rows, 0:kw] = ((dqe * eb + dqm * em) * scale).astype(BF16)
            dqk_ref[rows, kw:2 * kw] = (dke * ebl + dkn * en).astype(BF16)
            tke = dke * ke
            db = dqe * qe + dqm * qm - dkn * kn - tke
            dbl = jnp.sum(tke, axis=0, keepdims=True) + drow * ebl_row
            db = db + jnp.where(last_row, dbl, 0.0)
            dlg = _dot_01(up01, db)
            dgk_ref[rows, :] = dlg * ((1.0 - jnp.exp(GATE_NORMALIZER * lgc)) / GATE_NORMALIZER)
            return carry

        lax.fori_loop(0, nc, chunk, 0, unroll=True)
        dgk = dgk_ref[...]
        dgkb = dgk.astype(BF16)
        dgl_ref[...] = _dot_nt(dgkb, wg_ref[...]).astype(BF16)
        dwg_ref[...] += _dot_tn(gl_ref[...].astype(BF16), dgkb)
        dbg_ref[...] += _rsum8(dgk)

    rev = lambda i: (nt - 1 - i, 0)
    return pl.pallas_call(
        body, name=name, grid=(nt,),
        in_specs=[pl.BlockSpec((gt, 2 * kw), rev), pl.BlockSpec((gt, vw), rev), pl.BlockSpec((gt, kw), rev),
                  pl.BlockSpec((gt, vw), rev), pl.BlockSpec((nc, heads, dv, 128), lambda i: (nt - 1 - i, 0, 0, 0)),
                  pl.BlockSpec((gt, 128), rev), pl.BlockSpec((128, kw), lambda i: (0, 0))],
        out_shape=[S((t, 2 * kw), BF16), S((t, vw), BF16), S((t, 128), BF16), S((128, kw), F32), S((8, kw), F32)],
        out_specs=[pl.BlockSpec((gt, 2 * kw), rev), pl.BlockSpec((gt, vw), rev), pl.BlockSpec((gt, 128), rev),
                   pl.BlockSpec((128, kw), lambda i: (0, 0)), pl.BlockSpec((8, kw), lambda i: (0, 0))],
        scratch_shapes=[pltpu.VMEM((heads, dv, 128), F32), pltpu.VMEM((gt, kw), F32)],
        compiler_params=_cp("arbitrary"),
    )(qk, v, lg, do, sall, gl, wg)


def _conv_taps(cx_ref, halo_ref, first, cw):
    tm = cx_ref.shape[0]
    u = cx_ref[:, cw:2 * cw].astype(F32) * cx_ref[:, 2 * cw:3 * cw].astype(F32)
    uh = halo_ref[:, cw:2 * cw].astype(F32) * halo_ref[:, 2 * cw:3 * cw].astype(F32)
    uh = jnp.where(first, 0.0, uh)
    row = lax.broadcasted_iota(jnp.int32, (tm, cw), 0)
    u1 = jnp.where(row == 0, uh[15:16, :], pltpu.roll(u, 1, 0))
    u2 = jnp.where(row == 0, uh[14:15, :], jnp.where(row == 1, uh[15:16, :], pltpu.roll(u, 2, 0)))
    return u, u1, u2


def _head_norm(o_h, gn):
    rstd = lax.rsqrt(jnp.mean(o_h * o_h, axis=-1, keepdims=True) + EPS)
    ohat = o_h * rstd
    return ohat, rstd, ohat * gn


def _mix_out(cx, o, go, conv_w, gn, w_out, x, ada9, heads, name, plans=()):
    t, d = x.shape
    cw, vw = conv_w.shape[1], o.shape[1]
    dv = vw // heads
    tm = min(ROW_TILE, t)

    def body(cx_ref, halo_ref, o_ref, go_ref, cwt_ref, gn_ref, w_ref, x_ref, ada_ref, xo_ref, m_ref, y_ref):
        u, u1, u2 = _conv_taps(cx_ref, halo_ref, pl.program_id(0) == 0, cw)
        yc = cwt_ref[0:1, :] * u2 + cwt_ref[1:2, :] * u1 + cwt_ref[2:3, :] * u
        y_ref[:, 0:cw] = (cx_ref[:, 0:cw].astype(F32) * yc).astype(BF16)
        for h in range(heads):
            cols = slice(h * dv, (h + 1) * dv)
            _, _, on = _head_norm(o_ref[:, cols], gn_ref[...])
            g = go_ref[:, cols].astype(F32)
            y_ref[:, cw + h * dv:cw + (h + 1) * dv] = (on * (g * _sigmoid(g))).astype(BF16)
        m = _dot(y_ref[...], w_ref[...])
        m_ref[...] = m.astype(BF16)
        xo_ref[...] = x_ref[...] + ada_ref[5:6, :] * m

    return _pcall(
        body, name, [cx, cx, o, go, conv_w, gn, w_out, x, ada9], grid=(t // tm,),
        in_specs=[pl.BlockSpec((tm, 3 * cw), lambda i: (i, 0)),
                  pl.BlockSpec((16, 3 * cw), lambda i: (jnp.maximum(i * (tm // 16) - 1, 0), 0)),
                  pl.BlockSpec((tm, vw), lambda i: (i, 0)), pl.BlockSpec((tm, vw), lambda i: (i, 0)),
                  pl.BlockSpec((3, cw), lambda i: (0, 0)), pl.BlockSpec((1, dv), lambda i: (0, 0)),
                  pl.BlockSpec((cw + vw, d), lambda i: (0, 0)), pl.BlockSpec((tm, d), lambda i: (i, 0)),
                  pl.BlockSpec((N_ADA, d), lambda i: (0, 0))],
        out_shape=[S((t, d), F32), S((t, d), BF16), S((t, cw + vw), BF16)],
        out_specs=[pl.BlockSpec((tm, d), lambda i: (i, 0)), pl.BlockSpec((tm, d), lambda i: (i, 0)),
                   pl.BlockSpec((tm, cw + vw), lambda i: (i, 0))],
        sem=("parallel",), plans=plans)


def _mix_bwd_a(dxo, m, ada9, w_out, cx, o, go, conv_w, gn, heads, name, plans=()):
    t, d = dxo.shape
    cw, vw = conv_w.shape[1], o.shape[1]
    dv = vw // heads
    tm = min(ROW_TILE, t)

    def body(dxo_ref, m_ref, ada_ref, w_ref, cx_ref, halo_ref, o_ref, go_ref, cwt_ref, gn_ref,
             dm_ref, dyc_ref, dcb_ref, do_ref, dgo_ref, dg_ref, dcw_ref, dgn_ref):
        @pl.when(pl.program_id(0) == 0)
        def _():
            dg_ref[...] = jnp.zeros_like(dg_ref)
            dcw_ref[...] = jnp.zeros_like(dcw_ref)
            dgn_ref[...] = jnp.zeros_like(dgn_ref)

        dxo_t = dxo_ref[...]
        dmb = (ada_ref[5:6, :] * dxo_t).astype(BF16)
        dm_ref[...] = dmb
        dg_ref[...] += _rsum8(dxo_t * m_ref[...].astype(F32))
        dy = _dot_nt(dmb, w_ref[...])
        u, u1, u2 = _conv_taps(cx_ref, halo_ref, pl.program_id(0) == 0, cw)
        yc = cwt_ref[0:1, :] * u2 + cwt_ref[1:2, :] * u1 + cwt_ref[2:3, :] * u
        dyv = dy[:, 0:cw]
        dcb_ref[...] = (dyv * yc).astype(BF16)
        dyc = dyv * cx_ref[:, 0:cw].astype(F32)
        dyc_ref[...] = dyc
        dcw_ref[0] += _rsum8(dyc * u2)
        dcw_ref[1] += _rsum8(dyc * u1)
        dcw_ref[2] += _rsum8(dyc * u)
        for h in range(heads):
            cols = slice(h * dv, (h + 1) * dv)
            ohat, rstd, on = _head_norm(o_ref[:, cols], gn_ref[...])
            g = go_ref[:, cols].astype(F32)
            sg = _sigmoid_tanh(g)
            dyg = dy[:, cw + h * dv:cw + (h + 1) * dv]
            dgo_ref[:, cols] = (dyg * on * (sg * (1.0 + g * (1.0 - sg)))).astype(BF16)
            don = dyg * (g * sg)
            dgn_ref[...] += _rsum8(don * ohat)
            tt = don * gn_ref[...]
            do_ref[:, cols] = (rstd * (tt - ohat * jnp.mean(tt * ohat, axis=-1, keepdims=True))).astype(BF16)

    return _pcall(
        body, name, [dxo, m, ada9, w_out, cx, cx, o, go, conv_w, gn], grid=(t // tm,),
        in_specs=[pl.BlockSpec((tm, d), lambda i: (i, 0)), pl.BlockSpec((tm, d), lambda i: (i, 0)),
                  pl.BlockSpec((N_ADA, d), lambda i: (0, 0)), pl.BlockSpec((cw + vw, d), lambda i: (0, 0)),
                  pl.BlockSpec((tm, 3 * cw), lambda i: (i, 0)),
                  pl.BlockSpec((16, 3 * cw), lambda i: (jnp.maximum(i * (tm // 16) - 1, 0), 0)),
                  pl.BlockSpec((tm, vw), lambda i: (i, 0)), pl.BlockSpec((tm, vw), lambda i: (i, 0)),
                  pl.BlockSpec((3, cw), lambda i: (0, 0)), pl.BlockSpec((1, dv), lambda i: (0, 0))],
        out_shape=[S((t, d), BF16), S((t, cw), F32), S((t, cw), BF16), S((t, vw), BF16), S((t, vw), BF16),
                   S((8, d), F32), S((3, 8, cw), F32), S((8, dv), F32)],
        out_specs=[pl.BlockSpec((tm, d), lambda i: (i, 0)), pl.BlockSpec((tm, cw), lambda i: (i, 0)),
                   pl.BlockSpec((tm, cw), lambda i: (i, 0)), pl.BlockSpec((tm, vw), lambda i: (i, 0)),
                   pl.BlockSpec((tm, vw), lambda i: (i, 0)), pl.BlockSpec((8, d), lambda i: (0, 0)),
                   pl.BlockSpec((3, 8, cw), lambda i: (0, 0, 0)), pl.BlockSpec((8, dv), lambda i: (0, 0))],
        sem=("arbitrary",), plans=plans)


def _mix_bwd_b(dyc, cx, dcb, dqk, dvv, dgo, dgl, conv_w, w, x, dxo, ada9, nrm, name):
    t, d = x.shape
    cw = conv_w.shape[1]
    n = w.shape[0]
    tm = min(ROW_TILE, t)
    nt = t // tm
    pieces = [dcb.shape[1], cw, cw, dqk.shape[1], dvv.shape[1], dgo.shape[1], dgl.shape[1]]
    assert sum(pieces) == n

    def body(dyc_ref, nxt_ref, cx_ref, dcb_ref, dqk_ref, dv_ref, dgo_ref, dgl_ref, cwt_ref, w_ref, x_ref, dxo_ref, ada_ref, n_ref,
             dx_ref, dp_ref, dsh_ref, dsc_ref, dn_ref):
        i = pl.program_id(0)

        @pl.when(i == 0)
        def _():
            dsh_ref[...] = jnp.zeros_like(dsh_ref)
            dsc_ref[...] = jnp.zeros_like(dsc_ref)
            dn_ref[...] = jnp.zeros_like(dn_ref)

        dyc_t = dyc_ref[...]
        nxt = jnp.where(i == nt - 1, 0.0, nxt_ref[...])
        row = lax.broadcasted_iota(jnp.int32, (tm, cw), 0)
        d1 = jnp.where(row == tm - 1, nxt[0:1, :], pltpu.roll(dyc_t, tm - 1, 0))
        d2 = jnp.where(row == tm - 2, nxt[0:1, :], jnp.where(row == tm - 1, nxt[1:2, :], pltpu.roll(dyc_t, tm - 2, 0)))
        du = cwt_ref[2:3, :] * dyc_t + cwt_ref[1:2, :] * d1 + cwt_ref[0:1, :] * d2
        c0 = 0
        dp_ref[:, c0:c0 + cw] = dcb_ref[...]
        dp_ref[:, cw:2 * cw] = (du * cx_ref[:, 2 * cw:3 * cw].astype(F32)).astype(BF16)
        dp_ref[:, 2 * cw:3 * cw] = (du * cx_ref[:, cw:2 * cw].astype(F32)).astype(BF16)
        c0 = 3 * cw
        for ref in (dqk_ref, dv_ref, dgo_ref, dgl_ref):
            wd = ref.shape[1]
            dp_ref[:, c0:c0 + wd] = ref[...]
            c0 += wd
        dh = _dot(dp_ref[...], w_ref[...])
        dx, tsh, tsc, tn = _normmod_bwd(dh, x_ref[...], n_ref[...], ada_ref[4:5, :])
        dx_ref[...] = dxo_ref[...] + dx
        dsh_ref[...] += _rsum8(tsh)
        dsc_ref[...] += _rsum8(tsc)
        dn_ref[...] += _rsum8(tn)

    row_spec = lambda wd: pl.BlockSpec((tm, wd), lambda i: (i, 0))
    vec = pl.BlockSpec((8, d), lambda i: (0, 0))
    return pl.pallas_call(
        body, name=name, grid=(nt,),
        in_specs=[row_spec(cw), pl.BlockSpec((8, cw), lambda i: (jnp.minimum((i + 1) * (tm // 8), t // 8 - 1), 0)),
                  row_spec(3 * cw), row_spec(cw), row_spec(dqk.shape[1]), row_spec(dvv.shape[1]), row_spec(dgo.shape[1]),
                  row_spec(dgl.shape[1]), pl.BlockSpec((3, cw), lambda i: (0, 0)), pl.BlockSpec((n, d), lambda i: (0, 0)),
                  row_spec(d), row_spec(d), pl.BlockSpec((N_ADA, d), lambda i: (0, 0)), pl.BlockSpec((1, d), lambda i: (0, 0))],
        out_shape=[S((t, d), F32), S((t, n), BF16), S((8, d), F32), S((8, d), F32), S((8, d), F32)],
        out_specs=[row_spec(d), row_spec(n), vec, vec, vec],
        compiler_params=_cp("arbitrary"),
    )(dyc, dyc, cx, dcb, dqk, dvv, dgo, dgl, conv_w, w, x, dxo, ada9, nrm)


def _ffn_out_loss(s, w_out, x, ada9, g_row, res_scale, target, nrm, name):
    nb, t, bw = s.shape
    d = x.shape[1]
    tm = min(ROW_TILE, t)
    nt = t // tm

    def body(s_ref, w_ref, x_ref, ada_ref, tg_ref, n_ref, f_ref, loss_ref, dx_ref, dn_ref, acc_ref):
        i = pl.program_id(0)

        @pl.when(i == 0)
        def _():
            acc_ref[...] = jnp.zeros_like(acc_ref)
            dn_ref[...] = jnp.zeros_like(dn_ref)

        f = _dot(s_ref[0], w_ref[0])
        for b in range(1, nb):
            f = f + _dot(s_ref[b], w_ref[b])
        f_ref[...] = f.astype(BF16)
        xt = x_ref[...] + (res_scale * ada_ref[g_row:g_row + 1, :]) * f
        rstd = lax.rsqrt(jnp.mean(xt * xt, axis=-1, keepdims=True) + EPS)
        xhat = xt * rstd
        err = xhat * n_ref[...] - tg_ref[...]
        acc_ref[...] += _rsum8(err * err)
        dy = err * (1.0 / d)
        dn_ref[...] += _rsum8(dy * xhat)
        dxhat = dy * n_ref[...]
        dx_ref[...] = rstd * (dxhat - xhat * jnp.mean(dxhat * xhat, axis=-1, keepdims=True))

        @pl.when(i == nt - 1)
        def _():
            loss_ref[...] = jnp.full(loss_ref.shape, (0.5 / d) * jnp.sum(acc_ref[...]), F32)

    return pl.pallas_call(
        body, name=name, grid=(nt,),
        in_specs=[pl.BlockSpec((nb, tm, bw), lambda i: (0, i, 0)), pl.BlockSpec((nb, bw, d), lambda i: (0, 0, 0)),
                  pl.BlockSpec((tm, d), lambda i: (i, 0)), pl.BlockSpec((N_ADA, d), lambda i: (0, 0)),
                  pl.BlockSpec((tm, d), lambda i: (i, 0)), pl.BlockSpec((1, d), lambda i: (0, 0))],
        out_shape=[S((t, d), BF16), S((1, 128), F32), S((t, d), F32), S((8, d), F32)],
        out_specs=[pl.BlockSpec((tm, d), lambda i: (i, 0)), pl.BlockSpec((1, 128), lambda i: (0, 0)),
                   pl.BlockSpec((tm, d), lambda i: (i, 0)), pl.BlockSpec((8, d), lambda i: (0, 0))],
        scratch_shapes=[pltpu.VMEM((8, d), F32)],
        compiler_params=_cp("arbitrary"),
    )(s, w_out, x, ada9, target, nrm)


def _pack_smalls(vec_parts, dcw, dbg, dgn, dwg, loss_v, rank, name):
    d = vec_parts[0].shape[1]
    cw, kw, dv = dcw.shape[2], dbg.shape[1], dgn.shape[1]
    nv = len(vec_parts)
    loss_row = nv + 2 + rank * kw // d
    assert 2 * cw == d and cw + kw + dv <= d and (rank * kw) % d == 0 and loss_row < PACK_ROWS
    per_row = d // kw

    def body(*refs):
        vrefs, (dcw_ref, dbg_ref, dgn_ref, dwg_ref, loss_ref, o_ref) = refs[:nv], refs[nv:]
        o_ref[...] = jnp.zeros_like(o_ref)
        o_ref[loss_row:loss_row + 1, 0:loss_ref.shape[1]] = loss_ref[...]
        for r, ref in enumerate(vrefs):
            o_ref[r:r + 1, :] = jnp.sum(ref[...], axis=0, keepdims=True)
        o_ref[nv:nv + 1, 0:cw] = jnp.sum(dcw_ref[0], axis=0, keepdims=True)
        o_ref[nv:nv + 1, cw:2 * cw] = jnp.sum(dcw_ref[1], axis=0, keepdims=True)
        o_ref[nv + 1:nv + 2, 0:cw] = jnp.sum(dcw_ref[2], axis=0, keepdims=True)
        o_ref[nv + 1:nv + 2, cw:cw + kw] = jnp.sum(dbg_ref[...], axis=0, keepdims=True)
        o_ref[nv + 1:nv + 2, cw + kw:cw + kw + dv] = jnp.sum(dgn_ref[...], axis=0, keepdims=True)
        for r in range(rank):
            o_ref[nv + 2 + r // per_row:nv + 3 + r // per_row, (r % per_row) * kw:(r % per_row + 1) * kw] = dwg_ref[r:r + 1, :]

    return pl.pallas_call(body, name=name, out_shape=S((PACK_ROWS, d), F32), compiler_params=_cp())(*vec_parts, dcw, dbg, dgn, dwg, loss_v)


def _sum_slots(a, name):
    def body(a_ref, o_ref):
        acc = a_ref[0]
        for s in range(1, NDEV):
            acc = acc + a_ref[s]
        o_ref[...] = acc

    return pl.pallas_call(body, name=name, out_shape=S(a.shape[1:], F32), compiler_params=_cp())(a)


def _adamw(w, g, m, v):
    m = ADAM_B1 * m + (1.0 - ADAM_B1) * g
    v = ADAM_B2 * v + (1.0 - ADAM_B2) * (g * g)
    m_hat = m / (1.0 - ADAM_B1 ** ADAM_STEP)
    v_hat = v / (1.0 - ADAM_B2 ** ADAM_STEP)
    return -ADAM_LR * (m_hat / (jnp.sqrt(v_hat) + ADAM_EPS) + ADAM_WD * w), m, v


def _adam_slots(recv, w, m, v, name):
    r, c = w.shape
    slots = recv.shape[0]
    tr = _row_tile(r, c)

    def body(recv_ref, w_ref, m_ref, v_ref, g_ref, d_ref, mo_ref, vo_ref):
        g = recv_ref[0].astype(F32)
        for s in range(1, slots):
            g = g + recv_ref[s].astype(F32)
        g_ref[...] = g
        d_ref[...], mo_ref[...], vo_ref[...] = _adamw(w_ref[...], g, m_ref[...], v_ref[...])

    blk = pl.BlockSpec((tr, c), lambda i: (i, 0))
    return pl.pallas_call(
        body, name=name, grid=(r // tr,),
        in_specs=[pl.BlockSpec((slots, tr, c), lambda i: (0, i, 0)), blk, blk, blk],
        out_shape=[S((r, c), F32)] * 4, out_specs=[blk] * 4, compiler_params=_cp("parallel"),
    )(recv, w, m, v)


def _adam_w_ada(act_t, dada, w, m, v, name):
    r, c = w.shape
    tr = 128
    nb = act_t.shape[1]

    def body(a_ref, da_ref, w_ref, m_ref, v_ref, g_ref, d_ref, mo_ref, vo_ref):
        g = a_ref[:, 0:1] * da_ref[0:1, :]
        for b in range(1, nb):
            g = g + a_ref[:, b:b + 1] * da_ref[b:b + 1, :]
        g_ref[...] = g
        d_ref[...], mo_ref[...], vo_ref[...] = _adamw(w_ref[...], g, m_ref[...], v_ref[...])

    blk = pl.BlockSpec((tr, c), lambda i: (i, 0))
    return pl.pallas_call(
        body, name=name, grid=(r // tr,),
        in_specs=[pl.BlockSpec((tr, nb), lambda i: (i, 0)), pl.BlockSpec((nb, c), lambda i: (0, 0)), blk, blk, blk],
        out_shape=[S((r, c), F32)] * 4, out_specs=[blk] * 4, compiler_params=_cp("parallel"),
    )(act_t, dada, w, m, v)


def _adam_smalls(ws, gs, ms, vs, name):
    n = len(ws)

    def body(*refs):
        w_r, g_r, m_r, v_r = (refs[k * n:(k + 1) * n] for k in range(4))
        d_o, m_o, v_o = (refs[(4 + k) * n:(5 + k) * n] for k in range(3))
        for i in range(n):
            d_o[i][...], m_o[i][...], v_o[i][...] = _adamw(w_r[i][...], g_r[i][...], m_r[i][...], v_r[i][...])

    shapes = [S(w.shape, F32) for w in ws]
    outs = pl.pallas_call(body, name=name, out_shape=shapes * 3, compiler_params=_cp())(*ws, *gs, *ms, *vs)
    return outs[:n], outs[n:2 * n], outs[2 * n:]


def kernel(x, c, w_ada, b_ada, norm_ffn1, w_ffn1_in, w_ffn1_out, norm_mix, w_mix_in, conv_w, w_gk2, b_gk, gla_norm, w_mix_out, norm_ffn2, w_ffn2_in, w_ffn2_out, norm_final, loss_target, m_w_ada, m_b_ada, m_norm_ffn1, m_w_ffn1_in, m_w_ffn1_out, m_norm_mix, m_w_mix_in, m_conv_w, m_w_gk2, m_b_gk, m_gla_norm, m_w_mix_out, m_norm_ffn2, m_w_ffn2_in, m_w_ffn2_out, m_norm_final, v_w_ada, v_b_ada, v_norm_ffn1, v_w_ffn1_in, v_w_ffn1_out, v_norm_mix, v_w_mix_in, v_conv_w, v_w_gk2, v_b_gk, v_gla_norm, v_w_mix_out, v_norm_ffn2, v_w_ffn2_in, v_w_ffn2_out, v_norm_final):
    t, d = x.shape[1], x.shape[2]
    x0, tgt = x[0], loss_target[0]
    rank, kw = w_gk2.shape[1], w_gk2.shape[2] * NDEV
    cw = conv_w.shape[2] * NDEV
    dv = gla_norm.shape[1]
    vw = d - cw
    heads = vw // dv
    mix_cols = w_mix_in.shape[2]
    widths = [3 * cw, 2 * kw, vw, vw, 128]
    n_proj = 3 * cw + 2 * kw + 2 * vw + rank
    assert n_proj == mix_cols * NDEV and rank <= 128
    me = 4 * lax.axis_index("x") + 2 * lax.axis_index("y") + lax.axis_index("c")

    core = lax.axis_index("c").astype(jnp.int32).reshape(1)
    bf = lambda a: a[0].astype(BF16)
    bft = lambda a: a[0].T.astype(BF16)
    nb = NDEV // 2

    (ada_row, act_all), ((w1i, cwt_all, wg_all),) = _ada_rows(
        c, w_ada[0], b_ada, "ada_rows", plans=[_plan_gather([bft(w_ffn1_in), conv_w[0], w_gk2[0]])])
    ada9 = ada_row.reshape(N_ADA, d)
    cwt = cwt_all.transpose(1, 0, 2).reshape(conv_w.shape[1], cw)
    wg = jnp.pad(wg_all.transpose(1, 0, 2).reshape(rank, kw), ((0, 128 - rank), (0, 0))).astype(BF16)

    (h1, gu1, s1), ((w1o, wmi),) = _ffn_in(x0, ada9, norm_ffn1, w1i, 0, 1, "ffn1_in", plans=[_plan_gather([bf(w_ffn1_out), bft(w_mix_in)])])
    w1o = w1o.reshape(nb, -1, d)
    wmi = jnp.pad(wmi.reshape(n_proj, d), ((0, sum(widths) - n_proj), (0, 0)))
    w2i_mine = bft(w_ffn2_in)
    quarter = w2i_mine.shape[0] // 4
    part = lambda k, into=None: _plan_gather([w2i_mine], rows=(k * quarter, quarter), into=into)
    (x1, f1), ((wmo,), (w2i,)) = _ffn_out(s1, w1o, x0, ada9, 2, 0.5, "ffn1_out", plans=[_plan_gather([bf(w_mix_out)]), part(0)])
    wmo = wmo.reshape(cw + vw, d)
    (h2, cx, qk, vv, go, gl), ((w2i,),) = _mix_in(x1, ada9, norm_mix, wmi, widths, [BF16, F32, BF16, BF16, F32], "mix_in",
                                                 plans=[part(1, [w2i])])
    (o, lg, sall), ((w2i,),) = _gla_fwd(qk, vv, gl, wg, b_gk, heads, "gla_fwd", plans=[part(2, [w2i])])
    (x2, mm, ycat), ((w2i,),) = _mix_out(cx, o, go, cwt, gla_norm, wmo, x1, ada9, heads, "mix_out", plans=[part(3, [w2i])])
    (h3, gu3, s3), ((w2o,),) = _ffn_in(x2, ada9, norm_ffn2, w2i, 6, 7, "ffn2_in", plans=[_plan_gather([bf(w_ffn2_out)])])
    w2o = w2o.reshape(nb, -1, d)
    f3, loss_v, dx3, dnf = _ffn_out_loss(s3, w2o, x2, ada9, 8, 0.5, tgt, norm_final.reshape(1, d), "ffn2_out_loss")

    dx2, r2i, sum2o, (dsh3, dsc3, dg3, dn3), _, _ = _ffn_backward(
        dx3, x2, h3, gu3, s3, f3, ada9, norm_ffn2, w2i, w2o, (6, 7, 8), core, "ffn2_bwd")
    (dm, dyc, dcb, do, dgo, dg2, dcw, dgn), ((r2o,),) = _mix_bwd_a(dx2, mm, ada9, wmo, cx, o, go, cwt, gla_norm, heads, "mix_bwd_a",
                                                                 plans=[_plan_chip_swap([sum2o])])
    dqk, dvv, dgl, dwg, dbg = _gla_bwd(qk, vv, lg, do, sall, gl, wg, heads, "gla_bwd")
    dx1, dproj, dsh2, dsc2, dnm = _mix_bwd_b(dyc, cx, dcb, dqk, dvv, dgo, dgl, cwt, wmi, x1, dx2, ada9, norm_mix, "mix_bwd_b")
    n_pad = sum(widths)
    tn = n_pad // 5
    dwmi, _ = _tn_matmul(dproj, h2, lambda tk: (tk, tn), lambda sb, k: (k, sb), lambda tk: (tk, d), lambda sb, k: (k, 0),
                         (n_pad, d), (tn, d), lambda sb, k: (sb, 0), 5, "mix_dwin")
    dwmo, _ = _tn_matmul(ycat, dm, lambda tk: (tk, cw + vw), lambda sb, k: (k, 0), lambda tk: (tk, d), lambda sb, k: (k, 0),
                         (cw + vw, d), (cw + vw, d), lambda sb, k: (0, 0), 1, "mix_dwout")
    dwmi = dwmi[:n_proj].reshape(NDEV, mix_cols, d)
    dwmo = dwmo.reshape(NDEV, -1, d)
    dx0, r1i, sum1o, (dsh1, dsc1, dg1, dn1), _, ((rmi, rmo),) = _ffn_backward(
        dx1, x0, h1, gu1, s1, f1, ada9, norm_ffn1, w1i, w1o, (0, 1, 2), core, "ffn1_bwd",
        ds_plans=[_plan_sibling_swap([dwmi, dwmo])],
        dwin_plans=lambda moved: [_plan_chip_swap([_pair_add(dwmi, moved[0][0], core, "mix_dwin_add"),
                                                   _pair_add(dwmo, moved[0][1], core, "mix_dwout_add")])])
    pack = _pack_smalls([dn1, dnm, dn3, dnf, dsh1, dsc1, dg1, dsh2, dsc2, dg2, dsh3, dsc3, dg3], dcw, dbg, dgn, dwg, loss_v, rank, "pack_smalls")
    (r1o,), (pack_all,) = _exchange([_plan_chip_swap([sum1o]), _plan_all_to_all([pack], True)], "grads_last")
    tot = _sum_slots(pack_all, "sum_smalls")

    res = {}
    for nm, recv, w, m, v in (("w_ffn1_out", r1o, w_ffn1_out, m_w_ffn1_out, v_w_ffn1_out), ("w_mix_out", rmo, w_mix_out, m_w_mix_out, v_w_mix_out),
                              ("w_ffn2_out", r2o, w_ffn2_out, m_w_ffn2_out, v_w_ffn2_out)):
        res[nm] = [a[None] for a in _adam_slots(recv, w[0], m[0], v[0], "adam_" + nm)]
    for nm, recv, w, m, v in (("w_ffn1_in", r1i, w_ffn1_in, m_w_ffn1_in, v_w_ffn1_in), ("w_mix_in", rmi, w_mix_in, m_w_mix_in, v_w_mix_in),
                              ("w_ffn2_in", r2i, w_ffn2_in, m_w_ffn2_in, v_w_ffn2_in)):
        res[nm] = [a.T[None] for a in _adam_slots(recv, w[0].T, m[0].T, v[0].T, "adam_" + nm)]

    cols_ada = w_ada.shape[2]
    dada_all = pack_all[:, 4:4 + N_ADA, :].reshape(NDEV, N_ADA * d)
    dada_mine = lax.dynamic_slice_in_dim(dada_all, me * cols_ada, cols_ada, axis=1)
    res["w_ada"] = [a[None] for a in _adam_w_ada(act_all.T, dada_mine, w_ada[0], m_w_ada[0], v_w_ada[0], "adam_w_ada")]

    nv = 4 + N_ADA
    g_small = {
        "b_ada": tot[4:nv].reshape(1, N_ADA * d),
        "norm_ffn1": tot[0:1], "norm_mix": tot[1:2], "norm_ffn2": tot[2:3], "norm_final": tot[3:4],
        "conv_w": lax.dynamic_slice_in_dim(
            jnp.concatenate([tot[nv:nv + 1, 0:cw], tot[nv:nv + 1, cw:2 * cw], tot[nv + 1:nv + 2, 0:cw]], axis=0), me * (cw // NDEV), cw // NDEV, axis=1),
        "w_gk2": lax.dynamic_slice_in_dim(tot[nv + 2:nv + 2 + rank * kw // d].reshape(rank, kw), me * (kw // NDEV), kw // NDEV, axis=1),
        "b_gk": tot[nv + 1:nv + 2, cw:cw + kw],
        "gla_norm": tot[nv + 1:nv + 2, cw + kw:cw + kw + dv],
    }
    small = {"b_ada": (b_ada, m_b_ada, v_b_ada), "norm_ffn1": (norm_ffn1, m_norm_ffn1, v_norm_ffn1), "norm_mix": (norm_mix, m_norm_mix, v_norm_mix),
             "norm_ffn2": (norm_ffn2, m_norm_ffn2, v_norm_ffn2), "norm_final": (norm_final, m_norm_final, v_norm_final),
             "conv_w": (conv_w, m_conv_w, v_conv_w), "w_gk2": (w_gk2, m_w_gk2, v_w_gk2), "b_gk": (b_gk, m_b_gk, v_b_gk),
             "gla_norm": (gla_norm, m_gla_norm, v_gla_norm)}
    names = list(small)
    flat = lambda a: a.reshape(-1, a.shape[-1])
    dl, mo, vo = _adam_smalls([flat(small[n][0]) for n in names], [g_small[n] for n in names],
                              [flat(small[n][1]) for n in names], [flat(small[n][2]) for n in names], "adam_smalls")
    for i, n in enumerate(names):
        shp = small[n][0].shape
        res[n] = [g_small[n].reshape(shp), dl[i].reshape(shp), mo[i].reshape(shp), vo[i].reshape(shp)]

    loss = tot[nv + 2 + rank * kw // d, 0]
    order = ["w_ada", "b_ada", "norm_ffn1", "w_ffn1_in", "w_ffn1_out", "norm_mix", "w_mix_in", "conv_w", "w_gk2", "b_gk", "gla_norm",
             "w_mix_out", "norm_ffn2", "w_ffn2_in", "w_ffn2_out", "norm_final"]
    return (loss, dx0[None], *[res[n][0] for n in order], *[res[n][1] for n in order], *[res[n][2] for n in order], *[res[n][3] for n in order])
```

```python
import collections
import functools

import jax
import jax.numpy as jnp
from jax import lax
from jax.experimental import pallas as pl
from jax.experimental.pallas import tpu as pltpu

F32 = jnp.float32
BF16 = jnp.bfloat16
S = jax.ShapeDtypeStruct

NDEV = 8
EPS = 1e-6
GATE_NORMALIZER = 16.0
CHUNK = 128
N_ADA = 9
ADAM_LR, ADAM_B1, ADAM_B2, ADAM_EPS, ADAM_WD, ADAM_STEP = 0.001, 0.9, 0.999, 1e-08, 0.01, 10
V7X_VMEM_LIMIT = 56 * 1024 * 1024
ROW_TILE = 512
WIDE_ROW_TILE = 1024
K_TILE = 1024
EPILOGUE_ROWS = 256
LANES = 128
BF16_ROWS = 16
PACK_ROWS = 24
ANY = pl.BlockSpec(memory_space=pl.ANY)


def _cp(*sem):
    return pltpu.CompilerParams(dimension_semantics=sem or None, vmem_limit_bytes=V7X_VMEM_LIMIT)


def _dot(a, b):
    return jnp.dot(a, b, preferred_element_type=F32)


def _dot_nt(a, b):
    return lax.dot_general(a, b, (((1,), (1,)), ((), ())), preferred_element_type=F32)


def _dot_tn(a, b):
    return lax.dot_general(a, b, (((0,), (0,)), ((), ())), preferred_element_type=F32)


def _rsum8(a):
    r, c = a.shape
    return jnp.sum(a.reshape(r // 8, 8, c), axis=0)


def _row_tile(r, c):
    for cand in (256, 128, 176, 88, 64, 32, 16, 8):
        if r % cand == 0 and cand * c * 4 <= 1024 * 1024:
            return cand
    return r


def _sigmoid(x):
    return 1.0 / (1.0 + jnp.exp(-x))


def _sigmoid_tanh(x):
    return 0.5 * jnp.tanh(0.5 * x) + 0.5


def _normmod(x, nrm, sc, sh):
    rstd = lax.rsqrt(jnp.mean(x * x, axis=-1, keepdims=True) + EPS)
    xhat = x * rstd
    return xhat, rstd, (xhat * nrm) * (1.0 + sc) + sh


def _normmod_bwd(dh, x, nrm, sc):
    rstd = lax.rsqrt(jnp.mean(x * x, axis=-1, keepdims=True) + EPS)
    xhat = x * rstd
    dxhat = dh * (nrm * (1.0 + sc))
    dx = rstd * (dxhat - xhat * jnp.mean(dxhat * xhat, axis=-1, keepdims=True))
    return dx, dh, dh * (xhat * nrm), dh * ((1.0 + sc) * xhat)


def _place():
    x, y, c = lax.axis_index("x"), lax.axis_index("y"), lax.axis_index("c")
    return x, y, c, 4 * x + 2 * y + c


def _peer(x, y, c, k):
    px = 1 - x if k & 4 else x
    py = 1 - y if k & 2 else y
    pc = 1 - c if k & 1 else c
    return (px, py, pc), 4 * px + 2 * py + pc


def _remote(src, dst, send_sem, recv_sem, peer):
    return pltpu.make_async_remote_copy(src_ref=src, dst_ref=dst, send_sem=send_sem, recv_sem=recv_sem,
                                        device_id=peer, device_id_type=pl.DeviceIdType.MESH)


_Plan = collections.namedtuple("_Plan", "inputs out_shapes sem_shapes start finish aliases", defaults=({},))


def _plan_all_to_all(xs, gather):
    n = len(xs)

    def copies(ins, outs, sems, landed):
        send_sems, recv_sems, local_sems = sems
        x, y, c, me = _place()
        local = [pltpu.make_async_copy(ins[i] if gather else ins[i].at[me], outs[i].at[me], local_sems.at[i]) for i in range(n)]
        remote = []
        for k in range(1, NDEV):
            peer, pid = _peer(x, y, c, k)
            for i in range(n):
                remote.append(_remote(ins[i] if gather else ins[i].at[pid], outs[i].at[pid if landed else me],
                                      send_sems.at[i, k - 1], recv_sems.at[i, k - 1], peer))
        return local, remote

    def start(ins, outs, sems):
        local, remote = copies(ins, outs, sems, False)
        for cp in local + remote:
            cp.start()

    def finish(ins, outs, sems):
        local, remote = copies(ins, outs, sems, True)
        for cp in remote + local:
            cp.wait()

    return _Plan(list(xs), [S((NDEV,) + a.shape, a.dtype) if gather else S(a.shape, a.dtype) for a in xs],
                 [pltpu.SemaphoreType.DMA((n, NDEV - 1)), pltpu.SemaphoreType.DMA((n, NDEV - 1)), pltpu.SemaphoreType.DMA((n,))],
                 start, finish)


def _other_chips(x, y):
    return [(1 - x, y), (x, 1 - y), (1 - x, 1 - y)]


def _plan_gather(xs, rows=None, into=None):
    n = len(xs)

    def copies(ins, outs, sems, rest):
        send_sems, recv_sems, local_sems = sems
        x, y, c, me = _place()
        sib, sib_id = (x, y, 1 - c), 4 * x + 2 * y + 1 - c
        chips = _other_chips(x, y)
        mine = lambda i: ins[i] if rows is None else ins[i].at[pl.ds(*rows)]
        slot_of = lambda i, s: outs[i].at[s] if rows is None else outs[i].at[s, pl.ds(*rows)]
        local = [pltpu.make_async_copy(mine(i), slot_of(i, me), local_sems.at[i]) for i in range(n)]
        first = [_remote(mine(i), slot_of(i, me), send_sems.at[i, 0], recv_sems.at[i, 0], sib) for i in range(n)]
        first += [_remote(mine(i), slot_of(i, me), send_sems.at[i, 1 + j], recv_sems.at[i, 1 + j], (px, py, c))
                  for j, (px, py) in enumerate(chips) for i in range(n)]
        if not rest:
            return local, first
        from_sibling = [_remote(mine(i), slot_of(i, sib_id), send_sems.at[i, 0], recv_sems.at[i, 0], sib) for i in range(n)]
        arrive, forward = [], []
        for j, (px, py) in enumerate(chips):
            s = 4 * px + 2 * py
            arrive.append([_remote(mine(i), slot_of(i, s + c), send_sems.at[i, 1 + j], recv_sems.at[i, 1 + j], (px, py, c)) for i in range(n)])
            forward.append([_remote(slot_of(i, s + c), slot_of(i, s + c), send_sems.at[i, 4 + j], recv_sems.at[i, 4 + j], sib) for i in range(n)])
            from_sibling += [_remote(mine(i), slot_of(i, s + 1 - c), send_sems.at[i, 4 + j], recv_sems.at[i, 4 + j], sib) for i in range(n)]
        return local, first, arrive, forward, from_sibling

    def start(ins, outs, sems):
        local, first = copies(ins, outs, sems, False)
        for cp in local + first:
            cp.start()

    def finish(ins, outs, sems):
        local, first, arrive, forward, from_sibling = copies(ins, outs, sems, True)
        for landed, onward in zip(arrive, forward):
            for cp in landed:
                cp.wait_recv()
            for cp in onward:
                cp.start()
        for cp in from_sibling:
            cp.wait_recv()
        for cp in first + [cp for onward in forward for cp in onward]:
            cp.wait_send()
        for cp in local:
            cp.wait()

    return _Plan(list(xs) + list(into or []), [S((NDEV,) + a.shape, a.dtype) for a in xs],
                 [pltpu.SemaphoreType.DMA((n, NDEV - 1)), pltpu.SemaphoreType.DMA((n, NDEV - 1)), pltpu.SemaphoreType.DMA((n,))],
                 start, finish, {n + i: i for i in range(len(into or []))})


def _plan_sibling_swap(gs):
    n = len(gs)

    def copies(ins, outs, sems):
        send_sems, recv_sems = sems
        x, y, c, _ = _place()
        return [_remote(ins[i].at[2 * j + 1 - c], outs[i].at[j], send_sems.at[i, j], recv_sems.at[i, j], (x, y, 1 - c))
                for i in range(n) for j in range(NDEV // 2)]

    def start(ins, outs, sems):
        for cp in copies(ins, outs, sems):
            cp.start()

    def finish(ins, outs, sems):
        for cp in copies(ins, outs, sems):
            cp.wait()

    return _Plan(list(gs), [S((NDEV // 2,) + a.shape[1:], a.dtype) for a in gs],
                 [pltpu.SemaphoreType.DMA((n, NDEV // 2)), pltpu.SemaphoreType.DMA((n, NDEV // 2))], start, finish)


def _pair_add(g, r1, core, name):
    _, r, c = g.shape
    tr = r if r * c * 2 <= 2 * 1024 * 1024 else _row_tile(r, c)

    def body(core_ref, g_ref, r_ref, o_ref):
        o_ref[...] = (g_ref[...].astype(F32) + r_ref[...].astype(F32)).astype(BF16)

    return pl.pallas_call(
        body, name=name,
        grid_spec=pltpu.PrefetchScalarGridSpec(
            num_scalar_prefetch=1, grid=(NDEV // 2, r // tr),
            in_specs=[pl.BlockSpec((None, tr, c), lambda j, k, core_ref: (2 * j + core_ref[0], k, 0)),
                      pl.BlockSpec((None, tr, c), lambda j, k, core_ref: (j, k, 0))],
            out_specs=pl.BlockSpec((None, tr, c), lambda j, k, core_ref: (j, k, 0))),
        out_shape=S((NDEV // 2, r, c), BF16), compiler_params=_cp("parallel", "parallel"),
    )(core, g, r1)


def _plan_chip_swap(ps):
    n = len(ps)

    def copies(ins, outs, sems, landed):
        send_sems, recv_sems, local_sems = sems
        x, y, c, _ = _place()
        mine = 2 * x + y
        local = [pltpu.make_async_copy(ins[i].at[mine], outs[i].at[mine], local_sems.at[i]) for i in range(n)]
        remote = [_remote(ins[i].at[2 * px + py], outs[i].at[2 * px + py if landed else mine], send_sems.at[i, j], recv_sems.at[i, j], (px, py, c))
                  for j, (px, py) in enumerate(_other_chips(x, y)) for i in range(n)]
        return local, remote

    def start(ins, outs, sems):
        local, remote = copies(ins, outs, sems, False)
        for cp in local + remote:
            cp.start()

    def finish(ins, outs, sems):
        local, remote = copies(ins, outs, sems, True)
        for cp in remote + local:
            cp.wait()

    return _Plan(list(ps), [S(a.shape, a.dtype) for a in ps],
                 [pltpu.SemaphoreType.DMA((n, 3)), pltpu.SemaphoreType.DMA((n, 3)), pltpu.SemaphoreType.DMA((n,))], start, finish)


def _pcall(body, name, args, in_specs, out_shape, out_specs, grid=(), scratch_shapes=(), sem=(), plans=()):
    n_in, n_out, n_scr = len(args), len(out_shape), len(scratch_shapes)
    counts = [(len(p.inputs), len(p.out_shapes), len(p.sem_shapes)) for p in plans]
    c_args = [a for p in plans for a in p.inputs]
    c_outs = [s for p in plans for s in p.out_shapes]
    c_sems = [s for p in plans for s in p.sem_shapes]

    def wrapped(*refs):
        cuts = [n_in, len(c_args), n_out, len(c_outs), n_scr, len(c_sems)]
        ins, c_in, outs, c_out, scr, c_sem = [refs[sum(cuts[:k]):sum(cuts[:k + 1])] for k in range(6)]

        def halves(which):
            a = b = s = 0
            for p, (na, nb, ns) in zip(plans, counts):
                getattr(p, which)(c_in[a:a + na], c_out[b:b + nb], c_sem[s:s + ns])
                a, b, s = a + na, b + nb, s + ns

        if not plans:
            body(*ins, *outs, *scr)
        elif not grid:
            halves("start")
            body(*ins, *outs, *scr)
            halves("finish")
        else:
            first = functools.reduce(jnp.logical_and, [pl.program_id(a) == 0 for a in range(len(grid))])
            last = functools.reduce(jnp.logical_and, [pl.program_id(a) == grid[a] - 1 for a in range(len(grid))])
            pl.when(first)(lambda: halves("start"))
            body(*ins, *outs, *scr)
            pl.when(last)(lambda: halves("finish"))

    aliases, a, b = {}, n_in, n_out
    for p, (na, nb, _) in zip(plans, counts):
        aliases.update({a + k: b + v for k, v in p.aliases.items()})
        a, b = a + na, b + nb
    res = pl.pallas_call(
        wrapped, name=name, grid=grid, in_specs=list(in_specs) + [ANY] * len(c_args),
        out_shape=list(out_shape) + c_outs, out_specs=list(out_specs) + [ANY] * len(c_outs),
        scratch_shapes=list(scratch_shapes) + c_sems, input_output_aliases=aliases,
        compiler_params=_cp(*(("arbitrary",) * len(grid) if plans else sem)),
    )(*args, *c_args)
    c_res, b = [], n_out
    for _, nb, _ in counts:
        c_res.append(res[b:b + nb])
        b += nb
    return res[:n_out], c_res


def _exchange(plans, name):
    return _pcall(lambda: None, name, [], [], [], [], plans=plans)[1]


def _ada_rows(c, w_ada, b_ada, name, plans=()):
    d, cols = c.shape[1], w_ada.shape[1]
    gather_c = _plan_all_to_all([c], True)
    gather_p = _plan_all_to_all([S((NDEV, cols), F32)], True)
    n_sem = len(gather_c.sem_shapes)

    def body(c_ref, w_ref, b_ref, ada_ref, act_ref, c_all, p_mine, p_all, *sems):
        gather_c.start([c_ref], [c_all], sems[:n_sem])
        gather_c.finish([c_ref], [c_all], sems[:n_sem])
        for s in range(NDEV):
            cc = c_all[s]
            act_ref[s:s + 1, :] = cc * _sigmoid(cc)
        p_mine[...] = _dot(act_ref[...].astype(BF16), w_ref[...].astype(BF16))
        gather_p.start([p_mine], [p_all], sems[n_sem:])
        gather_p.finish([p_mine], [p_all], sems[n_sem:])
        me = _place()[3]
        for s in range(NDEV):
            ada_ref[:, s * cols:(s + 1) * cols] = p_all[s, pl.ds(me, 1), :] + b_ref[:, s * cols:(s + 1) * cols]

    whole = pl.BlockSpec(memory_space=pltpu.VMEM)
    return _pcall(body, name, [c, w_ada, b_ada], [whole] * 3, [S((1, NDEV * cols), F32), S((NDEV, d), F32)], [whole] * 2,
                  scratch_shapes=[pltpu.VMEM((NDEV,) + c.shape, F32), pltpu.VMEM((NDEV, cols), F32), pltpu.VMEM((NDEV, NDEV, cols), F32)]
                  + gather_c.sem_shapes + gather_p.sem_shapes, plans=plans)


def _ffn_in(x, ada9, nrm, w_in, sh_row, sc_row, name, plans=()):
    t, d = x.shape
    nb, bw = w_in.shape[0] // 2, w_in.shape[1]
    tm = min(WIDE_ROW_TILE, t)

    def body(x_ref, ada_ref, n_ref, wg_ref, wu_ref, h_ref, gu_ref, s_ref):
        @pl.when(pl.program_id(1) == 0)
        def _():
            _, _, h = _normmod(x_ref[...], n_ref[...], ada_ref[sc_row:sc_row + 1, :], ada_ref[sh_row:sh_row + 1, :])
            h_ref[...] = h.astype(BF16)

        h = h_ref[...]
        g = _dot_nt(h, wg_ref[...])
        u = _dot_nt(h, wu_ref[...])
        gu_ref[0] = g.astype(BF16)
        gu_ref[1] = u.astype(BF16)
        s_ref[...] = (g * _sigmoid(g) * u).astype(BF16)

    return _pcall(
        body, name, [x, ada9, nrm, w_in, w_in], grid=(t // tm, nb),
        in_specs=[pl.BlockSpec((tm, d), lambda i, j: (i, 0)), pl.BlockSpec((N_ADA, d), lambda i, j: (0, 0)),
                  pl.BlockSpec((1, d), lambda i, j: (0, 0)),
                  pl.BlockSpec((None, bw, d), lambda i, j: (j, 0, 0)), pl.BlockSpec((None, bw, d), lambda i, j: (j + nb, 0, 0))],
        out_shape=[S((t, d), BF16), S((2, nb, t, bw), BF16), S((nb, t, bw), BF16)],
        out_specs=[pl.BlockSpec((tm, d), lambda i, j: (i, 0)), pl.BlockSpec((2, None, tm, bw), lambda i, j: (0, j, i, 0)),
                   pl.BlockSpec((None, tm, bw), lambda i, j: (j, i, 0))],
        sem=("parallel", "arbitrary"), plans=plans)


def _ffn_out(s, w_out, x, ada9, g_row, res_scale, name, plans=()):
    nb, t, bw = s.shape
    d = x.shape[1]
    tm = min(ROW_TILE, t)

    def body(s_ref, w_ref, x_ref, ada_ref, xo_ref, f_ref):
        acc = _dot(s_ref[0], w_ref[0])
        for b in range(1, nb):
            acc = acc + _dot(s_ref[b], w_ref[b])
        f_ref[...] = acc.astype(BF16)
        xo_ref[...] = x_ref[...] + (res_scale * ada_ref[g_row:g_row + 1, :]) * acc

    return _pcall(
        body, name, [s, w_out, x, ada9], grid=(t // tm,),
        in_specs=[pl.BlockSpec((nb, tm, bw), lambda i: (0, i, 0)), pl.BlockSpec((nb, bw, d), lambda i: (0, 0, 0)),
                  pl.BlockSpec((tm, d), lambda i: (i, 0)), pl.BlockSpec((N_ADA, d), lambda i: (0, 0))],
        out_shape=[S((t, d), F32), S((t, d), BF16)],
        out_specs=[pl.BlockSpec((tm, d), lambda i: (i, 0)), pl.BlockSpec((tm, d), lambda i: (i, 0))],
        sem=("parallel",), plans=plans)


def _ffn_bwd_ds(dxo, f, ada9, w_out, gu, g_row, res_scale, name, plans=()):
    t, d = dxo.shape
    nb, bw = w_out.shape[0], w_out.shape[1]
    tm = min(WIDE_ROW_TILE, t)

    def body(dxo_ref, f_ref, ada_ref, w_ref, gu_ref, df_ref, da_ref, dg_ref):
        i, j = pl.program_id(0), pl.program_id(1)

        @pl.when((i == 0) & (j == 0))
        def _():
            dg_ref[...] = jnp.zeros_like(dg_ref)

        @pl.when(j == 0)
        def _():
            dxo_t = dxo_ref[...]
            df_ref[...] = ((res_scale * ada_ref[g_row:g_row + 1, :]) * dxo_t).astype(BF16)
            dg_ref[...] += res_scale * _rsum8(dxo_t * f_ref[...].astype(F32))

        ds = _dot_nt(df_ref[...], w_ref[...])
        g = gu_ref[0].astype(F32)
        u = gu_ref[1].astype(F32)
        sg = _sigmoid_tanh(g)
        da_ref[0] = (ds * u * (sg * (1.0 + g * (1.0 - sg)))).astype(BF16)
        da_ref[1] = (ds * (g * sg)).astype(BF16)

    return _pcall(
        body, name, [dxo, f, ada9, w_out, gu], grid=(t // tm, nb),
        in_specs=[pl.BlockSpec((tm, d), lambda i, j: (i, 0)), pl.BlockSpec((tm, d), lambda i, j: (i, 0)),
                  pl.BlockSpec((N_ADA, d), lambda i, j: (0, 0)), pl.BlockSpec((None, bw, d), lambda i, j: (j, 0, 0)),
                  pl.BlockSpec((2, None, tm, bw), lambda i, j: (0, j, i, 0))],
        out_shape=[S((t, d), BF16), S((2, nb, t, bw), BF16), S((8, d), F32)],
        out_specs=[pl.BlockSpec((tm, d), lambda i, j: (i, 0)), pl.BlockSpec((2, None, tm, bw), lambda i, j: (0, j, i, 0)),
                   pl.BlockSpec((8, d), lambda i, j: (0, 0))],
        sem=("arbitrary", "arbitrary"), plans=plans)


def _ffn_bwd_dh(da, w_in, x, dxo, ada9, nrm, sh_row, sc_row, name, plans=(), tiles=None):
    t, d = x.shape
    nb, bw = w_in.shape[0] // 2, w_in.shape[1]
    tm = min(WIDE_ROW_TILE, t)
    i0, ni = (0, t // tm) if tiles is None else tiles

    def body(da_ref, wg_ref, wu_ref, x_ref, dxo_ref, ada_ref, n_ref, dx_ref, dsh_ref, dsc_ref, dn_ref, acc_ref):
        i, j = pl.program_id(0), pl.program_id(1)

        @pl.when((i == 0) & (j == 0))
        def _():
            dsh_ref[...] = jnp.zeros_like(dsh_ref)
            dsc_ref[...] = jnp.zeros_like(dsc_ref)
            dn_ref[...] = jnp.zeros_like(dn_ref)

        part = _dot(da_ref[0], wg_ref[...]) + _dot(da_ref[1], wu_ref[...])

        @pl.when(j == 0)
        def _():
            acc_ref[...] = part

        @pl.when(j > 0)
        def _():
            acc_ref[...] += part

        @pl.when(j == nb - 1)
        def _():
            for r0 in range(0, tm, min(EPILOGUE_ROWS, tm)):
                rows = slice(r0, r0 + min(EPILOGUE_ROWS, tm))
                dx, tsh, tsc, tn = _normmod_bwd(acc_ref[rows, :], x_ref[rows, :], n_ref[...], ada_ref[sc_row:sc_row + 1, :])
                dx_ref[rows, :] = dxo_ref[rows, :] + dx
                dsh_ref[...] += _rsum8(tsh)
                dsc_ref[...] += _rsum8(tsc)
                dn_ref[...] += _rsum8(tn)

    vec = pl.BlockSpec((8, d), lambda i, j: (0, 0))
    return _pcall(
        body, name, [da, w_in, w_in, x, dxo, ada9, nrm], grid=(ni, nb),
        in_specs=[pl.BlockSpec((2, None, tm, bw), lambda i, j: (0, j, i + i0, 0)),
                  pl.BlockSpec((None, bw, d), lambda i, j: (j, 0, 0)), pl.BlockSpec((None, bw, d), lambda i, j: (j + nb, 0, 0)),
                  pl.BlockSpec((tm, d), lambda i, j: (i + i0, 0)), pl.BlockSpec((tm, d), lambda i, j: (i + i0, 0)),
                  pl.BlockSpec((N_ADA, d), lambda i, j: (0, 0)), pl.BlockSpec((1, d), lambda i, j: (0, 0))],
        out_shape=[S((ni * tm, d), F32), S((8, d), F32), S((8, d), F32), S((8, d), F32)],
        out_specs=[pl.BlockSpec((tm, d), lambda i, j: (i, 0)), vec, vec, vec],
        scratch_shapes=[pltpu.VMEM((tm, d), F32)],
        sem=("arbitrary", "arbitrary"), plans=plans)


def _tn_matmul(a, b, a_block, a_map, b_block, b_map, out_shape, out_block, out_map, nblk, name, plans=()):
    t = a.shape[-2]
    tk = min(K_TILE, t)
    nk = t // tk

    def body(a_ref, b_ref, o_ref, acc_ref):
        k = pl.program_id(1)
        for q in (range(a_ref.shape[0]) if len(a_ref.shape) == 3 else [Ellipsis]):
            part = _dot_tn(a_ref[q], b_ref[...])

            @pl.when(k == 0)
            def _():
                acc_ref[q] = part

            @pl.when(k > 0)
            def _():
                acc_ref[q] += part

        @pl.when(k == nk - 1)
        def _():
            o_ref[...] = acc_ref[...].astype(BF16)

    (out,), moved = _pcall(
        body, name, [a, b], grid=(nblk, nk),
        in_specs=[pl.BlockSpec(a_block(tk), a_map), pl.BlockSpec(b_block(tk), b_map)],
        out_shape=[S(out_shape, BF16)], out_specs=[pl.BlockSpec(out_block, out_map)],
        scratch_shapes=[pltpu.VMEM(tuple(n for n in out_block if n is not None), F32)],
        sem=("parallel", "arbitrary"), plans=plans)
    return out, moved


def _ffn_backward(dxo, x_in, h, gu, s, f, ada9, nrm, w_in, w_out, rows, core, name, ds_plans=(), dwin_plans=(), last=False):
    sh_row, sc_row, g_row = rows
    nb, t, bw = s.shape
    d = x_in.shape[1]
    (df, da, dg), ds_moved = _ffn_bwd_ds(dxo, f, ada9, w_out, gu, g_row, 0.5, name + "_ds", plans=ds_plans)
    dw_in, dwin_moved = _tn_matmul(
        da.reshape(2 * nb, t, bw), h, lambda tk: (2, tk, bw), lambda sb, k: (sb, k, 0), lambda tk: (tk, d), lambda sb, k: (k, 0),
        (2 * nb, bw, d), (2, bw, d), lambda sb, k: (sb, 0, 0), nb, name + "_dwin",
        plans=dwin_plans(ds_moved) if callable(dwin_plans) else dwin_plans)
    dw_out, ((half_in,),) = _tn_matmul(
        s, df, lambda tk: (None, tk, bw), lambda sb, k: (sb, k, 0), lambda tk: (tk, d), lambda sb, k: (k, 0),
        (nb, bw, d), (None, bw, d), lambda sb, k: (sb, 0, 0), nb, name + "_dwout", plans=[_plan_sibling_swap([dw_in])])
    dw_out = dw_out.reshape(NDEV, -1, d)
    sum_in = _pair_add(dw_in, half_in, core, name + "_dwin_add")
    both = [_plan_chip_swap([sum_in]), _plan_sibling_swap([dw_out])]
    if not last or t // min(WIDE_ROW_TILE, t) < 2:
        (dx, dsh, dsc, dn), ((recv_in,), (half_out,)) = _ffn_bwd_dh(da, w_in, x_in, dxo, ada9, nrm, sh_row, sc_row, name + "_dh", plans=both)
        out = _pair_add(dw_out, half_out, core, name + "_dwout_add")
        if last:
            ((out,),) = _exchange([_plan_chip_swap([out])], name + "_dwout_swap")
        return dx, recv_in, out, (dsh, dsc, dg, dn), ds_moved, dwin_moved
    half = t // min(WIDE_ROW_TILE, t) // 2
    (dx_a, *sums_a), ((recv_in,), (half_out,)) = _ffn_bwd_dh(
        da, w_in, x_in, dxo, ada9, nrm, sh_row, sc_row, name + "_dh_a", plans=both, tiles=(0, half))
    sum_out = _pair_add(dw_out, half_out, core, name + "_dwout_add")
    (dx_b, *sums_b), ((recv_out,),) = _ffn_bwd_dh(
        da, w_in, x_in, dxo, ada9, nrm, sh_row, sc_row, name + "_dh_b", plans=[_plan_chip_swap([sum_out])], tiles=(half, half))
    dsh, dsc, dn = [jnp.concatenate(p, axis=0) for p in zip(sums_a, sums_b)]
    return jnp.concatenate([dx_a, dx_b], axis=0), recv_in, recv_out, (dsh, dsc, dg, dn), ds_moved, dwin_moved


def _mix_in(x, ada9, nrm, w, widths, dts, name, plans=()):
    t, d = x.shape
    n = w.shape[0]
    tm = min(ROW_TILE, t)
    starts = [sum(widths[:i]) for i in range(len(widths))]

    def body(x_ref, ada_ref, n_ref, w_ref, h_ref, *out_refs):
        _, _, h = _normmod(x_ref[...], n_ref[...], ada_ref[4:5, :], ada_ref[3:4, :])
        hb = h.astype(BF16)
        h_ref[...] = hb
        for o_ref, st, wd in zip(out_refs, starts, widths):
            o_ref[...] = _dot_nt(hb, w_ref[st:st + wd, :]).astype(o_ref.dtype)

    return _pcall(
        body, name, [x, ada9, nrm, w], grid=(t // tm,),
        in_specs=[pl.BlockSpec((tm, d), lambda i: (i, 0)), pl.BlockSpec((N_ADA, d), lambda i: (0, 0)),
                  pl.BlockSpec((1, d), lambda i: (0, 0)), pl.BlockSpec((n, d), lambda i: (0, 0))],
        out_shape=[S((t, d), BF16)] + [S((t, wd), dt) for wd, dt in zip(widths, dts)],
        out_specs=[pl.BlockSpec((tm, d), lambda i: (i, 0))] + [pl.BlockSpec((tm, wd), lambda i: (i, 0)) for wd in widths],
        sem=("parallel",), plans=plans)


def _tri(lower):
    r = lax.broadcasted_iota(jnp.int32, (CHUNK, CHUNK), 0)
    c = lax.broadcasted_iota(jnp.int32, (CHUNK, CHUNK), 1)
    return (r >= c) if lower else (c >= r)


def _dot_01(m, x):
    hi = x.astype(BF16)
    r1 = x - hi.astype(F32)
    mid = r1.astype(BF16)
    lo = (r1 - mid.astype(F32)).astype(BF16)
    return _dot(m, hi) + _dot(m, mid) + _dot(m, lo)


def _gla_chunk_terms(q, k, lg, low01):
    b = _dot_01(low01, lg)
    bl = b[CHUNK - 1:CHUNK, :]
    r = 0.5 * bl
    eb, ebl, em, en = jnp.exp(b), jnp.exp(bl - b), jnp.exp(b - r), jnp.exp(r - b)
    return eb, ebl, em, en, jnp.exp(bl), q * eb, k * ebl, q * em, k * en


def _scores(qm_h, knp, qk1_h):
    r = lax.broadcasted_iota(jnp.int32, (CHUNK, CHUNK), 0)
    c = lax.broadcasted_iota(jnp.int32, (CHUNK, CHUNK), 1)
    p = jnp.where(r > c, _dot_nt(qm_h, knp), 0.0)
    return jnp.where(r == c, jnp.sum(qk1_h, axis=1, keepdims=True), p)


def _gla_fwd(qk, v, gl, wg, bg, heads, name, plans=()):
    t = qk.shape[0]
    kw, vw = qk.shape[1] // 2, v.shape[1]
    dk, dv = kw // heads, vw // heads
    assert dk == 64 and dv == 128 and kw % 128 == 0
    gt = min(ROW_TILE, t)
    nc = gt // CHUNK
    scale = dk ** -0.5

    def body(qk_ref, v_ref, gl_ref, wg_ref, bg_ref, o_ref, lg_ref, sall_ref, st_ref):
        @pl.when(pl.program_id(0) == 0)
        def _():
            st_ref[...] = jnp.zeros_like(st_ref)

        gk = _dot(gl_ref[...].astype(BF16), wg_ref[...]) + bg_ref[...]
        lg_ref[...] = (jnp.minimum(gk, 0.0) - jnp.log(1.0 + jnp.exp(-jnp.abs(gk)))) / GATE_NORMALIZER
        low01 = _tri(True).astype(BF16)
        lane = lax.broadcasted_iota(jnp.int32, (CHUNK, LANES), 1)

        def chunk(ci, carry):
            rows = pl.ds(pl.multiple_of(ci * CHUNK, CHUNK), CHUNK)
            q = qk_ref[rows, 0:kw] * scale
            k = qk_ref[rows, kw:2 * kw]
            qk1 = q.astype(BF16).astype(F32) * k.astype(BF16).astype(F32)
            eb, ebl, em, en, ebl_row, qe, ke, qm, kn = _gla_chunk_terms(q, k, lg_ref[rows, :], low01)
            for h in range(heads):
                lanes = slice(LANES * (h // 2), LANES * (h // 2) + LANES)
                own = (lane < 64) if h % 2 == 0 else (lane >= 64)
                knp = kn[:, lanes].astype(BF16)
                qm_h = jnp.where(own, qm[:, lanes], 0.0).astype(BF16)
                qe_h = jnp.where(own, qe[:, lanes], 0.0).astype(BF16)
                ke_h = jnp.where(own, ke[:, lanes], 0.0).astype(BF16)
                v_h = v_ref[rows, h * dv:(h + 1) * dv]
                st = st_ref[h]
                sall_ref[ci, h] = st
                p = _scores(qm_h, knp, jnp.where(own, qk1[:, lanes], 0.0))
                o_ref[rows, h * dv:(h + 1) * dv] = _dot(p.astype(BF16), v_h) + _dot_nt(qe_h, st.astype(BF16))
                st_ref[h] = st * ebl_row[:, lanes] + _dot_tn(v_h, ke_h)
            return carry

        lax.fori_loop(0, nc, chunk, 0, unroll=True)

    return _pcall(
        body, name, [qk, v, gl, wg, bg], grid=(t // gt,),
        in_specs=[pl.BlockSpec((gt, 2 * kw), lambda i: (i, 0)), pl.BlockSpec((gt, vw), lambda i: (i, 0)),
                  pl.BlockSpec((gt, LANES), lambda i: (i, 0)), pl.BlockSpec((LANES, kw), lambda i: (0, 0)),
                  pl.BlockSpec((1, kw), lambda i: (0, 0))],
        out_shape=[S((t, vw), F32), S((t, kw), F32), S((t // CHUNK, heads, dv, LANES), F32)],
        out_specs=[pl.BlockSpec((gt, vw), lambda i: (i, 0)), pl.BlockSpec((gt, kw), lambda i: (i, 0)),
                   pl.BlockSpec((nc, heads, dv, LANES), lambda i: (i, 0, 0, 0))],
        scratch_shapes=[pltpu.VMEM((heads, dv, LANES), F32)],
        sem=("arbitrary",), plans=plans)


def _gla_bwd(qk, v, lg, do, sall, gl, wg, heads, name):
    t = qk.shape[0]
    kw, vw = qk.shape[1] // 2, v.shape[1]
    dk, dv = kw // heads, vw // heads
    gt = min(ROW_TILE, t)
    nc = gt // CHUNK
    nt = t // gt
    scale = dk ** -0.5

    def body(qk_ref, v_ref, lg_ref, do_ref, sall_ref, gl_ref, wg_ref, dqk_ref, dv_ref, dgl_ref, dwg_ref, dbg_ref, dst_ref, dgk_ref):
        @pl.when(pl.program_id(0) == 0)
        def _():
            dst_ref[...] = jnp.zeros_like(dst_ref)
            dwg_ref[...] = jnp.zeros_like(dwg_ref)
            dbg_ref[...] = jnp.zeros_like(dbg_ref)

        low01 = _tri(True).astype(BF16)
        up01 = _tri(False).astype(BF16)
        causal = _tri(True)
        lane = lax.broadcasted_iota(jnp.int32, (CHUNK, LANES), 1)
        last_row = lax.broadcasted_iota(jnp.int32, (CHUNK, kw), 0) == CHUNK - 1

        def chunk(cj, carry):
            ci = nc - 1 - cj
            rows = pl.ds(pl.multiple_of(ci * CHUNK, CHUNK), CHUNK)
            q = qk_ref[rows, 0:kw] * scale
            k = qk_ref[rows, kw:2 * kw]
            qk1 = q.astype(BF16).astype(F32) * k.astype(BF16).astype(F32)
            lgc = lg_ref[rows, :]
            eb, ebl, em, en, ebl_row, qe, ke, qm, kn = _gla_chunk_terms(q, k, lgc, low01)
            dqe, dqm, dkn, dke, drow = [], [], [], [], []
            for pr in range(kw // LANES):
                lanes = slice(LANES * pr, LANES * pr + LANES)
                knp = kn[:, lanes].astype(BF16)
                parts = []
                for half in range(2):
                    h = 2 * pr + half
                    own = (lane < 64) if half == 0 else (lane >= 64)
                    qm_h = jnp.where(own, qm[:, lanes], 0.0).astype(BF16)
                    qe_h = jnp.where(own, qe[:, lanes], 0.0).astype(BF16)
                    ke_h = jnp.where(own, ke[:, lanes], 0.0).astype(BF16)
                    v_h = v_ref[rows, h * dv:(h + 1) * dv]
                    do_h = do_ref[rows, h * dv:(h + 1) * dv]
                    st = sall_ref[ci, h]
                    dst = dst_ref[h]
                    stb, dstb = st.astype(BF16), dst.astype(BF16)
                    p = _scores(qm_h, knp, jnp.where(own, qk1[:, lanes], 0.0)).astype(BF16)
                    dp = jnp.where(causal, _dot_nt(do_h, v_h), 0.0).astype(BF16)
                    dv_ref[rows, h * dv:(h + 1) * dv] = (_dot_tn(p, do_h) + _dot_nt(ke_h, dstb)).astype(BF16)
                    parts.append((jnp.where(own, _dot(dp, knp), 0.0), _dot_tn(dp, qm_h), _dot(do_h, stb), _dot(v_h, dstb),
                                  jnp.sum(st * dst, axis=0, keepdims=True)))
                    dst_ref[h] = dst * ebl_row[:, lanes] + _dot_tn(do_h, qe_h)
                dqm.append(parts[0][0] + parts[1][0])
                dkn.append(parts[0][1] + parts[1][1])
                dqe.append(parts[0][2] + parts[1][2])
                dke.append(parts[0][3] + parts[1][3])
                drow.append(parts[0][4] + parts[1][4])
            dqm, dkn, dqe, dke, drow = [jnp.concatenate(a, axis=1) for a in (dqm, dkn, dqe, dke, drow)]
            dqk_ref[rows, 0:kw] = ((dqe * eb + dqm * em) * scale).astype(BF16)
            dqk_ref[rows, kw:2 * kw] = (dke * ebl + dkn * en).astype(BF16)
            tke = dke * ke
            db = dqe * qe + dqm * qm - dkn * kn - tke
            dbl = jnp.sum(tke, axis=0, keepdims=True) + drow * ebl_row
            db = db + jnp.where(last_row, dbl, 0.0)
            dlg = _dot_01(up01, db)
            dgk_ref[rows, :] = dlg * ((1.0 - jnp.exp(GATE_NORMALIZER * lgc)) / GATE_NORMALIZER)
            return carry

        lax.fori_loop(0, nc, chunk, 0, unroll=True)
        dgk = dgk_ref[...]
        dgkb = dgk.astype(BF16)
        dgl_ref[...] = _dot_nt(dgkb, wg_ref[...]).astype(BF16)
        dwg_ref[...] += _dot_tn(gl_ref[...].astype(BF16), dgkb)
        dbg_ref[...] += _rsum8(dgk)

    rev = lambda i: (nt - 1 - i, 0)
    return pl.pallas_call(
        body, name=name, grid=(nt,),
        in_specs=[pl.BlockSpec((gt, 2 * kw), rev), pl.BlockSpec((gt, vw), rev), pl.BlockSpec((gt, kw), rev),
                  pl.BlockSpec((gt, vw), rev), pl.BlockSpec((nc, heads, dv, LANES), lambda i: (nt - 1 - i, 0, 0, 0)),
                  pl.BlockSpec((gt, LANES), rev), pl.BlockSpec((LANES, kw), lambda i: (0, 0))],
        out_shape=[S((t, 2 * kw), BF16), S((t, vw), BF16), S((t, LANES), BF16), S((LANES, kw), F32), S((8, kw), F32)],
        out_specs=[pl.BlockSpec((gt, 2 * kw), rev), pl.BlockSpec((gt, vw), rev), pl.BlockSpec((gt, LANES), rev),
                   pl.BlockSpec((LANES, kw), lambda i: (0, 0)), pl.BlockSpec((8, kw), lambda i: (0, 0))],
        scratch_shapes=[pltpu.VMEM((heads, dv, LANES), F32), pltpu.VMEM((gt, kw), F32)],
        compiler_params=_cp("arbitrary"),
    )(qk, v, lg, do, sall, gl, wg)


def _conv_taps(cx_ref, halo_ref, first, cw):
    tm = cx_ref.shape[0]
    u = cx_ref[:, cw:2 * cw].astype(F32) * cx_ref[:, 2 * cw:3 * cw].astype(F32)
    uh = halo_ref[:, cw:2 * cw].astype(F32) * halo_ref[:, 2 * cw:3 * cw].astype(F32)
    uh = jnp.where(first, 0.0, uh)
    before1, before2 = uh[BF16_ROWS - 1:BF16_ROWS, :], uh[BF16_ROWS - 2:BF16_ROWS - 1, :]
    row = lax.broadcasted_iota(jnp.int32, (tm, cw), 0)
    u1 = jnp.where(row == 0, before1, pltpu.roll(u, 1, 0))
    u2 = jnp.where(row == 0, before2, jnp.where(row == 1, before1, pltpu.roll(u, 2, 0)))
    return u, u1, u2


def _head_norm(o_h, gn):
    rstd = lax.rsqrt(jnp.mean(o_h * o_h, axis=-1, keepdims=True) + EPS)
    ohat = o_h * rstd
    return ohat, rstd, ohat * gn


def _mix_out(cx, o, go, conv_w, gn, w_out, x, ada9, heads, name, plans=()):
    t, d = x.shape
    cw, vw = conv_w.shape[1], o.shape[1]
    dv = vw // heads
    tm = min(ROW_TILE, t)

    def body(cx_ref, halo_ref, o_ref, go_ref, cwt_ref, gn_ref, w_ref, x_ref, ada_ref, xo_ref, m_ref, y_ref):
        u, u1, u2 = _conv_taps(cx_ref, halo_ref, pl.program_id(0) == 0, cw)
        yc = cwt_ref[0:1, :] * u2 + cwt_ref[1:2, :] * u1 + cwt_ref[2:3, :] * u
        y_ref[:, 0:cw] = (cx_ref[:, 0:cw].astype(F32) * yc).astype(BF16)
        for h in range(heads):
            cols = slice(h * dv, (h + 1) * dv)
            _, _, on = _head_norm(o_ref[:, cols], gn_ref[...])
            g = go_ref[:, cols].astype(F32)
            y_ref[:, cw + h * dv:cw + (h + 1) * dv] = (on * (g * _sigmoid(g))).astype(BF16)
        m = _dot(y_ref[...], w_ref[...])
        m_ref[...] = m.astype(BF16)
        xo_ref[...] = x_ref[...] + ada_ref[5:6, :] * m

    return _pcall(
        body, name, [cx, cx, o, go, conv_w, gn, w_out, x, ada9], grid=(t // tm,),
        in_specs=[pl.BlockSpec((tm, 3 * cw), lambda i: (i, 0)),
                  pl.BlockSpec((BF16_ROWS, 3 * cw), lambda i: (jnp.maximum(i * (tm // BF16_ROWS) - 1, 0), 0)),
                  pl.BlockSpec((tm, vw), lambda i: (i, 0)), pl.BlockSpec((tm, vw), lambda i: (i, 0)),
                  pl.BlockSpec((3, cw), lambda i: (0, 0)), pl.BlockSpec((1, dv), lambda i: (0, 0)),
                  pl.BlockSpec((cw + vw, d), lambda i: (0, 0)), pl.BlockSpec((tm, d), lambda i: (i, 0)),
                  pl.BlockSpec((N_ADA, d), lambda i: (0, 0))],
        out_shape=[S((t, d), F32), S((t, d), BF16), S((t, cw + vw), BF16)],
        out_specs=[pl.BlockSpec((tm, d), lambda i: (i, 0)), pl.BlockSpec((tm, d), lambda i: (i, 0)),
                   pl.BlockSpec((tm, cw + vw), lambda i: (i, 0))],
        sem=("parallel",), plans=plans)


def _mix_bwd_a(dxo, m, ada9, w_out, cx, o, go, conv_w, gn, heads, name, plans=()):
    t, d = dxo.shape
    cw, vw = conv_w.shape[1], o.shape[1]
    dv = vw // heads
    tm = min(ROW_TILE, t)

    def body(dxo_ref, m_ref, ada_ref, w_ref, cx_ref, halo_ref, o_ref, go_ref, cwt_ref, gn_ref,
             dm_ref, dyc_ref, dcb_ref, do_ref, dgo_ref, dg_ref, dcw_ref, dgn_ref):
        @pl.when(pl.program_id(0) == 0)
        def _():
            dg_ref[...] = jnp.zeros_like(dg_ref)
            dcw_ref[...] = jnp.zeros_like(dcw_ref)
            dgn_ref[...] = jnp.zeros_like(dgn_ref)

        dxo_t = dxo_ref[...]
        dmb = (ada_ref[5:6, :] * dxo_t).astype(BF16)
        dm_ref[...] = dmb
        dg_ref[...] += _rsum8(dxo_t * m_ref[...].astype(F32))
        dy = _dot_nt(dmb, w_ref[...])
        u, u1, u2 = _conv_taps(cx_ref, halo_ref, pl.program_id(0) == 0, cw)
        yc = cwt_ref[0:1, :] * u2 + cwt_ref[1:2, :] * u1 + cwt_ref[2:3, :] * u
        dyv = dy[:, 0:cw]
        dcb_ref[...] = (dyv * yc).astype(BF16)
        dyc = dyv * cx_ref[:, 0:cw].astype(F32)
        dyc_ref[...] = dyc
        dcw_ref[0] += _rsum8(dyc * u2)
        dcw_ref[1] += _rsum8(dyc * u1)
        dcw_ref[2] += _rsum8(dyc * u)
        for h in range(heads):
            cols = slice(h * dv, (h + 1) * dv)
            ohat, rstd, on = _head_norm(o_ref[:, cols], gn_ref[...])
            g = go_ref[:, cols].astype(F32)
            sg = _sigmoid_tanh(g)
            dyg = dy[:, cw + h * dv:cw + (h + 1) * dv]
            dgo_ref[:, cols] = (dyg * on * (sg * (1.0 + g * (1.0 - sg)))).astype(BF16)
            don = dyg * (g * sg)
            dgn_ref[...] += _rsum8(don * ohat)
            tt = don * gn_ref[...]
            do_ref[:, cols] = (rstd * (tt - ohat * jnp.mean(tt * ohat, axis=-1, keepdims=True))).astype(BF16)

    return _pcall(
        body, name, [dxo, m, ada9, w_out, cx, cx, o, go, conv_w, gn], grid=(t // tm,),
        in_specs=[pl.BlockSpec((tm, d), lambda i: (i, 0)), pl.BlockSpec((tm, d), lambda i: (i, 0)),
                  pl.BlockSpec((N_ADA, d), lambda i: (0, 0)), pl.BlockSpec((cw + vw, d), lambda i: (0, 0)),
                  pl.BlockSpec((tm, 3 * cw), lambda i: (i, 0)),
                  pl.BlockSpec((BF16_ROWS, 3 * cw), lambda i: (jnp.maximum(i * (tm // BF16_ROWS) - 1, 0), 0)),
                  pl.BlockSpec((tm, vw), lambda i: (i, 0)), pl.BlockSpec((tm, vw), lambda i: (i, 0)),
                  pl.BlockSpec((3, cw), lambda i: (0, 0)), pl.BlockSpec((1, dv), lambda i: (0, 0))],
        out_shape=[S((t, d), BF16), S((t, cw), F32), S((t, cw), BF16), S((t, vw), BF16), S((t, vw), BF16),
                   S((8, d), F32), S((3, 8, cw), F32), S((8, dv), F32)],
        out_specs=[pl.BlockSpec((tm, d), lambda i: (i, 0)), pl.BlockSpec((tm, cw), lambda i: (i, 0)),
                   pl.BlockSpec((tm, cw), lambda i: (i, 0)), pl.BlockSpec((tm, vw), lambda i: (i, 0)),
                   pl.BlockSpec((tm, vw), lambda i: (i, 0)), pl.BlockSpec((8, d), lambda i: (0, 0)),
                   pl.BlockSpec((3, 8, cw), lambda i: (0, 0, 0)), pl.BlockSpec((8, dv), lambda i: (0, 0))],
        sem=("arbitrary",), plans=plans)


def _mix_bwd_b(dyc, cx, dcb, dqk, dvv, dgo, dgl, conv_w, w, x, dxo, ada9, nrm, name):
    t, d = x.shape
    cw = conv_w.shape[1]
    n = w.shape[0]
    tm = min(ROW_TILE, t)
    nt = t // tm
    pieces = [dcb.shape[1], cw, cw, dqk.shape[1], dvv.shape[1], dgo.shape[1], dgl.shape[1]]
    assert sum(pieces) == n

    def body(dyc_ref, nxt_ref, cx_ref, dcb_ref, dqk_ref, dv_ref, dgo_ref, dgl_ref, cwt_ref, w_ref, x_ref, dxo_ref, ada_ref, n_ref,
             dx_ref, dp_ref, dsh_ref, dsc_ref, dn_ref):
        i = pl.program_id(0)

        @pl.when(i == 0)
        def _():
            dsh_ref[...] = jnp.zeros_like(dsh_ref)
            dsc_ref[...] = jnp.zeros_like(dsc_ref)
            dn_ref[...] = jnp.zeros_like(dn_ref)

        dyc_t = dyc_ref[...]
        nxt = jnp.where(i == nt - 1, 0.0, nxt_ref[...])
        row = lax.broadcasted_iota(jnp.int32, (tm, cw), 0)
        d1 = jnp.where(row == tm - 1, nxt[0:1, :], pltpu.roll(dyc_t, tm - 1, 0))
        d2 = jnp.where(row == tm - 2, nxt[0:1, :], jnp.where(row == tm - 1, nxt[1:2, :], pltpu.roll(dyc_t, tm - 2, 0)))
        du = cwt_ref[2:3, :] * dyc_t + cwt_ref[1:2, :] * d1 + cwt_ref[0:1, :] * d2
        c0 = 0
        dp_ref[:, c0:c0 + cw] = dcb_ref[...]
        dp_ref[:, cw:2 * cw] = (du * cx_ref[:, 2 * cw:3 * cw].astype(F32)).astype(BF16)
        dp_ref[:, 2 * cw:3 * cw] = (du * cx_ref[:, cw:2 * cw].astype(F32)).astype(BF16)
        c0 = 3 * cw
        for ref in (dqk_ref, dv_ref, dgo_ref, dgl_ref):
            wd = ref.shape[1]
            dp_ref[:, c0:c0 + wd] = ref[...]
            c0 += wd
        dh = _dot(dp_ref[...], w_ref[...])
        dx, tsh, tsc, tn = _normmod_bwd(dh, x_ref[...], n_ref[...], ada_ref[4:5, :])
        dx_ref[...] = dxo_ref[...] + dx
        dsh_ref[...] += _rsum8(tsh)
        dsc_ref[...] += _rsum8(tsc)
        dn_ref[...] += _rsum8(tn)

    row_spec = lambda wd: pl.BlockSpec((tm, wd), lambda i: (i, 0))
    vec = pl.BlockSpec((8, d), lambda i: (0, 0))
    return pl.pallas_call(
        body, name=name, grid=(nt,),
        in_specs=[row_spec(cw), pl.BlockSpec((8, cw), lambda i: (jnp.minimum((i + 1) * (tm // 8), t // 8 - 1), 0)),
                  row_spec(3 * cw), row_spec(cw), row_spec(dqk.shape[1]), row_spec(dvv.shape[1]), row_spec(dgo.shape[1]),
                  row_spec(dgl.shape[1]), pl.BlockSpec((3, cw), lambda i: (0, 0)), pl.BlockSpec((n, d), lambda i: (0, 0)),
                  row_spec(d), row_spec(d), pl.BlockSpec((N_ADA, d), lambda i: (0, 0)), pl.BlockSpec((1, d), lambda i: (0, 0))],
        out_shape=[S((t, d), F32), S((t, n), BF16), S((8, d), F32), S((8, d), F32), S((8, d), F32)],
        out_specs=[row_spec(d), row_spec(n), vec, vec, vec],
        compiler_params=_cp("arbitrary"),
    )(dyc, dyc, cx, dcb, dqk, dvv, dgo, dgl, conv_w, w, x, dxo, ada9, nrm)


def _ffn_out_loss(s, w_out, x, ada9, g_row, res_scale, target, nrm, name):
    nb, t, bw = s.shape
    d = x.shape[1]
    tm = min(ROW_TILE, t)
    nt = t // tm

    def body(s_ref, w_ref, x_ref, ada_ref, tg_ref, n_ref, f_ref, loss_ref, dx_ref, dn_ref, acc_ref):
        i = pl.program_id(0)

        @pl.when(i == 0)
        def _():
            acc_ref[...] = jnp.zeros_like(acc_ref)
            dn_ref[...] = jnp.zeros_like(dn_ref)

        f = _dot(s_ref[0], w_ref[0])
        for b in range(1, nb):
            f = f + _dot(s_ref[b], w_ref[b])
        f_ref[...] = f.astype(BF16)
        xt = x_ref[...] + (res_scale * ada_ref[g_row:g_row + 1, :]) * f
        rstd = lax.rsqrt(jnp.mean(xt * xt, axis=-1, keepdims=True) + EPS)
        xhat = xt * rstd
        err = xhat * n_ref[...] - tg_ref[...]
        acc_ref[...] += _rsum8(err * err)
        dy = err * (1.0 / d)
        dn_ref[...] += _rsum8(dy * xhat)
        dxhat = dy * n_ref[...]
        dx_ref[...] = rstd * (dxhat - xhat * jnp.mean(dxhat * xhat, axis=-1, keepdims=True))

        @pl.when(i == nt - 1)
        def _():
            loss_ref[...] = jnp.full(loss_ref.shape, (0.5 / d) * jnp.sum(acc_ref[...]), F32)

    return pl.pallas_call(
        body, name=name, grid=(nt,),
        in_specs=[pl.BlockSpec((nb, tm, bw), lambda i: (0, i, 0)), pl.BlockSpec((nb, bw, d), lambda i: (0, 0, 0)),
                  pl.BlockSpec((tm, d), lambda i: (i, 0)), pl.BlockSpec((N_ADA, d), lambda i: (0, 0)),
                  pl.BlockSpec((tm, d), lambda i: (i, 0)), pl.BlockSpec((1, d), lambda i: (0, 0))],
        out_shape=[S((t, d), BF16), S((1, LANES), F32), S((t, d), F32), S((8, d), F32)],
        out_specs=[pl.BlockSpec((tm, d), lambda i: (i, 0)), pl.BlockSpec((1, LANES), lambda i: (0, 0)),
                   pl.BlockSpec((tm, d), lambda i: (i, 0)), pl.BlockSpec((8, d), lambda i: (0, 0))],
        scratch_shapes=[pltpu.VMEM((8, d), F32)],
        compiler_params=_cp("arbitrary"),
    )(s, w_out, x, ada9, target, nrm)


def _pack_smalls(vec_parts, dcw, dbg, dgn, dwg, loss_v, rank, name):
    d = vec_parts[0].shape[1]
    cw, kw, dv = dcw.shape[2], dbg.shape[1], dgn.shape[1]
    nv = len(vec_parts)
    loss_row = nv + 2 + rank * kw // d
    assert 2 * cw == d and cw + kw + dv <= d and (rank * kw) % d == 0 and loss_row < PACK_ROWS
    per_row = d // kw

    def body(*refs):
        vrefs, (dcw_ref, dbg_ref, dgn_ref, dwg_ref, loss_ref, o_ref) = refs[:nv], refs[nv:]
        o_ref[...] = jnp.zeros_like(o_ref)
        o_ref[loss_row:loss_row + 1, 0:loss_ref.shape[1]] = loss_ref[...]
        for r, ref in enumerate(vrefs):
            o_ref[r:r + 1, :] = jnp.sum(ref[...], axis=0, keepdims=True)
        o_ref[nv:nv + 1, 0:cw] = jnp.sum(dcw_ref[0], axis=0, keepdims=True)
        o_ref[nv:nv + 1, cw:2 * cw] = jnp.sum(dcw_ref[1], axis=0, keepdims=True)
        o_ref[nv + 1:nv + 2, 0:cw] = jnp.sum(dcw_ref[2], axis=0, keepdims=True)
        o_ref[nv + 1:nv + 2, cw:cw + kw] = jnp.sum(dbg_ref[...], axis=0, keepdims=True)
        o_ref[nv + 1:nv + 2, cw + kw:cw + kw + dv] = jnp.sum(dgn_ref[...], axis=0, keepdims=True)
        for r in range(rank):
            o_ref[nv + 2 + r // per_row:nv + 3 + r // per_row, (r % per_row) * kw:(r % per_row + 1) * kw] = dwg_ref[r:r + 1, :]

    return pl.pallas_call(body, name=name, out_shape=S((PACK_ROWS, d), F32), compiler_params=_cp())(*vec_parts, dcw, dbg, dgn, dwg, loss_v)


def _sum_slots(a, name):
    def body(a_ref, o_ref):
        acc = a_ref[0]
        for s in range(1, NDEV):
            acc = acc + a_ref[s]
        o_ref[...] = acc

    return pl.pallas_call(body, name=name, out_shape=S(a.shape[1:], F32), compiler_params=_cp())(a)


def _adamw(w, g, m, v):
    m = ADAM_B1 * m + (1.0 - ADAM_B1) * g
    v = ADAM_B2 * v + (1.0 - ADAM_B2) * (g * g)
    m_hat = m / (1.0 - ADAM_B1 ** ADAM_STEP)
    v_hat = v / (1.0 - ADAM_B2 ** ADAM_STEP)
    return -ADAM_LR * (m_hat / (jnp.sqrt(v_hat) + ADAM_EPS) + ADAM_WD * w), m, v


def _adam_slots(recv, w, m, v, name):
    r, c = w.shape
    slots = recv.shape[0]
    tr = _row_tile(r, c)

    def body(recv_ref, w_ref, m_ref, v_ref, g_ref, d_ref, mo_ref, vo_ref):
        g = recv_ref[0].astype(F32)
        for s in range(1, slots):
            g = g + recv_ref[s].astype(F32)
        g_ref[...] = g
        d_ref[...], mo_ref[...], vo_ref[...] = _adamw(w_ref[...], g, m_ref[...], v_ref[...])

    blk = pl.BlockSpec((tr, c), lambda i: (i, 0))
    return pl.pallas_call(
        body, name=name, grid=(r // tr,),
        in_specs=[pl.BlockSpec((slots, tr, c), lambda i: (0, i, 0)), blk, blk, blk],
        out_shape=[S((r, c), F32)] * 4, out_specs=[blk] * 4, compiler_params=_cp("parallel"),
    )(recv, w, m, v)


def _adam_w_ada(act_t, dada, w, m, v, name):
    r, c = w.shape
    tr = 128
    nb = act_t.shape[1]

    def body(a_ref, da_ref, w_ref, m_ref, v_ref, g_ref, d_ref, mo_ref, vo_ref):
        g = a_ref[:, 0:1] * da_ref[0:1, :]
        for b in range(1, nb):
            g = g + a_ref[:, b:b + 1] * da_ref[b:b + 1, :]
        g_ref[...] = g
        d_ref[...], mo_ref[...], vo_ref[...] = _adamw(w_ref[...], g, m_ref[...], v_ref[...])

    blk = pl.BlockSpec((tr, c), lambda i: (i, 0))
    return pl.pallas_call(
        body, name=name, grid=(r // tr,),
        in_specs=[pl.BlockSpec((tr, nb), lambda i: (i, 0)), pl.BlockSpec((nb, c), lambda i: (0, 0)), blk, blk, blk],
        out_shape=[S((r, c), F32)] * 4, out_specs=[blk] * 4, compiler_params=_cp("parallel"),
    )(act_t, dada, w, m, v)


def _adam_smalls(ws, gs, ms, vs, name):
    n = len(ws)

    def body(*refs):
        w_r, g_r, m_r, v_r = (refs[k * n:(k + 1) * n] for k in range(4))
        d_o, m_o, v_o = (refs[(4 + k) * n:(5 + k) * n] for k in range(3))
        for i in range(n):
            d_o[i][...], m_o[i][...], v_o[i][...] = _adamw(w_r[i][...], g_r[i][...], m_r[i][...], v_r[i][...])

    shapes = [S(w.shape, F32) for w in ws]
    outs = pl.pallas_call(body, name=name, out_shape=shapes * 3, compiler_params=_cp())(*ws, *gs, *ms, *vs)
    return outs[:n], outs[n:2 * n], outs[2 * n:]


def kernel(x, c, w_ada, b_ada, norm_ffn1, w_ffn1_in, w_ffn1_out, norm_mix, w_mix_in, conv_w, w_gk2, b_gk, gla_norm, w_mix_out, norm_ffn2, w_ffn2_in, w_ffn2_out, norm_final, loss_target, m_w_ada, m_b_ada, m_norm_ffn1, m_w_ffn1_in, m_w_ffn1_out, m_norm_mix, m_w_mix_in, m_conv_w, m_w_gk2, m_b_gk, m_gla_norm, m_w_mix_out, m_norm_ffn2, m_w_ffn2_in, m_w_ffn2_out, m_norm_final, v_w_ada, v_b_ada, v_norm_ffn1, v_w_ffn1_in, v_w_ffn1_out, v_norm_mix, v_w_mix_in, v_conv_w, v_w_gk2, v_b_gk, v_gla_norm, v_w_mix_out, v_norm_ffn2, v_w_ffn2_in, v_w_ffn2_out, v_norm_final):
    t, d = x.shape[1], x.shape[2]
    x0, tgt = x[0], loss_target[0]
    rank, kw = w_gk2.shape[1], w_gk2.shape[2] * NDEV
    cw = conv_w.shape[2] * NDEV
    dv = gla_norm.shape[1]
    vw = d - cw
    heads = vw // dv
    mix_cols = w_mix_in.shape[2]
    widths = [3 * cw, 2 * kw, vw, vw, LANES]
    n_proj = 3 * cw + 2 * kw + 2 * vw + rank
    assert n_proj == mix_cols * NDEV and rank <= LANES
    me = 4 * lax.axis_index("x") + 2 * lax.axis_index("y") + lax.axis_index("c")

    core = lax.axis_index("c").astype(jnp.int32).reshape(1)
    bf = lambda a: a[0].astype(BF16)
    bft = lambda a: a[0].T.astype(BF16)
    nb = NDEV // 2

    (ada_row, act_all), ((w1i, cwt_all, wg_all),) = _ada_rows(
        c, w_ada[0], b_ada, "ada_rows", plans=[_plan_gather([bft(w_ffn1_in), conv_w[0], w_gk2[0]])])
    ada9 = ada_row.reshape(N_ADA, d)
    cwt = cwt_all.transpose(1, 0, 2).reshape(conv_w.shape[1], cw)
    wg = jnp.pad(wg_all.transpose(1, 0, 2).reshape(rank, kw), ((0, LANES - rank), (0, 0))).astype(BF16)

    (h1, gu1, s1), ((w1o, wmi),) = _ffn_in(x0, ada9, norm_ffn1, w1i, 0, 1, "ffn1_in", plans=[_plan_gather([bf(w_ffn1_out), bft(w_mix_in)])])
    w1o = w1o.reshape(nb, -1, d)
    wmi = jnp.pad(wmi.reshape(n_proj, d), ((0, sum(widths) - n_proj), (0, 0)))
    w2i_mine = bft(w_ffn2_in)
    quarter = w2i_mine.shape[0] // 4
    part = lambda k, into=None: _plan_gather([w2i_mine], rows=(k * quarter, quarter), into=into)
    (x1, f1), ((wmo,), (w2i,)) = _ffn_out(s1, w1o, x0, ada9, 2, 0.5, "ffn1_out", plans=[_plan_gather([bf(w_mix_out)]), part(0)])
    wmo = wmo.reshape(cw + vw, d)
    (h2, cx, qk, vv, go, gl), ((w2i,),) = _mix_in(x1, ada9, norm_mix, wmi, widths, [BF16, F32, BF16, BF16, F32], "mix_in",
                                                 plans=[part(1, [w2i])])
    (o, lg, sall), ((w2i,),) = _gla_fwd(qk, vv, gl, wg, b_gk, heads, "gla_fwd", plans=[part(2, [w2i])])
    (x2, mm, ycat), ((w2i,),) = _mix_out(cx, o, go, cwt, gla_norm, wmo, x1, ada9, heads, "mix_out", plans=[part(3, [w2i])])
    (h3, gu3, s3), ((w2o,),) = _ffn_in(x2, ada9, norm_ffn2, w2i, 6, 7, "ffn2_in", plans=[_plan_gather([bf(w_ffn2_out)])])
    w2o = w2o.reshape(nb, -1, d)
    f3, loss_v, dx3, dnf = _ffn_out_loss(s3, w2o, x2, ada9, 8, 0.5, tgt, norm_final.reshape(1, d), "ffn2_out_loss")

    dx2, r2i, sum2o, (dsh3, dsc3, dg3, dn3), _, _ = _ffn_backward(
        dx3, x2, h3, gu3, s3, f3, ada9, norm_ffn2, w2i, w2o, (6, 7, 8), core, "ffn2_bwd")
    (dm, dyc, dcb, do, dgo, dg2, dcw, dgn), ((r2o,),) = _mix_bwd_a(dx2, mm, ada9, wmo, cx, o, go, cwt, gla_norm, heads, "mix_bwd_a",
                                                                 plans=[_plan_chip_swap([sum2o])])
    dqk, dvv, dgl, dwg, dbg = _gla_bwd(qk, vv, lg, do, sall, gl, wg, heads, "gla_bwd")
    dx1, dproj, dsh2, dsc2, dnm = _mix_bwd_b(dyc, cx, dcb, dqk, dvv, dgo, dgl, cwt, wmi, x1, dx2, ada9, norm_mix, "mix_bwd_b")
    n_pad = sum(widths)
    tn = n_pad // 5
    dwmi, _ = _tn_matmul(dproj, h2, lambda tk: (tk, tn), lambda sb, k: (k, sb), lambda tk: (tk, d), lambda sb, k: (k, 0),
                         (n_pad, d), (tn, d), lambda sb, k: (sb, 0), 5, "mix_dwin")
    dwmo, _ = _tn_matmul(ycat, dm, lambda tk: (tk, cw + vw), lambda sb, k: (k, 0), lambda tk: (tk, d), lambda sb, k: (k, 0),
                         (cw + vw, d), (cw + vw, d), lambda sb, k: (0, 0), 1, "mix_dwout")
    dwmi = dwmi[:n_proj].reshape(NDEV, mix_cols, d)
    dwmo = dwmo.reshape(NDEV, -1, d)
    dx0, r1i, r1o, (dsh1, dsc1, dg1, dn1), _, ((rmi, rmo),) = _ffn_backward(
        dx1, x0, h1, gu1, s1, f1, ada9, norm_ffn1, w1i, w1o, (0, 1, 2), core, "ffn1_bwd",
        ds_plans=[_plan_sibling_swap([dwmi, dwmo])],
        dwin_plans=lambda moved: [_plan_chip_swap([_pair_add(dwmi, moved[0][0], core, "mix_dwin_add"),
                                                   _pair_add(dwmo, moved[0][1], core, "mix_dwout_add")])], last=True)
    pack = _pack_smalls([dn1, dnm, dn3, dnf, dsh1, dsc1, dg1, dsh2, dsc2, dg2, dsh3, dsc3, dg3], dcw, dbg, dgn, dwg, loss_v, rank, "pack_smalls")
    ((pack_all,),) = _exchange([_plan_all_to_all([pack], True)], "gather_smalls")
    tot = _sum_slots(pack_all, "sum_smalls")

    res = {}
    for nm, recv, w, m, v in (("w_ffn1_out", r1o, w_ffn1_out, m_w_ffn1_out, v_w_ffn1_out), ("w_mix_out", rmo, w_mix_out, m_w_mix_out, v_w_mix_out),
                              ("w_ffn2_out", r2o, w_ffn2_out, m_w_ffn2_out, v_w_ffn2_out)):
        res[nm] = [a[None] for a in _adam_slots(recv, w[0], m[0], v[0], "adam_" + nm)]
    for nm, recv, w, m, v in (("w_ffn1_in", r1i, w_ffn1_in, m_w_ffn1_in, v_w_ffn1_in), ("w_mix_in", rmi, w_mix_in, m_w_mix_in, v_w_mix_in),
                              ("w_ffn2_in", r2i, w_ffn2_in, m_w_ffn2_in, v_w_ffn2_in)):
        res[nm] = [a.T[None] for a in _adam_slots(recv, w[0].T, m[0].T, v[0].T, "adam_" + nm)]

    cols_ada = w_ada.shape[2]
    dada_all = pack_all[:, 4:4 + N_ADA, :].reshape(NDEV, N_ADA * d)
    dada_mine = lax.dynamic_slice_in_dim(dada_all, me * cols_ada, cols_ada, axis=1)
    res["w_ada"] = [a[None] for a in _adam_w_ada(act_all.T, dada_mine, w_ada[0], m_w_ada[0], v_w_ada[0], "adam_w_ada")]

    nv = 4 + N_ADA
    g_small = {
        "b_ada": tot[4:nv].reshape(1, N_ADA * d),
        "norm_ffn1": tot[0:1], "norm_mix": tot[1:2], "norm_ffn2": tot[2:3], "norm_final": tot[3:4],
        "conv_w": lax.dynamic_slice_in_dim(
            jnp.concatenate([tot[nv:nv + 1, 0:cw], tot[nv:nv + 1, cw:2 * cw], tot[nv + 1:nv + 2, 0:cw]], axis=0), me * (cw // NDEV), cw // NDEV, axis=1),
        "w_gk2": lax.dynamic_slice_in_dim(tot[nv + 2:nv + 2 + rank * kw // d].reshape(rank, kw), me * (kw // NDEV), kw // NDEV, axis=1),
        "b_gk": tot[nv + 1:nv + 2, cw:cw + kw],
        "gla_norm": tot[nv + 1:nv + 2, cw + kw:cw + kw + dv],
    }
    small = {"b_ada": (b_ada, m_b_ada, v_b_ada), "norm_ffn1": (norm_ffn1, m_norm_ffn1, v_norm_ffn1), "norm_mix": (norm_mix, m_norm_mix, v_norm_mix),
             "norm_ffn2": (norm_ffn2, m_norm_ffn2, v_norm_ffn2), "norm_final": (norm_final, m_norm_final, v_norm_final),
             "conv_w": (conv_w, m_conv_w, v_conv_w), "w_gk2": (w_gk2, m_w_gk2, v_w_gk2), "b_gk": (b_gk, m_b_gk, v_b_gk),
             "gla_norm": (gla_norm, m_gla_norm, v_gla_norm)}
    names = list(small)
    flat = lambda a: a.reshape(-1, a.shape[-1])
    dl, mo, vo = _adam_smalls([flat(small[n][0]) for n in names], [g_small[n] for n in names],
                              [flat(small[n][1]) for n in names], [flat(small[n][2]) for n in names], "adam_smalls")
    for i, n in enumerate(names):
        shp = small[n][0].shape
        res[n] = [g_small[n].reshape(shp), dl[i].reshape(shp), mo[i].reshape(shp), vo[i].reshape(shp)]

    loss = tot[nv + 2 + rank * kw // d, 0]
    order = ["w_ada", "b_ada", "norm_ffn1", "w_ffn1_in", "w_ffn1_out", "norm_mix", "w_mix_in", "conv_w", "w_gk2", "b_gk", "gla_norm",
             "w_mix_out", "norm_ffn2", "w_ffn2_in", "w_ffn2_out", "norm_final"]
    return (loss, dx0[None], *[res[n][0] for n in order], *[res[n][1] for n in order], *[res[n][2] for n in order], *[res[n][3] for n in order])
```

```python
import collections
import functools

import jax
import jax.numpy as jnp
from jax import lax
from jax.experimental import pallas as pl
from jax.experimental.pallas import tpu as pltpu

F32 = jnp.float32
BF16 = jnp.bfloat16
S = jax.ShapeDtypeStruct

NDEV = 8
EPS = 1e-6
GATE_NORMALIZER = 16.0
CHUNK = 128
N_ADA = 9
ADAM_LR, ADAM_B1, ADAM_B2, ADAM_EPS, ADAM_WD, ADAM_STEP = 0.001, 0.9, 0.999, 1e-08, 0.01, 10
V7X_VMEM_LIMIT = 56 * 1024 * 1024
ROW_TILE = 512
WIDE_ROW_TILE = 1024
K_TILE = 1024
EPILOGUE_ROWS = 256
LANES = 128
BF16_ROWS = 16
PACK_ROWS = 24
ANY = pl.BlockSpec(memory_space=pl.ANY)


def _cp(*sem):
    return pltpu.CompilerParams(dimension_semantics=sem or None, vmem_limit_bytes=V7X_VMEM_LIMIT)


def _dot(a, b):
    return jnp.dot(a, b, preferred_element_type=F32)


def _dot_nt(a, b):
    return lax.dot_general(a, b, (((1,), (1,)), ((), ())), preferred_element_type=F32)


def _dot_tn(a, b):
    return lax.dot_general(a, b, (((0,), (0,)), ((), ())), preferred_element_type=F32)


def _rsum8(a):
    r, c = a.shape
    return jnp.sum(a.reshape(r // 8, 8, c), axis=0)


def _row_tile(r, c):
    for cand in (256, 128, 176, 88, 64, 32, 16, 8):
        if r % cand == 0 and cand * c * 4 <= 1024 * 1024:
            return cand
    return r


def _sigmoid(x):
    return 1.0 / (1.0 + jnp.exp(-x))


def _sigmoid_tanh(x):
    return 0.5 * jnp.tanh(0.5 * x) + 0.5


def _normmod(x, nrm, sc, sh):
    rstd = lax.rsqrt(jnp.mean(x * x, axis=-1, keepdims=True) + EPS)
    xhat = x * rstd
    return xhat, rstd, (xhat * nrm) * (1.0 + sc) + sh


def _normmod_bwd(dh, x, nrm, sc):
    rstd = lax.rsqrt(jnp.mean(x * x, axis=-1, keepdims=True) + EPS)
    xhat = x * rstd
    dxhat = dh * (nrm * (1.0 + sc))
    dx = rstd * (dxhat - xhat * jnp.mean(dxhat * xhat, axis=-1, keepdims=True))
    return dx, dh, dh * (xhat * nrm), dh * ((1.0 + sc) * xhat)


def _place():
    x, y, c = lax.axis_index("x"), lax.axis_index("y"), lax.axis_index("c")
    return x, y, c, 4 * x + 2 * y + c


def _peer(x, y, c, k):
    px = 1 - x if k & 4 else x
    py = 1 - y if k & 2 else y
    pc = 1 - c if k & 1 else c
    return (px, py, pc), 4 * px + 2 * py + pc


def _remote(src, dst, send_sem, recv_sem, peer):
    return pltpu.make_async_remote_copy(src_ref=src, dst_ref=dst, send_sem=send_sem, recv_sem=recv_sem,
                                        device_id=peer, device_id_type=pl.DeviceIdType.MESH)


_Plan = collections.namedtuple("_Plan", "inputs out_shapes sem_shapes start finish aliases", defaults=({},))


def _plan_all_to_all(xs, gather):
    n = len(xs)

    def copies(ins, outs, sems, landed):
        send_sems, recv_sems, local_sems = sems
        x, y, c, me = _place()
        local = [pltpu.make_async_copy(ins[i] if gather else ins[i].at[me], outs[i].at[me], local_sems.at[i]) for i in range(n)]
        remote = []
        for k in range(1, NDEV):
            peer, pid = _peer(x, y, c, k)
            for i in range(n):
                remote.append(_remote(ins[i] if gather else ins[i].at[pid], outs[i].at[pid if landed else me],
                                      send_sems.at[i, k - 1], recv_sems.at[i, k - 1], peer))
        return local, remote

    def start(ins, outs, sems):
        local, remote = copies(ins, outs, sems, False)
        for cp in local + remote:
            cp.start()

    def finish(ins, outs, sems):
        local, remote = copies(ins, outs, sems, True)
        for cp in remote + local:
            cp.wait()

    return _Plan(list(xs), [S((NDEV,) + a.shape, a.dtype) if gather else S(a.shape, a.dtype) for a in xs],
                 [pltpu.SemaphoreType.DMA((n, NDEV - 1)), pltpu.SemaphoreType.DMA((n, NDEV - 1)), pltpu.SemaphoreType.DMA((n,))],
                 start, finish)


def _other_chips(x, y):
    return [(1 - x, y), (x, 1 - y), (1 - x, 1 - y)]


def _plan_gather(xs, rows=None, into=None):
    n = len(xs)

    def copies(ins, outs, sems, rest):
        send_sems, recv_sems, local_sems = sems
        x, y, c, me = _place()
        sib, sib_id = (x, y, 1 - c), 4 * x + 2 * y + 1 - c
        chips = _other_chips(x, y)
        mine = lambda i: ins[i] if rows is None else ins[i].at[pl.ds(*rows)]
        slot_of = lambda i, s: outs[i].at[s] if rows is None else outs[i].at[s, pl.ds(*rows)]
        local = [pltpu.make_async_copy(mine(i), slot_of(i, me), local_sems.at[i]) for i in range(n)]
        first = [_remote(mine(i), slot_of(i, me), send_sems.at[i, 0], recv_sems.at[i, 0], sib) for i in range(n)]
        first += [_remote(mine(i), slot_of(i, me), send_sems.at[i, 1 + j], recv_sems.at[i, 1 + j], (px, py, c))
                  for j, (px, py) in enumerate(chips) for i in range(n)]
        if not rest:
            return local, first
        from_sibling = [_remote(mine(i), slot_of(i, sib_id), send_sems.at[i, 0], recv_sems.at[i, 0], sib) for i in range(n)]
        arrive, forward = [], []
        for j, (px, py) in enumerate(chips):
            s = 4 * px + 2 * py
            arrive.append([_remote(mine(i), slot_of(i, s + c), send_sems.at[i, 1 + j], recv_sems.at[i, 1 + j], (px, py, c)) for i in range(n)])
            forward.append([_remote(slot_of(i, s + c), slot_of(i, s + c), send_sems.at[i, 4 + j], recv_sems.at[i, 4 + j], sib) for i in range(n)])
            from_sibling += [_remote(mine(i), slot_of(i, s + 1 - c), send_sems.at[i, 4 + j], recv_sems.at[i, 4 + j], sib) for i in range(n)]
        return local, first, arrive, forward, from_sibling

    def start(ins, outs, sems):
        local, first = copies(ins, outs, sems, False)
        for cp in local + first:
            cp.start()

    def finish(ins, outs, sems):
        local, first, arrive, forward, from_sibling = copies(ins, outs, sems, True)
        for landed, onward in zip(arrive, forward):
            for cp in landed:
                cp.wait_recv()
            for cp in onward:
                cp.start()
        for cp in from_sibling:
            cp.wait_recv()
        for cp in first + [cp for onward in forward for cp in onward]:
            cp.wait_send()
        for cp in local:
            cp.wait()

    return _Plan(list(xs) + list(into or []), [S((NDEV,) + a.shape, a.dtype) for a in xs],
                 [pltpu.SemaphoreType.DMA((n, NDEV - 1)), pltpu.SemaphoreType.DMA((n, NDEV - 1)), pltpu.SemaphoreType.DMA((n,))],
                 start, finish, {n + i: i for i in range(len(into or []))})


def _plan_sibling_swap(gs):
    n = len(gs)

    def copies(ins, outs, sems):
        send_sems, recv_sems = sems
        x, y, c, _ = _place()
        return [_remote(ins[i].at[2 * j + 1 - c], outs[i].at[j], send_sems.at[i, j], recv_sems.at[i, j], (x, y, 1 - c))
                for i in range(n) for j in range(NDEV // 2)]

    def start(ins, outs, sems):
        for cp in copies(ins, outs, sems):
            cp.start()

    def finish(ins, outs, sems):
        for cp in copies(ins, outs, sems):
            cp.wait()

    return _Plan(list(gs), [S((NDEV // 2,) + a.shape[1:], a.dtype) for a in gs],
                 [pltpu.SemaphoreType.DMA((n, NDEV // 2)), pltpu.SemaphoreType.DMA((n, NDEV // 2))], start, finish)


def _pair_add(g, r1, core, name):
    _, r, c = g.shape
    tr = r if r * c * 2 <= 2 * 1024 * 1024 else _row_tile(r, c)

    def body(core_ref, g_ref, r_ref, o_ref):
        o_ref[...] = (g_ref[...].astype(F32) + r_ref[...].astype(F32)).astype(BF16)

    return pl.pallas_call(
        body, name=name,
        grid_spec=pltpu.PrefetchScalarGridSpec(
            num_scalar_prefetch=1, grid=(NDEV // 2, r // tr),
            in_specs=[pl.BlockSpec((None, tr, c), lambda j, k, core_ref: (2 * j + core_ref[0], k, 0)),
                      pl.BlockSpec((None, tr, c), lambda j, k, core_ref: (j, k, 0))],
            out_specs=pl.BlockSpec((None, tr, c), lambda j, k, core_ref: (j, k, 0))),
        out_shape=S((NDEV // 2, r, c), BF16), compiler_params=_cp("parallel", "parallel"),
    )(core, g, r1)


def _plan_chip_swap(ps):
    n = len(ps)

    def copies(ins, outs, sems, landed):
        send_sems, recv_sems, local_sems = sems
        x, y, c, _ = _place()
        mine = 2 * x + y
        local = [pltpu.make_async_copy(ins[i].at[mine], outs[i].at[mine], local_sems.at[i]) for i in range(n)]
        remote = [_remote(ins[i].at[2 * px + py], outs[i].at[2 * px + py if landed else mine], send_sems.at[i, j], recv_sems.at[i, j], (px, py, c))
                  for j, (px, py) in enumerate(_other_chips(x, y)) for i in range(n)]
        return local, remote

    def start(ins, outs, sems):
        local, remote = copies(ins, outs, sems, False)
        for cp in local + remote:
            cp.start()

    def finish(ins, outs, sems):
        local, remote = copies(ins, outs, sems, True)
        for cp in remote + local:
            cp.wait()

    return _Plan(list(ps), [S(a.shape, a.dtype) for a in ps],
                 [pltpu.SemaphoreType.DMA((n, 3)), pltpu.SemaphoreType.DMA((n, 3)), pltpu.SemaphoreType.DMA((n,))], start, finish)


def _pcall(body, name, args, in_specs, out_shape, out_specs, grid=(), scratch_shapes=(), sem=(), plans=()):
    n_in, n_out, n_scr = len(args), len(out_shape), len(scratch_shapes)
    counts = [(len(p.inputs), len(p.out_shapes), len(p.sem_shapes)) for p in plans]
    c_args = [a for p in plans for a in p.inputs]
    c_outs = [s for p in plans for s in p.out_shapes]
    c_sems = [s for p in plans for s in p.sem_shapes]

    def wrapped(*refs):
        cuts = [n_in, len(c_args), n_out, len(c_outs), n_scr, len(c_sems)]
        ins, c_in, outs, c_out, scr, c_sem = [refs[sum(cuts[:k]):sum(cuts[:k + 1])] for k in range(6)]

        def halves(which):
            a = b = s = 0
            for p, (na, nb, ns) in zip(plans, counts):
                getattr(p, which)(c_in[a:a + na], c_out[b:b + nb], c_sem[s:s + ns])
                a, b, s = a + na, b + nb, s + ns

        if not plans:
            body(*ins, *outs, *scr)
        elif not grid:
            halves("start")
            body(*ins, *outs, *scr)
            halves("finish")
        else:
            first = functools.reduce(jnp.logical_and, [pl.program_id(a) == 0 for a in range(len(grid))])
            last = functools.reduce(jnp.logical_and, [pl.program_id(a) == grid[a] - 1 for a in range(len(grid))])
            pl.when(first)(lambda: halves("start"))
            body(*ins, *outs, *scr)
            pl.when(last)(lambda: halves("finish"))

    aliases, a, b = {}, n_in, n_out
    for p, (na, nb, _) in zip(plans, counts):
        aliases.update({a + k: b + v for k, v in p.aliases.items()})
        a, b = a + na, b + nb
    res = pl.pallas_call(
        wrapped, name=name, grid=grid, in_specs=list(in_specs) + [ANY] * len(c_args),
        out_shape=list(out_shape) + c_outs, out_specs=list(out_specs) + [ANY] * len(c_outs),
        scratch_shapes=list(scratch_shapes) + c_sems, input_output_aliases=aliases,
        compiler_params=_cp(*(("arbitrary",) * len(grid) if plans else sem)),
    )(*args, *c_args)
    c_res, b = [], n_out
    for _, nb, _ in counts:
        c_res.append(res[b:b + nb])
        b += nb
    return res[:n_out], c_res


def _exchange(plans, name):
    return _pcall(lambda: None, name, [], [], [], [], plans=plans)[1]


def _ada_rows(c, w_ada, b_ada, name, plans=()):
    d, cols = c.shape[1], w_ada.shape[1]
    gather_c = _plan_all_to_all([c], True)
    gather_p = _plan_all_to_all([S((NDEV, cols), F32)], True)
    n_sem = len(gather_c.sem_shapes)

    def body(c_ref, w_ref, b_ref, ada_ref, act_ref, c_all, p_mine, p_all, *sems):
        gather_c.start([c_ref], [c_all], sems[:n_sem])
        gather_c.finish([c_ref], [c_all], sems[:n_sem])
        for s in range(NDEV):
            cc = c_all[s]
            act_ref[s:s + 1, :] = cc * _sigmoid(cc)
        p_mine[...] = _dot(act_ref[...].astype(BF16), w_ref[...].astype(BF16))
        gather_p.start([p_mine], [p_all], sems[n_sem:])
        gather_p.finish([p_mine], [p_all], sems[n_sem:])
        me = _place()[3]
        for s in range(NDEV):
            ada_ref[:, s * cols:(s + 1) * cols] = p_all[s, pl.ds(me, 1), :] + b_ref[:, s * cols:(s + 1) * cols]

    whole = pl.BlockSpec(memory_space=pltpu.VMEM)
    return _pcall(body, name, [c, w_ada, b_ada], [whole] * 3, [S((1, NDEV * cols), F32), S((NDEV, d), F32)], [whole] * 2,
                  scratch_shapes=[pltpu.VMEM((NDEV,) + c.shape, F32), pltpu.VMEM((NDEV, cols), F32), pltpu.VMEM((NDEV, NDEV, cols), F32)]
                  + gather_c.sem_shapes + gather_p.sem_shapes, plans=plans)


def _ffn_in(x, ada9, nrm, w_in, sh_row, sc_row, name, plans=()):
    t, d = x.shape
    nb, bw = w_in.shape[0] // 2, w_in.shape[1]
    tm = min(WIDE_ROW_TILE, t)

    def body(x_ref, ada_ref, n_ref, wg_ref, wu_ref, h_ref, gu_ref, s_ref):
        @pl.when(pl.program_id(1) == 0)
        def _():
            _, _, h = _normmod(x_ref[...], n_ref[...], ada_ref[sc_row:sc_row + 1, :], ada_ref[sh_row:sh_row + 1, :])
            h_ref[...] = h.astype(BF16)

        h = h_ref[...]
        g = _dot_nt(h, wg_ref[...])
        u = _dot_nt(h, wu_ref[...])
        gu_ref[0] = g.astype(BF16)
        gu_ref[1] = u.astype(BF16)
        s_ref[...] = (g * _sigmoid(g) * u).astype(BF16)

    return _pcall(
        body, name, [x, ada9, nrm, w_in, w_in], grid=(t // tm, nb),
        in_specs=[pl.BlockSpec((tm, d), lambda i, j: (i, 0)), pl.BlockSpec((N_ADA, d), lambda i, j: (0, 0)),
                  pl.BlockSpec((1, d), lambda i, j: (0, 0)),
                  pl.BlockSpec((None, bw, d), lambda i, j: (j, 0, 0)), pl.BlockSpec((None, bw, d), lambda i, j: (j + nb, 0, 0))],
        out_shape=[S((t, d), BF16), S((2, nb, t, bw), BF16), S((nb, t, bw), BF16)],
        out_specs=[pl.BlockSpec((tm, d), lambda i, j: (i, 0)), pl.BlockSpec((2, None, tm, bw), lambda i, j: (0, j, i, 0)),
                   pl.BlockSpec((None, tm, bw), lambda i, j: (j, i, 0))],
        sem=("parallel", "arbitrary"), plans=plans)


def _ffn_out(s, w_out, x, ada9, g_row, res_scale, name, plans=()):
    nb, t, bw = s.shape
    d = x.shape[1]
    tm = min(ROW_TILE, t)

    def body(s_ref, w_ref, x_ref, ada_ref, xo_ref, f_ref):
        acc = _dot(s_ref[0], w_ref[0])
        for b in range(1, nb):
            acc = acc + _dot(s_ref[b], w_ref[b])
        f_ref[...] = acc.astype(BF16)
        xo_ref[...] = x_ref[...] + (res_scale * ada_ref[g_row:g_row + 1, :]) * acc

    return _pcall(
        body, name, [s, w_out, x, ada9], grid=(t // tm,),
        in_specs=[pl.BlockSpec((nb, tm, bw), lambda i: (0, i, 0)), pl.BlockSpec((nb, bw, d), lambda i: (0, 0, 0)),
                  pl.BlockSpec((tm, d), lambda i: (i, 0)), pl.BlockSpec((N_ADA, d), lambda i: (0, 0))],
        out_shape=[S((t, d), F32), S((t, d), BF16)],
        out_specs=[pl.BlockSpec((tm, d), lambda i: (i, 0)), pl.BlockSpec((tm, d), lambda i: (i, 0))],
        sem=("parallel",), plans=plans)


def _ffn_bwd_ds(dxo, f, ada9, w_out, gu, g_row, res_scale, name, plans=()):
    t, d = dxo.shape
    nb, bw = w_out.shape[0], w_out.shape[1]
    tm = min(WIDE_ROW_TILE, t)

    def body(dxo_ref, f_ref, ada_ref, w_ref, gu_ref, df_ref, da_ref, dg_ref):
        i, j = pl.program_id(0), pl.program_id(1)

        @pl.when((i == 0) & (j == 0))
        def _():
            dg_ref[...] = jnp.zeros_like(dg_ref)

        @pl.when(j == 0)
        def _():
            dxo_t = dxo_ref[...]
            df_ref[...] = ((res_scale * ada_ref[g_row:g_row + 1, :]) * dxo_t).astype(BF16)
            dg_ref[...] += res_scale * _rsum8(dxo_t * f_ref[...].astype(F32))

        ds = _dot_nt(df_ref[...], w_ref[...])
        g = gu_ref[0].astype(F32)
        u = gu_ref[1].astype(F32)
        sg = _sigmoid_tanh(g)
        da_ref[0] = (ds * u * (sg * (1.0 + g * (1.0 - sg)))).astype(BF16)
        da_ref[1] = (ds * (g * sg)).astype(BF16)

    return _pcall(
        body, name, [dxo, f, ada9, w_out, gu], grid=(t // tm, nb),
        in_specs=[pl.BlockSpec((tm, d), lambda i, j: (i, 0)), pl.BlockSpec((tm, d), lambda i, j: (i, 0)),
                  pl.BlockSpec((N_ADA, d), lambda i, j: (0, 0)), pl.BlockSpec((None, bw, d), lambda i, j: (j, 0, 0)),
                  pl.BlockSpec((2, None, tm, bw), lambda i, j: (0, j, i, 0))],
        out_shape=[S((t, d), BF16), S((2, nb, t, bw), BF16), S((8, d), F32)],
        out_specs=[pl.BlockSpec((tm, d), lambda i, j: (i, 0)), pl.BlockSpec((2, None, tm, bw), lambda i, j: (0, j, i, 0)),
                   pl.BlockSpec((8, d), lambda i, j: (0, 0))],
        sem=("arbitrary", "arbitrary"), plans=plans)


def _ffn_bwd_dh(da, w_in, x, dxo, ada9, nrm, sh_row, sc_row, name, plans=(), tiles=None):
    t, d = x.shape
    nb, bw = w_in.shape[0] // 2, w_in.shape[1]
    tm = min(WIDE_ROW_TILE, t)
    i0, ni = (0, t // tm) if tiles is None else tiles

    def body(da_ref, wg_ref, wu_ref, x_ref, dxo_ref, ada_ref, n_ref, dx_ref, dsh_ref, dsc_ref, dn_ref, acc_ref):
        i, j = pl.program_id(0), pl.program_id(1)

        @pl.when((i == 0) & (j == 0))
        def _():
            dsh_ref[...] = jnp.zeros_like(dsh_ref)
            dsc_ref[...] = jnp.zeros_like(dsc_ref)
            dn_ref[...] = jnp.zeros_like(dn_ref)

        part = _dot(da_ref[0], wg_ref[...]) + _dot(da_ref[1], wu_ref[...])

        @pl.when(j == 0)
        def _():
            acc_ref[...] = part

        @pl.when(j > 0)
        def _():
            acc_ref[...] += part

        @pl.when(j == nb - 1)
        def _():
            for r0 in range(0, tm, min(EPILOGUE_ROWS, tm)):
                rows = slice(r0, r0 + min(EPILOGUE_ROWS, tm))
                dx, tsh, tsc, tn = _normmod_bwd(acc_ref[rows, :], x_ref[rows, :], n_ref[...], ada_ref[sc_row:sc_row + 1, :])
                dx_ref[rows, :] = dxo_ref[rows, :] + dx
                dsh_ref[...] += _rsum8(tsh)
                dsc_ref[...] += _rsum8(tsc)
                dn_ref[...] += _rsum8(tn)

    vec = pl.BlockSpec((8, d), lambda i, j: (0, 0))
    return _pcall(
        body, name, [da, w_in, w_in, x, dxo, ada9, nrm], grid=(ni, nb),
        in_specs=[pl.BlockSpec((2, None, tm, bw), lambda i, j: (0, j, i + i0, 0)),
                  pl.BlockSpec((None, bw, d), lambda i, j: (j, 0, 0)), pl.BlockSpec((None, bw, d), lambda i, j: (j + nb, 0, 0)),
                  pl.BlockSpec((tm, d), lambda i, j: (i + i0, 0)), pl.BlockSpec((tm, d), lambda i, j: (i + i0, 0)),
                  pl.BlockSpec((N_ADA, d), lambda i, j: (0, 0)), pl.BlockSpec((1, d), lambda i, j: (0, 0))],
        out_shape=[S((ni * tm, d), F32), S((8, d), F32), S((8, d), F32), S((8, d), F32)],
        out_specs=[pl.BlockSpec((tm, d), lambda i, j: (i, 0)), vec, vec, vec],
        scratch_shapes=[pltpu.VMEM((tm, d), F32)],
        sem=("arbitrary", "arbitrary"), plans=plans)


def _tn_matmul(a, b, a_block, a_map, b_block, b_map, out_shape, out_block, out_map, nblk, name, plans=()):
    t = a.shape[-2]
    tk = min(K_TILE, t)
    nk = t // tk

    def body(a_ref, b_ref, o_ref, acc_ref):
        k = pl.program_id(1)
        for q in (range(a_ref.shape[0]) if len(a_ref.shape) == 3 else [Ellipsis]):
            part = _dot_tn(a_ref[q], b_ref[...])

            @pl.when(k == 0)
            def _():
                acc_ref[q] = part

            @pl.when(k > 0)
            def _():
                acc_ref[q] += part

        @pl.when(k == nk - 1)
        def _():
            o_ref[...] = acc_ref[...].astype(BF16)

    (out,), moved = _pcall(
        body, name, [a, b], grid=(nblk, nk),
        in_specs=[pl.BlockSpec(a_block(tk), a_map), pl.BlockSpec(b_block(tk), b_map)],
        out_shape=[S(out_shape, BF16)], out_specs=[pl.BlockSpec(out_block, out_map)],
        scratch_shapes=[pltpu.VMEM(tuple(n for n in out_block if n is not None), F32)],
        sem=("parallel", "arbitrary"), plans=plans)
    return out, moved


def _ffn_backward(dxo, x_in, h, gu, s, f, ada9, nrm, w_in, w_out, rows, core, name, ds_plans=(), dwin_plans=(), last=False):
    sh_row, sc_row, g_row = rows
    nb, t, bw = s.shape
    d = x_in.shape[1]
    (df, da, dg), ds_moved = _ffn_bwd_ds(dxo, f, ada9, w_out, gu, g_row, 0.5, name + "_ds", plans=ds_plans)
    dw_in, dwin_moved = _tn_matmul(
        da.reshape(2 * nb, t, bw), h, lambda tk: (2, tk, bw), lambda sb, k: (sb, k, 0), lambda tk: (tk, d), lambda sb, k: (k, 0),
        (2 * nb, bw, d), (2, bw, d), lambda sb, k: (sb, 0, 0), nb, name + "_dwin",
        plans=dwin_plans(ds_moved) if callable(dwin_plans) else dwin_plans)
    dw_out, ((half_in,),) = _tn_matmul(
        s, df, lambda tk: (None, tk, bw), lambda sb, k: (sb, k, 0), lambda tk: (tk, d), lambda sb, k: (k, 0),
        (nb, bw, d), (None, bw, d), lambda sb, k: (sb, 0, 0), nb, name + "_dwout", plans=[_plan_sibling_swap([dw_in])])
    dw_out = dw_out.reshape(NDEV, -1, d)
    sum_in = _pair_add(dw_in, half_in, core, name + "_dwin_add")
    both = [_plan_chip_swap([sum_in]), _plan_sibling_swap([dw_out])]
    if not last or t // min(WIDE_ROW_TILE, t) < 2:
        (dx, dsh, dsc, dn), ((recv_in,), (half_out,)) = _ffn_bwd_dh(da, w_in, x_in, dxo, ada9, nrm, sh_row, sc_row, name + "_dh", plans=both)
        out = _pair_add(dw_out, half_out, core, name + "_dwout_add")
        if last:
            ((out,),) = _exchange([_plan_chip_swap([out])], name + "_dwout_swap")
        return dx, recv_in, out, (dsh, dsc, dg, dn), ds_moved, dwin_moved
    n_tiles = t // min(WIDE_ROW_TILE, t)
    most = n_tiles - max(1, n_tiles // 4)
    (dx_a, *sums_a), ((recv_in,), (half_out,)) = _ffn_bwd_dh(
        da, w_in, x_in, dxo, ada9, nrm, sh_row, sc_row, name + "_dh_a", plans=both, tiles=(0, most))
    sum_out = _pair_add(dw_out, half_out, core, name + "_dwout_add")
    (dx_b, *sums_b), ((recv_out,),) = _ffn_bwd_dh(
        da, w_in, x_in, dxo, ada9, nrm, sh_row, sc_row, name + "_dh_b", plans=[_plan_chip_swap([sum_out])],
        tiles=(most, n_tiles - most))
    dsh, dsc, dn = [jnp.concatenate(p, axis=0) for p in zip(sums_a, sums_b)]
    return jnp.concatenate([dx_a, dx_b], axis=0), recv_in, recv_out, (dsh, dsc, dg, dn), ds_moved, dwin_moved


def _mix_in(x, ada9, nrm, w, widths, dts, name, plans=()):
    t, d = x.shape
    n = w.shape[0]
    tm = min(ROW_TILE, t)
    starts = [sum(widths[:i]) for i in range(len(widths))]

    def body(x_ref, ada_ref, n_ref, w_ref, h_ref, *out_refs):
        _, _, h = _normmod(x_ref[...], n_ref[...], ada_ref[4:5, :], ada_ref[3:4, :])
        hb = h.astype(BF16)
        h_ref[...] = hb
        for o_ref, st, wd in zip(out_refs, starts, widths):
            o_ref[...] = _dot_nt(hb, w_ref[st:st + wd, :]).astype(o_ref.dtype)

    return _pcall(
        body, name, [x, ada9, nrm, w], grid=(t // tm,),
        in_specs=[pl.BlockSpec((tm, d), lambda i: (i, 0)), pl.BlockSpec((N_ADA, d), lambda i: (0, 0)),
                  pl.BlockSpec((1, d), lambda i: (0, 0)), pl.BlockSpec((n, d), lambda i: (0, 0))],
        out_shape=[S((t, d), BF16)] + [S((t, wd), dt) for wd, dt in zip(widths, dts)],
        out_specs=[pl.BlockSpec((tm, d), lambda i: (i, 0))] + [pl.BlockSpec((tm, wd), lambda i: (i, 0)) for wd in widths],
        sem=("parallel",), plans=plans)


def _tri(lower):
    r = lax.broadcasted_iota(jnp.int32, (CHUNK, CHUNK), 0)
    c = lax.broadcasted_iota(jnp.int32, (CHUNK, CHUNK), 1)
    return (r >= c) if lower else (c >= r)


def _dot_01(m, x):
    hi = x.astype(BF16)
    r1 = x - hi.astype(F32)
    mid = r1.astype(BF16)
    lo = (r1 - mid.astype(F32)).astype(BF16)
    return _dot(m, hi) + _dot(m, mid) + _dot(m, lo)


def _gla_chunk_terms(q, k, lg, low01):
    b = _dot_01(low01, lg)
    bl = b[CHUNK - 1:CHUNK, :]
    r = 0.5 * bl
    eb, ebl, em, en = jnp.exp(b), jnp.exp(bl - b), jnp.exp(b - r), jnp.exp(r - b)
    return eb, ebl, em, en, jnp.exp(bl), q * eb, k * ebl, q * em, k * en


def _scores(qm_h, knp, qk1_h):
    r = lax.broadcasted_iota(jnp.int32, (CHUNK, CHUNK), 0)
    c = lax.broadcasted_iota(jnp.int32, (CHUNK, CHUNK), 1)
    p = jnp.where(r > c, _dot_nt(qm_h, knp), 0.0)
    return jnp.where(r == c, jnp.sum(qk1_h, axis=1, keepdims=True), p)


def _gla_fwd(qk, v, gl, wg, bg, heads, name, plans=()):
    t = qk.shape[0]
    kw, vw = qk.shape[1] // 2, v.shape[1]
    dk, dv = kw // heads, vw // heads
    assert dk == 64 and dv == 128 and kw % 128 == 0
    gt = min(ROW_TILE, t)
    nc = gt // CHUNK
    scale = dk ** -0.5

    def body(qk_ref, v_ref, gl_ref, wg_ref, bg_ref, o_ref, lg_ref, sall_ref, st_ref):
        @pl.when(pl.program_id(0) == 0)
        def _():
            st_ref[...] = jnp.zeros_like(st_ref)

        gk = _dot(gl_ref[...].astype(BF16), wg_ref[...]) + bg_ref[...]
        lg_ref[...] = (jnp.minimum(gk, 0.0) - jnp.log(1.0 + jnp.exp(-jnp.abs(gk)))) / GATE_NORMALIZER
        low01 = _tri(True).astype(BF16)
        lane = lax.broadcasted_iota(jnp.int32, (CHUNK, LANES), 1)

        def chunk(ci, carry):
            rows = pl.ds(pl.multiple_of(ci * CHUNK, CHUNK), CHUNK)
            q = qk_ref[rows, 0:kw] * scale
            k = qk_ref[rows, kw:2 * kw]
            qk1 = q.astype(BF16).astype(F32) * k.astype(BF16).astype(F32)
            eb, ebl, em, en, ebl_row, qe, ke, qm, kn = _gla_chunk_terms(q, k, lg_ref[rows, :], low01)
            for h in range(heads):
                lanes = slice(LANES * (h // 2), LANES * (h // 2) + LANES)
                own = (lane < 64) if h % 2 == 0 else (lane >= 64)
                knp = kn[:, lanes].astype(BF16)
                qm_h = jnp.where(own, qm[:, lanes], 0.0).astype(BF16)
                qe_h = jnp.where(own, qe[:, lanes], 0.0).astype(BF16)
                ke_h = jnp.where(own, ke[:, lanes], 0.0).astype(BF16)
                v_h = v_ref[rows, h * dv:(h + 1) * dv]
                st = st_ref[h]
                sall_ref[ci, h] = st
                p = _scores(qm_h, knp, jnp.where(own, qk1[:, lanes], 0.0))
                o_ref[rows, h * dv:(h + 1) * dv] = _dot(p.astype(BF16), v_h) + _dot_nt(qe_h, st.astype(BF16))
                st_ref[h] = st * ebl_row[:, lanes] + _dot_tn(v_h, ke_h)
            return carry

        lax.fori_loop(0, nc, chunk, 0, unroll=True)

    return _pcall(
        body, name, [qk, v, gl, wg, bg], grid=(t // gt,),
        in_specs=[pl.BlockSpec((gt, 2 * kw), lambda i: (i, 0)), pl.BlockSpec((gt, vw), lambda i: (i, 0)),
                  pl.BlockSpec((gt, LANES), lambda i: (i, 0)), pl.BlockSpec((LANES, kw), lambda i: (0, 0)),
                  pl.BlockSpec((1, kw), lambda i: (0, 0))],
        out_shape=[S((t, vw), F32), S((t, kw), F32), S((t // CHUNK, heads, dv, LANES), F32)],
        out_specs=[pl.BlockSpec((gt, vw), lambda i: (i, 0)), pl.BlockSpec((gt, kw), lambda i: (i, 0)),
                   pl.BlockSpec((nc, heads, dv, LANES), lambda i: (i, 0, 0, 0))],
        scratch_shapes=[pltpu.VMEM((heads, dv, LANES), F32)],
        sem=("arbitrary",), plans=plans)


def _gla_bwd(qk, v, lg, do, sall, gl, wg, heads, name):
    t = qk.shape[0]
    kw, vw = qk.shape[1] // 2, v.shape[1]
    dk, dv = kw // heads, vw // heads
    gt = min(ROW_TILE, t)
    nc = gt // CHUNK
    nt = t // gt
    scale = dk ** -0.5

    def body(qk_ref, v_ref, lg_ref, do_ref, sall_ref, gl_ref, wg_ref, dqk_ref, dv_ref, dgl_ref, dwg_ref, dbg_ref, dst_ref, dgk_ref):
        @pl.when(pl.program_id(0) == 0)
        def _():
            dst_ref[...] = jnp.zeros_like(dst_ref)
            dwg_ref[...] = jnp.zeros_like(dwg_ref)
            dbg_ref[...] = jnp.zeros_like(dbg_ref)

        low01 = _tri(True).astype(BF16)
        up01 = _tri(False).astype(BF16)
        causal = _tri(True)
        lane = lax.broadcasted_iota(jnp.int32, (CHUNK, LANES), 1)
        last_row = lax.broadcasted_iota(jnp.int32, (CHUNK, kw), 0) == CHUNK - 1

        def chunk(cj, carry):
            ci = nc - 1 - cj
            rows = pl.ds(pl.multiple_of(ci * CHUNK, CHUNK), CHUNK)
            q = qk_ref[rows, 0:kw] * scale
            k = qk_ref[rows, kw:2 * kw]
            qk1 = q.astype(BF16).astype(F32) * k.astype(BF16).astype(F32)
            lgc = lg_ref[rows, :]
            eb, ebl, em, en, ebl_row, qe, ke, qm, kn = _gla_chunk_terms(q, k, lgc, low01)
            dqe, dqm, dkn, dke, drow = [], [], [], [], []
            for pr in range(kw // LANES):
                lanes = slice(LANES * pr, LANES * pr + LANES)
                knp = kn[:, lanes].astype(BF16)
                parts = []
                for half in range(2):
                    h = 2 * pr + half
                    own = (lane < 64) if half == 0 else (lane >= 64)
                    qm_h = jnp.where(own, qm[:, lanes], 0.0).astype(BF16)
                    qe_h = jnp.where(own, qe[:, lanes], 0.0).astype(BF16)
                    ke_h = jnp.where(own, ke[:, lanes], 0.0).astype(BF16)
                    v_h = v_ref[rows, h * dv:(h + 1) * dv]
                    do_h = do_ref[rows, h * dv:(h + 1) * dv]
                    st = sall_ref[ci, h]
                    dst = dst_ref[h]
                    stb, dstb = st.astype(BF16), dst.astype(BF16)
                    p = _scores(qm_h, knp, jnp.where(own, qk1[:, lanes], 0.0)).astype(BF16)
                    dp = jnp.where(causal, _dot_nt(do_h, v_h), 0.0).astype(BF16)
                    dv_ref[rows, h * dv:(h + 1) * dv] = (_dot_tn(p, do_h) + _dot_nt(ke_h, dstb)).astype(BF16)
                    parts.append((jnp.where(own, _dot(dp, knp), 0.0), _dot_tn(dp, qm_h), _dot(do_h, stb), _dot(v_h, dstb),
                                  jnp.sum(st * dst, axis=0, keepdims=True)))
                    dst_ref[h] = dst * ebl_row[:, lanes] + _dot_tn(do_h, qe_h)
                dqm.append(parts[0][0] + parts[1][0])
                dkn.append(parts[0][1] + parts[1][1])
                dqe.append(parts[0][2] + parts[1][2])
                dke.append(parts[0][3] + parts[1][3])
                drow.append(parts[0][4] + parts[1][4])
            dqm, dkn, dqe, dke, drow = [jnp.concatenate(a, axis=1) for a in (dqm, dkn, dqe, dke, drow)]
            dqk_ref[rows, 0:kw] = ((dqe * eb + dqm * em) * scale).astype(BF16)
            dqk_ref[rows, kw:2 * kw] = (dke * ebl + dkn * en).astype(BF16)
            tke = dke * ke
            db = dqe * qe + dqm * qm - dkn * kn - tke
            dbl = jnp.sum(tke, axis=0, keepdims=True) + drow * ebl_row
            db = db + jnp.where(last_row, dbl, 0.0)
            dlg = _dot_01(up01, db)
            dgk_ref[rows, :] = dlg * ((1.0 - jnp.exp(GATE_NORMALIZER * lgc)) / GATE_NORMALIZER)
            return carry

        lax.fori_loop(0, nc, chunk, 0, unroll=True)
        dgk = dgk_ref[...]
        dgkb = dgk.astype(BF16)
        dgl_ref[...] = _dot_nt(dgkb, wg_ref[...]).astype(BF16)
        dwg_ref[...] += _dot_tn(gl_ref[...].astype(BF16), dgkb)
        dbg_ref[...] += _rsum8(dgk)

    rev = lambda i: (nt - 1 - i, 0)
    return pl.pallas_call(
        body, name=name, grid=(nt,),
        in_specs=[pl.BlockSpec((gt, 2 * kw), rev), pl.BlockSpec((gt, vw), rev), pl.BlockSpec((gt, kw), rev),
                  pl.BlockSpec((gt, vw), rev), pl.BlockSpec((nc, heads, dv, LANES), lambda i: (nt - 1 - i, 0, 0, 0)),
                  pl.BlockSpec((gt, LANES), rev), pl.BlockSpec((LANES, kw), lambda i: (0, 0))],
        out_shape=[S((t, 2 * kw), BF16), S((t, vw), BF16), S((t, LANES), BF16), S((LANES, kw), F32), S((8, kw), F32)],
        out_specs=[pl.BlockSpec((gt, 2 * kw), rev), pl.BlockSpec((gt, vw), rev), pl.BlockSpec((gt, LANES), rev),
                   pl.BlockSpec((LANES, kw), lambda i: (0, 0)), pl.BlockSpec((8, kw), lambda i: (0, 0))],
        scratch_shapes=[pltpu.VMEM((heads, dv, LANES), F32), pltpu.VMEM((gt, kw), F32)],
        compiler_params=_cp("arbitrary"),
    )(qk, v, lg, do, sall, gl, wg)


def _conv_taps(cx_ref, halo_ref, first, cw):
    tm = cx_ref.shape[0]
    u = cx_ref[:, cw:2 * cw].astype(F32) * cx_ref[:, 2 * cw:3 * cw].astype(F32)
    uh = halo_ref[:, cw:2 * cw].astype(F32) * halo_ref[:, 2 * cw:3 * cw].astype(F32)
    uh = jnp.where(first, 0.0, uh)
    before1, before2 = uh[BF16_ROWS - 1:BF16_ROWS, :], uh[BF16_ROWS - 2:BF16_ROWS - 1, :]
    row = lax.broadcasted_iota(jnp.int32, (tm, cw), 0)
    u1 = jnp.where(row == 0, before1, pltpu.roll(u, 1, 0))
    u2 = jnp.where(row == 0, before2, jnp.where(row == 1, before1, pltpu.roll(u, 2, 0)))
    return u, u1, u2


def _head_norm(o_h, gn):
    rstd = lax.rsqrt(jnp.mean(o_h * o_h, axis=-1, keepdims=True) + EPS)
    ohat = o_h * rstd
    return ohat, rstd, ohat * gn


def _mix_out(cx, o, go, conv_w, gn, w_out, x, ada9, heads, name, plans=()):
    t, d = x.shape
    cw, vw = conv_w.shape[1], o.shape[1]
    dv = vw // heads
    tm = min(ROW_TILE, t)

    def body(cx_ref, halo_ref, o_ref, go_ref, cwt_ref, gn_ref, w_ref, x_ref, ada_ref, xo_ref, m_ref, y_ref):
        u, u1, u2 = _conv_taps(cx_ref, halo_ref, pl.program_id(0) == 0, cw)
        yc = cwt_ref[0:1, :] * u2 + cwt_ref[1:2, :] * u1 + cwt_ref[2:3, :] * u
        y_ref[:, 0:cw] = (cx_ref[:, 0:cw].astype(F32) * yc).astype(BF16)
        for h in range(heads):
            cols = slice(h * dv, (h + 1) * dv)
            _, _, on = _head_norm(o_ref[:, cols], gn_ref[...])
            g = go_ref[:, cols].astype(F32)
            y_ref[:, cw + h * dv:cw + (h + 1) * dv] = (on * (g * _sigmoid(g))).astype(BF16)
        m = _dot(y_ref[...], w_ref[...])
        m_ref[...] = m.astype(BF16)
        xo_ref[...] = x_ref[...] + ada_ref[5:6, :] * m

    return _pcall(
        body, name, [cx, cx, o, go, conv_w, gn, w_out, x, ada9], grid=(t // tm,),
        in_specs=[pl.BlockSpec((tm, 3 * cw), lambda i: (i, 0)),
                  pl.BlockSpec((BF16_ROWS, 3 * cw), lambda i: (jnp.maximum(i * (tm // BF16_ROWS) - 1, 0), 0)),
                  pl.BlockSpec((tm, vw), lambda i: (i, 0)), pl.BlockSpec((tm, vw), lambda i: (i, 0)),
                  pl.BlockSpec((3, cw), lambda i: (0, 0)), pl.BlockSpec((1, dv), lambda i: (0, 0)),
                  pl.BlockSpec((cw + vw, d), lambda i: (0, 0)), pl.BlockSpec((tm, d), lambda i: (i, 0)),
                  pl.BlockSpec((N_ADA, d), lambda i: (0, 0))],
        out_shape=[S((t, d), F32), S((t, d), BF16), S((t, cw + vw), BF16)],
        out_specs=[pl.BlockSpec((tm, d), lambda i: (i, 0)), pl.BlockSpec((tm, d), lambda i: (i, 0)),
                   pl.BlockSpec((tm, cw + vw), lambda i: (i, 0))],
        sem=("parallel",), plans=plans)


def _mix_bwd_a(dxo, m, ada9, w_out, cx, o, go, conv_w, gn, heads, name, plans=()):
    t, d = dxo.shape
    cw, vw = conv_w.shape[1], o.shape[1]
    dv = vw // heads
    tm = min(ROW_TILE, t)

    def body(dxo_ref, m_ref, ada_ref, w_ref, cx_ref, halo_ref, o_ref, go_ref, cwt_ref, gn_ref,
             dm_ref, dyc_ref, dcb_ref, do_ref, dgo_ref, dg_ref, dcw_ref, dgn_ref):
        @pl.when(pl.program_id(0) == 0)
        def _():
            dg_ref[...] = jnp.zeros_like(dg_ref)
            dcw_ref[...] = jnp.zeros_like(dcw_ref)
            dgn_ref[...] = jnp.zeros_like(dgn_ref)

        dxo_t = dxo_ref[...]
        dmb = (ada_ref[5:6, :] * dxo_t).astype(BF16)
        dm_ref[...] = dmb
        dg_ref[...] += _rsum8(dxo_t * m_ref[...].astype(F32))
        dy = _dot_nt(dmb, w_ref[...])
        u, u1, u2 = _conv_taps(cx_ref, halo_ref, pl.program_id(0) == 0, cw)
        yc = cwt_ref[0:1, :] * u2 + cwt_ref[1:2, :] * u1 + cwt_ref[2:3, :] * u
        dyv = dy[:, 0:cw]
        dcb_ref[...] = (dyv * yc).astype(BF16)
        dyc = dyv * cx_ref[:, 0:cw].astype(F32)
        dyc_ref[...] = dyc
        dcw_ref[0] += _rsum8(dyc * u2)
        dcw_ref[1] += _rsum8(dyc * u1)
        dcw_ref[2] += _rsum8(dyc * u)
        for h in range(heads):
            cols = slice(h * dv, (h + 1) * dv)
            ohat, rstd, on = _head_norm(o_ref[:, cols], gn_ref[...])
            g = go_ref[:, cols].astype(F32)
            sg = _sigmoid_tanh(g)
            dyg = dy[:, cw + h * dv:cw + (h + 1) * dv]
            dgo_ref[:, cols] = (dyg * on * (sg * (1.0 + g * (1.0 - sg)))).astype(BF16)
            don = dyg * (g * sg)
            dgn_ref[...] += _rsum8(don * ohat)
            tt = don * gn_ref[...]
            do_ref[:, cols] = (rstd * (tt - ohat * jnp.mean(tt * ohat, axis=-1, keepdims=True))).astype(BF16)

    return _pcall(
        body, name, [dxo, m, ada9, w_out, cx, cx, o, go, conv_w, gn], grid=(t // tm,),
        in_specs=[pl.BlockSpec((tm, d), lambda i: (i, 0)), pl.BlockSpec((tm, d), lambda i: (i, 0)),
                  pl.BlockSpec((N_ADA, d), lambda i: (0, 0)), pl.BlockSpec((cw + vw, d), lambda i: (0, 0)),
                  pl.BlockSpec((tm, 3 * cw), lambda i: (i, 0)),
                  pl.BlockSpec((BF16_ROWS, 3 * cw), lambda i: (jnp.maximum(i * (tm // BF16_ROWS) - 1, 0), 0)),
                  pl.BlockSpec((tm, vw), lambda i: (i, 0)), pl.BlockSpec((tm, vw), lambda i: (i, 0)),
                  pl.BlockSpec((3, cw), lambda i: (0, 0)), pl.BlockSpec((1, dv), lambda i: (0, 0))],
        out_shape=[S((t, d), BF16), S((t, cw), F32), S((t, cw), BF16), S((t, vw), BF16), S((t, vw), BF16),
                   S((8, d), F32), S((3, 8, cw), F32), S((8, dv), F32)],
        out_specs=[pl.BlockSpec((tm, d), lambda i: (i, 0)), pl.BlockSpec((tm, cw), lambda i: (i, 0)),
                   pl.BlockSpec((tm, cw), lambda i: (i, 0)), pl.BlockSpec((tm, vw), lambda i: (i, 0)),
                   pl.BlockSpec((tm, vw), lambda i: (i, 0)), pl.BlockSpec((8, d), lambda i: (0, 0)),
                   pl.BlockSpec((3, 8, cw), lambda i: (0, 0, 0)), pl.BlockSpec((8, dv), lambda i: (0, 0))],
        sem=("arbitrary",), plans=plans)


def _mix_bwd_b(dyc, cx, dcb, dqk, dvv, dgo, dgl, conv_w, w, x, dxo, ada9, nrm, name):
    t, d = x.shape
    cw = conv_w.shape[1]
    n = w.shape[0]
    tm = min(ROW_TILE, t)
    nt = t // tm
    pieces = [dcb.shape[1], cw, cw, dqk.shape[1], dvv.shape[1], dgo.shape[1], dgl.shape[1]]
    assert sum(pieces) == n

    def body(dyc_ref, nxt_ref, cx_ref, dcb_ref, dqk_ref, dv_ref, dgo_ref, dgl_ref, cwt_ref, w_ref, x_ref, dxo_ref, ada_ref, n_ref,
             dx_ref, dp_ref, dsh_ref, dsc_ref, dn_ref):
        i = pl.program_id(0)

        @pl.when(i == 0)
        def _():
            dsh_ref[...] = jnp.zeros_like(dsh_ref)
            dsc_ref[...] = jnp.zeros_like(dsc_ref)
            dn_ref[...] = jnp.zeros_like(dn_ref)

        dyc_t = dyc_ref[...]
        nxt = jnp.where(i == nt - 1, 0.0, nxt_ref[...])
        row = lax.broadcasted_iota(jnp.int32, (tm, cw), 0)
        d1 = jnp.where(row == tm - 1, nxt[0:1, :], pltpu.roll(dyc_t, tm - 1, 0))
        d2 = jnp.where(row == tm - 2, nxt[0:1, :], jnp.where(row == tm - 1, nxt[1:2, :], pltpu.roll(dyc_t, tm - 2, 0)))
        du = cwt_ref[2:3, :] * dyc_t + cwt_ref[1:2, :] * d1 + cwt_ref[0:1, :] * d2
        c0 = 0
        dp_ref[:, c0:c0 + cw] = dcb_ref[...]
        dp_ref[:, cw:2 * cw] = (du * cx_ref[:, 2 * cw:3 * cw].astype(F32)).astype(BF16)
        dp_ref[:, 2 * cw:3 * cw] = (du * cx_ref[:, cw:2 * cw].astype(F32)).astype(BF16)
        c0 = 3 * cw
        for ref in (dqk_ref, dv_ref, dgo_ref, dgl_ref):
            wd = ref.shape[1]
            dp_ref[:, c0:c0 + wd] = ref[...]
            c0 += wd
        dh = _dot(dp_ref[...], w_ref[...])
        dx, tsh, tsc, tn = _normmod_bwd(dh, x_ref[...], n_ref[...], ada_ref[4:5, :])
        dx_ref[...] = dxo_ref[...] + dx
        dsh_ref[...] += _rsum8(tsh)
        dsc_ref[...] += _rsum8(tsc)
        dn_ref[...] += _rsum8(tn)

    row_spec = lambda wd: pl.BlockSpec((tm, wd), lambda i: (i, 0))
    vec = pl.BlockSpec((8, d), lambda i: (0, 0))
    return pl.pallas_call(
        body, name=name, grid=(nt,),
        in_specs=[row_spec(cw), pl.BlockSpec((8, cw), lambda i: (jnp.minimum((i + 1) * (tm // 8), t // 8 - 1), 0)),
                  row_spec(3 * cw), row_spec(cw), row_spec(dqk.shape[1]), row_spec(dvv.shape[1]), row_spec(dgo.shape[1]),
                  row_spec(dgl.shape[1]), pl.BlockSpec((3, cw), lambda i: (0, 0)), pl.BlockSpec((n, d), lambda i: (0, 0)),
                  row_spec(d), row_spec(d), pl.BlockSpec((N_ADA, d), lambda i: (0, 0)), pl.BlockSpec((1, d), lambda i: (0, 0))],
        out_shape=[S((t, d), F32), S((t, n), BF16), S((8, d), F32), S((8, d), F32), S((8, d), F32)],
        out_specs=[row_spec(d), row_spec(n), vec, vec, vec],
        compiler_params=_cp("arbitrary"),
    )(dyc, dyc, cx, dcb, dqk, dvv, dgo, dgl, conv_w, w, x, dxo, ada9, nrm)


def _ffn_out_loss(s, w_out, x, ada9, g_row, res_scale, target, nrm, name):
    nb, t, bw = s.shape
    d = x.shape[1]
    tm = min(ROW_TILE, t)
    nt = t // tm

    def body(s_ref, w_ref, x_ref, ada_ref, tg_ref, n_ref, f_ref, loss_ref, dx_ref, dn_ref, acc_ref):
        i = pl.program_id(0)

        @pl.when(i == 0)
        def _():
            acc_ref[...] = jnp.zeros_like(acc_ref)
            dn_ref[...] = jnp.zeros_like(dn_ref)

        f = _dot(s_ref[0], w_ref[0])
        for b in range(1, nb):
            f = f + _dot(s_ref[b], w_ref[b])
        f_ref[...] = f.astype(BF16)
        xt = x_ref[...] + (res_scale * ada_ref[g_row:g_row + 1, :]) * f
        rstd = lax.rsqrt(jnp.mean(xt * xt, axis=-1, keepdims=True) + EPS)
        xhat = xt * rstd
        err = xhat * n_ref[...] - tg_ref[...]
        acc_ref[...] += _rsum8(err * err)
        dy = err * (1.0 / d)
        dn_ref[...] += _rsum8(dy * xhat)
        dxhat = dy * n_ref[...]
        dx_ref[...] = rstd * (dxhat - xhat * jnp.mean(dxhat * xhat, axis=-1, keepdims=True))

        @pl.when(i == nt - 1)
        def _():
            loss_ref[...] = jnp.full(loss_ref.shape, (0.5 / d) * jnp.sum(acc_ref[...]), F32)

    return pl.pallas_call(
        body, name=name, grid=(nt,),
        in_specs=[pl.BlockSpec((nb, tm, bw), lambda i: (0, i, 0)), pl.BlockSpec((nb, bw, d), lambda i: (0, 0, 0)),
                  pl.BlockSpec((tm, d), lambda i: (i, 0)), pl.BlockSpec((N_ADA, d), lambda i: (0, 0)),
                  pl.BlockSpec((tm, d), lambda i: (i, 0)), pl.BlockSpec((1, d), lambda i: (0, 0))],
        out_shape=[S((t, d), BF16), S((1, LANES), F32), S((t, d), F32), S((8, d), F32)],
        out_specs=[pl.BlockSpec((tm, d), lambda i: (i, 0)), pl.BlockSpec((1, LANES), lambda i: (0, 0)),
                   pl.BlockSpec((tm, d), lambda i: (i, 0)), pl.BlockSpec((8, d), lambda i: (0, 0))],
        scratch_shapes=[pltpu.VMEM((8, d), F32)],
        compiler_params=_cp("arbitrary"),
    )(s, w_out, x, ada9, target, nrm)


def _pack_smalls(vec_parts, dcw, dbg, dgn, dwg, loss_v, rank, name):
    d = vec_parts[0].shape[1]
    cw, kw, dv = dcw.shape[2], dbg.shape[1], dgn.shape[1]
    nv = len(vec_parts)
    loss_row = nv + 2 + rank * kw // d
    assert 2 * cw == d and cw + kw + dv <= d and (rank * kw) % d == 0 and loss_row < PACK_ROWS
    per_row = d // kw

    def body(*refs):
        vrefs, (dcw_ref, dbg_ref, dgn_ref, dwg_ref, loss_ref, o_ref) = refs[:nv], refs[nv:]
        o_ref[...] = jnp.zeros_like(o_ref)
        o_ref[loss_row:loss_row + 1, 0:loss_ref.shape[1]] = loss_ref[...]
        for r, ref in enumerate(vrefs):
            o_ref[r:r + 1, :] = jnp.sum(ref[...], axis=0, keepdims=True)
        o_ref[nv:nv + 1, 0:cw] = jnp.sum(dcw_ref[0], axis=0, keepdims=True)
        o_ref[nv:nv + 1, cw:2 * cw] = jnp.sum(dcw_ref[1], axis=0, keepdims=True)
        o_ref[nv + 1:nv + 2, 0:cw] = jnp.sum(dcw_ref[2], axis=0, keepdims=True)
        o_ref[nv + 1:nv + 2, cw:cw + kw] = jnp.sum(dbg_ref[...], axis=0, keepdims=True)
        o_ref[nv + 1:nv + 2, cw + kw:cw + kw + dv] = jnp.sum(dgn_ref[...], axis=0, keepdims=True)
        for r in range(rank):
            o_ref[nv + 2 + r // per_row:nv + 3 + r // per_row, (r % per_row) * kw:(r % per_row + 1) * kw] = dwg_ref[r:r + 1, :]

    return pl.pallas_call(body, name=name, out_shape=S((PACK_ROWS, d), F32), compiler_params=_cp())(*vec_parts, dcw, dbg, dgn, dwg, loss_v)


def _sum_slots(a, name):
    def body(a_ref, o_ref):
        acc = a_ref[0]
        for s in range(1, NDEV):
            acc = acc + a_ref[s]
        o_ref[...] = acc

    return pl.pallas_call(body, name=name, out_shape=S(a.shape[1:], F32), compiler_params=_cp())(a)


def _adamw(w, g, m, v):
    m = ADAM_B1 * m + (1.0 - ADAM_B1) * g
    v = ADAM_B2 * v + (1.0 - ADAM_B2) * (g * g)
    m_hat = m / (1.0 - ADAM_B1 ** ADAM_STEP)
    v_hat = v / (1.0 - ADAM_B2 ** ADAM_STEP)
    return -ADAM_LR * (m_hat / (jnp.sqrt(v_hat) + ADAM_EPS) + ADAM_WD * w), m, v


def _adam_slots(recv, w, m, v, name):
    r, c = w.shape
    slots = recv.shape[0]
    tr = _row_tile(r, c)

    def body(recv_ref, w_ref, m_ref, v_ref, g_ref, d_ref, mo_ref, vo_ref):
        g = recv_ref[0].astype(F32)
        for s in range(1, slots):
            g = g + recv_ref[s].astype(F32)
        g_ref[...] = g
        d_ref[...], mo_ref[...], vo_ref[...] = _adamw(w_ref[...], g, m_ref[...], v_ref[...])

    blk = pl.BlockSpec((tr, c), lambda i: (i, 0))
    return pl.pallas_call(
        body, name=name, grid=(r // tr,),
        in_specs=[pl.BlockSpec((slots, tr, c), lambda i: (0, i, 0)), blk, blk, blk],
        out_shape=[S((r, c), F32)] * 4, out_specs=[blk] * 4, compiler_params=_cp("parallel"),
    )(recv, w, m, v)


def _adam_w_ada(act_t, dada, w, m, v, name):
    r, c = w.shape
    tr = 128
    nb = act_t.shape[1]

    def body(a_ref, da_ref, w_ref, m_ref, v_ref, g_ref, d_ref, mo_ref, vo_ref):
        g = a_ref[:, 0:1] * da_ref[0:1, :]
        for b in range(1, nb):
            g = g + a_ref[:, b:b + 1] * da_ref[b:b + 1, :]
        g_ref[...] = g
        d_ref[...], mo_ref[...], vo_ref[...] = _adamw(w_ref[...], g, m_ref[...], v_ref[...])

    blk = pl.BlockSpec((tr, c), lambda i: (i, 0))
    return pl.pallas_call(
        body, name=name, grid=(r // tr,),
        in_specs=[pl.BlockSpec((tr, nb), lambda i: (i, 0)), pl.BlockSpec((nb, c), lambda i: (0, 0)), blk, blk, blk],
        out_shape=[S((r, c), F32)] * 4, out_specs=[blk] * 4, compiler_params=_cp("parallel"),
    )(act_t, dada, w, m, v)


def _adam_smalls(ws, gs, ms, vs, name):
    n = len(ws)

    def body(*refs):
        w_r, g_r, m_r, v_r = (refs[k * n:(k + 1) * n] for k in range(4))
        d_o, m_o, v_o = (refs[(4 + k) * n:(5 + k) * n] for k in range(3))
        for i in range(n):
            d_o[i][...], m_o[i][...], v_o[i][...] = _adamw(w_r[i][...], g_r[i][...], m_r[i][...], v_r[i][...])

    shapes = [S(w.shape, F32) for w in ws]
    outs = pl.pallas_call(body, name=name, out_shape=shapes * 3, compiler_params=_cp())(*ws, *gs, *ms, *vs)
    return outs[:n], outs[n:2 * n], outs[2 * n:]


def kernel(x, c, w_ada, b_ada, norm_ffn1, w_ffn1_in, w_ffn1_out, norm_mix, w_mix_in, conv_w, w_gk2, b_gk, gla_norm, w_mix_out, norm_ffn2, w_ffn2_in, w_ffn2_out, norm_final, loss_target, m_w_ada, m_b_ada, m_norm_ffn1, m_w_ffn1_in, m_w_ffn1_out, m_norm_mix, m_w_mix_in, m_conv_w, m_w_gk2, m_b_gk, m_gla_norm, m_w_mix_out, m_norm_ffn2, m_w_ffn2_in, m_w_ffn2_out, m_norm_final, v_w_ada, v_b_ada, v_norm_ffn1, v_w_ffn1_in, v_w_ffn1_out, v_norm_mix, v_w_mix_in, v_conv_w, v_w_gk2, v_b_gk, v_gla_norm, v_w_mix_out, v_norm_ffn2, v_w_ffn2_in, v_w_ffn2_out, v_norm_final):
    t, d = x.shape[1], x.shape[2]
    x0, tgt = x[0], loss_target[0]
    rank, kw = w_gk2.shape[1], w_gk2.shape[2] * NDEV
    cw = conv_w.shape[2] * NDEV
    dv = gla_norm.shape[1]
    vw = d - cw
    heads = vw // dv
    mix_cols = w_mix_in.shape[2]
    widths = [3 * cw, 2 * kw, vw, vw, LANES]
    n_proj = 3 * cw + 2 * kw + 2 * vw + rank
    assert n_proj == mix_cols * NDEV and rank <= LANES
    me = 4 * lax.axis_index("x") + 2 * lax.axis_index("y") + lax.axis_index("c")

    core = lax.axis_index("c").astype(jnp.int32).reshape(1)
    bf = lambda a: a[0].astype(BF16)
    bft = lambda a: a[0].T.astype(BF16)
    nb = NDEV // 2

    (ada_row, act_all), ((w1i, cwt_all, wg_all),) = _ada_rows(
        c, w_ada[0], b_ada, "ada_rows", plans=[_plan_gather([bft(w_ffn1_in), conv_w[0], w_gk2[0]])])
    ada9 = ada_row.reshape(N_ADA, d)
    cwt = cwt_all.transpose(1, 0, 2).reshape(conv_w.shape[1], cw)
    wg = jnp.pad(wg_all.transpose(1, 0, 2).reshape(rank, kw), ((0, LANES - rank), (0, 0))).astype(BF16)

    (h1, gu1, s1), ((w1o, wmi),) = _ffn_in(x0, ada9, norm_ffn1, w1i, 0, 1, "ffn1_in", plans=[_plan_gather([bf(w_ffn1_out), bft(w_mix_in)])])
    w1o = w1o.reshape(nb, -1, d)
    wmi = jnp.pad(wmi.reshape(n_proj, d), ((0, sum(widths) - n_proj), (0, 0)))
    w2i_mine = bft(w_ffn2_in)
    quarter = w2i_mine.shape[0] // 4
    part = lambda k, into=None: _plan_gather([w2i_mine], rows=(k * quarter, quarter), into=into)
    (x1, f1), ((wmo,), (w2i,)) = _ffn_out(s1, w1o, x0, ada9, 2, 0.5, "ffn1_out", plans=[_plan_gather([bf(w_mix_out)]), part(0)])
    wmo = wmo.reshape(cw + vw, d)
    (h2, cx, qk, vv, go, gl), ((w2i,),) = _mix_in(x1, ada9, norm_mix, wmi, widths, [BF16, F32, BF16, BF16, F32], "mix_in",
                                                 plans=[part(1, [w2i])])
    (o, lg, sall), ((w2i,),) = _gla_fwd(qk, vv, gl, wg, b_gk, heads, "gla_fwd", plans=[part(2, [w2i])])
    (x2, mm, ycat), ((w2i,),) = _mix_out(cx, o, go, cwt, gla_norm, wmo, x1, ada9, heads, "mix_out", plans=[part(3, [w2i])])
    (h3, gu3, s3), ((w2o,),) = _ffn_in(x2, ada9, norm_ffn2, w2i, 6, 7, "ffn2_in", plans=[_plan_gather([bf(w_ffn2_out)])])
    w2o = w2o.reshape(nb, -1, d)
    f3, loss_v, dx3, dnf = _ffn_out_loss(s3, w2o, x2, ada9, 8, 0.5, tgt, norm_final.reshape(1, d), "ffn2_out_loss")

    dx2, r2i, sum2o, (dsh3, dsc3, dg3, dn3), _, _ = _ffn_backward(
        dx3, x2, h3, gu3, s3, f3, ada9, norm_ffn2, w2i, w2o, (6, 7, 8), core, "ffn2_bwd")
    (dm, dyc, dcb, do, dgo, dg2, dcw, dgn), ((r2o,),) = _mix_bwd_a(dx2, mm, ada9, wmo, cx, o, go, cwt, gla_norm, heads, "mix_bwd_a",
                                                                 plans=[_plan_chip_swap([sum2o])])
    dqk, dvv, dgl, dwg, dbg = _gla_bwd(qk, vv, lg, do, sall, gl, wg, heads, "gla_bwd")
    dx1, dproj, dsh2, dsc2, dnm = _mix_bwd_b(dyc, cx, dcb, dqk, dvv, dgo, dgl, cwt, wmi, x1, dx2, ada9, norm_mix, "mix_bwd_b")
    n_pad = sum(widths)
    tn = n_pad // 5
    dwmi, _ = _tn_matmul(dproj, h2, lambda tk: (tk, tn), lambda sb, k: (k, sb), lambda tk: (tk, d), lambda sb, k: (k, 0),
                         (n_pad, d), (tn, d), lambda sb, k: (sb, 0), 5, "mix_dwin")
    dwmo, _ = _tn_matmul(ycat, dm, lambda tk: (tk, cw + vw), lambda sb, k: (k, 0), lambda tk: (tk, d), lambda sb, k: (k, 0),
                         (cw + vw, d), (cw + vw, d), lambda sb, k: (0, 0), 1, "mix_dwout")
    dwmi = dwmi[:n_proj].reshape(NDEV, mix_cols, d)
    dwmo = dwmo.reshape(NDEV, -1, d)
    dx0, r1i, r1o, (dsh1, dsc1, dg1, dn1), _, ((rmi, rmo),) = _ffn_backward(
        dx1, x0, h1, gu1, s1, f1, ada9, norm_ffn1, w1i, w1o, (0, 1, 2), core, "ffn1_bwd",
        ds_plans=[_plan_sibling_swap([dwmi, dwmo])],
        dwin_plans=lambda moved: [_plan_chip_swap([_pair_add(dwmi, moved[0][0], core, "mix_dwin_add"),
                                                   _pair_add(dwmo, moved[0][1], core, "mix_dwout_add")])], last=True)
    pack = _pack_smalls([dn1, dnm, dn3, dnf, dsh1, dsc1, dg1, dsh2, dsc2, dg2, dsh3, dsc3, dg3], dcw, dbg, dgn, dwg, loss_v, rank, "pack_smalls")
    ((pack_all,),) = _exchange([_plan_all_to_all([pack], True)], "gather_smalls")
    tot = _sum_slots(pack_all, "sum_smalls")

    res = {}
    for nm, recv, w, m, v in (("w_ffn1_out", r1o, w_ffn1_out, m_w_ffn1_out, v_w_ffn1_out), ("w_mix_out", rmo, w_mix_out, m_w_mix_out, v_w_mix_out),
                              ("w_ffn2_out", r2o, w_ffn2_out, m_w_ffn2_out, v_w_ffn2_out)):
        res[nm] = [a[None] for a in _adam_slots(recv, w[0], m[0], v[0], "adam_" + nm)]
    for nm, recv, w, m, v in (("w_ffn1_in", r1i, w_ffn1_in, m_w_ffn1_in, v_w_ffn1_in), ("w_mix_in", rmi, w_mix_in, m_w_mix_in, v_w_mix_in),
                              ("w_ffn2_in", r2i, w_ffn2_in, m_w_ffn2_in, v_w_ffn2_in)):
        res[nm] = [a.T[None] for a in _adam_slots(recv, w[0].T, m[0].T, v[0].T, "adam_" + nm)]

    cols_ada = w_ada.shape[2]
    dada_all = pack_all[:, 4:4 + N_ADA, :].reshape(NDEV, N_ADA * d)
    dada_mine = lax.dynamic_slice_in_dim(dada_all, me * cols_ada, cols_ada, axis=1)
    res["w_ada"] = [a[None] for a in _adam_w_ada(act_all.T, dada_mine, w_ada[0], m_w_ada[0], v_w_ada[0], "adam_w_ada")]

    nv = 4 + N_ADA
    g_small = {
        "b_ada": tot[4:nv].reshape(1, N_ADA * d),
        "norm_ffn1": tot[0:1], "norm_mix": tot[1:2], "norm_ffn2": tot[2:3], "norm_final": tot[3:4],
        "conv_w": lax.dynamic_slice_in_dim(
            jnp.concatenate([tot[nv:nv + 1, 0:cw], tot[nv:nv + 1, cw:2 * cw], tot[nv + 1:nv + 2, 0:cw]], axis=0), me * (cw // NDEV), cw // NDEV, axis=1),
        "w_gk2": lax.dynamic_slice_in_dim(tot[nv + 2:nv + 2 + rank * kw // d].reshape(rank, kw), me * (kw // NDEV), kw // NDEV, axis=1),
        "b_gk": tot[nv + 1:nv + 2, cw:cw + kw],
        "gla_norm": tot[nv + 1:nv + 2, cw + kw:cw + kw + dv],
    }
    small = {"b_ada": (b_ada, m_b_ada, v_b_ada), "norm_ffn1": (norm_ffn1, m_norm_ffn1, v_norm_ffn1), "norm_mix": (norm_mix, m_norm_mix, v_norm_mix),
             "norm_ffn2": (norm_ffn2, m_norm_ffn2, v_norm_ffn2), "norm_final": (norm_final, m_norm_final, v_norm_final),
             "conv_w": (conv_w, m_conv_w, v_conv_w), "w_gk2": (w_gk2, m_w_gk2, v_w_gk2), "b_gk": (b_gk, m_b_gk, v_b_gk),
             "gla_norm": (gla_norm, m_gla_norm, v_gla_norm)}
    names = list(small)
    flat = lambda a: a.reshape(-1, a.shape[-1])
    dl, mo, vo = _adam_smalls([flat(small[n][0]) for n in names], [g_small[n] for n in names],
                              [flat(small[n][1]) for n in names], [flat(small[n][2]) for n in names], "adam_smalls")
    for i, n in enumerate(names):
        shp = small[n][0].shape
        res[n] = [g_small[n].reshape(shp), dl[i].reshape(shp), mo[i].reshape(shp), vo[i].reshape(shp)]

    loss = tot[nv + 2 + rank * kw // d, 0]
    order = ["w_ada", "b_ada", "norm_ffn1", "w_ffn1_in", "w_ffn1_out", "norm_mix", "w_mix_in", "conv_w", "w_gk2", "b_gk", "gla_norm",
             "w_mix_out", "norm_ffn2", "w_ffn2_in", "w_ffn2_out", "norm_final"]
    return (loss, dx0[None], *[res[n][0] for n in order], *[res[n][1] for n in order], *[res[n][2] for n in order], *[res[n][3] for n in order])
```

```python
import collections
import functools

import jax
import jax.numpy as jnp
from jax import lax
from jax.experimental import pallas as pl
from jax.experimental.pallas import tpu as pltpu

F32 = jnp.float32
BF16 = jnp.bfloat16
S = jax.ShapeDtypeStruct

NDEV = 8
EPS = 1e-6
GATE_NORMALIZER = 16.0
CHUNK = 128
N_ADA = 9
ADAM_LR, ADAM_B1, ADAM_B2, ADAM_EPS, ADAM_WD, ADAM_STEP = 0.001, 0.9, 0.999, 1e-08, 0.01, 10
V7X_VMEM_LIMIT = 56 * 1024 * 1024
ROW_TILE = 512
WIDE_ROW_TILE = 1024
K_TILE = 1024
EPILOGUE_ROWS = 256
LANES = 128
BF16_ROWS = 16
PACK_ROWS = 24
ANY = pl.BlockSpec(memory_space=pl.ANY)


def _cp(*sem):
    return pltpu.CompilerParams(dimension_semantics=sem or None, vmem_limit_bytes=V7X_VMEM_LIMIT)


def _dot(a, b):
    return jnp.dot(a, b, preferred_element_type=F32)


def _dot_nt(a, b):
    return lax.dot_general(a, b, (((1,), (1,)), ((), ())), preferred_element_type=F32)


def _dot_tn(a, b):
    return lax.dot_general(a, b, (((0,), (0,)), ((), ())), preferred_element_type=F32)


def _rsum8(a):
    r, c = a.shape
    return jnp.sum(a.reshape(r // 8, 8, c), axis=0)


def _row_tile(r, c):
    for cand in (256, 128, 176, 88, 64, 32, 16, 8):
        if r % cand == 0 and cand * c * 4 <= 1024 * 1024:
            return cand
    return r


def _sigmoid(x):
    return 1.0 / (1.0 + jnp.exp(-x))


def _sigmoid_tanh(x):
    return 0.5 * jnp.tanh(0.5 * x) + 0.5


def _normmod(x, nrm, sc, sh):
    rstd = lax.rsqrt(jnp.mean(x * x, axis=-1, keepdims=True) + EPS)
    xhat = x * rstd
    return xhat, rstd, (xhat * nrm) * (1.0 + sc) + sh


def _normmod_bwd(dh, x, nrm, sc):
    rstd = lax.rsqrt(jnp.mean(x * x, axis=-1, keepdims=True) + EPS)
    xhat = x * rstd
    dxhat = dh * (nrm * (1.0 + sc))
    dx = rstd * (dxhat - xhat * jnp.mean(dxhat * xhat, axis=-1, keepdims=True))
    return dx, dh, dh * (xhat * nrm), dh * ((1.0 + sc) * xhat)


def _place():
    x, y, c = lax.axis_index("x"), lax.axis_index("y"), lax.axis_index("c")
    return x, y, c, 4 * x + 2 * y + c


def _peer(x, y, c, k):
    px = 1 - x if k & 4 else x
    py = 1 - y if k & 2 else y
    pc = 1 - c if k & 1 else c
    return (px, py, pc), 4 * px + 2 * py + pc


def _remote(src, dst, send_sem, recv_sem, peer):
    return pltpu.make_async_remote_copy(src_ref=src, dst_ref=dst, send_sem=send_sem, recv_sem=recv_sem,
                                        device_id=peer, device_id_type=pl.DeviceIdType.MESH)


_Plan = collections.namedtuple("_Plan", "inputs out_shapes sem_shapes start finish aliases", defaults=({},))


def _plan_all_to_all(xs, gather):
    n = len(xs)

    def copies(ins, outs, sems, landed):
        send_sems, recv_sems, local_sems = sems
        x, y, c, me = _place()
        local = [pltpu.make_async_copy(ins[i] if gather else ins[i].at[me], outs[i].at[me], local_sems.at[i]) for i in range(n)]
        remote = []
        for k in range(1, NDEV):
            peer, pid = _peer(x, y, c, k)
            for i in range(n):
                remote.append(_remote(ins[i] if gather else ins[i].at[pid], outs[i].at[pid if landed else me],
                                      send_sems.at[i, k - 1], recv_sems.at[i, k - 1], peer))
        return local, remote

    def start(ins, outs, sems):
        local, remote = copies(ins, outs, sems, False)
        for cp in local + remote:
            cp.start()

    def finish(ins, outs, sems):
        local, remote = copies(ins, outs, sems, True)
        for cp in remote + local:
            cp.wait()

    return _Plan(list(xs), [S((NDEV,) + a.shape, a.dtype) if gather else S(a.shape, a.dtype) for a in xs],
                 [pltpu.SemaphoreType.DMA((n, NDEV - 1)), pltpu.SemaphoreType.DMA((n, NDEV - 1)), pltpu.SemaphoreType.DMA((n,))],
                 start, finish)


def _other_chips(x, y):
    return [(1 - x, y), (x, 1 - y), (1 - x, 1 - y)]


def _plan_gather(xs, rows=None, into=None):
    n = len(xs)

    def copies(ins, outs, sems, rest):
        send_sems, recv_sems, local_sems = sems
        x, y, c, me = _place()
        sib, sib_id = (x, y, 1 - c), 4 * x + 2 * y + 1 - c
        chips = _other_chips(x, y)
        mine = lambda i: ins[i] if rows is None else ins[i].at[pl.ds(*rows)]
        slot_of = lambda i, s: outs[i].at[s] if rows is None else outs[i].at[s, pl.ds(*rows)]
        local = [pltpu.make_async_copy(mine(i), slot_of(i, me), local_sems.at[i]) for i in range(n)]
        first = [_remote(mine(i), slot_of(i, me), send_sems.at[i, 0], recv_sems.at[i, 0], sib) for i in range(n)]
        first += [_remote(mine(i), slot_of(i, me), send_sems.at[i, 1 + j], recv_sems.at[i, 1 + j], (px, py, c))
                  for j, (px, py) in enumerate(chips) for i in range(n)]
        if not rest:
            return local, first
        from_sibling = [_remote(mine(i), slot_of(i, sib_id), send_sems.at[i, 0], recv_sems.at[i, 0], sib) for i in range(n)]
        arrive, forward = [], []
        for j, (px, py) in enumerate(chips):
            s = 4 * px + 2 * py
            arrive.append([_remote(mine(i), slot_of(i, s + c), send_sems.at[i, 1 + j], recv_sems.at[i, 1 + j], (px, py, c)) for i in range(n)])
            forward.append([_remote(slot_of(i, s + c), slot_of(i, s + c), send_sems.at[i, 4 + j], recv_sems.at[i, 4 + j], sib) for i in range(n)])
            from_sibling += [_remote(mine(i), slot_of(i, s + 1 - c), send_sems.at[i, 4 + j], recv_sems.at[i, 4 + j], sib) for i in range(n)]
        return local, first, arrive, forward, from_sibling

    def start(ins, outs, sems):
        local, first = copies(ins, outs, sems, False)
        for cp in local + first:
            cp.start()

    def finish(ins, outs, sems):
        local, first, arrive, forward, from_sibling = copies(ins, outs, sems, True)
        for landed, onward in zip(arrive, forward):
            for cp in landed:
                cp.wait_recv()
            for cp in onward:
                cp.start()
        for cp in from_sibling:
            cp.wait_recv()
        for cp in first + [cp for onward in forward for cp in onward]:
            cp.wait_send()
        for cp in local:
            cp.wait()

    return _Plan(list(xs) + list(into or []), [S((NDEV,) + a.shape, a.dtype) for a in xs],
                 [pltpu.SemaphoreType.DMA((n, NDEV - 1)), pltpu.SemaphoreType.DMA((n, NDEV - 1)), pltpu.SemaphoreType.DMA((n,))],
                 start, finish, {n + i: i for i in range(len(into or []))})


def _plan_sibling_swap(gs):
    n = len(gs)

    def copies(ins, outs, sems):
        send_sems, recv_sems = sems
        x, y, c, _ = _place()
        return [_remote(ins[i].at[2 * j + 1 - c], outs[i].at[j], send_sems.at[i, j], recv_sems.at[i, j], (x, y, 1 - c))
                for i in range(n) for j in range(NDEV // 2)]

    def start(ins, outs, sems):
        for cp in copies(ins, outs, sems):
            cp.start()

    def finish(ins, outs, sems):
        for cp in copies(ins, outs, sems):
            cp.wait()

    return _Plan(list(gs), [S((NDEV // 2,) + a.shape[1:], a.dtype) for a in gs],
                 [pltpu.SemaphoreType.DMA((n, NDEV // 2)), pltpu.SemaphoreType.DMA((n, NDEV // 2))], start, finish)


def _pair_add(g, r1, core, name):
    _, r, c = g.shape
    tr = r if r * c * 2 <= 2 * 1024 * 1024 else _row_tile(r, c)

    def body(core_ref, g_ref, r_ref, o_ref):
        o_ref[...] = (g_ref[...].astype(F32) + r_ref[...].astype(F32)).astype(BF16)

    return pl.pallas_call(
        body, name=name,
        grid_spec=pltpu.PrefetchScalarGridSpec(
            num_scalar_prefetch=1, grid=(NDEV // 2, r // tr),
            in_specs=[pl.BlockSpec((None, tr, c), lambda j, k, core_ref: (2 * j + core_ref[0], k, 0)),
                      pl.BlockSpec((None, tr, c), lambda j, k, core_ref: (j, k, 0))],
            out_specs=pl.BlockSpec((None, tr, c), lambda j, k, core_ref: (j, k, 0))),
        out_shape=S((NDEV // 2, r, c), BF16), compiler_params=_cp("parallel", "parallel"),
    )(core, g, r1)


def _plan_chip_swap(ps):
    n = len(ps)

    def copies(ins, outs, sems, landed):
        send_sems, recv_sems, local_sems = sems
        x, y, c, _ = _place()
        mine = 2 * x + y
        local = [pltpu.make_async_copy(ins[i].at[mine], outs[i].at[mine], local_sems.at[i]) for i in range(n)]
        remote = [_remote(ins[i].at[2 * px + py], outs[i].at[2 * px + py if landed else mine], send_sems.at[i, j], recv_sems.at[i, j], (px, py, c))
                  for j, (px, py) in enumerate(_other_chips(x, y)) for i in range(n)]
        return local, remote

    def start(ins, outs, sems):
        local, remote = copies(ins, outs, sems, False)
        for cp in local + remote:
            cp.start()

    def finish(ins, outs, sems):
        local, remote = copies(ins, outs, sems, True)
        for cp in remote + local:
            cp.wait()

    return _Plan(list(ps), [S(a.shape, a.dtype) for a in ps],
                 [pltpu.SemaphoreType.DMA((n, 3)), pltpu.SemaphoreType.DMA((n, 3)), pltpu.SemaphoreType.DMA((n,))], start, finish)


def _pcall(body, name, args, in_specs, out_shape, out_specs, grid=(), scratch_shapes=(), sem=(), plans=()):
    n_in, n_out, n_scr = len(args), len(out_shape), len(scratch_shapes)
    counts = [(len(p.inputs), len(p.out_shapes), len(p.sem_shapes)) for p in plans]
    c_args = [a for p in plans for a in p.inputs]
    c_outs = [s for p in plans for s in p.out_shapes]
    c_sems = [s for p in plans for s in p.sem_shapes]

    def wrapped(*refs):
        cuts = [n_in, len(c_args), n_out, len(c_outs), n_scr, len(c_sems)]
        ins, c_in, outs, c_out, scr, c_sem = [refs[sum(cuts[:k]):sum(cuts[:k + 1])] for k in range(6)]

        def halves(which):
            a = b = s = 0
            for p, (na, nb, ns) in zip(plans, counts):
                getattr(p, which)(c_in[a:a + na], c_out[b:b + nb], c_sem[s:s + ns])
                a, b, s = a + na, b + nb, s + ns

        if not plans:
            body(*ins, *outs, *scr)
        elif not grid:
            halves("start")
            body(*ins, *outs, *scr)
            halves("finish")
        else:
            first = functools.reduce(jnp.logical_and, [pl.program_id(a) == 0 for a in range(len(grid))])
            last = functools.reduce(jnp.logical_and, [pl.program_id(a) == grid[a] - 1 for a in range(len(grid))])
            pl.when(first)(lambda: halves("start"))
            body(*ins, *outs, *scr)
            pl.when(last)(lambda: halves("finish"))

    aliases, a, b = {}, n_in, n_out
    for p, (na, nb, _) in zip(plans, counts):
        aliases.update({a + k: b + v for k, v in p.aliases.items()})
        a, b = a + na, b + nb
    res = pl.pallas_call(
        wrapped, name=name, grid=grid, in_specs=list(in_specs) + [ANY] * len(c_args),
        out_shape=list(out_shape) + c_outs, out_specs=list(out_specs) + [ANY] * len(c_outs),
        scratch_shapes=list(scratch_shapes) + c_sems, input_output_aliases=aliases,
        compiler_params=_cp(*(("arbitrary",) * len(grid) if plans else sem)),
    )(*args, *c_args)
    c_res, b = [], n_out
    for _, nb, _ in counts:
        c_res.append(res[b:b + nb])
        b += nb
    return res[:n_out], c_res


def _exchange(plans, name):
    return _pcall(lambda: None, name, [], [], [], [], plans=plans)[1]


def _ada_rows(c, w_ada, b_ada, name, plans=()):
    d, cols = c.shape[1], w_ada.shape[1]
    gather_c = _plan_all_to_all([c], True)
    gather_p = _plan_all_to_all([S((NDEV, cols), F32)], True)
    n_sem = len(gather_c.sem_shapes)

    def body(c_ref, w_ref, b_ref, ada_ref, act_ref, c_all, p_mine, p_all, *sems):
        gather_c.start([c_ref], [c_all], sems[:n_sem])
        gather_c.finish([c_ref], [c_all], sems[:n_sem])
        for s in range(NDEV):
            cc = c_all[s]
            act_ref[s:s + 1, :] = cc * _sigmoid(cc)
        p_mine[...] = _dot(act_ref[...].astype(BF16), w_ref[...].astype(BF16))
        gather_p.start([p_mine], [p_all], sems[n_sem:])
        gather_p.finish([p_mine], [p_all], sems[n_sem:])
        me = _place()[3]
        for s in range(NDEV):
            ada_ref[:, s * cols:(s + 1) * cols] = p_all[s, pl.ds(me, 1), :] + b_ref[:, s * cols:(s + 1) * cols]

    whole = pl.BlockSpec(memory_space=pltpu.VMEM)
    return _pcall(body, name, [c, w_ada, b_ada], [whole] * 3, [S((1, NDEV * cols), F32), S((NDEV, d), F32)], [whole] * 2,
                  scratch_shapes=[pltpu.VMEM((NDEV,) + c.shape, F32), pltpu.VMEM((NDEV, cols), F32), pltpu.VMEM((NDEV, NDEV, cols), F32)]
                  + gather_c.sem_shapes + gather_p.sem_shapes, plans=plans)


def _ffn_in(x, ada9, nrm, w_in, sh_row, sc_row, name, plans=()):
    t, d = x.shape
    nb, bw = w_in.shape[0] // 2, w_in.shape[1]
    tm = min(WIDE_ROW_TILE, t)

    def body(x_ref, ada_ref, n_ref, wg_ref, wu_ref, h_ref, gu_ref, s_ref):
        @pl.when(pl.program_id(1) == 0)
        def _():
            _, _, h = _normmod(x_ref[...], n_ref[...], ada_ref[sc_row:sc_row + 1, :], ada_ref[sh_row:sh_row + 1, :])
            h_ref[...] = h.astype(BF16)

        h = h_ref[...]
        g = _dot_nt(h, wg_ref[...])
        u = _dot_nt(h, wu_ref[...])
        gu_ref[0] = g.astype(BF16)
        gu_ref[1] = u.astype(BF16)
        s_ref[...] = (g * _sigmoid(g) * u).astype(BF16)

    return _pcall(
        body, name, [x, ada9, nrm, w_in, w_in], grid=(t // tm, nb),
        in_specs=[pl.BlockSpec((tm, d), lambda i, j: (i, 0)), pl.BlockSpec((N_ADA, d), lambda i, j: (0, 0)),
                  pl.BlockSpec((1, d), lambda i, j: (0, 0)),
                  pl.BlockSpec((None, bw, d), lambda i, j: (j, 0, 0)), pl.BlockSpec((None, bw, d), lambda i, j: (j + nb, 0, 0))],
        out_shape=[S((t, d), BF16), S((2, nb, t, bw), BF16), S((nb, t, bw), BF16)],
        out_specs=[pl.BlockSpec((tm, d), lambda i, j: (i, 0)), pl.BlockSpec((2, None, tm, bw), lambda i, j: (0, j, i, 0)),
                   pl.BlockSpec((None, tm, bw), lambda i, j: (j, i, 0))],
        sem=("parallel", "arbitrary"), plans=plans)


def _ffn_out(s, w_out, x, ada9, g_row, res_scale, name, plans=()):
    nb, t, bw = s.shape
    d = x.shape[1]
    tm = min(ROW_TILE, t)

    def body(s_ref, w_ref, x_ref, ada_ref, xo_ref, f_ref):
        acc = _dot(s_ref[0], w_ref[0])
        for b in range(1, nb):
            acc = acc + _dot(s_ref[b], w_ref[b])
        f_ref[...] = acc.astype(BF16)
        xo_ref[...] = x_ref[...] + (res_scale * ada_ref[g_row:g_row + 1, :]) * acc

    return _pcall(
        body, name, [s, w_out, x, ada9], grid=(t // tm,),
        in_specs=[pl.BlockSpec((nb, tm, bw), lambda i: (0, i, 0)), pl.BlockSpec((nb, bw, d), lambda i: (0, 0, 0)),
                  pl.BlockSpec((tm, d), lambda i: (i, 0)), pl.BlockSpec((N_ADA, d), lambda i: (0, 0))],
        out_shape=[S((t, d), F32), S((t, d), BF16)],
        out_specs=[pl.BlockSpec((tm, d), lambda i: (i, 0)), pl.BlockSpec((tm, d), lambda i: (i, 0))],
        sem=("parallel",), plans=plans)


def _ffn_bwd_ds(dxo, f, ada9, w_out, gu, g_row, res_scale, name, plans=()):
    t, d = dxo.shape
    nb, bw = w_out.shape[0], w_out.shape[1]
    tm = min(WIDE_ROW_TILE, t)
    ni = t // tm

    def body(dxo_ref, f_ref, ada_ref, w_ref, gu_ref, da_ref, dg_ref, dw_ref, df_ref, acc_ref):
        i, j = pl.program_id(0), pl.program_id(1)

        @pl.when((i == 0) & (j == 0))
        def _():
            dg_ref[...] = jnp.zeros_like(dg_ref)

        @pl.when(j == 0)
        def _():
            dxo_t = dxo_ref[...]
            df_ref[...] = ((res_scale * ada_ref[g_row:g_row + 1, :]) * dxo_t).astype(BF16)
            dg_ref[...] += res_scale * _rsum8(dxo_t * f_ref[...].astype(F32))

        df = df_ref[...]
        ds = _dot_nt(df, w_ref[...])
        g = gu_ref[0].astype(F32)
        u = gu_ref[1].astype(F32)
        sg = _sigmoid_tanh(g)
        silu = g * sg
        da_ref[0] = (ds * u * (sg * (1.0 + g * (1.0 - sg)))).astype(BF16)
        da_ref[1] = (ds * silu).astype(BF16)
        part = _dot_tn((silu * u).astype(BF16), df)

        @pl.when(i == 0)
        def _():
            acc_ref[j] = part

        @pl.when(i > 0)
        def _():
            acc_ref[j] += part

        @pl.when(i == ni - 1)
        def _():
            dw_ref[...] = acc_ref[j].astype(BF16)

    return _pcall(
        body, name, [dxo, f, ada9, w_out, gu], grid=(ni, nb),
        in_specs=[pl.BlockSpec((tm, d), lambda i, j: (i, 0)), pl.BlockSpec((tm, d), lambda i, j: (i, 0)),
                  pl.BlockSpec((N_ADA, d), lambda i, j: (0, 0)), pl.BlockSpec((None, bw, d), lambda i, j: (j, 0, 0)),
                  pl.BlockSpec((2, None, tm, bw), lambda i, j: (0, j, i, 0))],
        out_shape=[S((2, nb, t, bw), BF16), S((8, d), F32), S((nb, bw, d), BF16)],
        out_specs=[pl.BlockSpec((2, None, tm, bw), lambda i, j: (0, j, i, 0)), pl.BlockSpec((8, d), lambda i, j: (0, 0)),
                   pl.BlockSpec((None, bw, d), lambda i, j: (jnp.where(i == ni - 1, j, 0), 0, 0))],
        scratch_shapes=[pltpu.VMEM((tm, d), BF16), pltpu.VMEM((nb, bw, d), F32)],
        sem=("arbitrary", "arbitrary"), plans=plans)


def _ffn_bwd_dh(da, w_in, x, dxo, ada9, nrm, sh_row, sc_row, name, plans=()):
    t, d = x.shape
    nb, bw = w_in.shape[0] // 2, w_in.shape[1]
    tm = min(WIDE_ROW_TILE, t)

    def body(da_ref, wg_ref, wu_ref, x_ref, dxo_ref, ada_ref, n_ref, dx_ref, dsh_ref, dsc_ref, dn_ref, acc_ref):
        i, j = pl.program_id(0), pl.program_id(1)

        @pl.when((i == 0) & (j == 0))
        def _():
            dsh_ref[...] = jnp.zeros_like(dsh_ref)
            dsc_ref[...] = jnp.zeros_like(dsc_ref)
            dn_ref[...] = jnp.zeros_like(dn_ref)

        part = _dot(da_ref[0], wg_ref[...]) + _dot(da_ref[1], wu_ref[...])

        @pl.when(j == 0)
        def _():
            acc_ref[...] = part

        @pl.when(j > 0)
        def _():
            acc_ref[...] += part

        @pl.when(j == nb - 1)
        def _():
            for r0 in range(0, tm, min(EPILOGUE_ROWS, tm)):
                rows = slice(r0, r0 + min(EPILOGUE_ROWS, tm))
                dx, tsh, tsc, tn = _normmod_bwd(acc_ref[rows, :], x_ref[rows, :], n_ref[...], ada_ref[sc_row:sc_row + 1, :])
                dx_ref[rows, :] = dxo_ref[rows, :] + dx
                dsh_ref[...] += _rsum8(tsh)
                dsc_ref[...] += _rsum8(tsc)
                dn_ref[...] += _rsum8(tn)

    vec = pl.BlockSpec((8, d), lambda i, j: (0, 0))
    return _pcall(
        body, name, [da, w_in, w_in, x, dxo, ada9, nrm], grid=(t // tm, nb),
        in_specs=[pl.BlockSpec((2, None, tm, bw), lambda i, j: (0, j, i, 0)),
                  pl.BlockSpec((None, bw, d), lambda i, j: (j, 0, 0)), pl.BlockSpec((None, bw, d), lambda i, j: (j + nb, 0, 0)),
                  pl.BlockSpec((tm, d), lambda i, j: (i, 0)), pl.BlockSpec((tm, d), lambda i, j: (i, 0)),
                  pl.BlockSpec((N_ADA, d), lambda i, j: (0, 0)), pl.BlockSpec((1, d), lambda i, j: (0, 0))],
        out_shape=[S((t, d), F32), S((8, d), F32), S((8, d), F32), S((8, d), F32)],
        out_specs=[pl.BlockSpec((tm, d), lambda i, j: (i, 0)), vec, vec, vec],
        scratch_shapes=[pltpu.VMEM((tm, d), F32)],
        sem=("arbitrary", "arbitrary"), plans=plans)


def _tn_matmul(a, b, a_block, a_map, b_block, b_map, out_shape, out_block, out_map, nblk, name, plans=()):
    t = a.shape[-2]
    tk = min(K_TILE, t)
    nk = t // tk

    def body(a_ref, b_ref, o_ref, acc_ref):
        k = pl.program_id(1)
        for q in (range(a_ref.shape[0]) if len(a_ref.shape) == 3 else [Ellipsis]):
            part = _dot_tn(a_ref[q], b_ref[...])

            @pl.when(k == 0)
            def _():
                acc_ref[q] = part

            @pl.when(k > 0)
            def _():
                acc_ref[q] += part

        @pl.when(k == nk - 1)
        def _():
            o_ref[...] = acc_ref[...].astype(BF16)

    (out,), moved = _pcall(
        body, name, [a, b], grid=(nblk, nk),
        in_specs=[pl.BlockSpec(a_block(tk), a_map), pl.BlockSpec(b_block(tk), b_map)],
        out_shape=[S(out_shape, BF16)], out_specs=[pl.BlockSpec(out_block, out_map)],
        scratch_shapes=[pltpu.VMEM(tuple(n for n in out_block if n is not None), F32)],
        sem=("parallel", "arbitrary"), plans=plans)
    return out, moved


def _ffn_backward(dxo, x_in, h, gu, f, ada9, nrm, w_in, w_out, rows, core, name, ds_plans=(), dwin_plans=()):
    sh_row, sc_row, g_row = rows
    _, nb, t, bw = gu.shape
    d = x_in.shape[1]
    (da, dg, dw_out), ds_moved = _ffn_bwd_ds(dxo, f, ada9, w_out, gu, g_row, 0.5, name + "_ds", plans=ds_plans)
    dw_out = dw_out.reshape(NDEV, -1, d)
    dw_in, ((half_out,), *dwin_moved) = _tn_matmul(
        da.reshape(2 * nb, t, bw), h, lambda tk: (2, tk, bw), lambda sb, k: (sb, k, 0), lambda tk: (tk, d), lambda sb, k: (k, 0),
        (2 * nb, bw, d), (2, bw, d), lambda sb, k: (sb, 0, 0), nb, name + "_dwin",
        plans=[_plan_sibling_swap([dw_out])] + list(dwin_plans(ds_moved) if callable(dwin_plans) else dwin_plans))
    ((half_in,),) = _exchange([_plan_sibling_swap([dw_in])], name + "_dwin_swap")
    sum_in = _pair_add(dw_in, half_in, core, name + "_dwin_add")
    sum_out = _pair_add(dw_out, half_out, core, name + "_dwout_add")
    (dx, dsh, dsc, dn), ((recv_in,),) = _ffn_bwd_dh(da, w_in, x_in, dxo, ada9, nrm, sh_row, sc_row, name + "_dh",
                                                   plans=[_plan_chip_swap([sum_in])])
    return dx, recv_in, sum_out, (dsh, dsc, dg, dn), ds_moved, dwin_moved


def _mix_in(x, ada9, nrm, w, widths, dts, name, plans=()):
    t, d = x.shape
    n = w.shape[0]
    tm = min(ROW_TILE, t)
    starts = [sum(widths[:i]) for i in range(len(widths))]

    def body(x_ref, ada_ref, n_ref, w_ref, h_ref, *out_refs):
        _, _, h = _normmod(x_ref[...], n_ref[...], ada_ref[4:5, :], ada_ref[3:4, :])
        hb = h.astype(BF16)
        h_ref[...] = hb
        for o_ref, st, wd in zip(out_refs, starts, widths):
            o_ref[...] = _dot_nt(hb, w_ref[st:st + wd, :]).astype(o_ref.dtype)

    return _pcall(
        body, name, [x, ada9, nrm, w], grid=(t // tm,),
        in_specs=[pl.BlockSpec((tm, d), lambda i: (i, 0)), pl.BlockSpec((N_ADA, d), lambda i: (0, 0)),
                  pl.BlockSpec((1, d), lambda i: (0, 0)), pl.BlockSpec((n, d), lambda i: (0, 0))],
        out_shape=[S((t, d), BF16)] + [S((t, wd), dt) for wd, dt in zip(widths, dts)],
        out_specs=[pl.BlockSpec((tm, d), lambda i: (i, 0))] + [pl.BlockSpec((tm, wd), lambda i: (i, 0)) for wd in widths],
        sem=("parallel",), plans=plans)


def _tri(lower):
    r = lax.broadcasted_iota(jnp.int32, (CHUNK, CHUNK), 0)
    c = lax.broadcasted_iota(jnp.int32, (CHUNK, CHUNK), 1)
    return (r >= c) if lower else (c >= r)


def _dot_01(m, x):
    hi = x.astype(BF16)
    r1 = x - hi.astype(F32)
    mid = r1.astype(BF16)
    lo = (r1 - mid.astype(F32)).astype(BF16)
    return _dot(m, hi) + _dot(m, mid) + _dot(m, lo)


def _gla_chunk_terms(q, k, lg, low01):
    b = _dot_01(low01, lg)
    bl = b[CHUNK - 1:CHUNK, :]
    r = 0.5 * bl
    eb, ebl, em, en = jnp.exp(b), jnp.exp(bl - b), jnp.exp(b - r), jnp.exp(r - b)
    return eb, ebl, em, en, jnp.exp(bl), q * eb, k * ebl, q * em, k * en


def _scores(qm_h, knp, qk1_h):
    r = lax.broadcasted_iota(jnp.int32, (CHUNK, CHUNK), 0)
    c = lax.broadcasted_iota(jnp.int32, (CHUNK, CHUNK), 1)
    p = jnp.where(r > c, _dot_nt(qm_h, knp), 0.0)
    return jnp.where(r == c, jnp.sum(qk1_h, axis=1, keepdims=True), p)


def _gla_fwd(qk, v, gl, wg, bg, heads, name, plans=()):
    t = qk.shape[0]
    kw, vw = qk.shape[1] // 2, v.shape[1]
    dk, dv = kw // heads, vw // heads
    assert dk == 64 and dv == 128 and kw % 128 == 0
    gt = min(ROW_TILE, t)
    nc = gt // CHUNK
    scale = dk ** -0.5

    def body(qk_ref, v_ref, gl_ref, wg_ref, bg_ref, o_ref, lg_ref, sall_ref, st_ref):
        @pl.when(pl.program_id(0) == 0)
        def _():
            st_ref[...] = jnp.zeros_like(st_ref)

        gk = _dot(gl_ref[...].astype(BF16), wg_ref[...]) + bg_ref[...]
        lg_ref[...] = (jnp.minimum(gk, 0.0) - jnp.log(1.0 + jnp.exp(-jnp.abs(gk)))) / GATE_NORMALIZER
        low01 = _tri(True).astype(BF16)
        lane = lax.broadcasted_iota(jnp.int32, (CHUNK, LANES), 1)

        def chunk(ci, carry):
            rows = pl.ds(pl.multiple_of(ci * CHUNK, CHUNK), CHUNK)
            q = qk_ref[rows, 0:kw] * scale
            k = qk_ref[rows, kw:2 * kw]
            qk1 = q.astype(BF16).astype(F32) * k.astype(BF16).astype(F32)
            eb, ebl, em, en, ebl_row, qe, ke, qm, kn = _gla_chunk_terms(q, k, lg_ref[rows, :], low01)
            for h in range(heads):
                lanes = slice(LANES * (h // 2), LANES * (h // 2) + LANES)
                own = (lane < 64) if h % 2 == 0 else (lane >= 64)
                knp = kn[:, lanes].astype(BF16)
                qm_h = jnp.where(own, qm[:, lanes], 0.0).astype(BF16)
                qe_h = jnp.where(own, qe[:, lanes], 0.0).astype(BF16)
                ke_h = jnp.where(own, ke[:, lanes], 0.0).astype(BF16)
                v_h = v_ref[rows, h * dv:(h + 1) * dv]
                st = st_ref[h]
                sall_ref[ci, h] = st
                p = _scores(qm_h, knp, jnp.where(own, qk1[:, lanes], 0.0))
                o_ref[rows, h * dv:(h + 1) * dv] = _dot(p.astype(BF16), v_h) + _dot_nt(qe_h, st.astype(BF16))
                st_ref[h] = st * ebl_row[:, lanes] + _dot_tn(v_h, ke_h)
            return carry

        lax.fori_loop(0, nc, chunk, 0, unroll=True)

    return _pcall(
        body, name, [qk, v, gl, wg, bg], grid=(t // gt,),
        in_specs=[pl.BlockSpec((gt, 2 * kw), lambda i: (i, 0)), pl.BlockSpec((gt, vw), lambda i: (i, 0)),
                  pl.BlockSpec((gt, LANES), lambda i: (i, 0)), pl.BlockSpec((LANES, kw), lambda i: (0, 0)),
                  pl.BlockSpec((1, kw), lambda i: (0, 0))],
        out_shape=[S((t, vw), F32), S((t, kw), F32), S((t // CHUNK, heads, dv, LANES), F32)],
        out_specs=[pl.BlockSpec((gt, vw), lambda i: (i, 0)), pl.BlockSpec((gt, kw), lambda i: (i, 0)),
                   pl.BlockSpec((nc, heads, dv, LANES), lambda i: (i, 0, 0, 0))],
        scratch_shapes=[pltpu.VMEM((heads, dv, LANES), F32)],
        sem=("arbitrary",), plans=plans)


def _gla_bwd(qk, v, lg, do, sall, gl, wg, heads, name):
    t = qk.shape[0]
    kw, vw = qk.shape[1] // 2, v.shape[1]
    dk, dv = kw // heads, vw // heads
    gt = min(ROW_TILE, t)
    nc = gt // CHUNK
    nt = t // gt
    scale = dk ** -0.5

    def body(qk_ref, v_ref, lg_ref, do_ref, sall_ref, gl_ref, wg_ref, dqk_ref, dv_ref, dgl_ref, dwg_ref, dbg_ref, dst_ref, dgk_ref):
        @pl.when(pl.program_id(0) == 0)
        def _():
            dst_ref[...] = jnp.zeros_like(dst_ref)
            dwg_ref[...] = jnp.zeros_like(dwg_ref)
            dbg_ref[...] = jnp.zeros_like(dbg_ref)

        low01 = _tri(True).astype(BF16)
        up01 = _tri(False).astype(BF16)
        causal = _tri(True)
        lane = lax.broadcasted_iota(jnp.int32, (CHUNK, LANES), 1)
        last_row = lax.broadcasted_iota(jnp.int32, (CHUNK, kw), 0) == CHUNK - 1

        def chunk(cj, carry):
            ci = nc - 1 - cj
            rows = pl.ds(pl.multiple_of(ci * CHUNK, CHUNK), CHUNK)
            q = qk_ref[rows, 0:kw] * scale
            k = qk_ref[rows, kw:2 * kw]
            qk1 = q.astype(BF16).astype(F32) * k.astype(BF16).astype(F32)
            lgc = lg_ref[rows, :]
            eb, ebl, em, en, ebl_row, qe, ke, qm, kn = _gla_chunk_terms(q, k, lgc, low01)
            dqe, dqm, dkn, dke, drow = [], [], [], [], []
            for pr in range(kw // LANES):
                lanes = slice(LANES * pr, LANES * pr + LANES)
                knp = kn[:, lanes].astype(BF16)
                parts = []
                for half in range(2):
                    h = 2 * pr + half
                    own = (lane < 64) if half == 0 else (lane >= 64)
                    qm_h = jnp.where(own, qm[:, lanes], 0.0).astype(BF16)
                    qe_h = jnp.where(own, qe[:, lanes], 0.0).astype(BF16)
                    ke_h = jnp.where(own, ke[:, lanes], 0.0).astype(BF16)
                    v_h = v_ref[rows, h * dv:(h + 1) * dv]
                    do_h = do_ref[rows, h * dv:(h + 1) * dv]
                    st = sall_ref[ci, h]
                    dst = dst_ref[h]
                    stb, dstb = st.astype(BF16), dst.astype(BF16)
                    p = _scores(qm_h, knp, jnp.where(own, qk1[:, lanes], 0.0)).astype(BF16)
                    dp = jnp.where(causal, _dot_nt(do_h, v_h), 0.0).astype(BF16)
                    dv_ref[rows, h * dv:(h + 1) * dv] = (_dot_tn(p, do_h) + _dot_nt(ke_h, dstb)).astype(BF16)
                    parts.append((jnp.where(own, _dot(dp, knp), 0.0), _dot_tn(dp, qm_h), _dot(do_h, stb), _dot(v_h, dstb),
                                  jnp.sum(st * dst, axis=0, keepdims=True)))
                    dst_ref[h] = dst * ebl_row[:, lanes] + _dot_tn(do_h, qe_h)
                dqm.append(parts[0][0] + parts[1][0])
                dkn.append(parts[0][1] + parts[1][1])
                dqe.append(parts[0][2] + parts[1][2])
                dke.append(parts[0][3] + parts[1][3])
                drow.append(parts[0][4] + parts[1][4])
            dqm, dkn, dqe, dke, drow = [jnp.concatenate(a, axis=1) for a in (dqm, dkn, dqe, dke, drow)]
            dqk_ref[rows, 0:kw] = ((dqe * eb + dqm * em) * scale).astype(BF16)
            dqk_ref[rows, kw:2 * kw] = (dke * ebl + dkn * en).astype(BF16)
            tke = dke * ke
            db = dqe * qe + dqm * qm - dkn * kn - tke
            dbl = jnp.sum(tke, axis=0, keepdims=True) + drow * ebl_row
            db = db + jnp.where(last_row, dbl, 0.0)
            dlg = _dot_01(up01, db)
            dgk_ref[rows, :] = dlg * ((1.0 - jnp.exp(GATE_NORMALIZER * lgc)) / GATE_NORMALIZER)
            return carry

        lax.fori_loop(0, nc, chunk, 0, unroll=True)
        dgk = dgk_ref[...]
        dgkb = dgk.astype(BF16)
        dgl_ref[...] = _dot_nt(dgkb, wg_ref[...]).astype(BF16)
        dwg_ref[...] += _dot_tn(gl_ref[...].astype(BF16), dgkb)
        dbg_ref[...] += _rsum8(dgk)

    rev = lambda i: (nt - 1 - i, 0)
    return pl.pallas_call(
        body, name=name, grid=(nt,),
        in_specs=[pl.BlockSpec((gt, 2 * kw), rev), pl.BlockSpec((gt, vw), rev), pl.BlockSpec((gt, kw), rev),
                  pl.BlockSpec((gt, vw), rev), pl.BlockSpec((nc, heads, dv, LANES), lambda i: (nt - 1 - i, 0, 0, 0)),
                  pl.BlockSpec((gt, LANES), rev), pl.BlockSpec((LANES, kw), lambda i: (0, 0))],
        out_shape=[S((t, 2 * kw), BF16), S((t, vw), BF16), S((t, LANES), BF16), S((LANES, kw), F32), S((8, kw), F32)],
        out_specs=[pl.BlockSpec((gt, 2 * kw), rev), pl.BlockSpec((gt, vw), rev), pl.BlockSpec((gt, LANES), rev),
                   pl.BlockSpec((LANES, kw), lambda i: (0, 0)), pl.BlockSpec((8, kw), lambda i: (0, 0))],
        scratch_shapes=[pltpu.VMEM((heads, dv, LANES), F32), pltpu.VMEM((gt, kw), F32)],
        compiler_params=_cp("arbitrary"),
    )(qk, v, lg, do, sall, gl, wg)


def _conv_taps(cx_ref, halo_ref, first, cw):
    tm = cx_ref.shape[0]
    u = cx_ref[:, cw:2 * cw].astype(F32) * cx_ref[:, 2 * cw:3 * cw].astype(F32)
    uh = halo_ref[:, cw:2 * cw].astype(F32) * halo_ref[:, 2 * cw:3 * cw].astype(F32)
    uh = jnp.where(first, 0.0, uh)
    before1, before2 = uh[BF16_ROWS - 1:BF16_ROWS, :], uh[BF16_ROWS - 2:BF16_ROWS - 1, :]
    row = lax.broadcasted_iota(jnp.int32, (tm, cw), 0)
    u1 = jnp.where(row == 0, before1, pltpu.roll(u, 1, 0))
    u2 = jnp.where(row == 0, before2, jnp.where(row == 1, before1, pltpu.roll(u, 2, 0)))
    return u, u1, u2


def _head_norm(o_h, gn):
    rstd = lax.rsqrt(jnp.mean(o_h * o_h, axis=-1, keepdims=True) + EPS)
    ohat = o_h * rstd
    return ohat, rstd, ohat * gn


def _mix_out(cx, o, go, conv_w, gn, w_out, x, ada9, heads, name, plans=()):
    t, d = x.shape
    cw, vw = conv_w.shape[1], o.shape[1]
    dv = vw // heads
    tm = min(ROW_TILE, t)

    def body(cx_ref, halo_ref, o_ref, go_ref, cwt_ref, gn_ref, w_ref, x_ref, ada_ref, xo_ref, m_ref, y_ref):
        u, u1, u2 = _conv_taps(cx_ref, halo_ref, pl.program_id(0) == 0, cw)
        yc = cwt_ref[0:1, :] * u2 + cwt_ref[1:2, :] * u1 + cwt_ref[2:3, :] * u
        y_ref[:, 0:cw] = (cx_ref[:, 0:cw].astype(F32) * yc).astype(BF16)
        for h in range(heads):
            cols = slice(h * dv, (h + 1) * dv)
            _, _, on = _head_norm(o_ref[:, cols], gn_ref[...])
            g = go_ref[:, cols].astype(F32)
            y_ref[:, cw + h * dv:cw + (h + 1) * dv] = (on * (g * _sigmoid(g))).astype(BF16)
        m = _dot(y_ref[...], w_ref[...])
        m_ref[...] = m.astype(BF16)
        xo_ref[...] = x_ref[...] + ada_ref[5:6, :] * m

    return _pcall(
        body, name, [cx, cx, o, go, conv_w, gn, w_out, x, ada9], grid=(t // tm,),
        in_specs=[pl.BlockSpec((tm, 3 * cw), lambda i: (i, 0)),
                  pl.BlockSpec((BF16_ROWS, 3 * cw), lambda i: (jnp.maximum(i * (tm // BF16_ROWS) - 1, 0), 0)),
                  pl.BlockSpec((tm, vw), lambda i: (i, 0)), pl.BlockSpec((tm, vw), lambda i: (i, 0)),
                  pl.BlockSpec((3, cw), lambda i: (0, 0)), pl.BlockSpec((1, dv), lambda i: (0, 0)),
                  pl.BlockSpec((cw + vw, d), lambda i: (0, 0)), pl.BlockSpec((tm, d), lambda i: (i, 0)),
                  pl.BlockSpec((N_ADA, d), lambda i: (0, 0))],
        out_shape=[S((t, d), F32), S((t, d), BF16), S((t, cw + vw), BF16)],
        out_specs=[pl.BlockSpec((tm, d), lambda i: (i, 0)), pl.BlockSpec((tm, d), lambda i: (i, 0)),
                   pl.BlockSpec((tm, cw + vw), lambda i: (i, 0))],
        sem=("parallel",), plans=plans)


def _mix_bwd_a(dxo, m, ada9, w_out, cx, o, go, conv_w, gn, heads, name, plans=()):
    t, d = dxo.shape
    cw, vw = conv_w.shape[1], o.shape[1]
    dv = vw // heads
    tm = min(ROW_TILE, t)

    def body(dxo_ref, m_ref, ada_ref, w_ref, cx_ref, halo_ref, o_ref, go_ref, cwt_ref, gn_ref,
             dm_ref, dyc_ref, dcb_ref, do_ref, dgo_ref, dg_ref, dcw_ref, dgn_ref):
        @pl.when(pl.program_id(0) == 0)
        def _():
            dg_ref[...] = jnp.zeros_like(dg_ref)
            dcw_ref[...] = jnp.zeros_like(dcw_ref)
            dgn_ref[...] = jnp.zeros_like(dgn_ref)

        dxo_t = dxo_ref[...]
        dmb = (ada_ref[5:6, :] * dxo_t).astype(BF16)
        dm_ref[...] = dmb
        dg_ref[...] += _rsum8(dxo_t * m_ref[...].astype(F32))
        dy = _dot_nt(dmb, w_ref[...])
        u, u1, u2 = _conv_taps(cx_ref, halo_ref, pl.program_id(0) == 0, cw)
        yc = cwt_ref[0:1, :] * u2 + cwt_ref[1:2, :] * u1 + cwt_ref[2:3, :] * u
        dyv = dy[:, 0:cw]
        dcb_ref[...] = (dyv * yc).astype(BF16)
        dyc = dyv * cx_ref[:, 0:cw].astype(F32)
        dyc_ref[...] = dyc
        dcw_ref[0] += _rsum8(dyc * u2)
        dcw_ref[1] += _rsum8(dyc * u1)
        dcw_ref[2] += _rsum8(dyc * u)
        for h in range(heads):
            cols = slice(h * dv, (h + 1) * dv)
            ohat, rstd, on = _head_norm(o_ref[:, cols], gn_ref[...])
            g = go_ref[:, cols].astype(F32)
            sg = _sigmoid_tanh(g)
            dyg = dy[:, cw + h * dv:cw + (h + 1) * dv]
            dgo_ref[:, cols] = (dyg * on * (sg * (1.0 + g * (1.0 - sg)))).astype(BF16)
            don = dyg * (g * sg)
            dgn_ref[...] += _rsum8(don * ohat)
            tt = don * gn_ref[...]
            do_ref[:, cols] = (rstd * (tt - ohat * jnp.mean(tt * ohat, axis=-1, keepdims=True))).astype(BF16)

    return _pcall(
        body, name, [dxo, m, ada9, w_out, cx, cx, o, go, conv_w, gn], grid=(t // tm,),
        in_specs=[pl.BlockSpec((tm, d), lambda i: (i, 0)), pl.BlockSpec((tm, d), lambda i: (i, 0)),
                  pl.BlockSpec((N_ADA, d), lambda i: (0, 0)), pl.BlockSpec((cw + vw, d), lambda i: (0, 0)),
                  pl.BlockSpec((tm, 3 * cw), lambda i: (i, 0)),
                  pl.BlockSpec((BF16_ROWS, 3 * cw), lambda i: (jnp.maximum(i * (tm // BF16_ROWS) - 1, 0), 0)),
                  pl.BlockSpec((tm, vw), lambda i: (i, 0)), pl.BlockSpec((tm, vw), lambda i: (i, 0)),
                  pl.BlockSpec((3, cw), lambda i: (0, 0)), pl.BlockSpec((1, dv), lambda i: (0, 0))],
        out_shape=[S((t, d), BF16), S((t, cw), F32), S((t, cw), BF16), S((t, vw), BF16), S((t, vw), BF16),
                   S((8, d), F32), S((3, 8, cw), F32), S((8, dv), F32)],
        out_specs=[pl.BlockSpec((tm, d), lambda i: (i, 0)), pl.BlockSpec((tm, cw), lambda i: (i, 0)),
                   pl.BlockSpec((tm, cw), lambda i: (i, 0)), pl.BlockSpec((tm, vw), lambda i: (i, 0)),
                   pl.BlockSpec((tm, vw), lambda i: (i, 0)), pl.BlockSpec((8, d), lambda i: (0, 0)),
                   pl.BlockSpec((3, 8, cw), lambda i: (0, 0, 0)), pl.BlockSpec((8, dv), lambda i: (0, 0))],
        sem=("arbitrary",), plans=plans)


def _mix_bwd_b(dyc, cx, dcb, dqk, dvv, dgo, dgl, conv_w, w, x, dxo, ada9, nrm, name):
    t, d = x.shape
    cw = conv_w.shape[1]
    n = w.shape[0]
    tm = min(ROW_TILE, t)
    nt = t // tm
    pieces = [dcb.shape[1], cw, cw, dqk.shape[1], dvv.shape[1], dgo.shape[1], dgl.shape[1]]
    assert sum(pieces) == n

    def body(dyc_ref, nxt_ref, cx_ref, dcb_ref, dqk_ref, dv_ref, dgo_ref, dgl_ref, cwt_ref, w_ref, x_ref, dxo_ref, ada_ref, n_ref,
             dx_ref, dp_ref, dsh_ref, dsc_ref, dn_ref):
        i = pl.program_id(0)

        @pl.when(i == 0)
        def _():
            dsh_ref[...] = jnp.zeros_like(dsh_ref)
            dsc_ref[...] = jnp.zeros_like(dsc_ref)
            dn_ref[...] = jnp.zeros_like(dn_ref)

        dyc_t = dyc_ref[...]
        nxt = jnp.where(i == nt - 1, 0.0, nxt_ref[...])
        row = lax.broadcasted_iota(jnp.int32, (tm, cw), 0)
        d1 = jnp.where(row == tm - 1, nxt[0:1, :], pltpu.roll(dyc_t, tm - 1, 0))
        d2 = jnp.where(row == tm - 2, nxt[0:1, :], jnp.where(row == tm - 1, nxt[1:2, :], pltpu.roll(dyc_t, tm - 2, 0)))
        du = cwt_ref[2:3, :] * dyc_t + cwt_ref[1:2, :] * d1 + cwt_ref[0:1, :] * d2
        c0 = 0
        dp_ref[:, c0:c0 + cw] = dcb_ref[...]
        dp_ref[:, cw:2 * cw] = (du * cx_ref[:, 2 * cw:3 * cw].astype(F32)).astype(BF16)
        dp_ref[:, 2 * cw:3 * cw] = (du * cx_ref[:, cw:2 * cw].astype(F32)).astype(BF16)
        c0 = 3 * cw
        for ref in (dqk_ref, dv_ref, dgo_ref, dgl_ref):
            wd = ref.shape[1]
            dp_ref[:, c0:c0 + wd] = ref[...]
            c0 += wd
        dh = _dot(dp_ref[...], w_ref[...])
        dx, tsh, tsc, tn = _normmod_bwd(dh, x_ref[...], n_ref[...], ada_ref[4:5, :])
        dx_ref[...] = dxo_ref[...] + dx
        dsh_ref[...] += _rsum8(tsh)
        dsc_ref[...] += _rsum8(tsc)
        dn_ref[...] += _rsum8(tn)

    row_spec = lambda wd: pl.BlockSpec((tm, wd), lambda i: (i, 0))
    vec = pl.BlockSpec((8, d), lambda i: (0, 0))
    return pl.pallas_call(
        body, name=name, grid=(nt,),
        in_specs=[row_spec(cw), pl.BlockSpec((8, cw), lambda i: (jnp.minimum((i + 1) * (tm // 8), t // 8 - 1), 0)),
                  row_spec(3 * cw), row_spec(cw), row_spec(dqk.shape[1]), row_spec(dvv.shape[1]), row_spec(dgo.shape[1]),
                  row_spec(dgl.shape[1]), pl.BlockSpec((3, cw), lambda i: (0, 0)), pl.BlockSpec((n, d), lambda i: (0, 0)),
                  row_spec(d), row_spec(d), pl.BlockSpec((N_ADA, d), lambda i: (0, 0)), pl.BlockSpec((1, d), lambda i: (0, 0))],
        out_shape=[S((t, d), F32), S((t, n), BF16), S((8, d), F32), S((8, d), F32), S((8, d), F32)],
        out_specs=[row_spec(d), row_spec(n), vec, vec, vec],
        compiler_params=_cp("arbitrary"),
    )(dyc, dyc, cx, dcb, dqk, dvv, dgo, dgl, conv_w, w, x, dxo, ada9, nrm)


def _ffn_out_loss(s, w_out, x, ada9, g_row, res_scale, target, nrm, name):
    nb, t, bw = s.shape
    d = x.shape[1]
    tm = min(ROW_TILE, t)
    nt = t // tm

    def body(s_ref, w_ref, x_ref, ada_ref, tg_ref, n_ref, f_ref, loss_ref, dx_ref, dn_ref, acc_ref):
        i = pl.program_id(0)

        @pl.when(i == 0)
        def _():
            acc_ref[...] = jnp.zeros_like(acc_ref)
            dn_ref[...] = jnp.zeros_like(dn_ref)

        f = _dot(s_ref[0], w_ref[0])
        for b in range(1, nb):
            f = f + _dot(s_ref[b], w_ref[b])
        f_ref[...] = f.astype(BF16)
        xt = x_ref[...] + (res_scale * ada_ref[g_row:g_row + 1, :]) * f
        rstd = lax.rsqrt(jnp.mean(xt * xt, axis=-1, keepdims=True) + EPS)
        xhat = xt * rstd
        err = xhat * n_ref[...] - tg_ref[...]
        acc_ref[...] += _rsum8(err * err)
        dy = err * (1.0 / d)
        dn_ref[...] += _rsum8(dy * xhat)
        dxhat = dy * n_ref[...]
        dx_ref[...] = rstd * (dxhat - xhat * jnp.mean(dxhat * xhat, axis=-1, keepdims=True))

        @pl.when(i == nt - 1)
        def _():
            loss_ref[...] = jnp.full(loss_ref.shape, (0.5 / d) * jnp.sum(acc_ref[...]), F32)

    return pl.pallas_call(
        body, name=name, grid=(nt,),
        in_specs=[pl.BlockSpec((nb, tm, bw), lambda i: (0, i, 0)), pl.BlockSpec((nb, bw, d), lambda i: (0, 0, 0)),
                  pl.BlockSpec((tm, d), lambda i: (i, 0)), pl.BlockSpec((N_ADA, d), lambda i: (0, 0)),
                  pl.BlockSpec((tm, d), lambda i: (i, 0)), pl.BlockSpec((1, d), lambda i: (0, 0))],
        out_shape=[S((t, d), BF16), S((1, LANES), F32), S((t, d), F32), S((8, d), F32)],
        out_specs=[pl.BlockSpec((tm, d), lambda i: (i, 0)), pl.BlockSpec((1, LANES), lambda i: (0, 0)),
                   pl.BlockSpec((tm, d), lambda i: (i, 0)), pl.BlockSpec((8, d), lambda i: (0, 0))],
        scratch_shapes=[pltpu.VMEM((8, d), F32)],
        compiler_params=_cp("arbitrary"),
    )(s, w_out, x, ada9, target, nrm)


def _pack_smalls(vec_parts, dcw, dbg, dgn, dwg, loss_v, rank, name):
    d = vec_parts[0].shape[1]
    cw, kw, dv = dcw.shape[2], dbg.shape[1], dgn.shape[1]
    nv = len(vec_parts)
    loss_row = nv + 2 + rank * kw // d
    assert 2 * cw == d and cw + kw + dv <= d and (rank * kw) % d == 0 and loss_row < PACK_ROWS
    per_row = d // kw

    def body(*refs):
        vrefs, (dcw_ref, dbg_ref, dgn_ref, dwg_ref, loss_ref, o_ref) = refs[:nv], refs[nv:]
        o_ref[...] = jnp.zeros_like(o_ref)
        o_ref[loss_row:loss_row + 1, 0:loss_ref.shape[1]] = loss_ref[...]
        for r, ref in enumerate(vrefs):
            o_ref[r:r + 1, :] = jnp.sum(ref[...], axis=0, keepdims=True)
        o_ref[nv:nv + 1, 0:cw] = jnp.sum(dcw_ref[0], axis=0, keepdims=True)
        o_ref[nv:nv + 1, cw:2 * cw] = jnp.sum(dcw_ref[1], axis=0, keepdims=True)
        o_ref[nv + 1:nv + 2, 0:cw] = jnp.sum(dcw_ref[2], axis=0, keepdims=True)
        o_ref[nv + 1:nv + 2, cw:cw + kw] = jnp.sum(dbg_ref[...], axis=0, keepdims=True)
        o_ref[nv + 1:nv + 2, cw + kw:cw + kw + dv] = jnp.sum(dgn_ref[...], axis=0, keepdims=True)
        for r in range(rank):
            o_ref[nv + 2 + r // per_row:nv + 3 + r // per_row, (r % per_row) * kw:(r % per_row + 1) * kw] = dwg_ref[r:r + 1, :]

    return pl.pallas_call(body, name=name, out_shape=S((PACK_ROWS, d), F32), compiler_params=_cp())(*vec_parts, dcw, dbg, dgn, dwg, loss_v)


def _sum_slots(a, name):
    def body(a_ref, o_ref):
        acc = a_ref[0]
        for s in range(1, NDEV):
            acc = acc + a_ref[s]
        o_ref[...] = acc

    return pl.pallas_call(body, name=name, out_shape=S(a.shape[1:], F32), compiler_params=_cp())(a)


def _adamw(w, g, m, v):
    m = ADAM_B1 * m + (1.0 - ADAM_B1) * g
    v = ADAM_B2 * v + (1.0 - ADAM_B2) * (g * g)
    m_hat = m / (1.0 - ADAM_B1 ** ADAM_STEP)
    v_hat = v / (1.0 - ADAM_B2 ** ADAM_STEP)
    return -ADAM_LR * (m_hat / (jnp.sqrt(v_hat) + ADAM_EPS) + ADAM_WD * w), m, v


def _adam_slots(recv, w, m, v, name):
    r, c = w.shape
    slots = recv.shape[0]
    tr = _row_tile(r, c)

    def body(recv_ref, w_ref, m_ref, v_ref, g_ref, d_ref, mo_ref, vo_ref):
        g = recv_ref[0].astype(F32)
        for s in range(1, slots):
            g = g + recv_ref[s].astype(F32)
        g_ref[...] = g
        d_ref[...], mo_ref[...], vo_ref[...] = _adamw(w_ref[...], g, m_ref[...], v_ref[...])

    blk = pl.BlockSpec((tr, c), lambda i: (i, 0))
    return pl.pallas_call(
        body, name=name, grid=(r // tr,),
        in_specs=[pl.BlockSpec((slots, tr, c), lambda i: (0, i, 0)), blk, blk, blk],
        out_shape=[S((r, c), F32)] * 4, out_specs=[blk] * 4, compiler_params=_cp("parallel"),
    )(recv, w, m, v)


def _adam_w_ada(act_t, dada, w, m, v, name):
    r, c = w.shape
    tr = 128
    nb = act_t.shape[1]

    def body(a_ref, da_ref, w_ref, m_ref, v_ref, g_ref, d_ref, mo_ref, vo_ref):
        g = a_ref[:, 0:1] * da_ref[0:1, :]
        for b in range(1, nb):
            g = g + a_ref[:, b:b + 1] * da_ref[b:b + 1, :]
        g_ref[...] = g
        d_ref[...], mo_ref[...], vo_ref[...] = _adamw(w_ref[...], g, m_ref[...], v_ref[...])

    blk = pl.BlockSpec((tr, c), lambda i: (i, 0))
    return pl.pallas_call(
        body, name=name, grid=(r // tr,),
        in_specs=[pl.BlockSpec((tr, nb), lambda i: (i, 0)), pl.BlockSpec((nb, c), lambda i: (0, 0)), blk, blk, blk],
        out_shape=[S((r, c), F32)] * 4, out_specs=[blk] * 4, compiler_params=_cp("parallel"),
    )(act_t, dada, w, m, v)


def _adam_smalls(ws, gs, ms, vs, name):
    n = len(ws)

    def body(*refs):
        w_r, g_r, m_r, v_r = (refs[k * n:(k + 1) * n] for k in range(4))
        d_o, m_o, v_o = (refs[(4 + k) * n:(5 + k) * n] for k in range(3))
        for i in range(n):
            d_o[i][...], m_o[i][...], v_o[i][...] = _adamw(w_r[i][...], g_r[i][...], m_r[i][...], v_r[i][...])

    shapes = [S(w.shape, F32) for w in ws]
    outs = pl.pallas_call(body, name=name, out_shape=shapes * 3, compiler_params=_cp())(*ws, *gs, *ms, *vs)
    return outs[:n], outs[n:2 * n], outs[2 * n:]


def kernel(x, c, w_ada, b_ada, norm_ffn1, w_ffn1_in, w_ffn1_out, norm_mix, w_mix_in, conv_w, w_gk2, b_gk, gla_norm, w_mix_out, norm_ffn2, w_ffn2_in, w_ffn2_out, norm_final, loss_target, m_w_ada, m_b_ada, m_norm_ffn1, m_w_ffn1_in, m_w_ffn1_out, m_norm_mix, m_w_mix_in, m_conv_w, m_w_gk2, m_b_gk, m_gla_norm, m_w_mix_out, m_norm_ffn2, m_w_ffn2_in, m_w_ffn2_out, m_norm_final, v_w_ada, v_b_ada, v_norm_ffn1, v_w_ffn1_in, v_w_ffn1_out, v_norm_mix, v_w_mix_in, v_conv_w, v_w_gk2, v_b_gk, v_gla_norm, v_w_mix_out, v_norm_ffn2, v_w_ffn2_in, v_w_ffn2_out, v_norm_final):
    t, d = x.shape[1], x.shape[2]
    x0, tgt = x[0], loss_target[0]
    rank, kw = w_gk2.shape[1], w_gk2.shape[2] * NDEV
    cw = conv_w.shape[2] * NDEV
    dv = gla_norm.shape[1]
    vw = d - cw
    heads = vw // dv
    mix_cols = w_mix_in.shape[2]
    widths = [3 * cw, 2 * kw, vw, vw, LANES]
    n_proj = 3 * cw + 2 * kw + 2 * vw + rank
    assert n_proj == mix_cols * NDEV and rank <= LANES
    me = 4 * lax.axis_index("x") + 2 * lax.axis_index("y") + lax.axis_index("c")

    core = lax.axis_index("c").astype(jnp.int32).reshape(1)
    bf = lambda a: a[0].astype(BF16)
    bft = lambda a: a[0].T.astype(BF16)
    nb = NDEV // 2

    (ada_row, act_all), ((w1i, cwt_all, wg_all),) = _ada_rows(
        c, w_ada[0], b_ada, "ada_rows", plans=[_plan_gather([bft(w_ffn1_in), conv_w[0], w_gk2[0]])])
    ada9 = ada_row.reshape(N_ADA, d)
    cwt = cwt_all.transpose(1, 0, 2).reshape(conv_w.shape[1], cw)
    wg = jnp.pad(wg_all.transpose(1, 0, 2).reshape(rank, kw), ((0, LANES - rank), (0, 0))).astype(BF16)

    (h1, gu1, s1), ((w1o, wmi),) = _ffn_in(x0, ada9, norm_ffn1, w1i, 0, 1, "ffn1_in", plans=[_plan_gather([bf(w_ffn1_out), bft(w_mix_in)])])
    w1o = w1o.reshape(nb, -1, d)
    wmi = jnp.pad(wmi.reshape(n_proj, d), ((0, sum(widths) - n_proj), (0, 0)))
    w2i_mine = bft(w_ffn2_in)
    quarter = w2i_mine.shape[0] // 4
    part = lambda k, into=None: _plan_gather([w2i_mine], rows=(k * quarter, quarter), into=into)
    (x1, f1), ((wmo,), (w2i,)) = _ffn_out(s1, w1o, x0, ada9, 2, 0.5, "ffn1_out", plans=[_plan_gather([bf(w_mix_out)]), part(0)])
    wmo = wmo.reshape(cw + vw, d)
    (h2, cx, qk, vv, go, gl), ((w2i,),) = _mix_in(x1, ada9, norm_mix, wmi, widths, [BF16, F32, BF16, BF16, F32], "mix_in",
                                                 plans=[part(1, [w2i])])
    (o, lg, sall), ((w2i,),) = _gla_fwd(qk, vv, gl, wg, b_gk, heads, "gla_fwd", plans=[part(2, [w2i])])
    (x2, mm, ycat), ((w2i,),) = _mix_out(cx, o, go, cwt, gla_norm, wmo, x1, ada9, heads, "mix_out", plans=[part(3, [w2i])])
    (h3, gu3, s3), ((w2o,),) = _ffn_in(x2, ada9, norm_ffn2, w2i, 6, 7, "ffn2_in", plans=[_plan_gather([bf(w_ffn2_out)])])
    w2o = w2o.reshape(nb, -1, d)
    f3, loss_v, dx3, dnf = _ffn_out_loss(s3, w2o, x2, ada9, 8, 0.5, tgt, norm_final.reshape(1, d), "ffn2_out_loss")

    dx2, r2i, sum2o, (dsh3, dsc3, dg3, dn3), _, _ = _ffn_backward(
        dx3, x2, h3, gu3, f3, ada9, norm_ffn2, w2i, w2o, (6, 7, 8), core, "ffn2_bwd")
    (dm, dyc, dcb, do, dgo, dg2, dcw, dgn), ((r2o,),) = _mix_bwd_a(dx2, mm, ada9, wmo, cx, o, go, cwt, gla_norm, heads, "mix_bwd_a",
                                                                 plans=[_plan_chip_swap([sum2o])])
    dqk, dvv, dgl, dwg, dbg = _gla_bwd(qk, vv, lg, do, sall, gl, wg, heads, "gla_bwd")
    dx1, dproj, dsh2, dsc2, dnm = _mix_bwd_b(dyc, cx, dcb, dqk, dvv, dgo, dgl, cwt, wmi, x1, dx2, ada9, norm_mix, "mix_bwd_b")
    n_pad = sum(widths)
    tn = n_pad // 5
    dwmi, _ = _tn_matmul(dproj, h2, lambda tk: (tk, tn), lambda sb, k: (k, sb), lambda tk: (tk, d), lambda sb, k: (k, 0),
                         (n_pad, d), (tn, d), lambda sb, k: (sb, 0), 5, "mix_dwin")
    dwmo, _ = _tn_matmul(ycat, dm, lambda tk: (tk, cw + vw), lambda sb, k: (k, 0), lambda tk: (tk, d), lambda sb, k: (k, 0),
                         (cw + vw, d), (cw + vw, d), lambda sb, k: (0, 0), 1, "mix_dwout")
    dwmi = dwmi[:n_proj].reshape(NDEV, mix_cols, d)
    dwmo = dwmo.reshape(NDEV, -1, d)
    dx0, r1i, sum1o, (dsh1, dsc1, dg1, dn1), _, ((rmi, rmo),) = _ffn_backward(
        dx1, x0, h1, gu1, f1, ada9, norm_ffn1, w1i, w1o, (0, 1, 2), core, "ffn1_bwd",
        ds_plans=[_plan_sibling_swap([dwmi, dwmo])],
        dwin_plans=lambda moved: [_plan_chip_swap([_pair_add(dwmi, moved[0][0], core, "mix_dwin_add"),
                                                   _pair_add(dwmo, moved[0][1], core, "mix_dwout_add")])])
    pack = _pack_smalls([dn1, dnm, dn3, dnf, dsh1, dsc1, dg1, dsh2, dsc2, dg2, dsh3, dsc3, dg3], dcw, dbg, dgn, dwg, loss_v, rank, "pack_smalls")
    (r1o,), (pack_all,) = _exchange([_plan_chip_swap([sum1o]), _plan_all_to_all([pack], True)], "grads_last")
    tot = _sum_slots(pack_all, "sum_smalls")

    res = {}
    for nm, recv, w, m, v in (("w_ffn1_out", r1o, w_ffn1_out, m_w_ffn1_out, v_w_ffn1_out), ("w_mix_out", rmo, w_mix_out, m_w_mix_out, v_w_mix_out),
                              ("w_ffn2_out", r2o, w_ffn2_out, m_w_ffn2_out, v_w_ffn2_out)):
        res[nm] = [a[None] for a in _adam_slots(recv, w[0], m[0], v[0], "adam_" + nm)]
    for nm, recv, w, m, v in (("w_ffn1_in", r1i, w_ffn1_in, m_w_ffn1_in, v_w_ffn1_in), ("w_mix_in", rmi, w_mix_in, m_w_mix_in, v_w_mix_in),
                              ("w_ffn2_in", r2i, w_ffn2_in, m_w_ffn2_in, v_w_ffn2_in)):
        res[nm] = [a.T[None] for a in _adam_slots(recv, w[0].T, m[0].T, v[0].T, "adam_" + nm)]

    cols_ada = w_ada.shape[2]
    dada_all = pack_all[:, 4:4 + N_ADA, :].reshape(NDEV, N_ADA * d)
    dada_mine = lax.dynamic_slice_in_dim(dada_all, me * cols_ada, cols_ada, axis=1)
    res["w_ada"] = [a[None] for a in _adam_w_ada(act_all.T, dada_mine, w_ada[0], m_w_ada[0], v_w_ada[0], "adam_w_ada")]

    nv = 4 + N_ADA
    g_small = {
        "b_ada": tot[4:nv].reshape(1, N_ADA * d),
        "norm_ffn1": tot[0:1], "norm_mix": tot[1:2], "norm_ffn2": tot[2:3], "norm_final": tot[3:4],
        "conv_w": lax.dynamic_slice_in_dim(
            jnp.concatenate([tot[nv:nv + 1, 0:cw], tot[nv:nv + 1, cw:2 * cw], tot[nv + 1:nv + 2, 0:cw]], axis=0), me * (cw // NDEV), cw // NDEV, axis=1),
        "w_gk2": lax.dynamic_slice_in_dim(tot[nv + 2:nv + 2 + rank * kw // d].reshape(rank, kw), me * (kw // NDEV), kw // NDEV, axis=1),
        "b_gk": tot[nv + 1:nv + 2, cw:cw + kw],
        "gla_norm": tot[nv + 1:nv + 2, cw + kw:cw + kw + dv],
    }
    small = {"b_ada": (b_ada, m_b_ada, v_b_ada), "norm_ffn1": (norm_ffn1, m_norm_ffn1, v_norm_ffn1), "norm_mix": (norm_mix, m_norm_mix, v_norm_mix),
             "norm_ffn2": (norm_ffn2, m_norm_ffn2, v_norm_ffn2), "norm_final": (norm_final, m_norm_final, v_norm_final),
             "conv_w": (conv_w, m_conv_w, v_conv_w), "w_gk2": (w_gk2, m_w_gk2, v_w_gk2), "b_gk": (b_gk, m_b_gk, v_b_gk),
             "gla_norm": (gla_norm, m_gla_norm, v_gla_norm)}
    names = list(small)
    flat = lambda a: a.reshape(-1, a.shape[-1])
    dl, mo, vo = _adam_smalls([flat(small[n][0]) for n in names], [g_small[n] for n in names],
                              [flat(small[n][1]) for n in names], [flat(small[n][2]) for n in names], "adam_smalls")
    for i, n in enumerate(names):
        shp = small[n][0].shape
        res[n] = [g_small[n].reshape(shp), dl[i].reshape(shp), mo[i].reshape(shp), vo[i].reshape(shp)]

    loss = tot[nv + 2 + rank * kw // d, 0]
    order = ["w_ada", "b_ada", "norm_ffn1", "w_ffn1_in", "w_ffn1_out", "norm_mix", "w_mix_in", "conv_w", "w_gk2", "b_gk", "gla_norm",
             "w_mix_out", "norm_ffn2", "w_ffn2_in", "w_ffn2_out", "norm_final"]
    return (loss, dx0[None], *[res[n][0] for n in order], *[res[n][1] for n in order], *[res[n][2] for n in order], *[res[n][3] for n in order])
```

```python
import collections
import functools

import jax
import jax.numpy as jnp
from jax import lax
from jax.experimental import pallas as pl
from jax.experimental.pallas import tpu as pltpu

F32 = jnp.float32
BF16 = jnp.bfloat16
S = jax.ShapeDtypeStruct

NDEV = 8
EPS = 1e-6
GATE_NORMALIZER = 16.0
CHUNK = 128
N_ADA = 9
ADAM_LR, ADAM_B1, ADAM_B2, ADAM_EPS, ADAM_WD, ADAM_STEP = 0.001, 0.9, 0.999, 1e-08, 0.01, 10
V7X_VMEM_LIMIT = 56 * 1024 * 1024
ROW_TILE = 512
WIDE_ROW_TILE = 1024
K_TILE = 1024
EPILOGUE_ROWS = 256
LANES = 128
BF16_ROWS = 16
PACK_ROWS = 24
ANY = pl.BlockSpec(memory_space=pl.ANY)


def _cp(*sem):
    return pltpu.CompilerParams(dimension_semantics=sem or None, vmem_limit_bytes=V7X_VMEM_LIMIT)


def _dot(a, b):
    return jnp.dot(a, b, preferred_element_type=F32)


def _dot_nt(a, b):
    return lax.dot_general(a, b, (((1,), (1,)), ((), ())), preferred_element_type=F32)


def _dot_tn(a, b):
    return lax.dot_general(a, b, (((0,), (0,)), ((), ())), preferred_element_type=F32)


def _rsum8(a):
    r, c = a.shape
    return jnp.sum(a.reshape(r // 8, 8, c), axis=0)


def _row_tile(r, c):
    for cand in (256, 128, 176, 88, 64, 32, 16, 8):
        if r % cand == 0 and cand * c * 4 <= 1024 * 1024:
            return cand
    return r


def _sigmoid(x):
    return 1.0 / (1.0 + jnp.exp(-x))


def _sigmoid_tanh(x):
    return 0.5 * jnp.tanh(0.5 * x) + 0.5


def _normmod(x, nrm, sc, sh):
    rstd = lax.rsqrt(jnp.mean(x * x, axis=-1, keepdims=True) + EPS)
    xhat = x * rstd
    return xhat, rstd, (xhat * nrm) * (1.0 + sc) + sh


def _normmod_bwd(dh, x, nrm, sc):
    rstd = lax.rsqrt(jnp.mean(x * x, axis=-1, keepdims=True) + EPS)
    xhat = x * rstd
    dxhat = dh * (nrm * (1.0 + sc))
    dx = rstd * (dxhat - xhat * jnp.mean(dxhat * xhat, axis=-1, keepdims=True))
    return dx, dh, dh * (xhat * nrm), dh * ((1.0 + sc) * xhat)


def _place():
    x, y, c = lax.axis_index("x"), lax.axis_index("y"), lax.axis_index("c")
    return x, y, c, 4 * x + 2 * y + c


def _peer(x, y, c, k):
    px = 1 - x if k & 4 else x
    py = 1 - y if k & 2 else y
    pc = 1 - c if k & 1 else c
    return (px, py, pc), 4 * px + 2 * py + pc


def _remote(src, dst, send_sem, recv_sem, peer):
    return pltpu.make_async_remote_copy(src_ref=src, dst_ref=dst, send_sem=send_sem, recv_sem=recv_sem,
                                        device_id=peer, device_id_type=pl.DeviceIdType.MESH)


_Plan = collections.namedtuple("_Plan", "inputs out_shapes sem_shapes start finish aliases", defaults=({},))


def _plan_all_to_all(xs, gather):
    n = len(xs)

    def copies(ins, outs, sems, landed):
        send_sems, recv_sems, local_sems = sems
        x, y, c, me = _place()
        local = [pltpu.make_async_copy(ins[i] if gather else ins[i].at[me], outs[i].at[me], local_sems.at[i]) for i in range(n)]
        remote = []
        for k in range(1, NDEV):
            peer, pid = _peer(x, y, c, k)
            for i in range(n):
                remote.append(_remote(ins[i] if gather else ins[i].at[pid], outs[i].at[pid if landed else me],
                                      send_sems.at[i, k - 1], recv_sems.at[i, k - 1], peer))
        return local, remote

    def start(ins, outs, sems):
        local, remote = copies(ins, outs, sems, False)
        for cp in local + remote:
            cp.start()

    def finish(ins, outs, sems):
        local, remote = copies(ins, outs, sems, True)
        for cp in remote + local:
            cp.wait()

    return _Plan(list(xs), [S((NDEV,) + a.shape, a.dtype) if gather else S(a.shape, a.dtype) for a in xs],
                 [pltpu.SemaphoreType.DMA((n, NDEV - 1)), pltpu.SemaphoreType.DMA((n, NDEV - 1)), pltpu.SemaphoreType.DMA((n,))],
                 start, finish)


def _other_chips(x, y):
    return [(1 - x, y), (x, 1 - y), (1 - x, 1 - y)]


def _plan_gather(xs, rows=None, into=None):
    n = len(xs)

    def copies(ins, outs, sems, rest):
        send_sems, recv_sems, local_sems = sems
        x, y, c, me = _place()
        sib, sib_id = (x, y, 1 - c), 4 * x + 2 * y + 1 - c
        chips = _other_chips(x, y)
        mine = lambda i: ins[i] if rows is None else ins[i].at[pl.ds(*rows)]
        slot_of = lambda i, s: outs[i].at[s] if rows is None else outs[i].at[s, pl.ds(*rows)]
        local = [pltpu.make_async_copy(mine(i), slot_of(i, me), local_sems.at[i]) for i in range(n)]
        first = [_remote(mine(i), slot_of(i, me), send_sems.at[i, 0], recv_sems.at[i, 0], sib) for i in range(n)]
        first += [_remote(mine(i), slot_of(i, me), send_sems.at[i, 1 + j], recv_sems.at[i, 1 + j], (px, py, c))
                  for j, (px, py) in enumerate(chips) for i in range(n)]
        if not rest:
            return local, first
        from_sibling = [_remote(mine(i), slot_of(i, sib_id), send_sems.at[i, 0], recv_sems.at[i, 0], sib) for i in range(n)]
        arrive, forward = [], []
        for j, (px, py) in enumerate(chips):
            s = 4 * px + 2 * py
            arrive.append([_remote(mine(i), slot_of(i, s + c), send_sems.at[i, 1 + j], recv_sems.at[i, 1 + j], (px, py, c)) for i in range(n)])
            forward.append([_remote(slot_of(i, s + c), slot_of(i, s + c), send_sems.at[i, 4 + j], recv_sems.at[i, 4 + j], sib) for i in range(n)])
            from_sibling += [_remote(mine(i), slot_of(i, s + 1 - c), send_sems.at[i, 4 + j], recv_sems.at[i, 4 + j], sib) for i in range(n)]
        return local, first, arrive, forward, from_sibling

    def start(ins, outs, sems):
        local, first = copies(ins, outs, sems, False)
        for cp in local + first:
            cp.start()

    def finish(ins, outs, sems):
        local, first, arrive, forward, from_sibling = copies(ins, outs, sems, True)
        for landed, onward in zip(arrive, forward):
            for cp in landed:
                cp.wait_recv()
            for cp in onward:
                cp.start()
        for cp in from_sibling:
            cp.wait_recv()
        for cp in first + [cp for onward in forward for cp in onward]:
            cp.wait_send()
        for cp in local:
            cp.wait()

    return _Plan(list(xs) + list(into or []), [S((NDEV,) + a.shape, a.dtype) for a in xs],
                 [pltpu.SemaphoreType.DMA((n, NDEV - 1)), pltpu.SemaphoreType.DMA((n, NDEV - 1)), pltpu.SemaphoreType.DMA((n,))],
                 start, finish, {n + i: i for i in range(len(into or []))})


def _plan_sibling_swap(gs):
    n = len(gs)

    def copies(ins, outs, sems):
        send_sems, recv_sems = sems
        x, y, c, _ = _place()
        return [_remote(ins[i].at[2 * j + 1 - c], outs[i].at[j], send_sems.at[i, j], recv_sems.at[i, j], (x, y, 1 - c))
                for i in range(n) for j in range(NDEV // 2)]

    def start(ins, outs, sems):
        for cp in copies(ins, outs, sems):
            cp.start()

    def finish(ins, outs, sems):
        for cp in copies(ins, outs, sems):
            cp.wait()

    return _Plan(list(gs), [S((NDEV // 2,) + a.shape[1:], a.dtype) for a in gs],
                 [pltpu.SemaphoreType.DMA((n, NDEV // 2)), pltpu.SemaphoreType.DMA((n, NDEV // 2))], start, finish)


def _pair_add(g, r1, core, name):
    _, r, c = g.shape
    tr = r if r * c * 2 <= 2 * 1024 * 1024 else _row_tile(r, c)

    def body(core_ref, g_ref, r_ref, o_ref):
        o_ref[...] = (g_ref[...].astype(F32) + r_ref[...].astype(F32)).astype(BF16)

    return pl.pallas_call(
        body, name=name,
        grid_spec=pltpu.PrefetchScalarGridSpec(
            num_scalar_prefetch=1, grid=(NDEV // 2, r // tr),
            in_specs=[pl.BlockSpec((None, tr, c), lambda j, k, core_ref: (2 * j + core_ref[0], k, 0)),
                      pl.BlockSpec((None, tr, c), lambda j, k, core_ref: (j, k, 0))],
            out_specs=pl.BlockSpec((None, tr, c), lambda j, k, core_ref: (j, k, 0))),
        out_shape=S((NDEV // 2, r, c), BF16), compiler_params=_cp("parallel", "parallel"),
    )(core, g, r1)


def _plan_chip_swap(ps, rows=None, into=None):
    n = len(ps)

    def copies(ins, outs, sems, landed):
        send_sems, recv_sems, local_sems = sems
        x, y, c, _ = _place()
        mine = 2 * x + y
        part = lambda ref, s: ref.at[s] if rows is None else ref.at[s, pl.ds(*rows)]
        local = [pltpu.make_async_copy(part(ins[i], mine), part(outs[i], mine), local_sems.at[i]) for i in range(n)]
        remote = [_remote(part(ins[i], 2 * px + py), part(outs[i], 2 * px + py if landed else mine),
                          send_sems.at[i, j], recv_sems.at[i, j], (px, py, c))
                  for j, (px, py) in enumerate(_other_chips(x, y)) for i in range(n)]
        return local, remote

    def start(ins, outs, sems):
        local, remote = copies(ins, outs, sems, False)
        for cp in local + remote:
            cp.start()

    def finish(ins, outs, sems):
        local, remote = copies(ins, outs, sems, True)
        for cp in remote + local:
            cp.wait()

    return _Plan(list(ps) + list(into or []), [S(a.shape, a.dtype) for a in ps],
                 [pltpu.SemaphoreType.DMA((n, 3)), pltpu.SemaphoreType.DMA((n, 3)), pltpu.SemaphoreType.DMA((n,))], start, finish,
                 {n + i: i for i in range(len(into or []))})


def _pcall(body, name, args, in_specs, out_shape, out_specs, grid=(), scratch_shapes=(), sem=(), plans=()):
    n_in, n_out, n_scr = len(args), len(out_shape), len(scratch_shapes)
    counts = [(len(p.inputs), len(p.out_shapes), len(p.sem_shapes)) for p in plans]
    c_args = [a for p in plans for a in p.inputs]
    c_outs = [s for p in plans for s in p.out_shapes]
    c_sems = [s for p in plans for s in p.sem_shapes]

    def wrapped(*refs):
        cuts = [n_in, len(c_args), n_out, len(c_outs), n_scr, len(c_sems)]
        ins, c_in, outs, c_out, scr, c_sem = [refs[sum(cuts[:k]):sum(cuts[:k + 1])] for k in range(6)]

        def halves(which):
            a = b = s = 0
            for p, (na, nb, ns) in zip(plans, counts):
                getattr(p, which)(c_in[a:a + na], c_out[b:b + nb], c_sem[s:s + ns])
                a, b, s = a + na, b + nb, s + ns

        if not plans:
            body(*ins, *outs, *scr)
        elif not grid:
            halves("start")
            body(*ins, *outs, *scr)
            halves("finish")
        else:
            first = functools.reduce(jnp.logical_and, [pl.program_id(a) == 0 for a in range(len(grid))])
            last = functools.reduce(jnp.logical_and, [pl.program_id(a) == grid[a] - 1 for a in range(len(grid))])
            pl.when(first)(lambda: halves("start"))
            body(*ins, *outs, *scr)
            pl.when(last)(lambda: halves("finish"))

    aliases, a, b = {}, n_in, n_out
    for p, (na, nb, _) in zip(plans, counts):
        aliases.update({a + k: b + v for k, v in p.aliases.items()})
        a, b = a + na, b + nb
    res = pl.pallas_call(
        wrapped, name=name, grid=grid, in_specs=list(in_specs) + [ANY] * len(c_args),
        out_shape=list(out_shape) + c_outs, out_specs=list(out_specs) + [ANY] * len(c_outs),
        scratch_shapes=list(scratch_shapes) + c_sems, input_output_aliases=aliases,
        compiler_params=_cp(*(("arbitrary",) * len(grid) if plans else sem)),
    )(*args, *c_args)
    c_res, b = [], n_out
    for _, nb, _ in counts:
        c_res.append(res[b:b + nb])
        b += nb
    return res[:n_out], c_res


def _exchange(plans, name):
    return _pcall(lambda: None, name, [], [], [], [], plans=plans)[1]


def _ada_rows(c, w_ada, b_ada, name, plans=()):
    d, cols = c.shape[1], w_ada.shape[1]
    gather_c = _plan_all_to_all([c], True)
    gather_p = _plan_all_to_all([S((NDEV, cols), F32)], True)
    n_sem = len(gather_c.sem_shapes)

    def body(c_ref, w_ref, b_ref, ada_ref, act_ref, c_all, p_mine, p_all, *sems):
        gather_c.start([c_ref], [c_all], sems[:n_sem])
        gather_c.finish([c_ref], [c_all], sems[:n_sem])
        for s in range(NDEV):
            cc = c_all[s]
            act_ref[s:s + 1, :] = cc * _sigmoid(cc)
        p_mine[...] = _dot(act_ref[...].astype(BF16), w_ref[...].astype(BF16))
        gather_p.start([p_mine], [p_all], sems[n_sem:])
        gather_p.finish([p_mine], [p_all], sems[n_sem:])
        me = _place()[3]
        for s in range(NDEV):
            ada_ref[:, s * cols:(s + 1) * cols] = p_all[s, pl.ds(me, 1), :] + b_ref[:, s * cols:(s + 1) * cols]

    whole = pl.BlockSpec(memory_space=pltpu.VMEM)
    return _pcall(body, name, [c, w_ada, b_ada], [whole] * 3, [S((1, NDEV * cols), F32), S((NDEV, d), F32)], [whole] * 2,
                  scratch_shapes=[pltpu.VMEM((NDEV,) + c.shape, F32), pltpu.VMEM((NDEV, cols), F32), pltpu.VMEM((NDEV, NDEV, cols), F32)]
                  + gather_c.sem_shapes + gather_p.sem_shapes, plans=plans)


def _ffn_in(x, ada9, nrm, w_in, sh_row, sc_row, name, plans=()):
    t, d = x.shape
    nb, bw = w_in.shape[0] // 2, w_in.shape[1]
    tm = min(WIDE_ROW_TILE, t)

    def body(x_ref, ada_ref, n_ref, wg_ref, wu_ref, h_ref, gu_ref, s_ref):
        @pl.when(pl.program_id(1) == 0)
        def _():
            _, _, h = _normmod(x_ref[...], n_ref[...], ada_ref[sc_row:sc_row + 1, :], ada_ref[sh_row:sh_row + 1, :])
            h_ref[...] = h.astype(BF16)

        h = h_ref[...]
        g = _dot_nt(h, wg_ref[...])
        u = _dot_nt(h, wu_ref[...])
        gu_ref[0] = g.astype(BF16)
        gu_ref[1] = u.astype(BF16)
        s_ref[...] = (g * _sigmoid(g) * u).astype(BF16)

    return _pcall(
        body, name, [x, ada9, nrm, w_in, w_in], grid=(t // tm, nb),
        in_specs=[pl.BlockSpec((tm, d), lambda i, j: (i, 0)), pl.BlockSpec((N_ADA, d), lambda i, j: (0, 0)),
                  pl.BlockSpec((1, d), lambda i, j: (0, 0)),
                  pl.BlockSpec((None, bw, d), lambda i, j: (j, 0, 0)), pl.BlockSpec((None, bw, d), lambda i, j: (j + nb, 0, 0))],
        out_shape=[S((t, d), BF16), S((2, nb, t, bw), BF16), S((nb, t, bw), BF16)],
        out_specs=[pl.BlockSpec((tm, d), lambda i, j: (i, 0)), pl.BlockSpec((2, None, tm, bw), lambda i, j: (0, j, i, 0)),
                   pl.BlockSpec((None, tm, bw), lambda i, j: (j, i, 0))],
        sem=("parallel", "arbitrary"), plans=plans)


def _ffn_out(s, w_out, x, ada9, g_row, res_scale, name, plans=()):
    nb, t, bw = s.shape
    d = x.shape[1]
    tm = min(ROW_TILE, t)

    def body(s_ref, w_ref, x_ref, ada_ref, xo_ref, f_ref):
        acc = _dot(s_ref[0], w_ref[0])
        for b in range(1, nb):
            acc = acc + _dot(s_ref[b], w_ref[b])
        f_ref[...] = acc.astype(BF16)
        xo_ref[...] = x_ref[...] + (res_scale * ada_ref[g_row:g_row + 1, :]) * acc

    return _pcall(
        body, name, [s, w_out, x, ada9], grid=(t // tm,),
        in_specs=[pl.BlockSpec((nb, tm, bw), lambda i: (0, i, 0)), pl.BlockSpec((nb, bw, d), lambda i: (0, 0, 0)),
                  pl.BlockSpec((tm, d), lambda i: (i, 0)), pl.BlockSpec((N_ADA, d), lambda i: (0, 0))],
        out_shape=[S((t, d), F32), S((t, d), BF16)],
        out_specs=[pl.BlockSpec((tm, d), lambda i: (i, 0)), pl.BlockSpec((tm, d), lambda i: (i, 0))],
        sem=("parallel",), plans=plans)


def _ffn_bwd_ds(dxo, f, ada9, w_out, gu, g_row, res_scale, name, plans=()):
    t, d = dxo.shape
    nb, bw = w_out.shape[0], w_out.shape[1]
    tm = min(WIDE_ROW_TILE, t)
    ni = t // tm

    def body(dxo_ref, f_ref, ada_ref, w_ref, gu_ref, da_ref, dg_ref, dw_ref, df_ref, acc_ref):
        i, j = pl.program_id(0), pl.program_id(1)

        @pl.when((i == 0) & (j == 0))
        def _():
            dg_ref[...] = jnp.zeros_like(dg_ref)

        @pl.when(j == 0)
        def _():
            dxo_t = dxo_ref[...]
            df_ref[...] = ((res_scale * ada_ref[g_row:g_row + 1, :]) * dxo_t).astype(BF16)
            dg_ref[...] += res_scale * _rsum8(dxo_t * f_ref[...].astype(F32))

        df = df_ref[...]
        ds = _dot_nt(df, w_ref[...])
        g = gu_ref[0].astype(F32)
        u = gu_ref[1].astype(F32)
        sg = _sigmoid_tanh(g)
        silu = g * sg
        da_ref[0] = (ds * u * (sg * (1.0 + g * (1.0 - sg)))).astype(BF16)
        da_ref[1] = (ds * silu).astype(BF16)
        part = _dot_tn((silu * u).astype(BF16), df)

        @pl.when(i == 0)
        def _():
            acc_ref[j] = part

        @pl.when(i > 0)
        def _():
            acc_ref[j] += part

        @pl.when(i == ni - 1)
        def _():
            dw_ref[...] = acc_ref[j].astype(BF16)

    return _pcall(
        body, name, [dxo, f, ada9, w_out, gu], grid=(ni, nb),
        in_specs=[pl.BlockSpec((tm, d), lambda i, j: (i, 0)), pl.BlockSpec((tm, d), lambda i, j: (i, 0)),
                  pl.BlockSpec((N_ADA, d), lambda i, j: (0, 0)), pl.BlockSpec((None, bw, d), lambda i, j: (j, 0, 0)),
                  pl.BlockSpec((2, None, tm, bw), lambda i, j: (0, j, i, 0))],
        out_shape=[S((2, nb, t, bw), BF16), S((8, d), F32), S((nb, bw, d), BF16)],
        out_specs=[pl.BlockSpec((2, None, tm, bw), lambda i, j: (0, j, i, 0)), pl.BlockSpec((8, d), lambda i, j: (0, 0)),
                   pl.BlockSpec((None, bw, d), lambda i, j: (jnp.where(i == ni - 1, j, 0), 0, 0))],
        scratch_shapes=[pltpu.VMEM((tm, d), BF16), pltpu.VMEM((nb, bw, d), F32)],
        sem=("arbitrary", "arbitrary"), plans=plans)


def _ffn_bwd_dh(da, w_in, x, dxo, ada9, nrm, sh_row, sc_row, name, plans=()):
    t, d = x.shape
    nb, bw = w_in.shape[0] // 2, w_in.shape[1]
    tm = min(WIDE_ROW_TILE, t)

    def body(da_ref, wg_ref, wu_ref, x_ref, dxo_ref, ada_ref, n_ref, dx_ref, dsh_ref, dsc_ref, dn_ref, acc_ref):
        i, j = pl.program_id(0), pl.program_id(1)

        @pl.when((i == 0) & (j == 0))
        def _():
            dsh_ref[...] = jnp.zeros_like(dsh_ref)
            dsc_ref[...] = jnp.zeros_like(dsc_ref)
            dn_ref[...] = jnp.zeros_like(dn_ref)

        part = _dot(da_ref[0], wg_ref[...]) + _dot(da_ref[1], wu_ref[...])

        @pl.when(j == 0)
        def _():
            acc_ref[...] = part

        @pl.when(j > 0)
        def _():
            acc_ref[...] += part

        @pl.when(j == nb - 1)
        def _():
            for r0 in range(0, tm, min(EPILOGUE_ROWS, tm)):
                rows = slice(r0, r0 + min(EPILOGUE_ROWS, tm))
                dx, tsh, tsc, tn = _normmod_bwd(acc_ref[rows, :], x_ref[rows, :], n_ref[...], ada_ref[sc_row:sc_row + 1, :])
                dx_ref[rows, :] = dxo_ref[rows, :] + dx
                dsh_ref[...] += _rsum8(tsh)
                dsc_ref[...] += _rsum8(tsc)
                dn_ref[...] += _rsum8(tn)

    vec = pl.BlockSpec((8, d), lambda i, j: (0, 0))
    return _pcall(
        body, name, [da, w_in, w_in, x, dxo, ada9, nrm], grid=(t // tm, nb),
        in_specs=[pl.BlockSpec((2, None, tm, bw), lambda i, j: (0, j, i, 0)),
                  pl.BlockSpec((None, bw, d), lambda i, j: (j, 0, 0)), pl.BlockSpec((None, bw, d), lambda i, j: (j + nb, 0, 0)),
                  pl.BlockSpec((tm, d), lambda i, j: (i, 0)), pl.BlockSpec((tm, d), lambda i, j: (i, 0)),
                  pl.BlockSpec((N_ADA, d), lambda i, j: (0, 0)), pl.BlockSpec((1, d), lambda i, j: (0, 0))],
        out_shape=[S((t, d), F32), S((8, d), F32), S((8, d), F32), S((8, d), F32)],
        out_specs=[pl.BlockSpec((tm, d), lambda i, j: (i, 0)), vec, vec, vec],
        scratch_shapes=[pltpu.VMEM((tm, d), F32)],
        sem=("arbitrary", "arbitrary"), plans=plans)


def _tn_matmul(a, b, a_block, a_map, b_block, b_map, out_shape, out_block, out_map, nblk, name, plans=()):
    t = a.shape[-2]
    tk = min(K_TILE, t)
    nk = t // tk

    def body(a_ref, b_ref, o_ref, acc_ref):
        k = pl.program_id(1)
        for q in (range(a_ref.shape[0]) if len(a_ref.shape) == 3 else [Ellipsis]):
            part = _dot_tn(a_ref[q], b_ref[...])

            @pl.when(k == 0)
            def _():
                acc_ref[q] = part

            @pl.when(k > 0)
            def _():
                acc_ref[q] += part

        @pl.when(k == nk - 1)
        def _():
            o_ref[...] = acc_ref[...].astype(BF16)

    (out,), moved = _pcall(
        body, name, [a, b], grid=(nblk, nk),
        in_specs=[pl.BlockSpec(a_block(tk), a_map), pl.BlockSpec(b_block(tk), b_map)],
        out_shape=[S(out_shape, BF16)], out_specs=[pl.BlockSpec(out_block, out_map)],
        scratch_shapes=[pltpu.VMEM(tuple(n for n in out_block if n is not None), F32)],
        sem=("parallel", "arbitrary"), plans=plans)
    return out, moved


def _ffn_backward(dxo, x_in, h, gu, f, ada9, nrm, w_in, w_out, rows, core, name, ds_plans=(), dwin_plans=(), later=False):
    sh_row, sc_row, g_row = rows
    _, nb, t, bw = gu.shape
    d = x_in.shape[1]
    (da, dg, dw_out), ds_moved = _ffn_bwd_ds(dxo, f, ada9, w_out, gu, g_row, 0.5, name + "_ds", plans=ds_plans)
    dw_out = dw_out.reshape(NDEV, -1, d)
    dw_in, ((half_out,), *dwin_moved) = _tn_matmul(
        da.reshape(2 * nb, t, bw), h, lambda tk: (2, tk, bw), lambda sb, k: (sb, k, 0), lambda tk: (tk, d), lambda sb, k: (k, 0),
        (2 * nb, bw, d), (2, bw, d), lambda sb, k: (sb, 0, 0), nb, name + "_dwin",
        plans=[_plan_sibling_swap([dw_out])] + list(dwin_plans(ds_moved) if callable(dwin_plans) else dwin_plans))
    sum_out = _pair_add(dw_out, half_out, core, name + "_dwout_add")
    if later:
        (dx, dsh, dsc, dn), ((half_in,),) = _ffn_bwd_dh(da, w_in, x_in, dxo, ada9, nrm, sh_row, sc_row, name + "_dh",
                                                       plans=[_plan_sibling_swap([dw_in])])
        return dx, _pair_add(dw_in, half_in, core, name + "_dwin_add"), sum_out, (dsh, dsc, dg, dn), ds_moved, dwin_moved
    ((half_in,),) = _exchange([_plan_sibling_swap([dw_in])], name + "_dwin_swap")
    sum_in = _pair_add(dw_in, half_in, core, name + "_dwin_add")
    (dx, dsh, dsc, dn), ((recv_in,),) = _ffn_bwd_dh(da, w_in, x_in, dxo, ada9, nrm, sh_row, sc_row, name + "_dh",
                                                   plans=[_plan_chip_swap([sum_in])])
    return dx, recv_in, sum_out, (dsh, dsc, dg, dn), ds_moved, dwin_moved


def _mix_in(x, ada9, nrm, w, widths, dts, name, plans=()):
    t, d = x.shape
    n = w.shape[0]
    tm = min(ROW_TILE, t)
    starts = [sum(widths[:i]) for i in range(len(widths))]

    def body(x_ref, ada_ref, n_ref, w_ref, h_ref, *out_refs):
        _, _, h = _normmod(x_ref[...], n_ref[...], ada_ref[4:5, :], ada_ref[3:4, :])
        hb = h.astype(BF16)
        h_ref[...] = hb
        for o_ref, st, wd in zip(out_refs, starts, widths):
            o_ref[...] = _dot_nt(hb, w_ref[st:st + wd, :]).astype(o_ref.dtype)

    return _pcall(
        body, name, [x, ada9, nrm, w], grid=(t // tm,),
        in_specs=[pl.BlockSpec((tm, d), lambda i: (i, 0)), pl.BlockSpec((N_ADA, d), lambda i: (0, 0)),
                  pl.BlockSpec((1, d), lambda i: (0, 0)), pl.BlockSpec((n, d), lambda i: (0, 0))],
        out_shape=[S((t, d), BF16)] + [S((t, wd), dt) for wd, dt in zip(widths, dts)],
        out_specs=[pl.BlockSpec((tm, d), lambda i: (i, 0))] + [pl.BlockSpec((tm, wd), lambda i: (i, 0)) for wd in widths],
        sem=("parallel",), plans=plans)


def _tri(lower):
    r = lax.broadcasted_iota(jnp.int32, (CHUNK, CHUNK), 0)
    c = lax.broadcasted_iota(jnp.int32, (CHUNK, CHUNK), 1)
    return (r >= c) if lower else (c >= r)


def _dot_01(m, x):
    hi = x.astype(BF16)
    r1 = x - hi.astype(F32)
    mid = r1.astype(BF16)
    lo = (r1 - mid.astype(F32)).astype(BF16)
    return _dot(m, hi) + _dot(m, mid) + _dot(m, lo)


def _gla_chunk_terms(q, k, lg, low01):
    b = _dot_01(low01, lg)
    bl = b[CHUNK - 1:CHUNK, :]
    r = 0.5 * bl
    eb, ebl, em, en = jnp.exp(b), jnp.exp(bl - b), jnp.exp(b - r), jnp.exp(r - b)
    return eb, ebl, em, en, jnp.exp(bl), q * eb, k * ebl, q * em, k * en


def _scores(qm_h, knp, qk1_h):
    r = lax.broadcasted_iota(jnp.int32, (CHUNK, CHUNK), 0)
    c = lax.broadcasted_iota(jnp.int32, (CHUNK, CHUNK), 1)
    p = jnp.where(r > c, _dot_nt(qm_h, knp), 0.0)
    return jnp.where(r == c, jnp.sum(qk1_h, axis=1, keepdims=True), p)


def _gla_fwd(qk, v, gl, wg, bg, heads, name, plans=()):
    t = qk.shape[0]
    kw, vw = qk.shape[1] // 2, v.shape[1]
    dk, dv = kw // heads, vw // heads
    assert dk == 64 and dv == 128 and kw % 128 == 0
    gt = min(ROW_TILE, t)
    nc = gt // CHUNK
    scale = dk ** -0.5

    def body(qk_ref, v_ref, gl_ref, wg_ref, bg_ref, o_ref, lg_ref, sall_ref, st_ref):
        @pl.when(pl.program_id(0) == 0)
        def _():
            st_ref[...] = jnp.zeros_like(st_ref)

        gk = _dot(gl_ref[...].astype(BF16), wg_ref[...]) + bg_ref[...]
        lg_ref[...] = (jnp.minimum(gk, 0.0) - jnp.log(1.0 + jnp.exp(-jnp.abs(gk)))) / GATE_NORMALIZER
        low01 = _tri(True).astype(BF16)
        lane = lax.broadcasted_iota(jnp.int32, (CHUNK, LANES), 1)

        def chunk(ci, carry):
            rows = pl.ds(pl.multiple_of(ci * CHUNK, CHUNK), CHUNK)
            q = qk_ref[rows, 0:kw] * scale
            k = qk_ref[rows, kw:2 * kw]
            qk1 = q.astype(BF16).astype(F32) * k.astype(BF16).astype(F32)
            eb, ebl, em, en, ebl_row, qe, ke, qm, kn = _gla_chunk_terms(q, k, lg_ref[rows, :], low01)
            for h in range(heads):
                lanes = slice(LANES * (h // 2), LANES * (h // 2) + LANES)
                own = (lane < 64) if h % 2 == 0 else (lane >= 64)
                knp = kn[:, lanes].astype(BF16)
                qm_h = jnp.where(own, qm[:, lanes], 0.0).astype(BF16)
                qe_h = jnp.where(own, qe[:, lanes], 0.0).astype(BF16)
                ke_h = jnp.where(own, ke[:, lanes], 0.0).astype(BF16)
                v_h = v_ref[rows, h * dv:(h + 1) * dv]
                st = st_ref[h]
                sall_ref[ci, h] = st
                p = _scores(qm_h, knp, jnp.where(own, qk1[:, lanes], 0.0))
                o_ref[rows, h * dv:(h + 1) * dv] = _dot(p.astype(BF16), v_h) + _dot_nt(qe_h, st.astype(BF16))
                st_ref[h] = st * ebl_row[:, lanes] + _dot_tn(v_h, ke_h)
            return carry

        lax.fori_loop(0, nc, chunk, 0, unroll=True)

    return _pcall(
        body, name, [qk, v, gl, wg, bg], grid=(t // gt,),
        in_specs=[pl.BlockSpec((gt, 2 * kw), lambda i: (i, 0)), pl.BlockSpec((gt, vw), lambda i: (i, 0)),
                  pl.BlockSpec((gt, LANES), lambda i: (i, 0)), pl.BlockSpec((LANES, kw), lambda i: (0, 0)),
                  pl.BlockSpec((1, kw), lambda i: (0, 0))],
        out_shape=[S((t, vw), F32), S((t, kw), F32), S((t // CHUNK, heads, dv, LANES), F32)],
        out_specs=[pl.BlockSpec((gt, vw), lambda i: (i, 0)), pl.BlockSpec((gt, kw), lambda i: (i, 0)),
                   pl.BlockSpec((nc, heads, dv, LANES), lambda i: (i, 0, 0, 0))],
        scratch_shapes=[pltpu.VMEM((heads, dv, LANES), F32)],
        sem=("arbitrary",), plans=plans)


def _gla_bwd(qk, v, lg, do, sall, gl, wg, heads, name, plans=()):
    t = qk.shape[0]
    kw, vw = qk.shape[1] // 2, v.shape[1]
    dk, dv = kw // heads, vw // heads
    gt = min(ROW_TILE, t)
    nc = gt // CHUNK
    nt = t // gt
    scale = dk ** -0.5

    def body(qk_ref, v_ref, lg_ref, do_ref, sall_ref, gl_ref, wg_ref, dqk_ref, dv_ref, dgl_ref, dwg_ref, dbg_ref, dst_ref, dgk_ref):
        @pl.when(pl.program_id(0) == 0)
        def _():
            dst_ref[...] = jnp.zeros_like(dst_ref)
            dwg_ref[...] = jnp.zeros_like(dwg_ref)
            dbg_ref[...] = jnp.zeros_like(dbg_ref)

        low01 = _tri(True).astype(BF16)
        up01 = _tri(False).astype(BF16)
        causal = _tri(True)
        lane = lax.broadcasted_iota(jnp.int32, (CHUNK, LANES), 1)
        last_row = lax.broadcasted_iota(jnp.int32, (CHUNK, kw), 0) == CHUNK - 1

        def chunk(cj, carry):
            ci = nc - 1 - cj
            rows = pl.ds(pl.multiple_of(ci * CHUNK, CHUNK), CHUNK)
            q = qk_ref[rows, 0:kw] * scale
            k = qk_ref[rows, kw:2 * kw]
            qk1 = q.astype(BF16).astype(F32) * k.astype(BF16).astype(F32)
            lgc = lg_ref[rows, :]
            eb, ebl, em, en, ebl_row, qe, ke, qm, kn = _gla_chunk_terms(q, k, lgc, low01)
            dqe, dqm, dkn, dke, drow = [], [], [], [], []
            for pr in range(kw // LANES):
                lanes = slice(LANES * pr, LANES * pr + LANES)
                knp = kn[:, lanes].astype(BF16)
                parts = []
                for half in range(2):
                    h = 2 * pr + half
                    own = (lane < 64) if half == 0 else (lane >= 64)
                    qm_h = jnp.where(own, qm[:, lanes], 0.0).astype(BF16)
                    qe_h = jnp.where(own, qe[:, lanes], 0.0).astype(BF16)
                    ke_h = jnp.where(own, ke[:, lanes], 0.0).astype(BF16)
                    v_h = v_ref[rows, h * dv:(h + 1) * dv]
                    do_h = do_ref[rows, h * dv:(h + 1) * dv]
                    st = sall_ref[ci, h]
                    dst = dst_ref[h]
                    stb, dstb = st.astype(BF16), dst.astype(BF16)
                    p = _scores(qm_h, knp, jnp.where(own, qk1[:, lanes], 0.0)).astype(BF16)
                    dp = jnp.where(causal, _dot_nt(do_h, v_h), 0.0).astype(BF16)
                    dv_ref[rows, h * dv:(h + 1) * dv] = (_dot_tn(p, do_h) + _dot_nt(ke_h, dstb)).astype(BF16)
                    parts.append((jnp.where(own, _dot(dp, knp), 0.0), _dot_tn(dp, qm_h), _dot(do_h, stb), _dot(v_h, dstb),
                                  jnp.sum(st * dst, axis=0, keepdims=True)))
                    dst_ref[h] = dst * ebl_row[:, lanes] + _dot_tn(do_h, qe_h)
                dqm.append(parts[0][0] + parts[1][0])
                dkn.append(parts[0][1] + parts[1][1])
                dqe.append(parts[0][2] + parts[1][2])
                dke.append(parts[0][3] + parts[1][3])
                drow.append(parts[0][4] + parts[1][4])
            dqm, dkn, dqe, dke, drow = [jnp.concatenate(a, axis=1) for a in (dqm, dkn, dqe, dke, drow)]
            dqk_ref[rows, 0:kw] = ((dqe * eb + dqm * em) * scale).astype(BF16)
            dqk_ref[rows, kw:2 * kw] = (dke * ebl + dkn * en).astype(BF16)
            tke = dke * ke
            db = dqe * qe + dqm * qm - dkn * kn - tke
            dbl = jnp.sum(tke, axis=0, keepdims=True) + drow * ebl_row
            db = db + jnp.where(last_row, dbl, 0.0)
            dlg = _dot_01(up01, db)
            dgk_ref[rows, :] = dlg * ((1.0 - jnp.exp(GATE_NORMALIZER * lgc)) / GATE_NORMALIZER)
            return carry

        lax.fori_loop(0, nc, chunk, 0, unroll=True)
        dgk = dgk_ref[...]
        dgkb = dgk.astype(BF16)
        dgl_ref[...] = _dot_nt(dgkb, wg_ref[...]).astype(BF16)
        dwg_ref[...] += _dot_tn(gl_ref[...].astype(BF16), dgkb)
        dbg_ref[...] += _rsum8(dgk)

    rev = lambda i: (nt - 1 - i, 0)
    return _pcall(
        body, name, [qk, v, lg, do, sall, gl, wg], grid=(nt,),
        in_specs=[pl.BlockSpec((gt, 2 * kw), rev), pl.BlockSpec((gt, vw), rev), pl.BlockSpec((gt, kw), rev),
                  pl.BlockSpec((gt, vw), rev), pl.BlockSpec((nc, heads, dv, LANES), lambda i: (nt - 1 - i, 0, 0, 0)),
                  pl.BlockSpec((gt, LANES), rev), pl.BlockSpec((LANES, kw), lambda i: (0, 0))],
        out_shape=[S((t, 2 * kw), BF16), S((t, vw), BF16), S((t, LANES), BF16), S((LANES, kw), F32), S((8, kw), F32)],
        out_specs=[pl.BlockSpec((gt, 2 * kw), rev), pl.BlockSpec((gt, vw), rev), pl.BlockSpec((gt, LANES), rev),
                   pl.BlockSpec((LANES, kw), lambda i: (0, 0)), pl.BlockSpec((8, kw), lambda i: (0, 0))],
        scratch_shapes=[pltpu.VMEM((heads, dv, LANES), F32), pltpu.VMEM((gt, kw), F32)],
        sem=("arbitrary",), plans=plans)


def _conv_taps(cx_ref, halo_ref, first, cw):
    tm = cx_ref.shape[0]
    u = cx_ref[:, cw:2 * cw].astype(F32) * cx_ref[:, 2 * cw:3 * cw].astype(F32)
    uh = halo_ref[:, cw:2 * cw].astype(F32) * halo_ref[:, 2 * cw:3 * cw].astype(F32)
    uh = jnp.where(first, 0.0, uh)
    before1, before2 = uh[BF16_ROWS - 1:BF16_ROWS, :], uh[BF16_ROWS - 2:BF16_ROWS - 1, :]
    row = lax.broadcasted_iota(jnp.int32, (tm, cw), 0)
    u1 = jnp.where(row == 0, before1, pltpu.roll(u, 1, 0))
    u2 = jnp.where(row == 0, before2, jnp.where(row == 1, before1, pltpu.roll(u, 2, 0)))
    return u, u1, u2


def _head_norm(o_h, gn):
    rstd = lax.rsqrt(jnp.mean(o_h * o_h, axis=-1, keepdims=True) + EPS)
    ohat = o_h * rstd
    return ohat, rstd, ohat * gn


def _mix_out(cx, o, go, conv_w, gn, w_out, x, ada9, heads, name, plans=()):
    t, d = x.shape
    cw, vw = conv_w.shape[1], o.shape[1]
    dv = vw // heads
    tm = min(ROW_TILE, t)

    def body(cx_ref, halo_ref, o_ref, go_ref, cwt_ref, gn_ref, w_ref, x_ref, ada_ref, xo_ref, m_ref, y_ref):
        u, u1, u2 = _conv_taps(cx_ref, halo_ref, pl.program_id(0) == 0, cw)
        yc = cwt_ref[0:1, :] * u2 + cwt_ref[1:2, :] * u1 + cwt_ref[2:3, :] * u
        y_ref[:, 0:cw] = (cx_ref[:, 0:cw].astype(F32) * yc).astype(BF16)
        for h in range(heads):
            cols = slice(h * dv, (h + 1) * dv)
            _, _, on = _head_norm(o_ref[:, cols], gn_ref[...])
            g = go_ref[:, cols].astype(F32)
            y_ref[:, cw + h * dv:cw + (h + 1) * dv] = (on * (g * _sigmoid(g))).astype(BF16)
        m = _dot(y_ref[...], w_ref[...])
        m_ref[...] = m.astype(BF16)
        xo_ref[...] = x_ref[...] + ada_ref[5:6, :] * m

    return _pcall(
        body, name, [cx, cx, o, go, conv_w, gn, w_out, x, ada9], grid=(t // tm,),
        in_specs=[pl.BlockSpec((tm, 3 * cw), lambda i: (i, 0)),
                  pl.BlockSpec((BF16_ROWS, 3 * cw), lambda i: (jnp.maximum(i * (tm // BF16_ROWS) - 1, 0), 0)),
                  pl.BlockSpec((tm, vw), lambda i: (i, 0)), pl.BlockSpec((tm, vw), lambda i: (i, 0)),
                  pl.BlockSpec((3, cw), lambda i: (0, 0)), pl.BlockSpec((1, dv), lambda i: (0, 0)),
                  pl.BlockSpec((cw + vw, d), lambda i: (0, 0)), pl.BlockSpec((tm, d), lambda i: (i, 0)),
                  pl.BlockSpec((N_ADA, d), lambda i: (0, 0))],
        out_shape=[S((t, d), F32), S((t, d), BF16), S((t, cw + vw), BF16)],
        out_specs=[pl.BlockSpec((tm, d), lambda i: (i, 0)), pl.BlockSpec((tm, d), lambda i: (i, 0)),
                   pl.BlockSpec((tm, cw + vw), lambda i: (i, 0))],
        sem=("parallel",), plans=plans)


def _mix_bwd_a(dxo, m, ycat, ada9, w_out, cx, o, go, conv_w, gn, heads, name, plans=()):
    t, d = dxo.shape
    cw, vw = conv_w.shape[1], o.shape[1]
    dv = vw // heads
    tm = min(ROW_TILE, t)
    nt = t // tm

    def body(dxo_ref, m_ref, y_ref, ada_ref, w_ref, cx_ref, halo_ref, o_ref, go_ref, cwt_ref, gn_ref,
             dw_ref, dyc_ref, dcb_ref, do_ref, dgo_ref, dg_ref, dcw_ref, dgn_ref, acc_ref):
        @pl.when(pl.program_id(0) == 0)
        def _():
            dg_ref[...] = jnp.zeros_like(dg_ref)
            dcw_ref[...] = jnp.zeros_like(dcw_ref)
            dgn_ref[...] = jnp.zeros_like(dgn_ref)

        dxo_t = dxo_ref[...]
        dmb = (ada_ref[5:6, :] * dxo_t).astype(BF16)
        part = _dot_tn(y_ref[...], dmb)

        @pl.when(pl.program_id(0) == 0)
        def _():
            acc_ref[...] = part

        @pl.when(pl.program_id(0) > 0)
        def _():
            acc_ref[...] += part

        @pl.when(pl.program_id(0) == nt - 1)
        def _():
            dw_ref[...] = acc_ref[...].astype(BF16)

        dg_ref[...] += _rsum8(dxo_t * m_ref[...].astype(F32))
        dy = _dot_nt(dmb, w_ref[...])
        u, u1, u2 = _conv_taps(cx_ref, halo_ref, pl.program_id(0) == 0, cw)
        yc = cwt_ref[0:1, :] * u2 + cwt_ref[1:2, :] * u1 + cwt_ref[2:3, :] * u
        dyv = dy[:, 0:cw]
        dcb_ref[...] = (dyv * yc).astype(BF16)
        dyc = dyv * cx_ref[:, 0:cw].astype(F32)
        dyc_ref[...] = dyc
        dcw_ref[0] += _rsum8(dyc * u2)
        dcw_ref[1] += _rsum8(dyc * u1)
        dcw_ref[2] += _rsum8(dyc * u)
        for h in range(heads):
            cols = slice(h * dv, (h + 1) * dv)
            ohat, rstd, on = _head_norm(o_ref[:, cols], gn_ref[...])
            g = go_ref[:, cols].astype(F32)
            sg = _sigmoid_tanh(g)
            dyg = dy[:, cw + h * dv:cw + (h + 1) * dv]
            dgo_ref[:, cols] = (dyg * on * (sg * (1.0 + g * (1.0 - sg)))).astype(BF16)
            don = dyg * (g * sg)
            dgn_ref[...] += _rsum8(don * ohat)
            tt = don * gn_ref[...]
            do_ref[:, cols] = (rstd * (tt - ohat * jnp.mean(tt * ohat, axis=-1, keepdims=True))).astype(BF16)

    return _pcall(
        body, name, [dxo, m, ycat, ada9, w_out, cx, cx, o, go, conv_w, gn], grid=(nt,),
        in_specs=[pl.BlockSpec((tm, d), lambda i: (i, 0)), pl.BlockSpec((tm, d), lambda i: (i, 0)),
                  pl.BlockSpec((tm, cw + vw), lambda i: (i, 0)),
                  pl.BlockSpec((N_ADA, d), lambda i: (0, 0)), pl.BlockSpec((cw + vw, d), lambda i: (0, 0)),
                  pl.BlockSpec((tm, 3 * cw), lambda i: (i, 0)),
                  pl.BlockSpec((BF16_ROWS, 3 * cw), lambda i: (jnp.maximum(i * (tm // BF16_ROWS) - 1, 0), 0)),
                  pl.BlockSpec((tm, vw), lambda i: (i, 0)), pl.BlockSpec((tm, vw), lambda i: (i, 0)),
                  pl.BlockSpec((3, cw), lambda i: (0, 0)), pl.BlockSpec((1, dv), lambda i: (0, 0))],
        out_shape=[S((cw + vw, d), BF16), S((t, cw), F32), S((t, cw), BF16), S((t, vw), BF16), S((t, vw), BF16),
                   S((8, d), F32), S((3, 8, cw), F32), S((8, dv), F32)],
        out_specs=[pl.BlockSpec((cw + vw, d), lambda i: (0, 0)), pl.BlockSpec((tm, cw), lambda i: (i, 0)),
                   pl.BlockSpec((tm, cw), lambda i: (i, 0)), pl.BlockSpec((tm, vw), lambda i: (i, 0)),
                   pl.BlockSpec((tm, vw), lambda i: (i, 0)), pl.BlockSpec((8, d), lambda i: (0, 0)),
                   pl.BlockSpec((3, 8, cw), lambda i: (0, 0, 0)), pl.BlockSpec((8, dv), lambda i: (0, 0))],
        scratch_shapes=[pltpu.VMEM((cw + vw, d), F32)],
        sem=("arbitrary",), plans=plans)


def _mix_bwd_b(dyc, cx, dcb, dqk, dvv, dgo, dgl, conv_w, w, x, dxo, ada9, nrm, name, plans=()):
    t, d = x.shape
    cw = conv_w.shape[1]
    n = w.shape[0]
    tm = min(ROW_TILE, t)
    nt = t // tm
    pieces = [dcb.shape[1], cw, cw, dqk.shape[1], dvv.shape[1], dgo.shape[1], dgl.shape[1]]
    assert sum(pieces) == n

    def body(dyc_ref, nxt_ref, cx_ref, dcb_ref, dqk_ref, dv_ref, dgo_ref, dgl_ref, cwt_ref, w_ref, x_ref, dxo_ref, ada_ref, n_ref,
             dx_ref, dp_ref, dsh_ref, dsc_ref, dn_ref):
        i = pl.program_id(0)

        @pl.when(i == 0)
        def _():
            dsh_ref[...] = jnp.zeros_like(dsh_ref)
            dsc_ref[...] = jnp.zeros_like(dsc_ref)
            dn_ref[...] = jnp.zeros_like(dn_ref)

        dyc_t = dyc_ref[...]
        nxt = jnp.where(i == nt - 1, 0.0, nxt_ref[...])
        row = lax.broadcasted_iota(jnp.int32, (tm, cw), 0)
        d1 = jnp.where(row == tm - 1, nxt[0:1, :], pltpu.roll(dyc_t, tm - 1, 0))
        d2 = jnp.where(row == tm - 2, nxt[0:1, :], jnp.where(row == tm - 1, nxt[1:2, :], pltpu.roll(dyc_t, tm - 2, 0)))
        du = cwt_ref[2:3, :] * dyc_t + cwt_ref[1:2, :] * d1 + cwt_ref[0:1, :] * d2
        c0 = 0
        dp_ref[:, c0:c0 + cw] = dcb_ref[...]
        dp_ref[:, cw:2 * cw] = (du * cx_ref[:, 2 * cw:3 * cw].astype(F32)).astype(BF16)
        dp_ref[:, 2 * cw:3 * cw] = (du * cx_ref[:, cw:2 * cw].astype(F32)).astype(BF16)
        c0 = 3 * cw
        for ref in (dqk_ref, dv_ref, dgo_ref, dgl_ref):
            wd = ref.shape[1]
            dp_ref[:, c0:c0 + wd] = ref[...]
            c0 += wd
        dh = _dot(dp_ref[...], w_ref[...])
        dx, tsh, tsc, tn = _normmod_bwd(dh, x_ref[...], n_ref[...], ada_ref[4:5, :])
        dx_ref[...] = dxo_ref[...] + dx
        dsh_ref[...] += _rsum8(tsh)
        dsc_ref[...] += _rsum8(tsc)
        dn_ref[...] += _rsum8(tn)

    row_spec = lambda wd: pl.BlockSpec((tm, wd), lambda i: (i, 0))
    vec = pl.BlockSpec((8, d), lambda i: (0, 0))
    return _pcall(
        body, name, [dyc, dyc, cx, dcb, dqk, dvv, dgo, dgl, conv_w, w, x, dxo, ada9, nrm], grid=(nt,),
        in_specs=[row_spec(cw), pl.BlockSpec((8, cw), lambda i: (jnp.minimum((i + 1) * (tm // 8), t // 8 - 1), 0)),
                  row_spec(3 * cw), row_spec(cw), row_spec(dqk.shape[1]), row_spec(dvv.shape[1]), row_spec(dgo.shape[1]),
                  row_spec(dgl.shape[1]), pl.BlockSpec((3, cw), lambda i: (0, 0)), pl.BlockSpec((n, d), lambda i: (0, 0)),
                  row_spec(d), row_spec(d), pl.BlockSpec((N_ADA, d), lambda i: (0, 0)), pl.BlockSpec((1, d), lambda i: (0, 0))],
        out_shape=[S((t, d), F32), S((t, n), BF16), S((8, d), F32), S((8, d), F32), S((8, d), F32)],
        out_specs=[row_spec(d), row_spec(n), vec, vec, vec],
        sem=("arbitrary",), plans=plans)


def _ffn_out_loss(s, w_out, x, ada9, g_row, res_scale, target, nrm, name):
    nb, t, bw = s.shape
    d = x.shape[1]
    tm = min(ROW_TILE, t)
    nt = t // tm

    def body(s_ref, w_ref, x_ref, ada_ref, tg_ref, n_ref, f_ref, loss_ref, dx_ref, dn_ref, acc_ref):
        i = pl.program_id(0)

        @pl.when(i == 0)
        def _():
            acc_ref[...] = jnp.zeros_like(acc_ref)
            dn_ref[...] = jnp.zeros_like(dn_ref)

        f = _dot(s_ref[0], w_ref[0])
        for b in range(1, nb):
            f = f + _dot(s_ref[b], w_ref[b])
        f_ref[...] = f.astype(BF16)
        xt = x_ref[...] + (res_scale * ada_ref[g_row:g_row + 1, :]) * f
        rstd = lax.rsqrt(jnp.mean(xt * xt, axis=-1, keepdims=True) + EPS)
        xhat = xt * rstd
        err = xhat * n_ref[...] - tg_ref[...]
        acc_ref[...] += _rsum8(err * err)
        dy = err * (1.0 / d)
        dn_ref[...] += _rsum8(dy * xhat)
        dxhat = dy * n_ref[...]
        dx_ref[...] = rstd * (dxhat - xhat * jnp.mean(dxhat * xhat, axis=-1, keepdims=True))

        @pl.when(i == nt - 1)
        def _():
            loss_ref[...] = jnp.full(loss_ref.shape, (0.5 / d) * jnp.sum(acc_ref[...]), F32)

    return pl.pallas_call(
        body, name=name, grid=(nt,),
        in_specs=[pl.BlockSpec((nb, tm, bw), lambda i: (0, i, 0)), pl.BlockSpec((nb, bw, d), lambda i: (0, 0, 0)),
                  pl.BlockSpec((tm, d), lambda i: (i, 0)), pl.BlockSpec((N_ADA, d), lambda i: (0, 0)),
                  pl.BlockSpec((tm, d), lambda i: (i, 0)), pl.BlockSpec((1, d), lambda i: (0, 0))],
        out_shape=[S((t, d), BF16), S((1, LANES), F32), S((t, d), F32), S((8, d), F32)],
        out_specs=[pl.BlockSpec((tm, d), lambda i: (i, 0)), pl.BlockSpec((1, LANES), lambda i: (0, 0)),
                   pl.BlockSpec((tm, d), lambda i: (i, 0)), pl.BlockSpec((8, d), lambda i: (0, 0))],
        scratch_shapes=[pltpu.VMEM((8, d), F32)],
        compiler_params=_cp("arbitrary"),
    )(s, w_out, x, ada9, target, nrm)


def _pack_smalls(vec_parts, dcw, dbg, dgn, dwg, loss_v, rank, name):
    d = vec_parts[0].shape[1]
    cw, kw, dv = dcw.shape[2], dbg.shape[1], dgn.shape[1]
    nv = len(vec_parts)
    loss_row = nv + 2 + rank * kw // d
    assert 2 * cw == d and cw + kw + dv <= d and (rank * kw) % d == 0 and loss_row < PACK_ROWS
    per_row = d // kw

    def body(*refs):
        vrefs, (dcw_ref, dbg_ref, dgn_ref, dwg_ref, loss_ref, o_ref) = refs[:nv], refs[nv:]
        o_ref[...] = jnp.zeros_like(o_ref)
        o_ref[loss_row:loss_row + 1, 0:loss_ref.shape[1]] = loss_ref[...]
        for r, ref in enumerate(vrefs):
            o_ref[r:r + 1, :] = jnp.sum(ref[...], axis=0, keepdims=True)
        o_ref[nv:nv + 1, 0:cw] = jnp.sum(dcw_ref[0], axis=0, keepdims=True)
        o_ref[nv:nv + 1, cw:2 * cw] = jnp.sum(dcw_ref[1], axis=0, keepdims=True)
        o_ref[nv + 1:nv + 2, 0:cw] = jnp.sum(dcw_ref[2], axis=0, keepdims=True)
        o_ref[nv + 1:nv + 2, cw:cw + kw] = jnp.sum(dbg_ref[...], axis=0, keepdims=True)
        o_ref[nv + 1:nv + 2, cw + kw:cw + kw + dv] = jnp.sum(dgn_ref[...], axis=0, keepdims=True)
        for r in range(rank):
            o_ref[nv + 2 + r // per_row:nv + 3 + r // per_row, (r % per_row) * kw:(r % per_row + 1) * kw] = dwg_ref[r:r + 1, :]

    return pl.pallas_call(body, name=name, out_shape=S((PACK_ROWS, d), F32), compiler_params=_cp())(*vec_parts, dcw, dbg, dgn, dwg, loss_v)


def _sum_slots(a, name):
    def body(a_ref, o_ref):
        acc = a_ref[0]
        for s in range(1, NDEV):
            acc = acc + a_ref[s]
        o_ref[...] = acc

    return pl.pallas_call(body, name=name, out_shape=S(a.shape[1:], F32), compiler_params=_cp())(a)


def _adamw(w, g, m, v):
    m = ADAM_B1 * m + (1.0 - ADAM_B1) * g
    v = ADAM_B2 * v + (1.0 - ADAM_B2) * (g * g)
    m_hat = m / (1.0 - ADAM_B1 ** ADAM_STEP)
    v_hat = v / (1.0 - ADAM_B2 ** ADAM_STEP)
    return -ADAM_LR * (m_hat / (jnp.sqrt(v_hat) + ADAM_EPS) + ADAM_WD * w), m, v


def _adam_slots(recv, w, m, v, name):
    r, c = w.shape
    slots = recv.shape[0]
    tr = _row_tile(r, c)

    def body(recv_ref, w_ref, m_ref, v_ref, g_ref, d_ref, mo_ref, vo_ref):
        g = recv_ref[0].astype(F32)
        for s in range(1, slots):
            g = g + recv_ref[s].astype(F32)
        g_ref[...] = g
        d_ref[...], mo_ref[...], vo_ref[...] = _adamw(w_ref[...], g, m_ref[...], v_ref[...])

    blk = pl.BlockSpec((tr, c), lambda i: (i, 0))
    return pl.pallas_call(
        body, name=name, grid=(r // tr,),
        in_specs=[pl.BlockSpec((slots, tr, c), lambda i: (0, i, 0)), blk, blk, blk],
        out_shape=[S((r, c), F32)] * 4, out_specs=[blk] * 4, compiler_params=_cp("parallel"),
    )(recv, w, m, v)


def _adam_w_ada(act_t, dada, w, m, v, name):
    r, c = w.shape
    tr = 128
    nb = act_t.shape[1]

    def body(a_ref, da_ref, w_ref, m_ref, v_ref, g_ref, d_ref, mo_ref, vo_ref):
        g = a_ref[:, 0:1] * da_ref[0:1, :]
        for b in range(1, nb):
            g = g + a_ref[:, b:b + 1] * da_ref[b:b + 1, :]
        g_ref[...] = g
        d_ref[...], mo_ref[...], vo_ref[...] = _adamw(w_ref[...], g, m_ref[...], v_ref[...])

    blk = pl.BlockSpec((tr, c), lambda i: (i, 0))
    return pl.pallas_call(
        body, name=name, grid=(r // tr,),
        in_specs=[pl.BlockSpec((tr, nb), lambda i: (i, 0)), pl.BlockSpec((nb, c), lambda i: (0, 0)), blk, blk, blk],
        out_shape=[S((r, c), F32)] * 4, out_specs=[blk] * 4, compiler_params=_cp("parallel"),
    )(act_t, dada, w, m, v)


def _adam_smalls(ws, gs, ms, vs, name):
    n = len(ws)

    def body(*refs):
        w_r, g_r, m_r, v_r = (refs[k * n:(k + 1) * n] for k in range(4))
        d_o, m_o, v_o = (refs[(4 + k) * n:(5 + k) * n] for k in range(3))
        for i in range(n):
            d_o[i][...], m_o[i][...], v_o[i][...] = _adamw(w_r[i][...], g_r[i][...], m_r[i][...], v_r[i][...])

    shapes = [S(w.shape, F32) for w in ws]
    outs = pl.pallas_call(body, name=name, out_shape=shapes * 3, compiler_params=_cp())(*ws, *gs, *ms, *vs)
    return outs[:n], outs[n:2 * n], outs[2 * n:]


def kernel(x, c, w_ada, b_ada, norm_ffn1, w_ffn1_in, w_ffn1_out, norm_mix, w_mix_in, conv_w, w_gk2, b_gk, gla_norm, w_mix_out, norm_ffn2, w_ffn2_in, w_ffn2_out, norm_final, loss_target, m_w_ada, m_b_ada, m_norm_ffn1, m_w_ffn1_in, m_w_ffn1_out, m_norm_mix, m_w_mix_in, m_conv_w, m_w_gk2, m_b_gk, m_gla_norm, m_w_mix_out, m_norm_ffn2, m_w_ffn2_in, m_w_ffn2_out, m_norm_final, v_w_ada, v_b_ada, v_norm_ffn1, v_w_ffn1_in, v_w_ffn1_out, v_norm_mix, v_w_mix_in, v_conv_w, v_w_gk2, v_b_gk, v_gla_norm, v_w_mix_out, v_norm_ffn2, v_w_ffn2_in, v_w_ffn2_out, v_norm_final):
    t, d = x.shape[1], x.shape[2]
    x0, tgt = x[0], loss_target[0]
    rank, kw = w_gk2.shape[1], w_gk2.shape[2] * NDEV
    cw = conv_w.shape[2] * NDEV
    dv = gla_norm.shape[1]
    vw = d - cw
    heads = vw // dv
    mix_cols = w_mix_in.shape[2]
    widths = [3 * cw, 2 * kw, vw, vw, LANES]
    n_proj = 3 * cw + 2 * kw + 2 * vw + rank
    assert n_proj == mix_cols * NDEV and rank <= LANES
    me = 4 * lax.axis_index("x") + 2 * lax.axis_index("y") + lax.axis_index("c")

    core = lax.axis_index("c").astype(jnp.int32).reshape(1)
    bf = lambda a: a[0].astype(BF16)
    bft = lambda a: a[0].T.astype(BF16)
    nb = NDEV // 2

    (ada_row, act_all), ((w1i, cwt_all, wg_all),) = _ada_rows(
        c, w_ada[0], b_ada, "ada_rows", plans=[_plan_gather([bft(w_ffn1_in), conv_w[0], w_gk2[0]])])
    ada9 = ada_row.reshape(N_ADA, d)
    cwt = cwt_all.transpose(1, 0, 2).reshape(conv_w.shape[1], cw)
    wg = jnp.pad(wg_all.transpose(1, 0, 2).reshape(rank, kw), ((0, LANES - rank), (0, 0))).astype(BF16)

    (h1, gu1, s1), ((w1o, wmi),) = _ffn_in(x0, ada9, norm_ffn1, w1i, 0, 1, "ffn1_in", plans=[_plan_gather([bf(w_ffn1_out), bft(w_mix_in)])])
    w1o = w1o.reshape(nb, -1, d)
    wmi = jnp.pad(wmi.reshape(n_proj, d), ((0, sum(widths) - n_proj), (0, 0)))
    w2i_mine = bft(w_ffn2_in)
    quarter = w2i_mine.shape[0] // 4
    part = lambda k, into=None: _plan_gather([w2i_mine], rows=(k * quarter, quarter), into=into)
    (x1, f1), ((wmo,), (w2i,)) = _ffn_out(s1, w1o, x0, ada9, 2, 0.5, "ffn1_out", plans=[_plan_gather([bf(w_mix_out)]), part(0)])
    wmo = wmo.reshape(cw + vw, d)
    (h2, cx, qk, vv, go, gl), ((w2i,),) = _mix_in(x1, ada9, norm_mix, wmi, widths, [BF16, F32, BF16, BF16, F32], "mix_in",
                                                 plans=[part(1, [w2i])])
    (o, lg, sall), ((w2i,),) = _gla_fwd(qk, vv, gl, wg, b_gk, heads, "gla_fwd", plans=[part(2, [w2i])])
    (x2, mm, ycat), ((w2i,),) = _mix_out(cx, o, go, cwt, gla_norm, wmo, x1, ada9, heads, "mix_out", plans=[part(3, [w2i])])
    (h3, gu3, s3), ((w2o,),) = _ffn_in(x2, ada9, norm_ffn2, w2i, 6, 7, "ffn2_in", plans=[_plan_gather([bf(w_ffn2_out)])])
    w2o = w2o.reshape(nb, -1, d)
    f3, loss_v, dx3, dnf = _ffn_out_loss(s3, w2o, x2, ada9, 8, 0.5, tgt, norm_final.reshape(1, d), "ffn2_out_loss")

    dx2, sum2i, sum2o, (dsh3, dsc3, dg3, dn3), _, _ = _ffn_backward(
        dx3, x2, h3, gu3, f3, ada9, norm_ffn2, w2i, w2o, (6, 7, 8), core, "ffn2_bwd", later=True)
    (dwmo, dyc, dcb, do, dgo, dg2, dcw, dgn), ((r2o,),) = _mix_bwd_a(dx2, mm, ycat, ada9, wmo, cx, o, go, cwt, gla_norm, heads, "mix_bwd_a",
                                                                   plans=[_plan_chip_swap([sum2o])])
    half = sum2i.shape[1] // 2
    (dqk, dvv, dgl, dwg, dbg), ((r2i,),) = _gla_bwd(qk, vv, lg, do, sall, gl, wg, heads, "gla_bwd",
                                                   plans=[_plan_chip_swap([sum2i], rows=(0, half))])
    (dx1, dproj, dsh2, dsc2, dnm), ((r2i,),) = _mix_bwd_b(dyc, cx, dcb, dqk, dvv, dgo, dgl, cwt, wmi, x1, dx2, ada9, norm_mix, "mix_bwd_b",
                                                         plans=[_plan_chip_swap([sum2i], rows=(half, half), into=[r2i])])
    n_pad = sum(widths)
    tn = n_pad // 5
    dwmi, _ = _tn_matmul(dproj, h2, lambda tk: (tk, tn), lambda sb, k: (k, sb), lambda tk: (tk, d), lambda sb, k: (k, 0),
                         (n_pad, d), (tn, d), lambda sb, k: (sb, 0), 5, "mix_dwin")
    dwmi = dwmi[:n_proj].reshape(NDEV, mix_cols, d)
    dwmo = dwmo.reshape(NDEV, -1, d)
    dx0, r1i, sum1o, (dsh1, dsc1, dg1, dn1), _, ((rmi, rmo),) = _ffn_backward(
        dx1, x0, h1, gu1, f1, ada9, norm_ffn1, w1i, w1o, (0, 1, 2), core, "ffn1_bwd",
        ds_plans=[_plan_sibling_swap([dwmi, dwmo])],
        dwin_plans=lambda moved: [_plan_chip_swap([_pair_add(dwmi, moved[0][0], core, "mix_dwin_add"),
                                                   _pair_add(dwmo, moved[0][1], core, "mix_dwout_add")])])
    pack = _pack_smalls([dn1, dnm, dn3, dnf, dsh1, dsc1, dg1, dsh2, dsc2, dg2, dsh3, dsc3, dg3], dcw, dbg, dgn, dwg, loss_v, rank, "pack_smalls")
    (r1o,), (pack_all,) = _exchange([_plan_chip_swap([sum1o]), _plan_all_to_all([pack], True)], "grads_last")
    tot = _sum_slots(pack_all, "sum_smalls")

    res = {}
    for nm, recv, w, m, v in (("w_ffn1_out", r1o, w_ffn1_out, m_w_ffn1_out, v_w_ffn1_out), ("w_mix_out", rmo, w_mix_out, m_w_mix_out, v_w_mix_out),
                              ("w_ffn2_out", r2o, w_ffn2_out, m_w_ffn2_out, v_w_ffn2_out)):
        res[nm] = [a[None] for a in _adam_slots(recv, w[0], m[0], v[0], "adam_" + nm)]
    for nm, recv, w, m, v in (("w_ffn1_in", r1i, w_ffn1_in, m_w_ffn1_in, v_w_ffn1_in), ("w_mix_in", rmi, w_mix_in, m_w_mix_in, v_w_mix_in),
                              ("w_ffn2_in", r2i, w_ffn2_in, m_w_ffn2_in, v_w_ffn2_in)):
        res[nm] = [a.T[None] for a in _adam_slots(recv, w[0].T, m[0].T, v[0].T, "adam_" + nm)]

    cols_ada = w_ada.shape[2]
    dada_all = pack_all[:, 4:4 + N_ADA, :].reshape(NDEV, N_ADA * d)
    dada_mine = lax.dynamic_slice_in_dim(dada_all, me * cols_ada, cols_ada, axis=1)
    res["w_ada"] = [a[None] for a in _adam_w_ada(act_all.T, dada_mine, w_ada[0], m_w_ada[0], v_w_ada[0], "adam_w_ada")]

    nv = 4 + N_ADA
    g_small = {
        "b_ada": tot[4:nv].reshape(1, N_ADA * d),
        "norm_ffn1": tot[0:1], "norm_mix": tot[1:2], "norm_ffn2": tot[2:3], "norm_final": tot[3:4],
        "conv_w": lax.dynamic_slice_in_dim(
            jnp.concatenate([tot[nv:nv + 1, 0:cw], tot[nv:nv + 1, cw:2 * cw], tot[nv + 1:nv + 2, 0:cw]], axis=0), me * (cw // NDEV), cw // NDEV, axis=1),
        "w_gk2": lax.dynamic_slice_in_dim(tot[nv + 2:nv + 2 + rank * kw // d].reshape(rank, kw), me * (kw // NDEV), kw // NDEV, axis=1),
        "b_gk": tot[nv + 1:nv + 2, cw:cw + kw],
        "gla_norm": tot[nv + 1:nv + 2, cw + kw:cw + kw + dv],
    }
    small = {"b_ada": (b_ada, m_b_ada, v_b_ada), "norm_ffn1": (norm_ffn1, m_norm_ffn1, v_norm_ffn1), "norm_mix": (norm_mix, m_norm_mix, v_norm_mix),
             "norm_ffn2": (norm_ffn2, m_norm_ffn2, v_norm_ffn2), "norm_final": (norm_final, m_norm_final, v_norm_final),
             "conv_w": (conv_w, m_conv_w, v_conv_w), "w_gk2": (w_gk2, m_w_gk2, v_w_gk2), "b_gk": (b_gk, m_b_gk, v_b_gk),
             "gla_norm": (gla_norm, m_gla_norm, v_gla_norm)}
    names = list(small)
    flat = lambda a: a.reshape(-1, a.shape[-1])
    dl, mo, vo = _adam_smalls([flat(small[n][0]) for n in names], [g_small[n] for n in names],
                              [flat(small[n][1]) for n in names], [flat(small[n][2]) for n in names], "adam_smalls")
    for i, n in enumerate(names):
        shp = small[n][0].shape
        res[n] = [g_small[n].reshape(shp), dl[i].reshape(shp), mo[i].reshape(shp), vo[i].reshape(shp)]

    loss = tot[nv + 2 + rank * kw // d, 0]
    order = ["w_ada", "b_ada", "norm_ffn1", "w_ffn1_in", "w_ffn1_out", "norm_mix", "w_mix_in", "conv_w", "w_gk2", "b_gk", "gla_norm",
             "w_mix_out", "norm_ffn2", "w_ffn2_in", "w_ffn2_out", "norm_final"]
    return (loss, dx0[None], *[res[n][0] for n in order], *[res[n][1] for n in order], *[res[n][2] for n in order], *[res[n][3] for n in order])
```

```python
import collections
import functools

import jax
import jax.numpy as jnp
from jax import lax
from jax.experimental import pallas as pl
from jax.experimental.pallas import tpu as pltpu

F32 = jnp.float32
BF16 = jnp.bfloat16
S = jax.ShapeDtypeStruct

NDEV = 8
EPS = 1e-6
GATE_NORMALIZER = 16.0
CHUNK = 128
N_ADA = 9
ADAM_LR, ADAM_B1, ADAM_B2, ADAM_EPS, ADAM_WD, ADAM_STEP = 0.001, 0.9, 0.999, 1e-08, 0.01, 10
V7X_VMEM_LIMIT = 56 * 1024 * 1024
ROW_TILE = 512
WIDE_ROW_TILE = 1024
K_TILE = 1024
EPILOGUE_ROWS = 256
LANES = 128
BF16_ROWS = 16
PACK_ROWS = 24
ANY = pl.BlockSpec(memory_space=pl.ANY)


def _cp(*sem):
    return pltpu.CompilerParams(dimension_semantics=sem or None, vmem_limit_bytes=V7X_VMEM_LIMIT)


def _dot(a, b):
    return jnp.dot(a, b, preferred_element_type=F32)


def _dot_nt(a, b):
    return lax.dot_general(a, b, (((1,), (1,)), ((), ())), preferred_element_type=F32)


def _dot_tn(a, b):
    return lax.dot_general(a, b, (((0,), (0,)), ((), ())), preferred_element_type=F32)


def _rsum8(a):
    r, c = a.shape
    return jnp.sum(a.reshape(r // 8, 8, c), axis=0)


def _row_tile(r, c):
    for cand in (256, 128, 176, 88, 64, 32, 16, 8):
        if r % cand == 0 and cand * c * 4 <= 1024 * 1024:
            return cand
    return r


def _sigmoid(x):
    return 1.0 / (1.0 + jnp.exp(-x))


def _sigmoid_tanh(x):
    return 0.5 * jnp.tanh(0.5 * x) + 0.5


def _normmod(x, nrm, sc, sh):
    rstd = lax.rsqrt(jnp.mean(x * x, axis=-1, keepdims=True) + EPS)
    xhat = x * rstd
    return xhat, rstd, (xhat * nrm) * (1.0 + sc) + sh


def _normmod_bwd(dh, x, nrm, sc):
    rstd = lax.rsqrt(jnp.mean(x * x, axis=-1, keepdims=True) + EPS)
    xhat = x * rstd
    dxhat = dh * (nrm * (1.0 + sc))
    dx = rstd * (dxhat - xhat * jnp.mean(dxhat * xhat, axis=-1, keepdims=True))
    return dx, dh, dh * (xhat * nrm), dh * ((1.0 + sc) * xhat)


def _place():
    x, y, c = lax.axis_index("x"), lax.axis_index("y"), lax.axis_index("c")
    return x, y, c, 4 * x + 2 * y + c


def _peer(x, y, c, k):
    px = 1 - x if k & 4 else x
    py = 1 - y if k & 2 else y
    pc = 1 - c if k & 1 else c
    return (px, py, pc), 4 * px + 2 * py + pc


def _remote(src, dst, send_sem, recv_sem, peer):
    return pltpu.make_async_remote_copy(src_ref=src, dst_ref=dst, send_sem=send_sem, recv_sem=recv_sem,
                                        device_id=peer, device_id_type=pl.DeviceIdType.MESH)


_Plan = collections.namedtuple("_Plan", "inputs out_shapes sem_shapes start finish aliases", defaults=({},))


def _plan_all_to_all(xs, gather):
    n = len(xs)

    def copies(ins, outs, sems, landed):
        send_sems, recv_sems, local_sems = sems
        x, y, c, me = _place()
        local = [pltpu.make_async_copy(ins[i] if gather else ins[i].at[me], outs[i].at[me], local_sems.at[i]) for i in range(n)]
        remote = []
        for k in range(1, NDEV):
            peer, pid = _peer(x, y, c, k)
            for i in range(n):
                remote.append(_remote(ins[i] if gather else ins[i].at[pid], outs[i].at[pid if landed else me],
                                      send_sems.at[i, k - 1], recv_sems.at[i, k - 1], peer))
        return local, remote

    def start(ins, outs, sems):
        local, remote = copies(ins, outs, sems, False)
        for cp in local + remote:
            cp.start()

    def finish(ins, outs, sems):
        local, remote = copies(ins, outs, sems, True)
        for cp in remote + local:
            cp.wait()

    return _Plan(list(xs), [S((NDEV,) + a.shape, a.dtype) if gather else S(a.shape, a.dtype) for a in xs],
                 [pltpu.SemaphoreType.DMA((n, NDEV - 1)), pltpu.SemaphoreType.DMA((n, NDEV - 1)), pltpu.SemaphoreType.DMA((n,))],
                 start, finish)


def _other_chips(x, y):
    return [(1 - x, y), (x, 1 - y), (1 - x, 1 - y)]


def _plan_gather(xs, rows=None, into=None):
    n = len(xs)

    def copies(ins, outs, sems, rest):
        send_sems, recv_sems, local_sems = sems
        x, y, c, me = _place()
        sib, sib_id = (x, y, 1 - c), 4 * x + 2 * y + 1 - c
        chips = _other_chips(x, y)
        mine = lambda i: ins[i] if rows is None else ins[i].at[pl.ds(*rows)]
        slot_of = lambda i, s: outs[i].at[s] if rows is None else outs[i].at[s, pl.ds(*rows)]
        local = [pltpu.make_async_copy(mine(i), slot_of(i, me), local_sems.at[i]) for i in range(n)]
        first = [_remote(mine(i), slot_of(i, me), send_sems.at[i, 0], recv_sems.at[i, 0], sib) for i in range(n)]
        first += [_remote(mine(i), slot_of(i, me), send_sems.at[i, 1 + j], recv_sems.at[i, 1 + j], (px, py, c))
                  for j, (px, py) in enumerate(chips) for i in range(n)]
        if not rest:
            return local, first
        from_sibling = [_remote(mine(i), slot_of(i, sib_id), send_sems.at[i, 0], recv_sems.at[i, 0], sib) for i in range(n)]
        arrive, forward = [], []
        for j, (px, py) in enumerate(chips):
            s = 4 * px + 2 * py
            arrive.append([_remote(mine(i), slot_of(i, s + c), send_sems.at[i, 1 + j], recv_sems.at[i, 1 + j], (px, py, c)) for i in range(n)])
            forward.append([_remote(slot_of(i, s + c), slot_of(i, s + c), send_sems.at[i, 4 + j], recv_sems.at[i, 4 + j], sib) for i in range(n)])
            from_sibling += [_remote(mine(i), slot_of(i, s + 1 - c), send_sems.at[i, 4 + j], recv_sems.at[i, 4 + j], sib) for i in range(n)]
        return local, first, arrive, forward, from_sibling

    def start(ins, outs, sems):
        local, first = copies(ins, outs, sems, False)
        for cp in local + first:
            cp.start()

    def finish(ins, outs, sems):
        local, first, arrive, forward, from_sibling = copies(ins, outs, sems, True)
        for landed, onward in zip(arrive, forward):
            for cp in landed:
                cp.wait_recv()
            for cp in onward:
                cp.start()
        for cp in from_sibling:
            cp.wait_recv()
        for cp in first + [cp for onward in forward for cp in onward]:
            cp.wait_send()
        for cp in local:
            cp.wait()

    return _Plan(list(xs) + list(into or []), [S((NDEV,) + a.shape, a.dtype) for a in xs],
                 [pltpu.SemaphoreType.DMA((n, NDEV - 1)), pltpu.SemaphoreType.DMA((n, NDEV - 1)), pltpu.SemaphoreType.DMA((n,))],
                 start, finish, {n + i: i for i in range(len(into or []))})


def _plan_sibling_swap(gs):
    n = len(gs)

    def copies(ins, outs, sems):
        send_sems, recv_sems = sems
        x, y, c, _ = _place()
        return [_remote(ins[i].at[2 * j + 1 - c], outs[i].at[j], send_sems.at[i, j], recv_sems.at[i, j], (x, y, 1 - c))
                for i in range(n) for j in range(NDEV // 2)]

    def start(ins, outs, sems):
        for cp in copies(ins, outs, sems):
            cp.start()

    def finish(ins, outs, sems):
        for cp in copies(ins, outs, sems):
            cp.wait()

    return _Plan(list(gs), [S((NDEV // 2,) + a.shape[1:], a.dtype) for a in gs],
                 [pltpu.SemaphoreType.DMA((n, NDEV // 2)), pltpu.SemaphoreType.DMA((n, NDEV // 2))], start, finish)


def _pair_add(g, r1, core, name):
    _, r, c = g.shape
    tr = r if r * c * 2 <= 2 * 1024 * 1024 else _row_tile(r, c)

    def body(core_ref, g_ref, r_ref, o_ref):
        o_ref[...] = (g_ref[...].astype(F32) + r_ref[...].astype(F32)).astype(BF16)

    return pl.pallas_call(
        body, name=name,
        grid_spec=pltpu.PrefetchScalarGridSpec(
            num_scalar_prefetch=1, grid=(NDEV // 2, r // tr),
            in_specs=[pl.BlockSpec((None, tr, c), lambda j, k, core_ref: (2 * j + core_ref[0], k, 0)),
                      pl.BlockSpec((None, tr, c), lambda j, k, core_ref: (j, k, 0))],
            out_specs=pl.BlockSpec((None, tr, c), lambda j, k, core_ref: (j, k, 0))),
        out_shape=S((NDEV // 2, r, c), BF16), compiler_params=_cp("parallel", "parallel"),
    )(core, g, r1)


def _plan_chip_swap(ps, rows=None, into=None):
    n = len(ps)

    def copies(ins, outs, sems, landed):
        send_sems, recv_sems, local_sems = sems
        x, y, c, _ = _place()
        mine = 2 * x + y
        part = lambda ref, s: ref.at[s] if rows is None else ref.at[s, pl.ds(*rows)]
        local = [pltpu.make_async_copy(part(ins[i], mine), part(outs[i], mine), local_sems.at[i]) for i in range(n)]
        remote = [_remote(part(ins[i], 2 * px + py), part(outs[i], 2 * px + py if landed else mine),
                          send_sems.at[i, j], recv_sems.at[i, j], (px, py, c))
                  for j, (px, py) in enumerate(_other_chips(x, y)) for i in range(n)]
        return local, remote

    def start(ins, outs, sems):
        local, remote = copies(ins, outs, sems, False)
        for cp in local + remote:
            cp.start()

    def finish(ins, outs, sems):
        local, remote = copies(ins, outs, sems, True)
        for cp in remote + local:
            cp.wait()

    return _Plan(list(ps) + list(into or []), [S(a.shape, a.dtype) for a in ps],
                 [pltpu.SemaphoreType.DMA((n, 3)), pltpu.SemaphoreType.DMA((n, 3)), pltpu.SemaphoreType.DMA((n,))], start, finish,
                 {n + i: i for i in range(len(into or []))})


def _pcall(body, name, args, in_specs, out_shape, out_specs, grid=(), scratch_shapes=(), sem=(), plans=(), starts_plans=False):
    n_in, n_out, n_scr = len(args), len(out_shape), len(scratch_shapes)
    counts = [(len(p.inputs), len(p.out_shapes), len(p.sem_shapes)) for p in plans]
    c_args = [a for p in plans for a in p.inputs]
    c_outs = [s for p in plans for s in p.out_shapes]
    c_sems = [s for p in plans for s in p.sem_shapes]

    def wrapped(*refs):
        cuts = [n_in, len(c_args), n_out, len(c_outs), n_scr, len(c_sems)]
        ins, c_in, outs, c_out, scr, c_sem = [refs[sum(cuts[:k]):sum(cuts[:k + 1])] for k in range(6)]

        def halves(which):
            a = b = s = 0
            for p, (na, nb, ns) in zip(plans, counts):
                getattr(p, which)(c_in[a:a + na], c_out[b:b + nb], c_sem[s:s + ns])
                a, b, s = a + na, b + nb, s + ns

        if not plans:
            body(*ins, *outs, *scr)
        elif not grid and starts_plans:
            body(lambda: halves("start"), *ins, *outs, *scr)
            halves("finish")
        elif not grid:
            halves("start")
            body(*ins, *outs, *scr)
            halves("finish")
        else:
            first = functools.reduce(jnp.logical_and, [pl.program_id(a) == 0 for a in range(len(grid))])
            last = functools.reduce(jnp.logical_and, [pl.program_id(a) == grid[a] - 1 for a in range(len(grid))])
            pl.when(first)(lambda: halves("start"))
            body(*ins, *outs, *scr)
            pl.when(last)(lambda: halves("finish"))

    aliases, a, b = {}, n_in, n_out
    for p, (na, nb, _) in zip(plans, counts):
        aliases.update({a + k: b + v for k, v in p.aliases.items()})
        a, b = a + na, b + nb
    res = pl.pallas_call(
        wrapped, name=name, grid=grid, in_specs=list(in_specs) + [ANY] * len(c_args),
        out_shape=list(out_shape) + c_outs, out_specs=list(out_specs) + [ANY] * len(c_outs),
        scratch_shapes=list(scratch_shapes) + c_sems, input_output_aliases=aliases,
        compiler_params=_cp(*(("arbitrary",) * len(grid) if plans else sem)),
    )(*args, *c_args)
    c_res, b = [], n_out
    for _, nb, _ in counts:
        c_res.append(res[b:b + nb])
        b += nb
    return res[:n_out], c_res


def _exchange(plans, name):
    return _pcall(lambda: None, name, [], [], [], [], plans=plans)[1]


def _ada_rows(c, w_ada, b_ada, name, plans=()):
    d, cols = c.shape[1], w_ada.shape[1]
    gather_c = _plan_all_to_all([c], True)
    gather_p = _plan_all_to_all([S((NDEV, cols), F32)], True)
    n_sem = len(gather_c.sem_shapes)

    def body(start_plans, c_ref, w_ref, b_ref, ada_ref, act_ref, c_all, p_mine, p_all, *sems):
        gather_c.start([c_ref], [c_all], sems[:n_sem])
        gather_c.finish([c_ref], [c_all], sems[:n_sem])
        for s in range(NDEV):
            cc = c_all[s]
            act_ref[s:s + 1, :] = cc * _sigmoid(cc)
        p_mine[...] = _dot(act_ref[...].astype(BF16), w_ref[...].astype(BF16))
        gather_p.start([p_mine], [p_all], sems[n_sem:])
        start_plans()
        gather_p.finish([p_mine], [p_all], sems[n_sem:])
        me = _place()[3]
        for s in range(NDEV):
            ada_ref[:, s * cols:(s + 1) * cols] = p_all[s, pl.ds(me, 1), :] + b_ref[:, s * cols:(s + 1) * cols]

    whole = pl.BlockSpec(memory_space=pltpu.VMEM)
    return _pcall(body, name, [c, w_ada, b_ada], [whole] * 3, [S((1, NDEV * cols), F32), S((NDEV, d), F32)], [whole] * 2,
                  scratch_shapes=[pltpu.VMEM((NDEV,) + c.shape, F32), pltpu.VMEM((NDEV, cols), F32), pltpu.VMEM((NDEV, NDEV, cols), F32)]
                  + gather_c.sem_shapes + gather_p.sem_shapes, plans=plans, starts_plans=True)


def _ffn_in(x, ada9, nrm, w_in, sh_row, sc_row, name, plans=()):
    t, d = x.shape
    nb, bw = w_in.shape[0] // 2, w_in.shape[1]
    tm = min(WIDE_ROW_TILE, t)

    def body(x_ref, ada_ref, n_ref, wg_ref, wu_ref, h_ref, gu_ref, s_ref):
        @pl.when(pl.program_id(1) == 0)
        def _():
            _, _, h = _normmod(x_ref[...], n_ref[...], ada_ref[sc_row:sc_row + 1, :], ada_ref[sh_row:sh_row + 1, :])
            h_ref[...] = h.astype(BF16)

        h = h_ref[...]
        g = _dot_nt(h, wg_ref[...])
        u = _dot_nt(h, wu_ref[...])
        gu_ref[0] = g.astype(BF16)
        gu_ref[1] = u.astype(BF16)
        s_ref[...] = (g * _sigmoid(g) * u).astype(BF16)

    return _pcall(
        body, name, [x, ada9, nrm, w_in, w_in], grid=(t // tm, nb),
        in_specs=[pl.BlockSpec((tm, d), lambda i, j: (i, 0)), pl.BlockSpec((N_ADA, d), lambda i, j: (0, 0)),
                  pl.BlockSpec((1, d), lambda i, j: (0, 0)),
                  pl.BlockSpec((None, bw, d), lambda i, j: (j, 0, 0)), pl.BlockSpec((None, bw, d), lambda i, j: (j + nb, 0, 0))],
        out_shape=[S((t, d), BF16), S((2, nb, t, bw), BF16), S((nb, t, bw), BF16)],
        out_specs=[pl.BlockSpec((tm, d), lambda i, j: (i, 0)), pl.BlockSpec((2, None, tm, bw), lambda i, j: (0, j, i, 0)),
                   pl.BlockSpec((None, tm, bw), lambda i, j: (j, i, 0))],
        sem=("parallel", "arbitrary"), plans=plans)


def _ffn_out(s, w_out, x, ada9, g_row, res_scale, name, plans=()):
    nb, t, bw = s.shape
    d = x.shape[1]
    tm = min(ROW_TILE, t)

    def body(s_ref, w_ref, x_ref, ada_ref, xo_ref, f_ref):
        acc = _dot(s_ref[0], w_ref[0])
        for b in range(1, nb):
            acc = acc + _dot(s_ref[b], w_ref[b])
        f_ref[...] = acc.astype(BF16)
        xo_ref[...] = x_ref[...] + (res_scale * ada_ref[g_row:g_row + 1, :]) * acc

    return _pcall(
        body, name, [s, w_out, x, ada9], grid=(t // tm,),
        in_specs=[pl.BlockSpec((nb, tm, bw), lambda i: (0, i, 0)), pl.BlockSpec((nb, bw, d), lambda i: (0, 0, 0)),
                  pl.BlockSpec((tm, d), lambda i: (i, 0)), pl.BlockSpec((N_ADA, d), lambda i: (0, 0))],
        out_shape=[S((t, d), F32), S((t, d), BF16)],
        out_specs=[pl.BlockSpec((tm, d), lambda i: (i, 0)), pl.BlockSpec((tm, d), lambda i: (i, 0))],
        sem=("parallel",), plans=plans)


def _ffn_bwd_ds(dxo, f, ada9, w_out, gu, g_row, res_scale, name, plans=()):
    t, d = dxo.shape
    nb, bw = w_out.shape[0], w_out.shape[1]
    tm = min(WIDE_ROW_TILE, t)
    ni = t // tm

    def body(dxo_ref, f_ref, ada_ref, w_ref, gu_ref, da_ref, dg_ref, dw_ref, df_ref, acc_ref):
        i, j = pl.program_id(0), pl.program_id(1)

        @pl.when((i == 0) & (j == 0))
        def _():
            dg_ref[...] = jnp.zeros_like(dg_ref)

        @pl.when(j == 0)
        def _():
            dxo_t = dxo_ref[...]
            df_ref[...] = ((res_scale * ada_ref[g_row:g_row + 1, :]) * dxo_t).astype(BF16)
            dg_ref[...] += res_scale * _rsum8(dxo_t * f_ref[...].astype(F32))

        df = df_ref[...]
        ds = _dot_nt(df, w_ref[...])
        g = gu_ref[0].astype(F32)
        u = gu_ref[1].astype(F32)
        sg = _sigmoid_tanh(g)
        silu = g * sg
        da_ref[0] = (ds * u * (sg * (1.0 + g * (1.0 - sg)))).astype(BF16)
        da_ref[1] = (ds * silu).astype(BF16)
        part = _dot_tn((silu * u).astype(BF16), df)

        @pl.when(i == 0)
        def _():
            acc_ref[j] = part

        @pl.when(i > 0)
        def _():
            acc_ref[j] += part

        @pl.when(i == ni - 1)
        def _():
            dw_ref[...] = acc_ref[j].astype(BF16)

    return _pcall(
        body, name, [dxo, f, ada9, w_out, gu], grid=(ni, nb),
        in_specs=[pl.BlockSpec((tm, d), lambda i, j: (i, 0)), pl.BlockSpec((tm, d), lambda i, j: (i, 0)),
                  pl.BlockSpec((N_ADA, d), lambda i, j: (0, 0)), pl.BlockSpec((None, bw, d), lambda i, j: (j, 0, 0)),
                  pl.BlockSpec((2, None, tm, bw), lambda i, j: (0, j, i, 0))],
        out_shape=[S((2, nb, t, bw), BF16), S((8, d), F32), S((nb, bw, d), BF16)],
        out_specs=[pl.BlockSpec((2, None, tm, bw), lambda i, j: (0, j, i, 0)), pl.BlockSpec((8, d), lambda i, j: (0, 0)),
                   pl.BlockSpec((None, bw, d), lambda i, j: (jnp.where(i == ni - 1, j, 0), 0, 0))],
        scratch_shapes=[pltpu.VMEM((tm, d), BF16), pltpu.VMEM((nb, bw, d), F32)],
        sem=("arbitrary", "arbitrary"), plans=plans)


def _ffn_bwd_dh(da, w_in, x, dxo, ada9, nrm, sh_row, sc_row, name, plans=()):
    t, d = x.shape
    nb, bw = w_in.shape[0] // 2, w_in.shape[1]
    tm = min(WIDE_ROW_TILE, t)

    def body(da_ref, wg_ref, wu_ref, x_ref, dxo_ref, ada_ref, n_ref, dx_ref, dsh_ref, dsc_ref, dn_ref, acc_ref):
        i, j = pl.program_id(0), pl.program_id(1)

        @pl.when((i == 0) & (j == 0))
        def _():
            dsh_ref[...] = jnp.zeros_like(dsh_ref)
            dsc_ref[...] = jnp.zeros_like(dsc_ref)
            dn_ref[...] = jnp.zeros_like(dn_ref)

        part = _dot(da_ref[0], wg_ref[...]) + _dot(da_ref[1], wu_ref[...])

        @pl.when(j == 0)
        def _():
            acc_ref[...] = part

        @pl.when(j > 0)
        def _():
            acc_ref[...] += part

        @pl.when(j == nb - 1)
        def _():
            for r0 in range(0, tm, min(EPILOGUE_ROWS, tm)):
                rows = slice(r0, r0 + min(EPILOGUE_ROWS, tm))
                dx, tsh, tsc, tn = _normmod_bwd(acc_ref[rows, :], x_ref[rows, :], n_ref[...], ada_ref[sc_row:sc_row + 1, :])
                dx_ref[rows, :] = dxo_ref[rows, :] + dx
                dsh_ref[...] += _rsum8(tsh)
                dsc_ref[...] += _rsum8(tsc)
                dn_ref[...] += _rsum8(tn)

    vec = pl.BlockSpec((8, d), lambda i, j: (0, 0))
    return _pcall(
        body, name, [da, w_in, w_in, x, dxo, ada9, nrm], grid=(t // tm, nb),
        in_specs=[pl.BlockSpec((2, None, tm, bw), lambda i, j: (0, j, i, 0)),
                  pl.BlockSpec((None, bw, d), lambda i, j: (j, 0, 0)), pl.BlockSpec((None, bw, d), lambda i, j: (j + nb, 0, 0)),
                  pl.BlockSpec((tm, d), lambda i, j: (i, 0)), pl.BlockSpec((tm, d), lambda i, j: (i, 0)),
                  pl.BlockSpec((N_ADA, d), lambda i, j: (0, 0)), pl.BlockSpec((1, d), lambda i, j: (0, 0))],
        out_shape=[S((t, d), F32), S((8, d), F32), S((8, d), F32), S((8, d), F32)],
        out_specs=[pl.BlockSpec((tm, d), lambda i, j: (i, 0)), vec, vec, vec],
        scratch_shapes=[pltpu.VMEM((tm, d), F32)],
        sem=("arbitrary", "arbitrary"), plans=plans)


def _tn_matmul(a, b, a_block, a_map, b_block, b_map, out_shape, out_block, out_map, nblk, name, plans=()):
    t = a.shape[-2]
    tk = min(K_TILE, t)
    nk = t // tk

    def body(a_ref, b_ref, o_ref, acc_ref):
        k = pl.program_id(1)
        for q in (range(a_ref.shape[0]) if len(a_ref.shape) == 3 else [Ellipsis]):
            part = _dot_tn(a_ref[q], b_ref[...])

            @pl.when(k == 0)
            def _():
                acc_ref[q] = part

            @pl.when(k > 0)
            def _():
                acc_ref[q] += part

        @pl.when(k == nk - 1)
        def _():
            o_ref[...] = acc_ref[...].astype(BF16)

    (out,), moved = _pcall(
        body, name, [a, b], grid=(nblk, nk),
        in_specs=[pl.BlockSpec(a_block(tk), a_map), pl.BlockSpec(b_block(tk), b_map)],
        out_shape=[S(out_shape, BF16)], out_specs=[pl.BlockSpec(out_block, out_map)],
        scratch_shapes=[pltpu.VMEM(tuple(n for n in out_block if n is not None), F32)],
        sem=("parallel", "arbitrary"), plans=plans)
    return out, moved


def _ffn_backward(dxo, x_in, h, gu, f, ada9, nrm, w_in, w_out, rows, core, name, ds_plans=(), dwin_plans=(), later=False):
    sh_row, sc_row, g_row = rows
    _, nb, t, bw = gu.shape
    d = x_in.shape[1]
    (da, dg, dw_out), ds_moved = _ffn_bwd_ds(dxo, f, ada9, w_out, gu, g_row, 0.5, name + "_ds", plans=ds_plans)
    dw_out = dw_out.reshape(NDEV, -1, d)
    dw_in, ((half_out,), *dwin_moved) = _tn_matmul(
        da.reshape(2 * nb, t, bw), h, lambda tk: (2, tk, bw), lambda sb, k: (sb, k, 0), lambda tk: (tk, d), lambda sb, k: (k, 0),
        (2 * nb, bw, d), (2, bw, d), lambda sb, k: (sb, 0, 0), nb, name + "_dwin",
        plans=[_plan_sibling_swap([dw_out])] + list(dwin_plans(ds_moved) if callable(dwin_plans) else dwin_plans))
    sum_out = _pair_add(dw_out, half_out, core, name + "_dwout_add")
    if later:
        (dx, dsh, dsc, dn), ((half_in,),) = _ffn_bwd_dh(da, w_in, x_in, dxo, ada9, nrm, sh_row, sc_row, name + "_dh",
                                                       plans=[_plan_sibling_swap([dw_in])])
        return dx, _pair_add(dw_in, half_in, core, name + "_dwin_add"), sum_out, (dsh, dsc, dg, dn), ds_moved, dwin_moved
    ((half_in,),) = _exchange([_plan_sibling_swap([dw_in])], name + "_dwin_swap")
    sum_in = _pair_add(dw_in, half_in, core, name + "_dwin_add")
    (dx, dsh, dsc, dn), ((recv_in,),) = _ffn_bwd_dh(da, w_in, x_in, dxo, ada9, nrm, sh_row, sc_row, name + "_dh",
                                                   plans=[_plan_chip_swap([sum_in])])
    return dx, recv_in, sum_out, (dsh, dsc, dg, dn), ds_moved, dwin_moved


def _mix_in(x, ada9, nrm, w, widths, dts, name, plans=()):
    t, d = x.shape
    n = w.shape[0]
    tm = min(ROW_TILE, t)
    starts = [sum(widths[:i]) for i in range(len(widths))]

    def body(x_ref, ada_ref, n_ref, w_ref, h_ref, *out_refs):
        _, _, h = _normmod(x_ref[...], n_ref[...], ada_ref[4:5, :], ada_ref[3:4, :])
        hb = h.astype(BF16)
        h_ref[...] = hb
        for o_ref, st, wd in zip(out_refs, starts, widths):
            o_ref[...] = _dot_nt(hb, w_ref[st:st + wd, :]).astype(o_ref.dtype)

    return _pcall(
        body, name, [x, ada9, nrm, w], grid=(t // tm,),
        in_specs=[pl.BlockSpec((tm, d), lambda i: (i, 0)), pl.BlockSpec((N_ADA, d), lambda i: (0, 0)),
                  pl.BlockSpec((1, d), lambda i: (0, 0)), pl.BlockSpec((n, d), lambda i: (0, 0))],
        out_shape=[S((t, d), BF16)] + [S((t, wd), dt) for wd, dt in zip(widths, dts)],
        out_specs=[pl.BlockSpec((tm, d), lambda i: (i, 0))] + [pl.BlockSpec((tm, wd), lambda i: (i, 0)) for wd in widths],
        sem=("parallel",), plans=plans)


def _tri(lower):
    r = lax.broadcasted_iota(jnp.int32, (CHUNK, CHUNK), 0)
    c = lax.broadcasted_iota(jnp.int32, (CHUNK, CHUNK), 1)
    return (r >= c) if lower else (c >= r)


def _dot_01(m, x):
    hi = x.astype(BF16)
    r1 = x - hi.astype(F32)
    mid = r1.astype(BF16)
    lo = (r1 - mid.astype(F32)).astype(BF16)
    return _dot(m, hi) + _dot(m, mid) + _dot(m, lo)


def _gla_chunk_terms(q, k, lg, low01):
    b = _dot_01(low01, lg)
    bl = b[CHUNK - 1:CHUNK, :]
    r = 0.5 * bl
    eb, ebl, em, en = jnp.exp(b), jnp.exp(bl - b), jnp.exp(b - r), jnp.exp(r - b)
    return eb, ebl, em, en, jnp.exp(bl), q * eb, k * ebl, q * em, k * en


def _scores(qm_h, knp, qk1_h):
    r = lax.broadcasted_iota(jnp.int32, (CHUNK, CHUNK), 0)
    c = lax.broadcasted_iota(jnp.int32, (CHUNK, CHUNK), 1)
    p = jnp.where(r > c, _dot_nt(qm_h, knp), 0.0)
    return jnp.where(r == c, jnp.sum(qk1_h, axis=1, keepdims=True), p)


def _gla_fwd(qk, v, gl, wg, bg, heads, name, plans=()):
    t = qk.shape[0]
    kw, vw = qk.shape[1] // 2, v.shape[1]
    dk, dv = kw // heads, vw // heads
    assert dk == 64 and dv == 128 and kw % 128 == 0
    gt = min(ROW_TILE, t)
    nc = gt // CHUNK
    scale = dk ** -0.5

    def body(qk_ref, v_ref, gl_ref, wg_ref, bg_ref, o_ref, lg_ref, sall_ref, st_ref):
        @pl.when(pl.program_id(0) == 0)
        def _():
            st_ref[...] = jnp.zeros_like(st_ref)

        gk = _dot(gl_ref[...].astype(BF16), wg_ref[...]) + bg_ref[...]
        lg_ref[...] = (jnp.minimum(gk, 0.0) - jnp.log(1.0 + jnp.exp(-jnp.abs(gk)))) / GATE_NORMALIZER
        low01 = _tri(True).astype(BF16)
        lane = lax.broadcasted_iota(jnp.int32, (CHUNK, LANES), 1)

        def chunk(ci, carry):
            rows = pl.ds(pl.multiple_of(ci * CHUNK, CHUNK), CHUNK)
            q = qk_ref[rows, 0:kw] * scale
            k = qk_ref[rows, kw:2 * kw]
            qk1 = q.astype(BF16).astype(F32) * k.astype(BF16).astype(F32)
            eb, ebl, em, en, ebl_row, qe, ke, qm, kn = _gla_chunk_terms(q, k, lg_ref[rows, :], low01)
            for h in range(heads):
                lanes = slice(LANES * (h // 2), LANES * (h // 2) + LANES)
                own = (lane < 64) if h % 2 == 0 else (lane >= 64)
                knp = kn[:, lanes].astype(BF16)
                qm_h = jnp.where(own, qm[:, lanes], 0.0).astype(BF16)
                qe_h = jnp.where(own, qe[:, lanes], 0.0).astype(BF16)
                ke_h = jnp.where(own, ke[:, lanes], 0.0).astype(BF16)
                v_h = v_ref[rows, h * dv:(h + 1) * dv]
                st = st_ref[h]
                sall_ref[ci, h] = st
                p = _scores(qm_h, knp, jnp.where(own, qk1[:, lanes], 0.0))
                o_ref[rows, h * dv:(h + 1) * dv] = _dot(p.astype(BF16), v_h) + _dot_nt(qe_h, st.astype(BF16))
                st_ref[h] = st * ebl_row[:, lanes] + _dot_tn(v_h, ke_h)
            return carry

        lax.fori_loop(0, nc, chunk, 0, unroll=True)

    return _pcall(
        body, name, [qk, v, gl, wg, bg], grid=(t // gt,),
        in_specs=[pl.BlockSpec((gt, 2 * kw), lambda i: (i, 0)), pl.BlockSpec((gt, vw), lambda i: (i, 0)),
                  pl.BlockSpec((gt, LANES), lambda i: (i, 0)), pl.BlockSpec((LANES, kw), lambda i: (0, 0)),
                  pl.BlockSpec((1, kw), lambda i: (0, 0))],
        out_shape=[S((t, vw), F32), S((t, kw), F32), S((t // CHUNK, heads, dv, LANES), F32)],
        out_specs=[pl.BlockSpec((gt, vw), lambda i: (i, 0)), pl.BlockSpec((gt, kw), lambda i: (i, 0)),
                   pl.BlockSpec((nc, heads, dv, LANES), lambda i: (i, 0, 0, 0))],
        scratch_shapes=[pltpu.VMEM((heads, dv, LANES), F32)],
        sem=("arbitrary",), plans=plans)


def _gla_bwd(qk, v, lg, do, sall, gl, wg, heads, name, plans=()):
    t = qk.shape[0]
    kw, vw = qk.shape[1] // 2, v.shape[1]
    dk, dv = kw // heads, vw // heads
    gt = min(ROW_TILE, t)
    nc = gt // CHUNK
    nt = t // gt
    scale = dk ** -0.5

    def body(qk_ref, v_ref, lg_ref, do_ref, sall_ref, gl_ref, wg_ref, dqk_ref, dv_ref, dgl_ref, dwg_ref, dbg_ref, dst_ref, dgk_ref):
        @pl.when(pl.program_id(0) == 0)
        def _():
            dst_ref[...] = jnp.zeros_like(dst_ref)
            dwg_ref[...] = jnp.zeros_like(dwg_ref)
            dbg_ref[...] = jnp.zeros_like(dbg_ref)

        low01 = _tri(True).astype(BF16)
        up01 = _tri(False).astype(BF16)
        causal = _tri(True)
        lane = lax.broadcasted_iota(jnp.int32, (CHUNK, LANES), 1)
        last_row = lax.broadcasted_iota(jnp.int32, (CHUNK, kw), 0) == CHUNK - 1

        def chunk(cj, carry):
            ci = nc - 1 - cj
            rows = pl.ds(pl.multiple_of(ci * CHUNK, CHUNK), CHUNK)
            q = qk_ref[rows, 0:kw] * scale
            k = qk_ref[rows, kw:2 * kw]
            qk1 = q.astype(BF16).astype(F32) * k.astype(BF16).astype(F32)
            lgc = lg_ref[rows, :]
            eb, ebl, em, en, ebl_row, qe, ke, qm, kn = _gla_chunk_terms(q, k, lgc, low01)
            dqe, dqm, dkn, dke, drow = [], [], [], [], []
            for pr in range(kw // LANES):
                lanes = slice(LANES * pr, LANES * pr + LANES)
                knp = kn[:, lanes].astype(BF16)
                parts = []
                for half in range(2):
                    h = 2 * pr + half
                    own = (lane < 64) if half == 0 else (lane >= 64)
                    qm_h = jnp.where(own, qm[:, lanes], 0.0).astype(BF16)
                    qe_h = jnp.where(own, qe[:, lanes], 0.0).astype(BF16)
                    ke_h = jnp.where(own, ke[:, lanes], 0.0).astype(BF16)
                    v_h = v_ref[rows, h * dv:(h + 1) * dv]
                    do_h = do_ref[rows, h * dv:(h + 1) * dv]
                    st = sall_ref[ci, h]
                    dst = dst_ref[h]
                    stb, dstb = st.astype(BF16), dst.astype(BF16)
                    p = _scores(qm_h, knp, jnp.where(own, qk1[:, lanes], 0.0)).astype(BF16)
                    dp = jnp.where(causal, _dot_nt(do_h, v_h), 0.0).astype(BF16)
                    dv_ref[rows, h * dv:(h + 1) * dv] = (_dot_tn(p, do_h) + _dot_nt(ke_h, dstb)).astype(BF16)
                    parts.append((jnp.where(own, _dot(dp, knp), 0.0), _dot_tn(dp, qm_h), _dot(do_h, stb), _dot(v_h, dstb),
                                  jnp.sum(st * dst, axis=0, keepdims=True)))
                    dst_ref[h] = dst * ebl_row[:, lanes] + _dot_tn(do_h, qe_h)
                dqm.append(parts[0][0] + parts[1][0])
                dkn.append(parts[0][1] + parts[1][1])
                dqe.append(parts[0][2] + parts[1][2])
                dke.append(parts[0][3] + parts[1][3])
                drow.append(parts[0][4] + parts[1][4])
            dqm, dkn, dqe, dke, drow = [jnp.concatenate(a, axis=1) for a in (dqm, dkn, dqe, dke, drow)]
            dqk_ref[rows, 0:kw] = ((dqe * eb + dqm * em) * scale).astype(BF16)
            dqk_ref[rows, kw:2 * kw] = (dke * ebl + dkn * en).astype(BF16)
            tke = dke * ke
            db = dqe * qe + dqm * qm - dkn * kn - tke
            dbl = jnp.sum(tke, axis=0, keepdims=True) + drow * ebl_row
            db = db + jnp.where(last_row, dbl, 0.0)
            dlg = _dot_01(up01, db)
            dgk_ref[rows, :] = dlg * ((1.0 - jnp.exp(GATE_NORMALIZER * lgc)) / GATE_NORMALIZER)
            return carry

        lax.fori_loop(0, nc, chunk, 0, unroll=True)
        dgk = dgk_ref[...]
        dgkb = dgk.astype(BF16)
        dgl_ref[...] = _dot_nt(dgkb, wg_ref[...]).astype(BF16)
        dwg_ref[...] += _dot_tn(gl_ref[...].astype(BF16), dgkb)
        dbg_ref[...] += _rsum8(dgk)

    rev = lambda i: (nt - 1 - i, 0)
    return _pcall(
        body, name, [qk, v, lg, do, sall, gl, wg], grid=(nt,),
        in_specs=[pl.BlockSpec((gt, 2 * kw), rev), pl.BlockSpec((gt, vw), rev), pl.BlockSpec((gt, kw), rev),
                  pl.BlockSpec((gt, vw), rev), pl.BlockSpec((nc, heads, dv, LANES), lambda i: (nt - 1 - i, 0, 0, 0)),
                  pl.BlockSpec((gt, LANES), rev), pl.BlockSpec((LANES, kw), lambda i: (0, 0))],
        out_shape=[S((t, 2 * kw), BF16), S((t, vw), BF16), S((t, LANES), BF16), S((LANES, kw), F32), S((8, kw), F32)],
        out_specs=[pl.BlockSpec((gt, 2 * kw), rev), pl.BlockSpec((gt, vw), rev), pl.BlockSpec((gt, LANES), rev),
                   pl.BlockSpec((LANES, kw), lambda i: (0, 0)), pl.BlockSpec((8, kw), lambda i: (0, 0))],
        scratch_shapes=[pltpu.VMEM((heads, dv, LANES), F32), pltpu.VMEM((gt, kw), F32)],
        sem=("arbitrary",), plans=plans)


def _conv_taps(cx_ref, halo_ref, first, cw):
    tm = cx_ref.shape[0]
    u = cx_ref[:, cw:2 * cw].astype(F32) * cx_ref[:, 2 * cw:3 * cw].astype(F32)
    uh = halo_ref[:, cw:2 * cw].astype(F32) * halo_ref[:, 2 * cw:3 * cw].astype(F32)
    uh = jnp.where(first, 0.0, uh)
    before1, before2 = uh[BF16_ROWS - 1:BF16_ROWS, :], uh[BF16_ROWS - 2:BF16_ROWS - 1, :]
    row = lax.broadcasted_iota(jnp.int32, (tm, cw), 0)
    u1 = jnp.where(row == 0, before1, pltpu.roll(u, 1, 0))
    u2 = jnp.where(row == 0, before2, jnp.where(row == 1, before1, pltpu.roll(u, 2, 0)))
    return u, u1, u2


def _head_norm(o_h, gn):
    rstd = lax.rsqrt(jnp.mean(o_h * o_h, axis=-1, keepdims=True) + EPS)
    ohat = o_h * rstd
    return ohat, rstd, ohat * gn


def _mix_out(cx, o, go, conv_w, gn, w_out, x, ada9, heads, name, plans=()):
    t, d = x.shape
    cw, vw = conv_w.shape[1], o.shape[1]
    dv = vw // heads
    tm = min(ROW_TILE, t)

    def body(cx_ref, halo_ref, o_ref, go_ref, cwt_ref, gn_ref, w_ref, x_ref, ada_ref, xo_ref, m_ref, y_ref):
        u, u1, u2 = _conv_taps(cx_ref, halo_ref, pl.program_id(0) == 0, cw)
        yc = cwt_ref[0:1, :] * u2 + cwt_ref[1:2, :] * u1 + cwt_ref[2:3, :] * u
        y_ref[:, 0:cw] = (cx_ref[:, 0:cw].astype(F32) * yc).astype(BF16)
        for h in range(heads):
            cols = slice(h * dv, (h + 1) * dv)
            _, _, on = _head_norm(o_ref[:, cols], gn_ref[...])
            g = go_ref[:, cols].astype(F32)
            y_ref[:, cw + h * dv:cw + (h + 1) * dv] = (on * (g * _sigmoid(g))).astype(BF16)
        m = _dot(y_ref[...], w_ref[...])
        m_ref[...] = m.astype(BF16)
        xo_ref[...] = x_ref[...] + ada_ref[5:6, :] * m

    return _pcall(
        body, name, [cx, cx, o, go, conv_w, gn, w_out, x, ada9], grid=(t // tm,),
        in_specs=[pl.BlockSpec((tm, 3 * cw), lambda i: (i, 0)),
                  pl.BlockSpec((BF16_ROWS, 3 * cw), lambda i: (jnp.maximum(i * (tm // BF16_ROWS) - 1, 0), 0)),
                  pl.BlockSpec((tm, vw), lambda i: (i, 0)), pl.BlockSpec((tm, vw), lambda i: (i, 0)),
                  pl.BlockSpec((3, cw), lambda i: (0, 0)), pl.BlockSpec((1, dv), lambda i: (0, 0)),
                  pl.BlockSpec((cw + vw, d), lambda i: (0, 0)), pl.BlockSpec((tm, d), lambda i: (i, 0)),
                  pl.BlockSpec((N_ADA, d), lambda i: (0, 0))],
        out_shape=[S((t, d), F32), S((t, d), BF16), S((t, cw + vw), BF16)],
        out_specs=[pl.BlockSpec((tm, d), lambda i: (i, 0)), pl.BlockSpec((tm, d), lambda i: (i, 0)),
                   pl.BlockSpec((tm, cw + vw), lambda i: (i, 0))],
        sem=("parallel",), plans=plans)


def _mix_bwd_a(dxo, m, ycat, ada9, w_out, cx, o, go, conv_w, gn, heads, name, plans=()):
    t, d = dxo.shape
    cw, vw = conv_w.shape[1], o.shape[1]
    dv = vw // heads
    tm = min(ROW_TILE, t)
    nt = t // tm

    def body(dxo_ref, m_ref, y_ref, ada_ref, w_ref, cx_ref, halo_ref, o_ref, go_ref, cwt_ref, gn_ref,
             dw_ref, dyc_ref, dcb_ref, do_ref, dgo_ref, dg_ref, dcw_ref, dgn_ref, acc_ref):
        @pl.when(pl.program_id(0) == 0)
        def _():
            dg_ref[...] = jnp.zeros_like(dg_ref)
            dcw_ref[...] = jnp.zeros_like(dcw_ref)
            dgn_ref[...] = jnp.zeros_like(dgn_ref)

        dxo_t = dxo_ref[...]
        dmb = (ada_ref[5:6, :] * dxo_t).astype(BF16)
        part = _dot_tn(y_ref[...], dmb)

        @pl.when(pl.program_id(0) == 0)
        def _():
            acc_ref[...] = part

        @pl.when(pl.program_id(0) > 0)
        def _():
            acc_ref[...] += part

        @pl.when(pl.program_id(0) == nt - 1)
        def _():
            dw_ref[...] = acc_ref[...].astype(BF16)

        dg_ref[...] += _rsum8(dxo_t * m_ref[...].astype(F32))
        dy = _dot_nt(dmb, w_ref[...])
        u, u1, u2 = _conv_taps(cx_ref, halo_ref, pl.program_id(0) == 0, cw)
        yc = cwt_ref[0:1, :] * u2 + cwt_ref[1:2, :] * u1 + cwt_ref[2:3, :] * u
        dyv = dy[:, 0:cw]
        dcb_ref[...] = (dyv * yc).astype(BF16)
        dyc = dyv * cx_ref[:, 0:cw].astype(F32)
        dyc_ref[...] = dyc
        dcw_ref[0] += _rsum8(dyc * u2)
        dcw_ref[1] += _rsum8(dyc * u1)
        dcw_ref[2] += _rsum8(dyc * u)
        for h in range(heads):
            cols = slice(h * dv, (h + 1) * dv)
            ohat, rstd, on = _head_norm(o_ref[:, cols], gn_ref[...])
            g = go_ref[:, cols].astype(F32)
            sg = _sigmoid_tanh(g)
            dyg = dy[:, cw + h * dv:cw + (h + 1) * dv]
            dgo_ref[:, cols] = (dyg * on * (sg * (1.0 + g * (1.0 - sg)))).astype(BF16)
            don = dyg * (g * sg)
            dgn_ref[...] += _rsum8(don * ohat)
            tt = don * gn_ref[...]
            do_ref[:, cols] = (rstd * (tt - ohat * jnp.mean(tt * ohat, axis=-1, keepdims=True))).astype(BF16)

    return _pcall(
        body, name, [dxo, m, ycat, ada9, w_out, cx, cx, o, go, conv_w, gn], grid=(nt,),
        in_specs=[pl.BlockSpec((tm, d), lambda i: (i, 0)), pl.BlockSpec((tm, d), lambda i: (i, 0)),
                  pl.BlockSpec((tm, cw + vw), lambda i: (i, 0)),
                  pl.BlockSpec((N_ADA, d), lambda i: (0, 0)), pl.BlockSpec((cw + vw, d), lambda i: (0, 0)),
                  pl.BlockSpec((tm, 3 * cw), lambda i: (i, 0)),
                  pl.BlockSpec((BF16_ROWS, 3 * cw), lambda i: (jnp.maximum(i * (tm // BF16_ROWS) - 1, 0), 0)),
                  pl.BlockSpec((tm, vw), lambda i: (i, 0)), pl.BlockSpec((tm, vw), lambda i: (i, 0)),
                  pl.BlockSpec((3, cw), lambda i: (0, 0)), pl.BlockSpec((1, dv), lambda i: (0, 0))],
        out_shape=[S((cw + vw, d), BF16), S((t, cw), F32), S((t, cw), BF16), S((t, vw), BF16), S((t, vw), BF16),
                   S((8, d), F32), S((3, 8, cw), F32), S((8, dv), F32)],
        out_specs=[pl.BlockSpec((cw + vw, d), lambda i: (0, 0)), pl.BlockSpec((tm, cw), lambda i: (i, 0)),
                   pl.BlockSpec((tm, cw), lambda i: (i, 0)), pl.BlockSpec((tm, vw), lambda i: (i, 0)),
                   pl.BlockSpec((tm, vw), lambda i: (i, 0)), pl.BlockSpec((8, d), lambda i: (0, 0)),
                   pl.BlockSpec((3, 8, cw), lambda i: (0, 0, 0)), pl.BlockSpec((8, dv), lambda i: (0, 0))],
        scratch_shapes=[pltpu.VMEM((cw + vw, d), F32)],
        sem=("arbitrary",), plans=plans)


def _mix_bwd_b(dyc, cx, dcb, dqk, dvv, dgo, dgl, conv_w, w, x, dxo, ada9, nrm, name, plans=()):
    t, d = x.shape
    cw = conv_w.shape[1]
    n = w.shape[0]
    tm = min(ROW_TILE, t)
    nt = t // tm
    pieces = [dcb.shape[1], cw, cw, dqk.shape[1], dvv.shape[1], dgo.shape[1], dgl.shape[1]]
    assert sum(pieces) == n

    def body(dyc_ref, nxt_ref, cx_ref, dcb_ref, dqk_ref, dv_ref, dgo_ref, dgl_ref, cwt_ref, w_ref, x_ref, dxo_ref, ada_ref, n_ref,
             dx_ref, dp_ref, dsh_ref, dsc_ref, dn_ref):
        i = pl.program_id(0)

        @pl.when(i == 0)
        def _():
            dsh_ref[...] = jnp.zeros_like(dsh_ref)
            dsc_ref[...] = jnp.zeros_like(dsc_ref)
            dn_ref[...] = jnp.zeros_like(dn_ref)

        dyc_t = dyc_ref[...]
        nxt = jnp.where(i == nt - 1, 0.0, nxt_ref[...])
        row = lax.broadcasted_iota(jnp.int32, (tm, cw), 0)
        d1 = jnp.where(row == tm - 1, nxt[0:1, :], pltpu.roll(dyc_t, tm - 1, 0))
        d2 = jnp.where(row == tm - 2, nxt[0:1, :], jnp.where(row == tm - 1, nxt[1:2, :], pltpu.roll(dyc_t, tm - 2, 0)))
        du = cwt_ref[2:3, :] * dyc_t + cwt_ref[1:2, :] * d1 + cwt_ref[0:1, :] * d2
        c0 = 0
        dp_ref[:, c0:c0 + cw] = dcb_ref[...]
        dp_ref[:, cw:2 * cw] = (du * cx_ref[:, 2 * cw:3 * cw].astype(F32)).astype(BF16)
        dp_ref[:, 2 * cw:3 * cw] = (du * cx_ref[:, cw:2 * cw].astype(F32)).astype(BF16)
        c0 = 3 * cw
        for ref in (dqk_ref, dv_ref, dgo_ref, dgl_ref):
            wd = ref.shape[1]
            dp_ref[:, c0:c0 + wd] = ref[...]
            c0 += wd
        dh = _dot(dp_ref[...], w_ref[...])
        dx, tsh, tsc, tn = _normmod_bwd(dh, x_ref[...], n_ref[...], ada_ref[4:5, :])
        dx_ref[...] = dxo_ref[...] + dx
        dsh_ref[...] += _rsum8(tsh)
        dsc_ref[...] += _rsum8(tsc)
        dn_ref[...] += _rsum8(tn)

    row_spec = lambda wd: pl.BlockSpec((tm, wd), lambda i: (i, 0))
    vec = pl.BlockSpec((8, d), lambda i: (0, 0))
    return _pcall(
        body, name, [dyc, dyc, cx, dcb, dqk, dvv, dgo, dgl, conv_w, w, x, dxo, ada9, nrm], grid=(nt,),
        in_specs=[row_spec(cw), pl.BlockSpec((8, cw), lambda i: (jnp.minimum((i + 1) * (tm // 8), t // 8 - 1), 0)),
                  row_spec(3 * cw), row_spec(cw), row_spec(dqk.shape[1]), row_spec(dvv.shape[1]), row_spec(dgo.shape[1]),
                  row_spec(dgl.shape[1]), pl.BlockSpec((3, cw), lambda i: (0, 0)), pl.BlockSpec((n, d), lambda i: (0, 0)),
                  row_spec(d), row_spec(d), pl.BlockSpec((N_ADA, d), lambda i: (0, 0)), pl.BlockSpec((1, d), lambda i: (0, 0))],
        out_shape=[S((t, d), F32), S((t, n), BF16), S((8, d), F32), S((8, d), F32), S((8, d), F32)],
        out_specs=[row_spec(d), row_spec(n), vec, vec, vec],
        sem=("arbitrary",), plans=plans)


def _ffn_out_loss(s, w_out, x, ada9, g_row, res_scale, target, nrm, name):
    nb, t, bw = s.shape
    d = x.shape[1]
    tm = min(ROW_TILE, t)
    nt = t // tm

    def body(s_ref, w_ref, x_ref, ada_ref, tg_ref, n_ref, f_ref, loss_ref, dx_ref, dn_ref, acc_ref):
        i = pl.program_id(0)

        @pl.when(i == 0)
        def _():
            acc_ref[...] = jnp.zeros_like(acc_ref)
            dn_ref[...] = jnp.zeros_like(dn_ref)

        f = _dot(s_ref[0], w_ref[0])
        for b in range(1, nb):
            f = f + _dot(s_ref[b], w_ref[b])
        f_ref[...] = f.astype(BF16)
        xt = x_ref[...] + (res_scale * ada_ref[g_row:g_row + 1, :]) * f
        rstd = lax.rsqrt(jnp.mean(xt * xt, axis=-1, keepdims=True) + EPS)
        xhat = xt * rstd
        err = xhat * n_ref[...] - tg_ref[...]
        acc_ref[...] += _rsum8(err * err)
        dy = err * (1.0 / d)
        dn_ref[...] += _rsum8(dy * xhat)
        dxhat = dy * n_ref[...]
        dx_ref[...] = rstd * (dxhat - xhat * jnp.mean(dxhat * xhat, axis=-1, keepdims=True))

        @pl.when(i == nt - 1)
        def _():
            loss_ref[...] = jnp.full(loss_ref.shape, (0.5 / d) * jnp.sum(acc_ref[...]), F32)

    return pl.pallas_call(
        body, name=name, grid=(nt,),
        in_specs=[pl.BlockSpec((nb, tm, bw), lambda i: (0, i, 0)), pl.BlockSpec((nb, bw, d), lambda i: (0, 0, 0)),
                  pl.BlockSpec((tm, d), lambda i: (i, 0)), pl.BlockSpec((N_ADA, d), lambda i: (0, 0)),
                  pl.BlockSpec((tm, d), lambda i: (i, 0)), pl.BlockSpec((1, d), lambda i: (0, 0))],
        out_shape=[S((t, d), BF16), S((1, LANES), F32), S((t, d), F32), S((8, d), F32)],
        out_specs=[pl.BlockSpec((tm, d), lambda i: (i, 0)), pl.BlockSpec((1, LANES), lambda i: (0, 0)),
                   pl.BlockSpec((tm, d), lambda i: (i, 0)), pl.BlockSpec((8, d), lambda i: (0, 0))],
        scratch_shapes=[pltpu.VMEM((8, d), F32)],
        compiler_params=_cp("arbitrary"),
    )(s, w_out, x, ada9, target, nrm)


def _pack_smalls(vec_parts, dcw, dbg, dgn, dwg, loss_v, rank, name):
    d = vec_parts[0].shape[1]
    cw, kw, dv = dcw.shape[2], dbg.shape[1], dgn.shape[1]
    nv = len(vec_parts)
    loss_row = nv + 2 + rank * kw // d
    assert 2 * cw == d and cw + kw + dv <= d and (rank * kw) % d == 0 and loss_row < PACK_ROWS
    per_row = d // kw

    def body(*refs):
        vrefs, (dcw_ref, dbg_ref, dgn_ref, dwg_ref, loss_ref, o_ref) = refs[:nv], refs[nv:]
        o_ref[...] = jnp.zeros_like(o_ref)
        o_ref[loss_row:loss_row + 1, 0:loss_ref.shape[1]] = loss_ref[...]
        for r, ref in enumerate(vrefs):
            o_ref[r:r + 1, :] = jnp.sum(ref[...], axis=0, keepdims=True)
        o_ref[nv:nv + 1, 0:cw] = jnp.sum(dcw_ref[0], axis=0, keepdims=True)
        o_ref[nv:nv + 1, cw:2 * cw] = jnp.sum(dcw_ref[1], axis=0, keepdims=True)
        o_ref[nv + 1:nv + 2, 0:cw] = jnp.sum(dcw_ref[2], axis=0, keepdims=True)
        o_ref[nv + 1:nv + 2, cw:cw + kw] = jnp.sum(dbg_ref[...], axis=0, keepdims=True)
        o_ref[nv + 1:nv + 2, cw + kw:cw + kw + dv] = jnp.sum(dgn_ref[...], axis=0, keepdims=True)
        for r in range(rank):
            o_ref[nv + 2 + r // per_row:nv + 3 + r // per_row, (r % per_row) * kw:(r % per_row + 1) * kw] = dwg_ref[r:r + 1, :]

    return pl.pallas_call(body, name=name, out_shape=S((PACK_ROWS, d), F32), compiler_params=_cp())(*vec_parts, dcw, dbg, dgn, dwg, loss_v)


def _sum_slots(a, name):
    def body(a_ref, o_ref):
        acc = a_ref[0]
        for s in range(1, NDEV):
            acc = acc + a_ref[s]
        o_ref[...] = acc

    return pl.pallas_call(body, name=name, out_shape=S(a.shape[1:], F32), compiler_params=_cp())(a)


def _adamw(w, g, m, v):
    m = ADAM_B1 * m + (1.0 - ADAM_B1) * g
    v = ADAM_B2 * v + (1.0 - ADAM_B2) * (g * g)
    m_hat = m / (1.0 - ADAM_B1 ** ADAM_STEP)
    v_hat = v / (1.0 - ADAM_B2 ** ADAM_STEP)
    return -ADAM_LR * (m_hat / (jnp.sqrt(v_hat) + ADAM_EPS) + ADAM_WD * w), m, v


def _adam_slots(recv, w, m, v, name):
    r, c = w.shape
    slots = recv.shape[0]
    tr = _row_tile(r, c)

    def body(recv_ref, w_ref, m_ref, v_ref, g_ref, d_ref, mo_ref, vo_ref):
        g = recv_ref[0].astype(F32)
        for s in range(1, slots):
            g = g + recv_ref[s].astype(F32)
        g_ref[...] = g
        d_ref[...], mo_ref[...], vo_ref[...] = _adamw(w_ref[...], g, m_ref[...], v_ref[...])

    blk = pl.BlockSpec((tr, c), lambda i: (i, 0))
    return pl.pallas_call(
        body, name=name, grid=(r // tr,),
        in_specs=[pl.BlockSpec((slots, tr, c), lambda i: (0, i, 0)), blk, blk, blk],
        out_shape=[S((r, c), F32)] * 4, out_specs=[blk] * 4, compiler_params=_cp("parallel"),
    )(recv, w, m, v)


def _adam_w_ada(act_t, dada, w, m, v, name):
    r, c = w.shape
    tr = 128
    nb = act_t.shape[1]

    def body(a_ref, da_ref, w_ref, m_ref, v_ref, g_ref, d_ref, mo_ref, vo_ref):
        g = a_ref[:, 0:1] * da_ref[0:1, :]
        for b in range(1, nb):
            g = g + a_ref[:, b:b + 1] * da_ref[b:b + 1, :]
        g_ref[...] = g
        d_ref[...], mo_ref[...], vo_ref[...] = _adamw(w_ref[...], g, m_ref[...], v_ref[...])

    blk = pl.BlockSpec((tr, c), lambda i: (i, 0))
    return pl.pallas_call(
        body, name=name, grid=(r // tr,),
        in_specs=[pl.BlockSpec((tr, nb), lambda i: (i, 0)), pl.BlockSpec((nb, c), lambda i: (0, 0)), blk, blk, blk],
        out_shape=[S((r, c), F32)] * 4, out_specs=[blk] * 4, compiler_params=_cp("parallel"),
    )(act_t, dada, w, m, v)


def _adam_smalls(ws, gs, ms, vs, name):
    n = len(ws)

    def body(*refs):
        w_r, g_r, m_r, v_r = (refs[k * n:(k + 1) * n] for k in range(4))
        d_o, m_o, v_o = (refs[(4 + k) * n:(5 + k) * n] for k in range(3))
        for i in range(n):
            d_o[i][...], m_o[i][...], v_o[i][...] = _adamw(w_r[i][...], g_r[i][...], m_r[i][...], v_r[i][...])

    shapes = [S(w.shape, F32) for w in ws]
    outs = pl.pallas_call(body, name=name, out_shape=shapes * 3, compiler_params=_cp())(*ws, *gs, *ms, *vs)
    return outs[:n], outs[n:2 * n], outs[2 * n:]


def kernel(x, c, w_ada, b_ada, norm_ffn1, w_ffn1_in, w_ffn1_out, norm_mix, w_mix_in, conv_w, w_gk2, b_gk, gla_norm, w_mix_out, norm_ffn2, w_ffn2_in, w_ffn2_out, norm_final, loss_target, m_w_ada, m_b_ada, m_norm_ffn1, m_w_ffn1_in, m_w_ffn1_out, m_norm_mix, m_w_mix_in, m_conv_w, m_w_gk2, m_b_gk, m_gla_norm, m_w_mix_out, m_norm_ffn2, m_w_ffn2_in, m_w_ffn2_out, m_norm_final, v_w_ada, v_b_ada, v_norm_ffn1, v_w_ffn1_in, v_w_ffn1_out, v_norm_mix, v_w_mix_in, v_conv_w, v_w_gk2, v_b_gk, v_gla_norm, v_w_mix_out, v_norm_ffn2, v_w_ffn2_in, v_w_ffn2_out, v_norm_final):
    t, d = x.shape[1], x.shape[2]
    x0, tgt = x[0], loss_target[0]
    rank, kw = w_gk2.shape[1], w_gk2.shape[2] * NDEV
    cw = conv_w.shape[2] * NDEV
    dv = gla_norm.shape[1]
    vw = d - cw
    heads = vw // dv
    mix_cols = w_mix_in.shape[2]
    widths = [3 * cw, 2 * kw, vw, vw, LANES]
    n_proj = 3 * cw + 2 * kw + 2 * vw + rank
    assert n_proj == mix_cols * NDEV and rank <= LANES
    me = 4 * lax.axis_index("x") + 2 * lax.axis_index("y") + lax.axis_index("c")

    core = lax.axis_index("c").astype(jnp.int32).reshape(1)
    bf = lambda a: a[0].astype(BF16)
    bft = lambda a: a[0].T.astype(BF16)
    nb = NDEV // 2

    (ada_row, act_all), ((w1i, cwt_all, wg_all),) = _ada_rows(
        c, w_ada[0], b_ada, "ada_rows", plans=[_plan_gather([bft(w_ffn1_in), conv_w[0], w_gk2[0]])])
    ada9 = ada_row.reshape(N_ADA, d)
    cwt = cwt_all.transpose(1, 0, 2).reshape(conv_w.shape[1], cw)
    wg = jnp.pad(wg_all.transpose(1, 0, 2).reshape(rank, kw), ((0, LANES - rank), (0, 0))).astype(BF16)

    (h1, gu1, s1), ((w1o, wmi),) = _ffn_in(x0, ada9, norm_ffn1, w1i, 0, 1, "ffn1_in", plans=[_plan_gather([bf(w_ffn1_out), bft(w_mix_in)])])
    w1o = w1o.reshape(nb, -1, d)
    wmi = jnp.pad(wmi.reshape(n_proj, d), ((0, sum(widths) - n_proj), (0, 0)))
    w2i_mine = bft(w_ffn2_in)
    quarter = w2i_mine.shape[0] // 4
    part = lambda k, into=None: _plan_gather([w2i_mine], rows=(k * quarter, quarter), into=into)
    (x1, f1), ((wmo,), (w2i,)) = _ffn_out(s1, w1o, x0, ada9, 2, 0.5, "ffn1_out", plans=[_plan_gather([bf(w_mix_out)]), part(0)])
    wmo = wmo.reshape(cw + vw, d)
    (h2, cx, qk, vv, go, gl), ((w2i,),) = _mix_in(x1, ada9, norm_mix, wmi, widths, [BF16, F32, BF16, BF16, F32], "mix_in",
                                                 plans=[part(1, [w2i])])
    (o, lg, sall), ((w2i,),) = _gla_fwd(qk, vv, gl, wg, b_gk, heads, "gla_fwd", plans=[part(2, [w2i])])
    (x2, mm, ycat), ((w2i,),) = _mix_out(cx, o, go, cwt, gla_norm, wmo, x1, ada9, heads, "mix_out", plans=[part(3, [w2i])])
    (h3, gu3, s3), ((w2o,),) = _ffn_in(x2, ada9, norm_ffn2, w2i, 6, 7, "ffn2_in", plans=[_plan_gather([bf(w_ffn2_out)])])
    w2o = w2o.reshape(nb, -1, d)
    f3, loss_v, dx3, dnf = _ffn_out_loss(s3, w2o, x2, ada9, 8, 0.5, tgt, norm_final.reshape(1, d), "ffn2_out_loss")

    dx2, sum2i, sum2o, (dsh3, dsc3, dg3, dn3), _, _ = _ffn_backward(
        dx3, x2, h3, gu3, f3, ada9, norm_ffn2, w2i, w2o, (6, 7, 8), core, "ffn2_bwd", later=True)
    (dwmo, dyc, dcb, do, dgo, dg2, dcw, dgn), ((r2o,),) = _mix_bwd_a(dx2, mm, ycat, ada9, wmo, cx, o, go, cwt, gla_norm, heads, "mix_bwd_a",
                                                                   plans=[_plan_chip_swap([sum2o])])
    half = sum2i.shape[1] // 2
    (dqk, dvv, dgl, dwg, dbg), ((r2i,),) = _gla_bwd(qk, vv, lg, do, sall, gl, wg, heads, "gla_bwd",
                                                   plans=[_plan_chip_swap([sum2i], rows=(0, half))])
    (dx1, dproj, dsh2, dsc2, dnm), ((r2i,),) = _mix_bwd_b(dyc, cx, dcb, dqk, dvv, dgo, dgl, cwt, wmi, x1, dx2, ada9, norm_mix, "mix_bwd_b",
                                                         plans=[_plan_chip_swap([sum2i], rows=(half, half), into=[r2i])])
    n_pad = sum(widths)
    tn = n_pad // 5
    dwmi, _ = _tn_matmul(dproj, h2, lambda tk: (tk, tn), lambda sb, k: (k, sb), lambda tk: (tk, d), lambda sb, k: (k, 0),
                         (n_pad, d), (tn, d), lambda sb, k: (sb, 0), 5, "mix_dwin")
    dwmi = dwmi[:n_proj].reshape(NDEV, mix_cols, d)
    dwmo = dwmo.reshape(NDEV, -1, d)
    dx0, r1i, sum1o, (dsh1, dsc1, dg1, dn1), _, ((rmi, rmo),) = _ffn_backward(
        dx1, x0, h1, gu1, f1, ada9, norm_ffn1, w1i, w1o, (0, 1, 2), core, "ffn1_bwd",
        ds_plans=[_plan_sibling_swap([dwmi, dwmo])],
        dwin_plans=lambda moved: [_plan_chip_swap([_pair_add(dwmi, moved[0][0], core, "mix_dwin_add"),
                                                   _pair_add(dwmo, moved[0][1], core, "mix_dwout_add")])])
    pack = _pack_smalls([dn1, dnm, dn3, dnf, dsh1, dsc1, dg1, dsh2, dsc2, dg2, dsh3, dsc3, dg3], dcw, dbg, dgn, dwg, loss_v, rank, "pack_smalls")
    (r1o,), (pack_all,) = _exchange([_plan_chip_swap([sum1o]), _plan_all_to_all([pack], True)], "grads_last")
    tot = _sum_slots(pack_all, "sum_smalls")

    res = {}
    for nm, recv, w, m, v in (("w_ffn1_out", r1o, w_ffn1_out, m_w_ffn1_out, v_w_ffn1_out), ("w_mix_out", rmo, w_mix_out, m_w_mix_out, v_w_mix_out),
                              ("w_ffn2_out", r2o, w_ffn2_out, m_w_ffn2_out, v_w_ffn2_out)):
        res[nm] = [a[None] for a in _adam_slots(recv, w[0], m[0], v[0], "adam_" + nm)]
    for nm, recv, w, m, v in (("w_ffn1_in", r1i, w_ffn1_in, m_w_ffn1_in, v_w_ffn1_in), ("w_mix_in", rmi, w_mix_in, m_w_mix_in, v_w_mix_in),
                              ("w_ffn2_in", r2i, w_ffn2_in, m_w_ffn2_in, v_w_ffn2_in)):
        res[nm] = [a.T[None] for a in _adam_slots(recv, w[0].T, m[0].T, v[0].T, "adam_" + nm)]

    cols_ada = w_ada.shape[2]
    dada_all = pack_all[:, 4:4 + N_ADA, :].reshape(NDEV, N_ADA * d)
    dada_mine = lax.dynamic_slice_in_dim(dada_all, me * cols_ada, cols_ada, axis=1)
    res["w_ada"] = [a[None] for a in _adam_w_ada(act_all.T, dada_mine, w_ada[0], m_w_ada[0], v_w_ada[0], "adam_w_ada")]

    nv = 4 + N_ADA
    g_small = {
        "b_ada": tot[4:nv].reshape(1, N_ADA * d),
        "norm_ffn1": tot[0:1], "norm_mix": tot[1:2], "norm_ffn2": tot[2:3], "norm_final": tot[3:4],
        "conv_w": lax.dynamic_slice_in_dim(
            jnp.concatenate([tot[nv:nv + 1, 0:cw], tot[nv:nv + 1, cw:2 * cw], tot[nv + 1:nv + 2, 0:cw]], axis=0), me * (cw // NDEV), cw // NDEV, axis=1),
        "w_gk2": lax.dynamic_slice_in_dim(tot[nv + 2:nv + 2 + rank * kw // d].reshape(rank, kw), me * (kw // NDEV), kw // NDEV, axis=1),
        "b_gk": tot[nv + 1:nv + 2, cw:cw + kw],
        "gla_norm": tot[nv + 1:nv + 2, cw + kw:cw + kw + dv],
    }
    small = {"b_ada": (b_ada, m_b_ada, v_b_ada), "norm_ffn1": (norm_ffn1, m_norm_ffn1, v_norm_ffn1), "norm_mix": (norm_mix, m_norm_mix, v_norm_mix),
             "norm_ffn2": (norm_ffn2, m_norm_ffn2, v_norm_ffn2), "norm_final": (norm_final, m_norm_final, v_norm_final),
             "conv_w": (conv_w, m_conv_w, v_conv_w), "w_gk2": (w_gk2, m_w_gk2, v_w_gk2), "b_gk": (b_gk, m_b_gk, v_b_gk),
             "gla_norm": (gla_norm, m_gla_norm, v_gla_norm)}
    names = list(small)
    flat = lambda a: a.reshape(-1, a.shape[-1])
    dl, mo, vo = _adam_smalls([flat(small[n][0]) for n in names], [g_small[n] for n in names],
                              [flat(small[n][1]) for n in names], [flat(small[n][2]) for n in names], "adam_smalls")
    for i, n in enumerate(names):
        shp = small[n][0].shape
        res[n] = [g_small[n].reshape(shp), dl[i].reshape(shp), mo[i].reshape(shp), vo[i].reshape(shp)]

    loss = tot[nv + 2 + rank * kw // d, 0]
    order = ["w_ada", "b_ada", "norm_ffn1", "w_ffn1_in", "w_ffn1_out", "norm_mix", "w_mix_in", "conv_w", "w_gk2", "b_gk", "gla_norm",
             "w_mix_out", "norm_ffn2", "w_ffn2_in", "w_ffn2_out", "norm_final"]
    return (loss, dx0[None], *[res[n][0] for n in order], *[res[n][1] for n in order], *[res[n][2] for n in order], *[res[n][3] for n in order])
```

```python
import collections
import functools

import jax
import jax.numpy as jnp
from jax import lax
from jax.experimental import pallas as pl
from jax.experimental.pallas import tpu as pltpu

F32 = jnp.float32
BF16 = jnp.bfloat16
S = jax.ShapeDtypeStruct

NDEV = 8
EPS = 1e-6
GATE_NORMALIZER = 16.0
CHUNK = 128
N_ADA = 9
ADAM_LR, ADAM_B1, ADAM_B2, ADAM_EPS, ADAM_WD, ADAM_STEP = 0.001, 0.9, 0.999, 1e-08, 0.01, 10
V7X_VMEM_LIMIT = 56 * 1024 * 1024
ROW_TILE = 512
WIDE_ROW_TILE = 1024
K_TILE = 1024
EPILOGUE_ROWS = 256
LANES = 128
BF16_ROWS = 16
PACK_ROWS = 24
ANY = pl.BlockSpec(memory_space=pl.ANY)


def _cp(*sem):
    return pltpu.CompilerParams(dimension_semantics=sem or None, vmem_limit_bytes=V7X_VMEM_LIMIT)


def _dot(a, b):
    return jnp.dot(a, b, preferred_element_type=F32)


def _dot_nt(a, b):
    return lax.dot_general(a, b, (((1,), (1,)), ((), ())), preferred_element_type=F32)


def _dot_tn(a, b):
    return lax.dot_general(a, b, (((0,), (0,)), ((), ())), preferred_element_type=F32)


def _rsum8(a):
    r, c = a.shape
    return jnp.sum(a.reshape(r // 8, 8, c), axis=0)


def _row_tile(r, c):
    for cand in (256, 128, 176, 88, 64, 32, 16, 8):
        if r % cand == 0 and cand * c * 4 <= 1024 * 1024:
            return cand
    return r


def _sigmoid(x):
    return 1.0 / (1.0 + jnp.exp(-x))


def _sigmoid_tanh(x):
    return 0.5 * jnp.tanh(0.5 * x) + 0.5


def _normmod(x, nrm, sc, sh):
    rstd = lax.rsqrt(jnp.mean(x * x, axis=-1, keepdims=True) + EPS)
    xhat = x * rstd
    return xhat, rstd, (xhat * nrm) * (1.0 + sc) + sh


def _normmod_bwd(dh, x, nrm, sc):
    rstd = lax.rsqrt(jnp.mean(x * x, axis=-1, keepdims=True) + EPS)
    xhat = x * rstd
    dxhat = dh * (nrm * (1.0 + sc))
    dx = rstd * (dxhat - xhat * jnp.mean(dxhat * xhat, axis=-1, keepdims=True))
    return dx, dh, dh * (xhat * nrm), dh * ((1.0 + sc) * xhat)


def _place():
    x, y, c = lax.axis_index("x"), lax.axis_index("y"), lax.axis_index("c")
    return x, y, c, 4 * x + 2 * y + c


def _peer(x, y, c, k):
    px = 1 - x if k & 4 else x
    py = 1 - y if k & 2 else y
    pc = 1 - c if k & 1 else c
    return (px, py, pc), 4 * px + 2 * py + pc


def _remote(src, dst, send_sem, recv_sem, peer):
    return pltpu.make_async_remote_copy(src_ref=src, dst_ref=dst, send_sem=send_sem, recv_sem=recv_sem,
                                        device_id=peer, device_id_type=pl.DeviceIdType.MESH)


_Plan = collections.namedtuple("_Plan", "inputs out_shapes sem_shapes start finish aliases", defaults=({},))


def _plan_all_to_all(xs, gather):
    n = len(xs)

    def copies(ins, outs, sems, landed):
        send_sems, recv_sems, local_sems = sems
        x, y, c, me = _place()
        local = [pltpu.make_async_copy(ins[i] if gather else ins[i].at[me], outs[i].at[me], local_sems.at[i]) for i in range(n)]
        remote = []
        for k in range(1, NDEV):
            peer, pid = _peer(x, y, c, k)
            for i in range(n):
                remote.append(_remote(ins[i] if gather else ins[i].at[pid], outs[i].at[pid if landed else me],
                                      send_sems.at[i, k - 1], recv_sems.at[i, k - 1], peer))
        return local, remote

    def start(ins, outs, sems):
        local, remote = copies(ins, outs, sems, False)
        for cp in local + remote:
            cp.start()

    def finish(ins, outs, sems):
        local, remote = copies(ins, outs, sems, True)
        for cp in remote + local:
            cp.wait()

    return _Plan(list(xs), [S((NDEV,) + a.shape, a.dtype) if gather else S(a.shape, a.dtype) for a in xs],
                 [pltpu.SemaphoreType.DMA((n, NDEV - 1)), pltpu.SemaphoreType.DMA((n, NDEV - 1)), pltpu.SemaphoreType.DMA((n,))],
                 start, finish)


def _other_chips(x, y):
    return [(1 - x, y), (x, 1 - y), (1 - x, 1 - y)]


def _plan_gather(xs, rows=None, into=None):
    n = len(xs)

    def copies(ins, outs, sems, rest):
        send_sems, recv_sems, local_sems = sems
        x, y, c, me = _place()
        sib, sib_id = (x, y, 1 - c), 4 * x + 2 * y + 1 - c
        chips = _other_chips(x, y)
        mine = lambda i: ins[i] if rows is None else ins[i].at[pl.ds(*rows)]
        slot_of = lambda i, s: outs[i].at[s] if rows is None else outs[i].at[s, pl.ds(*rows)]
        local = [pltpu.make_async_copy(mine(i), slot_of(i, me), local_sems.at[i]) for i in range(n)]
        first = [_remote(mine(i), slot_of(i, me), send_sems.at[i, 0], recv_sems.at[i, 0], sib) for i in range(n)]
        first += [_remote(mine(i), slot_of(i, me), send_sems.at[i, 1 + j], recv_sems.at[i, 1 + j], (px, py, c))
                  for j, (px, py) in enumerate(chips) for i in range(n)]
        if not rest:
            return local, first
        from_sibling = [_remote(mine(i), slot_of(i, sib_id), send_sems.at[i, 0], recv_sems.at[i, 0], sib) for i in range(n)]
        arrive, forward = [], []
        for j, (px, py) in enumerate(chips):
            s = 4 * px + 2 * py
            arrive.append([_remote(mine(i), slot_of(i, s + c), send_sems.at[i, 1 + j], recv_sems.at[i, 1 + j], (px, py, c)) for i in range(n)])
            forward.append([_remote(slot_of(i, s + c), slot_of(i, s + c), send_sems.at[i, 4 + j], recv_sems.at[i, 4 + j], sib) for i in range(n)])
            from_sibling += [_remote(mine(i), slot_of(i, s + 1 - c), send_sems.at[i, 4 + j], recv_sems.at[i, 4 + j], sib) for i in range(n)]
        return local, first, arrive, forward, from_sibling

    def start(ins, outs, sems):
        local, first = copies(ins, outs, sems, False)
        for cp in local + first:
            cp.start()

    def finish(ins, outs, sems):
        local, first, arrive, forward, from_sibling = copies(ins, outs, sems, True)
        for landed, onward in zip(arrive, forward):
            for cp in landed:
                cp.wait_recv()
            for cp in onward:
                cp.start()
        for cp in from_sibling:
            cp.wait_recv()
        for cp in first + [cp for onward in forward for cp in onward]:
            cp.wait_send()
        for cp in local:
            cp.wait()

    return _Plan(list(xs) + list(into or []), [S((NDEV,) + a.shape, a.dtype) for a in xs],
                 [pltpu.SemaphoreType.DMA((n, NDEV - 1)), pltpu.SemaphoreType.DMA((n, NDEV - 1)), pltpu.SemaphoreType.DMA((n,))],
                 start, finish, {n + i: i for i in range(len(into or []))})


def _plan_sibling_swap(gs):
    n = len(gs)

    def copies(ins, outs, sems):
        send_sems, recv_sems = sems
        x, y, c, _ = _place()
        return [_remote(ins[i].at[2 * j + 1 - c], outs[i].at[j], send_sems.at[i, j], recv_sems.at[i, j], (x, y, 1 - c))
                for i in range(n) for j in range(NDEV // 2)]

    def start(ins, outs, sems):
        for cp in copies(ins, outs, sems):
            cp.start()

    def finish(ins, outs, sems):
        for cp in copies(ins, outs, sems):
            cp.wait()

    return _Plan(list(gs), [S((NDEV // 2,) + a.shape[1:], a.dtype) for a in gs],
                 [pltpu.SemaphoreType.DMA((n, NDEV // 2)), pltpu.SemaphoreType.DMA((n, NDEV // 2))], start, finish)


def _pair_add(g, r1, core, name):
    _, r, c = g.shape
    tr = r if r * c * 2 <= 2 * 1024 * 1024 else _row_tile(r, c)

    def body(core_ref, g_ref, r_ref, o_ref):
        o_ref[...] = (g_ref[...].astype(F32) + r_ref[...].astype(F32)).astype(BF16)

    return pl.pallas_call(
        body, name=name,
        grid_spec=pltpu.PrefetchScalarGridSpec(
            num_scalar_prefetch=1, grid=(NDEV // 2, r // tr),
            in_specs=[pl.BlockSpec((None, tr, c), lambda j, k, core_ref: (2 * j + core_ref[0], k, 0)),
                      pl.BlockSpec((None, tr, c), lambda j, k, core_ref: (j, k, 0))],
            out_specs=pl.BlockSpec((None, tr, c), lambda j, k, core_ref: (j, k, 0))),
        out_shape=S((NDEV // 2, r, c), BF16), compiler_params=_cp("parallel", "parallel"),
    )(core, g, r1)


def _plan_chip_swap(ps, rows=None, into=None):
    n = len(ps)

    def copies(ins, outs, sems, landed):
        send_sems, recv_sems, local_sems = sems
        x, y, c, _ = _place()
        mine = 2 * x + y
        part = lambda ref, s: ref.at[s] if rows is None else ref.at[s, pl.ds(*rows)]
        local = [pltpu.make_async_copy(part(ins[i], mine), part(outs[i], mine), local_sems.at[i]) for i in range(n)]
        remote = [_remote(part(ins[i], 2 * px + py), part(outs[i], 2 * px + py if landed else mine),
                          send_sems.at[i, j], recv_sems.at[i, j], (px, py, c))
                  for j, (px, py) in enumerate(_other_chips(x, y)) for i in range(n)]
        return local, remote

    def start(ins, outs, sems):
        local, remote = copies(ins, outs, sems, False)
        for cp in local + remote:
            cp.start()

    def finish(ins, outs, sems):
        local, remote = copies(ins, outs, sems, True)
        for cp in remote + local:
            cp.wait()

    return _Plan(list(ps) + list(into or []), [S(a.shape, a.dtype) for a in ps],
                 [pltpu.SemaphoreType.DMA((n, 3)), pltpu.SemaphoreType.DMA((n, 3)), pltpu.SemaphoreType.DMA((n,))], start, finish,
                 {n + i: i for i in range(len(into or []))})


def _pcall(body, name, args, in_specs, out_shape, out_specs, grid=(), scratch_shapes=(), sem=(), plans=(), starts_plans=False):
    n_in, n_out, n_scr = len(args), len(out_shape), len(scratch_shapes)
    counts = [(len(p.inputs), len(p.out_shapes), len(p.sem_shapes)) for p in plans]
    c_args = [a for p in plans for a in p.inputs]
    c_outs = [s for p in plans for s in p.out_shapes]
    c_sems = [s for p in plans for s in p.sem_shapes]

    def wrapped(*refs):
        cuts = [n_in, len(c_args), n_out, len(c_outs), n_scr, len(c_sems)]
        ins, c_in, outs, c_out, scr, c_sem = [refs[sum(cuts[:k]):sum(cuts[:k + 1])] for k in range(6)]

        def halves(which):
            a = b = s = 0
            for p, (na, nb, ns) in zip(plans, counts):
                getattr(p, which)(c_in[a:a + na], c_out[b:b + nb], c_sem[s:s + ns])
                a, b, s = a + na, b + nb, s + ns

        if not plans:
            body(*ins, *outs, *scr)
        elif not grid and starts_plans:
            body(lambda: halves("start"), *ins, *outs, *scr)
            halves("finish")
        elif not grid:
            halves("start")
            body(*ins, *outs, *scr)
            halves("finish")
        else:
            first = functools.reduce(jnp.logical_and, [pl.program_id(a) == 0 for a in range(len(grid))])
            last = functools.reduce(jnp.logical_and, [pl.program_id(a) == grid[a] - 1 for a in range(len(grid))])
            pl.when(first)(lambda: halves("start"))
            body(*ins, *outs, *scr)
            pl.when(last)(lambda: halves("finish"))

    aliases, a, b = {}, n_in, n_out
    for p, (na, nb, _) in zip(plans, counts):
        aliases.update({a + k: b + v for k, v in p.aliases.items()})
        a, b = a + na, b + nb
    res = pl.pallas_call(
        wrapped, name=name, grid=grid, in_specs=list(in_specs) + [ANY] * len(c_args),
        out_shape=list(out_shape) + c_outs, out_specs=list(out_specs) + [ANY] * len(c_outs),
        scratch_shapes=list(scratch_shapes) + c_sems, input_output_aliases=aliases,
        compiler_params=_cp(*(("arbitrary",) * len(grid) if plans else sem)),
    )(*args, *c_args)
    c_res, b = [], n_out
    for _, nb, _ in counts:
        c_res.append(res[b:b + nb])
        b += nb
    return res[:n_out], c_res


def _exchange(plans, name):
    return _pcall(lambda: None, name, [], [], [], [], plans=plans)[1]


def _ada_rows(c, w_ada, b_ada, name, plans=()):
    d, cols = c.shape[1], w_ada.shape[1]
    gather_c = _plan_all_to_all([c], True)
    gather_p = _plan_all_to_all([S((NDEV, cols), F32)], True)
    n_sem = len(gather_c.sem_shapes)

    def body(start_plans, c_ref, w_ref, b_ref, ada_ref, act_ref, c_all, p_mine, p_all, *sems):
        gather_c.start([c_ref], [c_all], sems[:n_sem])
        gather_c.finish([c_ref], [c_all], sems[:n_sem])
        for s in range(NDEV):
            cc = c_all[s]
            act_ref[s:s + 1, :] = cc * _sigmoid(cc)
        p_mine[...] = _dot(act_ref[...].astype(BF16), w_ref[...].astype(BF16))
        gather_p.start([p_mine], [p_all], sems[n_sem:])
        start_plans()
        gather_p.finish([p_mine], [p_all], sems[n_sem:])
        me = _place()[3]
        for s in range(NDEV):
            ada_ref[:, s * cols:(s + 1) * cols] = p_all[s, pl.ds(me, 1), :] + b_ref[:, s * cols:(s + 1) * cols]

    whole = pl.BlockSpec(memory_space=pltpu.VMEM)
    return _pcall(body, name, [c, w_ada, b_ada], [whole] * 3, [S((1, NDEV * cols), F32), S((NDEV, d), F32)], [whole] * 2,
                  scratch_shapes=[pltpu.VMEM((NDEV,) + c.shape, F32), pltpu.VMEM((NDEV, cols), F32), pltpu.VMEM((NDEV, NDEV, cols), F32)]
                  + gather_c.sem_shapes + gather_p.sem_shapes, plans=plans, starts_plans=True)


def _ffn_in(x, ada9, nrm, w_in, sh_row, sc_row, name, plans=()):
    t, d = x.shape
    nb, bw = w_in.shape[0] // 2, w_in.shape[1]
    tm = min(WIDE_ROW_TILE, t)

    def body(x_ref, ada_ref, n_ref, wg_ref, wu_ref, h_ref, gu_ref, s_ref):
        @pl.when(pl.program_id(1) == 0)
        def _():
            _, _, h = _normmod(x_ref[...], n_ref[...], ada_ref[sc_row:sc_row + 1, :], ada_ref[sh_row:sh_row + 1, :])
            h_ref[...] = h.astype(BF16)

        h = h_ref[...]
        g = _dot_nt(h, wg_ref[...])
        u = _dot_nt(h, wu_ref[...])
        gu_ref[0] = g.astype(BF16)
        gu_ref[1] = u.astype(BF16)
        s_ref[...] = (g * _sigmoid(g) * u).astype(BF16)

    return _pcall(
        body, name, [x, ada9, nrm, w_in, w_in], grid=(t // tm, nb),
        in_specs=[pl.BlockSpec((tm, d), lambda i, j: (i, 0)), pl.BlockSpec((N_ADA, d), lambda i, j: (0, 0)),
                  pl.BlockSpec((1, d), lambda i, j: (0, 0)),
                  pl.BlockSpec((None, bw, d), lambda i, j: (j, 0, 0)), pl.BlockSpec((None, bw, d), lambda i, j: (j + nb, 0, 0))],
        out_shape=[S((t, d), BF16), S((2, nb, t, bw), BF16), S((nb, t, bw), BF16)],
        out_specs=[pl.BlockSpec((tm, d), lambda i, j: (i, 0)), pl.BlockSpec((2, None, tm, bw), lambda i, j: (0, j, i, 0)),
                   pl.BlockSpec((None, tm, bw), lambda i, j: (j, i, 0))],
        sem=("parallel", "arbitrary"), plans=plans)


def _ffn_out(s, w_out, x, ada9, g_row, res_scale, name, plans=()):
    nb, t, bw = s.shape
    d = x.shape[1]
    tm = min(ROW_TILE, t)

    def body(s_ref, w_ref, x_ref, ada_ref, xo_ref, f_ref):
        acc = _dot(s_ref[0], w_ref[0])
        for b in range(1, nb):
            acc = acc + _dot(s_ref[b], w_ref[b])
        f_ref[...] = acc.astype(BF16)
        xo_ref[...] = x_ref[...] + (res_scale * ada_ref[g_row:g_row + 1, :]) * acc

    return _pcall(
        body, name, [s, w_out, x, ada9], grid=(t // tm,),
        in_specs=[pl.BlockSpec((nb, tm, bw), lambda i: (0, i, 0)), pl.BlockSpec((nb, bw, d), lambda i: (0, 0, 0)),
                  pl.BlockSpec((tm, d), lambda i: (i, 0)), pl.BlockSpec((N_ADA, d), lambda i: (0, 0))],
        out_shape=[S((t, d), F32), S((t, d), BF16)],
        out_specs=[pl.BlockSpec((tm, d), lambda i: (i, 0)), pl.BlockSpec((tm, d), lambda i: (i, 0))],
        sem=("parallel",), plans=plans)


def _ffn_bwd_ds(dxo, f, ada9, w_out, gu, g_row, res_scale, name, plans=()):
    t, d = dxo.shape
    nb, bw = w_out.shape[0], w_out.shape[1]
    tm = min(WIDE_ROW_TILE, t)
    ni = t // tm

    def body(dxo_ref, f_ref, ada_ref, w_ref, gu_ref, da_ref, dg_ref, dw_ref, df_ref, acc_ref):
        i, j = pl.program_id(0), pl.program_id(1)

        @pl.when((i == 0) & (j == 0))
        def _():
            dg_ref[...] = jnp.zeros_like(dg_ref)

        @pl.when(j == 0)
        def _():
            dxo_t = dxo_ref[...]
            df_ref[...] = ((res_scale * ada_ref[g_row:g_row + 1, :]) * dxo_t).astype(BF16)
            dg_ref[...] += res_scale * _rsum8(dxo_t * f_ref[...].astype(F32))

        df = df_ref[...]
        ds = _dot_nt(df, w_ref[...])
        g = gu_ref[0].astype(F32)
        u = gu_ref[1].astype(F32)
        sg = _sigmoid_tanh(g)
        silu = g * sg
        da_ref[0] = (ds * u * (sg * (1.0 + g * (1.0 - sg)))).astype(BF16)
        da_ref[1] = (ds * silu).astype(BF16)
        part = _dot_tn((silu * u).astype(BF16), df)

        @pl.when(i == 0)
        def _():
            acc_ref[j] = part

        @pl.when(i > 0)
        def _():
            acc_ref[j] += part

        @pl.when(i == ni - 1)
        def _():
            dw_ref[...] = acc_ref[j].astype(BF16)

    return _pcall(
        body, name, [dxo, f, ada9, w_out, gu], grid=(ni, nb),
        in_specs=[pl.BlockSpec((tm, d), lambda i, j: (i, 0)), pl.BlockSpec((tm, d), lambda i, j: (i, 0)),
                  pl.BlockSpec((N_ADA, d), lambda i, j: (0, 0)), pl.BlockSpec((None, bw, d), lambda i, j: (j, 0, 0)),
                  pl.BlockSpec((2, None, tm, bw), lambda i, j: (0, j, i, 0))],
        out_shape=[S((2, nb, t, bw), BF16), S((8, d), F32), S((nb, bw, d), BF16)],
        out_specs=[pl.BlockSpec((2, None, tm, bw), lambda i, j: (0, j, i, 0)), pl.BlockSpec((8, d), lambda i, j: (0, 0)),
                   pl.BlockSpec((None, bw, d), lambda i, j: (jnp.where(i == ni - 1, j, 0), 0, 0))],
        scratch_shapes=[pltpu.VMEM((tm, d), BF16), pltpu.VMEM((nb, bw, d), F32)],
        sem=("arbitrary", "arbitrary"), plans=plans)


def _ffn_bwd_dh(da, w_in, x, dxo, ada9, nrm, sh_row, sc_row, name, plans=()):
    t, d = x.shape
    nb, bw = w_in.shape[0] // 2, w_in.shape[1]
    tm = min(WIDE_ROW_TILE, t)

    def body(da_ref, wg_ref, wu_ref, x_ref, dxo_ref, ada_ref, n_ref, dx_ref, dsh_ref, dsc_ref, dn_ref, acc_ref):
        i, j = pl.program_id(0), pl.program_id(1)

        @pl.when((i == 0) & (j == 0))
        def _():
            dsh_ref[...] = jnp.zeros_like(dsh_ref)
            dsc_ref[...] = jnp.zeros_like(dsc_ref)
            dn_ref[...] = jnp.zeros_like(dn_ref)

        part = _dot(da_ref[0], wg_ref[...]) + _dot(da_ref[1], wu_ref[...])

        @pl.when(j == 0)
        def _():
            acc_ref[...] = part

        @pl.when(j > 0)
        def _():
            acc_ref[...] += part

        @pl.when(j == nb - 1)
        def _():
            for r0 in range(0, tm, min(EPILOGUE_ROWS, tm)):
                rows = slice(r0, r0 + min(EPILOGUE_ROWS, tm))
                dx, tsh, tsc, tn = _normmod_bwd(acc_ref[rows, :], x_ref[rows, :], n_ref[...], ada_ref[sc_row:sc_row + 1, :])
                dx_ref[rows, :] = dxo_ref[rows, :] + dx
                dsh_ref[...] += _rsum8(tsh)
                dsc_ref[...] += _rsum8(tsc)
                dn_ref[...] += _rsum8(tn)

    vec = pl.BlockSpec((8, d), lambda i, j: (0, 0))
    return _pcall(
        body, name, [da, w_in, w_in, x, dxo, ada9, nrm], grid=(t // tm, nb),
        in_specs=[pl.BlockSpec((2, None, tm, bw), lambda i, j: (0, j, i, 0)),
                  pl.BlockSpec((None, bw, d), lambda i, j: (j, 0, 0)), pl.BlockSpec((None, bw, d), lambda i, j: (j + nb, 0, 0)),
                  pl.BlockSpec((tm, d), lambda i, j: (i, 0)), pl.BlockSpec((tm, d), lambda i, j: (i, 0)),
                  pl.BlockSpec((N_ADA, d), lambda i, j: (0, 0)), pl.BlockSpec((1, d), lambda i, j: (0, 0))],
        out_shape=[S((t, d), F32), S((8, d), F32), S((8, d), F32), S((8, d), F32)],
        out_specs=[pl.BlockSpec((tm, d), lambda i, j: (i, 0)), vec, vec, vec],
        scratch_shapes=[pltpu.VMEM((tm, d), F32)],
        sem=("arbitrary", "arbitrary"), plans=plans)


def _tn_matmul(a, b, a_block, a_map, b_block, b_map, out_shape, out_block, out_map, nblk, name, plans=()):
    t = a.shape[-2]
    tk = min(K_TILE, t)
    nk = t // tk

    def body(a_ref, b_ref, o_ref, acc_ref):
        k = pl.program_id(1)
        for q in (range(a_ref.shape[0]) if len(a_ref.shape) == 3 else [Ellipsis]):
            part = _dot_tn(a_ref[q], b_ref[...])

            @pl.when(k == 0)
            def _():
                acc_ref[q] = part

            @pl.when(k > 0)
            def _():
                acc_ref[q] += part

        @pl.when(k == nk - 1)
        def _():
            o_ref[...] = acc_ref[...].astype(BF16)

    (out,), moved = _pcall(
        body, name, [a, b], grid=(nblk, nk),
        in_specs=[pl.BlockSpec(a_block(tk), a_map), pl.BlockSpec(b_block(tk), b_map)],
        out_shape=[S(out_shape, BF16)], out_specs=[pl.BlockSpec(out_block, out_map)],
        scratch_shapes=[pltpu.VMEM(tuple(n for n in out_block if n is not None), F32)],
        sem=("parallel", "arbitrary"), plans=plans)
    return out, moved


def _ffn_backward(dxo, x_in, h, gu, f, ada9, nrm, w_in, w_out, rows, core, name, ds_plans=(), dwin_plans=(), later=False):
    sh_row, sc_row, g_row = rows
    _, nb, t, bw = gu.shape
    d = x_in.shape[1]
    (da, dg, dw_out), ds_moved = _ffn_bwd_ds(dxo, f, ada9, w_out, gu, g_row, 0.5, name + "_ds", plans=ds_plans)
    dw_out = dw_out.reshape(NDEV, -1, d)
    dw_in, ((half_out,), *dwin_moved) = _tn_matmul(
        da.reshape(2 * nb, t, bw), h, lambda tk: (2, tk, bw), lambda sb, k: (sb, k, 0), lambda tk: (tk, d), lambda sb, k: (k, 0),
        (2 * nb, bw, d), (2, bw, d), lambda sb, k: (sb, 0, 0), nb, name + "_dwin",
        plans=[_plan_sibling_swap([dw_out])] + list(dwin_plans(ds_moved) if callable(dwin_plans) else dwin_plans))
    sum_out = _pair_add(dw_out, half_out, core, name + "_dwout_add")
    if later:
        (dx, dsh, dsc, dn), ((half_in,),) = _ffn_bwd_dh(da, w_in, x_in, dxo, ada9, nrm, sh_row, sc_row, name + "_dh",
                                                       plans=[_plan_sibling_swap([dw_in])])
        return dx, _pair_add(dw_in, half_in, core, name + "_dwin_add"), sum_out, (dsh, dsc, dg, dn), ds_moved, dwin_moved
    ((half_in,),) = _exchange([_plan_sibling_swap([dw_in])], name + "_dwin_swap")
    sum_in = _pair_add(dw_in, half_in, core, name + "_dwin_add")
    head = sum_out.shape[1] // 2
    (dx, dsh, dsc, dn), ((recv_in,), (out_begun,)) = _ffn_bwd_dh(
        da, w_in, x_in, dxo, ada9, nrm, sh_row, sc_row, name + "_dh",
        plans=[_plan_chip_swap([sum_in]), _plan_chip_swap([sum_out], rows=(0, head))])
    return dx, recv_in, (sum_out, out_begun, head), (dsh, dsc, dg, dn), ds_moved, dwin_moved


def _mix_in(x, ada9, nrm, w, widths, dts, name, plans=()):
    t, d = x.shape
    n = w.shape[0]
    tm = min(ROW_TILE, t)
    starts = [sum(widths[:i]) for i in range(len(widths))]

    def body(x_ref, ada_ref, n_ref, w_ref, h_ref, *out_refs):
        _, _, h = _normmod(x_ref[...], n_ref[...], ada_ref[4:5, :], ada_ref[3:4, :])
        hb = h.astype(BF16)
        h_ref[...] = hb
        for o_ref, st, wd in zip(out_refs, starts, widths):
            o_ref[...] = _dot_nt(hb, w_ref[st:st + wd, :]).astype(o_ref.dtype)

    return _pcall(
        body, name, [x, ada9, nrm, w], grid=(t // tm,),
        in_specs=[pl.BlockSpec((tm, d), lambda i: (i, 0)), pl.BlockSpec((N_ADA, d), lambda i: (0, 0)),
                  pl.BlockSpec((1, d), lambda i: (0, 0)), pl.BlockSpec((n, d), lambda i: (0, 0))],
        out_shape=[S((t, d), BF16)] + [S((t, wd), dt) for wd, dt in zip(widths, dts)],
        out_specs=[pl.BlockSpec((tm, d), lambda i: (i, 0))] + [pl.BlockSpec((tm, wd), lambda i: (i, 0)) for wd in widths],
        sem=("parallel",), plans=plans)


def _tri(lower):
    r = lax.broadcasted_iota(jnp.int32, (CHUNK, CHUNK), 0)
    c = lax.broadcasted_iota(jnp.int32, (CHUNK, CHUNK), 1)
    return (r >= c) if lower else (c >= r)


def _dot_01(m, x):
    hi = x.astype(BF16)
    r1 = x - hi.astype(F32)
    mid = r1.astype(BF16)
    lo = (r1 - mid.astype(F32)).astype(BF16)
    return _dot(m, hi) + _dot(m, mid) + _dot(m, lo)


def _gla_chunk_terms(q, k, lg, low01):
    b = _dot_01(low01, lg)
    bl = b[CHUNK - 1:CHUNK, :]
    r = 0.5 * bl
    eb, ebl, em, en = jnp.exp(b), jnp.exp(bl - b), jnp.exp(b - r), jnp.exp(r - b)
    return eb, ebl, em, en, jnp.exp(bl), q * eb, k * ebl, q * em, k * en


def _scores(qm_h, knp, qk1_h):
    r = lax.broadcasted_iota(jnp.int32, (CHUNK, CHUNK), 0)
    c = lax.broadcasted_iota(jnp.int32, (CHUNK, CHUNK), 1)
    p = jnp.where(r > c, _dot_nt(qm_h, knp), 0.0)
    return jnp.where(r == c, jnp.sum(qk1_h, axis=1, keepdims=True), p)


def _gla_fwd(qk, v, gl, wg, bg, heads, name, plans=()):
    t = qk.shape[0]
    kw, vw = qk.shape[1] // 2, v.shape[1]
    dk, dv = kw // heads, vw // heads
    assert dk == 64 and dv == 128 and kw % 128 == 0
    gt = min(ROW_TILE, t)
    nc = gt // CHUNK
    scale = dk ** -0.5

    def body(qk_ref, v_ref, gl_ref, wg_ref, bg_ref, o_ref, lg_ref, sall_ref, st_ref):
        @pl.when(pl.program_id(0) == 0)
        def _():
            st_ref[...] = jnp.zeros_like(st_ref)

        gk = _dot(gl_ref[...].astype(BF16), wg_ref[...]) + bg_ref[...]
        lg_ref[...] = (jnp.minimum(gk, 0.0) - jnp.log(1.0 + jnp.exp(-jnp.abs(gk)))) / GATE_NORMALIZER
        low01 = _tri(True).astype(BF16)
        lane = lax.broadcasted_iota(jnp.int32, (CHUNK, LANES), 1)

        def chunk(ci, carry):
            rows = pl.ds(pl.multiple_of(ci * CHUNK, CHUNK), CHUNK)
            q = qk_ref[rows, 0:kw] * scale
            k = qk_ref[rows, kw:2 * kw]
            qk1 = q.astype(BF16).astype(F32) * k.astype(BF16).astype(F32)
            eb, ebl, em, en, ebl_row, qe, ke, qm, kn = _gla_chunk_terms(q, k, lg_ref[rows, :], low01)
            for h in range(heads):
                lanes = slice(LANES * (h // 2), LANES * (h // 2) + LANES)
                own = (lane < 64) if h % 2 == 0 else (lane >= 64)
                knp = kn[:, lanes].astype(BF16)
                qm_h = jnp.where(own, qm[:, lanes], 0.0).astype(BF16)
                qe_h = jnp.where(own, qe[:, lanes], 0.0).astype(BF16)
                ke_h = jnp.where(own, ke[:, lanes], 0.0).astype(BF16)
                v_h = v_ref[rows, h * dv:(h + 1) * dv]
                st = st_ref[h]
                sall_ref[ci, h] = st
                p = _scores(qm_h, knp, jnp.where(own, qk1[:, lanes], 0.0))
                o_ref[rows, h * dv:(h + 1) * dv] = _dot(p.astype(BF16), v_h) + _dot_nt(qe_h, st.astype(BF16))
                st_ref[h] = st * ebl_row[:, lanes] + _dot_tn(v_h, ke_h)
            return carry

        lax.fori_loop(0, nc, chunk, 0, unroll=True)

    return _pcall(
        body, name, [qk, v, gl, wg, bg], grid=(t // gt,),
        in_specs=[pl.BlockSpec((gt, 2 * kw), lambda i: (i, 0)), pl.BlockSpec((gt, vw), lambda i: (i, 0)),
                  pl.BlockSpec((gt, LANES), lambda i: (i, 0)), pl.BlockSpec((LANES, kw), lambda i: (0, 0)),
                  pl.BlockSpec((1, kw), lambda i: (0, 0))],
        out_shape=[S((t, vw), F32), S((t, kw), F32), S((t // CHUNK, heads, dv, LANES), F32)],
        out_specs=[pl.BlockSpec((gt, vw), lambda i: (i, 0)), pl.BlockSpec((gt, kw), lambda i: (i, 0)),
                   pl.BlockSpec((nc, heads, dv, LANES), lambda i: (i, 0, 0, 0))],
        scratch_shapes=[pltpu.VMEM((heads, dv, LANES), F32)],
        sem=("arbitrary",), plans=plans)


def _gla_bwd(qk, v, lg, do, sall, gl, wg, heads, name, plans=()):
    t = qk.shape[0]
    kw, vw = qk.shape[1] // 2, v.shape[1]
    dk, dv = kw // heads, vw // heads
    gt = min(ROW_TILE, t)
    nc = gt // CHUNK
    nt = t // gt
    scale = dk ** -0.5

    def body(qk_ref, v_ref, lg_ref, do_ref, sall_ref, gl_ref, wg_ref, dqk_ref, dv_ref, dgl_ref, dwg_ref, dbg_ref, dst_ref, dgk_ref):
        @pl.when(pl.program_id(0) == 0)
        def _():
            dst_ref[...] = jnp.zeros_like(dst_ref)
            dwg_ref[...] = jnp.zeros_like(dwg_ref)
            dbg_ref[...] = jnp.zeros_like(dbg_ref)

        low01 = _tri(True).astype(BF16)
        up01 = _tri(False).astype(BF16)
        causal = _tri(True)
        lane = lax.broadcasted_iota(jnp.int32, (CHUNK, LANES), 1)
        last_row = lax.broadcasted_iota(jnp.int32, (CHUNK, kw), 0) == CHUNK - 1

        def chunk(cj, carry):
            ci = nc - 1 - cj
            rows = pl.ds(pl.multiple_of(ci * CHUNK, CHUNK), CHUNK)
            q = qk_ref[rows, 0:kw] * scale
            k = qk_ref[rows, kw:2 * kw]
            qk1 = q.astype(BF16).astype(F32) * k.astype(BF16).astype(F32)
            lgc = lg_ref[rows, :]
            eb, ebl, em, en, ebl_row, qe, ke, qm, kn = _gla_chunk_terms(q, k, lgc, low01)
            dqe, dqm, dkn, dke, drow = [], [], [], [], []
            for pr in range(kw // LANES):
                lanes = slice(LANES * pr, LANES * pr + LANES)
                knp = kn[:, lanes].astype(BF16)
                parts = []
                for half in range(2):
                    h = 2 * pr + half
                    own = (lane < 64) if half == 0 else (lane >= 64)
                    qm_h = jnp.where(own, qm[:, lanes], 0.0).astype(BF16)
                    qe_h = jnp.where(own, qe[:, lanes], 0.0).astype(BF16)
                    ke_h = jnp.where(own, ke[:, lanes], 0.0).astype(BF16)
                    v_h = v_ref[rows, h * dv:(h + 1) * dv]
                    do_h = do_ref[rows, h * dv:(h + 1) * dv]
                    st = sall_ref[ci, h]
                    dst = dst_ref[h]
                    stb, dstb = st.astype(BF16), dst.astype(BF16)
                    p = _scores(qm_h, knp, jnp.where(own, qk1[:, lanes], 0.0)).astype(BF16)
                    dp = jnp.where(causal, _dot_nt(do_h, v_h), 0.0).astype(BF16)
                    dv_ref[rows, h * dv:(h + 1) * dv] = (_dot_tn(p, do_h) + _dot_nt(ke_h, dstb)).astype(BF16)
                    parts.append((jnp.where(own, _dot(dp, knp), 0.0), _dot_tn(dp, qm_h), _dot(do_h, stb), _dot(v_h, dstb),
                                  jnp.sum(st * dst, axis=0, keepdims=True)))
                    dst_ref[h] = dst * ebl_row[:, lanes] + _dot_tn(do_h, qe_h)
                dqm.append(parts[0][0] + parts[1][0])
                dkn.append(parts[0][1] + parts[1][1])
                dqe.append(parts[0][2] + parts[1][2])
                dke.append(parts[0][3] + parts[1][3])
                drow.append(parts[0][4] + parts[1][4])
            dqm, dkn, dqe, dke, drow = [jnp.concatenate(a, axis=1) for a in (dqm, dkn, dqe, dke, drow)]
            dqk_ref[rows, 0:kw] = ((dqe * eb + dqm * em) * scale).astype(BF16)
            dqk_ref[rows, kw:2 * kw] = (dke * ebl + dkn * en).astype(BF16)
            tke = dke * ke
            db = dqe * qe + dqm * qm - dkn * kn - tke
            dbl = jnp.sum(tke, axis=0, keepdims=True) + drow * ebl_row
            db = db + jnp.where(last_row, dbl, 0.0)
            dlg = _dot_01(up01, db)
            dgk_ref[rows, :] = dlg * ((1.0 - jnp.exp(GATE_NORMALIZER * lgc)) / GATE_NORMALIZER)
            return carry

        lax.fori_loop(0, nc, chunk, 0, unroll=True)
        dgk = dgk_ref[...]
        dgkb = dgk.astype(BF16)
        dgl_ref[...] = _dot_nt(dgkb, wg_ref[...]).astype(BF16)
        dwg_ref[...] += _dot_tn(gl_ref[...].astype(BF16), dgkb)
        dbg_ref[...] += _rsum8(dgk)

    rev = lambda i: (nt - 1 - i, 0)
    return _pcall(
        body, name, [qk, v, lg, do, sall, gl, wg], grid=(nt,),
        in_specs=[pl.BlockSpec((gt, 2 * kw), rev), pl.BlockSpec((gt, vw), rev), pl.BlockSpec((gt, kw), rev),
                  pl.BlockSpec((gt, vw), rev), pl.BlockSpec((nc, heads, dv, LANES), lambda i: (nt - 1 - i, 0, 0, 0)),
                  pl.BlockSpec((gt, LANES), rev), pl.BlockSpec((LANES, kw), lambda i: (0, 0))],
        out_shape=[S((t, 2 * kw), BF16), S((t, vw), BF16), S((t, LANES), BF16), S((LANES, kw), F32), S((8, kw), F32)],
        out_specs=[pl.BlockSpec((gt, 2 * kw), rev), pl.BlockSpec((gt, vw), rev), pl.BlockSpec((gt, LANES), rev),
                   pl.BlockSpec((LANES, kw), lambda i: (0, 0)), pl.BlockSpec((8, kw), lambda i: (0, 0))],
        scratch_shapes=[pltpu.VMEM((heads, dv, LANES), F32), pltpu.VMEM((gt, kw), F32)],
        sem=("arbitrary",), plans=plans)


def _conv_taps(cx_ref, halo_ref, first, cw):
    tm = cx_ref.shape[0]
    u = cx_ref[:, cw:2 * cw].astype(F32) * cx_ref[:, 2 * cw:3 * cw].astype(F32)
    uh = halo_ref[:, cw:2 * cw].astype(F32) * halo_ref[:, 2 * cw:3 * cw].astype(F32)
    uh = jnp.where(first, 0.0, uh)
    before1, before2 = uh[BF16_ROWS - 1:BF16_ROWS, :], uh[BF16_ROWS - 2:BF16_ROWS - 1, :]
    row = lax.broadcasted_iota(jnp.int32, (tm, cw), 0)
    u1 = jnp.where(row == 0, before1, pltpu.roll(u, 1, 0))
    u2 = jnp.where(row == 0, before2, jnp.where(row == 1, before1, pltpu.roll(u, 2, 0)))
    return u, u1, u2


def _head_norm(o_h, gn):
    rstd = lax.rsqrt(jnp.mean(o_h * o_h, axis=-1, keepdims=True) + EPS)
    ohat = o_h * rstd
    return ohat, rstd, ohat * gn


def _mix_out(cx, o, go, conv_w, gn, w_out, x, ada9, heads, name, plans=()):
    t, d = x.shape
    cw, vw = conv_w.shape[1], o.shape[1]
    dv = vw // heads
    tm = min(ROW_TILE, t)

    def body(cx_ref, halo_ref, o_ref, go_ref, cwt_ref, gn_ref, w_ref, x_ref, ada_ref, xo_ref, m_ref, y_ref):
        u, u1, u2 = _conv_taps(cx_ref, halo_ref, pl.program_id(0) == 0, cw)
        yc = cwt_ref[0:1, :] * u2 + cwt_ref[1:2, :] * u1 + cwt_ref[2:3, :] * u
        y_ref[:, 0:cw] = (cx_ref[:, 0:cw].astype(F32) * yc).astype(BF16)
        for h in range(heads):
            cols = slice(h * dv, (h + 1) * dv)
            _, _, on = _head_norm(o_ref[:, cols], gn_ref[...])
            g = go_ref[:, cols].astype(F32)
            y_ref[:, cw + h * dv:cw + (h + 1) * dv] = (on * (g * _sigmoid(g))).astype(BF16)
        m = _dot(y_ref[...], w_ref[...])
        m_ref[...] = m.astype(BF16)
        xo_ref[...] = x_ref[...] + ada_ref[5:6, :] * m

    return _pcall(
        body, name, [cx, cx, o, go, conv_w, gn, w_out, x, ada9], grid=(t // tm,),
        in_specs=[pl.BlockSpec((tm, 3 * cw), lambda i: (i, 0)),
                  pl.BlockSpec((BF16_ROWS, 3 * cw), lambda i: (jnp.maximum(i * (tm // BF16_ROWS) - 1, 0), 0)),
                  pl.BlockSpec((tm, vw), lambda i: (i, 0)), pl.BlockSpec((tm, vw), lambda i: (i, 0)),
                  pl.BlockSpec((3, cw), lambda i: (0, 0)), pl.BlockSpec((1, dv), lambda i: (0, 0)),
                  pl.BlockSpec((cw + vw, d), lambda i: (0, 0)), pl.BlockSpec((tm, d), lambda i: (i, 0)),
                  pl.BlockSpec((N_ADA, d), lambda i: (0, 0))],
        out_shape=[S((t, d), F32), S((t, d), BF16), S((t, cw + vw), BF16)],
        out_specs=[pl.BlockSpec((tm, d), lambda i: (i, 0)), pl.BlockSpec((tm, d), lambda i: (i, 0)),
                   pl.BlockSpec((tm, cw + vw), lambda i: (i, 0))],
        sem=("parallel",), plans=plans)


def _mix_bwd_a(dxo, m, ycat, ada9, w_out, cx, o, go, conv_w, gn, heads, name, plans=()):
    t, d = dxo.shape
    cw, vw = conv_w.shape[1], o.shape[1]
    dv = vw // heads
    tm = min(ROW_TILE, t)
    nt = t // tm

    def body(dxo_ref, m_ref, y_ref, ada_ref, w_ref, cx_ref, halo_ref, o_ref, go_ref, cwt_ref, gn_ref,
             dw_ref, dyc_ref, dcb_ref, do_ref, dgo_ref, dg_ref, dcw_ref, dgn_ref, acc_ref):
        @pl.when(pl.program_id(0) == 0)
        def _():
            dg_ref[...] = jnp.zeros_like(dg_ref)
            dcw_ref[...] = jnp.zeros_like(dcw_ref)
            dgn_ref[...] = jnp.zeros_like(dgn_ref)

        dxo_t = dxo_ref[...]
        dmb = (ada_ref[5:6, :] * dxo_t).astype(BF16)
        part = _dot_tn(y_ref[...], dmb)

        @pl.when(pl.program_id(0) == 0)
        def _():
            acc_ref[...] = part

        @pl.when(pl.program_id(0) > 0)
        def _():
            acc_ref[...] += part

        @pl.when(pl.program_id(0) == nt - 1)
        def _():
            dw_ref[...] = acc_ref[...].astype(BF16)

        dg_ref[...] += _rsum8(dxo_t * m_ref[...].astype(F32))
        dy = _dot_nt(dmb, w_ref[...])
        u, u1, u2 = _conv_taps(cx_ref, halo_ref, pl.program_id(0) == 0, cw)
        yc = cwt_ref[0:1, :] * u2 + cwt_ref[1:2, :] * u1 + cwt_ref[2:3, :] * u
        dyv = dy[:, 0:cw]
        dcb_ref[...] = (dyv * yc).astype(BF16)
        dyc = dyv * cx_ref[:, 0:cw].astype(F32)
        dyc_ref[...] = dyc
        dcw_ref[0] += _rsum8(dyc * u2)
        dcw_ref[1] += _rsum8(dyc * u1)
        dcw_ref[2] += _rsum8(dyc * u)
        for h in range(heads):
            cols = slice(h * dv, (h + 1) * dv)
            ohat, rstd, on = _head_norm(o_ref[:, cols], gn_ref[...])
            g = go_ref[:, cols].astype(F32)
            sg = _sigmoid_tanh(g)
            dyg = dy[:, cw + h * dv:cw + (h + 1) * dv]
            dgo_ref[:, cols] = (dyg * on * (sg * (1.0 + g * (1.0 - sg)))).astype(BF16)
            don = dyg * (g * sg)
            dgn_ref[...] += _rsum8(don * ohat)
            tt = don * gn_ref[...]
            do_ref[:, cols] = (rstd * (tt - ohat * jnp.mean(tt * ohat, axis=-1, keepdims=True))).astype(BF16)

    return _pcall(
        body, name, [dxo, m, ycat, ada9, w_out, cx, cx, o, go, conv_w, gn], grid=(nt,),
        in_specs=[pl.BlockSpec((tm, d), lambda i: (i, 0)), pl.BlockSpec((tm, d), lambda i: (i, 0)),
                  pl.BlockSpec((tm, cw + vw), lambda i: (i, 0)),
                  pl.BlockSpec((N_ADA, d), lambda i: (0, 0)), pl.BlockSpec((cw + vw, d), lambda i: (0, 0)),
                  pl.BlockSpec((tm, 3 * cw), lambda i: (i, 0)),
                  pl.BlockSpec((BF16_ROWS, 3 * cw), lambda i: (jnp.maximum(i * (tm // BF16_ROWS) - 1, 0), 0)),
                  pl.BlockSpec((tm, vw), lambda i: (i, 0)), pl.BlockSpec((tm, vw), lambda i: (i, 0)),
                  pl.BlockSpec((3, cw), lambda i: (0, 0)), pl.BlockSpec((1, dv), lambda i: (0, 0))],
        out_shape=[S((cw + vw, d), BF16), S((t, cw), F32), S((t, cw), BF16), S((t, vw), BF16), S((t, vw), BF16),
                   S((8, d), F32), S((3, 8, cw), F32), S((8, dv), F32)],
        out_specs=[pl.BlockSpec((cw + vw, d), lambda i: (0, 0)), pl.BlockSpec((tm, cw), lambda i: (i, 0)),
                   pl.BlockSpec((tm, cw), lambda i: (i, 0)), pl.BlockSpec((tm, vw), lambda i: (i, 0)),
                   pl.BlockSpec((tm, vw), lambda i: (i, 0)), pl.BlockSpec((8, d), lambda i: (0, 0)),
                   pl.BlockSpec((3, 8, cw), lambda i: (0, 0, 0)), pl.BlockSpec((8, dv), lambda i: (0, 0))],
        scratch_shapes=[pltpu.VMEM((cw + vw, d), F32)],
        sem=("arbitrary",), plans=plans)


def _mix_bwd_b(dyc, cx, dcb, dqk, dvv, dgo, dgl, conv_w, w, x, dxo, ada9, nrm, name, plans=()):
    t, d = x.shape
    cw = conv_w.shape[1]
    n = w.shape[0]
    tm = min(ROW_TILE, t)
    nt = t // tm
    pieces = [dcb.shape[1], cw, cw, dqk.shape[1], dvv.shape[1], dgo.shape[1], dgl.shape[1]]
    assert sum(pieces) == n

    def body(dyc_ref, nxt_ref, cx_ref, dcb_ref, dqk_ref, dv_ref, dgo_ref, dgl_ref, cwt_ref, w_ref, x_ref, dxo_ref, ada_ref, n_ref,
             dx_ref, dp_ref, dsh_ref, dsc_ref, dn_ref):
        i = pl.program_id(0)

        @pl.when(i == 0)
        def _():
            dsh_ref[...] = jnp.zeros_like(dsh_ref)
            dsc_ref[...] = jnp.zeros_like(dsc_ref)
            dn_ref[...] = jnp.zeros_like(dn_ref)

        dyc_t = dyc_ref[...]
        nxt = jnp.where(i == nt - 1, 0.0, nxt_ref[...])
        row = lax.broadcasted_iota(jnp.int32, (tm, cw), 0)
        d1 = jnp.where(row == tm - 1, nxt[0:1, :], pltpu.roll(dyc_t, tm - 1, 0))
        d2 = jnp.where(row == tm - 2, nxt[0:1, :], jnp.where(row == tm - 1, nxt[1:2, :], pltpu.roll(dyc_t, tm - 2, 0)))
        du = cwt_ref[2:3, :] * dyc_t + cwt_ref[1:2, :] * d1 + cwt_ref[0:1, :] * d2
        c0 = 0
        dp_ref[:, c0:c0 + cw] = dcb_ref[...]
        dp_ref[:, cw:2 * cw] = (du * cx_ref[:, 2 * cw:3 * cw].astype(F32)).astype(BF16)
        dp_ref[:, 2 * cw:3 * cw] = (du * cx_ref[:, cw:2 * cw].astype(F32)).astype(BF16)
        c0 = 3 * cw
        for ref in (dqk_ref, dv_ref, dgo_ref, dgl_ref):
            wd = ref.shape[1]
            dp_ref[:, c0:c0 + wd] = ref[...]
            c0 += wd
        dh = _dot(dp_ref[...], w_ref[...])
        dx, tsh, tsc, tn = _normmod_bwd(dh, x_ref[...], n_ref[...], ada_ref[4:5, :])
        dx_ref[...] = dxo_ref[...] + dx
        dsh_ref[...] += _rsum8(tsh)
        dsc_ref[...] += _rsum8(tsc)
        dn_ref[...] += _rsum8(tn)

    row_spec = lambda wd: pl.BlockSpec((tm, wd), lambda i: (i, 0))
    vec = pl.BlockSpec((8, d), lambda i: (0, 0))
    return _pcall(
        body, name, [dyc, dyc, cx, dcb, dqk, dvv, dgo, dgl, conv_w, w, x, dxo, ada9, nrm], grid=(nt,),
        in_specs=[row_spec(cw), pl.BlockSpec((8, cw), lambda i: (jnp.minimum((i + 1) * (tm // 8), t // 8 - 1), 0)),
                  row_spec(3 * cw), row_spec(cw), row_spec(dqk.shape[1]), row_spec(dvv.shape[1]), row_spec(dgo.shape[1]),
                  row_spec(dgl.shape[1]), pl.BlockSpec((3, cw), lambda i: (0, 0)), pl.BlockSpec((n, d), lambda i: (0, 0)),
                  row_spec(d), row_spec(d), pl.BlockSpec((N_ADA, d), lambda i: (0, 0)), pl.BlockSpec((1, d), lambda i: (0, 0))],
        out_shape=[S((t, d), F32), S((t, n), BF16), S((8, d), F32), S((8, d), F32), S((8, d), F32)],
        out_specs=[row_spec(d), row_spec(n), vec, vec, vec],
        sem=("arbitrary",), plans=plans)


def _ffn_out_loss(s, w_out, x, ada9, g_row, res_scale, target, nrm, name):
    nb, t, bw = s.shape
    d = x.shape[1]
    tm = min(ROW_TILE, t)
    nt = t // tm

    def body(s_ref, w_ref, x_ref, ada_ref, tg_ref, n_ref, f_ref, loss_ref, dx_ref, dn_ref, acc_ref):
        i = pl.program_id(0)

        @pl.when(i == 0)
        def _():
            acc_ref[...] = jnp.zeros_like(acc_ref)
            dn_ref[...] = jnp.zeros_like(dn_ref)

        f = _dot(s_ref[0], w_ref[0])
        for b in range(1, nb):
            f = f + _dot(s_ref[b], w_ref[b])
        f_ref[...] = f.astype(BF16)
        xt = x_ref[...] + (res_scale * ada_ref[g_row:g_row + 1, :]) * f
        rstd = lax.rsqrt(jnp.mean(xt * xt, axis=-1, keepdims=True) + EPS)
        xhat = xt * rstd
        err = xhat * n_ref[...] - tg_ref[...]
        acc_ref[...] += _rsum8(err * err)
        dy = err * (1.0 / d)
        dn_ref[...] += _rsum8(dy * xhat)
        dxhat = dy * n_ref[...]
        dx_ref[...] = rstd * (dxhat - xhat * jnp.mean(dxhat * xhat, axis=-1, keepdims=True))

        @pl.when(i == nt - 1)
        def _():
            loss_ref[...] = jnp.full(loss_ref.shape, (0.5 / d) * jnp.sum(acc_ref[...]), F32)

    return pl.pallas_call(
        body, name=name, grid=(nt,),
        in_specs=[pl.BlockSpec((nb, tm, bw), lambda i: (0, i, 0)), pl.BlockSpec((nb, bw, d), lambda i: (0, 0, 0)),
                  pl.BlockSpec((tm, d), lambda i: (i, 0)), pl.BlockSpec((N_ADA, d), lambda i: (0, 0)),
                  pl.BlockSpec((tm, d), lambda i: (i, 0)), pl.BlockSpec((1, d), lambda i: (0, 0))],
        out_shape=[S((t, d), BF16), S((1, LANES), F32), S((t, d), F32), S((8, d), F32)],
        out_specs=[pl.BlockSpec((tm, d), lambda i: (i, 0)), pl.BlockSpec((1, LANES), lambda i: (0, 0)),
                   pl.BlockSpec((tm, d), lambda i: (i, 0)), pl.BlockSpec((8, d), lambda i: (0, 0))],
        scratch_shapes=[pltpu.VMEM((8, d), F32)],
        compiler_params=_cp("arbitrary"),
    )(s, w_out, x, ada9, target, nrm)


def _pack_smalls(vec_parts, dcw, dbg, dgn, dwg, loss_v, rank, name):
    d = vec_parts[0].shape[1]
    cw, kw, dv = dcw.shape[2], dbg.shape[1], dgn.shape[1]
    nv = len(vec_parts)
    loss_row = nv + 2 + rank * kw // d
    assert 2 * cw == d and cw + kw + dv <= d and (rank * kw) % d == 0 and loss_row < PACK_ROWS
    per_row = d // kw

    def body(*refs):
        vrefs, (dcw_ref, dbg_ref, dgn_ref, dwg_ref, loss_ref, o_ref) = refs[:nv], refs[nv:]
        o_ref[...] = jnp.zeros_like(o_ref)
        o_ref[loss_row:loss_row + 1, 0:loss_ref.shape[1]] = loss_ref[...]
        for r, ref in enumerate(vrefs):
            o_ref[r:r + 1, :] = jnp.sum(ref[...], axis=0, keepdims=True)
        o_ref[nv:nv + 1, 0:cw] = jnp.sum(dcw_ref[0], axis=0, keepdims=True)
        o_ref[nv:nv + 1, cw:2 * cw] = jnp.sum(dcw_ref[1], axis=0, keepdims=True)
        o_ref[nv + 1:nv + 2, 0:cw] = jnp.sum(dcw_ref[2], axis=0, keepdims=True)
        o_ref[nv + 1:nv + 2, cw:cw + kw] = jnp.sum(dbg_ref[...], axis=0, keepdims=True)
        o_ref[nv + 1:nv + 2, cw + kw:cw + kw + dv] = jnp.sum(dgn_ref[...], axis=0, keepdims=True)
        for r in range(rank):
            o_ref[nv + 2 + r // per_row:nv + 3 + r // per_row, (r % per_row) * kw:(r % per_row + 1) * kw] = dwg_ref[r:r + 1, :]

    return pl.pallas_call(body, name=name, out_shape=S((PACK_ROWS, d), F32), compiler_params=_cp())(*vec_parts, dcw, dbg, dgn, dwg, loss_v)


def _sum_slots(a, name):
    def body(a_ref, o_ref):
        acc = a_ref[0]
        for s in range(1, NDEV):
            acc = acc + a_ref[s]
        o_ref[...] = acc

    return pl.pallas_call(body, name=name, out_shape=S(a.shape[1:], F32), compiler_params=_cp())(a)


def _adamw(w, g, m, v):
    m = ADAM_B1 * m + (1.0 - ADAM_B1) * g
    v = ADAM_B2 * v + (1.0 - ADAM_B2) * (g * g)
    m_hat = m / (1.0 - ADAM_B1 ** ADAM_STEP)
    v_hat = v / (1.0 - ADAM_B2 ** ADAM_STEP)
    return -ADAM_LR * (m_hat / (jnp.sqrt(v_hat) + ADAM_EPS) + ADAM_WD * w), m, v


def _adam_slots(recv, w, m, v, name):
    r, c = w.shape
    slots = recv.shape[0]
    tr = _row_tile(r, c)

    def body(recv_ref, w_ref, m_ref, v_ref, g_ref, d_ref, mo_ref, vo_ref):
        g = recv_ref[0].astype(F32)
        for s in range(1, slots):
            g = g + recv_ref[s].astype(F32)
        g_ref[...] = g
        d_ref[...], mo_ref[...], vo_ref[...] = _adamw(w_ref[...], g, m_ref[...], v_ref[...])

    blk = pl.BlockSpec((tr, c), lambda i: (i, 0))
    return pl.pallas_call(
        body, name=name, grid=(r // tr,),
        in_specs=[pl.BlockSpec((slots, tr, c), lambda i: (0, i, 0)), blk, blk, blk],
        out_shape=[S((r, c), F32)] * 4, out_specs=[blk] * 4, compiler_params=_cp("parallel"),
    )(recv, w, m, v)


def _adam_w_ada(act_t, dada, w, m, v, name):
    r, c = w.shape
    tr = 128
    nb = act_t.shape[1]

    def body(a_ref, da_ref, w_ref, m_ref, v_ref, g_ref, d_ref, mo_ref, vo_ref):
        g = a_ref[:, 0:1] * da_ref[0:1, :]
        for b in range(1, nb):
            g = g + a_ref[:, b:b + 1] * da_ref[b:b + 1, :]
        g_ref[...] = g
        d_ref[...], mo_ref[...], vo_ref[...] = _adamw(w_ref[...], g, m_ref[...], v_ref[...])

    blk = pl.BlockSpec((tr, c), lambda i: (i, 0))
    return pl.pallas_call(
        body, name=name, grid=(r // tr,),
        in_specs=[pl.BlockSpec((tr, nb), lambda i: (i, 0)), pl.BlockSpec((nb, c), lambda i: (0, 0)), blk, blk, blk],
        out_shape=[S((r, c), F32)] * 4, out_specs=[blk] * 4, compiler_params=_cp("parallel"),
    )(act_t, dada, w, m, v)


def _adam_smalls(ws, gs, ms, vs, name):
    n = len(ws)

    def body(*refs):
        w_r, g_r, m_r, v_r = (refs[k * n:(k + 1) * n] for k in range(4))
        d_o, m_o, v_o = (refs[(4 + k) * n:(5 + k) * n] for k in range(3))
        for i in range(n):
            d_o[i][...], m_o[i][...], v_o[i][...] = _adamw(w_r[i][...], g_r[i][...], m_r[i][...], v_r[i][...])

    shapes = [S(w.shape, F32) for w in ws]
    outs = pl.pallas_call(body, name=name, out_shape=shapes * 3, compiler_params=_cp())(*ws, *gs, *ms, *vs)
    return outs[:n], outs[n:2 * n], outs[2 * n:]


def kernel(x, c, w_ada, b_ada, norm_ffn1, w_ffn1_in, w_ffn1_out, norm_mix, w_mix_in, conv_w, w_gk2, b_gk, gla_norm, w_mix_out, norm_ffn2, w_ffn2_in, w_ffn2_out, norm_final, loss_target, m_w_ada, m_b_ada, m_norm_ffn1, m_w_ffn1_in, m_w_ffn1_out, m_norm_mix, m_w_mix_in, m_conv_w, m_w_gk2, m_b_gk, m_gla_norm, m_w_mix_out, m_norm_ffn2, m_w_ffn2_in, m_w_ffn2_out, m_norm_final, v_w_ada, v_b_ada, v_norm_ffn1, v_w_ffn1_in, v_w_ffn1_out, v_norm_mix, v_w_mix_in, v_conv_w, v_w_gk2, v_b_gk, v_gla_norm, v_w_mix_out, v_norm_ffn2, v_w_ffn2_in, v_w_ffn2_out, v_norm_final):
    t, d = x.shape[1], x.shape[2]
    x0, tgt = x[0], loss_target[0]
    rank, kw = w_gk2.shape[1], w_gk2.shape[2] * NDEV
    cw = conv_w.shape[2] * NDEV
    dv = gla_norm.shape[1]
    vw = d - cw
    heads = vw // dv
    mix_cols = w_mix_in.shape[2]
    widths = [3 * cw, 2 * kw, vw, vw, LANES]
    n_proj = 3 * cw + 2 * kw + 2 * vw + rank
    assert n_proj == mix_cols * NDEV and rank <= LANES
    me = 4 * lax.axis_index("x") + 2 * lax.axis_index("y") + lax.axis_index("c")

    core = lax.axis_index("c").astype(jnp.int32).reshape(1)
    bf = lambda a: a[0].astype(BF16)
    bft = lambda a: a[0].T.astype(BF16)
    nb = NDEV // 2

    (ada_row, act_all), ((w1i, cwt_all, wg_all),) = _ada_rows(
        c, w_ada[0], b_ada, "ada_rows", plans=[_plan_gather([bft(w_ffn1_in), conv_w[0], w_gk2[0]])])
    ada9 = ada_row.reshape(N_ADA, d)
    cwt = cwt_all.transpose(1, 0, 2).reshape(conv_w.shape[1], cw)
    wg = jnp.pad(wg_all.transpose(1, 0, 2).reshape(rank, kw), ((0, LANES - rank), (0, 0))).astype(BF16)

    (h1, gu1, s1), ((w1o, wmi),) = _ffn_in(x0, ada9, norm_ffn1, w1i, 0, 1, "ffn1_in", plans=[_plan_gather([bf(w_ffn1_out), bft(w_mix_in)])])
    w1o = w1o.reshape(nb, -1, d)
    wmi = jnp.pad(wmi.reshape(n_proj, d), ((0, sum(widths) - n_proj), (0, 0)))
    w2i_mine = bft(w_ffn2_in)
    quarter = w2i_mine.shape[0] // 4
    part = lambda k, into=None: _plan_gather([w2i_mine], rows=(k * quarter, quarter), into=into)
    (x1, f1), ((wmo,), (w2i,)) = _ffn_out(s1, w1o, x0, ada9, 2, 0.5, "ffn1_out", plans=[_plan_gather([bf(w_mix_out)]), part(0)])
    wmo = wmo.reshape(cw + vw, d)
    (h2, cx, qk, vv, go, gl), ((w2i,),) = _mix_in(x1, ada9, norm_mix, wmi, widths, [BF16, F32, BF16, BF16, F32], "mix_in",
                                                 plans=[part(1, [w2i])])
    (o, lg, sall), ((w2i,),) = _gla_fwd(qk, vv, gl, wg, b_gk, heads, "gla_fwd", plans=[part(2, [w2i])])
    (x2, mm, ycat), ((w2i,),) = _mix_out(cx, o, go, cwt, gla_norm, wmo, x1, ada9, heads, "mix_out", plans=[part(3, [w2i])])
    (h3, gu3, s3), ((w2o,),) = _ffn_in(x2, ada9, norm_ffn2, w2i, 6, 7, "ffn2_in", plans=[_plan_gather([bf(w_ffn2_out)])])
    w2o = w2o.reshape(nb, -1, d)
    f3, loss_v, dx3, dnf = _ffn_out_loss(s3, w2o, x2, ada9, 8, 0.5, tgt, norm_final.reshape(1, d), "ffn2_out_loss")

    dx2, sum2i, sum2o, (dsh3, dsc3, dg3, dn3), _, _ = _ffn_backward(
        dx3, x2, h3, gu3, f3, ada9, norm_ffn2, w2i, w2o, (6, 7, 8), core, "ffn2_bwd", later=True)
    (dwmo, dyc, dcb, do, dgo, dg2, dcw, dgn), ((r2o,),) = _mix_bwd_a(dx2, mm, ycat, ada9, wmo, cx, o, go, cwt, gla_norm, heads, "mix_bwd_a",
                                                                   plans=[_plan_chip_swap([sum2o])])
    half = sum2i.shape[1] // 2
    (dqk, dvv, dgl, dwg, dbg), ((r2i,),) = _gla_bwd(qk, vv, lg, do, sall, gl, wg, heads, "gla_bwd",
                                                   plans=[_plan_chip_swap([sum2i], rows=(0, half))])
    (dx1, dproj, dsh2, dsc2, dnm), ((r2i,),) = _mix_bwd_b(dyc, cx, dcb, dqk, dvv, dgo, dgl, cwt, wmi, x1, dx2, ada9, norm_mix, "mix_bwd_b",
                                                         plans=[_plan_chip_swap([sum2i], rows=(half, half), into=[r2i])])
    n_pad = sum(widths)
    tn = n_pad // 5
    dwmi, _ = _tn_matmul(dproj, h2, lambda tk: (tk, tn), lambda sb, k: (k, sb), lambda tk: (tk, d), lambda sb, k: (k, 0),
                         (n_pad, d), (tn, d), lambda sb, k: (sb, 0), 5, "mix_dwin")
    dwmi = dwmi[:n_proj].reshape(NDEV, mix_cols, d)
    dwmo = dwmo.reshape(NDEV, -1, d)
    dx0, r1i, (sum1o, r1o, head), (dsh1, dsc1, dg1, dn1), _, ((rmi, rmo),) = _ffn_backward(
        dx1, x0, h1, gu1, f1, ada9, norm_ffn1, w1i, w1o, (0, 1, 2), core, "ffn1_bwd",
        ds_plans=[_plan_sibling_swap([dwmi, dwmo])],
        dwin_plans=lambda moved: [_plan_chip_swap([_pair_add(dwmi, moved[0][0], core, "mix_dwin_add"),
                                                   _pair_add(dwmo, moved[0][1], core, "mix_dwout_add")])])
    pack = _pack_smalls([dn1, dnm, dn3, dnf, dsh1, dsc1, dg1, dsh2, dsc2, dg2, dsh3, dsc3, dg3], dcw, dbg, dgn, dwg, loss_v, rank, "pack_smalls")
    (r1o,), (pack_all,) = _exchange([_plan_chip_swap([sum1o], rows=(head, sum1o.shape[1] - head), into=[r1o]),
                                     _plan_all_to_all([pack], True)], "grads_last")
    tot = _sum_slots(pack_all, "sum_smalls")

    res = {}
    for nm, recv, w, m, v in (("w_ffn1_out", r1o, w_ffn1_out, m_w_ffn1_out, v_w_ffn1_out), ("w_mix_out", rmo, w_mix_out, m_w_mix_out, v_w_mix_out),
                              ("w_ffn2_out", r2o, w_ffn2_out, m_w_ffn2_out, v_w_ffn2_out)):
        res[nm] = [a[None] for a in _adam_slots(recv, w[0], m[0], v[0], "adam_" + nm)]
    for nm, recv, w, m, v in (("w_ffn1_in", r1i, w_ffn1_in, m_w_ffn1_in, v_w_ffn1_in), ("w_mix_in", rmi, w_mix_in, m_w_mix_in, v_w_mix_in),
                              ("w_ffn2_in", r2i, w_ffn2_in, m_w_ffn2_in, v_w_ffn2_in)):
        res[nm] = [a.T[None] for a in _adam_slots(recv, w[0].T, m[0].T, v[0].T, "adam_" + nm)]

    cols_ada = w_ada.shape[2]
    dada_all = pack_all[:, 4:4 + N_ADA, :].reshape(NDEV, N_ADA * d)
    dada_mine = lax.dynamic_slice_in_dim(dada_all, me * cols_ada, cols_ada, axis=1)
    res["w_ada"] = [a[None] for a in _adam_w_ada(act_all.T, dada_mine, w_ada[0], m_w_ada[0], v_w_ada[0], "adam_w_ada")]

    nv = 4 + N_ADA
    g_small = {
        "b_ada": tot[4:nv].reshape(1, N_ADA * d),
        "norm_ffn1": tot[0:1], "norm_mix": tot[1:2], "norm_ffn2": tot[2:3], "norm_final": tot[3:4],
        "conv_w": lax.dynamic_slice_in_dim(
            jnp.concatenate([tot[nv:nv + 1, 0:cw], tot[nv:nv + 1, cw:2 * cw], tot[nv + 1:nv + 2, 0:cw]], axis=0), me * (cw // NDEV), cw // NDEV, axis=1),
        "w_gk2": lax.dynamic_slice_in_dim(tot[nv + 2:nv + 2 + rank * kw // d].reshape(rank, kw), me * (kw // NDEV), kw // NDEV, axis=1),
        "b_gk": tot[nv + 1:nv + 2, cw:cw + kw],
        "gla_norm": tot[nv + 1:nv + 2, cw + kw:cw + kw + dv],
    }
    small = {"b_ada": (b_ada, m_b_ada, v_b_ada), "norm_ffn1": (norm_ffn1, m_norm_ffn1, v_norm_ffn1), "norm_mix": (norm_mix, m_norm_mix, v_norm_mix),
             "norm_ffn2": (norm_ffn2, m_norm_ffn2, v_norm_ffn2), "norm_final": (norm_final, m_norm_final, v_norm_final),
             "conv_w": (conv_w, m_conv_w, v_conv_w), "w_gk2": (w_gk2, m_w_gk2, v_w_gk2), "b_gk": (b_gk, m_b_gk, v_b_gk),
             "gla_norm": (gla_norm, m_gla_norm, v_gla_norm)}
    names = list(small)
    flat = lambda a: a.reshape(-1, a.shape[-1])
    dl, mo, vo = _adam_smalls([flat(small[n][0]) for n in names], [g_small[n] for n in names],
                              [flat(small[n][1]) for n in names], [flat(small[n][2]) for n in names], "adam_smalls")
    for i, n in enumerate(names):
        shp = small[n][0].shape
        res[n] = [g_small[n].reshape(shp), dl[i].reshape(shp), mo[i].reshape(shp), vo[i].reshape(shp)]

    loss = tot[nv + 2 + rank * kw // d, 0]
    order = ["w_ada", "b_ada", "norm_ffn1", "w_ffn1_in", "w_ffn1_out", "norm_mix", "w_mix_in", "conv_w", "w_gk2", "b_gk", "gla_norm",
             "w_mix_out", "norm_ffn2", "w_ffn2_in", "w_ffn2_out", "norm_final"]
    return (loss, dx0[None], *[res[n][0] for n in order], *[res[n][1] for n in order], *[res[n][2] for n in order], *[res[n][3] for n in order])
```

```python
import collections
import functools

import jax
import jax.numpy as jnp
from jax import lax
from jax.experimental import pallas as pl
from jax.experimental.pallas import tpu as pltpu

F32 = jnp.float32
BF16 = jnp.bfloat16
S = jax.ShapeDtypeStruct

NDEV = 8
EPS = 1e-6
GATE_NORMALIZER = 16.0
CHUNK = 128
N_ADA = 9
ADAM_LR, ADAM_B1, ADAM_B2, ADAM_EPS, ADAM_WD, ADAM_STEP = 0.001, 0.9, 0.999, 1e-08, 0.01, 10
V7X_VMEM_LIMIT = 56 * 1024 * 1024
ROW_TILE = 512
WIDE_ROW_TILE = 1024
K_TILE = 1024
EPILOGUE_ROWS = 256
LANES = 128
BF16_ROWS = 16
PACK_ROWS = 24
ANY = pl.BlockSpec(memory_space=pl.ANY)


def _cp(*sem):
    return pltpu.CompilerParams(dimension_semantics=sem or None, vmem_limit_bytes=V7X_VMEM_LIMIT)


def _dot(a, b):
    return jnp.dot(a, b, preferred_element_type=F32)


def _dot_nt(a, b):
    return lax.dot_general(a, b, (((1,), (1,)), ((), ())), preferred_element_type=F32)


def _dot_tn(a, b):
    return lax.dot_general(a, b, (((0,), (0,)), ((), ())), preferred_element_type=F32)


def _rsum8(a):
    r, c = a.shape
    return jnp.sum(a.reshape(r // 8, 8, c), axis=0)


def _row_tile(r, c):
    for cand in (256, 128, 176, 88, 64, 32, 16, 8):
        if r % cand == 0 and cand * c * 4 <= 1024 * 1024:
            return cand
    return r


def _sigmoid(x):
    return 1.0 / (1.0 + jnp.exp(-x))


def _sigmoid_tanh(x):
    return 0.5 * jnp.tanh(0.5 * x) + 0.5


def _normmod(x, nrm, sc, sh):
    rstd = lax.rsqrt(jnp.mean(x * x, axis=-1, keepdims=True) + EPS)
    xhat = x * rstd
    return xhat, rstd, (xhat * nrm) * (1.0 + sc) + sh


def _normmod_bwd(dh, x, nrm, sc):
    rstd = lax.rsqrt(jnp.mean(x * x, axis=-1, keepdims=True) + EPS)
    xhat = x * rstd
    dxhat = dh * (nrm * (1.0 + sc))
    dx = rstd * (dxhat - xhat * jnp.mean(dxhat * xhat, axis=-1, keepdims=True))
    return dx, dh, dh * (xhat * nrm), dh * ((1.0 + sc) * xhat)


def _place():
    x, y, c = lax.axis_index("x"), lax.axis_index("y"), lax.axis_index("c")
    return x, y, c, 4 * x + 2 * y + c


def _peer(x, y, c, k):
    px = 1 - x if k & 4 else x
    py = 1 - y if k & 2 else y
    pc = 1 - c if k & 1 else c
    return (px, py, pc), 4 * px + 2 * py + pc


def _remote(src, dst, send_sem, recv_sem, peer):
    return pltpu.make_async_remote_copy(src_ref=src, dst_ref=dst, send_sem=send_sem, recv_sem=recv_sem,
                                        device_id=peer, device_id_type=pl.DeviceIdType.MESH)


_Plan = collections.namedtuple("_Plan", "inputs out_shapes sem_shapes start finish aliases", defaults=({},))


def _plan_all_to_all(xs, gather):
    n = len(xs)

    def copies(ins, outs, sems, landed):
        send_sems, recv_sems, local_sems = sems
        x, y, c, me = _place()
        local = [pltpu.make_async_copy(ins[i] if gather else ins[i].at[me], outs[i].at[me], local_sems.at[i]) for i in range(n)]
        remote = []
        for k in range(1, NDEV):
            peer, pid = _peer(x, y, c, k)
            for i in range(n):
                remote.append(_remote(ins[i] if gather else ins[i].at[pid], outs[i].at[pid if landed else me],
                                      send_sems.at[i, k - 1], recv_sems.at[i, k - 1], peer))
        return local, remote

    def start(ins, outs, sems):
        local, remote = copies(ins, outs, sems, False)
        for cp in local + remote:
            cp.start()

    def finish(ins, outs, sems):
        local, remote = copies(ins, outs, sems, True)
        for cp in remote + local:
            cp.wait()

    return _Plan(list(xs), [S((NDEV,) + a.shape, a.dtype) if gather else S(a.shape, a.dtype) for a in xs],
                 [pltpu.SemaphoreType.DMA((n, NDEV - 1)), pltpu.SemaphoreType.DMA((n, NDEV - 1)), pltpu.SemaphoreType.DMA((n,))],
                 start, finish)


def _other_chips(x, y):
    return [(1 - x, y), (x, 1 - y), (1 - x, 1 - y)]


def _plan_gather(xs, rows=None, into=None):
    n = len(xs)

    def copies(ins, outs, sems, rest):
        send_sems, recv_sems, local_sems = sems
        x, y, c, me = _place()
        sib, sib_id = (x, y, 1 - c), 4 * x + 2 * y + 1 - c
        chips = _other_chips(x, y)
        mine = lambda i: ins[i] if rows is None else ins[i].at[pl.ds(*rows)]
        slot_of = lambda i, s: outs[i].at[s] if rows is None else outs[i].at[s, pl.ds(*rows)]
        local = [pltpu.make_async_copy(mine(i), slot_of(i, me), local_sems.at[i]) for i in range(n)]
        first = [_remote(mine(i), slot_of(i, me), send_sems.at[i, 0], recv_sems.at[i, 0], sib) for i in range(n)]
        first += [_remote(mine(i), slot_of(i, me), send_sems.at[i, 1 + j], recv_sems.at[i, 1 + j], (px, py, c))
                  for j, (px, py) in enumerate(chips) for i in range(n)]
        if not rest:
            return local, first
        from_sibling = [_remote(mine(i), slot_of(i, sib_id), send_sems.at[i, 0], recv_sems.at[i, 0], sib) for i in range(n)]
        arrive, forward = [], []
        for j, (px, py) in enumerate(chips):
            s = 4 * px + 2 * py
            arrive.append([_remote(mine(i), slot_of(i, s + c), send_sems.at[i, 1 + j], recv_sems.at[i, 1 + j], (px, py, c)) for i in range(n)])
            forward.append([_remote(slot_of(i, s + c), slot_of(i, s + c), send_sems.at[i, 4 + j], recv_sems.at[i, 4 + j], sib) for i in range(n)])
            from_sibling += [_remote(mine(i), slot_of(i, s + 1 - c), send_sems.at[i, 4 + j], recv_sems.at[i, 4 + j], sib) for i in range(n)]
        return local, first, arrive, forward, from_sibling

    def start(ins, outs, sems):
        local, first = copies(ins, outs, sems, False)
        for cp in local + first:
            cp.start()

    def finish(ins, outs, sems):
        local, first, arrive, forward, from_sibling = copies(ins, outs, sems, True)
        for landed, onward in zip(arrive, forward):
            for cp in landed:
                cp.wait_recv()
            for cp in onward:
                cp.start()
        for cp in from_sibling:
            cp.wait_recv()
        for cp in first + [cp for onward in forward for cp in onward]:
            cp.wait_send()
        for cp in local:
            cp.wait()

    return _Plan(list(xs) + list(into or []), [S((NDEV,) + a.shape, a.dtype) for a in xs],
                 [pltpu.SemaphoreType.DMA((n, NDEV - 1)), pltpu.SemaphoreType.DMA((n, NDEV - 1)), pltpu.SemaphoreType.DMA((n,))],
                 start, finish, {n + i: i for i in range(len(into or []))})


def _plan_sibling_swap(gs):
    n = len(gs)

    def copies(ins, outs, sems):
        send_sems, recv_sems = sems
        x, y, c, _ = _place()
        return [_remote(ins[i].at[2 * j + 1 - c], outs[i].at[j], send_sems.at[i, j], recv_sems.at[i, j], (x, y, 1 - c))
                for i in range(n) for j in range(NDEV // 2)]

    def start(ins, outs, sems):
        for cp in copies(ins, outs, sems):
            cp.start()

    def finish(ins, outs, sems):
        for cp in copies(ins, outs, sems):
            cp.wait()

    return _Plan(list(gs), [S((NDEV // 2,) + a.shape[1:], a.dtype) for a in gs],
                 [pltpu.SemaphoreType.DMA((n, NDEV // 2)), pltpu.SemaphoreType.DMA((n, NDEV // 2))], start, finish)


def _pair_add(g, r1, core, name):
    _, r, c = g.shape
    tr = r if r * c * 2 <= 2 * 1024 * 1024 else _row_tile(r, c)

    def body(core_ref, g_ref, r_ref, o_ref):
        o_ref[...] = (g_ref[...].astype(F32) + r_ref[...].astype(F32)).astype(BF16)

    return pl.pallas_call(
        body, name=name,
        grid_spec=pltpu.PrefetchScalarGridSpec(
            num_scalar_prefetch=1, grid=(NDEV // 2, r // tr),
            in_specs=[pl.BlockSpec((None, tr, c), lambda j, k, core_ref: (2 * j + core_ref[0], k, 0)),
                      pl.BlockSpec((None, tr, c), lambda j, k, core_ref: (j, k, 0))],
            out_specs=pl.BlockSpec((None, tr, c), lambda j, k, core_ref: (j, k, 0))),
        out_shape=S((NDEV // 2, r, c), BF16), compiler_params=_cp("parallel", "parallel"),
    )(core, g, r1)


def _plan_chip_swap(ps, rows=None, into=None):
    n = len(ps)

    def copies(ins, outs, sems, landed):
        send_sems, recv_sems, local_sems = sems
        x, y, c, _ = _place()
        mine = 2 * x + y
        part = lambda ref, s: ref.at[s] if rows is None else ref.at[s, pl.ds(*rows)]
        local = [pltpu.make_async_copy(part(ins[i], mine), part(outs[i], mine), local_sems.at[i]) for i in range(n)]
        remote = [_remote(part(ins[i], 2 * px + py), part(outs[i], 2 * px + py if landed else mine),
                          send_sems.at[i, j], recv_sems.at[i, j], (px, py, c))
                  for j, (px, py) in enumerate(_other_chips(x, y)) for i in range(n)]
        return local, remote

    def start(ins, outs, sems):
        local, remote = copies(ins, outs, sems, False)
        for cp in local + remote:
            cp.start()

    def finish(ins, outs, sems):
        local, remote = copies(ins, outs, sems, True)
        for cp in remote + local:
            cp.wait()

    return _Plan(list(ps) + list(into or []), [S(a.shape, a.dtype) for a in ps],
                 [pltpu.SemaphoreType.DMA((n, 3)), pltpu.SemaphoreType.DMA((n, 3)), pltpu.SemaphoreType.DMA((n,))], start, finish,
                 {n + i: i for i in range(len(into or []))})


def _pcall(body, name, args, in_specs, out_shape, out_specs, grid=(), scratch_shapes=(), sem=(), plans=(), starts_plans=False):
    n_in, n_out, n_scr = len(args), len(out_shape), len(scratch_shapes)
    counts = [(len(p.inputs), len(p.out_shapes), len(p.sem_shapes)) for p in plans]
    c_args = [a for p in plans for a in p.inputs]
    c_outs = [s for p in plans for s in p.out_shapes]
    c_sems = [s for p in plans for s in p.sem_shapes]

    def wrapped(*refs):
        cuts = [n_in, len(c_args), n_out, len(c_outs), n_scr, len(c_sems)]
        ins, c_in, outs, c_out, scr, c_sem = [refs[sum(cuts[:k]):sum(cuts[:k + 1])] for k in range(6)]

        def halves(which):
            a = b = s = 0
            for p, (na, nb, ns) in zip(plans, counts):
                getattr(p, which)(c_in[a:a + na], c_out[b:b + nb], c_sem[s:s + ns])
                a, b, s = a + na, b + nb, s + ns

        if not plans:
            body(*ins, *outs, *scr)
        elif not grid and starts_plans:
            body(lambda: halves("start"), *ins, *outs, *scr)
            halves("finish")
        elif not grid:
            halves("start")
            body(*ins, *outs, *scr)
            halves("finish")
        else:
            first = functools.reduce(jnp.logical_and, [pl.program_id(a) == 0 for a in range(len(grid))])
            last = functools.reduce(jnp.logical_and, [pl.program_id(a) == grid[a] - 1 for a in range(len(grid))])
            pl.when(first)(lambda: halves("start"))
            body(*ins, *outs, *scr)
            pl.when(last)(lambda: halves("finish"))

    aliases, a, b = {}, n_in, n_out
    for p, (na, nb, _) in zip(plans, counts):
        aliases.update({a + k: b + v for k, v in p.aliases.items()})
        a, b = a + na, b + nb
    res = pl.pallas_call(
        wrapped, name=name, grid=grid, in_specs=list(in_specs) + [ANY] * len(c_args),
        out_shape=list(out_shape) + c_outs, out_specs=list(out_specs) + [ANY] * len(c_outs),
        scratch_shapes=list(scratch_shapes) + c_sems, input_output_aliases=aliases,
        compiler_params=_cp(*(("arbitrary",) * len(grid) if plans else sem)),
    )(*args, *c_args)
    c_res, b = [], n_out
    for _, nb, _ in counts:
        c_res.append(res[b:b + nb])
        b += nb
    return res[:n_out], c_res


def _exchange(plans, name):
    return _pcall(lambda: None, name, [], [], [], [], plans=plans)[1]


def _ada_rows(c, w_ada, b_ada, name, plans=()):
    d, cols = c.shape[1], w_ada.shape[1]
    gather_c = _plan_all_to_all([c], True)
    gather_p = _plan_all_to_all([S((NDEV, cols), F32)], True)
    n_sem = len(gather_c.sem_shapes)

    def body(start_plans, c_ref, w_ref, b_ref, ada_ref, act_ref, c_all, p_mine, p_all, *sems):
        gather_c.start([c_ref], [c_all], sems[:n_sem])
        gather_c.finish([c_ref], [c_all], sems[:n_sem])
        for s in range(NDEV):
            cc = c_all[s]
            act_ref[s:s + 1, :] = cc * _sigmoid(cc)
        p_mine[...] = _dot(act_ref[...].astype(BF16), w_ref[...].astype(BF16))
        gather_p.start([p_mine], [p_all], sems[n_sem:])
        start_plans()
        gather_p.finish([p_mine], [p_all], sems[n_sem:])
        me = _place()[3]
        for s in range(NDEV):
            ada_ref[:, s * cols:(s + 1) * cols] = p_all[s, pl.ds(me, 1), :] + b_ref[:, s * cols:(s + 1) * cols]

    whole = pl.BlockSpec(memory_space=pltpu.VMEM)
    return _pcall(body, name, [c, w_ada, b_ada], [whole] * 3, [S((1, NDEV * cols), F32), S((NDEV, d), F32)], [whole] * 2,
                  scratch_shapes=[pltpu.VMEM((NDEV,) + c.shape, F32), pltpu.VMEM((NDEV, cols), F32), pltpu.VMEM((NDEV, NDEV, cols), F32)]
                  + gather_c.sem_shapes + gather_p.sem_shapes, plans=plans, starts_plans=True)


def _ffn_in(x, ada9, nrm, w_in, sh_row, sc_row, name, plans=()):
    t, d = x.shape
    nb, bw = w_in.shape[0] // 2, w_in.shape[1]
    tm = min(WIDE_ROW_TILE, t)

    def body(x_ref, ada_ref, n_ref, wg_ref, wu_ref, h_ref, gu_ref, s_ref):
        @pl.when(pl.program_id(1) == 0)
        def _():
            _, _, h = _normmod(x_ref[...], n_ref[...], ada_ref[sc_row:sc_row + 1, :], ada_ref[sh_row:sh_row + 1, :])
            h_ref[...] = h.astype(BF16)

        h = h_ref[...]
        g = _dot_nt(h, wg_ref[...])
        u = _dot_nt(h, wu_ref[...])
        gu_ref[0] = g.astype(BF16)
        gu_ref[1] = u.astype(BF16)
        s_ref[...] = (g * _sigmoid(g) * u).astype(BF16)

    return _pcall(
        body, name, [x, ada9, nrm, w_in, w_in], grid=(t // tm, nb),
        in_specs=[pl.BlockSpec((tm, d), lambda i, j: (i, 0)), pl.BlockSpec((N_ADA, d), lambda i, j: (0, 0)),
                  pl.BlockSpec((1, d), lambda i, j: (0, 0)),
                  pl.BlockSpec((None, bw, d), lambda i, j: (j, 0, 0)), pl.BlockSpec((None, bw, d), lambda i, j: (j + nb, 0, 0))],
        out_shape=[S((t, d), BF16), S((2, nb, t, bw), BF16), S((nb, t, bw), BF16)],
        out_specs=[pl.BlockSpec((tm, d), lambda i, j: (i, 0)), pl.BlockSpec((2, None, tm, bw), lambda i, j: (0, j, i, 0)),
                   pl.BlockSpec((None, tm, bw), lambda i, j: (j, i, 0))],
        sem=("parallel", "arbitrary"), plans=plans)


def _ffn_out(s, w_out, x, ada9, g_row, res_scale, name, plans=()):
    nb, t, bw = s.shape
    d = x.shape[1]
    tm = min(ROW_TILE, t)

    def body(s_ref, w_ref, x_ref, ada_ref, xo_ref, f_ref):
        acc = _dot(s_ref[0], w_ref[0])
        for b in range(1, nb):
            acc = acc + _dot(s_ref[b], w_ref[b])
        f_ref[...] = acc.astype(BF16)
        xo_ref[...] = x_ref[...] + (res_scale * ada_ref[g_row:g_row + 1, :]) * acc

    return _pcall(
        body, name, [s, w_out, x, ada9], grid=(t // tm,),
        in_specs=[pl.BlockSpec((nb, tm, bw), lambda i: (0, i, 0)), pl.BlockSpec((nb, bw, d), lambda i: (0, 0, 0)),
                  pl.BlockSpec((tm, d), lambda i: (i, 0)), pl.BlockSpec((N_ADA, d), lambda i: (0, 0))],
        out_shape=[S((t, d), F32), S((t, d), BF16)],
        out_specs=[pl.BlockSpec((tm, d), lambda i: (i, 0)), pl.BlockSpec((tm, d), lambda i: (i, 0))],
        sem=("parallel",), plans=plans)


def _ffn_bwd_ds(dxo, f, ada9, w_out, gu, g_row, res_scale, name, plans=()):
    t, d = dxo.shape
    nb, bw = w_out.shape[0], w_out.shape[1]
    tm = min(WIDE_ROW_TILE, t)
    ni = t // tm

    def body(dxo_ref, f_ref, ada_ref, w_ref, gu_ref, da_ref, dg_ref, dw_ref, df_ref, acc_ref):
        i, j = pl.program_id(0), pl.program_id(1)

        @pl.when((i == 0) & (j == 0))
        def _():
            dg_ref[...] = jnp.zeros_like(dg_ref)

        @pl.when(j == 0)
        def _():
            dxo_t = dxo_ref[...]
            df_ref[...] = ((res_scale * ada_ref[g_row:g_row + 1, :]) * dxo_t).astype(BF16)
            dg_ref[...] += res_scale * _rsum8(dxo_t * f_ref[...].astype(F32))

        df = df_ref[...]
        ds = _dot_nt(df, w_ref[...])
        g = gu_ref[0].astype(F32)
        u = gu_ref[1].astype(F32)
        sg = _sigmoid_tanh(g)
        silu = g * sg
        da_ref[0] = (ds * u * (sg * (1.0 + g * (1.0 - sg)))).astype(BF16)
        da_ref[1] = (ds * silu).astype(BF16)
        part = _dot_tn((silu * u).astype(BF16), df)

        @pl.when(i == 0)
        def _():
            acc_ref[j] = part

        @pl.when(i > 0)
        def _():
            acc_ref[j] += part

        @pl.when(i == ni - 1)
        def _():
            dw_ref[...] = acc_ref[j].astype(BF16)

    return _pcall(
        body, name, [dxo, f, ada9, w_out, gu], grid=(ni, nb),
        in_specs=[pl.BlockSpec((tm, d), lambda i, j: (i, 0)), pl.BlockSpec((tm, d), lambda i, j: (i, 0)),
                  pl.BlockSpec((N_ADA, d), lambda i, j: (0, 0)), pl.BlockSpec((None, bw, d), lambda i, j: (j, 0, 0)),
                  pl.BlockSpec((2, None, tm, bw), lambda i, j: (0, j, i, 0))],
        out_shape=[S((2, nb, t, bw), BF16), S((8, d), F32), S((nb, bw, d), BF16)],
        out_specs=[pl.BlockSpec((2, None, tm, bw), lambda i, j: (0, j, i, 0)), pl.BlockSpec((8, d), lambda i, j: (0, 0)),
                   pl.BlockSpec((None, bw, d), lambda i, j: (jnp.where(i == ni - 1, j, 0), 0, 0))],
        scratch_shapes=[pltpu.VMEM((tm, d), BF16), pltpu.VMEM((nb, bw, d), F32)],
        sem=("arbitrary", "arbitrary"), plans=plans)


def _ffn_bwd_dh(da, w_in, x, dxo, ada9, nrm, sh_row, sc_row, name, plans=()):
    t, d = x.shape
    nb, bw = w_in.shape[0] // 2, w_in.shape[1]
    tm = min(WIDE_ROW_TILE, t)

    def body(da_ref, wg_ref, wu_ref, x_ref, dxo_ref, ada_ref, n_ref, dx_ref, dsh_ref, dsc_ref, dn_ref, acc_ref):
        i, j = pl.program_id(0), pl.program_id(1)

        @pl.when((i == 0) & (j == 0))
        def _():
            dsh_ref[...] = jnp.zeros_like(dsh_ref)
            dsc_ref[...] = jnp.zeros_like(dsc_ref)
            dn_ref[...] = jnp.zeros_like(dn_ref)

        part = _dot(da_ref[0], wg_ref[...]) + _dot(da_ref[1], wu_ref[...])

        @pl.when(j == 0)
        def _():
            acc_ref[...] = part

        @pl.when(j > 0)
        def _():
            acc_ref[...] += part

        @pl.when(j == nb - 1)
        def _():
            for r0 in range(0, tm, min(EPILOGUE_ROWS, tm)):
                rows = slice(r0, r0 + min(EPILOGUE_ROWS, tm))
                dx, tsh, tsc, tn = _normmod_bwd(acc_ref[rows, :], x_ref[rows, :], n_ref[...], ada_ref[sc_row:sc_row + 1, :])
                dx_ref[rows, :] = dxo_ref[rows, :] + dx
                dsh_ref[...] += _rsum8(tsh)
                dsc_ref[...] += _rsum8(tsc)
                dn_ref[...] += _rsum8(tn)

    vec = pl.BlockSpec((8, d), lambda i, j: (0, 0))
    return _pcall(
        body, name, [da, w_in, w_in, x, dxo, ada9, nrm], grid=(t // tm, nb),
        in_specs=[pl.BlockSpec((2, None, tm, bw), lambda i, j: (0, j, i, 0)),
                  pl.BlockSpec((None, bw, d), lambda i, j: (j, 0, 0)), pl.BlockSpec((None, bw, d), lambda i, j: (j + nb, 0, 0)),
                  pl.BlockSpec((tm, d), lambda i, j: (i, 0)), pl.BlockSpec((tm, d), lambda i, j: (i, 0)),
                  pl.BlockSpec((N_ADA, d), lambda i, j: (0, 0)), pl.BlockSpec((1, d), lambda i, j: (0, 0))],
        out_shape=[S((t, d), F32), S((8, d), F32), S((8, d), F32), S((8, d), F32)],
        out_specs=[pl.BlockSpec((tm, d), lambda i, j: (i, 0)), vec, vec, vec],
        scratch_shapes=[pltpu.VMEM((tm, d), F32)],
        sem=("arbitrary", "arbitrary"), plans=plans)


def _tn_matmul(a, b, a_block, a_map, b_block, b_map, out_shape, out_block, out_map, nblk, name, plans=()):
    t = a.shape[-2]
    tk = min(K_TILE, t)
    nk = t // tk

    def body(a_ref, b_ref, o_ref, acc_ref):
        k = pl.program_id(1)
        for q in (range(a_ref.shape[0]) if len(a_ref.shape) == 3 else [Ellipsis]):
            part = _dot_tn(a_ref[q], b_ref[...])

            @pl.when(k == 0)
            def _():
                acc_ref[q] = part

            @pl.when(k > 0)
            def _():
                acc_ref[q] += part

        @pl.when(k == nk - 1)
        def _():
            o_ref[...] = acc_ref[...].astype(BF16)

    (out,), moved = _pcall(
        body, name, [a, b], grid=(nblk, nk),
        in_specs=[pl.BlockSpec(a_block(tk), a_map), pl.BlockSpec(b_block(tk), b_map)],
        out_shape=[S(out_shape, BF16)], out_specs=[pl.BlockSpec(out_block, out_map)],
        scratch_shapes=[pltpu.VMEM(tuple(n for n in out_block if n is not None), F32)],
        sem=("parallel", "arbitrary"), plans=plans)
    return out, moved


def _ffn_backward(dxo, x_in, h, gu, f, ada9, nrm, w_in, w_out, rows, core, name, ds_plans=(), dwin_plans=(), later=False):
    sh_row, sc_row, g_row = rows
    _, nb, t, bw = gu.shape
    d = x_in.shape[1]
    (da, dg, dw_out), ds_moved = _ffn_bwd_ds(dxo, f, ada9, w_out, gu, g_row, 0.5, name + "_ds", plans=ds_plans)
    dw_out = dw_out.reshape(NDEV, -1, d)
    dw_in, ((half_out,), *dwin_moved) = _tn_matmul(
        da.reshape(2 * nb, t, bw), h, lambda tk: (2, tk, bw), lambda sb, k: (sb, k, 0), lambda tk: (tk, d), lambda sb, k: (k, 0),
        (2 * nb, bw, d), (2, bw, d), lambda sb, k: (sb, 0, 0), nb, name + "_dwin",
        plans=[_plan_sibling_swap([dw_out])] + list(dwin_plans(ds_moved) if callable(dwin_plans) else dwin_plans))
    sum_out = _pair_add(dw_out, half_out, core, name + "_dwout_add")
    if later:
        (dx, dsh, dsc, dn), ((half_in,),) = _ffn_bwd_dh(da, w_in, x_in, dxo, ada9, nrm, sh_row, sc_row, name + "_dh",
                                                       plans=[_plan_sibling_swap([dw_in])])
        return dx, _pair_add(dw_in, half_in, core, name + "_dwin_add"), sum_out, (dsh, dsc, dg, dn), ds_moved, dwin_moved
    ((half_in,),) = _exchange([_plan_sibling_swap([dw_in])], name + "_dwin_swap")
    sum_in = _pair_add(dw_in, half_in, core, name + "_dwin_add")
    head = sum_out.shape[1] // 2
    (dx, dsh, dsc, dn), ((recv_in,), (out_begun,)) = _ffn_bwd_dh(
        da, w_in, x_in, dxo, ada9, nrm, sh_row, sc_row, name + "_dh",
        plans=[_plan_chip_swap([sum_in]), _plan_chip_swap([sum_out], rows=(0, head))])
    return dx, recv_in, (sum_out, out_begun, head), (dsh, dsc, dg, dn), ds_moved, dwin_moved


def _mix_in(x, ada9, nrm, w, widths, dts, name, plans=()):
    t, d = x.shape
    n = w.shape[0]
    tm = min(ROW_TILE, t)
    starts = [sum(widths[:i]) for i in range(len(widths))]

    def body(x_ref, ada_ref, n_ref, w_ref, h_ref, *out_refs):
        _, _, h = _normmod(x_ref[...], n_ref[...], ada_ref[4:5, :], ada_ref[3:4, :])
        hb = h.astype(BF16)
        h_ref[...] = hb
        for o_ref, st, wd in zip(out_refs, starts, widths):
            o_ref[...] = _dot_nt(hb, w_ref[st:st + wd, :]).astype(o_ref.dtype)

    return _pcall(
        body, name, [x, ada9, nrm, w], grid=(t // tm,),
        in_specs=[pl.BlockSpec((tm, d), lambda i: (i, 0)), pl.BlockSpec((N_ADA, d), lambda i: (0, 0)),
                  pl.BlockSpec((1, d), lambda i: (0, 0)), pl.BlockSpec((n, d), lambda i: (0, 0))],
        out_shape=[S((t, d), BF16)] + [S((t, wd), dt) for wd, dt in zip(widths, dts)],
        out_specs=[pl.BlockSpec((tm, d), lambda i: (i, 0))] + [pl.BlockSpec((tm, wd), lambda i: (i, 0)) for wd in widths],
        sem=("parallel",), plans=plans)


def _tri(lower):
    r = lax.broadcasted_iota(jnp.int32, (CHUNK, CHUNK), 0)
    c = lax.broadcasted_iota(jnp.int32, (CHUNK, CHUNK), 1)
    return (r >= c) if lower else (c >= r)


def _dot_01(m, x):
    hi = x.astype(BF16)
    r1 = x - hi.astype(F32)
    mid = r1.astype(BF16)
    lo = (r1 - mid.astype(F32)).astype(BF16)
    return _dot(m, hi) + _dot(m, mid) + _dot(m, lo)


def _gla_chunk_terms(q, k, lg, low01):
    b = _dot_01(low01, lg)
    bl = b[CHUNK - 1:CHUNK, :]
    r = 0.5 * bl
    eb, ebl, em, en = jnp.exp(b), jnp.exp(bl - b), jnp.exp(b - r), jnp.exp(r - b)
    return eb, ebl, em, en, jnp.exp(bl), q * eb, k * ebl, q * em, k * en


def _scores(qm_h, knp, qk1_h):
    r = lax.broadcasted_iota(jnp.int32, (CHUNK, CHUNK), 0)
    c = lax.broadcasted_iota(jnp.int32, (CHUNK, CHUNK), 1)
    p = jnp.where(r > c, _dot_nt(qm_h, knp), 0.0)
    return jnp.where(r == c, jnp.sum(qk1_h, axis=1, keepdims=True), p)


def _gla_fwd(qk, v, gl, wg, bg, heads, name, plans=()):
    t = qk.shape[0]
    kw, vw = qk.shape[1] // 2, v.shape[1]
    dk, dv = kw // heads, vw // heads
    assert dk == 64 and dv == 128 and kw % 128 == 0
    gt = min(ROW_TILE, t)
    nc = gt // CHUNK
    scale = dk ** -0.5

    def body(qk_ref, v_ref, gl_ref, wg_ref, bg_ref, o_ref, lg_ref, sall_ref, st_ref):
        @pl.when(pl.program_id(0) == 0)
        def _():
            st_ref[...] = jnp.zeros_like(st_ref)

        gk = _dot(gl_ref[...].astype(BF16), wg_ref[...]) + bg_ref[...]
        lg_ref[...] = (jnp.minimum(gk, 0.0) - jnp.log(1.0 + jnp.exp(-jnp.abs(gk)))) / GATE_NORMALIZER
        low01 = _tri(True).astype(BF16)
        lane = lax.broadcasted_iota(jnp.int32, (CHUNK, LANES), 1)

        def chunk(ci, carry):
            rows = pl.ds(pl.multiple_of(ci * CHUNK, CHUNK), CHUNK)
            q = qk_ref[rows, 0:kw] * scale
            k = qk_ref[rows, kw:2 * kw]
            qk1 = q.astype(BF16).astype(F32) * k.astype(BF16).astype(F32)
            eb, ebl, em, en, ebl_row, qe, ke, qm, kn = _gla_chunk_terms(q, k, lg_ref[rows, :], low01)
            for h in range(heads):
                lanes = slice(LANES * (h // 2), LANES * (h // 2) + LANES)
                own = (lane < 64) if h % 2 == 0 else (lane >= 64)
                knp = kn[:, lanes].astype(BF16)
                qm_h = jnp.where(own, qm[:, lanes], 0.0).astype(BF16)
                qe_h = jnp.where(own, qe[:, lanes], 0.0).astype(BF16)
                ke_h = jnp.where(own, ke[:, lanes], 0.0).astype(BF16)
                v_h = v_ref[rows, h * dv:(h + 1) * dv]
                st = st_ref[h]
                sall_ref[ci, h] = st
                p = _scores(qm_h, knp, jnp.where(own, qk1[:, lanes], 0.0))
                o_ref[rows, h * dv:(h + 1) * dv] = _dot(p.astype(BF16), v_h) + _dot_nt(qe_h, st.astype(BF16))
                st_ref[h] = st * ebl_row[:, lanes] + _dot_tn(v_h, ke_h)
            return carry

        lax.fori_loop(0, nc, chunk, 0, unroll=True)

    return _pcall(
        body, name, [qk, v, gl, wg, bg], grid=(t // gt,),
        in_specs=[pl.BlockSpec((gt, 2 * kw), lambda i: (i, 0)), pl.BlockSpec((gt, vw), lambda i: (i, 0)),
                  pl.BlockSpec((gt, LANES), lambda i: (i, 0)), pl.BlockSpec((LANES, kw), lambda i: (0, 0)),
                  pl.BlockSpec((1, kw), lambda i: (0, 0))],
        out_shape=[S((t, vw), F32), S((t, kw), F32), S((t // CHUNK, heads, dv, LANES), F32)],
        out_specs=[pl.BlockSpec((gt, vw), lambda i: (i, 0)), pl.BlockSpec((gt, kw), lambda i: (i, 0)),
                   pl.BlockSpec((nc, heads, dv, LANES), lambda i: (i, 0, 0, 0))],
        scratch_shapes=[pltpu.VMEM((heads, dv, LANES), F32)],
        sem=("arbitrary",), plans=plans)


def _gla_bwd(qk, v, lg, do, sall, gl, wg, heads, name, plans=()):
    t = qk.shape[0]
    kw, vw = qk.shape[1] // 2, v.shape[1]
    dk, dv = kw // heads, vw // heads
    gt = min(ROW_TILE, t)
    nc = gt // CHUNK
    nt = t // gt
    scale = dk ** -0.5

    def body(qk_ref, v_ref, lg_ref, do_ref, sall_ref, gl_ref, wg_ref, dqk_ref, dv_ref, dgl_ref, dwg_ref, dbg_ref, dst_ref, dgk_ref):
        @pl.when(pl.program_id(0) == 0)
        def _():
            dst_ref[...] = jnp.zeros_like(dst_ref)
            dwg_ref[...] = jnp.zeros_like(dwg_ref)
            dbg_ref[...] = jnp.zeros_like(dbg_ref)

        low01 = _tri(True).astype(BF16)
        up01 = _tri(False).astype(BF16)
        causal = _tri(True)
        lane = lax.broadcasted_iota(jnp.int32, (CHUNK, LANES), 1)
        last_row = lax.broadcasted_iota(jnp.int32, (CHUNK, kw), 0) == CHUNK - 1

        def chunk(cj, carry):
            ci = nc - 1 - cj
            rows = pl.ds(pl.multiple_of(ci * CHUNK, CHUNK), CHUNK)
            q = qk_ref[rows, 0:kw] * scale
            k = qk_ref[rows, kw:2 * kw]
            qk1 = q.astype(BF16).astype(F32) * k.astype(BF16).astype(F32)
            lgc = lg_ref[rows, :]
            eb, ebl, em, en, ebl_row, qe, ke, qm, kn = _gla_chunk_terms(q, k, lgc, low01)
            dqe, dqm, dkn, dke, drow = [], [], [], [], []
            for pr in range(kw // LANES):
                lanes = slice(LANES * pr, LANES * pr + LANES)
                knp = kn[:, lanes].astype(BF16)
                parts = []
                for half in range(2):
                    h = 2 * pr + half
                    own = (lane < 64) if half == 0 else (lane >= 64)
                    qm_h = jnp.where(own, qm[:, lanes], 0.0).astype(BF16)
                    qe_h = jnp.where(own, qe[:, lanes], 0.0).astype(BF16)
                    ke_h = jnp.where(own, ke[:, lanes], 0.0).astype(BF16)
                    v_h = v_ref[rows, h * dv:(h + 1) * dv]
                    do_h = do_ref[rows, h * dv:(h + 1) * dv]
                    st = sall_ref[ci, h]
                    dst = dst_ref[h]
                    stb, dstb = st.astype(BF16), dst.astype(BF16)
                    p = _scores(qm_h, knp, jnp.where(own, qk1[:, lanes], 0.0)).astype(BF16)
                    dp = jnp.where(causal, _dot_nt(do_h, v_h), 0.0).astype(BF16)
                    dv_ref[rows, h * dv:(h + 1) * dv] = (_dot_tn(p, do_h) + _dot_nt(ke_h, dstb)).astype(BF16)
                    parts.append((jnp.where(own, _dot(dp, knp), 0.0), _dot_tn(dp, qm_h), _dot(do_h, stb), _dot(v_h, dstb),
                                  jnp.sum(st * dst, axis=0, keepdims=True)))
                    dst_ref[h] = dst * ebl_row[:, lanes] + _dot_tn(do_h, qe_h)
                dqm.append(parts[0][0] + parts[1][0])
                dkn.append(parts[0][1] + parts[1][1])
                dqe.append(parts[0][2] + parts[1][2])
                dke.append(parts[0][3] + parts[1][3])
                drow.append(parts[0][4] + parts[1][4])
            dqm, dkn, dqe, dke, drow = [jnp.concatenate(a, axis=1) for a in (dqm, dkn, dqe, dke, drow)]
            dqk_ref[rows, 0:kw] = ((dqe * eb + dqm * em) * scale).astype(BF16)
            dqk_ref[rows, kw:2 * kw] = (dke * ebl + dkn * en).astype(BF16)
            tke = dke * ke
            db = dqe * qe + dqm * qm - dkn * kn - tke
            dbl = jnp.sum(tke, axis=0, keepdims=True) + drow * ebl_row
            db = db + jnp.where(last_row, dbl, 0.0)
            dlg = _dot_01(up01, db)
            dgk_ref[rows, :] = dlg * ((1.0 - jnp.exp(GATE_NORMALIZER * lgc)) / GATE_NORMALIZER)
            return carry

        lax.fori_loop(0, nc, chunk, 0, unroll=True)
        dgk = dgk_ref[...]
        dgkb = dgk.astype(BF16)
        dgl_ref[...] = _dot_nt(dgkb, wg_ref[...]).astype(BF16)
        dwg_ref[...] += _dot_tn(gl_ref[...].astype(BF16), dgkb)
        dbg_ref[...] += _rsum8(dgk)

    rev = lambda i: (nt - 1 - i, 0)
    return _pcall(
        body, name, [qk, v, lg, do, sall, gl, wg], grid=(nt,),
        in_specs=[pl.BlockSpec((gt, 2 * kw), rev), pl.BlockSpec((gt, vw), rev), pl.BlockSpec((gt, kw), rev),
                  pl.BlockSpec((gt, vw), rev), pl.BlockSpec((nc, heads, dv, LANES), lambda i: (nt - 1 - i, 0, 0, 0)),
                  pl.BlockSpec((gt, LANES), rev), pl.BlockSpec((LANES, kw), lambda i: (0, 0))],
        out_shape=[S((t, 2 * kw), BF16), S((t, vw), BF16), S((t, LANES), BF16), S((LANES, kw), F32), S((8, kw), F32)],
        out_specs=[pl.BlockSpec((gt, 2 * kw), rev), pl.BlockSpec((gt, vw), rev), pl.BlockSpec((gt, LANES), rev),
                   pl.BlockSpec((LANES, kw), lambda i: (0, 0)), pl.BlockSpec((8, kw), lambda i: (0, 0))],
        scratch_shapes=[pltpu.VMEM((heads, dv, LANES), F32), pltpu.VMEM((gt, kw), F32)],
        sem=("arbitrary",), plans=plans)


def _conv_taps(cx_ref, halo_ref, first, cw):
    tm = cx_ref.shape[0]
    u = cx_ref[:, cw:2 * cw].astype(F32) * cx_ref[:, 2 * cw:3 * cw].astype(F32)
    uh = halo_ref[:, cw:2 * cw].astype(F32) * halo_ref[:, 2 * cw:3 * cw].astype(F32)
    uh = jnp.where(first, 0.0, uh)
    before1, before2 = uh[BF16_ROWS - 1:BF16_ROWS, :], uh[BF16_ROWS - 2:BF16_ROWS - 1, :]
    row = lax.broadcasted_iota(jnp.int32, (tm, cw), 0)
    u1 = jnp.where(row == 0, before1, pltpu.roll(u, 1, 0))
    u2 = jnp.where(row == 0, before2, jnp.where(row == 1, before1, pltpu.roll(u, 2, 0)))
    return u, u1, u2


def _head_norm(o_h, gn):
    rstd = lax.rsqrt(jnp.mean(o_h * o_h, axis=-1, keepdims=True) + EPS)
    ohat = o_h * rstd
    return ohat, rstd, ohat * gn


def _mix_out(cx, o, go, conv_w, gn, w_out, x, ada9, heads, name, plans=()):
    t, d = x.shape
    cw, vw = conv_w.shape[1], o.shape[1]
    dv = vw // heads
    tm = min(ROW_TILE, t)

    def body(cx_ref, halo_ref, o_ref, go_ref, cwt_ref, gn_ref, w_ref, x_ref, ada_ref, xo_ref, m_ref, y_ref):
        u, u1, u2 = _conv_taps(cx_ref, halo_ref, pl.program_id(0) == 0, cw)
        yc = cwt_ref[0:1, :] * u2 + cwt_ref[1:2, :] * u1 + cwt_ref[2:3, :] * u
        y_ref[:, 0:cw] = (cx_ref[:, 0:cw].astype(F32) * yc).astype(BF16)
        for h in range(heads):
            cols = slice(h * dv, (h + 1) * dv)
            _, _, on = _head_norm(o_ref[:, cols], gn_ref[...])
            g = go_ref[:, cols].astype(F32)
            y_ref[:, cw + h * dv:cw + (h + 1) * dv] = (on * (g * _sigmoid(g))).astype(BF16)
        m = _dot(y_ref[...], w_ref[...])
        m_ref[...] = m.astype(BF16)
        xo_ref[...] = x_ref[...] + ada_ref[5:6, :] * m

    return _pcall(
        body, name, [cx, cx, o, go, conv_w, gn, w_out, x, ada9], grid=(t // tm,),
        in_specs=[pl.BlockSpec((tm, 3 * cw), lambda i: (i, 0)),
                  pl.BlockSpec((BF16_ROWS, 3 * cw), lambda i: (jnp.maximum(i * (tm // BF16_ROWS) - 1, 0), 0)),
                  pl.BlockSpec((tm, vw), lambda i: (i, 0)), pl.BlockSpec((tm, vw), lambda i: (i, 0)),
                  pl.BlockSpec((3, cw), lambda i: (0, 0)), pl.BlockSpec((1, dv), lambda i: (0, 0)),
                  pl.BlockSpec((cw + vw, d), lambda i: (0, 0)), pl.BlockSpec((tm, d), lambda i: (i, 0)),
                  pl.BlockSpec((N_ADA, d), lambda i: (0, 0))],
        out_shape=[S((t, d), F32), S((t, d), BF16), S((t, cw + vw), BF16)],
        out_specs=[pl.BlockSpec((tm, d), lambda i: (i, 0)), pl.BlockSpec((tm, d), lambda i: (i, 0)),
                   pl.BlockSpec((tm, cw + vw), lambda i: (i, 0))],
        sem=("parallel",), plans=plans)


def _mix_bwd_a(dxo, m, ycat, ada9, w_out, cx, o, go, conv_w, gn, heads, name, plans=()):
    t, d = dxo.shape
    cw, vw = conv_w.shape[1], o.shape[1]
    dv = vw // heads
    tm = min(ROW_TILE, t)
    nt = t // tm

    def body(dxo_ref, m_ref, y_ref, ada_ref, w_ref, cx_ref, halo_ref, o_ref, go_ref, cwt_ref, gn_ref,
             dw_ref, dyc_ref, dcb_ref, do_ref, dgo_ref, dg_ref, dcw_ref, dgn_ref, acc_ref):
        @pl.when(pl.program_id(0) == 0)
        def _():
            dg_ref[...] = jnp.zeros_like(dg_ref)
            dcw_ref[...] = jnp.zeros_like(dcw_ref)
            dgn_ref[...] = jnp.zeros_like(dgn_ref)

        dxo_t = dxo_ref[...]
        dmb = (ada_ref[5:6, :] * dxo_t).astype(BF16)
        part = _dot_tn(y_ref[...], dmb)

        @pl.when(pl.program_id(0) == 0)
        def _():
            acc_ref[...] = part

        @pl.when(pl.program_id(0) > 0)
        def _():
            acc_ref[...] += part

        @pl.when(pl.program_id(0) == nt - 1)
        def _():
            dw_ref[...] = acc_ref[...].astype(BF16)

        dg_ref[...] += _rsum8(dxo_t * m_ref[...].astype(F32))
        dy = _dot_nt(dmb, w_ref[...])
        u, u1, u2 = _conv_taps(cx_ref, halo_ref, pl.program_id(0) == 0, cw)
        yc = cwt_ref[0:1, :] * u2 + cwt_ref[1:2, :] * u1 + cwt_ref[2:3, :] * u
        dyv = dy[:, 0:cw]
        dcb_ref[...] = (dyv * yc).astype(BF16)
        dyc = dyv * cx_ref[:, 0:cw].astype(F32)
        dyc_ref[...] = dyc
        dcw_ref[0] += _rsum8(dyc * u2)
        dcw_ref[1] += _rsum8(dyc * u1)
        dcw_ref[2] += _rsum8(dyc * u)
        for h in range(heads):
            cols = slice(h * dv, (h + 1) * dv)
            ohat, rstd, on = _head_norm(o_ref[:, cols], gn_ref[...])
            g = go_ref[:, cols].astype(F32)
            sg = _sigmoid_tanh(g)
            dyg = dy[:, cw + h * dv:cw + (h + 1) * dv]
            dgo_ref[:, cols] = (dyg * on * (sg * (1.0 + g * (1.0 - sg)))).astype(BF16)
            don = dyg * (g * sg)
            dgn_ref[...] += _rsum8(don * ohat)
            tt = don * gn_ref[...]
            do_ref[:, cols] = (rstd * (tt - ohat * jnp.mean(tt * ohat, axis=-1, keepdims=True))).astype(BF16)

    return _pcall(
        body, name, [dxo, m, ycat, ada9, w_out, cx, cx, o, go, conv_w, gn], grid=(nt,),
        in_specs=[pl.BlockSpec((tm, d), lambda i: (i, 0)), pl.BlockSpec((tm, d), lambda i: (i, 0)),
                  pl.BlockSpec((tm, cw + vw), lambda i: (i, 0)),
                  pl.BlockSpec((N_ADA, d), lambda i: (0, 0)), pl.BlockSpec((cw + vw, d), lambda i: (0, 0)),
                  pl.BlockSpec((tm, 3 * cw), lambda i: (i, 0)),
                  pl.BlockSpec((BF16_ROWS, 3 * cw), lambda i: (jnp.maximum(i * (tm // BF16_ROWS) - 1, 0), 0)),
                  pl.BlockSpec((tm, vw), lambda i: (i, 0)), pl.BlockSpec((tm, vw), lambda i: (i, 0)),
                  pl.BlockSpec((3, cw), lambda i: (0, 0)), pl.BlockSpec((1, dv), lambda i: (0, 0))],
        out_shape=[S((cw + vw, d), BF16), S((t, cw), F32), S((t, cw), BF16), S((t, vw), BF16), S((t, vw), BF16),
                   S((8, d), F32), S((3, 8, cw), F32), S((8, dv), F32)],
        out_specs=[pl.BlockSpec((cw + vw, d), lambda i: (0, 0)), pl.BlockSpec((tm, cw), lambda i: (i, 0)),
                   pl.BlockSpec((tm, cw), lambda i: (i, 0)), pl.BlockSpec((tm, vw), lambda i: (i, 0)),
                   pl.BlockSpec((tm, vw), lambda i: (i, 0)), pl.BlockSpec((8, d), lambda i: (0, 0)),
                   pl.BlockSpec((3, 8, cw), lambda i: (0, 0, 0)), pl.BlockSpec((8, dv), lambda i: (0, 0))],
        scratch_shapes=[pltpu.VMEM((cw + vw, d), F32)],
        sem=("arbitrary",), plans=plans)


def _mix_bwd_b(dyc, cx, dcb, dqk, dvv, dgo, dgl, conv_w, w, x, dxo, ada9, nrm, name, plans=()):
    t, d = x.shape
    cw = conv_w.shape[1]
    n = w.shape[0]
    tm = min(ROW_TILE, t)
    nt = t // tm
    pieces = [dcb.shape[1], cw, cw, dqk.shape[1], dvv.shape[1], dgo.shape[1], dgl.shape[1]]
    assert sum(pieces) == n

    def body(dyc_ref, nxt_ref, cx_ref, dcb_ref, dqk_ref, dv_ref, dgo_ref, dgl_ref, cwt_ref, w_ref, x_ref, dxo_ref, ada_ref, n_ref,
             dx_ref, dp_ref, dsh_ref, dsc_ref, dn_ref):
        i = pl.program_id(0)

        @pl.when(i == 0)
        def _():
            dsh_ref[...] = jnp.zeros_like(dsh_ref)
            dsc_ref[...] = jnp.zeros_like(dsc_ref)
            dn_ref[...] = jnp.zeros_like(dn_ref)

        dyc_t = dyc_ref[...]
        nxt = jnp.where(i == nt - 1, 0.0, nxt_ref[...])
        row = lax.broadcasted_iota(jnp.int32, (tm, cw), 0)
        d1 = jnp.where(row == tm - 1, nxt[0:1, :], pltpu.roll(dyc_t, tm - 1, 0))
        d2 = jnp.where(row == tm - 2, nxt[0:1, :], jnp.where(row == tm - 1, nxt[1:2, :], pltpu.roll(dyc_t, tm - 2, 0)))
        du = cwt_ref[2:3, :] * dyc_t + cwt_ref[1:2, :] * d1 + cwt_ref[0:1, :] * d2
        c0 = 0
        dp_ref[:, c0:c0 + cw] = dcb_ref[...]
        dp_ref[:, cw:2 * cw] = (du * cx_ref[:, 2 * cw:3 * cw].astype(F32)).astype(BF16)
        dp_ref[:, 2 * cw:3 * cw] = (du * cx_ref[:, cw:2 * cw].astype(F32)).astype(BF16)
        c0 = 3 * cw
        for ref in (dqk_ref, dv_ref, dgo_ref, dgl_ref):
            wd = ref.shape[1]
            dp_ref[:, c0:c0 + wd] = ref[...]
            c0 += wd
        dh = _dot(dp_ref[...], w_ref[...])
        dx, tsh, tsc, tn = _normmod_bwd(dh, x_ref[...], n_ref[...], ada_ref[4:5, :])
        dx_ref[...] = dxo_ref[...] + dx
        dsh_ref[...] += _rsum8(tsh)
        dsc_ref[...] += _rsum8(tsc)
        dn_ref[...] += _rsum8(tn)

    row_spec = lambda wd: pl.BlockSpec((tm, wd), lambda i: (i, 0))
    vec = pl.BlockSpec((8, d), lambda i: (0, 0))
    return _pcall(
        body, name, [dyc, dyc, cx, dcb, dqk, dvv, dgo, dgl, conv_w, w, x, dxo, ada9, nrm], grid=(nt,),
        in_specs=[row_spec(cw), pl.BlockSpec((8, cw), lambda i: (jnp.minimum((i + 1) * (tm // 8), t // 8 - 1), 0)),
                  row_spec(3 * cw), row_spec(cw), row_spec(dqk.shape[1]), row_spec(dvv.shape[1]), row_spec(dgo.shape[1]),
                  row_spec(dgl.shape[1]), pl.BlockSpec((3, cw), lambda i: (0, 0)), pl.BlockSpec((n, d), lambda i: (0, 0)),
                  row_spec(d), row_spec(d), pl.BlockSpec((N_ADA, d), lambda i: (0, 0)), pl.BlockSpec((1, d), lambda i: (0, 0))],
        out_shape=[S((t, d), F32), S((t, n), BF16), S((8, d), F32), S((8, d), F32), S((8, d), F32)],
        out_specs=[row_spec(d), row_spec(n), vec, vec, vec],
        sem=("arbitrary",), plans=plans)


def _ffn_out_loss(s, w_out, x, ada9, g_row, res_scale, target, nrm, name):
    nb, t, bw = s.shape
    d = x.shape[1]
    tm = min(ROW_TILE, t)
    nt = t // tm

    def body(s_ref, w_ref, x_ref, ada_ref, tg_ref, n_ref, f_ref, loss_ref, dx_ref, dn_ref, acc_ref):
        i = pl.program_id(0)

        @pl.when(i == 0)
        def _():
            acc_ref[...] = jnp.zeros_like(acc_ref)
            dn_ref[...] = jnp.zeros_like(dn_ref)

        f = _dot(s_ref[0], w_ref[0])
        for b in range(1, nb):
            f = f + _dot(s_ref[b], w_ref[b])
        f_ref[...] = f.astype(BF16)
        xt = x_ref[...] + (res_scale * ada_ref[g_row:g_row + 1, :]) * f
        rstd = lax.rsqrt(jnp.mean(xt * xt, axis=-1, keepdims=True) + EPS)
        xhat = xt * rstd
        err = xhat * n_ref[...] - tg_ref[...]
        acc_ref[...] += _rsum8(err * err)
        dy = err * (1.0 / d)
        dn_ref[...] += _rsum8(dy * xhat)
        dxhat = dy * n_ref[...]
        dx_ref[...] = rstd * (dxhat - xhat * jnp.mean(dxhat * xhat, axis=-1, keepdims=True))

        @pl.when(i == nt - 1)
        def _():
            loss_ref[...] = jnp.full(loss_ref.shape, (0.5 / d) * jnp.sum(acc_ref[...]), F32)

    return pl.pallas_call(
        body, name=name, grid=(nt,),
        in_specs=[pl.BlockSpec((nb, tm, bw), lambda i: (0, i, 0)), pl.BlockSpec((nb, bw, d), lambda i: (0, 0, 0)),
                  pl.BlockSpec((tm, d), lambda i: (i, 0)), pl.BlockSpec((N_ADA, d), lambda i: (0, 0)),
                  pl.BlockSpec((tm, d), lambda i: (i, 0)), pl.BlockSpec((1, d), lambda i: (0, 0))],
        out_shape=[S((t, d), BF16), S((1, LANES), F32), S((t, d), F32), S((8, d), F32)],
        out_specs=[pl.BlockSpec((tm, d), lambda i: (i, 0)), pl.BlockSpec((1, LANES), lambda i: (0, 0)),
                   pl.BlockSpec((tm, d), lambda i: (i, 0)), pl.BlockSpec((8, d), lambda i: (0, 0))],
        scratch_shapes=[pltpu.VMEM((8, d), F32)],
        compiler_params=_cp("arbitrary"),
    )(s, w_out, x, ada9, target, nrm)


def _pack_smalls(vec_parts, dcw, dbg, dgn, dwg, loss_v, rank, name):
    d = vec_parts[0].shape[1]
    cw, kw, dv = dcw.shape[2], dbg.shape[1], dgn.shape[1]
    nv = len(vec_parts)
    loss_row = nv + 2 + rank * kw // d
    assert 2 * cw == d and cw + kw + dv <= d and (rank * kw) % d == 0 and loss_row < PACK_ROWS
    per_row = d // kw

    def body(*refs):
        vrefs, (dcw_ref, dbg_ref, dgn_ref, dwg_ref, loss_ref, o_ref) = refs[:nv], refs[nv:]
        o_ref[...] = jnp.zeros_like(o_ref)
        o_ref[loss_row:loss_row + 1, 0:loss_ref.shape[1]] = loss_ref[...]
        for r, ref in enumerate(vrefs):
            o_ref[r:r + 1, :] = jnp.sum(ref[...], axis=0, keepdims=True)
        o_ref[nv:nv + 1, 0:cw] = jnp.sum(dcw_ref[0], axis=0, keepdims=True)
        o_ref[nv:nv + 1, cw:2 * cw] = jnp.sum(dcw_ref[1], axis=0, keepdims=True)
        o_ref[nv + 1:nv + 2, 0:cw] = jnp.sum(dcw_ref[2], axis=0, keepdims=True)
        o_ref[nv + 1:nv + 2, cw:cw + kw] = jnp.sum(dbg_ref[...], axis=0, keepdims=True)
        o_ref[nv + 1:nv + 2, cw + kw:cw + kw + dv] = jnp.sum(dgn_ref[...], axis=0, keepdims=True)
        for r in range(rank):
            o_ref[nv + 2 + r // per_row:nv + 3 + r // per_row, (r % per_row) * kw:(r % per_row + 1) * kw] = dwg_ref[r:r + 1, :]

    return pl.pallas_call(body, name=name, out_shape=S((PACK_ROWS, d), F32), compiler_params=_cp())(*vec_parts, dcw, dbg, dgn, dwg, loss_v)


def _sum_slots(a, name):
    def body(a_ref, o_ref):
        acc = a_ref[0]
        for s in range(1, NDEV):
            acc = acc + a_ref[s]
        o_ref[...] = acc

    return pl.pallas_call(body, name=name, out_shape=S(a.shape[1:], F32), compiler_params=_cp())(a)


def _adamw(w, g, m, v):
    m = ADAM_B1 * m + (1.0 - ADAM_B1) * g
    v = ADAM_B2 * v + (1.0 - ADAM_B2) * (g * g)
    m_hat = m / (1.0 - ADAM_B1 ** ADAM_STEP)
    v_hat = v / (1.0 - ADAM_B2 ** ADAM_STEP)
    return -ADAM_LR * (m_hat / (jnp.sqrt(v_hat) + ADAM_EPS) + ADAM_WD * w), m, v


def _adam_slots(recv, w, m, v, name):
    r, c = w.shape
    slots = recv.shape[0]
    tr = _row_tile(r, c)

    def body(recv_ref, w_ref, m_ref, v_ref, g_ref, d_ref, mo_ref, vo_ref):
        g = recv_ref[0].astype(F32)
        for s in range(1, slots):
            g = g + recv_ref[s].astype(F32)
        g_ref[...] = g
        d_ref[...], mo_ref[...], vo_ref[...] = _adamw(w_ref[...], g, m_ref[...], v_ref[...])

    blk = pl.BlockSpec((tr, c), lambda i: (i, 0))
    return pl.pallas_call(
        body, name=name, grid=(r // tr,),
        in_specs=[pl.BlockSpec((slots, tr, c), lambda i: (0, i, 0)), blk, blk, blk],
        out_shape=[S((r, c), F32)] * 4, out_specs=[blk] * 4, compiler_params=_cp("parallel"),
    )(recv, w, m, v)


def _adam_w_ada(act_t, dada, w, m, v, name):
    r, c = w.shape
    tr = 128
    nb = act_t.shape[1]

    def body(a_ref, da_ref, w_ref, m_ref, v_ref, g_ref, d_ref, mo_ref, vo_ref):
        g = a_ref[:, 0:1] * da_ref[0:1, :]
        for b in range(1, nb):
            g = g + a_ref[:, b:b + 1] * da_ref[b:b + 1, :]
        g_ref[...] = g
        d_ref[...], mo_ref[...], vo_ref[...] = _adamw(w_ref[...], g, m_ref[...], v_ref[...])

    blk = pl.BlockSpec((tr, c), lambda i: (i, 0))
    return pl.pallas_call(
        body, name=name, grid=(r // tr,),
        in_specs=[pl.BlockSpec((tr, nb), lambda i: (i, 0)), pl.BlockSpec((nb, c), lambda i: (0, 0)), blk, blk, blk],
        out_shape=[S((r, c), F32)] * 4, out_specs=[blk] * 4, compiler_params=_cp("parallel"),
    )(act_t, dada, w, m, v)


def _adam_smalls(ws, gs, ms, vs, name):
    n = len(ws)

    def body(*refs):
        w_r, g_r, m_r, v_r = (refs[k * n:(k + 1) * n] for k in range(4))
        d_o, m_o, v_o = (refs[(4 + k) * n:(5 + k) * n] for k in range(3))
        for i in range(n):
            d_o[i][...], m_o[i][...], v_o[i][...] = _adamw(w_r[i][...], g_r[i][...], m_r[i][...], v_r[i][...])

    shapes = [S(w.shape, F32) for w in ws]
    outs = pl.pallas_call(body, name=name, out_shape=shapes * 3, compiler_params=_cp())(*ws, *gs, *ms, *vs)
    return outs[:n], outs[n:2 * n], outs[2 * n:]


def kernel(x, c, w_ada, b_ada, norm_ffn1, w_ffn1_in, w_ffn1_out, norm_mix, w_mix_in, conv_w, w_gk2, b_gk, gla_norm, w_mix_out, norm_ffn2, w_ffn2_in, w_ffn2_out, norm_final, loss_target, m_w_ada, m_b_ada, m_norm_ffn1, m_w_ffn1_in, m_w_ffn1_out, m_norm_mix, m_w_mix_in, m_conv_w, m_w_gk2, m_b_gk, m_gla_norm, m_w_mix_out, m_norm_ffn2, m_w_ffn2_in, m_w_ffn2_out, m_norm_final, v_w_ada, v_b_ada, v_norm_ffn1, v_w_ffn1_in, v_w_ffn1_out, v_norm_mix, v_w_mix_in, v_conv_w, v_w_gk2, v_b_gk, v_gla_norm, v_w_mix_out, v_norm_ffn2, v_w_ffn2_in, v_w_ffn2_out, v_norm_final):
    t, d = x.shape[1], x.shape[2]
    x0, tgt = x[0], loss_target[0]
    rank, kw = w_gk2.shape[1], w_gk2.shape[2] * NDEV
    cw = conv_w.shape[2] * NDEV
    dv = gla_norm.shape[1]
    vw = d - cw
    heads = vw // dv
    mix_cols = w_mix_in.shape[2]
    widths = [3 * cw, 2 * kw, vw, vw, LANES]
    n_proj = 3 * cw + 2 * kw + 2 * vw + rank
    assert n_proj == mix_cols * NDEV and rank <= LANES
    me = 4 * lax.axis_index("x") + 2 * lax.axis_index("y") + lax.axis_index("c")

    core = lax.axis_index("c").astype(jnp.int32).reshape(1)
    bf = lambda a: a[0].astype(BF16)
    bft = lambda a: a[0].T.astype(BF16)
    nb = NDEV // 2

    (ada_row, act_all), ((w1i, cwt_all, wg_all),) = _ada_rows(
        c, w_ada[0], b_ada, "ada_rows", plans=[_plan_gather([bft(w_ffn1_in), conv_w[0], w_gk2[0]])])
    ada9 = ada_row.reshape(N_ADA, d)
    cwt = cwt_all.transpose(1, 0, 2).reshape(conv_w.shape[1], cw)
    wg = jnp.pad(wg_all.transpose(1, 0, 2).reshape(rank, kw), ((0, LANES - rank), (0, 0))).astype(BF16)

    wmi_mine, w2i_mine = bft(w_mix_in), bft(w_ffn2_in)
    first_rows = (wmi_mine.shape[0] // 2) // BF16_ROWS * BF16_ROWS
    quarter = w2i_mine.shape[0] // 4
    part = lambda k, into=None: _plan_gather([w2i_mine], rows=(k * quarter, quarter), into=into)
    (h1, gu1, s1), ((w1o,), (wmi,)) = _ffn_in(x0, ada9, norm_ffn1, w1i, 0, 1, "ffn1_in",
                                           plans=[_plan_gather([bf(w_ffn1_out)]), _plan_gather([wmi_mine], rows=(0, first_rows))])
    w1o = w1o.reshape(nb, -1, d)
    (x1, f1), ((wmi,), (w2i,)) = _ffn_out(
        s1, w1o, x0, ada9, 2, 0.5, "ffn1_out",
        plans=[_plan_gather([wmi_mine], rows=(first_rows, wmi_mine.shape[0] - first_rows), into=[wmi]), part(0)])
    wmi = jnp.pad(wmi.reshape(n_proj, d), ((0, sum(widths) - n_proj), (0, 0)))
    (h2, cx, qk, vv, go, gl), ((wmo,), (w2i,)) = _mix_in(x1, ada9, norm_mix, wmi, widths, [BF16, F32, BF16, BF16, F32], "mix_in",
                                                        plans=[_plan_gather([bf(w_mix_out)]), part(1, [w2i])])
    wmo = wmo.reshape(cw + vw, d)
    (o, lg, sall), ((w2i,),) = _gla_fwd(qk, vv, gl, wg, b_gk, heads, "gla_fwd", plans=[part(2, [w2i])])
    (x2, mm, ycat), ((w2i,),) = _mix_out(cx, o, go, cwt, gla_norm, wmo, x1, ada9, heads, "mix_out", plans=[part(3, [w2i])])
    (h3, gu3, s3), ((w2o,),) = _ffn_in(x2, ada9, norm_ffn2, w2i, 6, 7, "ffn2_in", plans=[_plan_gather([bf(w_ffn2_out)])])
    w2o = w2o.reshape(nb, -1, d)
    f3, loss_v, dx3, dnf = _ffn_out_loss(s3, w2o, x2, ada9, 8, 0.5, tgt, norm_final.reshape(1, d), "ffn2_out_loss")

    dx2, sum2i, sum2o, (dsh3, dsc3, dg3, dn3), _, _ = _ffn_backward(
        dx3, x2, h3, gu3, f3, ada9, norm_ffn2, w2i, w2o, (6, 7, 8), core, "ffn2_bwd", later=True)
    (dwmo, dyc, dcb, do, dgo, dg2, dcw, dgn), ((r2o,),) = _mix_bwd_a(dx2, mm, ycat, ada9, wmo, cx, o, go, cwt, gla_norm, heads, "mix_bwd_a",
                                                                   plans=[_plan_chip_swap([sum2o])])
    half = sum2i.shape[1] // 2
    (dqk, dvv, dgl, dwg, dbg), ((r2i,),) = _gla_bwd(qk, vv, lg, do, sall, gl, wg, heads, "gla_bwd",
                                                   plans=[_plan_chip_swap([sum2i], rows=(0, half))])
    (dx1, dproj, dsh2, dsc2, dnm), ((r2i,),) = _mix_bwd_b(dyc, cx, dcb, dqk, dvv, dgo, dgl, cwt, wmi, x1, dx2, ada9, norm_mix, "mix_bwd_b",
                                                         plans=[_plan_chip_swap([sum2i], rows=(half, half), into=[r2i])])
    n_pad = sum(widths)
    tn = n_pad // 5
    dwmi, _ = _tn_matmul(dproj, h2, lambda tk: (tk, tn), lambda sb, k: (k, sb), lambda tk: (tk, d), lambda sb, k: (k, 0),
                         (n_pad, d), (tn, d), lambda sb, k: (sb, 0), 5, "mix_dwin")
    dwmi = dwmi[:n_proj].reshape(NDEV, mix_cols, d)
    dwmo = dwmo.reshape(NDEV, -1, d)
    dx0, r1i, (sum1o, r1o, head), (dsh1, dsc1, dg1, dn1), _, ((rmi, rmo),) = _ffn_backward(
        dx1, x0, h1, gu1, f1, ada9, norm_ffn1, w1i, w1o, (0, 1, 2), core, "ffn1_bwd",
        ds_plans=[_plan_sibling_swap([dwmi, dwmo])],
        dwin_plans=lambda moved: [_plan_chip_swap([_pair_add(dwmi, moved[0][0], core, "mix_dwin_add"),
                                                   _pair_add(dwmo, moved[0][1], core, "mix_dwout_add")])])
    pack = _pack_smalls([dn1, dnm, dn3, dnf, dsh1, dsc1, dg1, dsh2, dsc2, dg2, dsh3, dsc3, dg3], dcw, dbg, dgn, dwg, loss_v, rank, "pack_smalls")
    (r1o,), (pack_all,) = _exchange([_plan_chip_swap([sum1o], rows=(head, sum1o.shape[1] - head), into=[r1o]),
                                     _plan_all_to_all([pack], True)], "grads_last")
    tot = _sum_slots(pack_all, "sum_smalls")

    res = {}
    for nm, recv, w, m, v in (("w_ffn1_out", r1o, w_ffn1_out, m_w_ffn1_out, v_w_ffn1_out), ("w_mix_out", rmo, w_mix_out, m_w_mix_out, v_w_mix_out),
                              ("w_ffn2_out", r2o, w_ffn2_out, m_w_ffn2_out, v_w_ffn2_out)):
        res[nm] = [a[None] for a in _adam_slots(recv, w[0], m[0], v[0], "adam_" + nm)]
    for nm, recv, w, m, v in (("w_ffn1_in", r1i, w_ffn1_in, m_w_ffn1_in, v_w_ffn1_in), ("w_mix_in", rmi, w_mix_in, m_w_mix_in, v_w_mix_in),
                              ("w_ffn2_in", r2i, w_ffn2_in, m_w_ffn2_in, v_w_ffn2_in)):
        res[nm] = [a.T[None] for a in _adam_slots(recv, w[0].T, m[0].T, v[0].T, "adam_" + nm)]

    cols_ada = w_ada.shape[2]
    dada_all = pack_all[:, 4:4 + N_ADA, :].reshape(NDEV, N_ADA * d)
    dada_mine = lax.dynamic_slice_in_dim(dada_all, me * cols_ada, cols_ada, axis=1)
    res["w_ada"] = [a[None] for a in _adam_w_ada(act_all.T, dada_mine, w_ada[0], m_w_ada[0], v_w_ada[0], "adam_w_ada")]

    nv = 4 + N_ADA
    g_small = {
        "b_ada": tot[4:nv].reshape(1, N_ADA * d),
        "norm_ffn1": tot[0:1], "norm_mix": tot[1:2], "norm_ffn2": tot[2:3], "norm_final": tot[3:4],
        "conv_w": lax.dynamic_slice_in_dim(
            jnp.concatenate([tot[nv:nv + 1, 0:cw], tot[nv:nv + 1, cw:2 * cw], tot[nv + 1:nv + 2, 0:cw]], axis=0), me * (cw // NDEV), cw // NDEV, axis=1),
        "w_gk2": lax.dynamic_slice_in_dim(tot[nv + 2:nv + 2 + rank * kw // d].reshape(rank, kw), me * (kw // NDEV), kw // NDEV, axis=1),
        "b_gk": tot[nv + 1:nv + 2, cw:cw + kw],
        "gla_norm": tot[nv + 1:nv + 2, cw + kw:cw + kw + dv],
    }
    small = {"b_ada": (b_ada, m_b_ada, v_b_ada), "norm_ffn1": (norm_ffn1, m_norm_ffn1, v_norm_ffn1), "norm_mix": (norm_mix, m_norm_mix, v_norm_mix),
             "norm_ffn2": (norm_ffn2, m_norm_ffn2, v_norm_ffn2), "norm_final": (norm_final, m_norm_final, v_norm_final),
             "conv_w": (conv_w, m_conv_w, v_conv_w), "w_gk2": (w_gk2, m_w_gk2, v_w_gk2), "b_gk": (b_gk, m_b_gk, v_b_gk),
             "gla_norm": (gla_norm, m_gla_norm, v_gla_norm)}
    names = list(small)
    flat = lambda a: a.reshape(-1, a.shape[-1])
    dl, mo, vo = _adam_smalls([flat(small[n][0]) for n in names], [g_small[n] for n in names],
                              [flat(small[n][1]) for n in names], [flat(small[n][2]) for n in names], "adam_smalls")
    for i, n in enumerate(names):
        shp = small[n][0].shape
        res[n] = [g_small[n].reshape(shp), dl[i].reshape(shp), mo[i].reshape(shp), vo[i].reshape(shp)]

    loss = tot[nv + 2 + rank * kw // d, 0]
    order = ["w_ada", "b_ada", "norm_ffn1", "w_ffn1_in", "w_ffn1_out", "norm_mix", "w_mix_in", "conv_w", "w_gk2", "b_gk", "gla_norm",
             "w_mix_out", "norm_ffn2", "w_ffn2_in", "w_ffn2_out", "norm_final"]
    return (loss, dx0[None], *[res[n][0] for n in order], *[res[n][1] for n in order], *[res[n][2] for n in order], *[res[n][3] for n in order])
```

```python
import collections
import functools

import jax
import jax.numpy as jnp
from jax import lax
from jax.experimental import pallas as pl
from jax.experimental.pallas import tpu as pltpu

F32 = jnp.float32
BF16 = jnp.bfloat16
S = jax.ShapeDtypeStruct

NDEV = 8
EPS = 1e-6
GATE_NORMALIZER = 16.0
CHUNK = 128
N_ADA = 9
ADAM_LR, ADAM_B1, ADAM_B2, ADAM_EPS, ADAM_WD, ADAM_STEP = 0.001, 0.9, 0.999, 1e-08, 0.01, 10
V7X_VMEM_LIMIT = 56 * 1024 * 1024
ROW_TILE = 512
WIDE_ROW_TILE = 1024
K_TILE = 1024
EPILOGUE_ROWS = 256
LANES = 128
BF16_ROWS = 16
PACK_ROWS = 24
ANY = pl.BlockSpec(memory_space=pl.ANY)


def _cp(*sem):
    return pltpu.CompilerParams(dimension_semantics=sem or None, vmem_limit_bytes=V7X_VMEM_LIMIT)


def _dot(a, b):
    return jnp.dot(a, b, preferred_element_type=F32)


def _dot_nt(a, b):
    return lax.dot_general(a, b, (((1,), (1,)), ((), ())), preferred_element_type=F32)


def _dot_tn(a, b):
    return lax.dot_general(a, b, (((0,), (0,)), ((), ())), preferred_element_type=F32)


def _rsum8(a):
    r, c = a.shape
    return jnp.sum(a.reshape(r // 8, 8, c), axis=0)


def _row_tile(r, c):
    for cand in (256, 128, 176, 88, 64, 32, 16, 8):
        if r % cand == 0 and cand * c * 4 <= 1024 * 1024:
            return cand
    return r


def _sigmoid(x):
    return 1.0 / (1.0 + jnp.exp(-x))


def _sigmoid_tanh(x):
    return 0.5 * jnp.tanh(0.5 * x) + 0.5


def _normmod(x, nrm, sc, sh):
    rstd = lax.rsqrt(jnp.mean(x * x, axis=-1, keepdims=True) + EPS)
    xhat = x * rstd
    return xhat, rstd, (xhat * nrm) * (1.0 + sc) + sh


def _normmod_bwd(dh, x, nrm, sc):
    rstd = lax.rsqrt(jnp.mean(x * x, axis=-1, keepdims=True) + EPS)
    xhat = x * rstd
    dxhat = dh * (nrm * (1.0 + sc))
    dx = rstd * (dxhat - xhat * jnp.mean(dxhat * xhat, axis=-1, keepdims=True))
    return dx, dh, dh * (xhat * nrm), dh * ((1.0 + sc) * xhat)


def _place():
    x, y, c = lax.axis_index("x"), lax.axis_index("y"), lax.axis_index("c")
    return x, y, c, 4 * x + 2 * y + c


def _peer(x, y, c, k):
    px = 1 - x if k & 4 else x
    py = 1 - y if k & 2 else y
    pc = 1 - c if k & 1 else c
    return (px, py, pc), 4 * px + 2 * py + pc


def _remote(src, dst, send_sem, recv_sem, peer):
    return pltpu.make_async_remote_copy(src_ref=src, dst_ref=dst, send_sem=send_sem, recv_sem=recv_sem,
                                        device_id=peer, device_id_type=pl.DeviceIdType.MESH)


_Plan = collections.namedtuple("_Plan", "inputs out_shapes sem_shapes start finish aliases", defaults=({},))


def _plan_all_to_all(xs, gather):
    n = len(xs)

    def copies(ins, outs, sems, landed):
        send_sems, recv_sems, local_sems = sems
        x, y, c, me = _place()
        local = [pltpu.make_async_copy(ins[i] if gather else ins[i].at[me], outs[i].at[me], local_sems.at[i]) for i in range(n)]
        remote = []
        for k in range(1, NDEV):
            peer, pid = _peer(x, y, c, k)
            for i in range(n):
                remote.append(_remote(ins[i] if gather else ins[i].at[pid], outs[i].at[pid if landed else me],
                                      send_sems.at[i, k - 1], recv_sems.at[i, k - 1], peer))
        return local, remote

    def start(ins, outs, sems):
        local, remote = copies(ins, outs, sems, False)
        for cp in local + remote:
            cp.start()

    def finish(ins, outs, sems):
        local, remote = copies(ins, outs, sems, True)
        for cp in remote + local:
            cp.wait()

    return _Plan(list(xs), [S((NDEV,) + a.shape, a.dtype) if gather else S(a.shape, a.dtype) for a in xs],
                 [pltpu.SemaphoreType.DMA((n, NDEV - 1)), pltpu.SemaphoreType.DMA((n, NDEV - 1)), pltpu.SemaphoreType.DMA((n,))],
                 start, finish)


def _other_chips(x, y):
    return [(1 - x, y), (x, 1 - y), (1 - x, 1 - y)]


def _plan_gather(xs, rows=None, into=None):
    n = len(xs)

    def copies(ins, outs, sems, rest):
        send_sems, recv_sems, local_sems = sems
        x, y, c, me = _place()
        sib, sib_id = (x, y, 1 - c), 4 * x + 2 * y + 1 - c
        chips = _other_chips(x, y)
        mine = lambda i: ins[i] if rows is None else ins[i].at[pl.ds(*rows)]
        slot_of = lambda i, s: outs[i].at[s] if rows is None else outs[i].at[s, pl.ds(*rows)]
        local = [pltpu.make_async_copy(mine(i), slot_of(i, me), local_sems.at[i]) for i in range(n)]
        first = [_remote(mine(i), slot_of(i, me), send_sems.at[i, 0], recv_sems.at[i, 0], sib) for i in range(n)]
        first += [_remote(mine(i), slot_of(i, me), send_sems.at[i, 1 + j], recv_sems.at[i, 1 + j], (px, py, c))
                  for j, (px, py) in enumerate(chips) for i in range(n)]
        if not rest:
            return local, first
        from_sibling = [_remote(mine(i), slot_of(i, sib_id), send_sems.at[i, 0], recv_sems.at[i, 0], sib) for i in range(n)]
        arrive, forward = [], []
        for j, (px, py) in enumerate(chips):
            s = 4 * px + 2 * py
            arrive.append([_remote(mine(i), slot_of(i, s + c), send_sems.at[i, 1 + j], recv_sems.at[i, 1 + j], (px, py, c)) for i in range(n)])
            forward.append([_remote(slot_of(i, s + c), slot_of(i, s + c), send_sems.at[i, 4 + j], recv_sems.at[i, 4 + j], sib) for i in range(n)])
            from_sibling += [_remote(mine(i), slot_of(i, s + 1 - c), send_sems.at[i, 4 + j], recv_sems.at[i, 4 + j], sib) for i in range(n)]
        return local, first, arrive, forward, from_sibling

    def start(ins, outs, sems):
        local, first = copies(ins, outs, sems, False)
        for cp in local + first:
            cp.start()

    def finish(ins, outs, sems):
        local, first, arrive, forward, from_sibling = copies(ins, outs, sems, True)
        for landed, onward in zip(arrive, forward):
            for cp in landed:
                cp.wait_recv()
            for cp in onward:
                cp.start()
        for cp in from_sibling:
            cp.wait_recv()
        for cp in first + [cp for onward in forward for cp in onward]:
            cp.wait_send()
        for cp in local:
            cp.wait()

    return _Plan(list(xs) + list(into or []), [S((NDEV,) + a.shape, a.dtype) for a in xs],
                 [pltpu.SemaphoreType.DMA((n, NDEV - 1)), pltpu.SemaphoreType.DMA((n, NDEV - 1)), pltpu.SemaphoreType.DMA((n,))],
                 start, finish, {n + i: i for i in range(len(into or []))})


def _plan_sibling_swap(gs):
    n = len(gs)

    def copies(ins, outs, sems):
        send_sems, recv_sems = sems
        x, y, c, _ = _place()
        return [_remote(ins[i].at[2 * j + 1 - c], outs[i].at[j], send_sems.at[i, j], recv_sems.at[i, j], (x, y, 1 - c))
                for i in range(n) for j in range(NDEV // 2)]

    def start(ins, outs, sems):
        for cp in copies(ins, outs, sems):
            cp.start()

    def finish(ins, outs, sems):
        for cp in copies(ins, outs, sems):
            cp.wait()

    return _Plan(list(gs), [S((NDEV // 2,) + a.shape[1:], a.dtype) for a in gs],
                 [pltpu.SemaphoreType.DMA((n, NDEV // 2)), pltpu.SemaphoreType.DMA((n, NDEV // 2))], start, finish)


def _pair_add(g, r1, core, name):
    _, r, c = g.shape
    tr = r if r * c * 2 <= 2 * 1024 * 1024 else _row_tile(r, c)

    def body(core_ref, g_ref, r_ref, o_ref):
        o_ref[...] = (g_ref[...].astype(F32) + r_ref[...].astype(F32)).astype(BF16)

    return pl.pallas_call(
        body, name=name,
        grid_spec=pltpu.PrefetchScalarGridSpec(
            num_scalar_prefetch=1, grid=(NDEV // 2, r // tr),
            in_specs=[pl.BlockSpec((None, tr, c), lambda j, k, core_ref: (2 * j + core_ref[0], k, 0)),
                      pl.BlockSpec((None, tr, c), lambda j, k, core_ref: (j, k, 0))],
            out_specs=pl.BlockSpec((None, tr, c), lambda j, k, core_ref: (j, k, 0))),
        out_shape=S((NDEV // 2, r, c), BF16), compiler_params=_cp("parallel", "parallel"),
    )(core, g, r1)


def _plan_chip_swap(ps, rows=None, into=None):
    n = len(ps)

    def copies(ins, outs, sems, landed):
        send_sems, recv_sems, local_sems = sems
        x, y, c, _ = _place()
        mine = 2 * x + y
        part = lambda ref, s: ref.at[s] if rows is None else ref.at[s, pl.ds(*rows)]
        local = [pltpu.make_async_copy(part(ins[i], mine), part(outs[i], mine), local_sems.at[i]) for i in range(n)]
        remote = [_remote(part(ins[i], 2 * px + py), part(outs[i], 2 * px + py if landed else mine),
                          send_sems.at[i, j], recv_sems.at[i, j], (px, py, c))
                  for j, (px, py) in enumerate(_other_chips(x, y)) for i in range(n)]
        return local, remote

    def start(ins, outs, sems):
        local, remote = copies(ins, outs, sems, False)
        for cp in local + remote:
            cp.start()

    def finish(ins, outs, sems):
        local, remote = copies(ins, outs, sems, True)
        for cp in remote + local:
            cp.wait()

    return _Plan(list(ps) + list(into or []), [S(a.shape, a.dtype) for a in ps],
                 [pltpu.SemaphoreType.DMA((n, 3)), pltpu.SemaphoreType.DMA((n, 3)), pltpu.SemaphoreType.DMA((n,))], start, finish,
                 {n + i: i for i in range(len(into or []))})


def _pcall(body, name, args, in_specs, out_shape, out_specs, grid=(), scratch_shapes=(), sem=(), plans=(), starts_plans=False):
    n_in, n_out, n_scr = len(args), len(out_shape), len(scratch_shapes)
    counts = [(len(p.inputs), len(p.out_shapes), len(p.sem_shapes)) for p in plans]
    c_args = [a for p in plans for a in p.inputs]
    c_outs = [s for p in plans for s in p.out_shapes]
    c_sems = [s for p in plans for s in p.sem_shapes]

    def wrapped(*refs):
        cuts = [n_in, len(c_args), n_out, len(c_outs), n_scr, len(c_sems)]
        ins, c_in, outs, c_out, scr, c_sem = [refs[sum(cuts[:k]):sum(cuts[:k + 1])] for k in range(6)]

        def halves(which):
            a = b = s = 0
            for p, (na, nb, ns) in zip(plans, counts):
                getattr(p, which)(c_in[a:a + na], c_out[b:b + nb], c_sem[s:s + ns])
                a, b, s = a + na, b + nb, s + ns

        if not plans:
            body(*ins, *outs, *scr)
        elif not grid and starts_plans:
            body(lambda: halves("start"), *ins, *outs, *scr)
            halves("finish")
        elif not grid:
            halves("start")
            body(*ins, *outs, *scr)
            halves("finish")
        else:
            first = functools.reduce(jnp.logical_and, [pl.program_id(a) == 0 for a in range(len(grid))])
            last = functools.reduce(jnp.logical_and, [pl.program_id(a) == grid[a] - 1 for a in range(len(grid))])
            pl.when(first)(lambda: halves("start"))
            body(*ins, *outs, *scr)
            pl.when(last)(lambda: halves("finish"))

    aliases, a, b = {}, n_in, n_out
    for p, (na, nb, _) in zip(plans, counts):
        aliases.update({a + k: b + v for k, v in p.aliases.items()})
        a, b = a + na, b + nb
    res = pl.pallas_call(
        wrapped, name=name, grid=grid, in_specs=list(in_specs) + [ANY] * len(c_args),
        out_shape=list(out_shape) + c_outs, out_specs=list(out_specs) + [ANY] * len(c_outs),
        scratch_shapes=list(scratch_shapes) + c_sems, input_output_aliases=aliases,
        compiler_params=_cp(*(("arbitrary",) * len(grid) if plans else sem)),
    )(*args, *c_args)
    c_res, b = [], n_out
    for _, nb, _ in counts:
        c_res.append(res[b:b + nb])
        b += nb
    return res[:n_out], c_res


def _exchange(plans, name):
    return _pcall(lambda: None, name, [], [], [], [], plans=plans)[1]


def _ada_rows(c, w_ada, b_ada, name, plans=()):
    d, cols = c.shape[1], w_ada.shape[1]
    gather_c = _plan_all_to_all([c], True)
    gather_p = _plan_all_to_all([S((NDEV, cols), F32)], True)
    n_sem = len(gather_c.sem_shapes)

    def body(start_plans, c_ref, w_ref, b_ref, ada_ref, act_ref, c_all, p_mine, p_all, *sems):
        gather_c.start([c_ref], [c_all], sems[:n_sem])
        gather_c.finish([c_ref], [c_all], sems[:n_sem])
        for s in range(NDEV):
            cc = c_all[s]
            act_ref[s:s + 1, :] = cc * _sigmoid(cc)
        p_mine[...] = _dot(act_ref[...].astype(BF16), w_ref[...].astype(BF16))
        gather_p.start([p_mine], [p_all], sems[n_sem:])
        start_plans()
        gather_p.finish([p_mine], [p_all], sems[n_sem:])
        me = _place()[3]
        for s in range(NDEV):
            ada_ref[:, s * cols:(s + 1) * cols] = p_all[s, pl.ds(me, 1), :] + b_ref[:, s * cols:(s + 1) * cols]

    whole = pl.BlockSpec(memory_space=pltpu.VMEM)
    return _pcall(body, name, [c, w_ada, b_ada], [whole] * 3, [S((1, NDEV * cols), F32), S((NDEV, d), F32)], [whole] * 2,
                  scratch_shapes=[pltpu.VMEM((NDEV,) + c.shape, F32), pltpu.VMEM((NDEV, cols), F32), pltpu.VMEM((NDEV, NDEV, cols), F32)]
                  + gather_c.sem_shapes + gather_p.sem_shapes, plans=plans, starts_plans=True)


def _ffn_in(x, ada9, nrm, w_in, sh_row, sc_row, name, plans=(), h_made=None):
    t, d = (x if h_made is None else h_made).shape
    nb, bw = w_in.shape[0] // 2, w_in.shape[1]
    tm = min(WIDE_ROW_TILE, t)

    def body(*refs):
        if h_made is None:
            x_ref, ada_ref, n_ref, wg_ref, wu_ref, h_ref, gu_ref, s_ref = refs

            @pl.when(pl.program_id(1) == 0)
            def _():
                _, _, h = _normmod(x_ref[...], n_ref[...], ada_ref[sc_row:sc_row + 1, :], ada_ref[sh_row:sh_row + 1, :])
                h_ref[...] = h.astype(BF16)
        else:
            h_ref, wg_ref, wu_ref, gu_ref, s_ref = refs

        h = h_ref[...]
        g = _dot_nt(h, wg_ref[...])
        u = _dot_nt(h, wu_ref[...])
        gu_ref[0] = g.astype(BF16)
        gu_ref[1] = u.astype(BF16)
        s_ref[...] = (g * _sigmoid(g) * u).astype(BF16)

    rows = pl.BlockSpec((tm, d), lambda i, j: (i, 0))
    weights = [pl.BlockSpec((None, bw, d), lambda i, j: (j, 0, 0)), pl.BlockSpec((None, bw, d), lambda i, j: (j + nb, 0, 0))]
    results = ([S((2, nb, t, bw), BF16), S((nb, t, bw), BF16)],
               [pl.BlockSpec((2, None, tm, bw), lambda i, j: (0, j, i, 0)), pl.BlockSpec((None, tm, bw), lambda i, j: (j, i, 0))])
    if h_made is not None:
        (gu, s), moved = _pcall(body, name, [h_made, w_in, w_in], [rows] + weights, results[0], results[1], grid=(t // tm, nb),
                                sem=("parallel", "arbitrary"), plans=plans)
        return (h_made, gu, s), moved
    return _pcall(
        body, name, [x, ada9, nrm, w_in, w_in], grid=(t // tm, nb),
        in_specs=[rows, pl.BlockSpec((N_ADA, d), lambda i, j: (0, 0)), pl.BlockSpec((1, d), lambda i, j: (0, 0))] + weights,
        out_shape=[S((t, d), BF16)] + results[0], out_specs=[rows] + results[1],
        sem=("parallel", "arbitrary"), plans=plans)


def _ffn_out(s, w_out, x, ada9, g_row, res_scale, name, plans=()):
    nb, t, bw = s.shape
    d = x.shape[1]
    tm = min(ROW_TILE, t)

    def body(s_ref, w_ref, x_ref, ada_ref, xo_ref, f_ref):
        acc = _dot(s_ref[0], w_ref[0])
        for b in range(1, nb):
            acc = acc + _dot(s_ref[b], w_ref[b])
        f_ref[...] = acc.astype(BF16)
        xo_ref[...] = x_ref[...] + (res_scale * ada_ref[g_row:g_row + 1, :]) * acc

    return _pcall(
        body, name, [s, w_out, x, ada9], grid=(t // tm,),
        in_specs=[pl.BlockSpec((nb, tm, bw), lambda i: (0, i, 0)), pl.BlockSpec((nb, bw, d), lambda i: (0, 0, 0)),
                  pl.BlockSpec((tm, d), lambda i: (i, 0)), pl.BlockSpec((N_ADA, d), lambda i: (0, 0))],
        out_shape=[S((t, d), F32), S((t, d), BF16)],
        out_specs=[pl.BlockSpec((tm, d), lambda i: (i, 0)), pl.BlockSpec((tm, d), lambda i: (i, 0))],
        sem=("parallel",), plans=plans)


def _ffn_bwd_ds(dxo, f, ada9, w_out, gu, g_row, res_scale, name, plans=()):
    t, d = dxo.shape
    nb, bw = w_out.shape[0], w_out.shape[1]
    tm = min(WIDE_ROW_TILE, t)
    ni = t // tm

    def body(dxo_ref, f_ref, ada_ref, w_ref, gu_ref, da_ref, dg_ref, dw_ref, df_ref, acc_ref):
        i, j = pl.program_id(0), pl.program_id(1)

        @pl.when((i == 0) & (j == 0))
        def _():
            dg_ref[...] = jnp.zeros_like(dg_ref)

        @pl.when(j == 0)
        def _():
            dxo_t = dxo_ref[...]
            df_ref[...] = ((res_scale * ada_ref[g_row:g_row + 1, :]) * dxo_t).astype(BF16)
            dg_ref[...] += res_scale * _rsum8(dxo_t * f_ref[...].astype(F32))

        df = df_ref[...]
        ds = _dot_nt(df, w_ref[...])
        g = gu_ref[0].astype(F32)
        u = gu_ref[1].astype(F32)
        sg = _sigmoid_tanh(g)
        silu = g * sg
        da_ref[0] = (ds * u * (sg * (1.0 + g * (1.0 - sg)))).astype(BF16)
        da_ref[1] = (ds * silu).astype(BF16)
        part = _dot_tn((silu * u).astype(BF16), df)

        @pl.when(i == 0)
        def _():
            acc_ref[j] = part

        @pl.when(i > 0)
        def _():
            acc_ref[j] += part

        @pl.when(i == ni - 1)
        def _():
            dw_ref[...] = acc_ref[j].astype(BF16)

    return _pcall(
        body, name, [dxo, f, ada9, w_out, gu], grid=(ni, nb),
        in_specs=[pl.BlockSpec((tm, d), lambda i, j: (i, 0)), pl.BlockSpec((tm, d), lambda i, j: (i, 0)),
                  pl.BlockSpec((N_ADA, d), lambda i, j: (0, 0)), pl.BlockSpec((None, bw, d), lambda i, j: (j, 0, 0)),
                  pl.BlockSpec((2, None, tm, bw), lambda i, j: (0, j, i, 0))],
        out_shape=[S((2, nb, t, bw), BF16), S((8, d), F32), S((nb, bw, d), BF16)],
        out_specs=[pl.BlockSpec((2, None, tm, bw), lambda i, j: (0, j, i, 0)), pl.BlockSpec((8, d), lambda i, j: (0, 0)),
                   pl.BlockSpec((None, bw, d), lambda i, j: (jnp.where(i == ni - 1, j, 0), 0, 0))],
        scratch_shapes=[pltpu.VMEM((tm, d), BF16), pltpu.VMEM((nb, bw, d), F32)],
        sem=("arbitrary", "arbitrary"), plans=plans)


def _ffn_bwd_dh(da, w_in, x, dxo, ada9, nrm, sh_row, sc_row, name, plans=()):
    t, d = x.shape
    nb, bw = w_in.shape[0] // 2, w_in.shape[1]
    tm = min(WIDE_ROW_TILE, t)

    def body(da_ref, wg_ref, wu_ref, x_ref, dxo_ref, ada_ref, n_ref, dx_ref, dsh_ref, dsc_ref, dn_ref, acc_ref):
        i, j = pl.program_id(0), pl.program_id(1)

        @pl.when((i == 0) & (j == 0))
        def _():
            dsh_ref[...] = jnp.zeros_like(dsh_ref)
            dsc_ref[...] = jnp.zeros_like(dsc_ref)
            dn_ref[...] = jnp.zeros_like(dn_ref)

        part = _dot(da_ref[0], wg_ref[...]) + _dot(da_ref[1], wu_ref[...])

        @pl.when(j == 0)
        def _():
            acc_ref[...] = part

        @pl.when(j > 0)
        def _():
            acc_ref[...] += part

        @pl.when(j == nb - 1)
        def _():
            for r0 in range(0, tm, min(EPILOGUE_ROWS, tm)):
                rows = slice(r0, r0 + min(EPILOGUE_ROWS, tm))
                dx, tsh, tsc, tn = _normmod_bwd(acc_ref[rows, :], x_ref[rows, :], n_ref[...], ada_ref[sc_row:sc_row + 1, :])
                dx_ref[rows, :] = dxo_ref[rows, :] + dx
                dsh_ref[...] += _rsum8(tsh)
                dsc_ref[...] += _rsum8(tsc)
                dn_ref[...] += _rsum8(tn)

    vec = pl.BlockSpec((8, d), lambda i, j: (0, 0))
    return _pcall(
        body, name, [da, w_in, w_in, x, dxo, ada9, nrm], grid=(t // tm, nb),
        in_specs=[pl.BlockSpec((2, None, tm, bw), lambda i, j: (0, j, i, 0)),
                  pl.BlockSpec((None, bw, d), lambda i, j: (j, 0, 0)), pl.BlockSpec((None, bw, d), lambda i, j: (j + nb, 0, 0)),
                  pl.BlockSpec((tm, d), lambda i, j: (i, 0)), pl.BlockSpec((tm, d), lambda i, j: (i, 0)),
                  pl.BlockSpec((N_ADA, d), lambda i, j: (0, 0)), pl.BlockSpec((1, d), lambda i, j: (0, 0))],
        out_shape=[S((t, d), F32), S((8, d), F32), S((8, d), F32), S((8, d), F32)],
        out_specs=[pl.BlockSpec((tm, d), lambda i, j: (i, 0)), vec, vec, vec],
        scratch_shapes=[pltpu.VMEM((tm, d), F32)],
        sem=("arbitrary", "arbitrary"), plans=plans)


def _tn_matmul(a, b, a_block, a_map, b_block, b_map, out_shape, out_block, out_map, nblk, name, plans=()):
    t = a.shape[-2]
    tk = min(K_TILE, t)
    nk = t // tk

    def body(a_ref, b_ref, o_ref, acc_ref):
        k = pl.program_id(1)
        for q in (range(a_ref.shape[0]) if len(a_ref.shape) == 3 else [Ellipsis]):
            part = _dot_tn(a_ref[q], b_ref[...])

            @pl.when(k == 0)
            def _():
                acc_ref[q] = part

            @pl.when(k > 0)
            def _():
                acc_ref[q] += part

        @pl.when(k == nk - 1)
        def _():
            o_ref[...] = acc_ref[...].astype(BF16)

    (out,), moved = _pcall(
        body, name, [a, b], grid=(nblk, nk),
        in_specs=[pl.BlockSpec(a_block(tk), a_map), pl.BlockSpec(b_block(tk), b_map)],
        out_shape=[S(out_shape, BF16)], out_specs=[pl.BlockSpec(out_block, out_map)],
        scratch_shapes=[pltpu.VMEM(tuple(n for n in out_block if n is not None), F32)],
        sem=("parallel", "arbitrary"), plans=plans)
    return out, moved


def _ffn_backward(dxo, x_in, h, gu, f, ada9, nrm, w_in, w_out, rows, core, name, ds_plans=(), dwin_plans=(), later=False):
    sh_row, sc_row, g_row = rows
    _, nb, t, bw = gu.shape
    d = x_in.shape[1]
    (da, dg, dw_out), ds_moved = _ffn_bwd_ds(dxo, f, ada9, w_out, gu, g_row, 0.5, name + "_ds", plans=ds_plans)
    dw_out = dw_out.reshape(NDEV, -1, d)
    dw_in, ((half_out,), *dwin_moved) = _tn_matmul(
        da.reshape(2 * nb, t, bw), h, lambda tk: (2, tk, bw), lambda sb, k: (sb, k, 0), lambda tk: (tk, d), lambda sb, k: (k, 0),
        (2 * nb, bw, d), (2, bw, d), lambda sb, k: (sb, 0, 0), nb, name + "_dwin",
        plans=[_plan_sibling_swap([dw_out])] + list(dwin_plans(ds_moved) if callable(dwin_plans) else dwin_plans))
    sum_out = _pair_add(dw_out, half_out, core, name + "_dwout_add")
    if later:
        (dx, dsh, dsc, dn), ((half_in,),) = _ffn_bwd_dh(da, w_in, x_in, dxo, ada9, nrm, sh_row, sc_row, name + "_dh",
                                                       plans=[_plan_sibling_swap([dw_in])])
        return dx, _pair_add(dw_in, half_in, core, name + "_dwin_add"), sum_out, (dsh, dsc, dg, dn), ds_moved, dwin_moved
    ((half_in,),) = _exchange([_plan_sibling_swap([dw_in])], name + "_dwin_swap")
    sum_in = _pair_add(dw_in, half_in, core, name + "_dwin_add")
    head = sum_out.shape[1] // 2
    (dx, dsh, dsc, dn), ((recv_in,), (out_begun,)) = _ffn_bwd_dh(
        da, w_in, x_in, dxo, ada9, nrm, sh_row, sc_row, name + "_dh",
        plans=[_plan_chip_swap([sum_in]), _plan_chip_swap([sum_out], rows=(0, head))])
    return dx, recv_in, (sum_out, out_begun, head), (dsh, dsc, dg, dn), ds_moved, dwin_moved


def _mix_in(x, ada9, nrm, w, widths, dts, name, plans=()):
    t, d = x.shape
    n = w.shape[0]
    tm = min(ROW_TILE, t)
    starts = [sum(widths[:i]) for i in range(len(widths))]

    def body(x_ref, ada_ref, n_ref, w_ref, h_ref, *out_refs):
        _, _, h = _normmod(x_ref[...], n_ref[...], ada_ref[4:5, :], ada_ref[3:4, :])
        hb = h.astype(BF16)
        h_ref[...] = hb
        for o_ref, st, wd in zip(out_refs, starts, widths):
            o_ref[...] = _dot_nt(hb, w_ref[st:st + wd, :]).astype(o_ref.dtype)

    return _pcall(
        body, name, [x, ada9, nrm, w], grid=(t // tm,),
        in_specs=[pl.BlockSpec((tm, d), lambda i: (i, 0)), pl.BlockSpec((N_ADA, d), lambda i: (0, 0)),
                  pl.BlockSpec((1, d), lambda i: (0, 0)), pl.BlockSpec((n, d), lambda i: (0, 0))],
        out_shape=[S((t, d), BF16)] + [S((t, wd), dt) for wd, dt in zip(widths, dts)],
        out_specs=[pl.BlockSpec((tm, d), lambda i: (i, 0))] + [pl.BlockSpec((tm, wd), lambda i: (i, 0)) for wd in widths],
        sem=("parallel",), plans=plans)


def _tri(lower):
    r = lax.broadcasted_iota(jnp.int32, (CHUNK, CHUNK), 0)
    c = lax.broadcasted_iota(jnp.int32, (CHUNK, CHUNK), 1)
    return (r >= c) if lower else (c >= r)


def _dot_01(m, x):
    hi = x.astype(BF16)
    r1 = x - hi.astype(F32)
    mid = r1.astype(BF16)
    lo = (r1 - mid.astype(F32)).astype(BF16)
    return _dot(m, hi) + _dot(m, mid) + _dot(m, lo)


def _gla_chunk_terms(q, k, lg, low01):
    b = _dot_01(low01, lg)
    bl = b[CHUNK - 1:CHUNK, :]
    r = 0.5 * bl
    eb, ebl, em, en = jnp.exp(b), jnp.exp(bl - b), jnp.exp(b - r), jnp.exp(r - b)
    return eb, ebl, em, en, jnp.exp(bl), q * eb, k * ebl, q * em, k * en


def _scores(qm_h, knp, qk1_h):
    r = lax.broadcasted_iota(jnp.int32, (CHUNK, CHUNK), 0)
    c = lax.broadcasted_iota(jnp.int32, (CHUNK, CHUNK), 1)
    p = jnp.where(r > c, _dot_nt(qm_h, knp), 0.0)
    return jnp.where(r == c, jnp.sum(qk1_h, axis=1, keepdims=True), p)


def _gla_fwd(qk, v, gl, wg, bg, heads, name, plans=()):
    t = qk.shape[0]
    kw, vw = qk.shape[1] // 2, v.shape[1]
    dk, dv = kw // heads, vw // heads
    assert dk == 64 and dv == 128 and kw % 128 == 0
    gt = min(ROW_TILE, t)
    nc = gt // CHUNK
    scale = dk ** -0.5

    def body(qk_ref, v_ref, gl_ref, wg_ref, bg_ref, o_ref, lg_ref, sall_ref, st_ref):
        @pl.when(pl.program_id(0) == 0)
        def _():
            st_ref[...] = jnp.zeros_like(st_ref)

        gk = _dot(gl_ref[...].astype(BF16), wg_ref[...]) + bg_ref[...]
        lg_ref[...] = (jnp.minimum(gk, 0.0) - jnp.log(1.0 + jnp.exp(-jnp.abs(gk)))) / GATE_NORMALIZER
        low01 = _tri(True).astype(BF16)
        lane = lax.broadcasted_iota(jnp.int32, (CHUNK, LANES), 1)

        def chunk(ci, carry):
            rows = pl.ds(pl.multiple_of(ci * CHUNK, CHUNK), CHUNK)
            q = qk_ref[rows, 0:kw] * scale
            k = qk_ref[rows, kw:2 * kw]
            qk1 = q.astype(BF16).astype(F32) * k.astype(BF16).astype(F32)
            eb, ebl, em, en, ebl_row, qe, ke, qm, kn = _gla_chunk_terms(q, k, lg_ref[rows, :], low01)
            for h in range(heads):
                lanes = slice(LANES * (h // 2), LANES * (h // 2) + LANES)
                own = (lane < 64) if h % 2 == 0 else (lane >= 64)
                knp = kn[:, lanes].astype(BF16)
                qm_h = jnp.where(own, qm[:, lanes], 0.0).astype(BF16)
                qe_h = jnp.where(own, qe[:, lanes], 0.0).astype(BF16)
                ke_h = jnp.where(own, ke[:, lanes], 0.0).astype(BF16)
                v_h = v_ref[rows, h * dv:(h + 1) * dv]
                st = st_ref[h]
                sall_ref[ci, h] = st
                p = _scores(qm_h, knp, jnp.where(own, qk1[:, lanes], 0.0))
                o_ref[rows, h * dv:(h + 1) * dv] = _dot(p.astype(BF16), v_h) + _dot_nt(qe_h, st.astype(BF16))
                st_ref[h] = st * ebl_row[:, lanes] + _dot_tn(v_h, ke_h)
            return carry

        lax.fori_loop(0, nc, chunk, 0, unroll=True)

    return _pcall(
        body, name, [qk, v, gl, wg, bg], grid=(t // gt,),
        in_specs=[pl.BlockSpec((gt, 2 * kw), lambda i: (i, 0)), pl.BlockSpec((gt, vw), lambda i: (i, 0)),
                  pl.BlockSpec((gt, LANES), lambda i: (i, 0)), pl.BlockSpec((LANES, kw), lambda i: (0, 0)),
                  pl.BlockSpec((1, kw), lambda i: (0, 0))],
        out_shape=[S((t, vw), F32), S((t, kw), F32), S((t // CHUNK, heads, dv, LANES), F32)],
        out_specs=[pl.BlockSpec((gt, vw), lambda i: (i, 0)), pl.BlockSpec((gt, kw), lambda i: (i, 0)),
                   pl.BlockSpec((nc, heads, dv, LANES), lambda i: (i, 0, 0, 0))],
        scratch_shapes=[pltpu.VMEM((heads, dv, LANES), F32)],
        sem=("arbitrary",), plans=plans)


def _gla_bwd(qk, v, lg, do, sall, gl, wg, heads, name, plans=()):
    t = qk.shape[0]
    kw, vw = qk.shape[1] // 2, v.shape[1]
    dk, dv = kw // heads, vw // heads
    gt = min(ROW_TILE, t)
    nc = gt // CHUNK
    nt = t // gt
    scale = dk ** -0.5

    def body(qk_ref, v_ref, lg_ref, do_ref, sall_ref, gl_ref, wg_ref, dqk_ref, dv_ref, dgl_ref, dwg_ref, dbg_ref, dst_ref, dgk_ref):
        @pl.when(pl.program_id(0) == 0)
        def _():
            dst_ref[...] = jnp.zeros_like(dst_ref)
            dwg_ref[...] = jnp.zeros_like(dwg_ref)
            dbg_ref[...] = jnp.zeros_like(dbg_ref)

        low01 = _tri(True).astype(BF16)
        up01 = _tri(False).astype(BF16)
        causal = _tri(True)
        lane = lax.broadcasted_iota(jnp.int32, (CHUNK, LANES), 1)
        last_row = lax.broadcasted_iota(jnp.int32, (CHUNK, kw), 0) == CHUNK - 1

        def chunk(cj, carry):
            ci = nc - 1 - cj
            rows = pl.ds(pl.multiple_of(ci * CHUNK, CHUNK), CHUNK)
            q = qk_ref[rows, 0:kw] * scale
            k = qk_ref[rows, kw:2 * kw]
            qk1 = q.astype(BF16).astype(F32) * k.astype(BF16).astype(F32)
            lgc = lg_ref[rows, :]
            eb, ebl, em, en, ebl_row, qe, ke, qm, kn = _gla_chunk_terms(q, k, lgc, low01)
            dqe, dqm, dkn, dke, drow = [], [], [], [], []
            for pr in range(kw // LANES):
                lanes = slice(LANES * pr, LANES * pr + LANES)
                knp = kn[:, lanes].astype(BF16)
                parts = []
                for half in range(2):
                    h = 2 * pr + half
                    own = (lane < 64) if half == 0 else (lane >= 64)
                    qm_h = jnp.where(own, qm[:, lanes], 0.0).astype(BF16)
                    qe_h = jnp.where(own, qe[:, lanes], 0.0).astype(BF16)
                    ke_h = jnp.where(own, ke[:, lanes], 0.0).astype(BF16)
                    v_h = v_ref[rows, h * dv:(h + 1) * dv]
                    do_h = do_ref[rows, h * dv:(h + 1) * dv]
                    st = sall_ref[ci, h]
                    dst = dst_ref[h]
                    stb, dstb = st.astype(BF16), dst.astype(BF16)
                    p = _scores(qm_h, knp, jnp.where(own, qk1[:, lanes], 0.0)).astype(BF16)
                    dp = jnp.where(causal, _dot_nt(do_h, v_h), 0.0).astype(BF16)
                    dv_ref[rows, h * dv:(h + 1) * dv] = (_dot_tn(p, do_h) + _dot_nt(ke_h, dstb)).astype(BF16)
                    parts.append((jnp.where(own, _dot(dp, knp), 0.0), _dot_tn(dp, qm_h), _dot(do_h, stb), _dot(v_h, dstb),
                                  jnp.sum(st * dst, axis=0, keepdims=True)))
                    dst_ref[h] = dst * ebl_row[:, lanes] + _dot_tn(do_h, qe_h)
                dqm.append(parts[0][0] + parts[1][0])
                dkn.append(parts[0][1] + parts[1][1])
                dqe.append(parts[0][2] + parts[1][2])
                dke.append(parts[0][3] + parts[1][3])
                drow.append(parts[0][4] + parts[1][4])
            dqm, dkn, dqe, dke, drow = [jnp.concatenate(a, axis=1) for a in (dqm, dkn, dqe, dke, drow)]
            dqk_ref[rows, 0:kw] = ((dqe * eb + dqm * em) * scale).astype(BF16)
            dqk_ref[rows, kw:2 * kw] = (dke * ebl + dkn * en).astype(BF16)
            tke = dke * ke
            db = dqe * qe + dqm * qm - dkn * kn - tke
            dbl = jnp.sum(tke, axis=0, keepdims=True) + drow * ebl_row
            db = db + jnp.where(last_row, dbl, 0.0)
            dlg = _dot_01(up01, db)
            dgk_ref[rows, :] = dlg * ((1.0 - jnp.exp(GATE_NORMALIZER * lgc)) / GATE_NORMALIZER)
            return carry

        lax.fori_loop(0, nc, chunk, 0, unroll=True)
        dgk = dgk_ref[...]
        dgkb = dgk.astype(BF16)
        dgl_ref[...] = _dot_nt(dgkb, wg_ref[...]).astype(BF16)
        dwg_ref[...] += _dot_tn(gl_ref[...].astype(BF16), dgkb)
        dbg_ref[...] += _rsum8(dgk)

    rev = lambda i: (nt - 1 - i, 0)
    return _pcall(
        body, name, [qk, v, lg, do, sall, gl, wg], grid=(nt,),
        in_specs=[pl.BlockSpec((gt, 2 * kw), rev), pl.BlockSpec((gt, vw), rev), pl.BlockSpec((gt, kw), rev),
                  pl.BlockSpec((gt, vw), rev), pl.BlockSpec((nc, heads, dv, LANES), lambda i: (nt - 1 - i, 0, 0, 0)),
                  pl.BlockSpec((gt, LANES), rev), pl.BlockSpec((LANES, kw), lambda i: (0, 0))],
        out_shape=[S((t, 2 * kw), BF16), S((t, vw), BF16), S((t, LANES), BF16), S((LANES, kw), F32), S((8, kw), F32)],
        out_specs=[pl.BlockSpec((gt, 2 * kw), rev), pl.BlockSpec((gt, vw), rev), pl.BlockSpec((gt, LANES), rev),
                   pl.BlockSpec((LANES, kw), lambda i: (0, 0)), pl.BlockSpec((8, kw), lambda i: (0, 0))],
        scratch_shapes=[pltpu.VMEM((heads, dv, LANES), F32), pltpu.VMEM((gt, kw), F32)],
        sem=("arbitrary",), plans=plans)


def _conv_taps(cx_ref, halo_ref, first, cw):
    tm = cx_ref.shape[0]
    u = cx_ref[:, cw:2 * cw].astype(F32) * cx_ref[:, 2 * cw:3 * cw].astype(F32)
    uh = halo_ref[:, cw:2 * cw].astype(F32) * halo_ref[:, 2 * cw:3 * cw].astype(F32)
    uh = jnp.where(first, 0.0, uh)
    before1, before2 = uh[BF16_ROWS - 1:BF16_ROWS, :], uh[BF16_ROWS - 2:BF16_ROWS - 1, :]
    row = lax.broadcasted_iota(jnp.int32, (tm, cw), 0)
    u1 = jnp.where(row == 0, before1, pltpu.roll(u, 1, 0))
    u2 = jnp.where(row == 0, before2, jnp.where(row == 1, before1, pltpu.roll(u, 2, 0)))
    return u, u1, u2


def _head_norm(o_h, gn):
    rstd = lax.rsqrt(jnp.mean(o_h * o_h, axis=-1, keepdims=True) + EPS)
    ohat = o_h * rstd
    return ohat, rstd, ohat * gn


def _mix_out(cx, o, go, conv_w, gn, w_out, x, ada9, nrm_next, heads, name, plans=()):
    t, d = x.shape
    cw, vw = conv_w.shape[1], o.shape[1]
    dv = vw // heads
    tm = min(ROW_TILE, t)

    def body(cx_ref, halo_ref, o_ref, go_ref, cwt_ref, gn_ref, w_ref, x_ref, ada_ref, nn_ref, xo_ref, m_ref, y_ref, hn_ref):
        u, u1, u2 = _conv_taps(cx_ref, halo_ref, pl.program_id(0) == 0, cw)
        yc = cwt_ref[0:1, :] * u2 + cwt_ref[1:2, :] * u1 + cwt_ref[2:3, :] * u
        y_ref[:, 0:cw] = (cx_ref[:, 0:cw].astype(F32) * yc).astype(BF16)
        for h in range(heads):
            cols = slice(h * dv, (h + 1) * dv)
            _, _, on = _head_norm(o_ref[:, cols], gn_ref[...])
            g = go_ref[:, cols].astype(F32)
            y_ref[:, cw + h * dv:cw + (h + 1) * dv] = (on * (g * _sigmoid(g))).astype(BF16)
        m = _dot(y_ref[...], w_ref[...])
        m_ref[...] = m.astype(BF16)
        xo = x_ref[...] + ada_ref[5:6, :] * m
        xo_ref[...] = xo
        hn_ref[...] = _normmod(xo, nn_ref[...], ada_ref[7:8, :], ada_ref[6:7, :])[2].astype(BF16)

    return _pcall(
        body, name, [cx, cx, o, go, conv_w, gn, w_out, x, ada9, nrm_next], grid=(t // tm,),
        in_specs=[pl.BlockSpec((tm, 3 * cw), lambda i: (i, 0)),
                  pl.BlockSpec((BF16_ROWS, 3 * cw), lambda i: (jnp.maximum(i * (tm // BF16_ROWS) - 1, 0), 0)),
                  pl.BlockSpec((tm, vw), lambda i: (i, 0)), pl.BlockSpec((tm, vw), lambda i: (i, 0)),
                  pl.BlockSpec((3, cw), lambda i: (0, 0)), pl.BlockSpec((1, dv), lambda i: (0, 0)),
                  pl.BlockSpec((cw + vw, d), lambda i: (0, 0)), pl.BlockSpec((tm, d), lambda i: (i, 0)),
                  pl.BlockSpec((N_ADA, d), lambda i: (0, 0)), pl.BlockSpec((1, d), lambda i: (0, 0))],
        out_shape=[S((t, d), F32), S((t, d), BF16), S((t, cw + vw), BF16), S((t, d), BF16)],
        out_specs=[pl.BlockSpec((tm, d), lambda i: (i, 0)), pl.BlockSpec((tm, d), lambda i: (i, 0)),
                   pl.BlockSpec((tm, cw + vw), lambda i: (i, 0)), pl.BlockSpec((tm, d), lambda i: (i, 0))],
        sem=("parallel",), plans=plans)


def _mix_bwd_a(dxo, m, ycat, ada9, w_out, cx, o, go, conv_w, gn, heads, name, plans=()):
    t, d = dxo.shape
    cw, vw = conv_w.shape[1], o.shape[1]
    dv = vw // heads
    tm = min(ROW_TILE, t)
    nt = t // tm

    def body(dxo_ref, m_ref, y_ref, ada_ref, w_ref, cx_ref, halo_ref, o_ref, go_ref, cwt_ref, gn_ref,
             dw_ref, dyc_ref, dcb_ref, do_ref, dgo_ref, dg_ref, dcw_ref, dgn_ref, acc_ref):
        @pl.when(pl.program_id(0) == 0)
        def _():
            dg_ref[...] = jnp.zeros_like(dg_ref)
            dcw_ref[...] = jnp.zeros_like(dcw_ref)
            dgn_ref[...] = jnp.zeros_like(dgn_ref)

        dxo_t = dxo_ref[...]
        dmb = (ada_ref[5:6, :] * dxo_t).astype(BF16)
        part = _dot_tn(y_ref[...], dmb)

        @pl.when(pl.program_id(0) == 0)
        def _():
            acc_ref[...] = part

        @pl.when(pl.program_id(0) > 0)
        def _():
            acc_ref[...] += part

        @pl.when(pl.program_id(0) == nt - 1)
        def _():
            dw_ref[...] = acc_ref[...].astype(BF16)

        dg_ref[...] += _rsum8(dxo_t * m_ref[...].astype(F32))
        dy = _dot_nt(dmb, w_ref[...])
        u, u1, u2 = _conv_taps(cx_ref, halo_ref, pl.program_id(0) == 0, cw)
        yc = cwt_ref[0:1, :] * u2 + cwt_ref[1:2, :] * u1 + cwt_ref[2:3, :] * u
        dyv = dy[:, 0:cw]
        dcb_ref[...] = (dyv * yc).astype(BF16)
        dyc = dyv * cx_ref[:, 0:cw].astype(F32)
        dyc_ref[...] = dyc
        dcw_ref[0] += _rsum8(dyc * u2)
        dcw_ref[1] += _rsum8(dyc * u1)
        dcw_ref[2] += _rsum8(dyc * u)
        for h in range(heads):
            cols = slice(h * dv, (h + 1) * dv)
            ohat, rstd, on = _head_norm(o_ref[:, cols], gn_ref[...])
            g = go_ref[:, cols].astype(F32)
            sg = _sigmoid_tanh(g)
            dyg = dy[:, cw + h * dv:cw + (h + 1) * dv]
            dgo_ref[:, cols] = (dyg * on * (sg * (1.0 + g * (1.0 - sg)))).astype(BF16)
            don = dyg * (g * sg)
            dgn_ref[...] += _rsum8(don * ohat)
            tt = don * gn_ref[...]
            do_ref[:, cols] = (rstd * (tt - ohat * jnp.mean(tt * ohat, axis=-1, keepdims=True))).astype(BF16)

    return _pcall(
        body, name, [dxo, m, ycat, ada9, w_out, cx, cx, o, go, conv_w, gn], grid=(nt,),
        in_specs=[pl.BlockSpec((tm, d), lambda i: (i, 0)), pl.BlockSpec((tm, d), lambda i: (i, 0)),
                  pl.BlockSpec((tm, cw + vw), lambda i: (i, 0)),
                  pl.BlockSpec((N_ADA, d), lambda i: (0, 0)), pl.BlockSpec((cw + vw, d), lambda i: (0, 0)),
                  pl.BlockSpec((tm, 3 * cw), lambda i: (i, 0)),
                  pl.BlockSpec((BF16_ROWS, 3 * cw), lambda i: (jnp.maximum(i * (tm // BF16_ROWS) - 1, 0), 0)),
                  pl.BlockSpec((tm, vw), lambda i: (i, 0)), pl.BlockSpec((tm, vw), lambda i: (i, 0)),
                  pl.BlockSpec((3, cw), lambda i: (0, 0)), pl.BlockSpec((1, dv), lambda i: (0, 0))],
        out_shape=[S((cw + vw, d), BF16), S((t, cw), F32), S((t, cw), BF16), S((t, vw), BF16), S((t, vw), BF16),
                   S((8, d), F32), S((3, 8, cw), F32), S((8, dv), F32)],
        out_specs=[pl.BlockSpec((cw + vw, d), lambda i: (0, 0)), pl.BlockSpec((tm, cw), lambda i: (i, 0)),
                   pl.BlockSpec((tm, cw), lambda i: (i, 0)), pl.BlockSpec((tm, vw), lambda i: (i, 0)),
                   pl.BlockSpec((tm, vw), lambda i: (i, 0)), pl.BlockSpec((8, d), lambda i: (0, 0)),
                   pl.BlockSpec((3, 8, cw), lambda i: (0, 0, 0)), pl.BlockSpec((8, dv), lambda i: (0, 0))],
        scratch_shapes=[pltpu.VMEM((cw + vw, d), F32)],
        sem=("arbitrary",), plans=plans)


def _mix_bwd_b(dyc, cx, dcb, dqk, dvv, dgo, dgl, conv_w, w, x, dxo, ada9, nrm, name, plans=()):
    t, d = x.shape
    cw = conv_w.shape[1]
    n = w.shape[0]
    tm = min(ROW_TILE, t)
    nt = t // tm
    pieces = [dcb.shape[1], cw, cw, dqk.shape[1], dvv.shape[1], dgo.shape[1], dgl.shape[1]]
    assert sum(pieces) == n

    def body(dyc_ref, nxt_ref, cx_ref, dcb_ref, dqk_ref, dv_ref, dgo_ref, dgl_ref, cwt_ref, w_ref, x_ref, dxo_ref, ada_ref, n_ref,
             dx_ref, dp_ref, dsh_ref, dsc_ref, dn_ref):
        i = pl.program_id(0)

        @pl.when(i == 0)
        def _():
            dsh_ref[...] = jnp.zeros_like(dsh_ref)
            dsc_ref[...] = jnp.zeros_like(dsc_ref)
            dn_ref[...] = jnp.zeros_like(dn_ref)

        dyc_t = dyc_ref[...]
        nxt = jnp.where(i == nt - 1, 0.0, nxt_ref[...])
        row = lax.broadcasted_iota(jnp.int32, (tm, cw), 0)
        d1 = jnp.where(row == tm - 1, nxt[0:1, :], pltpu.roll(dyc_t, tm - 1, 0))
        d2 = jnp.where(row == tm - 2, nxt[0:1, :], jnp.where(row == tm - 1, nxt[1:2, :], pltpu.roll(dyc_t, tm - 2, 0)))
        du = cwt_ref[2:3, :] * dyc_t + cwt_ref[1:2, :] * d1 + cwt_ref[0:1, :] * d2
        c0 = 0
        dp_ref[:, c0:c0 + cw] = dcb_ref[...]
        dp_ref[:, cw:2 * cw] = (du * cx_ref[:, 2 * cw:3 * cw].astype(F32)).astype(BF16)
        dp_ref[:, 2 * cw:3 * cw] = (du * cx_ref[:, cw:2 * cw].astype(F32)).astype(BF16)
        c0 = 3 * cw
        for ref in (dqk_ref, dv_ref, dgo_ref, dgl_ref):
            wd = ref.shape[1]
            dp_ref[:, c0:c0 + wd] = ref[...]
            c0 += wd
        dh = _dot(dp_ref[...], w_ref[...])
        dx, tsh, tsc, tn = _normmod_bwd(dh, x_ref[...], n_ref[...], ada_ref[4:5, :])
        dx_ref[...] = dxo_ref[...] + dx
        dsh_ref[...] += _rsum8(tsh)
        dsc_ref[...] += _rsum8(tsc)
        dn_ref[...] += _rsum8(tn)

    row_spec = lambda wd: pl.BlockSpec((tm, wd), lambda i: (i, 0))
    vec = pl.BlockSpec((8, d), lambda i: (0, 0))
    return _pcall(
        body, name, [dyc, dyc, cx, dcb, dqk, dvv, dgo, dgl, conv_w, w, x, dxo, ada9, nrm], grid=(nt,),
        in_specs=[row_spec(cw), pl.BlockSpec((8, cw), lambda i: (jnp.minimum((i + 1) * (tm // 8), t // 8 - 1), 0)),
                  row_spec(3 * cw), row_spec(cw), row_spec(dqk.shape[1]), row_spec(dvv.shape[1]), row_spec(dgo.shape[1]),
                  row_spec(dgl.shape[1]), pl.BlockSpec((3, cw), lambda i: (0, 0)), pl.BlockSpec((n, d), lambda i: (0, 0)),
                  row_spec(d), row_spec(d), pl.BlockSpec((N_ADA, d), lambda i: (0, 0)), pl.BlockSpec((1, d), lambda i: (0, 0))],
        out_shape=[S((t, d), F32), S((t, n), BF16), S((8, d), F32), S((8, d), F32), S((8, d), F32)],
        out_specs=[row_spec(d), row_spec(n), vec, vec, vec],
        sem=("arbitrary",), plans=plans)


def _ffn_out_loss(s, w_out, x, ada9, g_row, res_scale, target, nrm, name):
    nb, t, bw = s.shape
    d = x.shape[1]
    tm = min(ROW_TILE, t)
    nt = t // tm

    def body(s_ref, w_ref, x_ref, ada_ref, tg_ref, n_ref, f_ref, loss_ref, dx_ref, dn_ref, acc_ref):
        i = pl.program_id(0)

        @pl.when(i == 0)
        def _():
            acc_ref[...] = jnp.zeros_like(acc_ref)
            dn_ref[...] = jnp.zeros_like(dn_ref)

        f = _dot(s_ref[0], w_ref[0])
        for b in range(1, nb):
            f = f + _dot(s_ref[b], w_ref[b])
        f_ref[...] = f.astype(BF16)
        xt = x_ref[...] + (res_scale * ada_ref[g_row:g_row + 1, :]) * f
        rstd = lax.rsqrt(jnp.mean(xt * xt, axis=-1, keepdims=True) + EPS)
        xhat = xt * rstd
        err = xhat * n_ref[...] - tg_ref[...]
        acc_ref[...] += _rsum8(err * err)
        dy = err * (1.0 / d)
        dn_ref[...] += _rsum8(dy * xhat)
        dxhat = dy * n_ref[...]
        dx_ref[...] = rstd * (dxhat - xhat * jnp.mean(dxhat * xhat, axis=-1, keepdims=True))

        @pl.when(i == nt - 1)
        def _():
            loss_ref[...] = jnp.full(loss_ref.shape, (0.5 / d) * jnp.sum(acc_ref[...]), F32)

    return pl.pallas_call(
        body, name=name, grid=(nt,),
        in_specs=[pl.BlockSpec((nb, tm, bw), lambda i: (0, i, 0)), pl.BlockSpec((nb, bw, d), lambda i: (0, 0, 0)),
                  pl.BlockSpec((tm, d), lambda i: (i, 0)), pl.BlockSpec((N_ADA, d), lambda i: (0, 0)),
                  pl.BlockSpec((tm, d), lambda i: (i, 0)), pl.BlockSpec((1, d), lambda i: (0, 0))],
        out_shape=[S((t, d), BF16), S((1, LANES), F32), S((t, d), F32), S((8, d), F32)],
        out_specs=[pl.BlockSpec((tm, d), lambda i: (i, 0)), pl.BlockSpec((1, LANES), lambda i: (0, 0)),
                   pl.BlockSpec((tm, d), lambda i: (i, 0)), pl.BlockSpec((8, d), lambda i: (0, 0))],
        scratch_shapes=[pltpu.VMEM((8, d), F32)],
        compiler_params=_cp("arbitrary"),
    )(s, w_out, x, ada9, target, nrm)


def _pack_smalls(vec_parts, dcw, dbg, dgn, dwg, loss_v, rank, name):
    d = vec_parts[0].shape[1]
    cw, kw, dv = dcw.shape[2], dbg.shape[1], dgn.shape[1]
    nv = len(vec_parts)
    loss_row = nv + 2 + rank * kw // d
    assert 2 * cw == d and cw + kw + dv <= d and (rank * kw) % d == 0 and loss_row < PACK_ROWS
    per_row = d // kw

    def body(*refs):
        vrefs, (dcw_ref, dbg_ref, dgn_ref, dwg_ref, loss_ref, o_ref) = refs[:nv], refs[nv:]
        o_ref[...] = jnp.zeros_like(o_ref)
        o_ref[loss_row:loss_row + 1, 0:loss_ref.shape[1]] = loss_ref[...]
        for r, ref in enumerate(vrefs):
            o_ref[r:r + 1, :] = jnp.sum(ref[...], axis=0, keepdims=True)
        o_ref[nv:nv + 1, 0:cw] = jnp.sum(dcw_ref[0], axis=0, keepdims=True)
        o_ref[nv:nv + 1, cw:2 * cw] = jnp.sum(dcw_ref[1], axis=0, keepdims=True)
        o_ref[nv + 1:nv + 2, 0:cw] = jnp.sum(dcw_ref[2], axis=0, keepdims=True)
        o_ref[nv + 1:nv + 2, cw:cw + kw] = jnp.sum(dbg_ref[...], axis=0, keepdims=True)
        o_ref[nv + 1:nv + 2, cw + kw:cw + kw + dv] = jnp.sum(dgn_ref[...], axis=0, keepdims=True)
        for r in range(rank):
            o_ref[nv + 2 + r // per_row:nv + 3 + r // per_row, (r % per_row) * kw:(r % per_row + 1) * kw] = dwg_ref[r:r + 1, :]

    return pl.pallas_call(body, name=name, out_shape=S((PACK_ROWS, d), F32), compiler_params=_cp())(*vec_parts, dcw, dbg, dgn, dwg, loss_v)


def _sum_slots(a, name):
    def body(a_ref, o_ref):
        acc = a_ref[0]
        for s in range(1, NDEV):
            acc = acc + a_ref[s]
        o_ref[...] = acc

    return pl.pallas_call(body, name=name, out_shape=S(a.shape[1:], F32), compiler_params=_cp())(a)


def _adamw(w, g, m, v):
    m = ADAM_B1 * m + (1.0 - ADAM_B1) * g
    v = ADAM_B2 * v + (1.0 - ADAM_B2) * (g * g)
    m_hat = m / (1.0 - ADAM_B1 ** ADAM_STEP)
    v_hat = v / (1.0 - ADAM_B2 ** ADAM_STEP)
    return -ADAM_LR * (m_hat / (jnp.sqrt(v_hat) + ADAM_EPS) + ADAM_WD * w), m, v


def _adam_slots(recv, w, m, v, name):
    r, c = w.shape
    slots = recv.shape[0]
    tr = _row_tile(r, c)

    def body(recv_ref, w_ref, m_ref, v_ref, g_ref, d_ref, mo_ref, vo_ref):
        g = recv_ref[0].astype(F32)
        for s in range(1, slots):
            g = g + recv_ref[s].astype(F32)
        g_ref[...] = g
        d_ref[...], mo_ref[...], vo_ref[...] = _adamw(w_ref[...], g, m_ref[...], v_ref[...])

    blk = pl.BlockSpec((tr, c), lambda i: (i, 0))
    return pl.pallas_call(
        body, name=name, grid=(r // tr,),
        in_specs=[pl.BlockSpec((slots, tr, c), lambda i: (0, i, 0)), blk, blk, blk],
        out_shape=[S((r, c), F32)] * 4, out_specs=[blk] * 4, compiler_params=_cp("parallel"),
    )(recv, w, m, v)


def _adam_w_ada(act_t, dada, w, m, v, name):
    r, c = w.shape
    tr = 128
    nb = act_t.shape[1]

    def body(a_ref, da_ref, w_ref, m_ref, v_ref, g_ref, d_ref, mo_ref, vo_ref):
        g = a_ref[:, 0:1] * da_ref[0:1, :]
        for b in range(1, nb):
            g = g + a_ref[:, b:b + 1] * da_ref[b:b + 1, :]
        g_ref[...] = g
        d_ref[...], mo_ref[...], vo_ref[...] = _adamw(w_ref[...], g, m_ref[...], v_ref[...])

    blk = pl.BlockSpec((tr, c), lambda i: (i, 0))
    return pl.pallas_call(
        body, name=name, grid=(r // tr,),
        in_specs=[pl.BlockSpec((tr, nb), lambda i: (i, 0)), pl.BlockSpec((nb, c), lambda i: (0, 0)), blk, blk, blk],
        out_shape=[S((r, c), F32)] * 4, out_specs=[blk] * 4, compiler_params=_cp("parallel"),
    )(act_t, dada, w, m, v)


def _adam_smalls(ws, gs, ms, vs, name):
    n = len(ws)

    def body(*refs):
        w_r, g_r, m_r, v_r = (refs[k * n:(k + 1) * n] for k in range(4))
        d_o, m_o, v_o = (refs[(4 + k) * n:(5 + k) * n] for k in range(3))
        for i in range(n):
            d_o[i][...], m_o[i][...], v_o[i][...] = _adamw(w_r[i][...], g_r[i][...], m_r[i][...], v_r[i][...])

    shapes = [S(w.shape, F32) for w in ws]
    outs = pl.pallas_call(body, name=name, out_shape=shapes * 3, compiler_params=_cp())(*ws, *gs, *ms, *vs)
    return outs[:n], outs[n:2 * n], outs[2 * n:]


def kernel(x, c, w_ada, b_ada, norm_ffn1, w_ffn1_in, w_ffn1_out, norm_mix, w_mix_in, conv_w, w_gk2, b_gk, gla_norm, w_mix_out, norm_ffn2, w_ffn2_in, w_ffn2_out, norm_final, loss_target, m_w_ada, m_b_ada, m_norm_ffn1, m_w_ffn1_in, m_w_ffn1_out, m_norm_mix, m_w_mix_in, m_conv_w, m_w_gk2, m_b_gk, m_gla_norm, m_w_mix_out, m_norm_ffn2, m_w_ffn2_in, m_w_ffn2_out, m_norm_final, v_w_ada, v_b_ada, v_norm_ffn1, v_w_ffn1_in, v_w_ffn1_out, v_norm_mix, v_w_mix_in, v_conv_w, v_w_gk2, v_b_gk, v_gla_norm, v_w_mix_out, v_norm_ffn2, v_w_ffn2_in, v_w_ffn2_out, v_norm_final):
    t, d = x.shape[1], x.shape[2]
    x0, tgt = x[0], loss_target[0]
    rank, kw = w_gk2.shape[1], w_gk2.shape[2] * NDEV
    cw = conv_w.shape[2] * NDEV
    dv = gla_norm.shape[1]
    vw = d - cw
    heads = vw // dv
    mix_cols = w_mix_in.shape[2]
    widths = [3 * cw, 2 * kw, vw, vw, LANES]
    n_proj = 3 * cw + 2 * kw + 2 * vw + rank
    assert n_proj == mix_cols * NDEV and rank <= LANES
    me = 4 * lax.axis_index("x") + 2 * lax.axis_index("y") + lax.axis_index("c")

    core = lax.axis_index("c").astype(jnp.int32).reshape(1)
    bf = lambda a: a[0].astype(BF16)
    bft = lambda a: a[0].T.astype(BF16)
    nb = NDEV // 2

    (ada_row, act_all), ((w1i, cwt_all, wg_all),) = _ada_rows(
        c, w_ada[0], b_ada, "ada_rows", plans=[_plan_gather([bft(w_ffn1_in), conv_w[0], w_gk2[0]])])
    ada9 = ada_row.reshape(N_ADA, d)
    cwt = cwt_all.transpose(1, 0, 2).reshape(conv_w.shape[1], cw)
    wg = jnp.pad(wg_all.transpose(1, 0, 2).reshape(rank, kw), ((0, LANES - rank), (0, 0))).astype(BF16)

    (h1, gu1, s1), ((w1o, wmi),) = _ffn_in(x0, ada9, norm_ffn1, w1i, 0, 1, "ffn1_in", plans=[_plan_gather([bf(w_ffn1_out), bft(w_mix_in)])])
    w1o = w1o.reshape(nb, -1, d)
    wmi = jnp.pad(wmi.reshape(n_proj, d), ((0, sum(widths) - n_proj), (0, 0)))
    w2i_mine = bft(w_ffn2_in)
    quarter = w2i_mine.shape[0] // 4
    part = lambda k, into=None: _plan_gather([w2i_mine], rows=(k * quarter, quarter), into=into)
    (x1, f1), ((wmo,), (w2i,)) = _ffn_out(s1, w1o, x0, ada9, 2, 0.5, "ffn1_out", plans=[_plan_gather([bf(w_mix_out)]), part(0)])
    wmo = wmo.reshape(cw + vw, d)
    (h2, cx, qk, vv, go, gl), ((w2i,),) = _mix_in(x1, ada9, norm_mix, wmi, widths, [BF16, F32, BF16, BF16, F32], "mix_in",
                                                 plans=[part(1, [w2i])])
    (o, lg, sall), ((w2i,),) = _gla_fwd(qk, vv, gl, wg, b_gk, heads, "gla_fwd", plans=[part(2, [w2i])])
    (x2, mm, ycat, h3), ((w2i,),) = _mix_out(cx, o, go, cwt, gla_norm, wmo, x1, ada9, norm_ffn2, heads, "mix_out", plans=[part(3, [w2i])])
    (h3, gu3, s3), ((w2o,),) = _ffn_in(None, None, None, w2i, 6, 7, "ffn2_in", plans=[_plan_gather([bf(w_ffn2_out)])], h_made=h3)
    w2o = w2o.reshape(nb, -1, d)
    f3, loss_v, dx3, dnf = _ffn_out_loss(s3, w2o, x2, ada9, 8, 0.5, tgt, norm_final.reshape(1, d), "ffn2_out_loss")

    dx2, sum2i, sum2o, (dsh3, dsc3, dg3, dn3), _, _ = _ffn_backward(
        dx3, x2, h3, gu3, f3, ada9, norm_ffn2, w2i, w2o, (6, 7, 8), core, "ffn2_bwd", later=True)
    (dwmo, dyc, dcb, do, dgo, dg2, dcw, dgn), ((r2o,),) = _mix_bwd_a(dx2, mm, ycat, ada9, wmo, cx, o, go, cwt, gla_norm, heads, "mix_bwd_a",
                                                                   plans=[_plan_chip_swap([sum2o])])
    half = sum2i.shape[1] // 2
    (dqk, dvv, dgl, dwg, dbg), ((r2i,),) = _gla_bwd(qk, vv, lg, do, sall, gl, wg, heads, "gla_bwd",
                                                   plans=[_plan_chip_swap([sum2i], rows=(0, half))])
    (dx1, dproj, dsh2, dsc2, dnm), ((r2i,),) = _mix_bwd_b(dyc, cx, dcb, dqk, dvv, dgo, dgl, cwt, wmi, x1, dx2, ada9, norm_mix, "mix_bwd_b",
                                                         plans=[_plan_chip_swap([sum2i], rows=(half, half), into=[r2i])])
    n_pad = sum(widths)
    tn = n_pad // 5
    dwmi, _ = _tn_matmul(dproj, h2, lambda tk: (tk, tn), lambda sb, k: (k, sb), lambda tk: (tk, d), lambda sb, k: (k, 0),
                         (n_pad, d), (tn, d), lambda sb, k: (sb, 0), 5, "mix_dwin")
    dwmi = dwmi[:n_proj].reshape(NDEV, mix_cols, d)
    dwmo = dwmo.reshape(NDEV, -1, d)
    dx0, r1i, (sum1o, r1o, head), (dsh1, dsc1, dg1, dn1), _, ((rmi, rmo),) = _ffn_backward(
        dx1, x0, h1, gu1, f1, ada9, norm_ffn1, w1i, w1o, (0, 1, 2), core, "ffn1_bwd",
        ds_plans=[_plan_sibling_swap([dwmi, dwmo])],
        dwin_plans=lambda moved: [_plan_chip_swap([_pair_add(dwmi, moved[0][0], core, "mix_dwin_add"),
                                                   _pair_add(dwmo, moved[0][1], core, "mix_dwout_add")])])
    pack = _pack_smalls([dn1, dnm, dn3, dnf, dsh1, dsc1, dg1, dsh2, dsc2, dg2, dsh3, dsc3, dg3], dcw, dbg, dgn, dwg, loss_v, rank, "pack_smalls")
    (r1o,), (pack_all,) = _exchange([_plan_chip_swap([sum1o], rows=(head, sum1o.shape[1] - head), into=[r1o]),
                                     _plan_all_to_all([pack], True)], "grads_last")
    tot = _sum_slots(pack_all, "sum_smalls")

    res = {}
    for nm, recv, w, m, v in (("w_ffn1_out", r1o, w_ffn1_out, m_w_ffn1_out, v_w_ffn1_out), ("w_mix_out", rmo, w_mix_out, m_w_mix_out, v_w_mix_out),
                              ("w_ffn2_out", r2o, w_ffn2_out, m_w_ffn2_out, v_w_ffn2_out)):
        res[nm] = [a[None] for a in _adam_slots(recv, w[0], m[0], v[0], "adam_" + nm)]
    for nm, recv, w, m, v in (("w_ffn1_in", r1i, w_ffn1_in, m_w_ffn1_in, v_w_ffn1_in), ("w_mix_in", rmi, w_mix_in, m_w_mix_in, v_w_mix_in),
                              ("w_ffn2_in", r2i, w_ffn2_in, m_w_ffn2_in, v_w_ffn2_in)):
        res[nm] = [a.T[None] for a in _adam_slots(recv, w[0].T, m[0].T, v[0].T, "adam_" + nm)]

    cols_ada = w_ada.shape[2]
    dada_all = pack_all[:, 4:4 + N_ADA, :].reshape(NDEV, N_ADA * d)
    dada_mine = lax.dynamic_slice_in_dim(dada_all, me * cols_ada, cols_ada, axis=1)
    res["w_ada"] = [a[None] for a in _adam_w_ada(act_all.T, dada_mine, w_ada[0], m_w_ada[0], v_w_ada[0], "adam_w_ada")]

    nv = 4 + N_ADA
    g_small = {
        "b_ada": tot[4:nv].reshape(1, N_ADA * d),
        "norm_ffn1": tot[0:1], "norm_mix": tot[1:2], "norm_ffn2": tot[2:3], "norm_final": tot[3:4],
        "conv_w": lax.dynamic_slice_in_dim(
            jnp.concatenate([tot[nv:nv + 1, 0:cw], tot[nv:nv + 1, cw:2 * cw], tot[nv + 1:nv + 2, 0:cw]], axis=0), me * (cw // NDEV), cw // NDEV, axis=1),
        "w_gk2": lax.dynamic_slice_in_dim(tot[nv + 2:nv + 2 + rank * kw // d].reshape(rank, kw), me * (kw // NDEV), kw // NDEV, axis=1),
        "b_gk": tot[nv + 1:nv + 2, cw:cw + kw],
        "gla_norm": tot[nv + 1:nv + 2, cw + kw:cw + kw + dv],
    }
    small = {"b_ada": (b_ada, m_b_ada, v_b_ada), "norm_ffn1": (norm_ffn1, m_norm_ffn1, v_norm_ffn1), "norm_mix": (norm_mix, m_norm_mix, v_norm_mix),
             "norm_ffn2": (norm_ffn2, m_norm_ffn2, v_norm_ffn2), "norm_final": (norm_final, m_norm_final, v_norm_final),
             "conv_w": (conv_w, m_conv_w, v_conv_w), "w_gk2": (w_gk2, m_w_gk2, v_w_gk2), "b_gk": (b_gk, m_b_gk, v_b_gk),
             "gla_norm": (gla_norm, m_gla_norm, v_gla_norm)}
    names = list(small)
    flat = lambda a: a.reshape(-1, a.shape[-1])
    dl, mo, vo = _adam_smalls([flat(small[n][0]) for n in names], [g_small[n] for n in names],
                              [flat(small[n][1]) for n in names], [flat(small[n][2]) for n in names], "adam_smalls")
    for i, n in enumerate(names):
        shp = small[n][0].shape
        res[n] = [g_small[n].reshape(shp), dl[i].reshape(shp), mo[i].reshape(shp), vo[i].reshape(shp)]

    loss = tot[nv + 2 + rank * kw // d, 0]
    order = ["w_ada", "b_ada", "norm_ffn1", "w_ffn1_in", "w_ffn1_out", "norm_mix", "w_mix_in", "conv_w", "w_gk2", "b_gk", "gla_norm",
             "w_mix_out", "norm_ffn2", "w_ffn2_in", "w_ffn2_out", "norm_final"]
    return (loss, dx0[None], *[res[n][0] for n in order], *[res[n][1] for n in order], *[res[n][2] for n in order], *[res[n][3] for n in order])
```

```python
import collections
import functools

import jax
import jax.numpy as jnp
from jax import lax
from jax.experimental import pallas as pl
from jax.experimental.pallas import tpu as pltpu

F32 = jnp.float32
BF16 = jnp.bfloat16
S = jax.ShapeDtypeStruct

NDEV = 8
EPS = 1e-6
GATE_NORMALIZER = 16.0
CHUNK = 128
N_ADA = 9
ADAM_LR, ADAM_B1, ADAM_B2, ADAM_EPS, ADAM_WD, ADAM_STEP = 0.001, 0.9, 0.999, 1e-08, 0.01, 10
V7X_VMEM_LIMIT = 56 * 1024 * 1024
ROW_TILE = 512
WIDE_ROW_TILE = 1024
K_TILE = 1024
EPILOGUE_ROWS = 256
LANES = 128
BF16_ROWS = 16
PACK_ROWS = 24
ANY = pl.BlockSpec(memory_space=pl.ANY)


def _cp(*sem):
    return pltpu.CompilerParams(dimension_semantics=sem or None, vmem_limit_bytes=V7X_VMEM_LIMIT)


def _dot(a, b):
    return jnp.dot(a, b, preferred_element_type=F32)


def _dot_nt(a, b):
    return lax.dot_general(a, b, (((1,), (1,)), ((), ())), preferred_element_type=F32)


def _dot_tn(a, b):
    return lax.dot_general(a, b, (((0,), (0,)), ((), ())), preferred_element_type=F32)


def _rsum8(a):
    r, c = a.shape
    return jnp.sum(a.reshape(r // 8, 8, c), axis=0)


def _row_tile(r, c):
    for cand in (256, 128, 176, 88, 64, 32, 16, 8):
        if r % cand == 0 and cand * c * 4 <= 1024 * 1024:
            return cand
    return r


def _sigmoid(x):
    return 1.0 / (1.0 + jnp.exp(-x))


def _sigmoid_tanh(x):
    return 0.5 * jnp.tanh(0.5 * x) + 0.5


def _normmod(x, nrm, sc, sh):
    rstd = lax.rsqrt(jnp.mean(x * x, axis=-1, keepdims=True) + EPS)
    xhat = x * rstd
    return xhat, rstd, (xhat * nrm) * (1.0 + sc) + sh


def _normmod_bwd(dh, x, nrm, sc):
    rstd = lax.rsqrt(jnp.mean(x * x, axis=-1, keepdims=True) + EPS)
    xhat = x * rstd
    dxhat = dh * (nrm * (1.0 + sc))
    dx = rstd * (dxhat - xhat * jnp.mean(dxhat * xhat, axis=-1, keepdims=True))
    return dx, dh, dh * (xhat * nrm), dh * ((1.0 + sc) * xhat)


def _place():
    x, y, c = lax.axis_index("x"), lax.axis_index("y"), lax.axis_index("c")
    return x, y, c, 4 * x + 2 * y + c


def _peer(x, y, c, k):
    px = 1 - x if k & 4 else x
    py = 1 - y if k & 2 else y
    pc = 1 - c if k & 1 else c
    return (px, py, pc), 4 * px + 2 * py + pc


def _remote(src, dst, send_sem, recv_sem, peer):
    return pltpu.make_async_remote_copy(src_ref=src, dst_ref=dst, send_sem=send_sem, recv_sem=recv_sem,
                                        device_id=peer, device_id_type=pl.DeviceIdType.MESH)


_Plan = collections.namedtuple("_Plan", "inputs out_shapes sem_shapes start finish aliases", defaults=({},))


def _plan_all_to_all(xs, gather):
    n = len(xs)

    def copies(ins, outs, sems, landed):
        send_sems, recv_sems, local_sems = sems
        x, y, c, me = _place()
        local = [pltpu.make_async_copy(ins[i] if gather else ins[i].at[me], outs[i].at[me], local_sems.at[i]) for i in range(n)]
        remote = []
        for k in range(1, NDEV):
            peer, pid = _peer(x, y, c, k)
            for i in range(n):
                remote.append(_remote(ins[i] if gather else ins[i].at[pid], outs[i].at[pid if landed else me],
                                      send_sems.at[i, k - 1], recv_sems.at[i, k - 1], peer))
        return local, remote

    def start(ins, outs, sems):
        local, remote = copies(ins, outs, sems, False)
        for cp in local + remote:
            cp.start()

    def finish(ins, outs, sems):
        local, remote = copies(ins, outs, sems, True)
        for cp in remote + local:
            cp.wait()

    return _Plan(list(xs), [S((NDEV,) + a.shape, a.dtype) if gather else S(a.shape, a.dtype) for a in xs],
                 [pltpu.SemaphoreType.DMA((n, NDEV - 1)), pltpu.SemaphoreType.DMA((n, NDEV - 1)), pltpu.SemaphoreType.DMA((n,))],
                 start, finish)


def _other_chips(x, y):
    return [(1 - x, y), (x, 1 - y), (1 - x, 1 - y)]


def _plan_gather(xs, rows=None, into=None):
    n = len(xs)

    def copies(ins, outs, sems, rest):
        send_sems, recv_sems, local_sems = sems
        x, y, c, me = _place()
        sib, sib_id = (x, y, 1 - c), 4 * x + 2 * y + 1 - c
        chips = _other_chips(x, y)
        mine = lambda i: ins[i] if rows is None else ins[i].at[pl.ds(*rows)]
        slot_of = lambda i, s: outs[i].at[s] if rows is None else outs[i].at[s, pl.ds(*rows)]
        local = [pltpu.make_async_copy(mine(i), slot_of(i, me), local_sems.at[i]) for i in range(n)]
        first = [_remote(mine(i), slot_of(i, me), send_sems.at[i, 0], recv_sems.at[i, 0], sib) for i in range(n)]
        first += [_remote(mine(i), slot_of(i, me), send_sems.at[i, 1 + j], recv_sems.at[i, 1 + j], (px, py, c))
                  for j, (px, py) in enumerate(chips) for i in range(n)]
        if not rest:
            return local, first
        from_sibling = [_remote(mine(i), slot_of(i, sib_id), send_sems.at[i, 0], recv_sems.at[i, 0], sib) for i in range(n)]
        arrive, forward = [], []
        for j, (px, py) in enumerate(chips):
            s = 4 * px + 2 * py
            arrive.append([_remote(mine(i), slot_of(i, s + c), send_sems.at[i, 1 + j], recv_sems.at[i, 1 + j], (px, py, c)) for i in range(n)])
            forward.append([_remote(slot_of(i, s + c), slot_of(i, s + c), send_sems.at[i, 4 + j], recv_sems.at[i, 4 + j], sib) for i in range(n)])
            from_sibling += [_remote(mine(i), slot_of(i, s + 1 - c), send_sems.at[i, 4 + j], recv_sems.at[i, 4 + j], sib) for i in range(n)]
        return local, first, arrive, forward, from_sibling

    def start(ins, outs, sems):
        local, first = copies(ins, outs, sems, False)
        for cp in local + first:
            cp.start()

    def finish(ins, outs, sems):
        local, first, arrive, forward, from_sibling = copies(ins, outs, sems, True)
        for landed, onward in zip(arrive, forward):
            for cp in landed:
                cp.wait_recv()
            for cp in onward:
                cp.start()
        for cp in from_sibling:
            cp.wait_recv()
        for cp in first + [cp for onward in forward for cp in onward]:
            cp.wait_send()
        for cp in local:
            cp.wait()

    return _Plan(list(xs) + list(into or []), [S((NDEV,) + a.shape, a.dtype) for a in xs],
                 [pltpu.SemaphoreType.DMA((n, NDEV - 1)), pltpu.SemaphoreType.DMA((n, NDEV - 1)), pltpu.SemaphoreType.DMA((n,))],
                 start, finish, {n + i: i for i in range(len(into or []))})


def _plan_sibling_swap(gs):
    n = len(gs)

    def copies(ins, outs, sems):
        send_sems, recv_sems = sems
        x, y, c, _ = _place()
        return [_remote(ins[i].at[2 * j + 1 - c], outs[i].at[j], send_sems.at[i, j], recv_sems.at[i, j], (x, y, 1 - c))
                for i in range(n) for j in range(NDEV // 2)]

    def start(ins, outs, sems):
        for cp in copies(ins, outs, sems):
            cp.start()

    def finish(ins, outs, sems):
        for cp in copies(ins, outs, sems):
            cp.wait()

    return _Plan(list(gs), [S((NDEV // 2,) + a.shape[1:], a.dtype) for a in gs],
                 [pltpu.SemaphoreType.DMA((n, NDEV // 2)), pltpu.SemaphoreType.DMA((n, NDEV // 2))], start, finish)


def _pair_add(g, r1, core, name):
    _, r, c = g.shape
    tr = r if r * c * 2 <= 2 * 1024 * 1024 else _row_tile(r, c)

    def body(core_ref, g_ref, r_ref, o_ref):
        o_ref[...] = (g_ref[...].astype(F32) + r_ref[...].astype(F32)).astype(BF16)

    return pl.pallas_call(
        body, name=name,
        grid_spec=pltpu.PrefetchScalarGridSpec(
            num_scalar_prefetch=1, grid=(NDEV // 2, r // tr),
            in_specs=[pl.BlockSpec((None, tr, c), lambda j, k, core_ref: (2 * j + core_ref[0], k, 0)),
                      pl.BlockSpec((None, tr, c), lambda j, k, core_ref: (j, k, 0))],
            out_specs=pl.BlockSpec((None, tr, c), lambda j, k, core_ref: (j, k, 0))),
        out_shape=S((NDEV // 2, r, c), BF16), compiler_params=_cp("parallel", "parallel"),
    )(core, g, r1)


def _plan_chip_swap(ps, rows=None, into=None):
    n = len(ps)

    def copies(ins, outs, sems, landed):
        send_sems, recv_sems, local_sems = sems
        x, y, c, _ = _place()
        mine = 2 * x + y
        part = lambda ref, s: ref.at[s] if rows is None else ref.at[s, pl.ds(*rows)]
        local = [pltpu.make_async_copy(part(ins[i], mine), part(outs[i], mine), local_sems.at[i]) for i in range(n)]
        remote = [_remote(part(ins[i], 2 * px + py), part(outs[i], 2 * px + py if landed else mine),
                          send_sems.at[i, j], recv_sems.at[i, j], (px, py, c))
                  for j, (px, py) in enumerate(_other_chips(x, y)) for i in range(n)]
        return local, remote

    def start(ins, outs, sems):
        local, remote = copies(ins, outs, sems, False)
        for cp in local + remote:
            cp.start()

    def finish(ins, outs, sems):
        local, remote = copies(ins, outs, sems, True)
        for cp in remote + local:
            cp.wait()

    return _Plan(list(ps) + list(into or []), [S(a.shape, a.dtype) for a in ps],
                 [pltpu.SemaphoreType.DMA((n, 3)), pltpu.SemaphoreType.DMA((n, 3)), pltpu.SemaphoreType.DMA((n,))], start, finish,
                 {n + i: i for i in range(len(into or []))})


def _pcall(body, name, args, in_specs, out_shape, out_specs, grid=(), scratch_shapes=(), sem=(), plans=(), starts_plans=False):
    n_in, n_out, n_scr = len(args), len(out_shape), len(scratch_shapes)
    counts = [(len(p.inputs), len(p.out_shapes), len(p.sem_shapes)) for p in plans]
    c_args = [a for p in plans for a in p.inputs]
    c_outs = [s for p in plans for s in p.out_shapes]
    c_sems = [s for p in plans for s in p.sem_shapes]

    def wrapped(*refs):
        cuts = [n_in, len(c_args), n_out, len(c_outs), n_scr, len(c_sems)]
        ins, c_in, outs, c_out, scr, c_sem = [refs[sum(cuts[:k]):sum(cuts[:k + 1])] for k in range(6)]

        def halves(which):
            a = b = s = 0
            for p, (na, nb, ns) in zip(plans, counts):
                getattr(p, which)(c_in[a:a + na], c_out[b:b + nb], c_sem[s:s + ns])
                a, b, s = a + na, b + nb, s + ns

        if not plans:
            body(*ins, *outs, *scr)
        elif not grid and starts_plans:
            body(lambda: halves("start"), *ins, *outs, *scr)
            halves("finish")
        elif not grid:
            halves("start")
            body(*ins, *outs, *scr)
            halves("finish")
        else:
            first = functools.reduce(jnp.logical_and, [pl.program_id(a) == 0 for a in range(len(grid))])
            last = functools.reduce(jnp.logical_and, [pl.program_id(a) == grid[a] - 1 for a in range(len(grid))])
            pl.when(first)(lambda: halves("start"))
            body(*ins, *outs, *scr)
            pl.when(last)(lambda: halves("finish"))

    aliases, a, b = {}, n_in, n_out
    for p, (na, nb, _) in zip(plans, counts):
        aliases.update({a + k: b + v for k, v in p.aliases.items()})
        a, b = a + na, b + nb
    res = pl.pallas_call(
        wrapped, name=name, grid=grid, in_specs=list(in_specs) + [ANY] * len(c_args),
        out_shape=list(out_shape) + c_outs, out_specs=list(out_specs) + [ANY] * len(c_outs),
        scratch_shapes=list(scratch_shapes) + c_sems, input_output_aliases=aliases,
        compiler_params=_cp(*(("arbitrary",) * len(grid) if plans else sem)),
    )(*args, *c_args)
    c_res, b = [], n_out
    for _, nb, _ in counts:
        c_res.append(res[b:b + nb])
        b += nb
    return res[:n_out], c_res


def _exchange(plans, name):
    return _pcall(lambda: None, name, [], [], [], [], plans=plans)[1]


def _ada_rows(c, w_ada, b_ada, name, plans=()):
    d, cols = c.shape[1], w_ada.shape[1]
    gather_c = _plan_all_to_all([c], True)
    gather_p = _plan_all_to_all([S((NDEV, cols), F32)], True)
    n_sem = len(gather_c.sem_shapes)

    def body(start_plans, c_ref, w_ref, b_ref, ada_ref, act_ref, c_all, p_mine, p_all, *sems):
        gather_c.start([c_ref], [c_all], sems[:n_sem])
        gather_c.finish([c_ref], [c_all], sems[:n_sem])
        for s in range(NDEV):
            cc = c_all[s]
            act_ref[s:s + 1, :] = cc * _sigmoid(cc)
        p_mine[...] = _dot(act_ref[...].astype(BF16), w_ref[...].astype(BF16))
        gather_p.start([p_mine], [p_all], sems[n_sem:])
        start_plans()
        gather_p.finish([p_mine], [p_all], sems[n_sem:])
        me = _place()[3]
        for s in range(NDEV):
            ada_ref[:, s * cols:(s + 1) * cols] = p_all[s, pl.ds(me, 1), :] + b_ref[:, s * cols:(s + 1) * cols]

    whole = pl.BlockSpec(memory_space=pltpu.VMEM)
    return _pcall(body, name, [c, w_ada, b_ada], [whole] * 3, [S((1, NDEV * cols), F32), S((NDEV, d), F32)], [whole] * 2,
                  scratch_shapes=[pltpu.VMEM((NDEV,) + c.shape, F32), pltpu.VMEM((NDEV, cols), F32), pltpu.VMEM((NDEV, NDEV, cols), F32)]
                  + gather_c.sem_shapes + gather_p.sem_shapes, plans=plans, starts_plans=True)


def _ffn_in(x, ada9, nrm, w_in, sh_row, sc_row, name, plans=(), h_made=None):
    t, d = (x if h_made is None else h_made).shape
    nb, bw = w_in.shape[0] // 2, w_in.shape[1]
    tm = min(WIDE_ROW_TILE, t)

    def body(*refs):
        if h_made is None:
            x_ref, ada_ref, n_ref, wg_ref, wu_ref, h_ref, gu_ref, s_ref = refs

            @pl.when(pl.program_id(1) == 0)
            def _():
                _, _, h = _normmod(x_ref[...], n_ref[...], ada_ref[sc_row:sc_row + 1, :], ada_ref[sh_row:sh_row + 1, :])
                h_ref[...] = h.astype(BF16)
        else:
            h_ref, wg_ref, wu_ref, gu_ref, s_ref = refs

        h = h_ref[...]
        g = _dot_nt(h, wg_ref[...])
        u = _dot_nt(h, wu_ref[...])
        gu_ref[0] = g.astype(BF16)
        gu_ref[1] = u.astype(BF16)
        s_ref[...] = (g * _sigmoid(g) * u).astype(BF16)

    rows = pl.BlockSpec((tm, d), lambda i, j: (i, 0))
    weights = [pl.BlockSpec((None, bw, d), lambda i, j: (j, 0, 0)), pl.BlockSpec((None, bw, d), lambda i, j: (j + nb, 0, 0))]
    results = ([S((2, nb, t, bw), BF16), S((nb, t, bw), BF16)],
               [pl.BlockSpec((2, None, tm, bw), lambda i, j: (0, j, i, 0)), pl.BlockSpec((None, tm, bw), lambda i, j: (j, i, 0))])
    if h_made is not None:
        (gu, s), moved = _pcall(body, name, [h_made, w_in, w_in], [rows] + weights, results[0], results[1], grid=(t // tm, nb),
                                sem=("parallel", "arbitrary"), plans=plans)
        return (h_made, gu, s), moved
    return _pcall(
        body, name, [x, ada9, nrm, w_in, w_in], grid=(t // tm, nb),
        in_specs=[rows, pl.BlockSpec((N_ADA, d), lambda i, j: (0, 0)), pl.BlockSpec((1, d), lambda i, j: (0, 0))] + weights,
        out_shape=[S((t, d), BF16)] + results[0], out_specs=[rows] + results[1],
        sem=("parallel", "arbitrary"), plans=plans)


def _ffn_out(s, w_out, x, ada9, g_row, res_scale, nrm_next, sh_next, sc_next, name, plans=()):
    nb, t, bw = s.shape
    d = x.shape[1]
    tm = min(ROW_TILE, t)

    def body(s_ref, w_ref, x_ref, ada_ref, nn_ref, xo_ref, f_ref, hn_ref):
        acc = _dot(s_ref[0], w_ref[0])
        for b in range(1, nb):
            acc = acc + _dot(s_ref[b], w_ref[b])
        f_ref[...] = acc.astype(BF16)
        xo = x_ref[...] + (res_scale * ada_ref[g_row:g_row + 1, :]) * acc
        xo_ref[...] = xo
        hn_ref[...] = _normmod(xo, nn_ref[...], ada_ref[sc_next:sc_next + 1, :], ada_ref[sh_next:sh_next + 1, :])[2].astype(BF16)

    rows = pl.BlockSpec((tm, d), lambda i: (i, 0))
    return _pcall(
        body, name, [s, w_out, x, ada9, nrm_next], grid=(t // tm,),
        in_specs=[pl.BlockSpec((nb, tm, bw), lambda i: (0, i, 0)), pl.BlockSpec((nb, bw, d), lambda i: (0, 0, 0)),
                  rows, pl.BlockSpec((N_ADA, d), lambda i: (0, 0)), pl.BlockSpec((1, d), lambda i: (0, 0))],
        out_shape=[S((t, d), F32), S((t, d), BF16), S((t, d), BF16)], out_specs=[rows, rows, rows],
        sem=("parallel",), plans=plans)


def _ffn_bwd_ds(dxo, f, ada9, w_out, gu, g_row, res_scale, name, plans=()):
    t, d = dxo.shape
    nb, bw = w_out.shape[0], w_out.shape[1]
    tm = min(WIDE_ROW_TILE, t)
    ni = t // tm

    def body(dxo_ref, f_ref, ada_ref, w_ref, gu_ref, da_ref, dg_ref, dw_ref, df_ref, acc_ref):
        i, j = pl.program_id(0), pl.program_id(1)

        @pl.when((i == 0) & (j == 0))
        def _():
            dg_ref[...] = jnp.zeros_like(dg_ref)

        @pl.when(j == 0)
        def _():
            dxo_t = dxo_ref[...]
            df_ref[...] = ((res_scale * ada_ref[g_row:g_row + 1, :]) * dxo_t).astype(BF16)
            dg_ref[...] += res_scale * _rsum8(dxo_t * f_ref[...].astype(F32))

        df = df_ref[...]
        ds = _dot_nt(df, w_ref[...])
        g = gu_ref[0].astype(F32)
        u = gu_ref[1].astype(F32)
        sg = _sigmoid_tanh(g)
        silu = g * sg
        da_ref[0] = (ds * u * (sg * (1.0 + g * (1.0 - sg)))).astype(BF16)
        da_ref[1] = (ds * silu).astype(BF16)
        part = _dot_tn((silu * u).astype(BF16), df)

        @pl.when(i == 0)
        def _():
            acc_ref[j] = part

        @pl.when(i > 0)
        def _():
            acc_ref[j] += part

        @pl.when(i == ni - 1)
        def _():
            dw_ref[...] = acc_ref[j].astype(BF16)

    return _pcall(
        body, name, [dxo, f, ada9, w_out, gu], grid=(ni, nb),
        in_specs=[pl.BlockSpec((tm, d), lambda i, j: (i, 0)), pl.BlockSpec((tm, d), lambda i, j: (i, 0)),
                  pl.BlockSpec((N_ADA, d), lambda i, j: (0, 0)), pl.BlockSpec((None, bw, d), lambda i, j: (j, 0, 0)),
                  pl.BlockSpec((2, None, tm, bw), lambda i, j: (0, j, i, 0))],
        out_shape=[S((2, nb, t, bw), BF16), S((8, d), F32), S((nb, bw, d), BF16)],
        out_specs=[pl.BlockSpec((2, None, tm, bw), lambda i, j: (0, j, i, 0)), pl.BlockSpec((8, d), lambda i, j: (0, 0)),
                   pl.BlockSpec((None, bw, d), lambda i, j: (jnp.where(i == ni - 1, j, 0), 0, 0))],
        scratch_shapes=[pltpu.VMEM((tm, d), BF16), pltpu.VMEM((nb, bw, d), F32)],
        sem=("arbitrary", "arbitrary"), plans=plans)


def _ffn_bwd_dh(da, w_in, x, dxo, ada9, nrm, sh_row, sc_row, name, plans=()):
    t, d = x.shape
    nb, bw = w_in.shape[0] // 2, w_in.shape[1]
    tm = min(WIDE_ROW_TILE, t)

    def body(da_ref, wg_ref, wu_ref, x_ref, dxo_ref, ada_ref, n_ref, dx_ref, dsh_ref, dsc_ref, dn_ref, acc_ref):
        i, j = pl.program_id(0), pl.program_id(1)

        @pl.when((i == 0) & (j == 0))
        def _():
            dsh_ref[...] = jnp.zeros_like(dsh_ref)
            dsc_ref[...] = jnp.zeros_like(dsc_ref)
            dn_ref[...] = jnp.zeros_like(dn_ref)

        part = _dot(da_ref[0], wg_ref[...]) + _dot(da_ref[1], wu_ref[...])

        @pl.when(j == 0)
        def _():
            acc_ref[...] = part

        @pl.when(j > 0)
        def _():
            acc_ref[...] += part

        @pl.when(j == nb - 1)
        def _():
            for r0 in range(0, tm, min(EPILOGUE_ROWS, tm)):
                rows = slice(r0, r0 + min(EPILOGUE_ROWS, tm))
                dx, tsh, tsc, tn = _normmod_bwd(acc_ref[rows, :], x_ref[rows, :], n_ref[...], ada_ref[sc_row:sc_row + 1, :])
                dx_ref[rows, :] = dxo_ref[rows, :] + dx
                dsh_ref[...] += _rsum8(tsh)
                dsc_ref[...] += _rsum8(tsc)
                dn_ref[...] += _rsum8(tn)

    vec = pl.BlockSpec((8, d), lambda i, j: (0, 0))
    return _pcall(
        body, name, [da, w_in, w_in, x, dxo, ada9, nrm], grid=(t // tm, nb),
        in_specs=[pl.BlockSpec((2, None, tm, bw), lambda i, j: (0, j, i, 0)),
                  pl.BlockSpec((None, bw, d), lambda i, j: (j, 0, 0)), pl.BlockSpec((None, bw, d), lambda i, j: (j + nb, 0, 0)),
                  pl.BlockSpec((tm, d), lambda i, j: (i, 0)), pl.BlockSpec((tm, d), lambda i, j: (i, 0)),
                  pl.BlockSpec((N_ADA, d), lambda i, j: (0, 0)), pl.BlockSpec((1, d), lambda i, j: (0, 0))],
        out_shape=[S((t, d), F32), S((8, d), F32), S((8, d), F32), S((8, d), F32)],
        out_specs=[pl.BlockSpec((tm, d), lambda i, j: (i, 0)), vec, vec, vec],
        scratch_shapes=[pltpu.VMEM((tm, d), F32)],
        sem=("arbitrary", "arbitrary"), plans=plans)


def _tn_matmul(a, b, a_block, a_map, b_block, b_map, out_shape, out_block, out_map, nblk, name, plans=()):
    t = a.shape[-2]
    tk = min(K_TILE, t)
    nk = t // tk

    def body(a_ref, b_ref, o_ref, acc_ref):
        k = pl.program_id(1)
        for q in (range(a_ref.shape[0]) if len(a_ref.shape) == 3 else [Ellipsis]):
            part = _dot_tn(a_ref[q], b_ref[...])

            @pl.when(k == 0)
            def _():
                acc_ref[q] = part

            @pl.when(k > 0)
            def _():
                acc_ref[q] += part

        @pl.when(k == nk - 1)
        def _():
            o_ref[...] = acc_ref[...].astype(BF16)

    (out,), moved = _pcall(
        body, name, [a, b], grid=(nblk, nk),
        in_specs=[pl.BlockSpec(a_block(tk), a_map), pl.BlockSpec(b_block(tk), b_map)],
        out_shape=[S(out_shape, BF16)], out_specs=[pl.BlockSpec(out_block, out_map)],
        scratch_shapes=[pltpu.VMEM(tuple(n for n in out_block if n is not None), F32)],
        sem=("parallel", "arbitrary"), plans=plans)
    return out, moved


def _ffn_backward(dxo, x_in, h, gu, f, ada9, nrm, w_in, w_out, rows, core, name, ds_plans=(), dwin_plans=(), later=False):
    sh_row, sc_row, g_row = rows
    _, nb, t, bw = gu.shape
    d = x_in.shape[1]
    (da, dg, dw_out), ds_moved = _ffn_bwd_ds(dxo, f, ada9, w_out, gu, g_row, 0.5, name + "_ds", plans=ds_plans)
    dw_out = dw_out.reshape(NDEV, -1, d)
    dw_in, ((half_out,), *dwin_moved) = _tn_matmul(
        da.reshape(2 * nb, t, bw), h, lambda tk: (2, tk, bw), lambda sb, k: (sb, k, 0), lambda tk: (tk, d), lambda sb, k: (k, 0),
        (2 * nb, bw, d), (2, bw, d), lambda sb, k: (sb, 0, 0), nb, name + "_dwin",
        plans=[_plan_sibling_swap([dw_out])] + list(dwin_plans(ds_moved) if callable(dwin_plans) else dwin_plans))
    sum_out = _pair_add(dw_out, half_out, core, name + "_dwout_add")
    if later:
        (dx, dsh, dsc, dn), ((half_in,),) = _ffn_bwd_dh(da, w_in, x_in, dxo, ada9, nrm, sh_row, sc_row, name + "_dh",
                                                       plans=[_plan_sibling_swap([dw_in])])
        return dx, _pair_add(dw_in, half_in, core, name + "_dwin_add"), sum_out, (dsh, dsc, dg, dn), ds_moved, dwin_moved
    ((half_in,),) = _exchange([_plan_sibling_swap([dw_in])], name + "_dwin_swap")
    sum_in = _pair_add(dw_in, half_in, core, name + "_dwin_add")
    head = sum_out.shape[1] // 2
    (dx, dsh, dsc, dn), ((recv_in,), (out_begun,)) = _ffn_bwd_dh(
        da, w_in, x_in, dxo, ada9, nrm, sh_row, sc_row, name + "_dh",
        plans=[_plan_chip_swap([sum_in]), _plan_chip_swap([sum_out], rows=(0, head))])
    return dx, recv_in, (sum_out, out_begun, head), (dsh, dsc, dg, dn), ds_moved, dwin_moved


def _mix_in(h, w, widths, dts, name, plans=()):
    t, d = h.shape
    n = w.shape[0]
    tm = min(ROW_TILE, t)
    starts = [sum(widths[:i]) for i in range(len(widths))]

    def body(h_ref, w_ref, *out_refs):
        hb = h_ref[...]
        for o_ref, st, wd in zip(out_refs, starts, widths):
            o_ref[...] = _dot_nt(hb, w_ref[st:st + wd, :]).astype(o_ref.dtype)

    return _pcall(
        body, name, [h, w], grid=(t // tm,),
        in_specs=[pl.BlockSpec((tm, d), lambda i: (i, 0)), pl.BlockSpec((n, d), lambda i: (0, 0))],
        out_shape=[S((t, wd), dt) for wd, dt in zip(widths, dts)],
        out_specs=[pl.BlockSpec((tm, wd), lambda i: (i, 0)) for wd in widths],
        sem=("parallel",), plans=plans)


def _tri(lower):
    r = lax.broadcasted_iota(jnp.int32, (CHUNK, CHUNK), 0)
    c = lax.broadcasted_iota(jnp.int32, (CHUNK, CHUNK), 1)
    return (r >= c) if lower else (c >= r)


def _dot_01(m, x):
    hi = x.astype(BF16)
    r1 = x - hi.astype(F32)
    mid = r1.astype(BF16)
    lo = (r1 - mid.astype(F32)).astype(BF16)
    return _dot(m, hi) + _dot(m, mid) + _dot(m, lo)


def _gla_chunk_terms(q, k, lg, low01):
    b = _dot_01(low01, lg)
    bl = b[CHUNK - 1:CHUNK, :]
    r = 0.5 * bl
    eb, ebl, em, en = jnp.exp(b), jnp.exp(bl - b), jnp.exp(b - r), jnp.exp(r - b)
    return eb, ebl, em, en, jnp.exp(bl), q * eb, k * ebl, q * em, k * en


def _scores(qm_h, knp, qk1_h):
    r = lax.broadcasted_iota(jnp.int32, (CHUNK, CHUNK), 0)
    c = lax.broadcasted_iota(jnp.int32, (CHUNK, CHUNK), 1)
    p = jnp.where(r > c, _dot_nt(qm_h, knp), 0.0)
    return jnp.where(r == c, jnp.sum(qk1_h, axis=1, keepdims=True), p)


def _gla_fwd(qk, v, gl, wg, bg, heads, name, plans=()):
    t = qk.shape[0]
    kw, vw = qk.shape[1] // 2, v.shape[1]
    dk, dv = kw // heads, vw // heads
    assert dk == 64 and dv == 128 and kw % 128 == 0
    gt = min(ROW_TILE, t)
    nc = gt // CHUNK
    scale = dk ** -0.5

    def body(qk_ref, v_ref, gl_ref, wg_ref, bg_ref, o_ref, lg_ref, sall_ref, st_ref):
        @pl.when(pl.program_id(0) == 0)
        def _():
            st_ref[...] = jnp.zeros_like(st_ref)

        gk = _dot(gl_ref[...].astype(BF16), wg_ref[...]) + bg_ref[...]
        lg_ref[...] = (jnp.minimum(gk, 0.0) - jnp.log(1.0 + jnp.exp(-jnp.abs(gk)))) / GATE_NORMALIZER
        low01 = _tri(True).astype(BF16)
        lane = lax.broadcasted_iota(jnp.int32, (CHUNK, LANES), 1)

        def chunk(ci, carry):
            rows = pl.ds(pl.multiple_of(ci * CHUNK, CHUNK), CHUNK)
            q = qk_ref[rows, 0:kw] * scale
            k = qk_ref[rows, kw:2 * kw]
            qk1 = q.astype(BF16).astype(F32) * k.astype(BF16).astype(F32)
            eb, ebl, em, en, ebl_row, qe, ke, qm, kn = _gla_chunk_terms(q, k, lg_ref[rows, :], low01)
            for h in range(heads):
                lanes = slice(LANES * (h // 2), LANES * (h // 2) + LANES)
                own = (lane < 64) if h % 2 == 0 else (lane >= 64)
                knp = kn[:, lanes].astype(BF16)
                qm_h = jnp.where(own, qm[:, lanes], 0.0).astype(BF16)
                qe_h = jnp.where(own, qe[:, lanes], 0.0).astype(BF16)
                ke_h = jnp.where(own, ke[:, lanes], 0.0).astype(BF16)
                v_h = v_ref[rows, h * dv:(h + 1) * dv]
                st = st_ref[h]
                sall_ref[ci, h] = st
                p = _scores(qm_h, knp, jnp.where(own, qk1[:, lanes], 0.0))
                o_ref[rows, h * dv:(h + 1) * dv] = _dot(p.astype(BF16), v_h) + _dot_nt(qe_h, st.astype(BF16))
                st_ref[h] = st * ebl_row[:, lanes] + _dot_tn(v_h, ke_h)
            return carry

        lax.fori_loop(0, nc, chunk, 0, unroll=True)

    return _pcall(
        body, name, [qk, v, gl, wg, bg], grid=(t // gt,),
        in_specs=[pl.BlockSpec((gt, 2 * kw), lambda i: (i, 0)), pl.BlockSpec((gt, vw), lambda i: (i, 0)),
                  pl.BlockSpec((gt, LANES), lambda i: (i, 0)), pl.BlockSpec((LANES, kw), lambda i: (0, 0)),
                  pl.BlockSpec((1, kw), lambda i: (0, 0))],
        out_shape=[S((t, vw), F32), S((t, kw), F32), S((t // CHUNK, heads, dv, LANES), F32)],
        out_specs=[pl.BlockSpec((gt, vw), lambda i: (i, 0)), pl.BlockSpec((gt, kw), lambda i: (i, 0)),
                   pl.BlockSpec((nc, heads, dv, LANES), lambda i: (i, 0, 0, 0))],
        scratch_shapes=[pltpu.VMEM((heads, dv, LANES), F32)],
        sem=("arbitrary",), plans=plans)


def _gla_bwd(qk, v, lg, do, sall, gl, wg, heads, name, plans=()):
    t = qk.shape[0]
    kw, vw = qk.shape[1] // 2, v.shape[1]
    dk, dv = kw // heads, vw // heads
    gt = min(ROW_TILE, t)
    nc = gt // CHUNK
    nt = t // gt
    scale = dk ** -0.5

    def body(qk_ref, v_ref, lg_ref, do_ref, sall_ref, gl_ref, wg_ref, dqk_ref, dv_ref, dgl_ref, dwg_ref, dbg_ref, dst_ref, dgk_ref):
        @pl.when(pl.program_id(0) == 0)
        def _():
            dst_ref[...] = jnp.zeros_like(dst_ref)
            dwg_ref[...] = jnp.zeros_like(dwg_ref)
            dbg_ref[...] = jnp.zeros_like(dbg_ref)

        low01 = _tri(True).astype(BF16)
        up01 = _tri(False).astype(BF16)
        causal = _tri(True)
        lane = lax.broadcasted_iota(jnp.int32, (CHUNK, LANES), 1)
        last_row = lax.broadcasted_iota(jnp.int32, (CHUNK, kw), 0) == CHUNK - 1

        def chunk(cj, carry):
            ci = nc - 1 - cj
            rows = pl.ds(pl.multiple_of(ci * CHUNK, CHUNK), CHUNK)
            q = qk_ref[rows, 0:kw] * scale
            k = qk_ref[rows, kw:2 * kw]
            qk1 = q.astype(BF16).astype(F32) * k.astype(BF16).astype(F32)
            lgc = lg_ref[rows, :]
            eb, ebl, em, en, ebl_row, qe, ke, qm, kn = _gla_chunk_terms(q, k, lgc, low01)
            dqe, dqm, dkn, dke, drow = [], [], [], [], []
            for pr in range(kw // LANES):
                lanes = slice(LANES * pr, LANES * pr + LANES)
                knp = kn[:, lanes].astype(BF16)
                parts = []
                for half in range(2):
                    h = 2 * pr + half
                    own = (lane < 64) if half == 0 else (lane >= 64)
                    qm_h = jnp.where(own, qm[:, lanes], 0.0).astype(BF16)
                    qe_h = jnp.where(own, qe[:, lanes], 0.0).astype(BF16)
                    ke_h = jnp.where(own, ke[:, lanes], 0.0).astype(BF16)
                    v_h = v_ref[rows, h * dv:(h + 1) * dv]
                    do_h = do_ref[rows, h * dv:(h + 1) * dv]
                    st = sall_ref[ci, h]
                    dst = dst_ref[h]
                    stb, dstb = st.astype(BF16), dst.astype(BF16)
                    p = _scores(qm_h, knp, jnp.where(own, qk1[:, lanes], 0.0)).astype(BF16)
                    dp = jnp.where(causal, _dot_nt(do_h, v_h), 0.0).astype(BF16)
                    dv_ref[rows, h * dv:(h + 1) * dv] = (_dot_tn(p, do_h) + _dot_nt(ke_h, dstb)).astype(BF16)
                    parts.append((jnp.where(own, _dot(dp, knp), 0.0), _dot_tn(dp, qm_h), _dot(do_h, stb), _dot(v_h, dstb),
                                  jnp.sum(st * dst, axis=0, keepdims=True)))
                    dst_ref[h] = dst * ebl_row[:, lanes] + _dot_tn(do_h, qe_h)
                dqm.append(parts[0][0] + parts[1][0])
                dkn.append(parts[0][1] + parts[1][1])
                dqe.append(parts[0][2] + parts[1][2])
                dke.append(parts[0][3] + parts[1][3])
                drow.append(parts[0][4] + parts[1][4])
            dqm, dkn, dqe, dke, drow = [jnp.concatenate(a, axis=1) for a in (dqm, dkn, dqe, dke, drow)]
            dqk_ref[rows, 0:kw] = ((dqe * eb + dqm * em) * scale).astype(BF16)
            dqk_ref[rows, kw:2 * kw] = (dke * ebl + dkn * en).astype(BF16)
            tke = dke * ke
            db = dqe * qe + dqm * qm - dkn * kn - tke
            dbl = jnp.sum(tke, axis=0, keepdims=True) + drow * ebl_row
            db = db + jnp.where(last_row, dbl, 0.0)
            dlg = _dot_01(up01, db)
            dgk_ref[rows, :] = dlg * ((1.0 - jnp.exp(GATE_NORMALIZER * lgc)) / GATE_NORMALIZER)
            return carry

        lax.fori_loop(0, nc, chunk, 0, unroll=True)
        dgk = dgk_ref[...]
        dgkb = dgk.astype(BF16)
        dgl_ref[...] = _dot_nt(dgkb, wg_ref[...]).astype(BF16)
        dwg_ref[...] += _dot_tn(gl_ref[...].astype(BF16), dgkb)
        dbg_ref[...] += _rsum8(dgk)

    rev = lambda i: (nt - 1 - i, 0)
    return _pcall(
        body, name, [qk, v, lg, do, sall, gl, wg], grid=(nt,),
        in_specs=[pl.BlockSpec((gt, 2 * kw), rev), pl.BlockSpec((gt, vw), rev), pl.BlockSpec((gt, kw), rev),
                  pl.BlockSpec((gt, vw), rev), pl.BlockSpec((nc, heads, dv, LANES), lambda i: (nt - 1 - i, 0, 0, 0)),
                  pl.BlockSpec((gt, LANES), rev), pl.BlockSpec((LANES, kw), lambda i: (0, 0))],
        out_shape=[S((t, 2 * kw), BF16), S((t, vw), BF16), S((t, LANES), BF16), S((LANES, kw), F32), S((8, kw), F32)],
        out_specs=[pl.BlockSpec((gt, 2 * kw), rev), pl.BlockSpec((gt, vw), rev), pl.BlockSpec((gt, LANES), rev),
                   pl.BlockSpec((LANES, kw), lambda i: (0, 0)), pl.BlockSpec((8, kw), lambda i: (0, 0))],
        scratch_shapes=[pltpu.VMEM((heads, dv, LANES), F32), pltpu.VMEM((gt, kw), F32)],
        sem=("arbitrary",), plans=plans)


def _conv_taps(cx_ref, halo_ref, first, cw):
    tm = cx_ref.shape[0]
    u = cx_ref[:, cw:2 * cw].astype(F32) * cx_ref[:, 2 * cw:3 * cw].astype(F32)
    uh = halo_ref[:, cw:2 * cw].astype(F32) * halo_ref[:, 2 * cw:3 * cw].astype(F32)
    uh = jnp.where(first, 0.0, uh)
    before1, before2 = uh[BF16_ROWS - 1:BF16_ROWS, :], uh[BF16_ROWS - 2:BF16_ROWS - 1, :]
    row = lax.broadcasted_iota(jnp.int32, (tm, cw), 0)
    u1 = jnp.where(row == 0, before1, pltpu.roll(u, 1, 0))
    u2 = jnp.where(row == 0, before2, jnp.where(row == 1, before1, pltpu.roll(u, 2, 0)))
    return u, u1, u2


def _head_norm(o_h, gn):
    rstd = lax.rsqrt(jnp.mean(o_h * o_h, axis=-1, keepdims=True) + EPS)
    ohat = o_h * rstd
    return ohat, rstd, ohat * gn


def _mix_out(cx, o, go, conv_w, gn, w_out, x, ada9, nrm_next, heads, name, plans=()):
    t, d = x.shape
    cw, vw = conv_w.shape[1], o.shape[1]
    dv = vw // heads
    tm = min(ROW_TILE, t)

    def body(cx_ref, halo_ref, o_ref, go_ref, cwt_ref, gn_ref, w_ref, x_ref, ada_ref, nn_ref, xo_ref, m_ref, y_ref, hn_ref):
        u, u1, u2 = _conv_taps(cx_ref, halo_ref, pl.program_id(0) == 0, cw)
        yc = cwt_ref[0:1, :] * u2 + cwt_ref[1:2, :] * u1 + cwt_ref[2:3, :] * u
        y_ref[:, 0:cw] = (cx_ref[:, 0:cw].astype(F32) * yc).astype(BF16)
        for h in range(heads):
            cols = slice(h * dv, (h + 1) * dv)
            _, _, on = _head_norm(o_ref[:, cols], gn_ref[...])
            g = go_ref[:, cols].astype(F32)
            y_ref[:, cw + h * dv:cw + (h + 1) * dv] = (on * (g * _sigmoid(g))).astype(BF16)
        m = _dot(y_ref[...], w_ref[...])
        m_ref[...] = m.astype(BF16)
        xo = x_ref[...] + ada_ref[5:6, :] * m
        xo_ref[...] = xo
        hn_ref[...] = _normmod(xo, nn_ref[...], ada_ref[7:8, :], ada_ref[6:7, :])[2].astype(BF16)

    return _pcall(
        body, name, [cx, cx, o, go, conv_w, gn, w_out, x, ada9, nrm_next], grid=(t // tm,),
        in_specs=[pl.BlockSpec((tm, 3 * cw), lambda i: (i, 0)),
                  pl.BlockSpec((BF16_ROWS, 3 * cw), lambda i: (jnp.maximum(i * (tm // BF16_ROWS) - 1, 0), 0)),
                  pl.BlockSpec((tm, vw), lambda i: (i, 0)), pl.BlockSpec((tm, vw), lambda i: (i, 0)),
                  pl.BlockSpec((3, cw), lambda i: (0, 0)), pl.BlockSpec((1, dv), lambda i: (0, 0)),
                  pl.BlockSpec((cw + vw, d), lambda i: (0, 0)), pl.BlockSpec((tm, d), lambda i: (i, 0)),
                  pl.BlockSpec((N_ADA, d), lambda i: (0, 0)), pl.BlockSpec((1, d), lambda i: (0, 0))],
        out_shape=[S((t, d), F32), S((t, d), BF16), S((t, cw + vw), BF16), S((t, d), BF16)],
        out_specs=[pl.BlockSpec((tm, d), lambda i: (i, 0)), pl.BlockSpec((tm, d), lambda i: (i, 0)),
                   pl.BlockSpec((tm, cw + vw), lambda i: (i, 0)), pl.BlockSpec((tm, d), lambda i: (i, 0))],
        sem=("parallel",), plans=plans)


def _mix_bwd_a(dxo, m, ycat, ada9, w_out, cx, o, go, conv_w, gn, heads, name, plans=()):
    t, d = dxo.shape
    cw, vw = conv_w.shape[1], o.shape[1]
    dv = vw // heads
    tm = min(ROW_TILE, t)
    nt = t // tm

    def body(dxo_ref, m_ref, y_ref, ada_ref, w_ref, cx_ref, halo_ref, o_ref, go_ref, cwt_ref, gn_ref,
             dw_ref, dyc_ref, dcb_ref, do_ref, dgo_ref, dg_ref, dcw_ref, dgn_ref, acc_ref):
        @pl.when(pl.program_id(0) == 0)
        def _():
            dg_ref[...] = jnp.zeros_like(dg_ref)
            dcw_ref[...] = jnp.zeros_like(dcw_ref)
            dgn_ref[...] = jnp.zeros_like(dgn_ref)

        dxo_t = dxo_ref[...]
        dmb = (ada_ref[5:6, :] * dxo_t).astype(BF16)
        part = _dot_tn(y_ref[...], dmb)

        @pl.when(pl.program_id(0) == 0)
        def _():
            acc_ref[...] = part

        @pl.when(pl.program_id(0) > 0)
        def _():
            acc_ref[...] += part

        @pl.when(pl.program_id(0) == nt - 1)
        def _():
            dw_ref[...] = acc_ref[...].astype(BF16)

        dg_ref[...] += _rsum8(dxo_t * m_ref[...].astype(F32))
        dy = _dot_nt(dmb, w_ref[...])
        u, u1, u2 = _conv_taps(cx_ref, halo_ref, pl.program_id(0) == 0, cw)
        yc = cwt_ref[0:1, :] * u2 + cwt_ref[1:2, :] * u1 + cwt_ref[2:3, :] * u
        dyv = dy[:, 0:cw]
        dcb_ref[...] = (dyv * yc).astype(BF16)
        dyc = dyv * cx_ref[:, 0:cw].astype(F32)
        dyc_ref[...] = dyc
        dcw_ref[0] += _rsum8(dyc * u2)
        dcw_ref[1] += _rsum8(dyc * u1)
        dcw_ref[2] += _rsum8(dyc * u)
        for h in range(heads):
            cols = slice(h * dv, (h + 1) * dv)
            ohat, rstd, on = _head_norm(o_ref[:, cols], gn_ref[...])
            g = go_ref[:, cols].astype(F32)
            sg = _sigmoid_tanh(g)
            dyg = dy[:, cw + h * dv:cw + (h + 1) * dv]
            dgo_ref[:, cols] = (dyg * on * (sg * (1.0 + g * (1.0 - sg)))).astype(BF16)
            don = dyg * (g * sg)
            dgn_ref[...] += _rsum8(don * ohat)
            tt = don * gn_ref[...]
            do_ref[:, cols] = (rstd * (tt - ohat * jnp.mean(tt * ohat, axis=-1, keepdims=True))).astype(BF16)

    return _pcall(
        body, name, [dxo, m, ycat, ada9, w_out, cx, cx, o, go, conv_w, gn], grid=(nt,),
        in_specs=[pl.BlockSpec((tm, d), lambda i: (i, 0)), pl.BlockSpec((tm, d), lambda i: (i, 0)),
                  pl.BlockSpec((tm, cw + vw), lambda i: (i, 0)),
                  pl.BlockSpec((N_ADA, d), lambda i: (0, 0)), pl.BlockSpec((cw + vw, d), lambda i: (0, 0)),
                  pl.BlockSpec((tm, 3 * cw), lambda i: (i, 0)),
                  pl.BlockSpec((BF16_ROWS, 3 * cw), lambda i: (jnp.maximum(i * (tm // BF16_ROWS) - 1, 0), 0)),
                  pl.BlockSpec((tm, vw), lambda i: (i, 0)), pl.BlockSpec((tm, vw), lambda i: (i, 0)),
                  pl.BlockSpec((3, cw), lambda i: (0, 0)), pl.BlockSpec((1, dv), lambda i: (0, 0))],
        out_shape=[S((cw + vw, d), BF16), S((t, cw), F32), S((t, cw), BF16), S((t, vw), BF16), S((t, vw), BF16),
                   S((8, d), F32), S((3, 8, cw), F32), S((8, dv), F32)],
        out_specs=[pl.BlockSpec((cw + vw, d), lambda i: (0, 0)), pl.BlockSpec((tm, cw), lambda i: (i, 0)),
                   pl.BlockSpec((tm, cw), lambda i: (i, 0)), pl.BlockSpec((tm, vw), lambda i: (i, 0)),
                   pl.BlockSpec((tm, vw), lambda i: (i, 0)), pl.BlockSpec((8, d), lambda i: (0, 0)),
                   pl.BlockSpec((3, 8, cw), lambda i: (0, 0, 0)), pl.BlockSpec((8, dv), lambda i: (0, 0))],
        scratch_shapes=[pltpu.VMEM((cw + vw, d), F32)],
        sem=("arbitrary",), plans=plans)


def _mix_bwd_b(dyc, cx, dcb, dqk, dvv, dgo, dgl, conv_w, w, x, dxo, ada9, nrm, name, plans=()):
    t, d = x.shape
    cw = conv_w.shape[1]
    n = w.shape[0]
    tm = min(ROW_TILE, t)
    nt = t // tm
    pieces = [dcb.shape[1], cw, cw, dqk.shape[1], dvv.shape[1], dgo.shape[1], dgl.shape[1]]
    assert sum(pieces) == n

    def body(dyc_ref, nxt_ref, cx_ref, dcb_ref, dqk_ref, dv_ref, dgo_ref, dgl_ref, cwt_ref, w_ref, x_ref, dxo_ref, ada_ref, n_ref,
             dx_ref, dp_ref, dsh_ref, dsc_ref, dn_ref):
        i = pl.program_id(0)

        @pl.when(i == 0)
        def _():
            dsh_ref[...] = jnp.zeros_like(dsh_ref)
            dsc_ref[...] = jnp.zeros_like(dsc_ref)
            dn_ref[...] = jnp.zeros_like(dn_ref)

        dyc_t = dyc_ref[...]
        nxt = jnp.where(i == nt - 1, 0.0, nxt_ref[...])
        row = lax.broadcasted_iota(jnp.int32, (tm, cw), 0)
        d1 = jnp.where(row == tm - 1, nxt[0:1, :], pltpu.roll(dyc_t, tm - 1, 0))
        d2 = jnp.where(row == tm - 2, nxt[0:1, :], jnp.where(row == tm - 1, nxt[1:2, :], pltpu.roll(dyc_t, tm - 2, 0)))
        du = cwt_ref[2:3, :] * dyc_t + cwt_ref[1:2, :] * d1 + cwt_ref[0:1, :] * d2
        c0 = 0
        dp_ref[:, c0:c0 + cw] = dcb_ref[...]
        dp_ref[:, cw:2 * cw] = (du * cx_ref[:, 2 * cw:3 * cw].astype(F32)).astype(BF16)
        dp_ref[:, 2 * cw:3 * cw] = (du * cx_ref[:, cw:2 * cw].astype(F32)).astype(BF16)
        c0 = 3 * cw
        for ref in (dqk_ref, dv_ref, dgo_ref, dgl_ref):
            wd = ref.shape[1]
            dp_ref[:, c0:c0 + wd] = ref[...]
            c0 += wd
        dh = _dot(dp_ref[...], w_ref[...])
        dx, tsh, tsc, tn = _normmod_bwd(dh, x_ref[...], n_ref[...], ada_ref[4:5, :])
        dx_ref[...] = dxo_ref[...] + dx
        dsh_ref[...] += _rsum8(tsh)
        dsc_ref[...] += _rsum8(tsc)
        dn_ref[...] += _rsum8(tn)

    row_spec = lambda wd: pl.BlockSpec((tm, wd), lambda i: (i, 0))
    vec = pl.BlockSpec((8, d), lambda i: (0, 0))
    return _pcall(
        body, name, [dyc, dyc, cx, dcb, dqk, dvv, dgo, dgl, conv_w, w, x, dxo, ada9, nrm], grid=(nt,),
        in_specs=[row_spec(cw), pl.BlockSpec((8, cw), lambda i: (jnp.minimum((i + 1) * (tm // 8), t // 8 - 1), 0)),
                  row_spec(3 * cw), row_spec(cw), row_spec(dqk.shape[1]), row_spec(dvv.shape[1]), row_spec(dgo.shape[1]),
                  row_spec(dgl.shape[1]), pl.BlockSpec((3, cw), lambda i: (0, 0)), pl.BlockSpec((n, d), lambda i: (0, 0)),
                  row_spec(d), row_spec(d), pl.BlockSpec((N_ADA, d), lambda i: (0, 0)), pl.BlockSpec((1, d), lambda i: (0, 0))],
        out_shape=[S((t, d), F32), S((t, n), BF16), S((8, d), F32), S((8, d), F32), S((8, d), F32)],
        out_specs=[row_spec(d), row_spec(n), vec, vec, vec],
        sem=("arbitrary",), plans=plans)


def _ffn_out_loss(s, w_out, x, ada9, g_row, res_scale, target, nrm, name):
    nb, t, bw = s.shape
    d = x.shape[1]
    tm = min(ROW_TILE, t)
    nt = t // tm

    def body(s_ref, w_ref, x_ref, ada_ref, tg_ref, n_ref, f_ref, loss_ref, dx_ref, dn_ref, acc_ref):
        i = pl.program_id(0)

        @pl.when(i == 0)
        def _():
            acc_ref[...] = jnp.zeros_like(acc_ref)
            dn_ref[...] = jnp.zeros_like(dn_ref)

        f = _dot(s_ref[0], w_ref[0])
        for b in range(1, nb):
            f = f + _dot(s_ref[b], w_ref[b])
        f_ref[...] = f.astype(BF16)
        xt = x_ref[...] + (res_scale * ada_ref[g_row:g_row + 1, :]) * f
        rstd = lax.rsqrt(jnp.mean(xt * xt, axis=-1, keepdims=True) + EPS)
        xhat = xt * rstd
        err = xhat * n_ref[...] - tg_ref[...]
        acc_ref[...] += _rsum8(err * err)
        dy = err * (1.0 / d)
        dn_ref[...] += _rsum8(dy * xhat)
        dxhat = dy * n_ref[...]
        dx_ref[...] = rstd * (dxhat - xhat * jnp.mean(dxhat * xhat, axis=-1, keepdims=True))

        @pl.when(i == nt - 1)
        def _():
            loss_ref[...] = jnp.full(loss_ref.shape, (0.5 / d) * jnp.sum(acc_ref[...]), F32)

    return pl.pallas_call(
        body, name=name, grid=(nt,),
        in_specs=[pl.BlockSpec((nb, tm, bw), lambda i: (0, i, 0)), pl.BlockSpec((nb, bw, d), lambda i: (0, 0, 0)),
                  pl.BlockSpec((tm, d), lambda i: (i, 0)), pl.BlockSpec((N_ADA, d), lambda i: (0, 0)),
                  pl.BlockSpec((tm, d), lambda i: (i, 0)), pl.BlockSpec((1, d), lambda i: (0, 0))],
        out_shape=[S((t, d), BF16), S((1, LANES), F32), S((t, d), F32), S((8, d), F32)],
        out_specs=[pl.BlockSpec((tm, d), lambda i: (i, 0)), pl.BlockSpec((1, LANES), lambda i: (0, 0)),
                   pl.BlockSpec((tm, d), lambda i: (i, 0)), pl.BlockSpec((8, d), lambda i: (0, 0))],
        scratch_shapes=[pltpu.VMEM((8, d), F32)],
        compiler_params=_cp("arbitrary"),
    )(s, w_out, x, ada9, target, nrm)


def _pack_smalls(vec_parts, dcw, dbg, dgn, dwg, loss_v, rank, name):
    d = vec_parts[0].shape[1]
    cw, kw, dv = dcw.shape[2], dbg.shape[1], dgn.shape[1]
    nv = len(vec_parts)
    loss_row = nv + 2 + rank * kw // d
    assert 2 * cw == d and cw + kw + dv <= d and (rank * kw) % d == 0 and loss_row < PACK_ROWS
    per_row = d // kw

    def body(*refs):
        vrefs, (dcw_ref, dbg_ref, dgn_ref, dwg_ref, loss_ref, o_ref) = refs[:nv], refs[nv:]
        o_ref[...] = jnp.zeros_like(o_ref)
        o_ref[loss_row:loss_row + 1, 0:loss_ref.shape[1]] = loss_ref[...]
        for r, ref in enumerate(vrefs):
            o_ref[r:r + 1, :] = jnp.sum(ref[...], axis=0, keepdims=True)
        o_ref[nv:nv + 1, 0:cw] = jnp.sum(dcw_ref[0], axis=0, keepdims=True)
        o_ref[nv:nv + 1, cw:2 * cw] = jnp.sum(dcw_ref[1], axis=0, keepdims=True)
        o_ref[nv + 1:nv + 2, 0:cw] = jnp.sum(dcw_ref[2], axis=0, keepdims=True)
        o_ref[nv + 1:nv + 2, cw:cw + kw] = jnp.sum(dbg_ref[...], axis=0, keepdims=True)
        o_ref[nv + 1:nv + 2, cw + kw:cw + kw + dv] = jnp.sum(dgn_ref[...], axis=0, keepdims=True)
        for r in range(rank):
            o_ref[nv + 2 + r // per_row:nv + 3 + r // per_row, (r % per_row) * kw:(r % per_row + 1) * kw] = dwg_ref[r:r + 1, :]

    return pl.pallas_call(body, name=name, out_shape=S((PACK_ROWS, d), F32), compiler_params=_cp())(*vec_parts, dcw, dbg, dgn, dwg, loss_v)


def _sum_slots(a, name):
    def body(a_ref, o_ref):
        acc = a_ref[0]
        for s in range(1, NDEV):
            acc = acc + a_ref[s]
        o_ref[...] = acc

    return pl.pallas_call(body, name=name, out_shape=S(a.shape[1:], F32), compiler_params=_cp())(a)


def _adamw(w, g, m, v):
    m = ADAM_B1 * m + (1.0 - ADAM_B1) * g
    v = ADAM_B2 * v + (1.0 - ADAM_B2) * (g * g)
    m_hat = m / (1.0 - ADAM_B1 ** ADAM_STEP)
    v_hat = v / (1.0 - ADAM_B2 ** ADAM_STEP)
    return -ADAM_LR * (m_hat / (jnp.sqrt(v_hat) + ADAM_EPS) + ADAM_WD * w), m, v


def _adam_slots(recv, w, m, v, name):
    r, c = w.shape
    slots = recv.shape[0]
    tr = _row_tile(r, c)

    def body(recv_ref, w_ref, m_ref, v_ref, g_ref, d_ref, mo_ref, vo_ref):
        g = recv_ref[0].astype(F32)
        for s in range(1, slots):
            g = g + recv_ref[s].astype(F32)
        g_ref[...] = g
        d_ref[...], mo_ref[...], vo_ref[...] = _adamw(w_ref[...], g, m_ref[...], v_ref[...])

    blk = pl.BlockSpec((tr, c), lambda i: (i, 0))
    return pl.pallas_call(
        body, name=name, grid=(r // tr,),
        in_specs=[pl.BlockSpec((slots, tr, c), lambda i: (0, i, 0)), blk, blk, blk],
        out_shape=[S((r, c), F32)] * 4, out_specs=[blk] * 4, compiler_params=_cp("parallel"),
    )(recv, w, m, v)


def _adam_w_ada(act_t, dada, w, m, v, name):
    r, c = w.shape
    tr = 128
    nb = act_t.shape[1]

    def body(a_ref, da_ref, w_ref, m_ref, v_ref, g_ref, d_ref, mo_ref, vo_ref):
        g = a_ref[:, 0:1] * da_ref[0:1, :]
        for b in range(1, nb):
            g = g + a_ref[:, b:b + 1] * da_ref[b:b + 1, :]
        g_ref[...] = g
        d_ref[...], mo_ref[...], vo_ref[...] = _adamw(w_ref[...], g, m_ref[...], v_ref[...])

    blk = pl.BlockSpec((tr, c), lambda i: (i, 0))
    return pl.pallas_call(
        body, name=name, grid=(r // tr,),
        in_specs=[pl.BlockSpec((tr, nb), lambda i: (i, 0)), pl.BlockSpec((nb, c), lambda i: (0, 0)), blk, blk, blk],
        out_shape=[S((r, c), F32)] * 4, out_specs=[blk] * 4, compiler_params=_cp("parallel"),
    )(act_t, dada, w, m, v)


def _adam_smalls(ws, gs, ms, vs, name):
    n = len(ws)

    def body(*refs):
        w_r, g_r, m_r, v_r = (refs[k * n:(k + 1) * n] for k in range(4))
        d_o, m_o, v_o = (refs[(4 + k) * n:(5 + k) * n] for k in range(3))
        for i in range(n):
            d_o[i][...], m_o[i][...], v_o[i][...] = _adamw(w_r[i][...], g_r[i][...], m_r[i][...], v_r[i][...])

    shapes = [S(w.shape, F32) for w in ws]
    outs = pl.pallas_call(body, name=name, out_shape=shapes * 3, compiler_params=_cp())(*ws, *gs, *ms, *vs)
    return outs[:n], outs[n:2 * n], outs[2 * n:]


def kernel(x, c, w_ada, b_ada, norm_ffn1, w_ffn1_in, w_ffn1_out, norm_mix, w_mix_in, conv_w, w_gk2, b_gk, gla_norm, w_mix_out, norm_ffn2, w_ffn2_in, w_ffn2_out, norm_final, loss_target, m_w_ada, m_b_ada, m_norm_ffn1, m_w_ffn1_in, m_w_ffn1_out, m_norm_mix, m_w_mix_in, m_conv_w, m_w_gk2, m_b_gk, m_gla_norm, m_w_mix_out, m_norm_ffn2, m_w_ffn2_in, m_w_ffn2_out, m_norm_final, v_w_ada, v_b_ada, v_norm_ffn1, v_w_ffn1_in, v_w_ffn1_out, v_norm_mix, v_w_mix_in, v_conv_w, v_w_gk2, v_b_gk, v_gla_norm, v_w_mix_out, v_norm_ffn2, v_w_ffn2_in, v_w_ffn2_out, v_norm_final):
    t, d = x.shape[1], x.shape[2]
    x0, tgt = x[0], loss_target[0]
    rank, kw = w_gk2.shape[1], w_gk2.shape[2] * NDEV
    cw = conv_w.shape[2] * NDEV
    dv = gla_norm.shape[1]
    vw = d - cw
    heads = vw // dv
    mix_cols = w_mix_in.shape[2]
    widths = [3 * cw, 2 * kw, vw, vw, LANES]
    n_proj = 3 * cw + 2 * kw + 2 * vw + rank
    assert n_proj == mix_cols * NDEV and rank <= LANES
    me = 4 * lax.axis_index("x") + 2 * lax.axis_index("y") + lax.axis_index("c")

    core = lax.axis_index("c").astype(jnp.int32).reshape(1)
    bf = lambda a: a[0].astype(BF16)
    bft = lambda a: a[0].T.astype(BF16)
    nb = NDEV // 2

    (ada_row, act_all), ((w1i, cwt_all, wg_all),) = _ada_rows(
        c, w_ada[0], b_ada, "ada_rows", plans=[_plan_gather([bft(w_ffn1_in), conv_w[0], w_gk2[0]])])
    ada9 = ada_row.reshape(N_ADA, d)
    cwt = cwt_all.transpose(1, 0, 2).reshape(conv_w.shape[1], cw)
    wg = jnp.pad(wg_all.transpose(1, 0, 2).reshape(rank, kw), ((0, LANES - rank), (0, 0))).astype(BF16)

    (h1, gu1, s1), ((w1o, wmi),) = _ffn_in(x0, ada9, norm_ffn1, w1i, 0, 1, "ffn1_in", plans=[_plan_gather([bf(w_ffn1_out), bft(w_mix_in)])])
    w1o = w1o.reshape(nb, -1, d)
    wmi = jnp.pad(wmi.reshape(n_proj, d), ((0, sum(widths) - n_proj), (0, 0)))
    w2i_mine = bft(w_ffn2_in)
    quarter = w2i_mine.shape[0] // 4
    part = lambda k, into=None: _plan_gather([w2i_mine], rows=(k * quarter, quarter), into=into)
    (x1, f1, h2), ((wmo,), (w2i,)) = _ffn_out(s1, w1o, x0, ada9, 2, 0.5, norm_mix, 3, 4, "ffn1_out",
                                              plans=[_plan_gather([bf(w_mix_out)]), part(0)])
    wmo = wmo.reshape(cw + vw, d)
    (cx, qk, vv, go, gl), ((w2i,),) = _mix_in(h2, wmi, widths, [BF16, F32, BF16, BF16, F32], "mix_in", plans=[part(1, [w2i])])
    (o, lg, sall), ((w2i,),) = _gla_fwd(qk, vv, gl, wg, b_gk, heads, "gla_fwd", plans=[part(2, [w2i])])
    (x2, mm, ycat, h3), ((w2i,),) = _mix_out(cx, o, go, cwt, gla_norm, wmo, x1, ada9, norm_ffn2, heads, "mix_out", plans=[part(3, [w2i])])
    (h3, gu3, s3), ((w2o,),) = _ffn_in(None, None, None, w2i, 6, 7, "ffn2_in", plans=[_plan_gather([bf(w_ffn2_out)])], h_made=h3)
    w2o = w2o.reshape(nb, -1, d)
    f3, loss_v, dx3, dnf = _ffn_out_loss(s3, w2o, x2, ada9, 8, 0.5, tgt, norm_final.reshape(1, d), "ffn2_out_loss")

    dx2, sum2i, sum2o, (dsh3, dsc3, dg3, dn3), _, _ = _ffn_backward(
        dx3, x2, h3, gu3, f3, ada9, norm_ffn2, w2i, w2o, (6, 7, 8), core, "ffn2_bwd", later=True)
    (dwmo, dyc, dcb, do, dgo, dg2, dcw, dgn), ((r2o,),) = _mix_bwd_a(dx2, mm, ycat, ada9, wmo, cx, o, go, cwt, gla_norm, heads, "mix_bwd_a",
                                                                   plans=[_plan_chip_swap([sum2o])])
    half = sum2i.shape[1] // 2
    (dqk, dvv, dgl, dwg, dbg), ((r2i,),) = _gla_bwd(qk, vv, lg, do, sall, gl, wg, heads, "gla_bwd",
                                                   plans=[_plan_chip_swap([sum2i], rows=(0, half))])
    (dx1, dproj, dsh2, dsc2, dnm), ((r2i,),) = _mix_bwd_b(dyc, cx, dcb, dqk, dvv, dgo, dgl, cwt, wmi, x1, dx2, ada9, norm_mix, "mix_bwd_b",
                                                         plans=[_plan_chip_swap([sum2i], rows=(half, half), into=[r2i])])
    n_pad = sum(widths)
    tn = n_pad // 5
    dwmi, _ = _tn_matmul(dproj, h2, lambda tk: (tk, tn), lambda sb, k: (k, sb), lambda tk: (tk, d), lambda sb, k: (k, 0),
                         (n_pad, d), (tn, d), lambda sb, k: (sb, 0), 5, "mix_dwin")
    dwmi = dwmi[:n_proj].reshape(NDEV, mix_cols, d)
    dwmo = dwmo.reshape(NDEV, -1, d)
    dx0, r1i, (sum1o, r1o, head), (dsh1, dsc1, dg1, dn1), _, ((rmi, rmo),) = _ffn_backward(
        dx1, x0, h1, gu1, f1, ada9, norm_ffn1, w1i, w1o, (0, 1, 2), core, "ffn1_bwd",
        ds_plans=[_plan_sibling_swap([dwmi, dwmo])],
        dwin_plans=lambda moved: [_plan_chip_swap([_pair_add(dwmi, moved[0][0], core, "mix_dwin_add"),
                                                   _pair_add(dwmo, moved[0][1], core, "mix_dwout_add")])])
    pack = _pack_smalls([dn1, dnm, dn3, dnf, dsh1, dsc1, dg1, dsh2, dsc2, dg2, dsh3, dsc3, dg3], dcw, dbg, dgn, dwg, loss_v, rank, "pack_smalls")
    (r1o,), (pack_all,) = _exchange([_plan_chip_swap([sum1o], rows=(head, sum1o.shape[1] - head), into=[r1o]),
                                     _plan_all_to_all([pack], True)], "grads_last")
    tot = _sum_slots(pack_all, "sum_smalls")

    res = {}
    for nm, recv, w, m, v in (("w_ffn1_out", r1o, w_ffn1_out, m_w_ffn1_out, v_w_ffn1_out), ("w_mix_out", rmo, w_mix_out, m_w_mix_out, v_w_mix_out),
                              ("w_ffn2_out", r2o, w_ffn2_out, m_w_ffn2_out, v_w_ffn2_out)):
        res[nm] = [a[None] for a in _adam_slots(recv, w[0], m[0], v[0], "adam_" + nm)]
    for nm, recv, w, m, v in (("w_ffn1_in", r1i, w_ffn1_in, m_w_ffn1_in, v_w_ffn1_in), ("w_mix_in", rmi, w_mix_in, m_w_mix_in, v_w_mix_in),
                              ("w_ffn2_in", r2i, w_ffn2_in, m_w_ffn2_in, v_w_ffn2_in)):
        res[nm] = [a.T[None] for a in _adam_slots(recv, w[0].T, m[0].T, v[0].T, "adam_" + nm)]

    cols_ada = w_ada.shape[2]
    dada_all = pack_all[:, 4:4 + N_ADA, :].reshape(NDEV, N_ADA * d)
    dada_mine = lax.dynamic_slice_in_dim(dada_all, me * cols_ada, cols_ada, axis=1)
    res["w_ada"] = [a[None] for a in _adam_w_ada(act_all.T, dada_mine, w_ada[0], m_w_ada[0], v_w_ada[0], "adam_w_ada")]

    nv = 4 + N_ADA
    g_small = {
        "b_ada": tot[4:nv].reshape(1, N_ADA * d),
        "norm_ffn1": tot[0:1], "norm_mix": tot[1:2], "norm_ffn2": tot[2:3], "norm_final": tot[3:4],
        "conv_w": lax.dynamic_slice_in_dim(
            jnp.concatenate([tot[nv:nv + 1, 0:cw], tot[nv:nv + 1, cw:2 * cw], tot[nv + 1:nv + 2, 0:cw]], axis=0), me * (cw // NDEV), cw // NDEV, axis=1),
        "w_gk2": lax.dynamic_slice_in_dim(tot[nv + 2:nv + 2 + rank * kw // d].reshape(rank, kw), me * (kw // NDEV), kw // NDEV, axis=1),
        "b_gk": tot[nv + 1:nv + 2, cw:cw + kw],
        "gla_norm": tot[nv + 1:nv + 2, cw + kw:cw + kw + dv],
    }
    small = {"b_ada": (b_ada, m_b_ada, v_b_ada), "norm_ffn1": (norm_ffn1, m_norm_ffn1, v_norm_ffn1), "norm_mix": (norm_mix, m_norm_mix, v_norm_mix),
             "norm_ffn2": (norm_ffn2, m_norm_ffn2, v_norm_ffn2), "norm_final": (norm_final, m_norm_final, v_norm_final),
             "conv_w": (conv_w, m_conv_w, v_conv_w), "w_gk2": (w_gk2, m_w_gk2, v_w_gk2), "b_gk": (b_gk, m_b_gk, v_b_gk),
             "gla_norm": (gla_norm, m_gla_norm, v_gla_norm)}
    names = list(small)
    flat = lambda a: a.reshape(-1, a.shape[-1])
    dl, mo, vo = _adam_smalls([flat(small[n][0]) for n in names], [g_small[n] for n in names],
                              [flat(small[n][1]) for n in names], [flat(small[n][2]) for n in names], "adam_smalls")
    for i, n in enumerate(names):
        shp = small[n][0].shape
        res[n] = [g_small[n].reshape(shp), dl[i].reshape(shp), mo[i].reshape(shp), vo[i].reshape(shp)]

    loss = tot[nv + 2 + rank * kw // d, 0]
    order = ["w_ada", "b_ada", "norm_ffn1", "w_ffn1_in", "w_ffn1_out", "norm_mix", "w_mix_in", "conv_w", "w_gk2", "b_gk", "gla_norm",
             "w_mix_out", "norm_ffn2", "w_ffn2_in", "w_ffn2_out", "norm_final"]
    return (loss, dx0[None], *[res[n][0] for n in order], *[res[n][1] for n in order], *[res[n][2] for n in order], *[res[n][3] for n in order])
```

```python
import collections
import functools

import jax
import jax.numpy as jnp
from jax import lax
from jax.experimental import pallas as pl
from jax.experimental.pallas import tpu as pltpu

F32 = jnp.float32
BF16 = jnp.bfloat16
S = jax.ShapeDtypeStruct

NDEV = 8
EPS = 1e-6
GATE_NORMALIZER = 16.0
CHUNK = 128
N_ADA = 9
ADAM_LR, ADAM_B1, ADAM_B2, ADAM_EPS, ADAM_WD, ADAM_STEP = 0.001, 0.9, 0.999, 1e-08, 0.01, 10
V7X_VMEM_LIMIT = 56 * 1024 * 1024
ROW_TILE = 512
WIDE_ROW_TILE = 1024
K_TILE = 1024
EPILOGUE_ROWS = 256
LANES = 128
BF16_ROWS = 16
PACK_ROWS = 24
ANY = pl.BlockSpec(memory_space=pl.ANY)


def _cp(*sem):
    return pltpu.CompilerParams(dimension_semantics=sem or None, vmem_limit_bytes=V7X_VMEM_LIMIT)


def _dot(a, b):
    return jnp.dot(a, b, preferred_element_type=F32)


def _dot_nt(a, b):
    return lax.dot_general(a, b, (((1,), (1,)), ((), ())), preferred_element_type=F32)


def _dot_tn(a, b):
    return lax.dot_general(a, b, (((0,), (0,)), ((), ())), preferred_element_type=F32)


def _rsum8(a):
    r, c = a.shape
    return jnp.sum(a.reshape(r // 8, 8, c), axis=0)


def _row_tile(r, c):
    for cand in (256, 128, 176, 88, 64, 32, 16, 8):
        if r % cand == 0 and cand * c * 4 <= 1024 * 1024:
            return cand
    return r


def _sigmoid(x):
    return 1.0 / (1.0 + jnp.exp(-x))


def _sigmoid_tanh(x):
    return 0.5 * jnp.tanh(0.5 * x) + 0.5


def _normmod(x, nrm, sc, sh):
    rstd = lax.rsqrt(jnp.mean(x * x, axis=-1, keepdims=True) + EPS)
    xhat = x * rstd
    return xhat, rstd, (xhat * nrm) * (1.0 + sc) + sh


def _normmod_bwd(dh, x, nrm, sc):
    rstd = lax.rsqrt(jnp.mean(x * x, axis=-1, keepdims=True) + EPS)
    xhat = x * rstd
    dxhat = dh * (nrm * (1.0 + sc))
    dx = rstd * (dxhat - xhat * jnp.mean(dxhat * xhat, axis=-1, keepdims=True))
    return dx, dh, dh * (xhat * nrm), dh * ((1.0 + sc) * xhat)


def _place():
    x, y, c = lax.axis_index("x"), lax.axis_index("y"), lax.axis_index("c")
    return x, y, c, 4 * x + 2 * y + c


def _peer(x, y, c, k):
    px = 1 - x if k & 4 else x
    py = 1 - y if k & 2 else y
    pc = 1 - c if k & 1 else c
    return (px, py, pc), 4 * px + 2 * py + pc


def _remote(src, dst, send_sem, recv_sem, peer):
    return pltpu.make_async_remote_copy(src_ref=src, dst_ref=dst, send_sem=send_sem, recv_sem=recv_sem,
                                        device_id=peer, device_id_type=pl.DeviceIdType.MESH)


_Plan = collections.namedtuple("_Plan", "inputs out_shapes sem_shapes start finish aliases", defaults=({},))


def _plan_all_to_all(xs, gather):
    n = len(xs)

    def copies(ins, outs, sems, landed):
        send_sems, recv_sems, local_sems = sems
        x, y, c, me = _place()
        local = [pltpu.make_async_copy(ins[i] if gather else ins[i].at[me], outs[i].at[me], local_sems.at[i]) for i in range(n)]
        remote = []
        for k in range(1, NDEV):
            peer, pid = _peer(x, y, c, k)
            for i in range(n):
                remote.append(_remote(ins[i] if gather else ins[i].at[pid], outs[i].at[pid if landed else me],
                                      send_sems.at[i, k - 1], recv_sems.at[i, k - 1], peer))
        return local, remote

    def start(ins, outs, sems):
        local, remote = copies(ins, outs, sems, False)
        for cp in local + remote:
            cp.start()

    def finish(ins, outs, sems):
        local, remote = copies(ins, outs, sems, True)
        for cp in remote + local:
            cp.wait()

    return _Plan(list(xs), [S((NDEV,) + a.shape, a.dtype) if gather else S(a.shape, a.dtype) for a in xs],
                 [pltpu.SemaphoreType.DMA((n, NDEV - 1)), pltpu.SemaphoreType.DMA((n, NDEV - 1)), pltpu.SemaphoreType.DMA((n,))],
                 start, finish)


def _other_chips(x, y):
    return [(1 - x, y), (x, 1 - y), (1 - x, 1 - y)]


def _plan_gather(xs, rows=None, into=None):
    n = len(xs)

    def copies(ins, outs, sems, rest):
        send_sems, recv_sems, local_sems = sems
        x, y, c, me = _place()
        sib, sib_id = (x, y, 1 - c), 4 * x + 2 * y + 1 - c
        chips = _other_chips(x, y)
        mine = lambda i: ins[i] if rows is None else ins[i].at[pl.ds(*rows)]
        slot_of = lambda i, s: outs[i].at[s] if rows is None else outs[i].at[s, pl.ds(*rows)]
        local = [pltpu.make_async_copy(mine(i), slot_of(i, me), local_sems.at[i]) for i in range(n)]
        first = [_remote(mine(i), slot_of(i, me), send_sems.at[i, 0], recv_sems.at[i, 0], sib) for i in range(n)]
        first += [_remote(mine(i), slot_of(i, me), send_sems.at[i, 1 + j], recv_sems.at[i, 1 + j], (px, py, c))
                  for j, (px, py) in enumerate(chips) for i in range(n)]
        if not rest:
            return local, first
        from_sibling = [_remote(mine(i), slot_of(i, sib_id), send_sems.at[i, 0], recv_sems.at[i, 0], sib) for i in range(n)]
        arrive, forward = [], []
        for j, (px, py) in enumerate(chips):
            s = 4 * px + 2 * py
            arrive.append([_remote(mine(i), slot_of(i, s + c), send_sems.at[i, 1 + j], recv_sems.at[i, 1 + j], (px, py, c)) for i in range(n)])
            forward.append([_remote(slot_of(i, s + c), slot_of(i, s + c), send_sems.at[i, 4 + j], recv_sems.at[i, 4 + j], sib) for i in range(n)])
            from_sibling += [_remote(mine(i), slot_of(i, s + 1 - c), send_sems.at[i, 4 + j], recv_sems.at[i, 4 + j], sib) for i in range(n)]
        return local, first, arrive, forward, from_sibling

    def start(ins, outs, sems):
        local, first = copies(ins, outs, sems, False)
        for cp in local + first:
            cp.start()

    def finish(ins, outs, sems):
        local, first, arrive, forward, from_sibling = copies(ins, outs, sems, True)
        for landed, onward in zip(arrive, forward):
            for cp in landed:
                cp.wait_recv()
            for cp in onward:
                cp.start()
        for cp in from_sibling:
            cp.wait_recv()
        for cp in first + [cp for onward in forward for cp in onward]:
            cp.wait_send()
        for cp in local:
            cp.wait()

    return _Plan(list(xs) + list(into or []), [S((NDEV,) + a.shape, a.dtype) for a in xs],
                 [pltpu.SemaphoreType.DMA((n, NDEV - 1)), pltpu.SemaphoreType.DMA((n, NDEV - 1)), pltpu.SemaphoreType.DMA((n,))],
                 start, finish, {n + i: i for i in range(len(into or []))})


def _plan_sibling_swap(gs):
    n = len(gs)

    def copies(ins, outs, sems):
        send_sems, recv_sems = sems
        x, y, c, _ = _place()
        return [_remote(ins[i].at[2 * j + 1 - c], outs[i].at[j], send_sems.at[i, j], recv_sems.at[i, j], (x, y, 1 - c))
                for i in range(n) for j in range(NDEV // 2)]

    def start(ins, outs, sems):
        for cp in copies(ins, outs, sems):
            cp.start()

    def finish(ins, outs, sems):
        for cp in copies(ins, outs, sems):
            cp.wait()

    return _Plan(list(gs), [S((NDEV // 2,) + a.shape[1:], a.dtype) for a in gs],
                 [pltpu.SemaphoreType.DMA((n, NDEV // 2)), pltpu.SemaphoreType.DMA((n, NDEV // 2))], start, finish)


def _pair_add(g, r1, core, name):
    _, r, c = g.shape
    tr = r if r * c * 2 <= 2 * 1024 * 1024 else _row_tile(r, c)

    def body(core_ref, g_ref, r_ref, o_ref):
        o_ref[...] = (g_ref[...].astype(F32) + r_ref[...].astype(F32)).astype(BF16)

    return pl.pallas_call(
        body, name=name,
        grid_spec=pltpu.PrefetchScalarGridSpec(
            num_scalar_prefetch=1, grid=(NDEV // 2, r // tr),
            in_specs=[pl.BlockSpec((None, tr, c), lambda j, k, core_ref: (2 * j + core_ref[0], k, 0)),
                      pl.BlockSpec((None, tr, c), lambda j, k, core_ref: (j, k, 0))],
            out_specs=pl.BlockSpec((None, tr, c), lambda j, k, core_ref: (j, k, 0))),
        out_shape=S((NDEV // 2, r, c), BF16), compiler_params=_cp("parallel", "parallel"),
    )(core, g, r1)


def _plan_chip_swap(ps, rows=None, into=None):
    n = len(ps)

    def copies(ins, outs, sems, landed):
        send_sems, recv_sems, local_sems = sems
        x, y, c, _ = _place()
        mine = 2 * x + y
        part = lambda ref, s: ref.at[s] if rows is None else ref.at[s, pl.ds(*rows)]
        local = [pltpu.make_async_copy(part(ins[i], mine), part(outs[i], mine), local_sems.at[i]) for i in range(n)]
        remote = [_remote(part(ins[i], 2 * px + py), part(outs[i], 2 * px + py if landed else mine),
                          send_sems.at[i, j], recv_sems.at[i, j], (px, py, c))
                  for j, (px, py) in enumerate(_other_chips(x, y)) for i in range(n)]
        return local, remote

    def start(ins, outs, sems):
        local, remote = copies(ins, outs, sems, False)
        for cp in local + remote:
            cp.start()

    def finish(ins, outs, sems):
        local, remote = copies(ins, outs, sems, True)
        for cp in remote + local:
            cp.wait()

    return _Plan(list(ps) + list(into or []), [S(a.shape, a.dtype) for a in ps],
                 [pltpu.SemaphoreType.DMA((n, 3)), pltpu.SemaphoreType.DMA((n, 3)), pltpu.SemaphoreType.DMA((n,))], start, finish,
                 {n + i: i for i in range(len(into or []))})


def _pcall(body, name, args, in_specs, out_shape, out_specs, grid=(), scratch_shapes=(), sem=(), plans=(), starts_plans=False):
    n_in, n_out, n_scr = len(args), len(out_shape), len(scratch_shapes)
    counts = [(len(p.inputs), len(p.out_shapes), len(p.sem_shapes)) for p in plans]
    c_args = [a for p in plans for a in p.inputs]
    c_outs = [s for p in plans for s in p.out_shapes]
    c_sems = [s for p in plans for s in p.sem_shapes]

    def wrapped(*refs):
        cuts = [n_in, len(c_args), n_out, len(c_outs), n_scr, len(c_sems)]
        ins, c_in, outs, c_out, scr, c_sem = [refs[sum(cuts[:k]):sum(cuts[:k + 1])] for k in range(6)]

        def halves(which):
            a = b = s = 0
            for p, (na, nb, ns) in zip(plans, counts):
                getattr(p, which)(c_in[a:a + na], c_out[b:b + nb], c_sem[s:s + ns])
                a, b, s = a + na, b + nb, s + ns

        if not plans:
            body(*ins, *outs, *scr)
        elif not grid and starts_plans:
            body(lambda: halves("start"), *ins, *outs, *scr)
            halves("finish")
        elif not grid:
            halves("start")
            body(*ins, *outs, *scr)
            halves("finish")
        else:
            first = functools.reduce(jnp.logical_and, [pl.program_id(a) == 0 for a in range(len(grid))])
            last = functools.reduce(jnp.logical_and, [pl.program_id(a) == grid[a] - 1 for a in range(len(grid))])
            pl.when(first)(lambda: halves("start"))
            body(*ins, *outs, *scr)
            pl.when(last)(lambda: halves("finish"))

    aliases, a, b = {}, n_in, n_out
    for p, (na, nb, _) in zip(plans, counts):
        aliases.update({a + k: b + v for k, v in p.aliases.items()})
        a, b = a + na, b + nb
    res = pl.pallas_call(
        wrapped, name=name, grid=grid, in_specs=list(in_specs) + [ANY] * len(c_args),
        out_shape=list(out_shape) + c_outs, out_specs=list(out_specs) + [ANY] * len(c_outs),
        scratch_shapes=list(scratch_shapes) + c_sems, input_output_aliases=aliases,
        compiler_params=_cp(*(("arbitrary",) * len(grid) if plans else sem)),
    )(*args, *c_args)
    c_res, b = [], n_out
    for _, nb, _ in counts:
        c_res.append(res[b:b + nb])
        b += nb
    return res[:n_out], c_res


def _exchange(plans, name):
    return _pcall(lambda: None, name, [], [], [], [], plans=plans)[1]


def _ada_rows(c, w_ada, b_ada, name, plans=()):
    d, cols = c.shape[1], w_ada.shape[1]
    gather_c = _plan_all_to_all([c], True)
    gather_p = _plan_all_to_all([S((NDEV, cols), F32)], True)
    n_sem = len(gather_c.sem_shapes)

    def body(start_plans, c_ref, w_ref, b_ref, ada_ref, act_ref, c_all, p_mine, p_all, *sems):
        gather_c.start([c_ref], [c_all], sems[:n_sem])
        gather_c.finish([c_ref], [c_all], sems[:n_sem])
        for s in range(NDEV):
            cc = c_all[s]
            act_ref[s:s + 1, :] = cc * _sigmoid(cc)
        p_mine[...] = _dot(act_ref[...].astype(BF16), w_ref[...].astype(BF16))
        gather_p.start([p_mine], [p_all], sems[n_sem:])
        start_plans()
        gather_p.finish([p_mine], [p_all], sems[n_sem:])
        me = _place()[3]
        for s in range(NDEV):
            ada_ref[:, s * cols:(s + 1) * cols] = p_all[s, pl.ds(me, 1), :] + b_ref[:, s * cols:(s + 1) * cols]

    whole = pl.BlockSpec(memory_space=pltpu.VMEM)
    return _pcall(body, name, [c, w_ada, b_ada], [whole] * 3, [S((1, NDEV * cols), F32), S((NDEV, d), F32)], [whole] * 2,
                  scratch_shapes=[pltpu.VMEM((NDEV,) + c.shape, F32), pltpu.VMEM((NDEV, cols), F32), pltpu.VMEM((NDEV, NDEV, cols), F32)]
                  + gather_c.sem_shapes + gather_p.sem_shapes, plans=plans, starts_plans=True)


def _ffn_in(x, ada9, nrm, w_in, sh_row, sc_row, name, plans=(), h_made=None):
    t, d = (x if h_made is None else h_made).shape
    nb, bw = w_in.shape[0] // 2, w_in.shape[1]
    tm = min(WIDE_ROW_TILE, t)

    def body(*refs):
        if h_made is None:
            x_ref, ada_ref, n_ref, wg_ref, wu_ref, h_ref, gu_ref, s_ref = refs

            @pl.when(pl.program_id(1) == 0)
            def _():
                _, _, h = _normmod(x_ref[...], n_ref[...], ada_ref[sc_row:sc_row + 1, :], ada_ref[sh_row:sh_row + 1, :])
                h_ref[...] = h.astype(BF16)
        else:
            h_ref, wg_ref, wu_ref, gu_ref, s_ref = refs

        h = h_ref[...]
        g = _dot_nt(h, wg_ref[...])
        u = _dot_nt(h, wu_ref[...])
        gu_ref[0] = g.astype(BF16)
        gu_ref[1] = u.astype(BF16)
        s_ref[...] = (g * _sigmoid(g) * u).astype(BF16)

    rows = pl.BlockSpec((tm, d), lambda i, j: (i, 0))
    weights = [pl.BlockSpec((None, bw, d), lambda i, j: (j, 0, 0)), pl.BlockSpec((None, bw, d), lambda i, j: (j + nb, 0, 0))]
    results = ([S((2, nb, t, bw), BF16), S((nb, t, bw), BF16)],
               [pl.BlockSpec((2, None, tm, bw), lambda i, j: (0, j, i, 0)), pl.BlockSpec((None, tm, bw), lambda i, j: (j, i, 0))])
    if h_made is not None:
        (gu, s), moved = _pcall(body, name, [h_made, w_in, w_in], [rows] + weights, results[0], results[1], grid=(t // tm, nb),
                                sem=("parallel", "arbitrary"), plans=plans)
        return (h_made, gu, s), moved
    return _pcall(
        body, name, [x, ada9, nrm, w_in, w_in], grid=(t // tm, nb),
        in_specs=[rows, pl.BlockSpec((N_ADA, d), lambda i, j: (0, 0)), pl.BlockSpec((1, d), lambda i, j: (0, 0))] + weights,
        out_shape=[S((t, d), BF16)] + results[0], out_specs=[rows] + results[1],
        sem=("parallel", "arbitrary"), plans=plans)


def _ffn_out(s, w_out, x, ada9, g_row, res_scale, name, plans=()):
    nb, t, bw = s.shape
    d = x.shape[1]
    tm = min(ROW_TILE, t)

    def body(s_ref, w_ref, x_ref, ada_ref, xo_ref, f_ref):
        acc = _dot(s_ref[0], w_ref[0])
        for b in range(1, nb):
            acc = acc + _dot(s_ref[b], w_ref[b])
        f_ref[...] = acc.astype(BF16)
        xo_ref[...] = x_ref[...] + (res_scale * ada_ref[g_row:g_row + 1, :]) * acc

    return _pcall(
        body, name, [s, w_out, x, ada9], grid=(t // tm,),
        in_specs=[pl.BlockSpec((nb, tm, bw), lambda i: (0, i, 0)), pl.BlockSpec((nb, bw, d), lambda i: (0, 0, 0)),
                  pl.BlockSpec((tm, d), lambda i: (i, 0)), pl.BlockSpec((N_ADA, d), lambda i: (0, 0))],
        out_shape=[S((t, d), F32), S((t, d), BF16)],
        out_specs=[pl.BlockSpec((tm, d), lambda i: (i, 0)), pl.BlockSpec((tm, d), lambda i: (i, 0))],
        sem=("parallel",), plans=plans)


def _ffn_bwd_ds(dxo, f, ada9, w_out, gu, g_row, res_scale, name, plans=()):
    t, d = dxo.shape
    nb, bw = w_out.shape[0], w_out.shape[1]
    tm = min(WIDE_ROW_TILE, t)
    ni = t // tm

    def body(dxo_ref, f_ref, ada_ref, w_ref, gu_ref, da_ref, dg_ref, dw_ref, df_ref, acc_ref):
        i, j = pl.program_id(0), pl.program_id(1)

        @pl.when((i == 0) & (j == 0))
        def _():
            dg_ref[...] = jnp.zeros_like(dg_ref)

        @pl.when(j == 0)
        def _():
            dxo_t = dxo_ref[...]
            df_ref[...] = ((res_scale * ada_ref[g_row:g_row + 1, :]) * dxo_t).astype(BF16)
            dg_ref[...] += res_scale * _rsum8(dxo_t * f_ref[...].astype(F32))

        df = df_ref[...]
        ds = _dot_nt(df, w_ref[...])
        g = gu_ref[0].astype(F32)
        u = gu_ref[1].astype(F32)
        sg = _sigmoid_tanh(g)
        silu = g * sg
        da_ref[0] = (ds * u * (sg * (1.0 + g * (1.0 - sg)))).astype(BF16)
        da_ref[1] = (ds * silu).astype(BF16)
        part = _dot_tn((silu * u).astype(BF16), df)

        @pl.when(i == 0)
        def _():
            acc_ref[j] = part

        @pl.when(i > 0)
        def _():
            acc_ref[j] += part

        @pl.when(i == ni - 1)
        def _():
            dw_ref[...] = acc_ref[j].astype(BF16)

    return _pcall(
        body, name, [dxo, f, ada9, w_out, gu], grid=(ni, nb),
        in_specs=[pl.BlockSpec((tm, d), lambda i, j: (i, 0)), pl.BlockSpec((tm, d), lambda i, j: (i, 0)),
                  pl.BlockSpec((N_ADA, d), lambda i, j: (0, 0)), pl.BlockSpec((None, bw, d), lambda i, j: (j, 0, 0)),
                  pl.BlockSpec((2, None, tm, bw), lambda i, j: (0, j, i, 0))],
        out_shape=[S((2, nb, t, bw), BF16), S((8, d), F32), S((nb, bw, d), BF16)],
        out_specs=[pl.BlockSpec((2, None, tm, bw), lambda i, j: (0, j, i, 0)), pl.BlockSpec((8, d), lambda i, j: (0, 0)),
                   pl.BlockSpec((None, bw, d), lambda i, j: (jnp.where(i == ni - 1, j, 0), 0, 0))],
        scratch_shapes=[pltpu.VMEM((tm, d), BF16), pltpu.VMEM((nb, bw, d), F32)],
        sem=("arbitrary", "arbitrary"), plans=plans)


def _ffn_bwd_dh(da, w_in, x, dxo, ada9, nrm, sh_row, sc_row, name, plans=()):
    t, d = x.shape
    nb, bw = w_in.shape[0] // 2, w_in.shape[1]
    tm = min(WIDE_ROW_TILE, t)

    def body(da_ref, wg_ref, wu_ref, x_ref, dxo_ref, ada_ref, n_ref, dx_ref, dsh_ref, dsc_ref, dn_ref, acc_ref):
        i, j = pl.program_id(0), pl.program_id(1)

        @pl.when((i == 0) & (j == 0))
        def _():
            dsh_ref[...] = jnp.zeros_like(dsh_ref)
            dsc_ref[...] = jnp.zeros_like(dsc_ref)
            dn_ref[...] = jnp.zeros_like(dn_ref)

        part = _dot(da_ref[0], wg_ref[...]) + _dot(da_ref[1], wu_ref[...])

        @pl.when(j == 0)
        def _():
            acc_ref[...] = part

        @pl.when(j > 0)
        def _():
            acc_ref[...] += part

        @pl.when(j == nb - 1)
        def _():
            for r0 in range(0, tm, min(EPILOGUE_ROWS, tm)):
                rows = slice(r0, r0 + min(EPILOGUE_ROWS, tm))
                dx, tsh, tsc, tn = _normmod_bwd(acc_ref[rows, :], x_ref[rows, :], n_ref[...], ada_ref[sc_row:sc_row + 1, :])
                dx_ref[rows, :] = dxo_ref[rows, :] + dx
                dsh_ref[...] += _rsum8(tsh)
                dsc_ref[...] += _rsum8(tsc)
                dn_ref[...] += _rsum8(tn)

    vec = pl.BlockSpec((8, d), lambda i, j: (0, 0))
    return _pcall(
        body, name, [da, w_in, w_in, x, dxo, ada9, nrm], grid=(t // tm, nb),
        in_specs=[pl.BlockSpec((2, None, tm, bw), lambda i, j: (0, j, i, 0)),
                  pl.BlockSpec((None, bw, d), lambda i, j: (j, 0, 0)), pl.BlockSpec((None, bw, d), lambda i, j: (j + nb, 0, 0)),
                  pl.BlockSpec((tm, d), lambda i, j: (i, 0)), pl.BlockSpec((tm, d), lambda i, j: (i, 0)),
                  pl.BlockSpec((N_ADA, d), lambda i, j: (0, 0)), pl.BlockSpec((1, d), lambda i, j: (0, 0))],
        out_shape=[S((t, d), F32), S((8, d), F32), S((8, d), F32), S((8, d), F32)],
        out_specs=[pl.BlockSpec((tm, d), lambda i, j: (i, 0)), vec, vec, vec],
        scratch_shapes=[pltpu.VMEM((tm, d), F32)],
        sem=("arbitrary", "arbitrary"), plans=plans)


def _tn_matmul(a, b, a_block, a_map, b_block, b_map, out_shape, out_block, out_map, nblk, name, plans=()):
    t = a.shape[-2]
    tk = min(K_TILE, t)
    nk = t // tk

    def body(a_ref, b_ref, o_ref, acc_ref):
        k = pl.program_id(1)
        for q in (range(a_ref.shape[0]) if len(a_ref.shape) == 3 else [Ellipsis]):
            part = _dot_tn(a_ref[q], b_ref[...])

            @pl.when(k == 0)
            def _():
                acc_ref[q] = part

            @pl.when(k > 0)
            def _():
                acc_ref[q] += part

        @pl.when(k == nk - 1)
        def _():
            o_ref[...] = acc_ref[...].astype(BF16)

    (out,), moved = _pcall(
        body, name, [a, b], grid=(nblk, nk),
        in_specs=[pl.BlockSpec(a_block(tk), a_map), pl.BlockSpec(b_block(tk), b_map)],
        out_shape=[S(out_shape, BF16)], out_specs=[pl.BlockSpec(out_block, out_map)],
        scratch_shapes=[pltpu.VMEM(tuple(n for n in out_block if n is not None), F32)],
        sem=("parallel", "arbitrary"), plans=plans)
    return out, moved


def _ffn_backward(dxo, x_in, h, gu, f, ada9, nrm, w_in, w_out, rows, core, name, ds_plans=(), dwin_plans=(), later=False):
    sh_row, sc_row, g_row = rows
    _, nb, t, bw = gu.shape
    d = x_in.shape[1]
    (da, dg, dw_out), ds_moved = _ffn_bwd_ds(dxo, f, ada9, w_out, gu, g_row, 0.5, name + "_ds", plans=ds_plans)
    dw_out = dw_out.reshape(NDEV, -1, d)
    dw_in, ((half_out,), *dwin_moved) = _tn_matmul(
        da.reshape(2 * nb, t, bw), h, lambda tk: (2, tk, bw), lambda sb, k: (sb, k, 0), lambda tk: (tk, d), lambda sb, k: (k, 0),
        (2 * nb, bw, d), (2, bw, d), lambda sb, k: (sb, 0, 0), nb, name + "_dwin",
        plans=[_plan_sibling_swap([dw_out])] + list(dwin_plans(ds_moved) if callable(dwin_plans) else dwin_plans))
    sum_out = _pair_add(dw_out, half_out, core, name + "_dwout_add")
    if later:
        (dx, dsh, dsc, dn), ((half_in,),) = _ffn_bwd_dh(da, w_in, x_in, dxo, ada9, nrm, sh_row, sc_row, name + "_dh",
                                                       plans=[_plan_sibling_swap([dw_in])])
        return dx, _pair_add(dw_in, half_in, core, name + "_dwin_add"), sum_out, (dsh, dsc, dg, dn), ds_moved, dwin_moved
    ((half_in,),) = _exchange([_plan_sibling_swap([dw_in])], name + "_dwin_swap")
    sum_in = _pair_add(dw_in, half_in, core, name + "_dwin_add")
    head = (3 * sum_out.shape[1] // 4) // BF16_ROWS * BF16_ROWS
    (dx, dsh, dsc, dn), ((recv_in,), (out_begun,)) = _ffn_bwd_dh(
        da, w_in, x_in, dxo, ada9, nrm, sh_row, sc_row, name + "_dh",
        plans=[_plan_chip_swap([sum_in]), _plan_chip_swap([sum_out], rows=(0, head))])
    return dx, recv_in, (sum_out, out_begun, head), (dsh, dsc, dg, dn), ds_moved, dwin_moved


def _mix_in(x, ada9, nrm, w, widths, dts, name, plans=()):
    t, d = x.shape
    n = w.shape[0]
    tm = min(ROW_TILE, t)
    starts = [sum(widths[:i]) for i in range(len(widths))]

    def body(x_ref, ada_ref, n_ref, w_ref, h_ref, *out_refs):
        _, _, h = _normmod(x_ref[...], n_ref[...], ada_ref[4:5, :], ada_ref[3:4, :])
        hb = h.astype(BF16)
        h_ref[...] = hb
        for o_ref, st, wd in zip(out_refs, starts, widths):
            o_ref[...] = _dot_nt(hb, w_ref[st:st + wd, :]).astype(o_ref.dtype)

    return _pcall(
        body, name, [x, ada9, nrm, w], grid=(t // tm,),
        in_specs=[pl.BlockSpec((tm, d), lambda i: (i, 0)), pl.BlockSpec((N_ADA, d), lambda i: (0, 0)),
                  pl.BlockSpec((1, d), lambda i: (0, 0)), pl.BlockSpec((n, d), lambda i: (0, 0))],
        out_shape=[S((t, d), BF16)] + [S((t, wd), dt) for wd, dt in zip(widths, dts)],
        out_specs=[pl.BlockSpec((tm, d), lambda i: (i, 0))] + [pl.BlockSpec((tm, wd), lambda i: (i, 0)) for wd in widths],
        sem=("parallel",), plans=plans)


def _tri(lower):
    r = lax.broadcasted_iota(jnp.int32, (CHUNK, CHUNK), 0)
    c = lax.broadcasted_iota(jnp.int32, (CHUNK, CHUNK), 1)
    return (r >= c) if lower else (c >= r)


def _dot_01(m, x):
    hi = x.astype(BF16)
    r1 = x - hi.astype(F32)
    mid = r1.astype(BF16)
    lo = (r1 - mid.astype(F32)).astype(BF16)
    return _dot(m, hi) + _dot(m, mid) + _dot(m, lo)


def _gla_chunk_terms(q, k, lg, low01):
    b = _dot_01(low01, lg)
    bl = b[CHUNK - 1:CHUNK, :]
    r = 0.5 * bl
    eb, ebl, em, en = jnp.exp(b), jnp.exp(bl - b), jnp.exp(b - r), jnp.exp(r - b)
    return eb, ebl, em, en, jnp.exp(bl), q * eb, k * ebl, q * em, k * en


def _scores(qm_h, knp, qk1_h):
    r = lax.broadcasted_iota(jnp.int32, (CHUNK, CHUNK), 0)
    c = lax.broadcasted_iota(jnp.int32, (CHUNK, CHUNK), 1)
    p = jnp.where(r > c, _dot_nt(qm_h, knp), 0.0)
    return jnp.where(r == c, jnp.sum(qk1_h, axis=1, keepdims=True), p)


def _gla_fwd(qk, v, gl, wg, bg, heads, name, plans=()):
    t = qk.shape[0]
    kw, vw = qk.shape[1] // 2, v.shape[1]
    dk, dv = kw // heads, vw // heads
    assert dk == 64 and dv == 128 and kw % 128 == 0
    gt = min(ROW_TILE, t)
    nc = gt // CHUNK
    scale = dk ** -0.5

    def body(qk_ref, v_ref, gl_ref, wg_ref, bg_ref, o_ref, lg_ref, sall_ref, st_ref):
        @pl.when(pl.program_id(0) == 0)
        def _():
            st_ref[...] = jnp.zeros_like(st_ref)

        gk = _dot(gl_ref[...].astype(BF16), wg_ref[...]) + bg_ref[...]
        lg_ref[...] = (jnp.minimum(gk, 0.0) - jnp.log(1.0 + jnp.exp(-jnp.abs(gk)))) / GATE_NORMALIZER
        low01 = _tri(True).astype(BF16)
        lane = lax.broadcasted_iota(jnp.int32, (CHUNK, LANES), 1)

        def chunk(ci, carry):
            rows = pl.ds(pl.multiple_of(ci * CHUNK, CHUNK), CHUNK)
            q = qk_ref[rows, 0:kw] * scale
            k = qk_ref[rows, kw:2 * kw]
            qk1 = q.astype(BF16).astype(F32) * k.astype(BF16).astype(F32)
            eb, ebl, em, en, ebl_row, qe, ke, qm, kn = _gla_chunk_terms(q, k, lg_ref[rows, :], low01)
            for h in range(heads):
                lanes = slice(LANES * (h // 2), LANES * (h // 2) + LANES)
                own = (lane < 64) if h % 2 == 0 else (lane >= 64)
                knp = kn[:, lanes].astype(BF16)
                qm_h = jnp.where(own, qm[:, lanes], 0.0).astype(BF16)
                qe_h = jnp.where(own, qe[:, lanes], 0.0).astype(BF16)
                ke_h = jnp.where(own, ke[:, lanes], 0.0).astype(BF16)
                v_h = v_ref[rows, h * dv:(h + 1) * dv]
                st = st_ref[h]
                sall_ref[ci, h] = st
                p = _scores(qm_h, knp, jnp.where(own, qk1[:, lanes], 0.0))
                o_ref[rows, h * dv:(h + 1) * dv] = _dot(p.astype(BF16), v_h) + _dot_nt(qe_h, st.astype(BF16))
                st_ref[h] = st * ebl_row[:, lanes] + _dot_tn(v_h, ke_h)
            return carry

        lax.fori_loop(0, nc, chunk, 0, unroll=True)

    return _pcall(
        body, name, [qk, v, gl, wg, bg], grid=(t // gt,),
        in_specs=[pl.BlockSpec((gt, 2 * kw), lambda i: (i, 0)), pl.BlockSpec((gt, vw), lambda i: (i, 0)),
                  pl.BlockSpec((gt, LANES), lambda i: (i, 0)), pl.BlockSpec((LANES, kw), lambda i: (0, 0)),
                  pl.BlockSpec((1, kw), lambda i: (0, 0))],
        out_shape=[S((t, vw), F32), S((t, kw), F32), S((t // CHUNK, heads, dv, LANES), F32)],
        out_specs=[pl.BlockSpec((gt, vw), lambda i: (i, 0)), pl.BlockSpec((gt, kw), lambda i: (i, 0)),
                   pl.BlockSpec((nc, heads, dv, LANES), lambda i: (i, 0, 0, 0))],
        scratch_shapes=[pltpu.VMEM((heads, dv, LANES), F32)],
        sem=("arbitrary",), plans=plans)


def _gla_bwd(qk, v, lg, do, sall, gl, wg, heads, name, plans=()):
    t = qk.shape[0]
    kw, vw = qk.shape[1] // 2, v.shape[1]
    dk, dv = kw // heads, vw // heads
    gt = min(ROW_TILE, t)
    nc = gt // CHUNK
    nt = t // gt
    scale = dk ** -0.5

    def body(qk_ref, v_ref, lg_ref, do_ref, sall_ref, gl_ref, wg_ref, dqk_ref, dv_ref, dgl_ref, dwg_ref, dbg_ref, dst_ref, dgk_ref):
        @pl.when(pl.program_id(0) == 0)
        def _():
            dst_ref[...] = jnp.zeros_like(dst_ref)
            dwg_ref[...] = jnp.zeros_like(dwg_ref)
            dbg_ref[...] = jnp.zeros_like(dbg_ref)

        low01 = _tri(True).astype(BF16)
        up01 = _tri(False).astype(BF16)
        causal = _tri(True)
        lane = lax.broadcasted_iota(jnp.int32, (CHUNK, LANES), 1)
        last_row = lax.broadcasted_iota(jnp.int32, (CHUNK, kw), 0) == CHUNK - 1

        def chunk(cj, carry):
            ci = nc - 1 - cj
            rows = pl.ds(pl.multiple_of(ci * CHUNK, CHUNK), CHUNK)
            q = qk_ref[rows, 0:kw] * scale
            k = qk_ref[rows, kw:2 * kw]
            qk1 = q.astype(BF16).astype(F32) * k.astype(BF16).astype(F32)
            lgc = lg_ref[rows, :]
            eb, ebl, em, en, ebl_row, qe, ke, qm, kn = _gla_chunk_terms(q, k, lgc, low01)
            dqe, dqm, dkn, dke, drow = [], [], [], [], []
            for pr in range(kw // LANES):
                lanes = slice(LANES * pr, LANES * pr + LANES)
                knp = kn[:, lanes].astype(BF16)
                parts = []
                for half in range(2):
                    h = 2 * pr + half
                    own = (lane < 64) if half == 0 else (lane >= 64)
                    qm_h = jnp.where(own, qm[:, lanes], 0.0).astype(BF16)
                    qe_h = jnp.where(own, qe[:, lanes], 0.0).astype(BF16)
                    ke_h = jnp.where(own, ke[:, lanes], 0.0).astype(BF16)
                    v_h = v_ref[rows, h * dv:(h + 1) * dv]
                    do_h = do_ref[rows, h * dv:(h + 1) * dv]
                    st = sall_ref[ci, h]
                    dst = dst_ref[h]
                    stb, dstb = st.astype(BF16), dst.astype(BF16)
                    p = _scores(qm_h, knp, jnp.where(own, qk1[:, lanes], 0.0)).astype(BF16)
                    dp = jnp.where(causal, _dot_nt(do_h, v_h), 0.0).astype(BF16)
                    dv_ref[rows, h * dv:(h + 1) * dv] = (_dot_tn(p, do_h) + _dot_nt(ke_h, dstb)).astype(BF16)
                    parts.append((jnp.where(own, _dot(dp, knp), 0.0), _dot_tn(dp, qm_h), _dot(do_h, stb), _dot(v_h, dstb),
                                  jnp.sum(st * dst, axis=0, keepdims=True)))
                    dst_ref[h] = dst * ebl_row[:, lanes] + _dot_tn(do_h, qe_h)
                dqm.append(parts[0][0] + parts[1][0])
                dkn.append(parts[0][1] + parts[1][1])
                dqe.append(parts[0][2] + parts[1][2])
                dke.append(parts[0][3] + parts[1][3])
                drow.append(parts[0][4] + parts[1][4])
            dqm, dkn, dqe, dke, drow = [jnp.concatenate(a, axis=1) for a in (dqm, dkn, dqe, dke, drow)]
            dqk_ref[rows, 0:kw] = ((dqe * eb + dqm * em) * scale).astype(BF16)
            dqk_ref[rows, kw:2 * kw] = (dke * ebl + dkn * en).astype(BF16)
            tke = dke * ke
            db = dqe * qe + dqm * qm - dkn * kn - tke
            dbl = jnp.sum(tke, axis=0, keepdims=True) + drow * ebl_row
            db = db + jnp.where(last_row, dbl, 0.0)
            dlg = _dot_01(up01, db)
            dgk_ref[rows, :] = dlg * ((1.0 - jnp.exp(GATE_NORMALIZER * lgc)) / GATE_NORMALIZER)
            return carry

        lax.fori_loop(0, nc, chunk, 0, unroll=True)
        dgk = dgk_ref[...]
        dgkb = dgk.astype(BF16)
        dgl_ref[...] = _dot_nt(dgkb, wg_ref[...]).astype(BF16)
        dwg_ref[...] += _dot_tn(gl_ref[...].astype(BF16), dgkb)
        dbg_ref[...] += _rsum8(dgk)

    rev = lambda i: (nt - 1 - i, 0)
    return _pcall(
        body, name, [qk, v, lg, do, sall, gl, wg], grid=(nt,),
        in_specs=[pl.BlockSpec((gt, 2 * kw), rev), pl.BlockSpec((gt, vw), rev), pl.BlockSpec((gt, kw), rev),
                  pl.BlockSpec((gt, vw), rev), pl.BlockSpec((nc, heads, dv, LANES), lambda i: (nt - 1 - i, 0, 0, 0)),
                  pl.BlockSpec((gt, LANES), rev), pl.BlockSpec((LANES, kw), lambda i: (0, 0))],
        out_shape=[S((t, 2 * kw), BF16), S((t, vw), BF16), S((t, LANES), BF16), S((LANES, kw), F32), S((8, kw), F32)],
        out_specs=[pl.BlockSpec((gt, 2 * kw), rev), pl.BlockSpec((gt, vw), rev), pl.BlockSpec((gt, LANES), rev),
                   pl.BlockSpec((LANES, kw), lambda i: (0, 0)), pl.BlockSpec((8, kw), lambda i: (0, 0))],
        scratch_shapes=[pltpu.VMEM((heads, dv, LANES), F32), pltpu.VMEM((gt, kw), F32)],
        sem=("arbitrary",), plans=plans)


def _conv_taps(cx_ref, halo_ref, first, cw):
    tm = cx_ref.shape[0]
    u = cx_ref[:, cw:2 * cw].astype(F32) * cx_ref[:, 2 * cw:3 * cw].astype(F32)
    uh = halo_ref[:, cw:2 * cw].astype(F32) * halo_ref[:, 2 * cw:3 * cw].astype(F32)
    uh = jnp.where(first, 0.0, uh)
    before1, before2 = uh[BF16_ROWS - 1:BF16_ROWS, :], uh[BF16_ROWS - 2:BF16_ROWS - 1, :]
    row = lax.broadcasted_iota(jnp.int32, (tm, cw), 0)
    u1 = jnp.where(row == 0, before1, pltpu.roll(u, 1, 0))
    u2 = jnp.where(row == 0, before2, jnp.where(row == 1, before1, pltpu.roll(u, 2, 0)))
    return u, u1, u2


def _head_norm(o_h, gn):
    rstd = lax.rsqrt(jnp.mean(o_h * o_h, axis=-1, keepdims=True) + EPS)
    ohat = o_h * rstd
    return ohat, rstd, ohat * gn


def _mix_out(cx, o, go, conv_w, gn, w_out, x, ada9, nrm_next, heads, name, plans=()):
    t, d = x.shape
    cw, vw = conv_w.shape[1], o.shape[1]
    dv = vw // heads
    tm = min(ROW_TILE, t)

    def body(cx_ref, halo_ref, o_ref, go_ref, cwt_ref, gn_ref, w_ref, x_ref, ada_ref, nn_ref, xo_ref, m_ref, y_ref, hn_ref):
        u, u1, u2 = _conv_taps(cx_ref, halo_ref, pl.program_id(0) == 0, cw)
        yc = cwt_ref[0:1, :] * u2 + cwt_ref[1:2, :] * u1 + cwt_ref[2:3, :] * u
        y_ref[:, 0:cw] = (cx_ref[:, 0:cw].astype(F32) * yc).astype(BF16)
        for h in range(heads):
            cols = slice(h * dv, (h + 1) * dv)
            _, _, on = _head_norm(o_ref[:, cols], gn_ref[...])
            g = go_ref[:, cols].astype(F32)
            y_ref[:, cw + h * dv:cw + (h + 1) * dv] = (on * (g * _sigmoid(g))).astype(BF16)
        m = _dot(y_ref[...], w_ref[...])
        m_ref[...] = m.astype(BF16)
        xo = x_ref[...] + ada_ref[5:6, :] * m
        xo_ref[...] = xo
        hn_ref[...] = _normmod(xo, nn_ref[...], ada_ref[7:8, :], ada_ref[6:7, :])[2].astype(BF16)

    return _pcall(
        body, name, [cx, cx, o, go, conv_w, gn, w_out, x, ada9, nrm_next], grid=(t // tm,),
        in_specs=[pl.BlockSpec((tm, 3 * cw), lambda i: (i, 0)),
                  pl.BlockSpec((BF16_ROWS, 3 * cw), lambda i: (jnp.maximum(i * (tm // BF16_ROWS) - 1, 0), 0)),
                  pl.BlockSpec((tm, vw), lambda i: (i, 0)), pl.BlockSpec((tm, vw), lambda i: (i, 0)),
                  pl.BlockSpec((3, cw), lambda i: (0, 0)), pl.BlockSpec((1, dv), lambda i: (0, 0)),
                  pl.BlockSpec((cw + vw, d), lambda i: (0, 0)), pl.BlockSpec((tm, d), lambda i: (i, 0)),
                  pl.BlockSpec((N_ADA, d), lambda i: (0, 0)), pl.BlockSpec((1, d), lambda i: (0, 0))],
        out_shape=[S((t, d), F32), S((t, d), BF16), S((t, cw + vw), BF16), S((t, d), BF16)],
        out_specs=[pl.BlockSpec((tm, d), lambda i: (i, 0)), pl.BlockSpec((tm, d), lambda i: (i, 0)),
                   pl.BlockSpec((tm, cw + vw), lambda i: (i, 0)), pl.BlockSpec((tm, d), lambda i: (i, 0))],
        sem=("parallel",), plans=plans)


def _mix_bwd_a(dxo, m, ycat, ada9, w_out, cx, o, go, conv_w, gn, heads, name, plans=()):
    t, d = dxo.shape
    cw, vw = conv_w.shape[1], o.shape[1]
    dv = vw // heads
    tm = min(ROW_TILE, t)
    nt = t // tm

    def body(dxo_ref, m_ref, y_ref, ada_ref, w_ref, cx_ref, halo_ref, o_ref, go_ref, cwt_ref, gn_ref,
             dw_ref, dyc_ref, dcb_ref, do_ref, dgo_ref, dg_ref, dcw_ref, dgn_ref, acc_ref):
        @pl.when(pl.program_id(0) == 0)
        def _():
            dg_ref[...] = jnp.zeros_like(dg_ref)
            dcw_ref[...] = jnp.zeros_like(dcw_ref)
            dgn_ref[...] = jnp.zeros_like(dgn_ref)

        dxo_t = dxo_ref[...]
        dmb = (ada_ref[5:6, :] * dxo_t).astype(BF16)
        part = _dot_tn(y_ref[...], dmb)

        @pl.when(pl.program_id(0) == 0)
        def _():
            acc_ref[...] = part

        @pl.when(pl.program_id(0) > 0)
        def _():
            acc_ref[...] += part

        @pl.when(pl.program_id(0) == nt - 1)
        def _():
            dw_ref[...] = acc_ref[...].astype(BF16)

        dg_ref[...] += _rsum8(dxo_t * m_ref[...].astype(F32))
        dy = _dot_nt(dmb, w_ref[...])
        u, u1, u2 = _conv_taps(cx_ref, halo_ref, pl.program_id(0) == 0, cw)
        yc = cwt_ref[0:1, :] * u2 + cwt_ref[1:2, :] * u1 + cwt_ref[2:3, :] * u
        dyv = dy[:, 0:cw]
        dcb_ref[...] = (dyv * yc).astype(BF16)
        dyc = dyv * cx_ref[:, 0:cw].astype(F32)
        dyc_ref[...] = dyc
        dcw_ref[0] += _rsum8(dyc * u2)
        dcw_ref[1] += _rsum8(dyc * u1)
        dcw_ref[2] += _rsum8(dyc * u)
        for h in range(heads):
            cols = slice(h * dv, (h + 1) * dv)
            ohat, rstd, on = _head_norm(o_ref[:, cols], gn_ref[...])
            g = go_ref[:, cols].astype(F32)
            sg = _sigmoid_tanh(g)
            dyg = dy[:, cw + h * dv:cw + (h + 1) * dv]
            dgo_ref[:, cols] = (dyg * on * (sg * (1.0 + g * (1.0 - sg)))).astype(BF16)
            don = dyg * (g * sg)
            dgn_ref[...] += _rsum8(don * ohat)
            tt = don * gn_ref[...]
            do_ref[:, cols] = (rstd * (tt - ohat * jnp.mean(tt * ohat, axis=-1, keepdims=True))).astype(BF16)

    return _pcall(
        body, name, [dxo, m, ycat, ada9, w_out, cx, cx, o, go, conv_w, gn], grid=(nt,),
        in_specs=[pl.BlockSpec((tm, d), lambda i: (i, 0)), pl.BlockSpec((tm, d), lambda i: (i, 0)),
                  pl.BlockSpec((tm, cw + vw), lambda i: (i, 0)),
                  pl.BlockSpec((N_ADA, d), lambda i: (0, 0)), pl.BlockSpec((cw + vw, d), lambda i: (0, 0)),
                  pl.BlockSpec((tm, 3 * cw), lambda i: (i, 0)),
                  pl.BlockSpec((BF16_ROWS, 3 * cw), lambda i: (jnp.maximum(i * (tm // BF16_ROWS) - 1, 0), 0)),
                  pl.BlockSpec((tm, vw), lambda i: (i, 0)), pl.BlockSpec((tm, vw), lambda i: (i, 0)),
                  pl.BlockSpec((3, cw), lambda i: (0, 0)), pl.BlockSpec((1, dv), lambda i: (0, 0))],
        out_shape=[S((cw + vw, d), BF16), S((t, cw), F32), S((t, cw), BF16), S((t, vw), BF16), S((t, vw), BF16),
                   S((8, d), F32), S((3, 8, cw), F32), S((8, dv), F32)],
        out_specs=[pl.BlockSpec((cw + vw, d), lambda i: (0, 0)), pl.BlockSpec((tm, cw), lambda i: (i, 0)),
                   pl.BlockSpec((tm, cw), lambda i: (i, 0)), pl.BlockSpec((tm, vw), lambda i: (i, 0)),
                   pl.BlockSpec((tm, vw), lambda i: (i, 0)), pl.BlockSpec((8, d), lambda i: (0, 0)),
                   pl.BlockSpec((3, 8, cw), lambda i: (0, 0, 0)), pl.BlockSpec((8, dv), lambda i: (0, 0))],
        scratch_shapes=[pltpu.VMEM((cw + vw, d), F32)],
        sem=("arbitrary",), plans=plans)


def _mix_bwd_b(dyc, cx, dcb, dqk, dvv, dgo, dgl, conv_w, w, x, dxo, ada9, nrm, name, plans=()):
    t, d = x.shape
    cw = conv_w.shape[1]
    n = w.shape[0]
    tm = min(ROW_TILE, t)
    nt = t // tm
    pieces = [dcb.shape[1], cw, cw, dqk.shape[1], dvv.shape[1], dgo.shape[1], dgl.shape[1]]
    assert sum(pieces) == n

    def body(dyc_ref, nxt_ref, cx_ref, dcb_ref, dqk_ref, dv_ref, dgo_ref, dgl_ref, cwt_ref, w_ref, x_ref, dxo_ref, ada_ref, n_ref,
             dx_ref, dp_ref, dsh_ref, dsc_ref, dn_ref):
        i = pl.program_id(0)

        @pl.when(i == 0)
        def _():
            dsh_ref[...] = jnp.zeros_like(dsh_ref)
            dsc_ref[...] = jnp.zeros_like(dsc_ref)
            dn_ref[...] = jnp.zeros_like(dn_ref)

        dyc_t = dyc_ref[...]
        nxt = jnp.where(i == nt - 1, 0.0, nxt_ref[...])
        row = lax.broadcasted_iota(jnp.int32, (tm, cw), 0)
        d1 = jnp.where(row == tm - 1, nxt[0:1, :], pltpu.roll(dyc_t, tm - 1, 0))
        d2 = jnp.where(row == tm - 2, nxt[0:1, :], jnp.where(row == tm - 1, nxt[1:2, :], pltpu.roll(dyc_t, tm - 2, 0)))
        du = cwt_ref[2:3, :] * dyc_t + cwt_ref[1:2, :] * d1 + cwt_ref[0:1, :] * d2
        c0 = 0
        dp_ref[:, c0:c0 + cw] = dcb_ref[...]
        dp_ref[:, cw:2 * cw] = (du * cx_ref[:, 2 * cw:3 * cw].astype(F32)).astype(BF16)
        dp_ref[:, 2 * cw:3 * cw] = (du * cx_ref[:, cw:2 * cw].astype(F32)).astype(BF16)
        c0 = 3 * cw
        for ref in (dqk_ref, dv_ref, dgo_ref, dgl_ref):
            wd = ref.shape[1]
            dp_ref[:, c0:c0 + wd] = ref[...]
            c0 += wd
        dh = _dot(dp_ref[...], w_ref[...])
        dx, tsh, tsc, tn = _normmod_bwd(dh, x_ref[...], n_ref[...], ada_ref[4:5, :])
        dx_ref[...] = dxo_ref[...] + dx
        dsh_ref[...] += _rsum8(tsh)
        dsc_ref[...] += _rsum8(tsc)
        dn_ref[...] += _rsum8(tn)

    row_spec = lambda wd: pl.BlockSpec((tm, wd), lambda i: (i, 0))
    vec = pl.BlockSpec((8, d), lambda i: (0, 0))
    return _pcall(
        body, name, [dyc, dyc, cx, dcb, dqk, dvv, dgo, dgl, conv_w, w, x, dxo, ada9, nrm], grid=(nt,),
        in_specs=[row_spec(cw), pl.BlockSpec((8, cw), lambda i: (jnp.minimum((i + 1) * (tm // 8), t // 8 - 1), 0)),
                  row_spec(3 * cw), row_spec(cw), row_spec(dqk.shape[1]), row_spec(dvv.shape[1]), row_spec(dgo.shape[1]),
                  row_spec(dgl.shape[1]), pl.BlockSpec((3, cw), lambda i: (0, 0)), pl.BlockSpec((n, d), lambda i: (0, 0)),
                  row_spec(d), row_spec(d), pl.BlockSpec((N_ADA, d), lambda i: (0, 0)), pl.BlockSpec((1, d), lambda i: (0, 0))],
        out_shape=[S((t, d), F32), S((t, n), BF16), S((8, d), F32), S((8, d), F32), S((8, d), F32)],
        out_specs=[row_spec(d), row_spec(n), vec, vec, vec],
        sem=("arbitrary",), plans=plans)


def _ffn_out_loss(s, w_out, x, ada9, g_row, res_scale, target, nrm, name):
    nb, t, bw = s.shape
    d = x.shape[1]
    tm = min(ROW_TILE, t)
    nt = t // tm

    def body(s_ref, w_ref, x_ref, ada_ref, tg_ref, n_ref, f_ref, loss_ref, dx_ref, dn_ref, acc_ref):
        i = pl.program_id(0)

        @pl.when(i == 0)
        def _():
            acc_ref[...] = jnp.zeros_like(acc_ref)
            dn_ref[...] = jnp.zeros_like(dn_ref)

        f = _dot(s_ref[0], w_ref[0])
        for b in range(1, nb):
            f = f + _dot(s_ref[b], w_ref[b])
        f_ref[...] = f.astype(BF16)
        xt = x_ref[...] + (res_scale * ada_ref[g_row:g_row + 1, :]) * f
        rstd = lax.rsqrt(jnp.mean(xt * xt, axis=-1, keepdims=True) + EPS)
        xhat = xt * rstd
        err = xhat * n_ref[...] - tg_ref[...]
        acc_ref[...] += _rsum8(err * err)
        dy = err * (1.0 / d)
        dn_ref[...] += _rsum8(dy * xhat)
        dxhat = dy * n_ref[...]
        dx_ref[...] = rstd * (dxhat - xhat * jnp.mean(dxhat * xhat, axis=-1, keepdims=True))

        @pl.when(i == nt - 1)
        def _():
            loss_ref[...] = jnp.full(loss_ref.shape, (0.5 / d) * jnp.sum(acc_ref[...]), F32)

    return pl.pallas_call(
        body, name=name, grid=(nt,),
        in_specs=[pl.BlockSpec((nb, tm, bw), lambda i: (0, i, 0)), pl.BlockSpec((nb, bw, d), lambda i: (0, 0, 0)),
                  pl.BlockSpec((tm, d), lambda i: (i, 0)), pl.BlockSpec((N_ADA, d), lambda i: (0, 0)),
                  pl.BlockSpec((tm, d), lambda i: (i, 0)), pl.BlockSpec((1, d), lambda i: (0, 0))],
        out_shape=[S((t, d), BF16), S((1, LANES), F32), S((t, d), F32), S((8, d), F32)],
        out_specs=[pl.BlockSpec((tm, d), lambda i: (i, 0)), pl.BlockSpec((1, LANES), lambda i: (0, 0)),
                   pl.BlockSpec((tm, d), lambda i: (i, 0)), pl.BlockSpec((8, d), lambda i: (0, 0))],
        scratch_shapes=[pltpu.VMEM((8, d), F32)],
        compiler_params=_cp("arbitrary"),
    )(s, w_out, x, ada9, target, nrm)


def _pack_smalls(vec_parts, dcw, dbg, dgn, dwg, loss_v, rank, name):
    d = vec_parts[0].shape[1]
    cw, kw, dv = dcw.shape[2], dbg.shape[1], dgn.shape[1]
    nv = len(vec_parts)
    loss_row = nv + 2 + rank * kw // d
    assert 2 * cw == d and cw + kw + dv <= d and (rank * kw) % d == 0 and loss_row < PACK_ROWS
    per_row = d // kw

    def body(*refs):
        vrefs, (dcw_ref, dbg_ref, dgn_ref, dwg_ref, loss_ref, o_ref) = refs[:nv], refs[nv:]
        o_ref[...] = jnp.zeros_like(o_ref)
        o_ref[loss_row:loss_row + 1, 0:loss_ref.shape[1]] = loss_ref[...]
        for r, ref in enumerate(vrefs):
            o_ref[r:r + 1, :] = jnp.sum(ref[...], axis=0, keepdims=True)
        o_ref[nv:nv + 1, 0:cw] = jnp.sum(dcw_ref[0], axis=0, keepdims=True)
        o_ref[nv:nv + 1, cw:2 * cw] = jnp.sum(dcw_ref[1], axis=0, keepdims=True)
        o_ref[nv + 1:nv + 2, 0:cw] = jnp.sum(dcw_ref[2], axis=0, keepdims=True)
        o_ref[nv + 1:nv + 2, cw:cw + kw] = jnp.sum(dbg_ref[...], axis=0, keepdims=True)
        o_ref[nv + 1:nv + 2, cw + kw:cw + kw + dv] = jnp.sum(dgn_ref[...], axis=0, keepdims=True)
        for r in range(rank):
            o_ref[nv + 2 + r // per_row:nv + 3 + r // per_row, (r % per_row) * kw:(r % per_row + 1) * kw] = dwg_ref[r:r + 1, :]

    return pl.pallas_call(body, name=name, out_shape=S((PACK_ROWS, d), F32), compiler_params=_cp())(*vec_parts, dcw, dbg, dgn, dwg, loss_v)


def _sum_slots(a, name):
    def body(a_ref, o_ref):
        acc = a_ref[0]
        for s in range(1, NDEV):
            acc = acc + a_ref[s]
        o_ref[...] = acc

    return pl.pallas_call(body, name=name, out_shape=S(a.shape[1:], F32), compiler_params=_cp())(a)


def _adamw(w, g, m, v):
    m = ADAM_B1 * m + (1.0 - ADAM_B1) * g
    v = ADAM_B2 * v + (1.0 - ADAM_B2) * (g * g)
    m_hat = m / (1.0 - ADAM_B1 ** ADAM_STEP)
    v_hat = v / (1.0 - ADAM_B2 ** ADAM_STEP)
    return -ADAM_LR * (m_hat / (jnp.sqrt(v_hat) + ADAM_EPS) + ADAM_WD * w), m, v


def _adam_slots(recv, w, m, v, name):
    r, c = w.shape
    slots = recv.shape[0]
    tr = _row_tile(r, c)

    def body(recv_ref, w_ref, m_ref, v_ref, g_ref, d_ref, mo_ref, vo_ref):
        g = recv_ref[0].astype(F32)
        for s in range(1, slots):
            g = g + recv_ref[s].astype(F32)
        g_ref[...] = g
        d_ref[...], mo_ref[...], vo_ref[...] = _adamw(w_ref[...], g, m_ref[...], v_ref[...])

    blk = pl.BlockSpec((tr, c), lambda i: (i, 0))
    return pl.pallas_call(
        body, name=name, grid=(r // tr,),
        in_specs=[pl.BlockSpec((slots, tr, c), lambda i: (0, i, 0)), blk, blk, blk],
        out_shape=[S((r, c), F32)] * 4, out_specs=[blk] * 4, compiler_params=_cp("parallel"),
    )(recv, w, m, v)


def _adam_w_ada(act_t, dada, w, m, v, name):
    r, c = w.shape
    tr = 128
    nb = act_t.shape[1]

    def body(a_ref, da_ref, w_ref, m_ref, v_ref, g_ref, d_ref, mo_ref, vo_ref):
        g = a_ref[:, 0:1] * da_ref[0:1, :]
        for b in range(1, nb):
            g = g + a_ref[:, b:b + 1] * da_ref[b:b + 1, :]
        g_ref[...] = g
        d_ref[...], mo_ref[...], vo_ref[...] = _adamw(w_ref[...], g, m_ref[...], v_ref[...])

    blk = pl.BlockSpec((tr, c), lambda i: (i, 0))
    return pl.pallas_call(
        body, name=name, grid=(r // tr,),
        in_specs=[pl.BlockSpec((tr, nb), lambda i: (i, 0)), pl.BlockSpec((nb, c), lambda i: (0, 0)), blk, blk, blk],
        out_shape=[S((r, c), F32)] * 4, out_specs=[blk] * 4, compiler_params=_cp("parallel"),
    )(act_t, dada, w, m, v)


def _adam_smalls(ws, gs, ms, vs, name):
    n = len(ws)

    def body(*refs):
        w_r, g_r, m_r, v_r = (refs[k * n:(k + 1) * n] for k in range(4))
        d_o, m_o, v_o = (refs[(4 + k) * n:(5 + k) * n] for k in range(3))
        for i in range(n):
            d_o[i][...], m_o[i][...], v_o[i][...] = _adamw(w_r[i][...], g_r[i][...], m_r[i][...], v_r[i][...])

    shapes = [S(w.shape, F32) for w in ws]
    outs = pl.pallas_call(body, name=name, out_shape=shapes * 3, compiler_params=_cp())(*ws, *gs, *ms, *vs)
    return outs[:n], outs[n:2 * n], outs[2 * n:]


def kernel(x, c, w_ada, b_ada, norm_ffn1, w_ffn1_in, w_ffn1_out, norm_mix, w_mix_in, conv_w, w_gk2, b_gk, gla_norm, w_mix_out, norm_ffn2, w_ffn2_in, w_ffn2_out, norm_final, loss_target, m_w_ada, m_b_ada, m_norm_ffn1, m_w_ffn1_in, m_w_ffn1_out, m_norm_mix, m_w_mix_in, m_conv_w, m_w_gk2, m_b_gk, m_gla_norm, m_w_mix_out, m_norm_ffn2, m_w_ffn2_in, m_w_ffn2_out, m_norm_final, v_w_ada, v_b_ada, v_norm_ffn1, v_w_ffn1_in, v_w_ffn1_out, v_norm_mix, v_w_mix_in, v_conv_w, v_w_gk2, v_b_gk, v_gla_norm, v_w_mix_out, v_norm_ffn2, v_w_ffn2_in, v_w_ffn2_out, v_norm_final):
    t, d = x.shape[1], x.shape[2]
    x0, tgt = x[0], loss_target[0]
    rank, kw = w_gk2.shape[1], w_gk2.shape[2] * NDEV
    cw = conv_w.shape[2] * NDEV
    dv = gla_norm.shape[1]
    vw = d - cw
    heads = vw // dv
    mix_cols = w_mix_in.shape[2]
    widths = [3 * cw, 2 * kw, vw, vw, LANES]
    n_proj = 3 * cw + 2 * kw + 2 * vw + rank
    assert n_proj == mix_cols * NDEV and rank <= LANES
    me = 4 * lax.axis_index("x") + 2 * lax.axis_index("y") + lax.axis_index("c")

    core = lax.axis_index("c").astype(jnp.int32).reshape(1)
    bf = lambda a: a[0].astype(BF16)
    bft = lambda a: a[0].T.astype(BF16)
    nb = NDEV // 2

    (ada_row, act_all), ((w1i, cwt_all, wg_all),) = _ada_rows(
        c, w_ada[0], b_ada, "ada_rows", plans=[_plan_gather([bft(w_ffn1_in), conv_w[0], w_gk2[0]])])
    ada9 = ada_row.reshape(N_ADA, d)
    cwt = cwt_all.transpose(1, 0, 2).reshape(conv_w.shape[1], cw)
    wg = jnp.pad(wg_all.transpose(1, 0, 2).reshape(rank, kw), ((0, LANES - rank), (0, 0))).astype(BF16)

    (h1, gu1, s1), ((w1o, wmi),) = _ffn_in(x0, ada9, norm_ffn1, w1i, 0, 1, "ffn1_in", plans=[_plan_gather([bf(w_ffn1_out), bft(w_mix_in)])])
    w1o = w1o.reshape(nb, -1, d)
    wmi = jnp.pad(wmi.reshape(n_proj, d), ((0, sum(widths) - n_proj), (0, 0)))
    w2i_mine = bft(w_ffn2_in)
    quarter = w2i_mine.shape[0] // 4
    part = lambda k, into=None: _plan_gather([w2i_mine], rows=(k * quarter, quarter), into=into)
    (x1, f1), ((wmo,), (w2i,)) = _ffn_out(s1, w1o, x0, ada9, 2, 0.5, "ffn1_out", plans=[_plan_gather([bf(w_mix_out)]), part(0)])
    wmo = wmo.reshape(cw + vw, d)
    (h2, cx, qk, vv, go, gl), ((w2i,),) = _mix_in(x1, ada9, norm_mix, wmi, widths, [BF16, F32, BF16, BF16, F32], "mix_in",
                                                 plans=[part(1, [w2i])])
    (o, lg, sall), ((w2i,),) = _gla_fwd(qk, vv, gl, wg, b_gk, heads, "gla_fwd", plans=[part(2, [w2i])])
    (x2, mm, ycat, h3), ((w2i,),) = _mix_out(cx, o, go, cwt, gla_norm, wmo, x1, ada9, norm_ffn2, heads, "mix_out", plans=[part(3, [w2i])])
    (h3, gu3, s3), ((w2o,),) = _ffn_in(None, None, None, w2i, 6, 7, "ffn2_in", plans=[_plan_gather([bf(w_ffn2_out)])], h_made=h3)
    w2o = w2o.reshape(nb, -1, d)
    f3, loss_v, dx3, dnf = _ffn_out_loss(s3, w2o, x2, ada9, 8, 0.5, tgt, norm_final.reshape(1, d), "ffn2_out_loss")

    dx2, sum2i, sum2o, (dsh3, dsc3, dg3, dn3), _, _ = _ffn_backward(
        dx3, x2, h3, gu3, f3, ada9, norm_ffn2, w2i, w2o, (6, 7, 8), core, "ffn2_bwd", later=True)
    (dwmo, dyc, dcb, do, dgo, dg2, dcw, dgn), ((r2o,),) = _mix_bwd_a(dx2, mm, ycat, ada9, wmo, cx, o, go, cwt, gla_norm, heads, "mix_bwd_a",
                                                                   plans=[_plan_chip_swap([sum2o])])
    half = sum2i.shape[1] // 2
    (dqk, dvv, dgl, dwg, dbg), ((r2i,),) = _gla_bwd(qk, vv, lg, do, sall, gl, wg, heads, "gla_bwd",
                                                   plans=[_plan_chip_swap([sum2i], rows=(0, half))])
    (dx1, dproj, dsh2, dsc2, dnm), ((r2i,),) = _mix_bwd_b(dyc, cx, dcb, dqk, dvv, dgo, dgl, cwt, wmi, x1, dx2, ada9, norm_mix, "mix_bwd_b",
                                                         plans=[_plan_chip_swap([sum2i], rows=(half, half), into=[r2i])])
    n_pad = sum(widths)
    tn = n_pad // 5
    dwmi, _ = _tn_matmul(dproj, h2, lambda tk: (tk, tn), lambda sb, k: (k, sb), lambda tk: (tk, d), lambda sb, k: (k, 0),
                         (n_pad, d), (tn, d), lambda sb, k: (sb, 0), 5, "mix_dwin")
    dwmi = dwmi[:n_proj].reshape(NDEV, mix_cols, d)
    dwmo = dwmo.reshape(NDEV, -1, d)
    dx0, r1i, (sum1o, r1o, head), (dsh1, dsc1, dg1, dn1), _, ((rmi, rmo),) = _ffn_backward(
        dx1, x0, h1, gu1, f1, ada9, norm_ffn1, w1i, w1o, (0, 1, 2), core, "ffn1_bwd",
        ds_plans=[_plan_sibling_swap([dwmi, dwmo])],
        dwin_plans=lambda moved: [_plan_chip_swap([_pair_add(dwmi, moved[0][0], core, "mix_dwin_add"),
                                                   _pair_add(dwmo, moved[0][1], core, "mix_dwout_add")])])
    pack = _pack_smalls([dn1, dnm, dn3, dnf, dsh1, dsc1, dg1, dsh2, dsc2, dg2, dsh3, dsc3, dg3], dcw, dbg, dgn, dwg, loss_v, rank, "pack_smalls")
    (r1o,), (pack_all,) = _exchange([_plan_chip_swap([sum1o], rows=(head, sum1o.shape[1] - head), into=[r1o]),
                                     _plan_all_to_all([pack], True)], "grads_last")
    tot = _sum_slots(pack_all, "sum_smalls")

    res = {}
    for nm, recv, w, m, v in (("w_ffn1_out", r1o, w_ffn1_out, m_w_ffn1_out, v_w_ffn1_out), ("w_mix_out", rmo, w_mix_out, m_w_mix_out, v_w_mix_out),
                              ("w_ffn2_out", r2o, w_ffn2_out, m_w_ffn2_out, v_w_ffn2_out)):
        res[nm] = [a[None] for a in _adam_slots(recv, w[0], m[0], v[0], "adam_" + nm)]
    for nm, recv, w, m, v in (("w_ffn1_in", r1i, w_ffn1_in, m_w_ffn1_in, v_w_ffn1_in), ("w_mix_in", rmi, w_mix_in, m_w_mix_in, v_w_mix_in),
                              ("w_ffn2_in", r2i, w_ffn2_in, m_w_ffn2_in, v_w_ffn2_in)):
        res[nm] = [a.T[None] for a in _adam_slots(recv, w[0].T, m[0].T, v[0].T, "adam_" + nm)]

    cols_ada = w_ada.shape[2]
    dada_all = pack_all[:, 4:4 + N_ADA, :].reshape(NDEV, N_ADA * d)
    dada_mine = lax.dynamic_slice_in_dim(dada_all, me * cols_ada, cols_ada, axis=1)
    res["w_ada"] = [a[None] for a in _adam_w_ada(act_all.T, dada_mine, w_ada[0], m_w_ada[0], v_w_ada[0], "adam_w_ada")]

    nv = 4 + N_ADA
    g_small = {
        "b_ada": tot[4:nv].reshape(1, N_ADA * d),
        "norm_ffn1": tot[0:1], "norm_mix": tot[1:2], "norm_ffn2": tot[2:3], "norm_final": tot[3:4],
        "conv_w": lax.dynamic_slice_in_dim(
            jnp.concatenate([tot[nv:nv + 1, 0:cw], tot[nv:nv + 1, cw:2 * cw], tot[nv + 1:nv + 2, 0:cw]], axis=0), me * (cw // NDEV), cw // NDEV, axis=1),
        "w_gk2": lax.dynamic_slice_in_dim(tot[nv + 2:nv + 2 + rank * kw // d].reshape(rank, kw), me * (kw // NDEV), kw // NDEV, axis=1),
        "b_gk": tot[nv + 1:nv + 2, cw:cw + kw],
        "gla_norm": tot[nv + 1:nv + 2, cw + kw:cw + kw + dv],
    }
    small = {"b_ada": (b_ada, m_b_ada, v_b_ada), "norm_ffn1": (norm_ffn1, m_norm_ffn1, v_norm_ffn1), "norm_mix": (norm_mix, m_norm_mix, v_norm_mix),
             "norm_ffn2": (norm_ffn2, m_norm_ffn2, v_norm_ffn2), "norm_final": (norm_final, m_norm_final, v_norm_final),
             "conv_w": (conv_w, m_conv_w, v_conv_w), "w_gk2": (w_gk2, m_w_gk2, v_w_gk2), "b_gk": (b_gk, m_b_gk, v_b_gk),
             "gla_norm": (gla_norm, m_gla_norm, v_gla_norm)}
    names = list(small)
    flat = lambda a: a.reshape(-1, a.shape[-1])
    dl, mo, vo = _adam_smalls([flat(small[n][0]) for n in names], [g_small[n] for n in names],
                              [flat(small[n][1]) for n in names], [flat(small[n][2]) for n in names], "adam_smalls")
    for i, n in enumerate(names):
        shp = small[n][0].shape
        res[n] = [g_small[n].reshape(shp), dl[i].reshape(shp), mo[i].reshape(shp), vo[i].reshape(shp)]

    loss = tot[nv + 2 + rank * kw // d, 0]
    order = ["w_ada", "b_ada", "norm_ffn1", "w_ffn1_in", "w_ffn1_out", "norm_mix", "w_mix_in", "conv_w", "w_gk2", "b_gk", "gla_norm",
             "w_mix_out", "norm_ffn2", "w_ffn2_in", "w_ffn2_out", "norm_final"]
    return (loss, dx0[None], *[res[n][0] for n in order], *[res[n][1] for n in order], *[res[n][2] for n in order], *[res[n][3] for n in order])
```

```python
import collections
import functools

import jax
import jax.numpy as jnp
from jax import lax
from jax.experimental import pallas as pl
from jax.experimental.pallas import tpu as pltpu

F32 = jnp.float32
BF16 = jnp.bfloat16
S = jax.ShapeDtypeStruct

NDEV = 8
EPS = 1e-6
GATE_NORMALIZER = 16.0
CHUNK = 128
N_ADA = 9
ADAM_LR, ADAM_B1, ADAM_B2, ADAM_EPS, ADAM_WD, ADAM_STEP = 0.001, 0.9, 0.999, 1e-08, 0.01, 10
V7X_VMEM_LIMIT = 56 * 1024 * 1024
ROW_TILE = 512
WIDE_ROW_TILE = 1024
K_TILE = 1024
EPILOGUE_ROWS = 256
LANES = 128
BF16_ROWS = 16
PACK_ROWS = 24
ANY = pl.BlockSpec(memory_space=pl.ANY)


def _cp(*sem):
    return pltpu.CompilerParams(dimension_semantics=sem or None, vmem_limit_bytes=V7X_VMEM_LIMIT)


def _dot(a, b):
    return jnp.dot(a, b, preferred_element_type=F32)


def _dot_nt(a, b):
    return lax.dot_general(a, b, (((1,), (1,)), ((), ())), preferred_element_type=F32)


def _dot_tn(a, b):
    return lax.dot_general(a, b, (((0,), (0,)), ((), ())), preferred_element_type=F32)


def _rsum8(a):
    r, c = a.shape
    return jnp.sum(a.reshape(r // 8, 8, c), axis=0)


def _row_tile(r, c):
    for cand in (256, 128, 176, 88, 64, 32, 16, 8):
        if r % cand == 0 and cand * c * 4 <= 1024 * 1024:
            return cand
    return r


def _sigmoid(x):
    return 1.0 / (1.0 + jnp.exp(-x))


def _sigmoid_tanh(x):
    return 0.5 * jnp.tanh(0.5 * x) + 0.5


def _normmod(x, nrm, sc, sh):
    rstd = lax.rsqrt(jnp.mean(x * x, axis=-1, keepdims=True) + EPS)
    xhat = x * rstd
    return xhat, rstd, (xhat * nrm) * (1.0 + sc) + sh


def _normmod_bwd(dh, x, nrm, sc):
    rstd = lax.rsqrt(jnp.mean(x * x, axis=-1, keepdims=True) + EPS)
    xhat = x * rstd
    dxhat = dh * (nrm * (1.0 + sc))
    dx = rstd * (dxhat - xhat * jnp.mean(dxhat * xhat, axis=-1, keepdims=True))
    return dx, dh, dh * (xhat * nrm), dh * ((1.0 + sc) * xhat)


def _place():
    x, y, c = lax.axis_index("x"), lax.axis_index("y"), lax.axis_index("c")
    return x, y, c, 4 * x + 2 * y + c


def _peer(x, y, c, k):
    px = 1 - x if k & 4 else x
    py = 1 - y if k & 2 else y
    pc = 1 - c if k & 1 else c
    return (px, py, pc), 4 * px + 2 * py + pc


def _remote(src, dst, send_sem, recv_sem, peer):
    return pltpu.make_async_remote_copy(src_ref=src, dst_ref=dst, send_sem=send_sem, recv_sem=recv_sem,
                                        device_id=peer, device_id_type=pl.DeviceIdType.MESH)


_Plan = collections.namedtuple("_Plan", "inputs out_shapes sem_shapes start finish aliases", defaults=({},))


def _plan_all_to_all(xs, gather):
    n = len(xs)

    def copies(ins, outs, sems, landed):
        send_sems, recv_sems, local_sems = sems
        x, y, c, me = _place()
        local = [pltpu.make_async_copy(ins[i] if gather else ins[i].at[me], outs[i].at[me], local_sems.at[i]) for i in range(n)]
        remote = []
        for k in range(1, NDEV):
            peer, pid = _peer(x, y, c, k)
            for i in range(n):
                remote.append(_remote(ins[i] if gather else ins[i].at[pid], outs[i].at[pid if landed else me],
                                      send_sems.at[i, k - 1], recv_sems.at[i, k - 1], peer))
        return local, remote

    def start(ins, outs, sems):
        local, remote = copies(ins, outs, sems, False)
        for cp in local + remote:
            cp.start()

    def finish(ins, outs, sems):
        local, remote = copies(ins, outs, sems, True)
        for cp in remote + local:
            cp.wait()

    return _Plan(list(xs), [S((NDEV,) + a.shape, a.dtype) if gather else S(a.shape, a.dtype) for a in xs],
                 [pltpu.SemaphoreType.DMA((n, NDEV - 1)), pltpu.SemaphoreType.DMA((n, NDEV - 1)), pltpu.SemaphoreType.DMA((n,))],
                 start, finish)


def _other_chips(x, y):
    return [(1 - x, y), (x, 1 - y), (1 - x, 1 - y)]


def _plan_gather(xs, rows=None, into=None):
    n = len(xs)

    def copies(ins, outs, sems, rest):
        send_sems, recv_sems, local_sems = sems
        x, y, c, me = _place()
        sib, sib_id = (x, y, 1 - c), 4 * x + 2 * y + 1 - c
        chips = _other_chips(x, y)
        mine = lambda i: ins[i] if rows is None else ins[i].at[pl.ds(*rows)]
        slot_of = lambda i, s: outs[i].at[s] if rows is None else outs[i].at[s, pl.ds(*rows)]
        local = [pltpu.make_async_copy(mine(i), slot_of(i, me), local_sems.at[i]) for i in range(n)]
        first = [_remote(mine(i), slot_of(i, me), send_sems.at[i, 0], recv_sems.at[i, 0], sib) for i in range(n)]
        first += [_remote(mine(i), slot_of(i, me), send_sems.at[i, 1 + j], recv_sems.at[i, 1 + j], (px, py, c))
                  for j, (px, py) in enumerate(chips) for i in range(n)]
        if not rest:
            return local, first
        from_sibling = [_remote(mine(i), slot_of(i, sib_id), send_sems.at[i, 0], recv_sems.at[i, 0], sib) for i in range(n)]
        arrive, forward = [], []
        for j, (px, py) in enumerate(chips):
            s = 4 * px + 2 * py
            arrive.append([_remote(mine(i), slot_of(i, s + c), send_sems.at[i, 1 + j], recv_sems.at[i, 1 + j], (px, py, c)) for i in range(n)])
            forward.append([_remote(slot_of(i, s + c), slot_of(i, s + c), send_sems.at[i, 4 + j], recv_sems.at[i, 4 + j], sib) for i in range(n)])
            from_sibling += [_remote(mine(i), slot_of(i, s + 1 - c), send_sems.at[i, 4 + j], recv_sems.at[i, 4 + j], sib) for i in range(n)]
        return local, first, arrive, forward, from_sibling

    def start(ins, outs, sems):
        local, first = copies(ins, outs, sems, False)
        for cp in local + first:
            cp.start()

    def finish(ins, outs, sems):
        local, first, arrive, forward, from_sibling = copies(ins, outs, sems, True)
        for landed, onward in zip(arrive, forward):
            for cp in landed:
                cp.wait_recv()
            for cp in onward:
                cp.start()
        for cp in from_sibling:
            cp.wait_recv()
        for cp in first + [cp for onward in forward for cp in onward]:
            cp.wait_send()
        for cp in local:
            cp.wait()

    return _Plan(list(xs) + list(into or []), [S((NDEV,) + a.shape, a.dtype) for a in xs],
                 [pltpu.SemaphoreType.DMA((n, NDEV - 1)), pltpu.SemaphoreType.DMA((n, NDEV - 1)), pltpu.SemaphoreType.DMA((n,))],
                 start, finish, {n + i: i for i in range(len(into or []))})


def _plan_sibling_swap(gs):
    n = len(gs)

    def copies(ins, outs, sems):
        send_sems, recv_sems = sems
        x, y, c, _ = _place()
        return [_remote(ins[i].at[2 * j + 1 - c], outs[i].at[j], send_sems.at[i, j], recv_sems.at[i, j], (x, y, 1 - c))
                for i in range(n) for j in range(NDEV // 2)]

    def start(ins, outs, sems):
        for cp in copies(ins, outs, sems):
            cp.start()

    def finish(ins, outs, sems):
        for cp in copies(ins, outs, sems):
            cp.wait()

    return _Plan(list(gs), [S((NDEV // 2,) + a.shape[1:], a.dtype) for a in gs],
                 [pltpu.SemaphoreType.DMA((n, NDEV // 2)), pltpu.SemaphoreType.DMA((n, NDEV // 2))], start, finish)


def _pair_add(g, r1, core, name):
    _, r, c = g.shape
    tr = r if r * c * 2 <= 2 * 1024 * 1024 else _row_tile(r, c)

    def body(core_ref, g_ref, r_ref, o_ref):
        o_ref[...] = (g_ref[...].astype(F32) + r_ref[...].astype(F32)).astype(BF16)

    return pl.pallas_call(
        body, name=name,
        grid_spec=pltpu.PrefetchScalarGridSpec(
            num_scalar_prefetch=1, grid=(NDEV // 2, r // tr),
            in_specs=[pl.BlockSpec((None, tr, c), lambda j, k, core_ref: (2 * j + core_ref[0], k, 0)),
                      pl.BlockSpec((None, tr, c), lambda j, k, core_ref: (j, k, 0))],
            out_specs=pl.BlockSpec((None, tr, c), lambda j, k, core_ref: (j, k, 0))),
        out_shape=S((NDEV // 2, r, c), BF16), compiler_params=_cp("parallel", "parallel"),
    )(core, g, r1)


def _plan_chip_swap(ps, rows=None, into=None):
    n = len(ps)

    def copies(ins, outs, sems, landed):
        send_sems, recv_sems, local_sems = sems
        x, y, c, _ = _place()
        mine = 2 * x + y
        part = lambda ref, s: ref.at[s] if rows is None else ref.at[s, pl.ds(*rows)]
        local = [pltpu.make_async_copy(part(ins[i], mine), part(outs[i], mine), local_sems.at[i]) for i in range(n)]
        remote = [_remote(part(ins[i], 2 * px + py), part(outs[i], 2 * px + py if landed else mine),
                          send_sems.at[i, j], recv_sems.at[i, j], (px, py, c))
                  for j, (px, py) in enumerate(_other_chips(x, y)) for i in range(n)]
        return local, remote

    def start(ins, outs, sems):
        local, remote = copies(ins, outs, sems, False)
        for cp in local + remote:
            cp.start()

    def finish(ins, outs, sems):
        local, remote = copies(ins, outs, sems, True)
        for cp in remote + local:
            cp.wait()

    return _Plan(list(ps) + list(into or []), [S(a.shape, a.dtype) for a in ps],
                 [pltpu.SemaphoreType.DMA((n, 3)), pltpu.SemaphoreType.DMA((n, 3)), pltpu.SemaphoreType.DMA((n,))], start, finish,
                 {n + i: i for i in range(len(into or []))})


def _pcall(body, name, args, in_specs, out_shape, out_specs, grid=(), scratch_shapes=(), sem=(), plans=(), starts_plans=False):
    n_in, n_out, n_scr = len(args), len(out_shape), len(scratch_shapes)
    counts = [(len(p.inputs), len(p.out_shapes), len(p.sem_shapes)) for p in plans]
    c_args = [a for p in plans for a in p.inputs]
    c_outs = [s for p in plans for s in p.out_shapes]
    c_sems = [s for p in plans for s in p.sem_shapes]

    def wrapped(*refs):
        cuts = [n_in, len(c_args), n_out, len(c_outs), n_scr, len(c_sems)]
        ins, c_in, outs, c_out, scr, c_sem = [refs[sum(cuts[:k]):sum(cuts[:k + 1])] for k in range(6)]

        def halves(which):
            a = b = s = 0
            for p, (na, nb, ns) in zip(plans, counts):
                getattr(p, which)(c_in[a:a + na], c_out[b:b + nb], c_sem[s:s + ns])
                a, b, s = a + na, b + nb, s + ns

        if not plans:
            body(*ins, *outs, *scr)
        elif not grid and starts_plans:
            body(lambda: halves("start"), *ins, *outs, *scr)
            halves("finish")
        elif not grid:
            halves("start")
            body(*ins, *outs, *scr)
            halves("finish")
        else:
            first = functools.reduce(jnp.logical_and, [pl.program_id(a) == 0 for a in range(len(grid))])
            last = functools.reduce(jnp.logical_and, [pl.program_id(a) == grid[a] - 1 for a in range(len(grid))])
            pl.when(first)(lambda: halves("start"))
            body(*ins, *outs, *scr)
            pl.when(last)(lambda: halves("finish"))

    aliases, a, b = {}, n_in, n_out
    for p, (na, nb, _) in zip(plans, counts):
        aliases.update({a + k: b + v for k, v in p.aliases.items()})
        a, b = a + na, b + nb
    res = pl.pallas_call(
        wrapped, name=name, grid=grid, in_specs=list(in_specs) + [ANY] * len(c_args),
        out_shape=list(out_shape) + c_outs, out_specs=list(out_specs) + [ANY] * len(c_outs),
        scratch_shapes=list(scratch_shapes) + c_sems, input_output_aliases=aliases,
        compiler_params=_cp(*(("arbitrary",) * len(grid) if plans else sem)),
    )(*args, *c_args)
    c_res, b = [], n_out
    for _, nb, _ in counts:
        c_res.append(res[b:b + nb])
        b += nb
    return res[:n_out], c_res


def _exchange(plans, name):
    return _pcall(lambda: None, name, [], [], [], [], plans=plans)[1]


def _ada_rows(c, w_ada, b_ada, name, plans=()):
    d, cols = c.shape[1], w_ada.shape[1]
    gather_c = _plan_all_to_all([c], True)
    gather_p = _plan_all_to_all([S((NDEV, cols), F32)], True)
    n_sem = len(gather_c.sem_shapes)

    def body(start_plans, c_ref, w_ref, b_ref, ada_ref, act_ref, c_all, p_mine, p_all, *sems):
        gather_c.start([c_ref], [c_all], sems[:n_sem])
        gather_c.finish([c_ref], [c_all], sems[:n_sem])
        for s in range(NDEV):
            cc = c_all[s]
            act_ref[s:s + 1, :] = cc * _sigmoid(cc)
        p_mine[...] = _dot(act_ref[...].astype(BF16), w_ref[...].astype(BF16))
        gather_p.start([p_mine], [p_all], sems[n_sem:])
        start_plans()
        gather_p.finish([p_mine], [p_all], sems[n_sem:])
        me = _place()[3]
        for s in range(NDEV):
            ada_ref[:, s * cols:(s + 1) * cols] = p_all[s, pl.ds(me, 1), :] + b_ref[:, s * cols:(s + 1) * cols]

    whole = pl.BlockSpec(memory_space=pltpu.VMEM)
    return _pcall(body, name, [c, w_ada, b_ada], [whole] * 3, [S((1, NDEV * cols), F32), S((NDEV, d), F32)], [whole] * 2,
                  scratch_shapes=[pltpu.VMEM((NDEV,) + c.shape, F32), pltpu.VMEM((NDEV, cols), F32), pltpu.VMEM((NDEV, NDEV, cols), F32)]
                  + gather_c.sem_shapes + gather_p.sem_shapes, plans=plans, starts_plans=True)


def _ffn_in(x, ada9, nrm, w_in, sh_row, sc_row, name, plans=(), h_made=None):
    t, d = (x if h_made is None else h_made).shape
    nb, bw = w_in.shape[0] // 2, w_in.shape[1]
    tm = min(WIDE_ROW_TILE, t)

    def body(*refs):
        if h_made is None:
            x_ref, ada_ref, n_ref, wg_ref, wu_ref, h_ref, gu_ref, s_ref = refs

            @pl.when(pl.program_id(1) == 0)
            def _():
                _, _, h = _normmod(x_ref[...], n_ref[...], ada_ref[sc_row:sc_row + 1, :], ada_ref[sh_row:sh_row + 1, :])
                h_ref[...] = h.astype(BF16)
        else:
            h_ref, wg_ref, wu_ref, gu_ref, s_ref = refs

        h = h_ref[...]
        g = _dot_nt(h, wg_ref[...])
        u = _dot_nt(h, wu_ref[...])
        gu_ref[0] = g.astype(BF16)
        gu_ref[1] = u.astype(BF16)
        s_ref[...] = (g * _sigmoid(g) * u).astype(BF16)

    rows = pl.BlockSpec((tm, d), lambda i, j: (i, 0))
    weights = [pl.BlockSpec((None, bw, d), lambda i, j: (j, 0, 0)), pl.BlockSpec((None, bw, d), lambda i, j: (j + nb, 0, 0))]
    results = ([S((2, nb, t, bw), BF16), S((nb, t, bw), BF16)],
               [pl.BlockSpec((2, None, tm, bw), lambda i, j: (0, j, i, 0)), pl.BlockSpec((None, tm, bw), lambda i, j: (j, i, 0))])
    if h_made is not None:
        (gu, s), moved = _pcall(body, name, [h_made, w_in, w_in], [rows] + weights, results[0], results[1], grid=(t // tm, nb),
                                sem=("parallel", "arbitrary"), plans=plans)
        return (h_made, gu, s), moved
    return _pcall(
        body, name, [x, ada9, nrm, w_in, w_in], grid=(t // tm, nb),
        in_specs=[rows, pl.BlockSpec((N_ADA, d), lambda i, j: (0, 0)), pl.BlockSpec((1, d), lambda i, j: (0, 0))] + weights,
        out_shape=[S((t, d), BF16)] + results[0], out_specs=[rows] + results[1],
        sem=("parallel", "arbitrary"), plans=plans)


def _ffn_out(s, w_out, x, ada9, g_row, res_scale, name, plans=()):
    nb, t, bw = s.shape
    d = x.shape[1]
    tm = min(WIDE_ROW_TILE, t)

    def body(s_ref, w_ref, x_ref, ada_ref, xo_ref, f_ref):
        acc = _dot(s_ref[0], w_ref[0])
        for b in range(1, nb):
            acc = acc + _dot(s_ref[b], w_ref[b])
        f_ref[...] = acc.astype(BF16)
        xo_ref[...] = x_ref[...] + (res_scale * ada_ref[g_row:g_row + 1, :]) * acc

    return _pcall(
        body, name, [s, w_out, x, ada9], grid=(t // tm,),
        in_specs=[pl.BlockSpec((nb, tm, bw), lambda i: (0, i, 0)),
                  pl.BlockSpec((nb, bw, d), lambda i: (0, 0, 0), pipeline_mode=pl.Buffered(1)),
                  pl.BlockSpec((tm, d), lambda i: (i, 0)), pl.BlockSpec((N_ADA, d), lambda i: (0, 0))],
        out_shape=[S((t, d), F32), S((t, d), BF16)],
        out_specs=[pl.BlockSpec((tm, d), lambda i: (i, 0)), pl.BlockSpec((tm, d), lambda i: (i, 0))],
        sem=("parallel",), plans=plans)


def _ffn_bwd_ds(dxo, f, ada9, w_out, gu, g_row, res_scale, name, plans=()):
    t, d = dxo.shape
    nb, bw = w_out.shape[0], w_out.shape[1]
    tm = min(WIDE_ROW_TILE, t)
    ni = t // tm

    def body(dxo_ref, f_ref, ada_ref, w_ref, gu_ref, da_ref, dg_ref, dw_ref, df_ref, acc_ref):
        i, j = pl.program_id(0), pl.program_id(1)

        @pl.when((i == 0) & (j == 0))
        def _():
            dg_ref[...] = jnp.zeros_like(dg_ref)

        @pl.when(j == 0)
        def _():
            dxo_t = dxo_ref[...]
            df_ref[...] = ((res_scale * ada_ref[g_row:g_row + 1, :]) * dxo_t).astype(BF16)
            dg_ref[...] += res_scale * _rsum8(dxo_t * f_ref[...].astype(F32))

        df = df_ref[...]
        ds = _dot_nt(df, w_ref[...])
        g = gu_ref[0].astype(F32)
        u = gu_ref[1].astype(F32)
        sg = _sigmoid_tanh(g)
        silu = g * sg
        da_ref[0] = (ds * u * (sg * (1.0 + g * (1.0 - sg)))).astype(BF16)
        da_ref[1] = (ds * silu).astype(BF16)
        part = _dot_tn((silu * u).astype(BF16), df)

        @pl.when(i == 0)
        def _():
            acc_ref[j] = part

        @pl.when(i > 0)
        def _():
            acc_ref[j] += part

        @pl.when(i == ni - 1)
        def _():
            dw_ref[...] = acc_ref[j].astype(BF16)

    return _pcall(
        body, name, [dxo, f, ada9, w_out, gu], grid=(ni, nb),
        in_specs=[pl.BlockSpec((tm, d), lambda i, j: (i, 0)), pl.BlockSpec((tm, d), lambda i, j: (i, 0)),
                  pl.BlockSpec((N_ADA, d), lambda i, j: (0, 0)), pl.BlockSpec((None, bw, d), lambda i, j: (j, 0, 0)),
                  pl.BlockSpec((2, None, tm, bw), lambda i, j: (0, j, i, 0))],
        out_shape=[S((2, nb, t, bw), BF16), S((8, d), F32), S((nb, bw, d), BF16)],
        out_specs=[pl.BlockSpec((2, None, tm, bw), lambda i, j: (0, j, i, 0)), pl.BlockSpec((8, d), lambda i, j: (0, 0)),
                   pl.BlockSpec((None, bw, d), lambda i, j: (jnp.where(i == ni - 1, j, 0), 0, 0))],
        scratch_shapes=[pltpu.VMEM((tm, d), BF16), pltpu.VMEM((nb, bw, d), F32)],
        sem=("arbitrary", "arbitrary"), plans=plans)


def _ffn_bwd_dh(da, w_in, x, dxo, ada9, nrm, sh_row, sc_row, name, plans=()):
    t, d = x.shape
    nb, bw = w_in.shape[0] // 2, w_in.shape[1]
    tm = min(WIDE_ROW_TILE, t)

    def body(da_ref, wg_ref, wu_ref, x_ref, dxo_ref, ada_ref, n_ref, dx_ref, dsh_ref, dsc_ref, dn_ref, acc_ref):
        i, j = pl.program_id(0), pl.program_id(1)

        @pl.when((i == 0) & (j == 0))
        def _():
            dsh_ref[...] = jnp.zeros_like(dsh_ref)
            dsc_ref[...] = jnp.zeros_like(dsc_ref)
            dn_ref[...] = jnp.zeros_like(dn_ref)

        part = _dot(da_ref[0], wg_ref[...]) + _dot(da_ref[1], wu_ref[...])

        @pl.when(j == 0)
        def _():
            acc_ref[...] = part

        @pl.when(j > 0)
        def _():
            acc_ref[...] += part

        @pl.when(j == nb - 1)
        def _():
            for r0 in range(0, tm, min(EPILOGUE_ROWS, tm)):
                rows = slice(r0, r0 + min(EPILOGUE_ROWS, tm))
                dx, tsh, tsc, tn = _normmod_bwd(acc_ref[rows, :], x_ref[rows, :], n_ref[...], ada_ref[sc_row:sc_row + 1, :])
                dx_ref[rows, :] = dxo_ref[rows, :] + dx
                dsh_ref[...] += _rsum8(tsh)
                dsc_ref[...] += _rsum8(tsc)
                dn_ref[...] += _rsum8(tn)

    vec = pl.BlockSpec((8, d), lambda i, j: (0, 0))
    return _pcall(
        body, name, [da, w_in, w_in, x, dxo, ada9, nrm], grid=(t // tm, nb),
        in_specs=[pl.BlockSpec((2, None, tm, bw), lambda i, j: (0, j, i, 0)),
                  pl.BlockSpec((None, bw, d), lambda i, j: (j, 0, 0)), pl.BlockSpec((None, bw, d), lambda i, j: (j + nb, 0, 0)),
                  pl.BlockSpec((tm, d), lambda i, j: (i, 0)), pl.BlockSpec((tm, d), lambda i, j: (i, 0)),
                  pl.BlockSpec((N_ADA, d), lambda i, j: (0, 0)), pl.BlockSpec((1, d), lambda i, j: (0, 0))],
        out_shape=[S((t, d), F32), S((8, d), F32), S((8, d), F32), S((8, d), F32)],
        out_specs=[pl.BlockSpec((tm, d), lambda i, j: (i, 0)), vec, vec, vec],
        scratch_shapes=[pltpu.VMEM((tm, d), F32)],
        sem=("arbitrary", "arbitrary"), plans=plans)


def _tn_matmul(a, b, a_block, a_map, b_block, b_map, out_shape, out_block, out_map, nblk, name, plans=()):
    t = a.shape[-2]
    tk = min(K_TILE, t)
    nk = t // tk

    def body(a_ref, b_ref, o_ref, acc_ref):
        k = pl.program_id(1)
        for q in (range(a_ref.shape[0]) if len(a_ref.shape) == 3 else [Ellipsis]):
            part = _dot_tn(a_ref[q], b_ref[...])

            @pl.when(k == 0)
            def _():
                acc_ref[q] = part

            @pl.when(k > 0)
            def _():
                acc_ref[q] += part

        @pl.when(k == nk - 1)
        def _():
            o_ref[...] = acc_ref[...].astype(BF16)

    (out,), moved = _pcall(
        body, name, [a, b], grid=(nblk, nk),
        in_specs=[pl.BlockSpec(a_block(tk), a_map), pl.BlockSpec(b_block(tk), b_map)],
        out_shape=[S(out_shape, BF16)], out_specs=[pl.BlockSpec(out_block, out_map)],
        scratch_shapes=[pltpu.VMEM(tuple(n for n in out_block if n is not None), F32)],
        sem=("parallel", "arbitrary"), plans=plans)
    return out, moved


def _ffn_backward(dxo, x_in, h, gu, f, ada9, nrm, w_in, w_out, rows, core, name, ds_plans=(), dwin_plans=(), later=False):
    sh_row, sc_row, g_row = rows
    _, nb, t, bw = gu.shape
    d = x_in.shape[1]
    (da, dg, dw_out), ds_moved = _ffn_bwd_ds(dxo, f, ada9, w_out, gu, g_row, 0.5, name + "_ds", plans=ds_plans)
    dw_out = dw_out.reshape(NDEV, -1, d)
    dw_in, ((half_out,), *dwin_moved) = _tn_matmul(
        da.reshape(2 * nb, t, bw), h, lambda tk: (2, tk, bw), lambda sb, k: (sb, k, 0), lambda tk: (tk, d), lambda sb, k: (k, 0),
        (2 * nb, bw, d), (2, bw, d), lambda sb, k: (sb, 0, 0), nb, name + "_dwin",
        plans=[_plan_sibling_swap([dw_out])] + list(dwin_plans(ds_moved) if callable(dwin_plans) else dwin_plans))
    sum_out = _pair_add(dw_out, half_out, core, name + "_dwout_add")
    if later:
        (dx, dsh, dsc, dn), ((half_in,),) = _ffn_bwd_dh(da, w_in, x_in, dxo, ada9, nrm, sh_row, sc_row, name + "_dh",
                                                       plans=[_plan_sibling_swap([dw_in])])
        return dx, _pair_add(dw_in, half_in, core, name + "_dwin_add"), sum_out, (dsh, dsc, dg, dn), ds_moved, dwin_moved
    ((half_in,),) = _exchange([_plan_sibling_swap([dw_in])], name + "_dwin_swap")
    sum_in = _pair_add(dw_in, half_in, core, name + "_dwin_add")
    head = (3 * sum_out.shape[1] // 4) // BF16_ROWS * BF16_ROWS
    (dx, dsh, dsc, dn), ((recv_in,), (out_begun,)) = _ffn_bwd_dh(
        da, w_in, x_in, dxo, ada9, nrm, sh_row, sc_row, name + "_dh",
        plans=[_plan_chip_swap([sum_in]), _plan_chip_swap([sum_out], rows=(0, head))])
    return dx, recv_in, (sum_out, out_begun, head), (dsh, dsc, dg, dn), ds_moved, dwin_moved


def _mix_in(x, ada9, nrm, w, widths, dts, name, plans=()):
    t, d = x.shape
    n = w.shape[0]
    tm = min(ROW_TILE, t)
    starts = [sum(widths[:i]) for i in range(len(widths))]

    def body(x_ref, ada_ref, n_ref, w_ref, h_ref, *out_refs):
        _, _, h = _normmod(x_ref[...], n_ref[...], ada_ref[4:5, :], ada_ref[3:4, :])
        hb = h.astype(BF16)
        h_ref[...] = hb
        for o_ref, st, wd in zip(out_refs, starts, widths):
            o_ref[...] = _dot_nt(hb, w_ref[st:st + wd, :]).astype(o_ref.dtype)

    return _pcall(
        body, name, [x, ada9, nrm, w], grid=(t // tm,),
        in_specs=[pl.BlockSpec((tm, d), lambda i: (i, 0)), pl.BlockSpec((N_ADA, d), lambda i: (0, 0)),
                  pl.BlockSpec((1, d), lambda i: (0, 0)), pl.BlockSpec((n, d), lambda i: (0, 0))],
        out_shape=[S((t, d), BF16)] + [S((t, wd), dt) for wd, dt in zip(widths, dts)],
        out_specs=[pl.BlockSpec((tm, d), lambda i: (i, 0))] + [pl.BlockSpec((tm, wd), lambda i: (i, 0)) for wd in widths],
        sem=("parallel",), plans=plans)


def _tri(lower):
    r = lax.broadcasted_iota(jnp.int32, (CHUNK, CHUNK), 0)
    c = lax.broadcasted_iota(jnp.int32, (CHUNK, CHUNK), 1)
    return (r >= c) if lower else (c >= r)


def _dot_01(m, x):
    hi = x.astype(BF16)
    r1 = x - hi.astype(F32)
    mid = r1.astype(BF16)
    lo = (r1 - mid.astype(F32)).astype(BF16)
    return _dot(m, hi) + _dot(m, mid) + _dot(m, lo)


def _gla_chunk_terms(q, k, lg, low01):
    b = _dot_01(low01, lg)
    bl = b[CHUNK - 1:CHUNK, :]
    r = 0.5 * bl
    eb, ebl, em, en = jnp.exp(b), jnp.exp(bl - b), jnp.exp(b - r), jnp.exp(r - b)
    return eb, ebl, em, en, jnp.exp(bl), q * eb, k * ebl, q * em, k * en


def _scores(qm_h, knp, qk1_h):
    r = lax.broadcasted_iota(jnp.int32, (CHUNK, CHUNK), 0)
    c = lax.broadcasted_iota(jnp.int32, (CHUNK, CHUNK), 1)
    p = jnp.where(r > c, _dot_nt(qm_h, knp), 0.0)
    return jnp.where(r == c, jnp.sum(qk1_h, axis=1, keepdims=True), p)


def _gla_fwd(qk, v, gl, wg, bg, heads, name, plans=()):
    t = qk.shape[0]
    kw, vw = qk.shape[1] // 2, v.shape[1]
    dk, dv = kw // heads, vw // heads
    assert dk == 64 and dv == 128 and kw % 128 == 0
    gt = min(ROW_TILE, t)
    nc = gt // CHUNK
    scale = dk ** -0.5

    def body(qk_ref, v_ref, gl_ref, wg_ref, bg_ref, o_ref, lg_ref, sall_ref, st_ref):
        @pl.when(pl.program_id(0) == 0)
        def _():
            st_ref[...] = jnp.zeros_like(st_ref)

        gk = _dot(gl_ref[...].astype(BF16), wg_ref[...]) + bg_ref[...]
        lg_ref[...] = (jnp.minimum(gk, 0.0) - jnp.log(1.0 + jnp.exp(-jnp.abs(gk)))) / GATE_NORMALIZER
        low01 = _tri(True).astype(BF16)
        lane = lax.broadcasted_iota(jnp.int32, (CHUNK, LANES), 1)

        def chunk(ci, carry):
            rows = pl.ds(pl.multiple_of(ci * CHUNK, CHUNK), CHUNK)
            q = qk_ref[rows, 0:kw] * scale
            k = qk_ref[rows, kw:2 * kw]
            qk1 = q.astype(BF16).astype(F32) * k.astype(BF16).astype(F32)
            eb, ebl, em, en, ebl_row, qe, ke, qm, kn = _gla_chunk_terms(q, k, lg_ref[rows, :], low01)
            for h in range(heads):
                lanes = slice(LANES * (h // 2), LANES * (h // 2) + LANES)
                own = (lane < 64) if h % 2 == 0 else (lane >= 64)
                knp = kn[:, lanes].astype(BF16)
                qm_h = jnp.where(own, qm[:, lanes], 0.0).astype(BF16)
                qe_h = jnp.where(own, qe[:, lanes], 0.0).astype(BF16)
                ke_h = jnp.where(own, ke[:, lanes], 0.0).astype(BF16)
                v_h = v_ref[rows, h * dv:(h + 1) * dv]
                st = st_ref[h]
                sall_ref[ci, h] = st
                p = _scores(qm_h, knp, jnp.where(own, qk1[:, lanes], 0.0))
                o_ref[rows, h * dv:(h + 1) * dv] = _dot(p.astype(BF16), v_h) + _dot_nt(qe_h, st.astype(BF16))
                st_ref[h] = st * ebl_row[:, lanes] + _dot_tn(v_h, ke_h)
            return carry

        lax.fori_loop(0, nc, chunk, 0, unroll=True)

    return _pcall(
        body, name, [qk, v, gl, wg, bg], grid=(t // gt,),
        in_specs=[pl.BlockSpec((gt, 2 * kw), lambda i: (i, 0)), pl.BlockSpec((gt, vw), lambda i: (i, 0)),
                  pl.BlockSpec((gt, LANES), lambda i: (i, 0)), pl.BlockSpec((LANES, kw), lambda i: (0, 0)),
                  pl.BlockSpec((1, kw), lambda i: (0, 0))],
        out_shape=[S((t, vw), F32), S((t, kw), F32), S((t // CHUNK, heads, dv, LANES), F32)],
        out_specs=[pl.BlockSpec((gt, vw), lambda i: (i, 0)), pl.BlockSpec((gt, kw), lambda i: (i, 0)),
                   pl.BlockSpec((nc, heads, dv, LANES), lambda i: (i, 0, 0, 0))],
        scratch_shapes=[pltpu.VMEM((heads, dv, LANES), F32)],
        sem=("arbitrary",), plans=plans)


def _gla_bwd(qk, v, lg, do, sall, gl, wg, heads, name, plans=()):
    t = qk.shape[0]
    kw, vw = qk.shape[1] // 2, v.shape[1]
    dk, dv = kw // heads, vw // heads
    gt = min(ROW_TILE, t)
    nc = gt // CHUNK
    nt = t // gt
    scale = dk ** -0.5

    def body(qk_ref, v_ref, lg_ref, do_ref, sall_ref, gl_ref, wg_ref, dqk_ref, dv_ref, dgl_ref, dwg_ref, dbg_ref, dst_ref, dgk_ref):
        @pl.when(pl.program_id(0) == 0)
        def _():
            dst_ref[...] = jnp.zeros_like(dst_ref)
            dwg_ref[...] = jnp.zeros_like(dwg_ref)
            dbg_ref[...] = jnp.zeros_like(dbg_ref)

        low01 = _tri(True).astype(BF16)
        up01 = _tri(False).astype(BF16)
        causal = _tri(True)
        lane = lax.broadcasted_iota(jnp.int32, (CHUNK, LANES), 1)
        last_row = lax.broadcasted_iota(jnp.int32, (CHUNK, kw), 0) == CHUNK - 1

        def chunk(cj, carry):
            ci = nc - 1 - cj
            rows = pl.ds(pl.multiple_of(ci * CHUNK, CHUNK), CHUNK)
            q = qk_ref[rows, 0:kw] * scale
            k = qk_ref[rows, kw:2 * kw]
            qk1 = q.astype(BF16).astype(F32) * k.astype(BF16).astype(F32)
            lgc = lg_ref[rows, :]
            eb, ebl, em, en, ebl_row, qe, ke, qm, kn = _gla_chunk_terms(q, k, lgc, low01)
            dqe, dqm, dkn, dke, drow = [], [], [], [], []
            for pr in range(kw // LANES):
                lanes = slice(LANES * pr, LANES * pr + LANES)
                knp = kn[:, lanes].astype(BF16)
                parts = []
                for half in range(2):
                    h = 2 * pr + half
                    own = (lane < 64) if half == 0 else (lane >= 64)
                    qm_h = jnp.where(own, qm[:, lanes], 0.0).astype(BF16)
                    qe_h = jnp.where(own, qe[:, lanes], 0.0).astype(BF16)
                    ke_h = jnp.where(own, ke[:, lanes], 0.0).astype(BF16)
                    v_h = v_ref[rows, h * dv:(h + 1) * dv]
                    do_h = do_ref[rows, h * dv:(h + 1) * dv]
                    st = sall_ref[ci, h]
                    dst = dst_ref[h]
                    stb, dstb = st.astype(BF16), dst.astype(BF16)
                    p = _scores(qm_h, knp, jnp.where(own, qk1[:, lanes], 0.0)).astype(BF16)
                    dp = jnp.where(causal, _dot_nt(do_h, v_h), 0.0).astype(BF16)
                    dv_ref[rows, h * dv:(h + 1) * dv] = (_dot_tn(p, do_h) + _dot_nt(ke_h, dstb)).astype(BF16)
                    parts.append((jnp.where(own, _dot(dp, knp), 0.0), _dot_tn(dp, qm_h), _dot(do_h, stb), _dot(v_h, dstb),
                                  jnp.sum(st * dst, axis=0, keepdims=True)))
                    dst_ref[h] = dst * ebl_row[:, lanes] + _dot_tn(do_h, qe_h)
                dqm.append(parts[0][0] + parts[1][0])
                dkn.append(parts[0][1] + parts[1][1])
                dqe.append(parts[0][2] + parts[1][2])
                dke.append(parts[0][3] + parts[1][3])
                drow.append(parts[0][4] + parts[1][4])
            dqm, dkn, dqe, dke, drow = [jnp.concatenate(a, axis=1) for a in (dqm, dkn, dqe, dke, drow)]
            dqk_ref[rows, 0:kw] = ((dqe * eb + dqm * em) * scale).astype(BF16)
            dqk_ref[rows, kw:2 * kw] = (dke * ebl + dkn * en).astype(BF16)
            tke = dke * ke
            db = dqe * qe + dqm * qm - dkn * kn - tke
            dbl = jnp.sum(tke, axis=0, keepdims=True) + drow * ebl_row
            db = db + jnp.where(last_row, dbl, 0.0)
            dlg = _dot_01(up01, db)
            dgk_ref[rows, :] = dlg * ((1.0 - jnp.exp(GATE_NORMALIZER * lgc)) / GATE_NORMALIZER)
            return carry

        lax.fori_loop(0, nc, chunk, 0, unroll=True)
        dgk = dgk_ref[...]
        dgkb = dgk.astype(BF16)
        dgl_ref[...] = _dot_nt(dgkb, wg_ref[...]).astype(BF16)
        dwg_ref[...] += _dot_tn(gl_ref[...].astype(BF16), dgkb)
        dbg_ref[...] += _rsum8(dgk)

    rev = lambda i: (nt - 1 - i, 0)
    return _pcall(
        body, name, [qk, v, lg, do, sall, gl, wg], grid=(nt,),
        in_specs=[pl.BlockSpec((gt, 2 * kw), rev), pl.BlockSpec((gt, vw), rev), pl.BlockSpec((gt, kw), rev),
                  pl.BlockSpec((gt, vw), rev), pl.BlockSpec((nc, heads, dv, LANES), lambda i: (nt - 1 - i, 0, 0, 0)),
                  pl.BlockSpec((gt, LANES), rev), pl.BlockSpec((LANES, kw), lambda i: (0, 0))],
        out_shape=[S((t, 2 * kw), BF16), S((t, vw), BF16), S((t, LANES), BF16), S((LANES, kw), F32), S((8, kw), F32)],
        out_specs=[pl.BlockSpec((gt, 2 * kw), rev), pl.BlockSpec((gt, vw), rev), pl.BlockSpec((gt, LANES), rev),
                   pl.BlockSpec((LANES, kw), lambda i: (0, 0)), pl.BlockSpec((8, kw), lambda i: (0, 0))],
        scratch_shapes=[pltpu.VMEM((heads, dv, LANES), F32), pltpu.VMEM((gt, kw), F32)],
        sem=("arbitrary",), plans=plans)


def _conv_taps(cx_ref, halo_ref, first, cw):
    tm = cx_ref.shape[0]
    u = cx_ref[:, cw:2 * cw].astype(F32) * cx_ref[:, 2 * cw:3 * cw].astype(F32)
    uh = halo_ref[:, cw:2 * cw].astype(F32) * halo_ref[:, 2 * cw:3 * cw].astype(F32)
    uh = jnp.where(first, 0.0, uh)
    before1, before2 = uh[BF16_ROWS - 1:BF16_ROWS, :], uh[BF16_ROWS - 2:BF16_ROWS - 1, :]
    row = lax.broadcasted_iota(jnp.int32, (tm, cw), 0)
    u1 = jnp.where(row == 0, before1, pltpu.roll(u, 1, 0))
    u2 = jnp.where(row == 0, before2, jnp.where(row == 1, before1, pltpu.roll(u, 2, 0)))
    return u, u1, u2


def _head_norm(o_h, gn):
    rstd = lax.rsqrt(jnp.mean(o_h * o_h, axis=-1, keepdims=True) + EPS)
    ohat = o_h * rstd
    return ohat, rstd, ohat * gn


def _mix_out(cx, o, go, conv_w, gn, w_out, x, ada9, nrm_next, heads, name, plans=()):
    t, d = x.shape
    cw, vw = conv_w.shape[1], o.shape[1]
    dv = vw // heads
    tm = min(ROW_TILE, t)

    def body(cx_ref, halo_ref, o_ref, go_ref, cwt_ref, gn_ref, w_ref, x_ref, ada_ref, nn_ref, xo_ref, m_ref, y_ref, hn_ref):
        u, u1, u2 = _conv_taps(cx_ref, halo_ref, pl.program_id(0) == 0, cw)
        yc = cwt_ref[0:1, :] * u2 + cwt_ref[1:2, :] * u1 + cwt_ref[2:3, :] * u
        y_ref[:, 0:cw] = (cx_ref[:, 0:cw].astype(F32) * yc).astype(BF16)
        for h in range(heads):
            cols = slice(h * dv, (h + 1) * dv)
            _, _, on = _head_norm(o_ref[:, cols], gn_ref[...])
            g = go_ref[:, cols].astype(F32)
            y_ref[:, cw + h * dv:cw + (h + 1) * dv] = (on * (g * _sigmoid(g))).astype(BF16)
        m = _dot(y_ref[...], w_ref[...])
        m_ref[...] = m.astype(BF16)
        xo = x_ref[...] + ada_ref[5:6, :] * m
        xo_ref[...] = xo
        hn_ref[...] = _normmod(xo, nn_ref[...], ada_ref[7:8, :], ada_ref[6:7, :])[2].astype(BF16)

    return _pcall(
        body, name, [cx, cx, o, go, conv_w, gn, w_out, x, ada9, nrm_next], grid=(t // tm,),
        in_specs=[pl.BlockSpec((tm, 3 * cw), lambda i: (i, 0)),
                  pl.BlockSpec((BF16_ROWS, 3 * cw), lambda i: (jnp.maximum(i * (tm // BF16_ROWS) - 1, 0), 0)),
                  pl.BlockSpec((tm, vw), lambda i: (i, 0)), pl.BlockSpec((tm, vw), lambda i: (i, 0)),
                  pl.BlockSpec((3, cw), lambda i: (0, 0)), pl.BlockSpec((1, dv), lambda i: (0, 0)),
                  pl.BlockSpec((cw + vw, d), lambda i: (0, 0)), pl.BlockSpec((tm, d), lambda i: (i, 0)),
                  pl.BlockSpec((N_ADA, d), lambda i: (0, 0)), pl.BlockSpec((1, d), lambda i: (0, 0))],
        out_shape=[S((t, d), F32), S((t, d), BF16), S((t, cw + vw), BF16), S((t, d), BF16)],
        out_specs=[pl.BlockSpec((tm, d), lambda i: (i, 0)), pl.BlockSpec((tm, d), lambda i: (i, 0)),
                   pl.BlockSpec((tm, cw + vw), lambda i: (i, 0)), pl.BlockSpec((tm, d), lambda i: (i, 0))],
        sem=("parallel",), plans=plans)


def _mix_bwd_a(dxo, m, ycat, ada9, w_out, cx, o, go, conv_w, gn, heads, name, plans=()):
    t, d = dxo.shape
    cw, vw = conv_w.shape[1], o.shape[1]
    dv = vw // heads
    tm = min(ROW_TILE, t)
    nt = t // tm

    def body(dxo_ref, m_ref, y_ref, ada_ref, w_ref, cx_ref, halo_ref, o_ref, go_ref, cwt_ref, gn_ref,
             dw_ref, dyc_ref, dcb_ref, do_ref, dgo_ref, dg_ref, dcw_ref, dgn_ref, acc_ref):
        @pl.when(pl.program_id(0) == 0)
        def _():
            dg_ref[...] = jnp.zeros_like(dg_ref)
            dcw_ref[...] = jnp.zeros_like(dcw_ref)
            dgn_ref[...] = jnp.zeros_like(dgn_ref)

        dxo_t = dxo_ref[...]
        dmb = (ada_ref[5:6, :] * dxo_t).astype(BF16)
        part = _dot_tn(y_ref[...], dmb)

        @pl.when(pl.program_id(0) == 0)
        def _():
            acc_ref[...] = part

        @pl.when(pl.program_id(0) > 0)
        def _():
            acc_ref[...] += part

        @pl.when(pl.program_id(0) == nt - 1)
        def _():
            dw_ref[...] = acc_ref[...].astype(BF16)

        dg_ref[...] += _rsum8(dxo_t * m_ref[...].astype(F32))
        dy = _dot_nt(dmb, w_ref[...])
        u, u1, u2 = _conv_taps(cx_ref, halo_ref, pl.program_id(0) == 0, cw)
        yc = cwt_ref[0:1, :] * u2 + cwt_ref[1:2, :] * u1 + cwt_ref[2:3, :] * u
        dyv = dy[:, 0:cw]
        dcb_ref[...] = (dyv * yc).astype(BF16)
        dyc = dyv * cx_ref[:, 0:cw].astype(F32)
        dyc_ref[...] = dyc
        dcw_ref[0] += _rsum8(dyc * u2)
        dcw_ref[1] += _rsum8(dyc * u1)
        dcw_ref[2] += _rsum8(dyc * u)
        for h in range(heads):
            cols = slice(h * dv, (h + 1) * dv)
            ohat, rstd, on = _head_norm(o_ref[:, cols], gn_ref[...])
            g = go_ref[:, cols].astype(F32)
            sg = _sigmoid_tanh(g)
            dyg = dy[:, cw + h * dv:cw + (h + 1) * dv]
            dgo_ref[:, cols] = (dyg * on * (sg * (1.0 + g * (1.0 - sg)))).astype(BF16)
            don = dyg * (g * sg)
            dgn_ref[...] += _rsum8(don * ohat)
            tt = don * gn_ref[...]
            do_ref[:, cols] = (rstd * (tt - ohat * jnp.mean(tt * ohat, axis=-1, keepdims=True))).astype(BF16)

    return _pcall(
        body, name, [dxo, m, ycat, ada9, w_out, cx, cx, o, go, conv_w, gn], grid=(nt,),
        in_specs=[pl.BlockSpec((tm, d), lambda i: (i, 0)), pl.BlockSpec((tm, d), lambda i: (i, 0)),
                  pl.BlockSpec((tm, cw + vw), lambda i: (i, 0)),
                  pl.BlockSpec((N_ADA, d), lambda i: (0, 0)), pl.BlockSpec((cw + vw, d), lambda i: (0, 0)),
                  pl.BlockSpec((tm, 3 * cw), lambda i: (i, 0)),
                  pl.BlockSpec((BF16_ROWS, 3 * cw), lambda i: (jnp.maximum(i * (tm // BF16_ROWS) - 1, 0), 0)),
                  pl.BlockSpec((tm, vw), lambda i: (i, 0)), pl.BlockSpec((tm, vw), lambda i: (i, 0)),
                  pl.BlockSpec((3, cw), lambda i: (0, 0)), pl.BlockSpec((1, dv), lambda i: (0, 0))],
        out_shape=[S((cw + vw, d), BF16), S((t, cw), F32), S((t, cw), BF16), S((t, vw), BF16), S((t, vw), BF16),
                   S((8, d), F32), S((3, 8, cw), F32), S((8, dv), F32)],
        out_specs=[pl.BlockSpec((cw + vw, d), lambda i: (0, 0)), pl.BlockSpec((tm, cw), lambda i: (i, 0)),
                   pl.BlockSpec((tm, cw), lambda i: (i, 0)), pl.BlockSpec((tm, vw), lambda i: (i, 0)),
                   pl.BlockSpec((tm, vw), lambda i: (i, 0)), pl.BlockSpec((8, d), lambda i: (0, 0)),
                   pl.BlockSpec((3, 8, cw), lambda i: (0, 0, 0)), pl.BlockSpec((8, dv), lambda i: (0, 0))],
        scratch_shapes=[pltpu.VMEM((cw + vw, d), F32)],
        sem=("arbitrary",), plans=plans)


def _mix_bwd_b(dyc, cx, dcb, dqk, dvv, dgo, dgl, conv_w, w, x, dxo, ada9, nrm, name, plans=()):
    t, d = x.shape
    cw = conv_w.shape[1]
    n = w.shape[0]
    tm = min(ROW_TILE, t)
    nt = t // tm
    pieces = [dcb.shape[1], cw, cw, dqk.shape[1], dvv.shape[1], dgo.shape[1], dgl.shape[1]]
    assert sum(pieces) == n

    def body(dyc_ref, nxt_ref, cx_ref, dcb_ref, dqk_ref, dv_ref, dgo_ref, dgl_ref, cwt_ref, w_ref, x_ref, dxo_ref, ada_ref, n_ref,
             dx_ref, dp_ref, dsh_ref, dsc_ref, dn_ref):
        i = pl.program_id(0)

        @pl.when(i == 0)
        def _():
            dsh_ref[...] = jnp.zeros_like(dsh_ref)
            dsc_ref[...] = jnp.zeros_like(dsc_ref)
            dn_ref[...] = jnp.zeros_like(dn_ref)

        dyc_t = dyc_ref[...]
        nxt = jnp.where(i == nt - 1, 0.0, nxt_ref[...])
        row = lax.broadcasted_iota(jnp.int32, (tm, cw), 0)
        d1 = jnp.where(row == tm - 1, nxt[0:1, :], pltpu.roll(dyc_t, tm - 1, 0))
        d2 = jnp.where(row == tm - 2, nxt[0:1, :], jnp.where(row == tm - 1, nxt[1:2, :], pltpu.roll(dyc_t, tm - 2, 0)))
        du = cwt_ref[2:3, :] * dyc_t + cwt_ref[1:2, :] * d1 + cwt_ref[0:1, :] * d2
        c0 = 0
        dp_ref[:, c0:c0 + cw] = dcb_ref[...]
        dp_ref[:, cw:2 * cw] = (du * cx_ref[:, 2 * cw:3 * cw].astype(F32)).astype(BF16)
        dp_ref[:, 2 * cw:3 * cw] = (du * cx_ref[:, cw:2 * cw].astype(F32)).astype(BF16)
        c0 = 3 * cw
        for ref in (dqk_ref, dv_ref, dgo_ref, dgl_ref):
            wd = ref.shape[1]
            dp_ref[:, c0:c0 + wd] = ref[...]
            c0 += wd
        dh = _dot(dp_ref[...], w_ref[...])
        dx, tsh, tsc, tn = _normmod_bwd(dh, x_ref[...], n_ref[...], ada_ref[4:5, :])
        dx_ref[...] = dxo_ref[...] + dx
        dsh_ref[...] += _rsum8(tsh)
        dsc_ref[...] += _rsum8(tsc)
        dn_ref[...] += _rsum8(tn)

    row_spec = lambda wd: pl.BlockSpec((tm, wd), lambda i: (i, 0))
    vec = pl.BlockSpec((8, d), lambda i: (0, 0))
    return _pcall(
        body, name, [dyc, dyc, cx, dcb, dqk, dvv, dgo, dgl, conv_w, w, x, dxo, ada9, nrm], grid=(nt,),
        in_specs=[row_spec(cw), pl.BlockSpec((8, cw), lambda i: (jnp.minimum((i + 1) * (tm // 8), t // 8 - 1), 0)),
                  row_spec(3 * cw), row_spec(cw), row_spec(dqk.shape[1]), row_spec(dvv.shape[1]), row_spec(dgo.shape[1]),
                  row_spec(dgl.shape[1]), pl.BlockSpec((3, cw), lambda i: (0, 0)), pl.BlockSpec((n, d), lambda i: (0, 0)),
                  row_spec(d), row_spec(d), pl.BlockSpec((N_ADA, d), lambda i: (0, 0)), pl.BlockSpec((1, d), lambda i: (0, 0))],
        out_shape=[S((t, d), F32), S((t, n), BF16), S((8, d), F32), S((8, d), F32), S((8, d), F32)],
        out_specs=[row_spec(d), row_spec(n), vec, vec, vec],
        sem=("arbitrary",), plans=plans)


def _ffn_out_loss(s, w_out, x, ada9, g_row, res_scale, target, nrm, name):
    nb, t, bw = s.shape
    d = x.shape[1]
    tm = min(ROW_TILE, t)
    nt = t // tm

    def body(s_ref, w_ref, x_ref, ada_ref, tg_ref, n_ref, f_ref, loss_ref, dx_ref, dn_ref, acc_ref):
        i = pl.program_id(0)

        @pl.when(i == 0)
        def _():
            acc_ref[...] = jnp.zeros_like(acc_ref)
            dn_ref[...] = jnp.zeros_like(dn_ref)

        f = _dot(s_ref[0], w_ref[0])
        for b in range(1, nb):
            f = f + _dot(s_ref[b], w_ref[b])
        f_ref[...] = f.astype(BF16)
        xt = x_ref[...] + (res_scale * ada_ref[g_row:g_row + 1, :]) * f
        rstd = lax.rsqrt(jnp.mean(xt * xt, axis=-1, keepdims=True) + EPS)
        xhat = xt * rstd
        err = xhat * n_ref[...] - tg_ref[...]
        acc_ref[...] += _rsum8(err * err)
        dy = err * (1.0 / d)
        dn_ref[...] += _rsum8(dy * xhat)
        dxhat = dy * n_ref[...]
        dx_ref[...] = rstd * (dxhat - xhat * jnp.mean(dxhat * xhat, axis=-1, keepdims=True))

        @pl.when(i == nt - 1)
        def _():
            loss_ref[...] = jnp.full(loss_ref.shape, (0.5 / d) * jnp.sum(acc_ref[...]), F32)

    return pl.pallas_call(
        body, name=name, grid=(nt,),
        in_specs=[pl.BlockSpec((nb, tm, bw), lambda i: (0, i, 0)), pl.BlockSpec((nb, bw, d), lambda i: (0, 0, 0)),
                  pl.BlockSpec((tm, d), lambda i: (i, 0)), pl.BlockSpec((N_ADA, d), lambda i: (0, 0)),
                  pl.BlockSpec((tm, d), lambda i: (i, 0)), pl.BlockSpec((1, d), lambda i: (0, 0))],
        out_shape=[S((t, d), BF16), S((1, LANES), F32), S((t, d), F32), S((8, d), F32)],
        out_specs=[pl.BlockSpec((tm, d), lambda i: (i, 0)), pl.BlockSpec((1, LANES), lambda i: (0, 0)),
                   pl.BlockSpec((tm, d), lambda i: (i, 0)), pl.BlockSpec((8, d), lambda i: (0, 0))],
        scratch_shapes=[pltpu.VMEM((8, d), F32)],
        compiler_params=_cp("arbitrary"),
    )(s, w_out, x, ada9, target, nrm)


def _pack_smalls(vec_parts, dcw, dbg, dgn, dwg, loss_v, rank, name):
    d = vec_parts[0].shape[1]
    cw, kw, dv = dcw.shape[2], dbg.shape[1], dgn.shape[1]
    nv = len(vec_parts)
    loss_row = nv + 2 + rank * kw // d
    assert 2 * cw == d and cw + kw + dv <= d and (rank * kw) % d == 0 and loss_row < PACK_ROWS
    per_row = d // kw

    def body(*refs):
        vrefs, (dcw_ref, dbg_ref, dgn_ref, dwg_ref, loss_ref, o_ref) = refs[:nv], refs[nv:]
        o_ref[...] = jnp.zeros_like(o_ref)
        o_ref[loss_row:loss_row + 1, 0:loss_ref.shape[1]] = loss_ref[...]
        for r, ref in enumerate(vrefs):
            o_ref[r:r + 1, :] = jnp.sum(ref[...], axis=0, keepdims=True)
        o_ref[nv:nv + 1, 0:cw] = jnp.sum(dcw_ref[0], axis=0, keepdims=True)
        o_ref[nv:nv + 1, cw:2 * cw] = jnp.sum(dcw_ref[1], axis=0, keepdims=True)
        o_ref[nv + 1:nv + 2, 0:cw] = jnp.sum(dcw_ref[2], axis=0, keepdims=True)
        o_ref[nv + 1:nv + 2, cw:cw + kw] = jnp.sum(dbg_ref[...], axis=0, keepdims=True)
        o_ref[nv + 1:nv + 2, cw + kw:cw + kw + dv] = jnp.sum(dgn_ref[...], axis=0, keepdims=True)
        for r in range(rank):
            o_ref[nv + 2 + r // per_row:nv + 3 + r // per_row, (r % per_row) * kw:(r % per_row + 1) * kw] = dwg_ref[r:r + 1, :]

    return pl.pallas_call(body, name=name, out_shape=S((PACK_ROWS, d), F32), compiler_params=_cp())(*vec_parts, dcw, dbg, dgn, dwg, loss_v)


def _sum_slots(a, name):
    def body(a_ref, o_ref):
        acc = a_ref[0]
        for s in range(1, NDEV):
            acc = acc + a_ref[s]
        o_ref[...] = acc

    return pl.pallas_call(body, name=name, out_shape=S(a.shape[1:], F32), compiler_params=_cp())(a)


def _adamw(w, g, m, v):
    m = ADAM_B1 * m + (1.0 - ADAM_B1) * g
    v = ADAM_B2 * v + (1.0 - ADAM_B2) * (g * g)
    m_hat = m / (1.0 - ADAM_B1 ** ADAM_STEP)
    v_hat = v / (1.0 - ADAM_B2 ** ADAM_STEP)
    return -ADAM_LR * (m_hat / (jnp.sqrt(v_hat) + ADAM_EPS) + ADAM_WD * w), m, v


def _adam_slots(recv, w, m, v, name):
    r, c = w.shape
    slots = recv.shape[0]
    tr = _row_tile(r, c)

    def body(recv_ref, w_ref, m_ref, v_ref, g_ref, d_ref, mo_ref, vo_ref):
        g = recv_ref[0].astype(F32)
        for s in range(1, slots):
            g = g + recv_ref[s].astype(F32)
        g_ref[...] = g
        d_ref[...], mo_ref[...], vo_ref[...] = _adamw(w_ref[...], g, m_ref[...], v_ref[...])

    blk = pl.BlockSpec((tr, c), lambda i: (i, 0))
    return pl.pallas_call(
        body, name=name, grid=(r // tr,),
        in_specs=[pl.BlockSpec((slots, tr, c), lambda i: (0, i, 0)), blk, blk, blk],
        out_shape=[S((r, c), F32)] * 4, out_specs=[blk] * 4, compiler_params=_cp("parallel"),
    )(recv, w, m, v)


def _adam_w_ada(act_t, dada, w, m, v, name):
    r, c = w.shape
    tr = 128
    nb = act_t.shape[1]

    def body(a_ref, da_ref, w_ref, m_ref, v_ref, g_ref, d_ref, mo_ref, vo_ref):
        g = a_ref[:, 0:1] * da_ref[0:1, :]
        for b in range(1, nb):
            g = g + a_ref[:, b:b + 1] * da_ref[b:b + 1, :]
        g_ref[...] = g
        d_ref[...], mo_ref[...], vo_ref[...] = _adamw(w_ref[...], g, m_ref[...], v_ref[...])

    blk = pl.BlockSpec((tr, c), lambda i: (i, 0))
    return pl.pallas_call(
        body, name=name, grid=(r // tr,),
        in_specs=[pl.BlockSpec((tr, nb), lambda i: (i, 0)), pl.BlockSpec((nb, c), lambda i: (0, 0)), blk, blk, blk],
        out_shape=[S((r, c), F32)] * 4, out_specs=[blk] * 4, compiler_params=_cp("parallel"),
    )(act_t, dada, w, m, v)


def _adam_smalls(ws, gs, ms, vs, name):
    n = len(ws)

    def body(*refs):
        w_r, g_r, m_r, v_r = (refs[k * n:(k + 1) * n] for k in range(4))
        d_o, m_o, v_o = (refs[(4 + k) * n:(5 + k) * n] for k in range(3))
        for i in range(n):
            d_o[i][...], m_o[i][...], v_o[i][...] = _adamw(w_r[i][...], g_r[i][...], m_r[i][...], v_r[i][...])

    shapes = [S(w.shape, F32) for w in ws]
    outs = pl.pallas_call(body, name=name, out_shape=shapes * 3, compiler_params=_cp())(*ws, *gs, *ms, *vs)
    return outs[:n], outs[n:2 * n], outs[2 * n:]


def kernel(x, c, w_ada, b_ada, norm_ffn1, w_ffn1_in, w_ffn1_out, norm_mix, w_mix_in, conv_w, w_gk2, b_gk, gla_norm, w_mix_out, norm_ffn2, w_ffn2_in, w_ffn2_out, norm_final, loss_target, m_w_ada, m_b_ada, m_norm_ffn1, m_w_ffn1_in, m_w_ffn1_out, m_norm_mix, m_w_mix_in, m_conv_w, m_w_gk2, m_b_gk, m_gla_norm, m_w_mix_out, m_norm_ffn2, m_w_ffn2_in, m_w_ffn2_out, m_norm_final, v_w_ada, v_b_ada, v_norm_ffn1, v_w_ffn1_in, v_w_ffn1_out, v_norm_mix, v_w_mix_in, v_conv_w, v_w_gk2, v_b_gk, v_gla_norm, v_w_mix_out, v_norm_ffn2, v_w_ffn2_in, v_w_ffn2_out, v_norm_final):
    t, d = x.shape[1], x.shape[2]
    x0, tgt = x[0], loss_target[0]
    rank, kw = w_gk2.shape[1], w_gk2.shape[2] * NDEV
    cw = conv_w.shape[2] * NDEV
    dv = gla_norm.shape[1]
    vw = d - cw
    heads = vw // dv
    mix_cols = w_mix_in.shape[2]
    widths = [3 * cw, 2 * kw, vw, vw, LANES]
    n_proj = 3 * cw + 2 * kw + 2 * vw + rank
    assert n_proj == mix_cols * NDEV and rank <= LANES
    me = 4 * lax.axis_index("x") + 2 * lax.axis_index("y") + lax.axis_index("c")

    core = lax.axis_index("c").astype(jnp.int32).reshape(1)
    bf = lambda a: a[0].astype(BF16)
    bft = lambda a: a[0].T.astype(BF16)
    nb = NDEV // 2

    (ada_row, act_all), ((w1i, cwt_all, wg_all),) = _ada_rows(
        c, w_ada[0], b_ada, "ada_rows", plans=[_plan_gather([bft(w_ffn1_in), conv_w[0], w_gk2[0]])])
    ada9 = ada_row.reshape(N_ADA, d)
    cwt = cwt_all.transpose(1, 0, 2).reshape(conv_w.shape[1], cw)
    wg = jnp.pad(wg_all.transpose(1, 0, 2).reshape(rank, kw), ((0, LANES - rank), (0, 0))).astype(BF16)

    (h1, gu1, s1), ((w1o, wmi),) = _ffn_in(x0, ada9, norm_ffn1, w1i, 0, 1, "ffn1_in", plans=[_plan_gather([bf(w_ffn1_out), bft(w_mix_in)])])
    w1o = w1o.reshape(nb, -1, d)
    wmi = jnp.pad(wmi.reshape(n_proj, d), ((0, sum(widths) - n_proj), (0, 0)))
    w2i_mine = bft(w_ffn2_in)
    quarter = w2i_mine.shape[0] // 4
    part = lambda k, into=None: _plan_gather([w2i_mine], rows=(k * quarter, quarter), into=into)
    (x1, f1), ((wmo,), (w2i,)) = _ffn_out(s1, w1o, x0, ada9, 2, 0.5, "ffn1_out", plans=[_plan_gather([bf(w_mix_out)]), part(0)])
    wmo = wmo.reshape(cw + vw, d)
    (h2, cx, qk, vv, go, gl), ((w2i,),) = _mix_in(x1, ada9, norm_mix, wmi, widths, [BF16, F32, BF16, BF16, F32], "mix_in",
                                                 plans=[part(1, [w2i])])
    (o, lg, sall), ((w2i,),) = _gla_fwd(qk, vv, gl, wg, b_gk, heads, "gla_fwd", plans=[part(2, [w2i])])
    (x2, mm, ycat, h3), ((w2i,),) = _mix_out(cx, o, go, cwt, gla_norm, wmo, x1, ada9, norm_ffn2, heads, "mix_out", plans=[part(3, [w2i])])
    (h3, gu3, s3), ((w2o,),) = _ffn_in(None, None, None, w2i, 6, 7, "ffn2_in", plans=[_plan_gather([bf(w_ffn2_out)])], h_made=h3)
    w2o = w2o.reshape(nb, -1, d)
    f3, loss_v, dx3, dnf = _ffn_out_loss(s3, w2o, x2, ada9, 8, 0.5, tgt, norm_final.reshape(1, d), "ffn2_out_loss")

    dx2, sum2i, sum2o, (dsh3, dsc3, dg3, dn3), _, _ = _ffn_backward(
        dx3, x2, h3, gu3, f3, ada9, norm_ffn2, w2i, w2o, (6, 7, 8), core, "ffn2_bwd", later=True)
    (dwmo, dyc, dcb, do, dgo, dg2, dcw, dgn), ((r2o,),) = _mix_bwd_a(dx2, mm, ycat, ada9, wmo, cx, o, go, cwt, gla_norm, heads, "mix_bwd_a",
                                                                   plans=[_plan_chip_swap([sum2o])])
    half = sum2i.shape[1] // 2
    (dqk, dvv, dgl, dwg, dbg), ((r2i,),) = _gla_bwd(qk, vv, lg, do, sall, gl, wg, heads, "gla_bwd",
                                                   plans=[_plan_chip_swap([sum2i], rows=(0, half))])
    (dx1, dproj, dsh2, dsc2, dnm), ((r2i,),) = _mix_bwd_b(dyc, cx, dcb, dqk, dvv, dgo, dgl, cwt, wmi, x1, dx2, ada9, norm_mix, "mix_bwd_b",
                                                         plans=[_plan_chip_swap([sum2i], rows=(half, half), into=[r2i])])
    n_pad = sum(widths)
    tn = n_pad // 5
    dwmi, _ = _tn_matmul(dproj, h2, lambda tk: (tk, tn), lambda sb, k: (k, sb), lambda tk: (tk, d), lambda sb, k: (k, 0),
                         (n_pad, d), (tn, d), lambda sb, k: (sb, 0), 5, "mix_dwin")
    dwmi = dwmi[:n_proj].reshape(NDEV, mix_cols, d)
    dwmo = dwmo.reshape(NDEV, -1, d)
    dx0, r1i, (sum1o, r1o, head), (dsh1, dsc1, dg1, dn1), _, ((rmi, rmo),) = _ffn_backward(
        dx1, x0, h1, gu1, f1, ada9, norm_ffn1, w1i, w1o, (0, 1, 2), core, "ffn1_bwd",
        ds_plans=[_plan_sibling_swap([dwmi, dwmo])],
        dwin_plans=lambda moved: [_plan_chip_swap([_pair_add(dwmi, moved[0][0], core, "mix_dwin_add"),
                                                   _pair_add(dwmo, moved[0][1], core, "mix_dwout_add")])])
    pack = _pack_smalls([dn1, dnm, dn3, dnf, dsh1, dsc1, dg1, dsh2, dsc2, dg2, dsh3, dsc3, dg3], dcw, dbg, dgn, dwg, loss_v, rank, "pack_smalls")
    (r1o,), (pack_all,) = _exchange([_plan_chip_swap([sum1o], rows=(head, sum1o.shape[1] - head), into=[r1o]),
                                     _plan_all_to_all([pack], True)], "grads_last")
    tot = _sum_slots(pack_all, "sum_smalls")

    res = {}
    for nm, recv, w, m, v in (("w_ffn1_out", r1o, w_ffn1_out, m_w_ffn1_out, v_w_ffn1_out), ("w_mix_out", rmo, w_mix_out, m_w_mix_out, v_w_mix_out),
                              ("w_ffn2_out", r2o, w_ffn2_out, m_w_ffn2_out, v_w_ffn2_out)):
        res[nm] = [a[None] for a in _adam_slots(recv, w[0], m[0], v[0], "adam_" + nm)]
    for nm, recv, w, m, v in (("w_ffn1_in", r1i, w_ffn1_in, m_w_ffn1_in, v_w_ffn1_in), ("w_mix_in", rmi, w_mix_in, m_w_mix_in, v_w_mix_in),
                              ("w_ffn2_in", r2i, w_ffn2_in, m_w_ffn2_in, v_w_ffn2_in)):
        res[nm] = [a.T[None] for a in _adam_slots(recv, w[0].T, m[0].T, v[0].T, "adam_" + nm)]

    cols_ada = w_ada.shape[2]
    dada_all = pack_all[:, 4:4 + N_ADA, :].reshape(NDEV, N_ADA * d)
    dada_mine = lax.dynamic_slice_in_dim(dada_all, me * cols_ada, cols_ada, axis=1)
    res["w_ada"] = [a[None] for a in _adam_w_ada(act_all.T, dada_mine, w_ada[0], m_w_ada[0], v_w_ada[0], "adam_w_ada")]

    nv = 4 + N_ADA
    g_small = {
        "b_ada": tot[4:nv].reshape(1, N_ADA * d),
        "norm_ffn1": tot[0:1], "norm_mix": tot[1:2], "norm_ffn2": tot[2:3], "norm_final": tot[3:4],
        "conv_w": lax.dynamic_slice_in_dim(
            jnp.concatenate([tot[nv:nv + 1, 0:cw], tot[nv:nv + 1, cw:2 * cw], tot[nv + 1:nv + 2, 0:cw]], axis=0), me * (cw // NDEV), cw // NDEV, axis=1),
        "w_gk2": lax.dynamic_slice_in_dim(tot[nv + 2:nv + 2 + rank * kw // d].reshape(rank, kw), me * (kw // NDEV), kw // NDEV, axis=1),
        "b_gk": tot[nv + 1:nv + 2, cw:cw + kw],
        "gla_norm": tot[nv + 1:nv + 2, cw + kw:cw + kw + dv],
    }
    small = {"b_ada": (b_ada, m_b_ada, v_b_ada), "norm_ffn1": (norm_ffn1, m_norm_ffn1, v_norm_ffn1), "norm_mix": (norm_mix, m_norm_mix, v_norm_mix),
             "norm_ffn2": (norm_ffn2, m_norm_ffn2, v_norm_ffn2), "norm_final": (norm_final, m_norm_final, v_norm_final),
             "conv_w": (conv_w, m_conv_w, v_conv_w), "w_gk2": (w_gk2, m_w_gk2, v_w_gk2), "b_gk": (b_gk, m_b_gk, v_b_gk),
             "gla_norm": (gla_norm, m_gla_norm, v_gla_norm)}
    names = list(small)
    flat = lambda a: a.reshape(-1, a.shape[-1])
    dl, mo, vo = _adam_smalls([flat(small[n][0]) for n in names], [g_small[n] for n in names],
                              [flat(small[n][1]) for n in names], [flat(small[n][2]) for n in names], "adam_smalls")
    for i, n in enumerate(names):
        shp = small[n][0].shape
        res[n] = [g_small[n].reshape(shp), dl[i].reshape(shp), mo[i].reshape(shp), vo[i].reshape(shp)]

    loss = tot[nv + 2 + rank * kw // d, 0]
    order = ["w_ada", "b_ada", "norm_ffn1", "w_ffn1_in", "w_ffn1_out", "norm_mix", "w_mix_in", "conv_w", "w_gk2", "b_gk", "gla_norm",
             "w_mix_out", "norm_ffn2", "w_ffn2_in", "w_ffn2_out", "norm_final"]
    return (loss, dx0[None], *[res[n][0] for n in order], *[res[n][1] for n in order], *[res[n][2] for n in order], *[res[n][3] for n in order])
```
